```python
import math
import jax, jax.numpy as jnp
from jax import lax
import numpy as np

D_MODEL = 1024
BATCH = 8
SEQ = 8192
DEPTH = 2

N_MEM = 256
CONV_WIDTH = D_MODEL // 2
CONV_KSIZE = 31
ATT_PATTERNS = ((128, 1), (512, 4), (2048, 16))
N_GROUPS = len(ATT_PATTERNS)
HEADS_PER_GROUP = 4
ATT_HEAD_DIM = D_MODEL // 16
N_ATT_HEADS = N_GROUPS * HEADS_PER_GROUP
ATT_WIDTH = N_ATT_HEADS * ATT_HEAD_DIM
ATT_OUT_WIDTH = HEADS_PER_GROUP * ATT_HEAD_DIM
MEM_HEADS = 4
MEM_HEAD_DIM = D_MODEL // 8
MEM_WIDTH = MEM_HEADS * MEM_HEAD_DIM
N_BRANCHES = 3
IN_WIDTH = 2 * CONV_WIDTH + 3 * ATT_WIDTH + MEM_WIDTH + N_BRANCHES * D_MODEL
FFN_HIDDEN = -(-(8 * D_MODEL) // (3 * 256)) * 256
NUM_BUCKETS = 32
MAX_DISTANCE = 1024
RMS_EPS = 1e-6
LN_EPS = 1e-5
NEG_INF = -1e30

kernel_name = "hybrid_gated_conv_dilated_memory_encoder"


def _rms_norm(x, g):
    xf = x.astype(jnp.float32)
    y = xf * lax.rsqrt(jnp.mean(xf * xf, axis=-1, keepdims=True) + RMS_EPS)
    return (y * g.astype(jnp.float32)).astype(x.dtype)


def _layer_norm(x, g, b):
    xf = x.astype(jnp.float32)
    mu = jnp.mean(xf, axis=-1, keepdims=True)
    xc = xf - mu
    y = xc * lax.rsqrt(jnp.mean(xc * xc, axis=-1, keepdims=True) + LN_EPS)
    return (y * g.astype(jnp.float32) + b.astype(jnp.float32)).astype(x.dtype)


def _t5_bucket(rel):
    nb = NUM_BUCKETS // 2
    max_exact = nb // 2
    ret = jnp.where(rel > 0, nb, 0)
    n = jnp.abs(rel)
    nf = jnp.maximum(n, 1).astype(jnp.float32)
    large = max_exact + (jnp.log(nf / max_exact) / math.log(MAX_DISTANCE / max_exact)
                         * (nb - max_exact)).astype(jnp.int32)
    large = jnp.minimum(large, nb - 1)
    return ret + jnp.where(n < max_exact, n, large)


def _dilated_group(q, k, v, bias_tab, dilation, radius):
    B, S, H, E = q.shape
    blk = radius
    unit = dilation * blk
    sp = -(-S // unit) * unit
    L = sp // dilation
    nb = L // blk

    def to_blocks(t):
        t = jnp.pad(t, ((0, 0), (0, sp - S), (0, 0), (0, 0)))
        t = t.reshape(B, L, dilation, H, E).transpose(0, 2, 1, 3, 4)
        return t.reshape(B, dilation, nb, blk, H, E)

    def band_keys(t):
        tp = jnp.pad(t, ((0, 0), (0, 0), (1, 1), (0, 0), (0, 0), (0, 0)))
        return jnp.concatenate([tp[:, :, :-2], tp[:, :, 1:-1], tp[:, :, 2:]], axis=3)

    qb = to_blocks(q)
    kb = band_keys(to_blocks(k))
    vb = band_keys(to_blocks(v))
    valid = (jnp.arange(sp) < S).reshape(L, dilation).T.reshape(dilation, nb, blk)
    vp = jnp.pad(valid, ((0, 0), (1, 1), (0, 0)))
    kvalid = jnp.concatenate([vp[:, :-2], vp[:, 1:-1], vp[:, 2:]], axis=2)

    s = jnp.einsum('brnqhe,brnkhe->brnhqk', qb, kb).astype(jnp.float32) * (E ** -0.5)
    off = jnp.arange(3 * blk)[None, :] - blk - jnp.arange(blk)[:, None]
    band = jnp.abs(off) <= radius
    bias = bias_tab[_t5_bucket(off * dilation)].transpose(2, 0, 1).astype(jnp.float32)
    mask = band & kvalid[:, :, None, None, :]
    s = jnp.where(mask, s + bias, NEG_INF)
    m = jnp.max(s, axis=-1, keepdims=True)
    e = jnp.exp(s - m)
    den = jnp.sum(e, axis=-1, keepdims=True)
    lse = (m + jnp.log(den))[..., 0]
    o = jnp.einsum('brnhqk,brnkhe->brnqhe', (e / den).astype(v.dtype), vb)

    def from_blocks(t):
        tail = t.shape[4:]
        t = t.reshape(B, dilation, L, *tail).swapaxes(1, 2).reshape(B, sp, *tail)
        return t[:, :S]

    return from_blocks(o), from_blocks(lse.transpose(0, 1, 2, 4, 3))


def _dilated_attention(att_in, rel_bias):
    B, S, _ = att_in.shape
    qkv = att_in.reshape(B, S, 3, N_GROUPS, HEADS_PER_GROUP, ATT_HEAD_DIM)
    outs, lses = [], []
    for g, (window, dilation) in enumerate(ATT_PATTERNS):
        radius = window // (2 * dilation)
        tab = rel_bias[:, g * HEADS_PER_GROUP:(g + 1) * HEADS_PER_GROUP]
        o, l = _dilated_group(qkv[:, :, 0, g], qkv[:, :, 1, g], qkv[:, :, 2, g], tab, dilation, radius)
        outs.append(o)
        lses.append(l)
    w = jax.nn.softmax(jnp.stack(lses, axis=0), axis=0)
    o = jnp.sum(w[..., None].astype(outs[0].dtype) * jnp.stack(outs, axis=0), axis=0)
    return o.reshape(B, S, ATT_OUT_WIDTH)


def _conv_module(u, w_dw, b_dw, ln_g, ln_b, w_o):
    a, gt = jnp.split(u, 2, axis=-1)
    u = a * jax.nn.sigmoid(gt)
    pad = CONV_KSIZE // 2
    y = lax.conv_general_dilated(u, w_dw[:, None, :], window_strides=(1,), padding=((pad, pad),),
                                 dimension_numbers=('NWC', 'WIO', 'NWC'),
                                 feature_group_count=CONV_WIDTH) + b_dw
    y = jax.nn.silu(_layer_norm(y, ln_g, ln_b))
    return y @ w_o


def _memory_attention(q_in, mem, g_mem, w_kv, w_o):
    B, S, _ = q_in.shape
    M = mem.shape[1]
    q = q_in.reshape(B, S, MEM_HEADS, MEM_HEAD_DIM)
    kv = (_rms_norm(mem, g_mem) @ w_kv).reshape(B, M, 2, MEM_HEADS, MEM_HEAD_DIM)
    k, v = kv[:, :, 0], kv[:, :, 1]
    s = jnp.einsum('bshe,bmhe->bhsm', q, k).astype(jnp.float32) * (MEM_HEAD_DIM ** -0.5)
    p = jax.nn.softmax(s, axis=-1).astype(v.dtype)
    o = jnp.einsum('bhsm,bmhe->bshe', p, v).reshape(B, S, MEM_WIDTH)
    return o @ w_o


def _fwd_setup_inputs(seed: int = 0) -> dict:
    key = jax.random.key(seed)
    ks = jax.random.split(key, 24)
    f32 = jnp.float32

    def nrm(k, shape, scale):
        return jax.random.normal(k, shape, f32) * scale

    def gain(k, shape):
        return 1.0 + 0.05 * jax.random.normal(k, shape, f32)

    return {
        "x": jax.random.normal(ks[0], (BATCH, SEQ, D_MODEL), f32),
        "mem": jax.random.normal(ks[1], (BATCH, N_MEM, D_MODEL), f32),
        "rel_bias": nrm(ks[2], (NUM_BUCKETS, N_ATT_HEADS), 0.5),
        "norm_mix_pre": gain(ks[3], (DEPTH, D_MODEL)),
        "w_in": nrm(ks[4], (DEPTH, D_MODEL, IN_WIDTH), D_MODEL ** -0.5),
        "b_gate": nrm(ks[5], (DEPTH, N_BRANCHES * D_MODEL), 0.01),
        "conv_dw": nrm(ks[6], (DEPTH, CONV_KSIZE, CONV_WIDTH), CONV_KSIZE ** -0.5),
        "conv_dw_bias": nrm(ks[7], (DEPTH, CONV_WIDTH), 0.02),
        "conv_ln_g": gain(ks[8], (DEPTH, CONV_WIDTH)),
        "conv_ln_b": nrm(ks[9], (DEPTH, CONV_WIDTH), 0.02),
        "w_conv_out": nrm(ks[10], (DEPTH, CONV_WIDTH, D_MODEL), CONV_WIDTH ** -0.5),
        "w_att_out": nrm(ks[11], (DEPTH, ATT_OUT_WIDTH, D_MODEL), ATT_OUT_WIDTH ** -0.5),
        "norm_mem": gain(ks[12], (DEPTH, D_MODEL)),
        "w_mem_kv": nrm(ks[13], (DEPTH, D_MODEL, 2 * MEM_WIDTH), D_MODEL ** -0.5),
        "w_mem_out": nrm(ks[14], (DEPTH, MEM_WIDTH, D_MODEL), MEM_WIDTH ** -0.5),
        "w_out": nrm(ks[15], (DEPTH, D_MODEL, D_MODEL), D_MODEL ** -0.5),
        "norm_mix_post": gain(ks[16], (DEPTH, D_MODEL)),
        "norm_ffn_pre": gain(ks[17], (DEPTH, D_MODEL)),
        "w_ffn_in": nrm(ks[18], (DEPTH, D_MODEL, 2 * FFN_HIDDEN), D_MODEL ** -0.5),
        "w_ffn_out": nrm(ks[19], (DEPTH, FFN_HIDDEN, D_MODEL), FFN_HIDDEN ** -0.5),
        "norm_ffn_post": gain(ks[20], (DEPTH, D_MODEL)),
    }


def _fwd_reference(x, mem, rel_bias, norm_mix_pre, w_in, b_gate, conv_dw, conv_dw_bias, conv_ln_g,
              conv_ln_b, w_conv_out, w_att_out, norm_mem, w_mem_kv, w_mem_out, w_out,
              norm_mix_post, norm_ffn_pre, w_ffn_in, w_ffn_out, norm_ffn_post):
    B, S, _ = x.shape
    c1 = 2 * CONV_WIDTH
    c2 = c1 + 3 * ATT_WIDTH
    c3 = c2 + MEM_WIDTH
    for l in range(DEPTH):
        h = _rms_norm(x, norm_mix_pre[l])
        z = h @ w_in[l]
        y_conv = _conv_module(z[..., :c1], conv_dw[l], conv_dw_bias[l], conv_ln_g[l],
                              conv_ln_b[l], w_conv_out[l])
        y_att = _dilated_attention(z[..., c1:c2], rel_bias) @ w_att_out[l]
        y_mem = _memory_attention(z[..., c2:c3], mem, norm_mem[l], w_mem_kv[l], w_mem_out[l])
        gates = jax.nn.sigmoid(z[..., c3:] + b_gate[l]).reshape(B, S, N_BRANCHES, D_MODEL)
        merged = gates[:, :, 0] * y_conv + gates[:, :, 1] * y_att + gates[:, :, 2] * y_mem
        x = x + _rms_norm(merged @ w_out[l], norm_mix_post[l])
        h = _rms_norm(x, norm_ffn_pre[l])
        gu = h @ w_ffn_in[l]
        g_ff, u_ff = gu[..., :FFN_HIDDEN], gu[..., FFN_HIDDEN:]
        x = x + _rms_norm((jax.nn.silu(g_ff) * u_ff) @ w_ffn_out[l], norm_ffn_post[l])
    return x


import jax as _jax
import jax.numpy as _jnp

TWIN_FORMAT = 'train_step'
FWD_PARAMS = ['x', 'mem', 'rel_bias', 'norm_mix_pre', 'w_in', 'b_gate', 'conv_dw', 'conv_dw_bias', 'conv_ln_g', 'conv_ln_b', 'w_conv_out', 'w_att_out', 'norm_mem', 'w_mem_kv', 'w_mem_out', 'w_out', 'norm_mix_post', 'norm_ffn_pre', 'w_ffn_in', 'w_ffn_out', 'norm_ffn_post']
TWIN_WEIGHTS = ['rel_bias', 'norm_mix_pre', 'w_in', 'b_gate', 'conv_dw', 'conv_dw_bias', 'conv_ln_g', 'conv_ln_b', 'w_conv_out', 'w_att_out', 'norm_mem', 'w_mem_kv', 'w_mem_out', 'w_out', 'norm_mix_post', 'norm_ffn_pre', 'w_ffn_in', 'w_ffn_out', 'norm_ffn_post']
TWIN_DIFF_INPUT = 'x'
TWIN_INPUTS = ['x', 'mem', 'rel_bias', 'norm_mix_pre', 'w_in', 'b_gate', 'conv_dw', 'conv_dw_bias', 'conv_ln_g', 'conv_ln_b', 'w_conv_out', 'w_att_out', 'norm_mem', 'w_mem_kv', 'w_mem_out', 'w_out', 'norm_mix_post', 'norm_ffn_pre', 'w_ffn_in', 'w_ffn_out', 'norm_ffn_post', 'loss_target', 'm_rel_bias', 'm_norm_mix_pre', 'm_w_in', 'm_b_gate', 'm_conv_dw', 'm_conv_dw_bias', 'm_conv_ln_g', 'm_conv_ln_b', 'm_w_conv_out', 'm_w_att_out', 'm_norm_mem', 'm_w_mem_kv', 'm_w_mem_out', 'm_w_out', 'm_norm_mix_post', 'm_norm_ffn_pre', 'm_w_ffn_in', 'm_w_ffn_out', 'm_norm_ffn_post', 'v_rel_bias', 'v_norm_mix_pre', 'v_w_in', 'v_b_gate', 'v_conv_dw', 'v_conv_dw_bias', 'v_conv_ln_g', 'v_conv_ln_b', 'v_w_conv_out', 'v_w_att_out', 'v_norm_mem', 'v_w_mem_kv', 'v_w_mem_out', 'v_w_out', 'v_norm_mix_post', 'v_norm_ffn_pre', 'v_w_ffn_in', 'v_w_ffn_out', 'v_norm_ffn_post']
TWIN_OUTPUTS = ['loss', 'grad_x', 'grad_rel_bias', 'grad_norm_mix_pre', 'grad_w_in', 'grad_b_gate', 'grad_conv_dw', 'grad_conv_dw_bias', 'grad_conv_ln_g', 'grad_conv_ln_b', 'grad_w_conv_out', 'grad_w_att_out', 'grad_norm_mem', 'grad_w_mem_kv', 'grad_w_mem_out', 'grad_w_out', 'grad_norm_mix_post', 'grad_norm_ffn_pre', 'grad_w_ffn_in', 'grad_w_ffn_out', 'grad_norm_ffn_post', 'delta_rel_bias', 'delta_norm_mix_pre', 'delta_w_in', 'delta_b_gate', 'delta_conv_dw', 'delta_conv_dw_bias', 'delta_conv_ln_g', 'delta_conv_ln_b', 'delta_w_conv_out', 'delta_w_att_out', 'delta_norm_mem', 'delta_w_mem_kv', 'delta_w_mem_out', 'delta_w_out', 'delta_norm_mix_post', 'delta_norm_ffn_pre', 'delta_w_ffn_in', 'delta_w_ffn_out', 'delta_norm_ffn_post', 'new_m_rel_bias', 'new_m_norm_mix_pre', 'new_m_w_in', 'new_m_b_gate', 'new_m_conv_dw', 'new_m_conv_dw_bias', 'new_m_conv_ln_g', 'new_m_conv_ln_b', 'new_m_w_conv_out', 'new_m_w_att_out', 'new_m_norm_mem', 'new_m_w_mem_kv', 'new_m_w_mem_out', 'new_m_w_out', 'new_m_norm_mix_post', 'new_m_norm_ffn_pre', 'new_m_w_ffn_in', 'new_m_w_ffn_out', 'new_m_norm_ffn_post', 'new_v_rel_bias', 'new_v_norm_mix_pre', 'new_v_w_in', 'new_v_b_gate', 'new_v_conv_dw', 'new_v_conv_dw_bias', 'new_v_conv_ln_g', 'new_v_conv_ln_b', 'new_v_w_conv_out', 'new_v_w_att_out', 'new_v_norm_mem', 'new_v_w_mem_kv', 'new_v_w_mem_out', 'new_v_w_out', 'new_v_norm_mix_post', 'new_v_norm_ffn_pre', 'new_v_w_ffn_in', 'new_v_w_ffn_out', 'new_v_norm_ffn_post']
TWIN_LEAF_KINDS = {'loss': 'loss', 'grad_x': 'grad_x', 'grad_rel_bias': 'grad_w', 'grad_norm_mix_pre': 'grad_w', 'grad_w_in': 'grad_w', 'grad_b_gate': 'grad_w', 'grad_conv_dw': 'grad_w', 'grad_conv_dw_bias': 'grad_w', 'grad_conv_ln_g': 'grad_w', 'grad_conv_ln_b': 'grad_w', 'grad_w_conv_out': 'grad_w', 'grad_w_att_out': 'grad_w', 'grad_norm_mem': 'grad_w', 'grad_w_mem_kv': 'grad_w', 'grad_w_mem_out': 'grad_w', 'grad_w_out': 'grad_w', 'grad_norm_mix_post': 'grad_w', 'grad_norm_ffn_pre': 'grad_w', 'grad_w_ffn_in': 'grad_w', 'grad_w_ffn_out': 'grad_w', 'grad_norm_ffn_post': 'grad_w', 'delta_rel_bias': 'delta_w', 'delta_norm_mix_pre': 'delta_w', 'delta_w_in': 'delta_w', 'delta_b_gate': 'delta_w', 'delta_conv_dw': 'delta_w', 'delta_conv_dw_bias': 'delta_w', 'delta_conv_ln_g': 'delta_w', 'delta_conv_ln_b': 'delta_w', 'delta_w_conv_out': 'delta_w', 'delta_w_att_out': 'delta_w', 'delta_norm_mem': 'delta_w', 'delta_w_mem_kv': 'delta_w', 'delta_w_mem_out': 'delta_w', 'delta_w_out': 'delta_w', 'delta_norm_mix_post': 'delta_w', 'delta_norm_ffn_pre': 'delta_w', 'delta_w_ffn_in': 'delta_w', 'delta_w_ffn_out': 'delta_w', 'delta_norm_ffn_post': 'delta_w', 'new_m_rel_bias': 'new_m', 'new_m_norm_mix_pre': 'new_m', 'new_m_w_in': 'new_m', 'new_m_b_gate': 'new_m', 'new_m_conv_dw': 'new_m', 'new_m_conv_dw_bias': 'new_m', 'new_m_conv_ln_g': 'new_m', 'new_m_conv_ln_b': 'new_m', 'new_m_w_conv_out': 'new_m', 'new_m_w_att_out': 'new_m', 'new_m_norm_mem': 'new_m', 'new_m_w_mem_kv': 'new_m', 'new_m_w_mem_out': 'new_m', 'new_m_w_out': 'new_m', 'new_m_norm_mix_post': 'new_m', 'new_m_norm_ffn_pre': 'new_m', 'new_m_w_ffn_in': 'new_m', 'new_m_w_ffn_out': 'new_m', 'new_m_norm_ffn_post': 'new_m', 'new_v_rel_bias': 'new_v', 'new_v_norm_mix_pre': 'new_v', 'new_v_w_in': 'new_v', 'new_v_b_gate': 'new_v', 'new_v_conv_dw': 'new_v', 'new_v_conv_dw_bias': 'new_v', 'new_v_conv_ln_g': 'new_v', 'new_v_conv_ln_b': 'new_v', 'new_v_w_conv_out': 'new_v', 'new_v_w_att_out': 'new_v', 'new_v_norm_mem': 'new_v', 'new_v_w_mem_kv': 'new_v', 'new_v_w_mem_out': 'new_v', 'new_v_w_out': 'new_v', 'new_v_norm_mix_post': 'new_v', 'new_v_norm_ffn_pre': 'new_v', 'new_v_w_ffn_in': 'new_v', 'new_v_w_ffn_out': 'new_v', 'new_v_norm_ffn_post': 'new_v'}


def _forward(args):
    return _fwd_reference(*[args[k] for k in FWD_PARAMS])


def _output_shape():
    def fwd():
        inp = _fwd_setup_inputs(0)
        return _fwd_reference(*[inp[k] for k in FWD_PARAMS])
    out = _jax.eval_shape(fwd)
    return out.shape, out.dtype

N_MICROBATCH = 1
ADAM_LR = 0.001
ADAM_B1 = 0.9
ADAM_B2 = 0.999
ADAM_EPS = 1e-08
ADAM_WD = 0.01
ADAM_STEP = 10
PER_EXAMPLE_BATCH_AXIS = {'x': 0, 'mem': 0, 'loss_target': 0}
SHARED_INPUTS = []
_WEIGHT_DTYPES = {'rel_bias': _jnp.float32, 'norm_mix_pre': _jnp.float32, 'w_in': _jnp.float32, 'b_gate': _jnp.float32, 'conv_dw': _jnp.float32, 'conv_dw_bias': _jnp.float32, 'conv_ln_g': _jnp.float32, 'conv_ln_b': _jnp.float32, 'w_conv_out': _jnp.float32, 'w_att_out': _jnp.float32, 'norm_mem': _jnp.float32, 'w_mem_kv': _jnp.float32, 'w_mem_out': _jnp.float32, 'w_out': _jnp.float32, 'norm_mix_post': _jnp.float32, 'norm_ffn_pre': _jnp.float32, 'w_ffn_in': _jnp.float32, 'w_ffn_out': _jnp.float32, 'norm_ffn_post': _jnp.float32}
MOMENT_SCALE = {'rel_bias': 1.567921e+00, 'norm_mix_pre': 2.591237e+00, 'w_in': 9.732923e-01, 'b_gate': 2.020532e+00, 'conv_dw': 3.309761e+00, 'conv_dw_bias': 5.383821e+01, 'conv_ln_g': 2.029064e+01, 'conv_ln_b': 2.942487e+01, 'w_conv_out': 8.125305e+00, 'w_att_out': 2.127598e+00, 'norm_mem': 8.762870e-01, 'w_mem_kv': 7.611102e-01, 'w_mem_out': 7.970745e-01, 'w_out': 8.109726e+00, 'norm_mix_post': 6.581107e+01, 'norm_ffn_pre': 3.373761e+00, 'w_ffn_in': 1.399581e+00, 'w_ffn_out': 2.960067e+00, 'norm_ffn_post': 6.399174e+01}


def _to_microbatches(a, axis):
    t = _jnp.moveaxis(a, axis, 0)
    t = t.reshape((N_MICROBATCH, t.shape[0] // N_MICROBATCH) + t.shape[1:])
    return _jnp.moveaxis(t, 1, axis + 1)


def setup_inputs(seed: int = 0) -> dict:
    inp = _fwd_setup_inputs(seed)
    key = _jax.random.fold_in(_jax.random.key(seed), 7919)
    shape, _ = _output_shape()
    out = dict(inp)
    out["loss_target"] = _jax.random.normal(_jax.random.fold_in(key, 0), shape, _jnp.float32)
    for i, name in enumerate(TWIN_WEIGHTS):
        w = inp[name].astype(_jnp.float32)
        if MOMENT_SCALE is None:
            s = _jnp.sqrt(_jnp.mean(_jnp.square(w)) + 1e-30)
        else:
            s = MOMENT_SCALE[name]
        km, kv = _jax.random.split(_jax.random.fold_in(key, i + 1))
        out[name] = w
        out["m_" + name] = s * _jax.random.normal(km, w.shape, _jnp.float32)
        out["v_" + name] = (s * s) * _jax.random.uniform(kv, w.shape, _jnp.float32, 0.5, 1.5)
    if N_MICROBATCH > 1:
        for name, axis in PER_EXAMPLE_BATCH_AXIS.items():
            out[name] = _to_microbatches(out[name], axis)
    return {'x': out['x'], 'mem': out['mem'], 'rel_bias': out['rel_bias'], 'norm_mix_pre': out['norm_mix_pre'], 'w_in': out['w_in'], 'b_gate': out['b_gate'], 'conv_dw': out['conv_dw'], 'conv_dw_bias': out['conv_dw_bias'], 'conv_ln_g': out['conv_ln_g'], 'conv_ln_b': out['conv_ln_b'], 'w_conv_out': out['w_conv_out'], 'w_att_out': out['w_att_out'], 'norm_mem': out['norm_mem'], 'w_mem_kv': out['w_mem_kv'], 'w_mem_out': out['w_mem_out'], 'w_out': out['w_out'], 'norm_mix_post': out['norm_mix_post'], 'norm_ffn_pre': out['norm_ffn_pre'], 'w_ffn_in': out['w_ffn_in'], 'w_ffn_out': out['w_ffn_out'], 'norm_ffn_post': out['norm_ffn_post'], 'loss_target': out['loss_target'], 'm_rel_bias': out['m_rel_bias'], 'm_norm_mix_pre': out['m_norm_mix_pre'], 'm_w_in': out['m_w_in'], 'm_b_gate': out['m_b_gate'], 'm_conv_dw': out['m_conv_dw'], 'm_conv_dw_bias': out['m_conv_dw_bias'], 'm_conv_ln_g': out['m_conv_ln_g'], 'm_conv_ln_b': out['m_conv_ln_b'], 'm_w_conv_out': out['m_w_conv_out'], 'm_w_att_out': out['m_w_att_out'], 'm_norm_mem': out['m_norm_mem'], 'm_w_mem_kv': out['m_w_mem_kv'], 'm_w_mem_out': out['m_w_mem_out'], 'm_w_out': out['m_w_out'], 'm_norm_mix_post': out['m_norm_mix_post'], 'm_norm_ffn_pre': out['m_norm_ffn_pre'], 'm_w_ffn_in': out['m_w_ffn_in'], 'm_w_ffn_out': out['m_w_ffn_out'], 'm_norm_ffn_post': out['m_norm_ffn_post'], 'v_rel_bias': out['v_rel_bias'], 'v_norm_mix_pre': out['v_norm_mix_pre'], 'v_w_in': out['v_w_in'], 'v_b_gate': out['v_b_gate'], 'v_conv_dw': out['v_conv_dw'], 'v_conv_dw_bias': out['v_conv_dw_bias'], 'v_conv_ln_g': out['v_conv_ln_g'], 'v_conv_ln_b': out['v_conv_ln_b'], 'v_w_conv_out': out['v_w_conv_out'], 'v_w_att_out': out['v_w_att_out'], 'v_norm_mem': out['v_norm_mem'], 'v_w_mem_kv': out['v_w_mem_kv'], 'v_w_mem_out': out['v_w_mem_out'], 'v_w_out': out['v_w_out'], 'v_norm_mix_post': out['v_norm_mix_post'], 'v_norm_ffn_pre': out['v_norm_ffn_pre'], 'v_w_ffn_in': out['v_w_ffn_in'], 'v_w_ffn_out': out['v_w_ffn_out'], 'v_norm_ffn_post': out['v_norm_ffn_post']}


def _loss(weights, diff, rest, loss_target):
    with _jax.named_scope("forward"):
        args = {**rest, TWIN_DIFF_INPUT: diff, **{k: w.astype(_WEIGHT_DTYPES[k]) for k, w in weights.items()}}
        y = _forward(args)
    with _jax.named_scope("loss_head"):
        err = _jnp.square(y.astype(_jnp.float32) - loss_target)
        return 0.5 * _jnp.sum(_jnp.mean(err, axis=-1)) if err.ndim else 0.5 * err


def _adamw(w, g, m, v):
    m = ADAM_B1 * m + (1.0 - ADAM_B1) * g
    v = ADAM_B2 * v + (1.0 - ADAM_B2) * _jnp.square(g)
    m_hat = m / (1.0 - ADAM_B1 ** ADAM_STEP)
    v_hat = v / (1.0 - ADAM_B2 ** ADAM_STEP)
    delta = -ADAM_LR * (m_hat / (_jnp.sqrt(v_hat) + ADAM_EPS) + ADAM_WD * w)
    return delta, m, v


def reference(x, mem, rel_bias, norm_mix_pre, w_in, b_gate, conv_dw, conv_dw_bias, conv_ln_g, conv_ln_b, w_conv_out, w_att_out, norm_mem, w_mem_kv, w_mem_out, w_out, norm_mix_post, norm_ffn_pre, w_ffn_in, w_ffn_out, norm_ffn_post, loss_target, m_rel_bias, m_norm_mix_pre, m_w_in, m_b_gate, m_conv_dw, m_conv_dw_bias, m_conv_ln_g, m_conv_ln_b, m_w_conv_out, m_w_att_out, m_norm_mem, m_w_mem_kv, m_w_mem_out, m_w_out, m_norm_mix_post, m_norm_ffn_pre, m_w_ffn_in, m_w_ffn_out, m_norm_ffn_post, v_rel_bias, v_norm_mix_pre, v_w_in, v_b_gate, v_conv_dw, v_conv_dw_bias, v_conv_ln_g, v_conv_ln_b, v_w_conv_out, v_w_att_out, v_norm_mem, v_w_mem_kv, v_w_mem_out, v_w_out, v_norm_mix_post, v_norm_ffn_pre, v_w_ffn_in, v_w_ffn_out, v_norm_ffn_post):
    given = dict(x=x, mem=mem, rel_bias=rel_bias, norm_mix_pre=norm_mix_pre, w_in=w_in, b_gate=b_gate, conv_dw=conv_dw, conv_dw_bias=conv_dw_bias, conv_ln_g=conv_ln_g, conv_ln_b=conv_ln_b, w_conv_out=w_conv_out, w_att_out=w_att_out, norm_mem=norm_mem, w_mem_kv=w_mem_kv, w_mem_out=w_mem_out, w_out=w_out, norm_mix_post=norm_mix_post, norm_ffn_pre=norm_ffn_pre, w_ffn_in=w_ffn_in, w_ffn_out=w_ffn_out, norm_ffn_post=norm_ffn_post, loss_target=loss_target, m_rel_bias=m_rel_bias, m_norm_mix_pre=m_norm_mix_pre, m_w_in=m_w_in, m_b_gate=m_b_gate, m_conv_dw=m_conv_dw, m_conv_dw_bias=m_conv_dw_bias, m_conv_ln_g=m_conv_ln_g, m_conv_ln_b=m_conv_ln_b, m_w_conv_out=m_w_conv_out, m_w_att_out=m_w_att_out, m_norm_mem=m_norm_mem, m_w_mem_kv=m_w_mem_kv, m_w_mem_out=m_w_mem_out, m_w_out=m_w_out, m_norm_mix_post=m_norm_mix_post, m_norm_ffn_pre=m_norm_ffn_pre, m_w_ffn_in=m_w_ffn_in, m_w_ffn_out=m_w_ffn_out, m_norm_ffn_post=m_norm_ffn_post, v_rel_bias=v_rel_bias, v_norm_mix_pre=v_norm_mix_pre, v_w_in=v_w_in, v_b_gate=v_b_gate, v_conv_dw=v_conv_dw, v_conv_dw_bias=v_conv_dw_bias, v_conv_ln_g=v_conv_ln_g, v_conv_ln_b=v_conv_ln_b, v_w_conv_out=v_w_conv_out, v_w_att_out=v_w_att_out, v_norm_mem=v_norm_mem, v_w_mem_kv=v_w_mem_kv, v_w_mem_out=v_w_mem_out, v_w_out=v_w_out, v_norm_mix_post=v_norm_mix_post, v_norm_ffn_pre=v_norm_ffn_pre, v_w_ffn_in=v_w_ffn_in, v_w_ffn_out=v_w_ffn_out, v_norm_ffn_post=v_norm_ffn_post)
    weights = {n: given[n] for n in TWIN_WEIGHTS}
    shared = {n: given[n] for n in SHARED_INPUTS}
    per_example = {n: given[n] for n in ['x', 'mem']}
    grad_fn = _jax.value_and_grad(_loss, argnums=(0, 1))

    def one_microbatch(ex, loss_target):
        ex = dict(ex)
        diff = ex.pop(TWIN_DIFF_INPUT)
        return grad_fn(weights, diff, {**shared, **ex}, loss_target)

    if N_MICROBATCH == 1:
        loss, (grad_w, grad_x) = one_microbatch(per_example, given["loss_target"])
    else:
        def body(carry, xs):
            loss_sum, grad_sum = carry
            l_k, (gw_k, gx_k) = one_microbatch(xs[0], xs[1])
            with _jax.named_scope("update"):
                return (loss_sum + l_k, _jax.tree.map(_jnp.add, grad_sum, gw_k)), gx_k

        init = (_jnp.zeros((), _jnp.float32), _jax.tree.map(_jnp.zeros_like, weights))
        (loss, grad_w), grad_x = _jax.lax.scan(body, init, (per_example, given["loss_target"]))
    with _jax.named_scope("update"):
        delta_w, new_m, new_v = {}, {}, {}
        for n in TWIN_WEIGHTS:
            delta_w[n], new_m[n], new_v[n] = _adamw(weights[n], grad_w[n], given["m_" + n], given["v_" + n])
    return (loss, grad_x, *[grad_w[n] for n in TWIN_WEIGHTS], *[delta_w[n] for n in TWIN_WEIGHTS],
            *[new_m[n] for n in TWIN_WEIGHTS], *[new_v[n] for n in TWIN_WEIGHTS])
```

```python
import functools

import numpy as np
import jax
import jax.numpy as jnp
from jax import lax
from jax.experimental import pallas as pl
from jax.experimental.pallas import tpu as pltpu

F32 = jnp.float32
BF16 = jnp.bfloat16

N_DEV = 8
D = 1024
DEPTH = 2
CONV_W = 512
CONV_K = 31
CONV_PAD = 16
ATT_PATTERNS = ((128, 1), (512, 4), (2048, 16))
ATT_RADIUS = 64
HEADS = 4
HEAD_DIM = 64
ATT_GW = HEADS * HEAD_DIM
ATT_W = 3 * ATT_GW
MEM_HEADS = 4
MEM_HD = 128
MEM_W = 512
N_MEM = 256
FFN_H = 2816
NUM_BUCKETS = 32
MAX_DISTANCE = 1024
RMS_EPS = 1e-6
LN_EPS = 1e-5
NEG_INF = -1e30
SEG = (2 * CONV_W, 3 * ATT_W, MEM_W, 3 * D)
ADAM_LR, ADAM_B1, ADAM_B2, ADAM_EPS, ADAM_WD, ADAM_STEP = 0.001, 0.9, 0.999, 1e-08, 0.01, 10

SLAB_ALIGN = 128
VMEM_LIMIT_V7X = 48 * 1024 * 1024
MESH = pl.DeviceIdType.MESH


def _cp(n_axes):
    return pltpu.CompilerParams(dimension_semantics=("arbitrary",) * n_axes, vmem_limit_bytes=VMEM_LIMIT_V7X)


def _row(v):
    return v.reshape(1, -1)


def _mm(a, b, name, *, ta=False, tb=False, out_dtype=BF16, acc=None, tm=1024, tn=1024, tk=1024):
    kdim, m = a.shape if ta else a.shape[::-1]
    n, kb = b.shape if tb else b.shape[::-1]
    assert kb == kdim and a.dtype == BF16 and b.dtype == BF16
    tm, tn, tk = min(tm, m), min(tn, n), min(tk, kdim)
    assert m % tm == 0 and n % tn == 0 and kdim % tk == 0, (name, m, n, kdim, tm, tn, tk)
    nk = kdim // tk
    a_spec = pl.BlockSpec((tk, tm), lambda i, j, k: (k, i)) if ta else pl.BlockSpec((tm, tk), lambda i, j, k: (i, k))
    b_spec = pl.BlockSpec((tn, tk), lambda i, j, k: (j, k)) if tb else pl.BlockSpec((tk, tn), lambda i, j, k: (k, j))
    o_spec = pl.BlockSpec((tm, tn), lambda i, j, k: (i, j))
    dims = (((0 if ta else 1,), (1 if tb else 0,)), ((), ()))
    has_acc = acc is not None
    use_scratch = nk > 1 and out_dtype != F32
    assert not has_acc or out_dtype == F32

    def body(*refs):
        a_ref, b_ref = refs[0], refs[1]
        o_ref = refs[2 + has_acc]
        p = lax.dot_general(a_ref[...], b_ref[...], dims, preferred_element_type=F32)
        if nk == 1:
            if has_acc:
                p = p + refs[2][...]
            o_ref[...] = p.astype(o_ref.dtype)
            return
        k = pl.program_id(2)
        t_ref = refs[3 + has_acc] if use_scratch else o_ref

        @pl.when(k == 0)
        def _():
            t_ref[...] = (p + refs[2][...]) if has_acc else p

        @pl.when(k > 0)
        def _():
            t_ref[...] += p

        if use_scratch:
            @pl.when(k == nk - 1)
            def _():
                o_ref[...] = t_ref[...].astype(o_ref.dtype)

    return pl.pallas_call(
        body, name=name, grid=(m // tm, n // tn, nk),
        in_specs=[a_spec, b_spec] + ([o_spec] if has_acc else []), out_specs=o_spec,
        out_shape=jax.ShapeDtypeStruct((m, n), out_dtype),
        scratch_shapes=[pltpu.VMEM((tm, tn), F32)] if use_scratch else [],
        compiler_params=_cp(3),
    )(*((a, b, acc) if has_acc else (a, b)))


def _rms_val(v, g):
    return v * lax.rsqrt(jnp.mean(v * v, axis=-1, keepdims=True) + RMS_EPS) * g


def _norm_plain(v, g, name, tr=512):
    t, d = v.shape
    tr = min(tr, t)

    def body(v_ref, g_ref, o_ref):
        o_ref[...] = _rms_val(v_ref[...], g_ref[...]).astype(BF16)

    return pl.pallas_call(
        body, name=name, grid=(t // tr,),
        in_specs=[pl.BlockSpec((tr, d), lambda i: (i, 0)), pl.BlockSpec((1, d), lambda i: (0, 0))],
        out_specs=pl.BlockSpec((tr, d), lambda i: (i, 0)),
        out_shape=jax.ShapeDtypeStruct((t, d), BF16), compiler_params=_cp(1),
    )(v, _row(g))


def _norm_res(xres, v, g, g_next, name, tr=512):
    t, d = v.shape
    tr = min(tr, t)
    two = g_next is not None

    def body(*refs):
        x_ref, v_ref, g_ref = refs[:3]
        xn = x_ref[...] + _rms_val(v_ref[...], g_ref[...])
        refs[3 + two][...] = xn
        if two:
            refs[5][...] = _rms_val(xn, refs[3][...]).astype(BF16)

    row = pl.BlockSpec((tr, d), lambda i: (i, 0))
    vec = pl.BlockSpec((1, d), lambda i: (0, 0))
    outs = pl.pallas_call(
        body, name=name, grid=(t // tr,),
        in_specs=[row, row, vec] + ([vec] if two else []),
        out_specs=[row, row] if two else [row],
        out_shape=[jax.ShapeDtypeStruct((t, d), F32)] + ([jax.ShapeDtypeStruct((t, d), BF16)] if two else []),
        compiler_params=_cp(1),
    )(*((xres, v, _row(g)) + ((_row(g_next),) if two else ())))
    return (outs[0], outs[1]) if two else (outs[0], None)


def _norm_bwd(v, g, dout, dres, out_dtype, name, tr=512):
    t, d = v.shape
    tr = min(tr, t)
    nt = t // tr
    has_res = dres is not None

    def body(*refs):
        v_ref, g_ref, do_ref = refs[:3]
        dv_ref, dg_ref = refs[3 + has_res], refs[4 + has_res]
        i = pl.program_id(0)
        vv = v_ref[...]
        dy = do_ref[...].astype(F32)
        r = lax.rsqrt(jnp.mean(vv * vv, axis=-1, keepdims=True) + RMS_EPS)
        vhat = vv * r
        dvh = dy * g_ref[...]
        dv = r * (dvh - vhat * jnp.mean(dvh * vhat, axis=-1, keepdims=True))
        if has_res:
            dv = dv + refs[3][...]
        dv_ref[...] = dv.astype(dv_ref.dtype)
        part = jnp.sum(dy * vhat, axis=0, keepdims=True)

        @pl.when(i == 0)
        def _():
            dg_ref[...] = jnp.zeros_like(dg_ref)

        dg_ref[0:1, :] += part

    row = pl.BlockSpec((tr, d), lambda i: (i, 0))
    dv, dg = pl.pallas_call(
        body, name=name, grid=(nt,),
        in_specs=[row, pl.BlockSpec((1, d), lambda i: (0, 0)), row] + ([row] if has_res else []),
        out_specs=[row, pl.BlockSpec((8, d), lambda i: (0, 0))],
        out_shape=[jax.ShapeDtypeStruct((t, d), out_dtype), jax.ShapeDtypeStruct((8, d), F32)],
        compiler_params=_cp(1),
    )(*((v, _row(g), dout) + ((dres,) if has_res else ())))
    return dv, dg[0]


def _loss_head(y, target, name, tr=512):
    t, d = y.shape
    tr = min(tr, t)
    nt = t // tr

    def body(y_ref, t_ref, dy_ref, l_ref):
        i = pl.program_id(0)
        err = y_ref[...] - t_ref[...]
        dy_ref[...] = err * (1.0 / d)

        @pl.when(i == 0)
        def _():
            l_ref[...] = jnp.zeros_like(l_ref)

        l_ref[...] += jnp.sum(err * err) * (0.5 / d)

    row = pl.BlockSpec((tr, d), lambda i: (i, 0))
    dy, l = pl.pallas_call(
        body, name=name, grid=(nt,), in_specs=[row, row],
        out_specs=[row, pl.BlockSpec((8, 128), lambda i: (0, 0))],
        out_shape=[jax.ShapeDtypeStruct((t, d), F32), jax.ShapeDtypeStruct((8, 128), F32)],
        compiler_params=_cp(1),
    )(y, target)
    return l[0, 0], dy


def _sigmoid(v):
    return 1.0 / (1.0 + jnp.exp(-v))


def _halo_specs(tq, width, nt, halo):
    per = tq // halo
    last = nt * per - 1
    main = pl.BlockSpec((tq, width), lambda i: (i, 0))
    prev = pl.BlockSpec((halo, width), lambda i: (jnp.maximum(i * per - 1, 0), 0))
    nxt = pl.BlockSpec((halo, width), lambda i: (jnp.minimum((i + 1) * per, last), 0))
    return prev, main, nxt


def _glu_window(zp, zm, zn, i, nt):
    def glu(z):
        z = z.astype(F32)
        return z[:, :CONV_W] * _sigmoid(z[:, CONV_W:])

    up = jnp.where(i > 0, glu(zp), 0.0)
    un = jnp.where(i < nt - 1, glu(zn), 0.0)
    return jnp.concatenate([up, glu(zm), un], axis=0)


def _shifted(win, shift, rows):
    if shift == 0:
        return win[:rows]
    return pltpu.roll(win, win.shape[0] - shift, 0)[:rows]


def _conv_fwd(zc, w, b, ln_g, ln_b, name, tq=256):
    t = zc.shape[0]
    tq = min(tq, t)
    nt = t // tq

    def body(zp_ref, zm_ref, zn_ref, w_ref, b_ref, g_ref, bb_ref, y_ref, c_ref):
        i = pl.program_id(0)
        win = _glu_window(zp_ref[...], zm_ref[...], zn_ref[...], i, nt)
        wv = w_ref[...]
        y = jnp.zeros((tq, CONV_W), F32) + b_ref[...]
        for k in range(CONV_K):
            y = y + _shifted(win, k + 1, tq) * wv[k:k + 1, :]
        y_ref[...] = y
        mu = jnp.mean(y, axis=-1, keepdims=True)
        yc = y - mu
        ln = yc * lax.rsqrt(jnp.mean(yc * yc, axis=-1, keepdims=True) + LN_EPS) * g_ref[...] + bb_ref[...]
        c_ref[...] = (ln * _sigmoid(ln)).astype(BF16)

    vec = pl.BlockSpec((1, CONV_W), lambda i: (0, 0))
    out = pl.BlockSpec((tq, CONV_W), lambda i: (i, 0))
    return pl.pallas_call(
        body, name=name, grid=(nt,),
        in_specs=[*_halo_specs(tq, 2 * CONV_W, nt, CONV_PAD), pl.BlockSpec((32, CONV_W), lambda i: (0, 0)), vec, vec, vec],
        out_specs=[out, out],
        out_shape=[jax.ShapeDtypeStruct((t, CONV_W), F32), jax.ShapeDtypeStruct((t, CONV_W), BF16)],
        compiler_params=_cp(1),
    )(zc, zc, zc, w, _row(b), _row(ln_g), _row(ln_b))


def _conv_bwd(zc, y, dc, w, ln_g, ln_b, name, tq=256):
    t = zc.shape[0]
    tq = min(tq, t)
    nt = t // tq
    rows = tq + 2 * CONV_PAD

    def body(zp_ref, zm_ref, zn_ref, yp_ref, ym_ref, yn_ref, dp_ref, dm_ref, dn_ref, w_ref, g_ref, bb_ref,
             dz_ref, dw_ref, db_ref, dg_ref, dbb_ref):
        i = pl.program_id(0)
        uwin = _glu_window(zp_ref[...], zm_ref[...], zn_ref[...], i, nt)
        ywin = jnp.concatenate([yp_ref[...], ym_ref[...], yn_ref[...]], axis=0)
        dcw = jnp.concatenate([dp_ref[...], dm_ref[...], dn_ref[...]], axis=0).astype(F32)
        mu = jnp.mean(ywin, axis=-1, keepdims=True)
        yc = ywin - mu
        rstd = lax.rsqrt(jnp.mean(yc * yc, axis=-1, keepdims=True) + LN_EPS)
        yhat = yc * rstd
        ln = yhat * g_ref[...] + bb_ref[...]
        sg = _sigmoid(ln)
        dl = dcw * (sg * (1.0 + ln * (1.0 - sg)))
        ridx = lax.broadcasted_iota(jnp.int32, (rows, 1), 0)
        inside = ((ridx >= CONV_PAD) | (i > 0)) & ((ridx < tq + CONV_PAD) | (i < nt - 1))
        dl = jnp.where(inside, dl, 0.0)
        dyh = dl * g_ref[...]
        dy = rstd * (dyh - jnp.mean(dyh, axis=-1, keepdims=True) - yhat * jnp.mean(dyh * yhat, axis=-1, keepdims=True))
        dy = jnp.where(inside, dy, 0.0)
        main = slice(CONV_PAD, CONV_PAD + tq)
        dlm, yhm, dym = dl[main], yhat[main], dy[main]

        @pl.when(i == 0)
        def _():
            dw_ref[...] = jnp.zeros_like(dw_ref)
            db_ref[...] = jnp.zeros_like(db_ref)
            dg_ref[...] = jnp.zeros_like(dg_ref)
            dbb_ref[...] = jnp.zeros_like(dbb_ref)

        dg_ref[0:1, :] += jnp.sum(dlm * yhm, axis=0, keepdims=True)
        dbb_ref[0:1, :] += jnp.sum(dlm, axis=0, keepdims=True)
        db_ref[0:1, :] += jnp.sum(dym, axis=0, keepdims=True)
        wv = w_ref[...]
        du = jnp.zeros((tq, CONV_W), F32)
        for k in range(CONV_K):
            du = du + _shifted(dy, 2 * CONV_PAD - 1 - k, tq) * wv[k:k + 1, :]
            dw_ref[k:k + 1, :] += jnp.sum(dym * _shifted(uwin, k + 1, tq), axis=0, keepdims=True)
        zm = zm_ref[...].astype(F32)
        a, gt = zm[:, :CONV_W], zm[:, CONV_W:]
        sgt = _sigmoid(gt)
        dz_ref[:, :CONV_W] = (du * sgt).astype(BF16)
        dz_ref[:, CONV_W:] = (du * a * sgt * (1.0 - sgt)).astype(BF16)

    vec = pl.BlockSpec((1, CONV_W), lambda i: (0, 0))
    acc = pl.BlockSpec((8, CONV_W), lambda i: (0, 0))
    dz, dw, db, dg, dbb = pl.pallas_call(
        body, name=name, grid=(nt,),
        in_specs=[*_halo_specs(tq, 2 * CONV_W, nt, CONV_PAD), *_halo_specs(tq, CONV_W, nt, CONV_PAD),
                  *_halo_specs(tq, CONV_W, nt, CONV_PAD), pl.BlockSpec((32, CONV_W), lambda i: (0, 0)), vec, vec],
        out_specs=[pl.BlockSpec((tq, 2 * CONV_W), lambda i: (i, 0)), pl.BlockSpec((32, CONV_W), lambda i: (0, 0)), acc, acc, acc],
        out_shape=[jax.ShapeDtypeStruct((t, 2 * CONV_W), BF16), jax.ShapeDtypeStruct((32, CONV_W), F32),
                   jax.ShapeDtypeStruct((8, CONV_W), F32), jax.ShapeDtypeStruct((8, CONV_W), F32),
                   jax.ShapeDtypeStruct((8, CONV_W), F32)],
        compiler_params=_cp(1),
    )(zc, zc, zc, y, y, y, dc, dc, dc, w, _row(ln_g), _row(ln_b))
    return dz, dw, db[0], dg[0], dbb[0]


def _att_tiles(seq_len):
    tq = min(256, seq_len)
    return tq, min(128, tq)


def _t5_bucket_np(rel):
    nb = NUM_BUCKETS // 2
    max_exact = nb // 2
    ret = np.where(rel > 0, nb, 0)
    n = np.abs(rel)
    nf = np.maximum(n, 1).astype(np.float32)
    large = max_exact + (np.log(nf / np.float32(max_exact)) / np.float32(np.log(MAX_DISTANCE / max_exact))
                         * np.float32(nb - max_exact)).astype(np.int32)
    large = np.minimum(large, nb - 1)
    return ret + np.where(n < max_exact, n, large)


def _bucket_index(sb, dilation):
    off = np.arange(sb + 2 * ATT_RADIUS)[None, :] - ATT_RADIUS - np.arange(sb)[:, None]
    idx = _t5_bucket_np(off * dilation).astype(np.int32)
    return np.where(np.abs(off) <= ATT_RADIUS, idx, -1).astype(np.int32)


def _att_fwd(q, k, v, bias, seq_len, name):
    t = q.shape[0]
    tq, sb = _att_tiles(seq_len)
    nbl = seq_len // tq
    nt = t // tq
    per = tq // ATT_RADIUS
    w = sb + 2 * ATT_RADIUS
    last_blk = t // ATT_RADIUS - 1

    def body(q_ref, kp_ref, kc_ref, kn_ref, vp_ref, vc_ref, vn_ref, b_ref, o_ref, l_ref):
        n = pl.program_id(0) % nbl
        kw = jnp.concatenate([kp_ref[...], kc_ref[...], kn_ref[...]], axis=0)
        vw = jnp.concatenate([vp_ref[...], vc_ref[...], vn_ref[...]], axis=0)
        kpos = n * tq - ATT_RADIUS + lax.broadcasted_iota(jnp.int32, (1, tq + 2 * ATT_RADIUS), 1)
        valid = (kpos >= 0) & (kpos < seq_len)
        for j in range(tq // sb):
            qj = q_ref[j * sb:(j + 1) * sb, :]
            kj, vj, okj = kw[j * sb:j * sb + w], vw[j * sb:j * sb + w], valid[:, j * sb:j * sb + w]
            outs, lses = [], []
            for h in range(HEADS):
                hs = slice(h * HEAD_DIM, (h + 1) * HEAD_DIM)
                s = lax.dot_general(qj[:, hs], kj[:, hs], (((1,), (1,)), ((), ())), preferred_element_type=F32)
                s = jnp.where(okj, s * (HEAD_DIM ** -0.5) + b_ref[h], NEG_INF)
                m = jnp.max(s, axis=-1, keepdims=True)
                e = jnp.exp(s - m)
                den = jnp.sum(e, axis=-1, keepdims=True)
                o = jnp.dot(e.astype(BF16), vj[:, hs], preferred_element_type=F32) / den
                outs.append(o)
                lses.append(jnp.broadcast_to(m + jnp.log(den), (sb, HEAD_DIM)))
            o_ref[j * sb:(j + 1) * sb, :] = jnp.concatenate(outs, axis=1)
            l_ref[j * sb:(j + 1) * sb, :] = jnp.concatenate(lses, axis=1)

    main = pl.BlockSpec((tq, ATT_GW), lambda i: (i, 0))
    prev = pl.BlockSpec((ATT_RADIUS, ATT_GW), lambda i: (jnp.maximum(i * per - 1, 0), 0))
    nxt = pl.BlockSpec((ATT_RADIUS, ATT_GW), lambda i: (jnp.minimum((i + 1) * per, last_blk), 0))
    return pl.pallas_call(
        body, name=name, grid=(nt,),
        in_specs=[main, prev, main, nxt, prev, main, nxt, pl.BlockSpec((HEADS, sb, w), lambda i: (0, 0, 0))],
        out_specs=[main, main],
        out_shape=[jax.ShapeDtypeStruct((t, ATT_GW), F32), jax.ShapeDtypeStruct((t, ATT_GW), F32)],
        compiler_params=_cp(1),
    )(q, k, k, k, v, v, v, bias)


def _att_combine(outs, lses, name, tr=512):
    t = outs[0].shape[0]
    tr = min(tr, t)

    def body(o0, o1, o2, l0, l1, l2, a_ref, lse_ref):
        la, lb, lc = l0[...], l1[...], l2[...]
        m = jnp.maximum(jnp.maximum(la, lb), lc)
        ea, eb, ec = jnp.exp(la - m), jnp.exp(lb - m), jnp.exp(lc - m)
        den = ea + eb + ec
        a_ref[...] = ((ea * o0[...] + eb * o1[...] + ec * o2[...]) / den).astype(BF16)
        lse_ref[...] = m + jnp.log(den)

    row = pl.BlockSpec((tr, ATT_GW), lambda i: (i, 0))
    return pl.pallas_call(
        body, name=name, grid=(t // tr,), in_specs=[row] * 6, out_specs=[row, row],
        out_shape=[jax.ShapeDtypeStruct((t, ATT_GW), BF16), jax.ShapeDtypeStruct((t, ATT_GW), F32)],
        compiler_params=_cp(1),
    )(*outs, *lses)


def _att_bwd(q, k, v, bias, o, do, lse, seq_len, name):
    t = q.shape[0]
    tq, sb = _att_tiles(seq_len)
    nbl = seq_len // tq
    n_res = t // seq_len
    per = tq // ATT_RADIUS
    w = sb + 2 * ATT_RADIUS
    last_blk = t // ATT_RADIUS - 1
    acc_rows = 2 * tq + ATT_RADIUS
    scale = HEAD_DIM ** -0.5

    def body(q_ref, kp_ref, kc_ref, kn_ref, vp_ref, vc_ref, vn_ref, b_ref, o_ref, do_ref, l_ref,
             dq_ref, dk_ref, dv_ref, db_ref, ak_ref, av_ref):
        r, n = pl.program_id(0), pl.program_id(1)

        @pl.when((r == 0) & (n == 0))
        def _():
            db_ref[...] = jnp.zeros_like(db_ref)

        @pl.when(n == 0)
        def _():
            ak_ref[...] = jnp.zeros_like(ak_ref)
            av_ref[...] = jnp.zeros_like(av_ref)

        @pl.when(n < nbl)
        def _():
            kw = jnp.concatenate([kp_ref[...], kc_ref[...], kn_ref[...]], axis=0)
            vw = jnp.concatenate([vp_ref[...], vc_ref[...], vn_ref[...]], axis=0)
            kpos = n * tq - ATT_RADIUS + lax.broadcasted_iota(jnp.int32, (1, tq + 2 * ATT_RADIUS), 1)
            valid = (kpos >= 0) & (kpos < seq_len)
            for j in range(tq // sb):
                rows = slice(j * sb, (j + 1) * sb)
                qj, doj = q_ref[rows, :], do_ref[rows, :]
                dd = doj.astype(F32) * o_ref[rows, :].astype(F32)
                lj = l_ref[rows, :]
                kj, vj, okj = kw[j * sb:j * sb + w], vw[j * sb:j * sb + w], valid[:, j * sb:j * sb + w]
                dqs, dks, dvs = [], [], []
                for h in range(HEADS):
                    hs = slice(h * HEAD_DIM, (h + 1) * HEAD_DIM)
                    s = lax.dot_general(qj[:, hs], kj[:, hs], (((1,), (1,)), ((), ())), preferred_element_type=F32)
                    s = jnp.where(okj, s * scale + b_ref[h], NEG_INF)
                    p = jnp.exp(s - lj[:, h * HEAD_DIM:h * HEAD_DIM + 1])
                    dp = lax.dot_general(doj[:, hs], vj[:, hs], (((1,), (1,)), ((), ())), preferred_element_type=F32)
                    ds = p * (dp - jnp.sum(dd[:, hs], axis=-1, keepdims=True))
                    db_ref[h] += ds
                    dsb = ds.astype(BF16)
                    dqs.append(jnp.dot(dsb, kj[:, hs], preferred_element_type=F32) * scale)
                    dks.append(lax.dot_general(dsb, qj[:, hs], (((0,), (0,)), ((), ())), preferred_element_type=F32) * scale)
                    dvs.append(lax.dot_general(p.astype(BF16), doj[:, hs], (((0,), (0,)), ((), ())), preferred_element_type=F32))
                dq_ref[rows, :] = jnp.concatenate(dqs, axis=1).astype(BF16)
                a0 = j * sb + tq - ATT_RADIUS
                ak_ref[a0:a0 + w, :] += jnp.concatenate(dks, axis=1)
                av_ref[a0:a0 + w, :] += jnp.concatenate(dvs, axis=1)

        dk_ref[...] = ak_ref[0:tq, :].astype(BF16)
        dv_ref[...] = av_ref[0:tq, :].astype(BF16)
        keep = tq + ATT_RADIUS
        nk, nv = ak_ref[tq:acc_rows, :], av_ref[tq:acc_rows, :]
        ak_ref[0:keep, :] = nk
        av_ref[0:keep, :] = nv
        ak_ref[keep:acc_rows, :] = jnp.zeros((acc_rows - keep, ATT_GW), F32)
        av_ref[keep:acc_rows, :] = jnp.zeros((acc_rows - keep, ATT_GW), F32)

    def tile(r, n):
        return r * nbl + jnp.minimum(n, nbl - 1)

    main = pl.BlockSpec((tq, ATT_GW), lambda r, n: (tile(r, n), 0))
    prev = pl.BlockSpec((ATT_RADIUS, ATT_GW), lambda r, n: (jnp.maximum(tile(r, n) * per - 1, 0), 0))
    nxt = pl.BlockSpec((ATT_RADIUS, ATT_GW), lambda r, n: (jnp.minimum((tile(r, n) + 1) * per, last_blk), 0))
    lag = pl.BlockSpec((tq, ATT_GW), lambda r, n: (r * nbl + jnp.maximum(n - 1, 0), 0))
    bspec = pl.BlockSpec((HEADS, sb, w), lambda r, n: (0, 0, 0))
    return pl.pallas_call(
        body, name=name, grid=(n_res, nbl + 1),
        in_specs=[main, prev, main, nxt, prev, main, nxt, bspec, main, main, main],
        out_specs=[main, lag, lag, bspec],
        out_shape=[jax.ShapeDtypeStruct((t, ATT_GW), BF16)] * 3 + [jax.ShapeDtypeStruct((HEADS, sb, w), F32)],
        scratch_shapes=[pltpu.VMEM((acc_rows, ATT_GW), F32), pltpu.VMEM((acc_rows, ATT_GW), F32)],
        compiler_params=_cp(2),
    )(q, k, k, k, v, v, v, bias, o, do, lse)


def _bias_grad(db, idx, name):
    _, sb, w = db.shape

    def body(db_ref, idx_ref, o_ref):
        ix = idx_ref[...]
        lane = lax.broadcasted_iota(jnp.int32, (1, 128), 1)
        rows = []
        for h in range(HEADS):
            d = db_ref[h]
            acc = jnp.zeros((1, 128), F32)
            for b in range(NUM_BUCKETS):
                acc = acc + jnp.where(lane == b, jnp.sum(jnp.where(ix == b, d, 0.0)), 0.0)
            rows.append(acc)
        o_ref[...] = jnp.concatenate(rows + [jnp.zeros((8 - HEADS, 128), F32)], axis=0)

    return pl.pallas_call(
        body, name=name, out_shape=jax.ShapeDtypeStruct((8, 128), F32),
        in_specs=[pl.BlockSpec(memory_space=pltpu.VMEM), pl.BlockSpec(memory_space=pltpu.VMEM)],
        out_specs=pl.BlockSpec(memory_space=pltpu.VMEM),
    )(db, idx)


def _mem_fwd(zq, kv, name, tq=512):
    t = zq.shape[0]
    tq = min(tq, t)
    scale = MEM_HD ** -0.5

    def body(q_ref, kv_ref, o_ref):
        outs = []
        for h in range(MEM_HEADS):
            hs = slice(h * MEM_HD, (h + 1) * MEM_HD)
            kh = kv_ref[:, h * MEM_HD:(h + 1) * MEM_HD]
            vh = kv_ref[:, MEM_W + h * MEM_HD:MEM_W + (h + 1) * MEM_HD]
            s = lax.dot_general(q_ref[:, hs], kh, (((1,), (1,)), ((), ())), preferred_element_type=F32) * scale
            e = jnp.exp(s - jnp.max(s, axis=-1, keepdims=True))
            den = jnp.sum(e, axis=-1, keepdims=True)
            outs.append(jnp.dot(e.astype(BF16), vh, preferred_element_type=F32) / den)
        o_ref[...] = jnp.concatenate(outs, axis=1).astype(BF16)

    row = pl.BlockSpec((tq, MEM_W), lambda i: (i, 0))
    return pl.pallas_call(
        body, name=name, grid=(t // tq,),
        in_specs=[row, pl.BlockSpec((N_MEM, 2 * MEM_W), lambda i: (0, 0))], out_specs=row,
        out_shape=jax.ShapeDtypeStruct((t, MEM_W), BF16), compiler_params=_cp(1),
    )(zq, kv)


def _mem_bwd(zq, kv, do, name, tq=512):
    t = zq.shape[0]
    tq = min(tq, t)
    scale = MEM_HD ** -0.5

    def body(q_ref, kv_ref, do_ref, dq_ref, dkv_ref):
        @pl.when(pl.program_id(0) == 0)
        def _():
            dkv_ref[...] = jnp.zeros_like(dkv_ref)

        dqs = []
        for h in range(MEM_HEADS):
            hs = slice(h * MEM_HD, (h + 1) * MEM_HD)
            vs = slice(MEM_W + h * MEM_HD, MEM_W + (h + 1) * MEM_HD)
            qh, doh = q_ref[:, hs], do_ref[:, hs]
            kh, vh = kv_ref[:, hs], kv_ref[:, vs]
            s = lax.dot_general(qh, kh, (((1,), (1,)), ((), ())), preferred_element_type=F32) * scale
            e = jnp.exp(s - jnp.max(s, axis=-1, keepdims=True))
            p = e / jnp.sum(e, axis=-1, keepdims=True)
            dp = lax.dot_general(doh, vh, (((1,), (1,)), ((), ())), preferred_element_type=F32)
            ds = p * (dp - jnp.sum(p * dp, axis=-1, keepdims=True))
            dsb = ds.astype(BF16)
            dqs.append(jnp.dot(dsb, kh, preferred_element_type=F32) * scale)
            dkv_ref[:, hs] += lax.dot_general(dsb, qh, (((0,), (0,)), ((), ())), preferred_element_type=F32) * scale
            dkv_ref[:, vs] += lax.dot_general(p.astype(BF16), doh, (((0,), (0,)), ((), ())), preferred_element_type=F32)
        dq_ref[...] = jnp.concatenate(dqs, axis=1).astype(BF16)

    row = pl.BlockSpec((tq, MEM_W), lambda i: (i, 0))
    full = pl.BlockSpec((N_MEM, 2 * MEM_W), lambda i: (0, 0))
    return pl.pallas_call(
        body, name=name, grid=(t // tq,), in_specs=[row, full, row], out_specs=[row, full],
        out_shape=[jax.ShapeDtypeStruct((t, MEM_W), BF16), jax.ShapeDtypeStruct((N_MEM, 2 * MEM_W), F32)],
        compiler_params=_cp(1),
    )(zq, kv, do)


def _gate_fwd(zg, bg, ys, name, tr=256):
    t = zg.shape[0]
    tr = min(tr, t)

    def body(z_ref, b_ref, y0, y1, y2, o_ref):
        g = _sigmoid(z_ref[...].astype(F32) + b_ref[...])
        o_ref[...] = (g[:, :D] * y0[...].astype(F32) + g[:, D:2 * D] * y1[...].astype(F32)
                      + g[:, 2 * D:] * y2[...].astype(F32)).astype(BF16)

    row = pl.BlockSpec((tr, D), lambda i: (i, 0))
    return pl.pallas_call(
        body, name=name, grid=(t // tr,),
        in_specs=[pl.BlockSpec((tr, 3 * D), lambda i: (i, 0)), pl.BlockSpec((1, 3 * D), lambda i: (0, 0)), row, row, row],
        out_specs=row, out_shape=jax.ShapeDtypeStruct((t, D), BF16), compiler_params=_cp(1),
    )(zg, _row(bg), *ys)


def _gate_bwd(zg, bg, ys, dmerged, name, tr=256):
    t = zg.shape[0]
    tr = min(tr, t)

    def body(z_ref, b_ref, y0, y1, y2, dm_ref, d0, d1, d2, dz_ref, db_ref):
        @pl.when(pl.program_id(0) == 0)
        def _():
            db_ref[...] = jnp.zeros_like(db_ref)

        g = _sigmoid(z_ref[...].astype(F32) + b_ref[...])
        dm = dm_ref[...].astype(F32)
        for i, (y_ref, d_ref) in enumerate(((y0, d0), (y1, d1), (y2, d2))):
            gi = g[:, i * D:(i + 1) * D]
            d_ref[...] = (dm * gi).astype(BF16)
            dz = dm * y_ref[...].astype(F32) * gi * (1.0 - gi)
            dz_ref[:, i * D:(i + 1) * D] = dz.astype(BF16)
            db_ref[0:1, i * D:(i + 1) * D] += jnp.sum(dz, axis=0, keepdims=True)

    row = pl.BlockSpec((tr, D), lambda i: (i, 0))
    wide = pl.BlockSpec((tr, 3 * D), lambda i: (i, 0))
    d0, d1, d2, dz, db = pl.pallas_call(
        body, name=name, grid=(t // tr,),
        in_specs=[wide, pl.BlockSpec((1, 3 * D), lambda i: (0, 0)), row, row, row, row],
        out_specs=[row, row, row, wide, pl.BlockSpec((8, 3 * D), lambda i: (0, 0))],
        out_shape=[jax.ShapeDtypeStruct((t, D), BF16)] * 3 + [jax.ShapeDtypeStruct((t, 3 * D), BF16),
                                                              jax.ShapeDtypeStruct((8, 3 * D), F32)],
        compiler_params=_cp(1),
    )(zg, _row(bg), *ys, dmerged)
    return (d0, d1, d2), dz, db[0]


def _swiglu_fwd(gu, name, tr=256):
    t = gu.shape[0]
    tr = min(tr, t)

    def body(gu_ref, o_ref):
        g = gu_ref[:, :FFN_H].astype(F32)
        o_ref[...] = (g * _sigmoid(g) * gu_ref[:, FFN_H:].astype(F32)).astype(BF16)

    return pl.pallas_call(
        body, name=name, grid=(t // tr,), in_specs=[pl.BlockSpec((tr, 2 * FFN_H), lambda i: (i, 0))],
        out_specs=pl.BlockSpec((tr, FFN_H), lambda i: (i, 0)),
        out_shape=jax.ShapeDtypeStruct((t, FFN_H), BF16), compiler_params=_cp(1),
    )(gu)


def _swiglu_bwd(gu, dact, name, tr=256):
    t = gu.shape[0]
    tr = min(tr, t)

    def body(gu_ref, da_ref, o_ref):
        g = gu_ref[:, :FFN_H].astype(F32)
        u = gu_ref[:, FFN_H:].astype(F32)
        da = da_ref[...].astype(F32)
        sg = _sigmoid(g)
        o_ref[:, :FFN_H] = (da * u * (sg * (1.0 + g * (1.0 - sg)))).astype(BF16)
        o_ref[:, FFN_H:] = (da * g * sg).astype(BF16)

    wide = pl.BlockSpec((tr, 2 * FFN_H), lambda i: (i, 0))
    return pl.pallas_call(
        body, name=name, grid=(t // tr,), in_specs=[wide, pl.BlockSpec((tr, FFN_H), lambda i: (i, 0))],
        out_specs=wide, out_shape=jax.ShapeDtypeStruct((t, 2 * FFN_H), BF16), compiler_params=_cp(1),
    )(gu, dact)


def _adamw(w, g, m, v, name, tr=SLAB_ALIGN):
    r, d = w.shape
    tr = min(tr, r)
    assert r % tr == 0

    def body(w_ref, g_ref, m_ref, v_ref, d_ref, nm_ref, nv_ref):
        gg = g_ref[...]
        m2 = ADAM_B1 * m_ref[...] + (1.0 - ADAM_B1) * gg
        v2 = ADAM_B2 * v_ref[...] + (1.0 - ADAM_B2) * (gg * gg)
        m_hat = m2 / (1.0 - ADAM_B1 ** ADAM_STEP)
        v_hat = v2 / (1.0 - ADAM_B2 ** ADAM_STEP)
        d_ref[...] = -ADAM_LR * (m_hat / (jnp.sqrt(v_hat) + ADAM_EPS) + ADAM_WD * w_ref[...])
        nm_ref[...] = m2
        nv_ref[...] = v2

    row = pl.BlockSpec((tr, d), lambda i: (i, 0))
    return pl.pallas_call(
        body, name=name, grid=(r // tr,), in_specs=[row] * 4, out_specs=[row] * 3,
        out_shape=[jax.ShapeDtypeStruct((r, d), F32)] * 3, compiler_params=_cp(1),
    )(w, g, m, v)


def _sum_slots(parts, out_dtype, name, tr=SLAB_ALIGN):
    n, r, d = parts.shape
    tr = min(tr, r)
    assert r % tr == 0

    def body(p_ref, o_ref):
        acc = p_ref[0].astype(F32)
        for s in range(1, n):
            acc = acc + p_ref[s].astype(F32)
        o_ref[...] = acc.astype(o_ref.dtype)

    return pl.pallas_call(
        body, name=name, grid=(r // tr,), in_specs=[pl.BlockSpec((n, tr, d), lambda i: (0, i, 0))],
        out_specs=pl.BlockSpec((tr, d), lambda i: (i, 0)),
        out_shape=jax.ShapeDtypeStruct((r, d), out_dtype), compiler_params=_cp(1),
    )(parts)


def _sum_pairs(mine, theirs, name, tr=SLAB_ALIGN):
    n, r, d = mine.shape
    tr = min(tr, r)
    assert r % tr == 0

    def body(a_ref, b_ref, o_ref):
        o_ref[...] = (a_ref[...].astype(F32) + b_ref[...].astype(F32)).astype(o_ref.dtype)

    blk = pl.BlockSpec((1, tr, d), lambda s, i: (s, i, 0))
    return pl.pallas_call(
        body, name=name, grid=(n, r // tr), in_specs=[blk, blk], out_specs=blk,
        out_shape=jax.ShapeDtypeStruct((n, r, d), mine.dtype), compiler_params=_cp(2),
    )(mine, theirs)


def _place():
    return lax.axis_index("x"), lax.axis_index("y"), lax.axis_index("c")


def _all_gather(shard, name):
    r, d = shard.shape

    def body(x_ref, out_ref, send_sems, recv_sems, local_sem):
        x, y, c = _place()
        me, sibling = (x, y, c), (x, y, 1 - c)
        chips = [(1 - x, y), (x, 1 - y), (1 - x, 1 - y)]

        def slot(px, py, pc):
            return out_ref.at[4 * px + 2 * py + pc]

        def copy(k, block, to, src=None):
            return pltpu.make_async_remote_copy(
                src_ref=slot(*block) if src is None else src, dst_ref=slot(*block),
                send_sem=send_sems.at[k], recv_sem=recv_sems.at[k], device_id=to, device_id_type=MESH)

        mine = pltpu.make_async_copy(x_ref, slot(*me), local_sem)
        mine.start()
        first = [copy(0, me, sibling, src=x_ref)]
        first += [copy(1 + j, me, (*chip, c), src=x_ref) for j, chip in enumerate(chips)]
        for cp in first:
            cp.start()
        passed = [copy(4 + j, (*chip, c), sibling) for j, chip in enumerate(chips)]
        for j, chip in enumerate(chips):
            copy(1 + j, (*chip, c), me).wait_recv()
            passed[j].start()
        copy(0, sibling, me).wait_recv()
        for j, chip in enumerate(chips):
            copy(4 + j, (*chip, 1 - c), me).wait_recv()
        for cp in first + passed:
            cp.wait_send()
        mine.wait()

    return pl.pallas_call(
        body, name=name, out_shape=jax.ShapeDtypeStruct((N_DEV, r, d), shard.dtype),
        in_specs=[pl.BlockSpec(memory_space=pl.ANY)], out_specs=pl.BlockSpec(memory_space=pl.ANY),
        scratch_shapes=[pltpu.SemaphoreType.DMA((7,)), pltpu.SemaphoreType.DMA((7,)), pltpu.SemaphoreType.DMA],
    )(shard)


def _swap_core(parts, name):
    _, n, r, d = parts.shape

    def body(p_ref, out_ref, send_sem, recv_sem):
        x, y, c = _place()
        cp = pltpu.make_async_remote_copy(src_ref=p_ref.at[1 - c], dst_ref=out_ref, send_sem=send_sem, recv_sem=recv_sem,
                                          device_id=(x, y, 1 - c), device_id_type=MESH)
        cp.start()
        cp.wait()

    return pl.pallas_call(
        body, name=name, out_shape=jax.ShapeDtypeStruct((n, r, d), parts.dtype),
        in_specs=[pl.BlockSpec(memory_space=pl.ANY)], out_specs=pl.BlockSpec(memory_space=pl.ANY),
        scratch_shapes=[pltpu.SemaphoreType.DMA, pltpu.SemaphoreType.DMA],
    )(parts)


def _swap_chips(parts, name):
    n, r, d = parts.shape

    def body(p_ref, out_ref, send_sems, recv_sems, local_sem):
        x, y, c = _place()
        chips = [(1 - x, y), (x, 1 - y), (1 - x, 1 - y)]
        mine = pltpu.make_async_copy(p_ref.at[2 * x + y], out_ref.at[2 * x + y], local_sem)
        mine.start()
        sends = [pltpu.make_async_remote_copy(src_ref=p_ref.at[2 * cx + cy], dst_ref=out_ref.at[2 * x + y],
                                              send_sem=send_sems.at[j], recv_sem=recv_sems.at[j],
                                              device_id=(cx, cy, c), device_id_type=MESH)
                 for j, (cx, cy) in enumerate(chips)]
        for cp in sends:
            cp.start()
        for j, (cx, cy) in enumerate(chips):
            pltpu.make_async_remote_copy(src_ref=p_ref.at[2 * x + y], dst_ref=out_ref.at[2 * cx + cy],
                                         send_sem=send_sems.at[j], recv_sem=recv_sems.at[j],
                                         device_id=(cx, cy, c), device_id_type=MESH).wait_recv()
        for cp in sends:
            cp.wait_send()
        mine.wait()

    return pl.pallas_call(
        body, name=name, out_shape=jax.ShapeDtypeStruct((n, r, d), parts.dtype),
        in_specs=[pl.BlockSpec(memory_space=pl.ANY)], out_specs=pl.BlockSpec(memory_space=pl.ANY),
        scratch_shapes=[pltpu.SemaphoreType.DMA((3,)), pltpu.SemaphoreType.DMA((3,)), pltpu.SemaphoreType.DMA],
    )(parts)


def _reduce_scatter(contrib, tag):
    _, r, d = contrib.shape
    by_core = contrib.reshape(4, 2, r, d).transpose(1, 0, 2, 3)
    c = lax.axis_index("c")
    got = _swap_core(by_core, f"rs_core_{tag}")
    mine = lax.dynamic_index_in_dim(by_core, c, axis=0, keepdims=False)
    chip_sum = _sum_pairs(mine, got, f"rs_pair_{tag}")
    return _sum_slots(_swap_chips(chip_sum, f"rs_chip_{tag}"), F32, f"rs_sum_{tag}")


BIG = (("w_in", (D, 864), 1), ("w_conv_out", (CONV_W, 128), 1), ("w_att_out", (ATT_GW, 128), 1), ("w_mem_kv", (128, D), 0),
       ("w_mem_out", (MEM_W, 128), 1), ("w_out", (128, D), 0), ("w_ffn_in", (D, 704), 1), ("w_ffn_out", (352, D), 0),
       ("conv_dw", (CONV_K, 64), 1))
SMALL = (("rel_bias", (NUM_BUCKETS, 12)), ("norm_mix_pre", (DEPTH, D)), ("b_gate", (DEPTH, 3 * D)),
         ("conv_dw_bias", (DEPTH, CONV_W)), ("conv_ln_g", (DEPTH, CONV_W)), ("conv_ln_b", (DEPTH, CONV_W)),
         ("norm_mem", (DEPTH, D)), ("norm_mix_post", (DEPTH, D)), ("norm_ffn_pre", (DEPTH, D)), ("norm_ffn_post", (DEPTH, D)))
TWIN_WEIGHTS = ("rel_bias", "norm_mix_pre", "w_in", "b_gate", "conv_dw", "conv_dw_bias", "conv_ln_g", "conv_ln_b", "w_conv_out",
                "w_att_out", "norm_mem", "w_mem_kv", "w_mem_out", "w_out", "norm_mix_post", "norm_ffn_pre", "w_ffn_in",
                "w_ffn_out", "norm_ffn_post")


def _rows_of(shape):
    return -(-int(np.prod(shape)) // D)


LAYER_ROWS = -(-sum(_rows_of(s) for _, s, _ in BIG) // SLAB_ALIGN) * SLAB_ALIGN
SMALL_ROWS = 32


def _pack(pieces, rows, dtype):
    flat = []
    for p in pieces:
        p = p.reshape(-1).astype(dtype)
        flat.append(jnp.pad(p, (0, _rows_of(p.shape) * D - p.shape[0])))
    out = jnp.concatenate(flat).reshape(-1, D)
    return jnp.pad(out, ((0, rows - out.shape[0]), (0, 0)))


def _unpack(slab, shapes):
    out, r0 = [], 0
    for s in shapes:
        n = int(np.prod(s))
        out.append(slab[r0:r0 + _rows_of(s)].reshape(-1)[:n].reshape(s))
        r0 += _rows_of(s)
    return out


def _pack_layer(blocks, layer, dtype):
    return _pack([blocks[n][layer] for n, _, _ in BIG], LAYER_ROWS, dtype)


def _full_weights(gathered):
    out, r0 = {}, 0
    for name, shape, axis in BIG:
        n, rows = int(np.prod(shape)), _rows_of(shape)
        blk = gathered[:, r0:r0 + rows].reshape(N_DEV, -1)[:, :n].reshape(N_DEV, *shape)
        if axis == 0:
            out[name] = blk.reshape(N_DEV * shape[0], shape[1])
        else:
            out[name] = blk.transpose(1, 0, 2).reshape(shape[0], N_DEV * shape[1])
        r0 += rows
    return out


def _scatter_layout(grads, dtype):
    pieces = []
    for name, shape, axis in BIG:
        g = grads[name]
        if axis == 0:
            blk = g.reshape(N_DEV, shape[0], shape[1])
        else:
            blk = g.reshape(shape[0], N_DEV, shape[1]).transpose(1, 0, 2)
        blk = blk.reshape(N_DEV, -1).astype(dtype)
        rows = _rows_of(shape)
        pieces.append(jnp.pad(blk, ((0, 0), (0, rows * D - blk.shape[1]))).reshape(N_DEV, rows, D))
    out = jnp.concatenate(pieces, axis=1)
    return jnp.pad(out, ((0, 0), (0, LAYER_ROWS - out.shape[1]), (0, 0)))


def _residue_major(a, dilation):
    if dilation == 1:
        return a
    t, w = a.shape
    return a.reshape(t // dilation, dilation, w).transpose(1, 0, 2).reshape(t, w)


def _token_major(a, dilation):
    if dilation == 1:
        return a
    t, w = a.shape
    return a.reshape(dilation, t // dilation, w).transpose(1, 0, 2).reshape(t, w)


def _split_w_in(w_in):
    edges = np.cumsum((0,) + SEG)
    return [w_in[:, edges[i]:edges[i + 1]] for i in range(4)]


def _layer_fwd(l, x, h, nm, wts, sm, bias_tabs):
    t = x.shape[0]
    w_seg = _split_w_in(wts["w_in"])
    zc = _mm(h, w_seg[0], f"fwd_in_conv")
    za = _mm(h, w_seg[1], f"fwd_in_att", tn=1152)
    zq = _mm(h, w_seg[2], f"fwd_in_memq")
    zg = _mm(h, w_seg[3], f"fwd_in_gate")
    conv_w = jnp.pad(wts["conv_dw"].astype(F32), ((0, 32 - CONV_K), (0, 0)))
    y, cact = _conv_fwd(zc, conv_w, sm["conv_dw_bias"][l], sm["conv_ln_g"][l], sm["conv_ln_b"][l], "conv_fwd")
    qkv, outs, lses = [], [], []
    for g, (_, dil) in enumerate(ATT_PATTERNS):
        q, k, v = (_residue_major(za[:, (i * 3 + g) * ATT_GW:(i * 3 + g + 1) * ATT_GW], dil) for i in range(3))
        o, ls = _att_fwd(q, k, v, bias_tabs[g], t // dil, f"att_fwd_d{dil}")
        qkv.append((q, k, v))
        outs.append(_token_major(o, dil))
        lses.append(_token_major(ls, dil))
    att, lse = _att_combine(outs, lses, "att_combine")
    kv = _mm(nm, wts["w_mem_kv"], "fwd_mem_kv")
    om = _mem_fwd(zq, kv, "mem_fwd")
    ys = (_mm(cact, wts["w_conv_out"], "fwd_conv_out", tm=2048), _mm(att, wts["w_att_out"], "fwd_att_out", tm=2048),
          _mm(om, wts["w_mem_out"], "fwd_mem_out", tm=2048))
    merged = _gate_fwd(zg, sm["b_gate"][l], ys, "gate_fwd")
    ymix = _mm(merged, wts["w_out"], "fwd_out", out_dtype=F32)
    x1, h2 = _norm_res(x, ymix, sm["norm_mix_post"][l], sm["norm_ffn_pre"][l], "norm_mix_post")
    gu = _mm(h2, wts["w_ffn_in"], "fwd_ffn_in", tn=1408)
    act = _swiglu_fwd(gu, "swiglu_fwd")
    f = _mm(act, wts["w_ffn_out"], "fwd_ffn_out", out_dtype=F32, tk=1408)
    g_next = sm["norm_mix_pre"][l + 1] if l + 1 < DEPTH else None
    x2, h_next = _norm_res(x1, f, sm["norm_ffn_post"][l], g_next, "norm_ffn_post" if g_next is not None else "norm_ffn_last")
    saved = dict(x=x, h=h, zc=zc, zq=zq, zg=zg, y=y, cact=cact, qkv=qkv, att=att, lse=lse, kv=kv, om=om, ys=ys,
                 merged=merged, ymix=ymix, x1=x1, h2=h2, gu=gu, act=act, f=f)
    return x2, h_next, saved


def _layer_bwd(l, dx2, s, nm, mem, wts, sm, bias_tabs, bucket_idx):
    t = dx2.shape[0]
    gb, gs = {}, {}
    df, gs["norm_ffn_post"] = _norm_bwd(s["f"], sm["norm_ffn_post"][l], dx2, None, BF16, "bwd_norm_ffn_post")
    dact = _mm(df, wts["w_ffn_out"], "bwd_ffn_out_d", tb=True, tn=1408)
    gb["w_ffn_out"] = _mm(s["act"], df, "bwd_ffn_out_w", ta=True, out_dtype=F32, tm=1408)
    dgu = _swiglu_bwd(s["gu"], dact, "swiglu_bwd")
    dh2 = _mm(dgu, wts["w_ffn_in"], "bwd_ffn_in_d", tb=True, out_dtype=F32, tk=1408)
    gb["w_ffn_in"] = _mm(s["h2"], dgu, "bwd_ffn_in_w", ta=True, out_dtype=F32, tn=1408)
    dx1, gs["norm_ffn_pre"] = _norm_bwd(s["x1"], sm["norm_ffn_pre"][l], dh2, dx2, F32, "bwd_norm_ffn_pre")
    dymix, gs["norm_mix_post"] = _norm_bwd(s["ymix"], sm["norm_mix_post"][l], dx1, None, BF16, "bwd_norm_mix_post")
    dmerged = _mm(dymix, wts["w_out"], "bwd_out_d", tb=True)
    gb["w_out"] = _mm(s["merged"], dymix, "bwd_out_w", ta=True, out_dtype=F32)
    dys, dzg, gs["b_gate"] = _gate_bwd(s["zg"], sm["b_gate"][l], s["ys"], dmerged, "gate_bwd")
    dcact = _mm(dys[0], wts["w_conv_out"], "bwd_conv_out_d", tb=True, tm=2048)
    gb["w_conv_out"] = _mm(s["cact"], dys[0], "bwd_conv_out_w", ta=True, out_dtype=F32)
    conv_w = jnp.pad(wts["conv_dw"].astype(F32), ((0, 32 - CONV_K), (0, 0)))
    dzc, dconv, gs["conv_dw_bias"], gs["conv_ln_g"], gs["conv_ln_b"] = _conv_bwd(
        s["zc"], s["y"], dcact, conv_w, sm["conv_ln_g"][l], sm["conv_ln_b"][l], "conv_bwd")
    gb["conv_dw"] = dconv[:CONV_K]
    datt = _mm(dys[1], wts["w_att_out"], "bwd_att_out_d", tb=True, tm=2048)
    gb["w_att_out"] = _mm(s["att"], dys[1], "bwd_att_out_w", ta=True, out_dtype=F32)
    dqkv, rel = [None] * 9, []
    for g, (_, dil) in enumerate(ATT_PATTERNS):
        q, k, v = s["qkv"][g]
        dq, dk, dv, dbias = _att_bwd(q, k, v, bias_tabs[g], _residue_major(s["att"], dil), _residue_major(datt, dil),
                                     _residue_major(s["lse"], dil), t // dil, f"att_bwd_d{dil}")
        for i, a in enumerate((dq, dk, dv)):
            dqkv[i * 3 + g] = _token_major(a, dil)
        rel.append(_bias_grad(dbias, bucket_idx[g], f"bias_grad_d{dil}")[:HEADS, :NUM_BUCKETS])
    dza = jnp.concatenate(dqkv, axis=1)
    gs["rel_bias"] = jnp.concatenate(rel, axis=0).T
    dom = _mm(dys[2], wts["w_mem_out"], "bwd_mem_out_d", tb=True, tm=2048)
    gb["w_mem_out"] = _mm(s["om"], dys[2], "bwd_mem_out_w", ta=True, out_dtype=F32)
    dzq, dkv = _mem_bwd(s["zq"], s["kv"], dom, "mem_bwd")
    dkv = dkv.astype(BF16)
    gb["w_mem_kv"] = _mm(nm, dkv, "bwd_mem_kv_w", ta=True, out_dtype=F32)
    dnm = _mm(dkv, wts["w_mem_kv"], "bwd_mem_kv_d", tb=True, out_dtype=F32)
    _, gs["norm_mem"] = _norm_bwd(mem, sm["norm_mem"][l], dnm, None, BF16, "bwd_norm_mem")
    w_seg = _split_w_in(wts["w_in"])
    dh, dws = None, []
    for i, (dz, nm_) in enumerate(((dzc, "conv"), (dza, "att"), (dzq, "memq"), (dzg, "gate"))):
        tk = 1152 if nm_ == "att" else 1024
        dh = _mm(dz, w_seg[i], f"bwd_in_{nm_}_d", tb=True, out_dtype=F32, acc=dh, tk=tk)
        dws.append(_mm(s["h"], dz, f"bwd_in_{nm_}_w", ta=True, out_dtype=F32, tn=1152 if nm_ == "att" else 1024))
    gb["w_in"] = jnp.concatenate(dws, axis=1)
    dx, gs["norm_mix_pre"] = _norm_bwd(s["x"], sm["norm_mix_pre"][l], dh, dx1, F32, "bwd_norm_mix_pre")
    return dx, gb, gs


def _local_step(x, mem, target, wts, sm):
    t = x.shape[0]
    bias_tabs, bucket_idx = [], []
    for g, (_, dil) in enumerate(ATT_PATTERNS):
        idx = _bucket_index(_att_tiles(t // dil)[1], dil)
        tab = sm["rel_bias"][:, g * HEADS:(g + 1) * HEADS]
        bias = jnp.where(idx[None] >= 0, jnp.transpose(tab[np.maximum(idx, 0)], (2, 0, 1)), NEG_INF)
        bias_tabs.append(bias.astype(F32))
        bucket_idx.append(jnp.asarray(idx))
    h = _norm_plain(x, sm["norm_mix_pre"][0], "norm_first")
    saved, nms = [], []
    for l in range(DEPTH):
        nm = _norm_plain(mem, sm["norm_mem"][l], "norm_mem")
        x, h, s = _layer_fwd(l, x, h, nm, wts[l], sm, bias_tabs)
        saved.append(s)
        nms.append(nm)
    loss, dx = _loss_head(x, target, "loss_head")
    gbig, gsmall = [None] * DEPTH, {}
    for l in reversed(range(DEPTH)):
        dx, gb, gs = _layer_bwd(l, dx, saved[l], nms[l], mem, wts[l], sm, bias_tabs, bucket_idx)
        gbig[l] = gb
        for n, v in gs.items():
            gsmall.setdefault(n, {})[l] = v
    small = {}
    for n, _ in SMALL:
        small[n] = gsmall[n][0] + gsmall[n][1] if n == "rel_bias" else jnp.stack([gsmall[n][0], gsmall[n][1]])
    return loss, dx, gbig, small


def kernel(x, mem, rel_bias, norm_mix_pre, w_in, b_gate, conv_dw, conv_dw_bias, conv_ln_g, conv_ln_b, w_conv_out, w_att_out, norm_mem, w_mem_kv, w_mem_out, w_out, norm_mix_post, norm_ffn_pre, w_ffn_in, w_ffn_out, norm_ffn_post, loss_target, m_rel_bias, m_norm_mix_pre, m_w_in, m_b_gate, m_conv_dw, m_conv_dw_bias, m_conv_ln_g, m_conv_ln_b, m_w_conv_out, m_w_att_out, m_norm_mem, m_w_mem_kv, m_w_mem_out, m_w_out, m_norm_mix_post, m_norm_ffn_pre, m_w_ffn_in, m_w_ffn_out, m_norm_ffn_post, v_rel_bias, v_norm_mix_pre, v_w_in, v_b_gate, v_conv_dw, v_conv_dw_bias, v_conv_ln_g, v_conv_ln_b, v_w_conv_out, v_w_att_out, v_norm_mem, v_w_mem_kv, v_w_mem_out, v_w_out, v_norm_mix_post, v_norm_ffn_pre, v_w_ffn_in, v_w_ffn_out, v_norm_ffn_post):
    args = dict(locals())
    w = {n: args[n] for n in TWIN_WEIGHTS}
    m = {n: args["m_" + n] for n in TWIN_WEIGHTS}
    v = {n: args["v_" + n] for n in TWIN_WEIGHTS}
    sm = {n: w[n] for n, _ in SMALL}

    wts = [_full_weights(_all_gather(_pack_layer(w, l, BF16), f"gather_weights")) for l in range(DEPTH)]
    loss, dx, gbig, gsmall = _local_step(x[0], mem[0], loss_target[0], wts, sm)
    loss = lax.psum(loss, ("x", "y", "c"))

    big_shapes = [s for _, s, _ in BIG]
    g_layers = [_unpack(_reduce_scatter(_scatter_layout(gbig[l], BF16), "layer"), big_shapes) for l in range(DEPTH)]
    small_shapes = [s for _, s in SMALL]
    g_small = _unpack(_sum_slots(_all_gather(_pack([gsmall[n] for n, _ in SMALL], SMALL_ROWS, F32), "gather_small"),
                                 F32, "sum_small"), small_shapes)
    grads = {n: jnp.stack([g_layers[0][i], g_layers[1][i]]) for i, (n, _, _) in enumerate(BIG)}
    grads.update({n: g_small[i] for i, (n, _) in enumerate(SMALL)})

    order = [n for n, _, _ in BIG] + [n for n, _ in SMALL]
    shapes = [w[n].shape for n in order]
    rows = -(-sum(_rows_of(s) for s in shapes) // SLAB_ALIGN) * SLAB_ALIGN
    slabs = [_pack([src[n] for n in order], rows, F32) for src in (w, grads, m, v)]
    delta, new_m, new_v = (dict(zip(order, _unpack(o, shapes))) for o in _adamw(*slabs, "adamw"))

    return (loss, dx[None], *[grads[n] for n in TWIN_WEIGHTS], *[delta[n] for n in TWIN_WEIGHTS],
            *[new_m[n] for n in TWIN_WEIGHTS], *[new_v[n] for n in TWIN_WEIGHTS])
```

```python
import functools

import numpy as np
import jax
import jax.numpy as jnp
from jax import lax
from jax.experimental import pallas as pl
from jax.experimental.pallas import tpu as pltpu

F32 = jnp.float32
BF16 = jnp.bfloat16

N_DEV = 8
D = 1024
DEPTH = 2
CONV_W = 512
CONV_K = 31
CONV_PAD = 16
ATT_PATTERNS = ((128, 1), (512, 4), (2048, 16))
ATT_RADIUS = 64
HEADS = 4
HEAD_DIM = 64
ATT_GW = HEADS * HEAD_DIM
ATT_W = 3 * ATT_GW
MEM_HEADS = 4
MEM_HD = 128
MEM_W = 512
N_MEM = 256
FFN_H = 2816
NUM_BUCKETS = 32
MAX_DISTANCE = 1024
RMS_EPS = 1e-6
LN_EPS = 1e-5
NEG_INF = -1e30
SEG = (2 * CONV_W, 3 * ATT_W, MEM_W, 3 * D)
ADAM_LR, ADAM_B1, ADAM_B2, ADAM_EPS, ADAM_WD, ADAM_STEP = 0.001, 0.9, 0.999, 1e-08, 0.01, 10

SLAB_ALIGN = 128
VMEM_LIMIT_V7X = 48 * 1024 * 1024
MESH = pl.DeviceIdType.MESH


def _cp(n_axes):
    return pltpu.CompilerParams(dimension_semantics=("arbitrary",) * n_axes, vmem_limit_bytes=VMEM_LIMIT_V7X)


def _row(v):
    return v.reshape(1, -1)


def _mm(a, b, name, *, ta=False, tb=False, out_dtype=BF16, acc=None, tm=1024, tn=1024, tk=1024):
    kdim, m = a.shape if ta else a.shape[::-1]
    n, kb = b.shape if tb else b.shape[::-1]
    assert kb == kdim and a.dtype == BF16 and b.dtype == BF16
    tm, tn, tk = min(tm, m), min(tn, n), min(tk, kdim)
    assert m % tm == 0 and n % tn == 0 and kdim % tk == 0, (name, m, n, kdim, tm, tn, tk)
    nk = kdim // tk
    a_spec = pl.BlockSpec((tk, tm), lambda i, j, k: (k, i)) if ta else pl.BlockSpec((tm, tk), lambda i, j, k: (i, k))
    b_spec = pl.BlockSpec((tn, tk), lambda i, j, k: (j, k)) if tb else pl.BlockSpec((tk, tn), lambda i, j, k: (k, j))
    o_spec = pl.BlockSpec((tm, tn), lambda i, j, k: (i, j))
    dims = (((0 if ta else 1,), (1 if tb else 0,)), ((), ()))
    has_acc = acc is not None
    use_scratch = nk > 1 and out_dtype != F32
    assert not has_acc or out_dtype == F32

    def body(*refs):
        a_ref, b_ref = refs[0], refs[1]
        o_ref = refs[2 + has_acc]
        p = lax.dot_general(a_ref[...], b_ref[...], dims, preferred_element_type=F32)
        if nk == 1:
            if has_acc:
                p = p + refs[2][...]
            o_ref[...] = p.astype(o_ref.dtype)
            return
        k = pl.program_id(2)
        t_ref = refs[3 + has_acc] if use_scratch else o_ref

        @pl.when(k == 0)
        def _():
            t_ref[...] = (p + refs[2][...]) if has_acc else p

        @pl.when(k > 0)
        def _():
            t_ref[...] += p

        if use_scratch:
            @pl.when(k == nk - 1)
            def _():
                o_ref[...] = t_ref[...].astype(o_ref.dtype)

    return pl.pallas_call(
        body, name=name, grid=(m // tm, n // tn, nk),
        in_specs=[a_spec, b_spec] + ([o_spec] if has_acc else []), out_specs=o_spec,
        out_shape=jax.ShapeDtypeStruct((m, n), out_dtype),
        scratch_shapes=[pltpu.VMEM((tm, tn), F32)] if use_scratch else [],
        compiler_params=_cp(3),
    )(*((a, b, acc) if has_acc else (a, b)))


def _rms_val(v, g):
    return v * lax.rsqrt(jnp.mean(v * v, axis=-1, keepdims=True) + RMS_EPS) * g


def _norm_plain(v, g, name, tr=512):
    t, d = v.shape
    tr = min(tr, t)

    def body(v_ref, g_ref, o_ref):
        o_ref[...] = _rms_val(v_ref[...], g_ref[...]).astype(BF16)

    return pl.pallas_call(
        body, name=name, grid=(t // tr,),
        in_specs=[pl.BlockSpec((tr, d), lambda i: (i, 0)), pl.BlockSpec((1, d), lambda i: (0, 0))],
        out_specs=pl.BlockSpec((tr, d), lambda i: (i, 0)),
        out_shape=jax.ShapeDtypeStruct((t, d), BF16), compiler_params=_cp(1),
    )(v, _row(g))


def _norm_res(xres, v, g, g_next, name, tr=512):
    t, d = v.shape
    tr = min(tr, t)
    two = g_next is not None

    def body(*refs):
        x_ref, v_ref, g_ref = refs[:3]
        xn = x_ref[...] + _rms_val(v_ref[...], g_ref[...])
        refs[3 + two][...] = xn
        if two:
            refs[5][...] = _rms_val(xn, refs[3][...]).astype(BF16)

    row = pl.BlockSpec((tr, d), lambda i: (i, 0))
    vec = pl.BlockSpec((1, d), lambda i: (0, 0))
    outs = pl.pallas_call(
        body, name=name, grid=(t // tr,),
        in_specs=[row, row, vec] + ([vec] if two else []),
        out_specs=[row, row] if two else [row],
        out_shape=[jax.ShapeDtypeStruct((t, d), F32)] + ([jax.ShapeDtypeStruct((t, d), BF16)] if two else []),
        compiler_params=_cp(1),
    )(*((xres, v, _row(g)) + ((_row(g_next),) if two else ())))
    return (outs[0], outs[1]) if two else (outs[0], None)


def _norm_bwd(v, g, dout, dres, out_dtype, name, tr=512):
    t, d = v.shape
    tr = min(tr, t)
    nt = t // tr
    has_res = dres is not None

    def body(*refs):
        v_ref, g_ref, do_ref = refs[:3]
        dv_ref, dg_ref = refs[3 + has_res], refs[4 + has_res]
        i = pl.program_id(0)
        vv = v_ref[...]
        dy = do_ref[...].astype(F32)
        r = lax.rsqrt(jnp.mean(vv * vv, axis=-1, keepdims=True) + RMS_EPS)
        vhat = vv * r
        dvh = dy * g_ref[...]
        dv = r * (dvh - vhat * jnp.mean(dvh * vhat, axis=-1, keepdims=True))
        if has_res:
            dv = dv + refs[3][...]
        dv_ref[...] = dv.astype(dv_ref.dtype)
        part = jnp.sum(dy * vhat, axis=0, keepdims=True)

        @pl.when(i == 0)
        def _():
            dg_ref[...] = jnp.zeros_like(dg_ref)

        dg_ref[0:1, :] += part

    row = pl.BlockSpec((tr, d), lambda i: (i, 0))
    dv, dg = pl.pallas_call(
        body, name=name, grid=(nt,),
        in_specs=[row, pl.BlockSpec((1, d), lambda i: (0, 0)), row] + ([row] if has_res else []),
        out_specs=[row, pl.BlockSpec((8, d), lambda i: (0, 0))],
        out_shape=[jax.ShapeDtypeStruct((t, d), out_dtype), jax.ShapeDtypeStruct((8, d), F32)],
        compiler_params=_cp(1),
    )(*((v, _row(g), dout) + ((dres,) if has_res else ())))
    return dv, dg[0]


def _loss_head(y, target, name, tr=512):
    t, d = y.shape
    tr = min(tr, t)
    nt = t // tr

    def body(y_ref, t_ref, dy_ref, l_ref):
        i = pl.program_id(0)
        err = y_ref[...] - t_ref[...]
        dy_ref[...] = err * (1.0 / d)

        @pl.when(i == 0)
        def _():
            l_ref[...] = jnp.zeros_like(l_ref)

        l_ref[...] += jnp.sum(err * err) * (0.5 / d)

    row = pl.BlockSpec((tr, d), lambda i: (i, 0))
    dy, l = pl.pallas_call(
        body, name=name, grid=(nt,), in_specs=[row, row],
        out_specs=[row, pl.BlockSpec((8, 128), lambda i: (0, 0))],
        out_shape=[jax.ShapeDtypeStruct((t, d), F32), jax.ShapeDtypeStruct((8, 128), F32)],
        compiler_params=_cp(1),
    )(y, target)
    return l[0, 0], dy


def _sigmoid(v):
    return 1.0 / (1.0 + jnp.exp(-v))


def _halo_specs(tq, width, nt, halo):
    per = tq // halo
    last = nt * per - 1
    main = pl.BlockSpec((tq, width), lambda i: (i, 0))
    prev = pl.BlockSpec((halo, width), lambda i: (jnp.maximum(i * per - 1, 0), 0))
    nxt = pl.BlockSpec((halo, width), lambda i: (jnp.minimum((i + 1) * per, last), 0))
    return prev, main, nxt


def _glu_window(zp, zm, zn, i, nt):
    def glu(z):
        z = z.astype(F32)
        return z[:, :CONV_W] * _sigmoid(z[:, CONV_W:])

    up = jnp.where(i > 0, glu(zp), 0.0)
    un = jnp.where(i < nt - 1, glu(zn), 0.0)
    return jnp.concatenate([up, glu(zm), un], axis=0)


def _shifted(win, shift, rows):
    if shift == 0:
        return win[:rows]
    return pltpu.roll(win, win.shape[0] - shift, 0)[:rows]


def _conv_fwd(zc, w, b, ln_g, ln_b, name, tq=256):
    t = zc.shape[0]
    tq = min(tq, t)
    nt = t // tq

    def body(zp_ref, zm_ref, zn_ref, w_ref, b_ref, g_ref, bb_ref, y_ref, c_ref):
        i = pl.program_id(0)
        win = _glu_window(zp_ref[...], zm_ref[...], zn_ref[...], i, nt)
        wv = w_ref[...]
        y = jnp.zeros((tq, CONV_W), F32) + b_ref[...]
        for k in range(CONV_K):
            y = y + _shifted(win, k + 1, tq) * wv[k:k + 1, :]
        y_ref[...] = y
        mu = jnp.mean(y, axis=-1, keepdims=True)
        yc = y - mu
        ln = yc * lax.rsqrt(jnp.mean(yc * yc, axis=-1, keepdims=True) + LN_EPS) * g_ref[...] + bb_ref[...]
        c_ref[...] = (ln * _sigmoid(ln)).astype(BF16)

    vec = pl.BlockSpec((1, CONV_W), lambda i: (0, 0))
    out = pl.BlockSpec((tq, CONV_W), lambda i: (i, 0))
    return pl.pallas_call(
        body, name=name, grid=(nt,),
        in_specs=[*_halo_specs(tq, 2 * CONV_W, nt, CONV_PAD), pl.BlockSpec((32, CONV_W), lambda i: (0, 0)), vec, vec, vec],
        out_specs=[out, out],
        out_shape=[jax.ShapeDtypeStruct((t, CONV_W), F32), jax.ShapeDtypeStruct((t, CONV_W), BF16)],
        compiler_params=_cp(1),
    )(zc, zc, zc, w, _row(b), _row(ln_g), _row(ln_b))


def _conv_bwd(zc, y, dc, w, ln_g, ln_b, name, tq=256):
    t = zc.shape[0]
    tq = min(tq, t)
    nt = t // tq
    rows = tq + 2 * CONV_PAD

    def body(zp_ref, zm_ref, zn_ref, yp_ref, ym_ref, yn_ref, dp_ref, dm_ref, dn_ref, w_ref, g_ref, bb_ref,
             dz_ref, dw_ref, db_ref, dg_ref, dbb_ref):
        i = pl.program_id(0)
        uwin = _glu_window(zp_ref[...], zm_ref[...], zn_ref[...], i, nt)
        ywin = jnp.concatenate([yp_ref[...], ym_ref[...], yn_ref[...]], axis=0)
        dcw = jnp.concatenate([dp_ref[...], dm_ref[...], dn_ref[...]], axis=0).astype(F32)
        mu = jnp.mean(ywin, axis=-1, keepdims=True)
        yc = ywin - mu
        rstd = lax.rsqrt(jnp.mean(yc * yc, axis=-1, keepdims=True) + LN_EPS)
        yhat = yc * rstd
        ln = yhat * g_ref[...] + bb_ref[...]
        sg = _sigmoid(ln)
        dl = dcw * (sg * (1.0 + ln * (1.0 - sg)))
        ridx = lax.broadcasted_iota(jnp.int32, (rows, 1), 0)
        inside = ((ridx >= CONV_PAD) | (i > 0)) & ((ridx < tq + CONV_PAD) | (i < nt - 1))
        dl = jnp.where(inside, dl, 0.0)
        dyh = dl * g_ref[...]
        dy = rstd * (dyh - jnp.mean(dyh, axis=-1, keepdims=True) - yhat * jnp.mean(dyh * yhat, axis=-1, keepdims=True))
        dy = jnp.where(inside, dy, 0.0)
        main = slice(CONV_PAD, CONV_PAD + tq)
        dlm, yhm, dym = dl[main], yhat[main], dy[main]

        @pl.when(i == 0)
        def _():
            dw_ref[...] = jnp.zeros_like(dw_ref)
            db_ref[...] = jnp.zeros_like(db_ref)
            dg_ref[...] = jnp.zeros_like(dg_ref)
            dbb_ref[...] = jnp.zeros_like(dbb_ref)

        dg_ref[0:1, :] += jnp.sum(dlm * yhm, axis=0, keepdims=True)
        dbb_ref[0:1, :] += jnp.sum(dlm, axis=0, keepdims=True)
        db_ref[0:1, :] += jnp.sum(dym, axis=0, keepdims=True)
        wv = w_ref[...]
        du = jnp.zeros((tq, CONV_W), F32)
        for k in range(CONV_K):
            du = du + _shifted(dy, 2 * CONV_PAD - 1 - k, tq) * wv[k:k + 1, :]
            dw_ref[k:k + 1, :] += jnp.sum(dym * _shifted(uwin, k + 1, tq), axis=0, keepdims=True)
        zm = zm_ref[...].astype(F32)
        a, gt = zm[:, :CONV_W], zm[:, CONV_W:]
        sgt = _sigmoid(gt)
        dz_ref[:, :CONV_W] = (du * sgt).astype(BF16)
        dz_ref[:, CONV_W:] = (du * a * sgt * (1.0 - sgt)).astype(BF16)

    vec = pl.BlockSpec((1, CONV_W), lambda i: (0, 0))
    acc = pl.BlockSpec((8, CONV_W), lambda i: (0, 0))
    dz, dw, db, dg, dbb = pl.pallas_call(
        body, name=name, grid=(nt,),
        in_specs=[*_halo_specs(tq, 2 * CONV_W, nt, CONV_PAD), *_halo_specs(tq, CONV_W, nt, CONV_PAD),
                  *_halo_specs(tq, CONV_W, nt, CONV_PAD), pl.BlockSpec((32, CONV_W), lambda i: (0, 0)), vec, vec],
        out_specs=[pl.BlockSpec((tq, 2 * CONV_W), lambda i: (i, 0)), pl.BlockSpec((32, CONV_W), lambda i: (0, 0)), acc, acc, acc],
        out_shape=[jax.ShapeDtypeStruct((t, 2 * CONV_W), BF16), jax.ShapeDtypeStruct((32, CONV_W), F32),
                   jax.ShapeDtypeStruct((8, CONV_W), F32), jax.ShapeDtypeStruct((8, CONV_W), F32),
                   jax.ShapeDtypeStruct((8, CONV_W), F32)],
        compiler_params=_cp(1),
    )(zc, zc, zc, y, y, y, dc, dc, dc, w, _row(ln_g), _row(ln_b))
    return dz, dw, db[0], dg[0], dbb[0]


def _att_tiles(seq_len):
    tq = min(256, seq_len)
    return tq, min(128, tq)


def _t5_bucket_np(rel):
    nb = NUM_BUCKETS // 2
    max_exact = nb // 2
    ret = np.where(rel > 0, nb, 0)
    n = np.abs(rel)
    nf = np.maximum(n, 1).astype(np.float32)
    large = max_exact + (np.log(nf / np.float32(max_exact)) / np.float32(np.log(MAX_DISTANCE / max_exact))
                         * np.float32(nb - max_exact)).astype(np.int32)
    large = np.minimum(large, nb - 1)
    return ret + np.where(n < max_exact, n, large)


def _bucket_index(sb, dilation):
    off = np.arange(sb + 2 * ATT_RADIUS)[None, :] - ATT_RADIUS - np.arange(sb)[:, None]
    idx = _t5_bucket_np(off * dilation).astype(np.int32)
    return np.where(np.abs(off) <= ATT_RADIUS, idx, -1).astype(np.int32)


def _bias_table(tab, idx, name):
    sb, w = idx.shape

    def body(tab_ref, idx_ref, o_ref):
        ix = idx_ref[...]
        for h in range(HEADS):
            acc = jnp.full((sb, w), NEG_INF, F32)
            for b in range(NUM_BUCKETS):
                acc = jnp.where(ix == b, tab_ref[b, h], acc)
            o_ref[h] = acc

    return pl.pallas_call(
        body, name=name, out_shape=jax.ShapeDtypeStruct((HEADS, sb, w), F32),
        in_specs=[pl.BlockSpec(memory_space=pltpu.SMEM), pl.BlockSpec(memory_space=pltpu.VMEM)],
        out_specs=pl.BlockSpec(memory_space=pltpu.VMEM),
    )(tab, idx)


def _qkv_specs(tq, per, last_blk, cols, tile):
    def main(col):
        return pl.BlockSpec((tq, ATT_GW), lambda *g: (tile(*g), col))

    def prev(col):
        return pl.BlockSpec((ATT_RADIUS, ATT_GW), lambda *g: (jnp.maximum(tile(*g) * per - 1, 0), col))

    def nxt(col):
        return pl.BlockSpec((ATT_RADIUS, ATT_GW), lambda *g: (jnp.minimum((tile(*g) + 1) * per, last_blk), col))

    cq, ck, cv = cols
    return [main(cq), prev(ck), main(ck), nxt(ck), prev(cv), main(cv), nxt(cv)]


def _att_fwd(q, k, v, bias, seq_len, name, cols=(0, 0, 0)):
    t = q.shape[0]
    tq, sb = _att_tiles(seq_len)
    nbl = seq_len // tq
    nt = t // tq
    per = tq // ATT_RADIUS
    w = sb + 2 * ATT_RADIUS
    last_blk = t // ATT_RADIUS - 1

    def body(q_ref, kp_ref, kc_ref, kn_ref, vp_ref, vc_ref, vn_ref, b_ref, o_ref, l_ref):
        n = pl.program_id(0) % nbl
        kw = jnp.concatenate([kp_ref[...], kc_ref[...], kn_ref[...]], axis=0)
        vw = jnp.concatenate([vp_ref[...], vc_ref[...], vn_ref[...]], axis=0)
        kpos = n * tq - ATT_RADIUS + lax.broadcasted_iota(jnp.int32, (1, tq + 2 * ATT_RADIUS), 1)
        valid = (kpos >= 0) & (kpos < seq_len)
        for j in range(tq // sb):
            qj = q_ref[j * sb:(j + 1) * sb, :]
            kj, vj, okj = kw[j * sb:j * sb + w], vw[j * sb:j * sb + w], valid[:, j * sb:j * sb + w]
            outs, lses = [], []
            for h in range(HEADS):
                hs = slice(h * HEAD_DIM, (h + 1) * HEAD_DIM)
                s = lax.dot_general(qj[:, hs], kj[:, hs], (((1,), (1,)), ((), ())), preferred_element_type=F32)
                s = jnp.where(okj, s * (HEAD_DIM ** -0.5) + b_ref[h], NEG_INF)
                m = jnp.max(s, axis=-1, keepdims=True)
                e = jnp.exp(s - m)
                den = jnp.sum(e, axis=-1, keepdims=True)
                o = jnp.dot(e.astype(BF16), vj[:, hs], preferred_element_type=F32) / den
                outs.append(o)
                lses.append(jnp.broadcast_to(m + jnp.log(den), (sb, HEAD_DIM)))
            o_ref[j * sb:(j + 1) * sb, :] = jnp.concatenate(outs, axis=1).astype(BF16)
            l_ref[j * sb:(j + 1) * sb, :] = jnp.concatenate(lses, axis=1)

    main = pl.BlockSpec((tq, ATT_GW), lambda i: (i, 0))
    return pl.pallas_call(
        body, name=name, grid=(nt,),
        in_specs=[*_qkv_specs(tq, per, last_blk, cols, lambda i: i), pl.BlockSpec((HEADS, sb, w), lambda i: (0, 0, 0))],
        out_specs=[main, main],
        out_shape=[jax.ShapeDtypeStruct((t, ATT_GW), BF16), jax.ShapeDtypeStruct((t, ATT_GW), F32)],
        compiler_params=_cp(1),
    )(q, k, k, k, v, v, v, bias)


PERM_ROWS = 512
LANES = 128


def _spec_rm(d, tr, width, col=0):
    return pl.BlockSpec((d, tr // d, width), lambda i: (0, i, col))


def _rm_view(a, d):
    return a.reshape(d, a.shape[0] // d, a.shape[1])


def _gather_residues(scr, val, d, out_ref, col0):
    tr, w = val.shape
    for c in range(w // LANES):
        scr[...] = val[:, c * LANES:(c + 1) * LANES].astype(F32)
        for r in range(d):
            out_ref[r, :, col0 + c * LANES:col0 + (c + 1) * LANES] = scr[pl.ds(r, tr // d, stride=d), :].astype(out_ref.dtype)


def _scatter_residues(scr, ref, d):
    w = ref.shape[2]
    cols = []
    for c in range(w // LANES):
        for r in range(d):
            scr[pl.ds(r, scr.shape[0] // d, stride=d), :] = ref[r, :, c * LANES:(c + 1) * LANES].astype(F32)
        cols.append(scr[...])
    return jnp.concatenate(cols, axis=1)


def _to_residue_major(srcs, d, name, tr=PERM_ROWS):
    t = srcs[0][0].shape[0]
    n = len(srcs)

    def body(*refs):
        scr = refs[2 * n]
        for k in range(n):
            _gather_residues(scr, refs[k][...], d, refs[n + k], 0)

    outs = pl.pallas_call(
        body, name=name, grid=(t // tr,),
        in_specs=[pl.BlockSpec((tr, ATT_GW), functools.partial(lambda i, col: (i, col), col=col)) for _, col in srcs],
        out_specs=[_spec_rm(d, tr, ATT_GW)] * n,
        out_shape=[jax.ShapeDtypeStruct((d, t // d, ATT_GW), a.dtype) for a, _ in srcs],
        scratch_shapes=[pltpu.VMEM((tr, LANES), F32)], compiler_params=_cp(1),
    )(*[a for a, _ in srcs])
    return [o.reshape(t, ATT_GW) for o in outs]


def _att_combine(outs, lses, name, tr=PERM_ROWS):
    t = outs[0].shape[0]
    dils = [d for _, d in ATT_PATTERNS]

    def body(o0, o1, o2, l0, l1, l2, a_ref, lse_ref, scr):
        os_, ls_ = [], []
        for o_ref, l_ref, d in zip((o0, o1, o2), (l0, l1, l2), dils):
            if d == 1:
                os_.append(o_ref[...].astype(F32))
                ls_.append(l_ref[...])
            else:
                os_.append(_scatter_residues(scr, o_ref, d))
                ls_.append(_scatter_residues(scr, l_ref, d))
        la, lb, lc = ls_
        m = jnp.maximum(jnp.maximum(la, lb), lc)
        ea, eb, ec = jnp.exp(la - m), jnp.exp(lb - m), jnp.exp(lc - m)
        den = ea + eb + ec
        a_ref[...] = ((ea * os_[0] + eb * os_[1] + ec * os_[2]) / den).astype(BF16)
        lse_ref[...] = m + jnp.log(den)

    row = pl.BlockSpec((tr, ATT_GW), lambda i: (i, 0))
    specs = [row if d == 1 else _spec_rm(d, tr, ATT_GW) for d in dils]
    views = lambda arrs: [a if d == 1 else _rm_view(a, d) for a, d in zip(arrs, dils)]
    return pl.pallas_call(
        body, name=name, grid=(t // tr,), in_specs=specs * 2, out_specs=[row, row],
        out_shape=[jax.ShapeDtypeStruct((t, ATT_GW), BF16), jax.ShapeDtypeStruct((t, ATT_GW), F32)],
        scratch_shapes=[pltpu.VMEM((tr, LANES), F32)], compiler_params=_cp(1),
    )(*views(outs), *views(lses))


def _assemble_dza(dqkv, name, tr=PERM_ROWS):
    t = dqkv[0][0].shape[0]
    dils = [d for _, d in ATT_PATTERNS]

    def body(*refs):
        o_ref, scr = refs[9], refs[10]
        for g, d in enumerate(dils):
            for c in range(3):
                ref = refs[g * 3 + c]
                val = ref[...] if d == 1 else _scatter_residues(scr, ref, d).astype(BF16)
                o_ref[:, (c * 3 + g) * ATT_GW:(c * 3 + g + 1) * ATT_GW] = val

    row = pl.BlockSpec((tr, ATT_GW), lambda i: (i, 0))
    specs = [row if d == 1 else _spec_rm(d, tr, ATT_GW) for d in dils for _ in range(3)]
    args = [a if d == 1 else _rm_view(a, d) for trio, d in zip(dqkv, dils) for a in trio]
    return pl.pallas_call(
        body, name=name, grid=(t // tr,), in_specs=specs, out_specs=pl.BlockSpec((tr, 3 * ATT_W), lambda i: (i, 0)),
        out_shape=jax.ShapeDtypeStruct((t, 3 * ATT_W), BF16),
        scratch_shapes=[pltpu.VMEM((tr, LANES), F32)], compiler_params=_cp(1),
    )(*args)


def _att_bwd(q, k, v, bias, o, do, lse, seq_len, name, cols=(0, 0, 0)):
    t = q.shape[0]
    tq, sb = _att_tiles(seq_len)
    nbl = seq_len // tq
    n_res = t // seq_len
    per = tq // ATT_RADIUS
    w = sb + 2 * ATT_RADIUS
    last_blk = t // ATT_RADIUS - 1
    acc_rows = 2 * tq + ATT_RADIUS
    scale = HEAD_DIM ** -0.5

    def body(q_ref, kp_ref, kc_ref, kn_ref, vp_ref, vc_ref, vn_ref, b_ref, o_ref, do_ref, l_ref,
             dq_ref, dk_ref, dv_ref, db_ref, ak_ref, av_ref):
        r, n = pl.program_id(0), pl.program_id(1)

        @pl.when((r == 0) & (n == 0))
        def _():
            db_ref[...] = jnp.zeros_like(db_ref)

        @pl.when(n == 0)
        def _():
            ak_ref[...] = jnp.zeros_like(ak_ref)
            av_ref[...] = jnp.zeros_like(av_ref)

        @pl.when(n < nbl)
        def _():
            kw = jnp.concatenate([kp_ref[...], kc_ref[...], kn_ref[...]], axis=0)
            vw = jnp.concatenate([vp_ref[...], vc_ref[...], vn_ref[...]], axis=0)
            kpos = n * tq - ATT_RADIUS + lax.broadcasted_iota(jnp.int32, (1, tq + 2 * ATT_RADIUS), 1)
            valid = (kpos >= 0) & (kpos < seq_len)
            for j in range(tq // sb):
                rows = slice(j * sb, (j + 1) * sb)
                qj, doj = q_ref[rows, :], do_ref[rows, :]
                dd = doj.astype(F32) * o_ref[rows, :].astype(F32)
                lj = l_ref[rows, :]
                kj, vj, okj = kw[j * sb:j * sb + w], vw[j * sb:j * sb + w], valid[:, j * sb:j * sb + w]
                dqs, dks, dvs = [], [], []
                for h in range(HEADS):
                    hs = slice(h * HEAD_DIM, (h + 1) * HEAD_DIM)
                    s = lax.dot_general(qj[:, hs], kj[:, hs], (((1,), (1,)), ((), ())), preferred_element_type=F32)
                    s = jnp.where(okj, s * scale + b_ref[h], NEG_INF)
                    p = jnp.exp(s - lj[:, h * HEAD_DIM:h * HEAD_DIM + 1])
                    dp = lax.dot_general(doj[:, hs], vj[:, hs], (((1,), (1,)), ((), ())), preferred_element_type=F32)
                    ds = p * (dp - jnp.sum(dd[:, hs], axis=-1, keepdims=True))
                    db_ref[h] += ds
                    dsb = ds.astype(BF16)
                    dqs.append(jnp.dot(dsb, kj[:, hs], preferred_element_type=F32) * scale)
                    dks.append(lax.dot_general(dsb, qj[:, hs], (((0,), (0,)), ((), ())), preferred_element_type=F32) * scale)
                    dvs.append(lax.dot_general(p.astype(BF16), doj[:, hs], (((0,), (0,)), ((), ())), preferred_element_type=F32))
                dq_ref[rows, :] = jnp.concatenate(dqs, axis=1).astype(BF16)
                a0 = j * sb + tq - ATT_RADIUS
                ak_ref[a0:a0 + w, :] += jnp.concatenate(dks, axis=1)
                av_ref[a0:a0 + w, :] += jnp.concatenate(dvs, axis=1)

        dk_ref[...] = ak_ref[0:tq, :].astype(BF16)
        dv_ref[...] = av_ref[0:tq, :].astype(BF16)
        keep = tq + ATT_RADIUS
        nk, nv = ak_ref[tq:acc_rows, :], av_ref[tq:acc_rows, :]
        ak_ref[0:keep, :] = nk
        av_ref[0:keep, :] = nv
        ak_ref[keep:acc_rows, :] = jnp.zeros((acc_rows - keep, ATT_GW), F32)
        av_ref[keep:acc_rows, :] = jnp.zeros((acc_rows - keep, ATT_GW), F32)

    def tile(r, n):
        return r * nbl + jnp.minimum(n, nbl - 1)

    main = pl.BlockSpec((tq, ATT_GW), lambda r, n: (tile(r, n), 0))
    lag = pl.BlockSpec((tq, ATT_GW), lambda r, n: (r * nbl + jnp.maximum(n - 1, 0), 0))
    bspec = pl.BlockSpec((HEADS, sb, w), lambda r, n: (0, 0, 0))
    return pl.pallas_call(
        body, name=name, grid=(n_res, nbl + 1),
        in_specs=[*_qkv_specs(tq, per, last_blk, cols, tile), bspec, main, main, main],
        out_specs=[main, lag, lag, bspec],
        out_shape=[jax.ShapeDtypeStruct((t, ATT_GW), BF16)] * 3 + [jax.ShapeDtypeStruct((HEADS, sb, w), F32)],
        scratch_shapes=[pltpu.VMEM((acc_rows, ATT_GW), F32), pltpu.VMEM((acc_rows, ATT_GW), F32)],
        compiler_params=_cp(2),
    )(q, k, k, k, v, v, v, bias, o, do, lse)


def _bias_grad(db, idx, name):
    _, sb, w = db.shape

    def body(db_ref, idx_ref, o_ref):
        ix = idx_ref[...]
        lane = lax.broadcasted_iota(jnp.int32, (1, 128), 1)
        rows = []
        for h in range(HEADS):
            d = db_ref[h]
            acc = jnp.zeros((1, 128), F32)
            for b in range(NUM_BUCKETS):
                acc = acc + jnp.where(lane == b, jnp.sum(jnp.where(ix == b, d, 0.0)), 0.0)
            rows.append(acc)
        o_ref[...] = jnp.concatenate(rows + [jnp.zeros((8 - HEADS, 128), F32)], axis=0)

    return pl.pallas_call(
        body, name=name, out_shape=jax.ShapeDtypeStruct((8, 128), F32),
        in_specs=[pl.BlockSpec(memory_space=pltpu.VMEM), pl.BlockSpec(memory_space=pltpu.VMEM)],
        out_specs=pl.BlockSpec(memory_space=pltpu.VMEM),
    )(db, idx)


def _mem_fwd(zq, kv, name, tq=512):
    t = zq.shape[0]
    tq = min(tq, t)
    scale = MEM_HD ** -0.5

    def body(q_ref, kv_ref, o_ref):
        outs = []
        for h in range(MEM_HEADS):
            hs = slice(h * MEM_HD, (h + 1) * MEM_HD)
            kh = kv_ref[:, h * MEM_HD:(h + 1) * MEM_HD]
            vh = kv_ref[:, MEM_W + h * MEM_HD:MEM_W + (h + 1) * MEM_HD]
            s = lax.dot_general(q_ref[:, hs], kh, (((1,), (1,)), ((), ())), preferred_element_type=F32) * scale
            e = jnp.exp(s - jnp.max(s, axis=-1, keepdims=True))
            den = jnp.sum(e, axis=-1, keepdims=True)
            outs.append(jnp.dot(e.astype(BF16), vh, preferred_element_type=F32) / den)
        o_ref[...] = jnp.concatenate(outs, axis=1).astype(BF16)

    row = pl.BlockSpec((tq, MEM_W), lambda i: (i, 0))
    return pl.pallas_call(
        body, name=name, grid=(t // tq,),
        in_specs=[row, pl.BlockSpec((N_MEM, 2 * MEM_W), lambda i: (0, 0))], out_specs=row,
        out_shape=jax.ShapeDtypeStruct((t, MEM_W), BF16), compiler_params=_cp(1),
    )(zq, kv)


def _mem_bwd(zq, kv, do, name, tq=512):
    t = zq.shape[0]
    tq = min(tq, t)
    scale = MEM_HD ** -0.5

    def body(q_ref, kv_ref, do_ref, dq_ref, dkv_ref):
        @pl.when(pl.program_id(0) == 0)
        def _():
            dkv_ref[...] = jnp.zeros_like(dkv_ref)

        dqs = []
        for h in range(MEM_HEADS):
            hs = slice(h * MEM_HD, (h + 1) * MEM_HD)
            vs = slice(MEM_W + h * MEM_HD, MEM_W + (h + 1) * MEM_HD)
            qh, doh = q_ref[:, hs], do_ref[:, hs]
            kh, vh = kv_ref[:, hs], kv_ref[:, vs]
            s = lax.dot_general(qh, kh, (((1,), (1,)), ((), ())), preferred_element_type=F32) * scale
            e = jnp.exp(s - jnp.max(s, axis=-1, keepdims=True))
            p = e / jnp.sum(e, axis=-1, keepdims=True)
            dp = lax.dot_general(doh, vh, (((1,), (1,)), ((), ())), preferred_element_type=F32)
            ds = p * (dp - jnp.sum(p * dp, axis=-1, keepdims=True))
            dsb = ds.astype(BF16)
            dqs.append(jnp.dot(dsb, kh, preferred_element_type=F32) * scale)
            dkv_ref[:, hs] += lax.dot_general(dsb, qh, (((0,), (0,)), ((), ())), preferred_element_type=F32) * scale
            dkv_ref[:, vs] += lax.dot_general(p.astype(BF16), doh, (((0,), (0,)), ((), ())), preferred_element_type=F32)
        dq_ref[...] = jnp.concatenate(dqs, axis=1).astype(BF16)

    row = pl.BlockSpec((tq, MEM_W), lambda i: (i, 0))
    full = pl.BlockSpec((N_MEM, 2 * MEM_W), lambda i: (0, 0))
    return pl.pallas_call(
        body, name=name, grid=(t // tq,), in_specs=[row, full, row], out_specs=[row, full],
        out_shape=[jax.ShapeDtypeStruct((t, MEM_W), BF16), jax.ShapeDtypeStruct((N_MEM, 2 * MEM_W), F32)],
        compiler_params=_cp(1),
    )(zq, kv, do)


def _gate_fwd(zg, bg, ys, name, tr=256):
    t = zg.shape[0]
    tr = min(tr, t)

    def body(z_ref, b_ref, y0, y1, y2, o_ref):
        g = _sigmoid(z_ref[...].astype(F32) + b_ref[...])
        o_ref[...] = (g[:, :D] * y0[...].astype(F32) + g[:, D:2 * D] * y1[...].astype(F32)
                      + g[:, 2 * D:] * y2[...].astype(F32)).astype(BF16)

    row = pl.BlockSpec((tr, D), lambda i: (i, 0))
    return pl.pallas_call(
        body, name=name, grid=(t // tr,),
        in_specs=[pl.BlockSpec((tr, 3 * D), lambda i: (i, 0)), pl.BlockSpec((1, 3 * D), lambda i: (0, 0)), row, row, row],
        out_specs=row, out_shape=jax.ShapeDtypeStruct((t, D), BF16), compiler_params=_cp(1),
    )(zg, _row(bg), *ys)


def _gate_bwd(zg, bg, ys, dmerged, name, tr=256):
    t = zg.shape[0]
    tr = min(tr, t)

    def body(z_ref, b_ref, y0, y1, y2, dm_ref, d0, d1, d2, dz_ref, db_ref):
        @pl.when(pl.program_id(0) == 0)
        def _():
            db_ref[...] = jnp.zeros_like(db_ref)

        g = _sigmoid(z_ref[...].astype(F32) + b_ref[...])
        dm = dm_ref[...].astype(F32)
        for i, (y_ref, d_ref) in enumerate(((y0, d0), (y1, d1), (y2, d2))):
            gi = g[:, i * D:(i + 1) * D]
            d_ref[...] = (dm * gi).astype(BF16)
            dz = dm * y_ref[...].astype(F32) * gi * (1.0 - gi)
            dz_ref[:, i * D:(i + 1) * D] = dz.astype(BF16)
            db_ref[0:1, i * D:(i + 1) * D] += jnp.sum(dz, axis=0, keepdims=True)

    row = pl.BlockSpec((tr, D), lambda i: (i, 0))
    wide = pl.BlockSpec((tr, 3 * D), lambda i: (i, 0))
    d0, d1, d2, dz, db = pl.pallas_call(
        body, name=name, grid=(t // tr,),
        in_specs=[wide, pl.BlockSpec((1, 3 * D), lambda i: (0, 0)), row, row, row, row],
        out_specs=[row, row, row, wide, pl.BlockSpec((8, 3 * D), lambda i: (0, 0))],
        out_shape=[jax.ShapeDtypeStruct((t, D), BF16)] * 3 + [jax.ShapeDtypeStruct((t, 3 * D), BF16),
                                                              jax.ShapeDtypeStruct((8, 3 * D), F32)],
        compiler_params=_cp(1),
    )(zg, _row(bg), *ys, dmerged)
    return (d0, d1, d2), dz, db[0]


def _swiglu_fwd(gu, name, tr=256):
    t = gu.shape[0]
    tr = min(tr, t)

    def body(gu_ref, o_ref):
        g = gu_ref[:, :FFN_H].astype(F32)
        o_ref[...] = (g * _sigmoid(g) * gu_ref[:, FFN_H:].astype(F32)).astype(BF16)

    return pl.pallas_call(
        body, name=name, grid=(t // tr,), in_specs=[pl.BlockSpec((tr, 2 * FFN_H), lambda i: (i, 0))],
        out_specs=pl.BlockSpec((tr, FFN_H), lambda i: (i, 0)),
        out_shape=jax.ShapeDtypeStruct((t, FFN_H), BF16), compiler_params=_cp(1),
    )(gu)


def _swiglu_bwd(gu, dact, name, tr=256):
    t = gu.shape[0]
    tr = min(tr, t)

    def body(gu_ref, da_ref, o_ref):
        g = gu_ref[:, :FFN_H].astype(F32)
        u = gu_ref[:, FFN_H:].astype(F32)
        da = da_ref[...].astype(F32)
        sg = _sigmoid(g)
        o_ref[:, :FFN_H] = (da * u * (sg * (1.0 + g * (1.0 - sg)))).astype(BF16)
        o_ref[:, FFN_H:] = (da * g * sg).astype(BF16)

    wide = pl.BlockSpec((tr, 2 * FFN_H), lambda i: (i, 0))
    return pl.pallas_call(
        body, name=name, grid=(t // tr,), in_specs=[wide, pl.BlockSpec((tr, FFN_H), lambda i: (i, 0))],
        out_specs=wide, out_shape=jax.ShapeDtypeStruct((t, 2 * FFN_H), BF16), compiler_params=_cp(1),
    )(gu, dact)


def _adamw(w, g, m, v, name):
    shape = w.shape
    w, g, m, v = (a.reshape(-1, shape[-1]) for a in (w, g, m, v))
    r, d = w.shape
    tr = next((c for c in (512, 352, 256, 128, 64, 32, 16, 8) if r % c == 0), r)

    def body(w_ref, g_ref, m_ref, v_ref, d_ref, nm_ref, nv_ref):
        gg = g_ref[...]
        m2 = ADAM_B1 * m_ref[...] + (1.0 - ADAM_B1) * gg
        v2 = ADAM_B2 * v_ref[...] + (1.0 - ADAM_B2) * (gg * gg)
        m_hat = m2 / (1.0 - ADAM_B1 ** ADAM_STEP)
        v_hat = v2 / (1.0 - ADAM_B2 ** ADAM_STEP)
        d_ref[...] = -ADAM_LR * (m_hat / (jnp.sqrt(v_hat) + ADAM_EPS) + ADAM_WD * w_ref[...])
        nm_ref[...] = m2
        nv_ref[...] = v2

    row = pl.BlockSpec((tr, d), lambda i: (i, 0))
    outs = pl.pallas_call(
        body, name=name, grid=(r // tr,), in_specs=[row] * 4, out_specs=[row] * 3,
        out_shape=[jax.ShapeDtypeStruct((r, d), F32)] * 3, compiler_params=_cp(1),
    )(w, g, m, v)
    return [o.reshape(shape) for o in outs]


def _sum_slots(parts, out_dtype, name, tr=SLAB_ALIGN):
    n, r, d = parts.shape
    tr = min(tr, r)
    assert r % tr == 0

    def body(p_ref, o_ref):
        acc = p_ref[0].astype(F32)
        for s in range(1, n):
            acc = acc + p_ref[s].astype(F32)
        o_ref[...] = acc.astype(o_ref.dtype)

    return pl.pallas_call(
        body, name=name, grid=(r // tr,), in_specs=[pl.BlockSpec((n, tr, d), lambda i: (0, i, 0))],
        out_specs=pl.BlockSpec((tr, d), lambda i: (i, 0)),
        out_shape=jax.ShapeDtypeStruct((r, d), out_dtype), compiler_params=_cp(1),
    )(parts)


def _sum_pairs(mine, theirs, name, tr=SLAB_ALIGN):
    n, r, d = mine.shape
    tr = min(tr, r)
    assert r % tr == 0

    def body(a_ref, b_ref, o_ref):
        o_ref[...] = (a_ref[...].astype(F32) + b_ref[...].astype(F32)).astype(o_ref.dtype)

    blk = pl.BlockSpec((1, tr, d), lambda s, i: (s, i, 0))
    return pl.pallas_call(
        body, name=name, grid=(n, r // tr), in_specs=[blk, blk], out_specs=blk,
        out_shape=jax.ShapeDtypeStruct((n, r, d), mine.dtype), compiler_params=_cp(2),
    )(mine, theirs)


def _place():
    return lax.axis_index("x"), lax.axis_index("y"), lax.axis_index("c")


def _all_gather(shard, name):
    r, d = shard.shape

    def body(x_ref, out_ref, send_sems, recv_sems, local_sem):
        x, y, c = _place()
        me, sibling = (x, y, c), (x, y, 1 - c)
        chips = [(1 - x, y), (x, 1 - y), (1 - x, 1 - y)]

        def slot(px, py, pc):
            return out_ref.at[4 * px + 2 * py + pc]

        def copy(k, block, to, src=None):
            return pltpu.make_async_remote_copy(
                src_ref=slot(*block) if src is None else src, dst_ref=slot(*block),
                send_sem=send_sems.at[k], recv_sem=recv_sems.at[k], device_id=to, device_id_type=MESH)

        mine = pltpu.make_async_copy(x_ref, slot(*me), local_sem)
        mine.start()
        first = [copy(0, me, sibling, src=x_ref)]
        first += [copy(1 + j, me, (*chip, c), src=x_ref) for j, chip in enumerate(chips)]
        for cp in first:
            cp.start()
        passed = [copy(4 + j, (*chip, c), sibling) for j, chip in enumerate(chips)]
        for j, chip in enumerate(chips):
            copy(1 + j, (*chip, c), me).wait_recv()
            passed[j].start()
        copy(0, sibling, me).wait_recv()
        for j, chip in enumerate(chips):
            copy(4 + j, (*chip, 1 - c), me).wait_recv()
        for cp in first + passed:
            cp.wait_send()
        mine.wait()

    return pl.pallas_call(
        body, name=name, out_shape=jax.ShapeDtypeStruct((N_DEV, r, d), shard.dtype),
        in_specs=[pl.BlockSpec(memory_space=pl.ANY)], out_specs=pl.BlockSpec(memory_space=pl.ANY),
        scratch_shapes=[pltpu.SemaphoreType.DMA((7,)), pltpu.SemaphoreType.DMA((7,)), pltpu.SemaphoreType.DMA],
    )(shard)


def _swap_core(parts, name):
    _, n, r, d = parts.shape

    def body(p_ref, out_ref, send_sem, recv_sem):
        x, y, c = _place()
        cp = pltpu.make_async_remote_copy(src_ref=p_ref.at[1 - c], dst_ref=out_ref, send_sem=send_sem, recv_sem=recv_sem,
                                          device_id=(x, y, 1 - c), device_id_type=MESH)
        cp.start()
        cp.wait()

    return pl.pallas_call(
        body, name=name, out_shape=jax.ShapeDtypeStruct((n, r, d), parts.dtype),
        in_specs=[pl.BlockSpec(memory_space=pl.ANY)], out_specs=pl.BlockSpec(memory_space=pl.ANY),
        scratch_shapes=[pltpu.SemaphoreType.DMA, pltpu.SemaphoreType.DMA],
    )(parts)


def _swap_chips(parts, name):
    n, r, d = parts.shape

    def body(p_ref, out_ref, send_sems, recv_sems, local_sem):
        x, y, c = _place()
        chips = [(1 - x, y), (x, 1 - y), (1 - x, 1 - y)]
        mine = pltpu.make_async_copy(p_ref.at[2 * x + y], out_ref.at[2 * x + y], local_sem)
        mine.start()
        sends = [pltpu.make_async_remote_copy(src_ref=p_ref.at[2 * cx + cy], dst_ref=out_ref.at[2 * x + y],
                                              send_sem=send_sems.at[j], recv_sem=recv_sems.at[j],
                                              device_id=(cx, cy, c), device_id_type=MESH)
                 for j, (cx, cy) in enumerate(chips)]
        for cp in sends:
            cp.start()
        for j, (cx, cy) in enumerate(chips):
            pltpu.make_async_remote_copy(src_ref=p_ref.at[2 * x + y], dst_ref=out_ref.at[2 * cx + cy],
                                         send_sem=send_sems.at[j], recv_sem=recv_sems.at[j],
                                         device_id=(cx, cy, c), device_id_type=MESH).wait_recv()
        for cp in sends:
            cp.wait_send()
        mine.wait()

    return pl.pallas_call(
        body, name=name, out_shape=jax.ShapeDtypeStruct((n, r, d), parts.dtype),
        in_specs=[pl.BlockSpec(memory_space=pl.ANY)], out_specs=pl.BlockSpec(memory_space=pl.ANY),
        scratch_shapes=[pltpu.SemaphoreType.DMA((3,)), pltpu.SemaphoreType.DMA((3,)), pltpu.SemaphoreType.DMA],
    )(parts)


def _reduce_scatter(contrib, tag):
    _, r, d = contrib.shape
    by_core = contrib.reshape(4, 2, r, d).transpose(1, 0, 2, 3)
    c = lax.axis_index("c")
    got = _swap_core(by_core, f"rs_core_{tag}")
    mine = lax.dynamic_index_in_dim(by_core, c, axis=0, keepdims=False)
    chip_sum = _sum_pairs(mine, got, f"rs_pair_{tag}")
    return _sum_slots(_swap_chips(chip_sum, f"rs_chip_{tag}"), F32, f"rs_sum_{tag}")


BIG = (("w_in", (D, 864), 1), ("w_conv_out", (CONV_W, 128), 1), ("w_att_out", (ATT_GW, 128), 1), ("w_mem_kv", (128, D), 0),
       ("w_mem_out", (MEM_W, 128), 1), ("w_out", (128, D), 0), ("w_ffn_in", (D, 704), 1), ("w_ffn_out", (352, D), 0),
       ("conv_dw", (CONV_K, 64), 1))
SMALL = (("rel_bias", (NUM_BUCKETS, 12)), ("norm_mix_pre", (DEPTH, D)), ("b_gate", (DEPTH, 3 * D)),
         ("conv_dw_bias", (DEPTH, CONV_W)), ("conv_ln_g", (DEPTH, CONV_W)), ("conv_ln_b", (DEPTH, CONV_W)),
         ("norm_mem", (DEPTH, D)), ("norm_mix_post", (DEPTH, D)), ("norm_ffn_pre", (DEPTH, D)), ("norm_ffn_post", (DEPTH, D)))
TWIN_WEIGHTS = ("rel_bias", "norm_mix_pre", "w_in", "b_gate", "conv_dw", "conv_dw_bias", "conv_ln_g", "conv_ln_b", "w_conv_out",
                "w_att_out", "norm_mem", "w_mem_kv", "w_mem_out", "w_out", "norm_mix_post", "norm_ffn_pre", "w_ffn_in",
                "w_ffn_out", "norm_ffn_post")


def _rows_of(shape):
    return -(-int(np.prod(shape)) // D)


def _row_form(shape, axis):
    return tuple(shape) if axis == 0 else (shape[1], shape[0])


LAYER_ROWS = -(-sum(_rows_of(s) for _, s, _ in BIG) // SLAB_ALIGN) * SLAB_ALIGN
SMALL_ROWS = 32


def _as_slab_rows(p, lead=()):
    if p.shape[-1] == D:
        return p
    n = int(np.prod(p.shape[len(lead):]))
    p = p.reshape(*lead, n)
    p = jnp.pad(p, [(0, 0)] * len(lead) + [(0, _rows_of((n,)) * D - n)])
    return p.reshape(*lead, -1, D)


def _from_slab_rows(rows, shape, lead=()):
    if shape[-1] == D:
        return rows
    n = int(np.prod(shape))
    return rows.reshape(*lead, -1)[..., :n].reshape(*lead, *shape)


def _pack(pieces, rows, dtype):
    parts = [_as_slab_rows(p.astype(dtype)) for p in pieces]
    used = sum(p.shape[0] for p in parts)
    return jnp.concatenate(parts + [jnp.zeros((rows - used, D), dtype)], axis=0)


def _unpack(slab, shapes):
    out, r0 = [], 0
    for s in shapes:
        out.append(_from_slab_rows(slab[r0:r0 + _rows_of(s)], s))
        r0 += _rows_of(s)
    return out


def _pack_layer(blocks, layer):
    return _pack([blocks[n][layer] if axis == 0 else blocks[n][layer].T for n, _, axis in BIG], LAYER_ROWS, BF16)


def _full_weights(gathered):
    out, r0 = {}, 0
    for name, shape, axis in BIG:
        r, k = _row_form(shape, axis)
        rows = _rows_of(shape)
        out[name] = _from_slab_rows(gathered[:, r0:r0 + rows], (r, k), lead=(N_DEV,)).reshape(N_DEV * r, k)
        r0 += rows
    return out


def _scatter_layout(grads):
    parts = []
    for name, shape, axis in BIG:
        r, k = _row_form(shape, axis)
        parts.append(_as_slab_rows(grads[name].astype(BF16).reshape(N_DEV, r, k), lead=(N_DEV,)))
    used = sum(p.shape[1] for p in parts)
    return jnp.concatenate(parts + [jnp.zeros((N_DEV, LAYER_ROWS - used, D), BF16)], axis=1)


def _unpack_blocks(slab):
    out = _unpack(slab, [_row_form(s, axis) for _, s, axis in BIG])
    return [b if axis == 0 else b.T for b, (_, _, axis) in zip(out, BIG)]


def _split_w_in(w_in):
    edges = np.cumsum((0,) + SEG)
    return [w_in[edges[i]:edges[i + 1]] for i in range(4)]


def _conv_taps(conv_dw):
    return jnp.pad(conv_dw.astype(F32).T, ((0, 32 - CONV_K), (0, 0)))


def _layer_fwd(l, x, h, nm, wts, sm, bias_tabs):
    t = x.shape[0]
    w_seg = _split_w_in(wts["w_in"])
    zc = _mm(h, w_seg[0], "fwd_in_conv", tb=True)
    za = _mm(h, w_seg[1], "fwd_in_att", tb=True, tn=1152)
    zq = _mm(h, w_seg[2], "fwd_in_memq", tb=True)
    zg = _mm(h, w_seg[3], "fwd_in_gate", tb=True)
    y, cact = _conv_fwd(zc, _conv_taps(wts["conv_dw"]), sm["conv_dw_bias"][l], sm["conv_ln_g"][l], sm["conv_ln_b"][l],
                        "conv_fwd")
    qkv, outs, lses = [], [], []
    for g, (_, dil) in enumerate(ATT_PATTERNS):
        cols = tuple(i * 3 + g for i in range(3))
        if dil == 1:
            trio, at = (za, za, za), cols
        else:
            trio, at = _to_residue_major([(za, c) for c in cols], dil, f"qkv_residue_d{dil}"), (0, 0, 0)
        o, ls = _att_fwd(*trio, bias_tabs[g], t // dil, f"att_fwd_d{dil}", cols=at)
        qkv.append((trio, at))
        outs.append(o)
        lses.append(ls)
    att, lse = _att_combine(outs, lses, "att_combine")
    kv = _mm(nm, wts["w_mem_kv"], "fwd_mem_kv")
    om = _mem_fwd(zq, kv, "mem_fwd")
    ys = (_mm(cact, wts["w_conv_out"], "fwd_conv_out", tb=True, tm=2048),
          _mm(att, wts["w_att_out"], "fwd_att_out", tb=True, tm=2048),
          _mm(om, wts["w_mem_out"], "fwd_mem_out", tb=True, tm=2048))
    merged = _gate_fwd(zg, sm["b_gate"][l], ys, "gate_fwd")
    ymix = _mm(merged, wts["w_out"], "fwd_out", out_dtype=F32)
    x1, h2 = _norm_res(x, ymix, sm["norm_mix_post"][l], sm["norm_ffn_pre"][l], "norm_mix_post")
    gu = _mm(h2, wts["w_ffn_in"], "fwd_ffn_in", tb=True, tn=1408)
    act = _swiglu_fwd(gu, "swiglu_fwd")
    f = _mm(act, wts["w_ffn_out"], "fwd_ffn_out", out_dtype=F32, tk=1408)
    g_next = sm["norm_mix_pre"][l + 1] if l + 1 < DEPTH else None
    x2, h_next = _norm_res(x1, f, sm["norm_ffn_post"][l], g_next, "norm_ffn_post" if g_next is not None else "norm_ffn_last")
    saved = dict(x=x, h=h, zc=zc, zq=zq, zg=zg, y=y, cact=cact, qkv=qkv, att=att, lse=lse, kv=kv, om=om, ys=ys,
                 merged=merged, ymix=ymix, x1=x1, h2=h2, gu=gu, act=act, f=f)
    return x2, h_next, saved


def _layer_bwd(l, dx2, s, nm, mem, wts, sm, bias_tabs, bucket_idx):
    t = dx2.shape[0]
    gb, gs = {}, {}
    df, gs["norm_ffn_post"] = _norm_bwd(s["f"], sm["norm_ffn_post"][l], dx2, None, BF16, "bwd_norm_ffn_post")
    dact = _mm(df, wts["w_ffn_out"], "bwd_ffn_out_d", tb=True, tn=1408)
    gb["w_ffn_out"] = _mm(s["act"], df, "bwd_ffn_out_w", ta=True, out_dtype=F32, tm=1408)
    dgu = _swiglu_bwd(s["gu"], dact, "swiglu_bwd")
    dh2 = _mm(dgu, wts["w_ffn_in"], "bwd_ffn_in_d", out_dtype=F32, tk=1408)
    gb["w_ffn_in"] = _mm(dgu, s["h2"], "bwd_ffn_in_w", ta=True, out_dtype=F32, tm=1408)
    dx1, gs["norm_ffn_pre"] = _norm_bwd(s["x1"], sm["norm_ffn_pre"][l], dh2, dx2, F32, "bwd_norm_ffn_pre")
    dymix, gs["norm_mix_post"] = _norm_bwd(s["ymix"], sm["norm_mix_post"][l], dx1, None, BF16, "bwd_norm_mix_post")
    dmerged = _mm(dymix, wts["w_out"], "bwd_out_d", tb=True)
    gb["w_out"] = _mm(s["merged"], dymix, "bwd_out_w", ta=True, out_dtype=F32)
    dys, dzg, gs["b_gate"] = _gate_bwd(s["zg"], sm["b_gate"][l], s["ys"], dmerged, "gate_bwd")
    dcact = _mm(dys[0], wts["w_conv_out"], "bwd_conv_out_d", tm=2048)
    gb["w_conv_out"] = _mm(dys[0], s["cact"], "bwd_conv_out_w", ta=True, out_dtype=F32)
    dzc, dconv, gs["conv_dw_bias"], gs["conv_ln_g"], gs["conv_ln_b"] = _conv_bwd(
        s["zc"], s["y"], dcact, _conv_taps(wts["conv_dw"]), sm["conv_ln_g"][l], sm["conv_ln_b"][l], "conv_bwd")
    gb["conv_dw"] = dconv[:CONV_K].T
    datt = _mm(dys[1], wts["w_att_out"], "bwd_att_out_d", tm=2048)
    gb["w_att_out"] = _mm(dys[1], s["att"], "bwd_att_out_w", ta=True, out_dtype=F32)
    dqkv, rel = [], []
    for g, (_, dil) in enumerate(ATT_PATTERNS):
        trio, at = s["qkv"][g]
        if dil == 1:
            o_r, do_r, lse_r = s["att"], datt, s["lse"]
        else:
            o_r, do_r, lse_r = _to_residue_major([(s["att"], 0), (datt, 0), (s["lse"], 0)], dil, f"att_bwd_residue_d{dil}")
        dq, dk, dv, dbias = _att_bwd(*trio, bias_tabs[g], o_r, do_r, lse_r, t // dil, f"att_bwd_d{dil}", cols=at)
        dqkv.append((dq, dk, dv))
        rel.append(_bias_grad(dbias, bucket_idx[g], f"bias_grad_d{dil}")[:HEADS, :NUM_BUCKETS])
    dza = _assemble_dza(dqkv, "att_bwd_assemble")
    gs["rel_bias"] = jnp.concatenate(rel, axis=0).T
    dom = _mm(dys[2], wts["w_mem_out"], "bwd_mem_out_d", tm=2048)
    gb["w_mem_out"] = _mm(dys[2], s["om"], "bwd_mem_out_w", ta=True, out_dtype=F32)
    dzq, dkv = _mem_bwd(s["zq"], s["kv"], dom, "mem_bwd")
    dkv = dkv.astype(BF16)
    gb["w_mem_kv"] = _mm(nm, dkv, "bwd_mem_kv_w", ta=True, out_dtype=F32)
    dnm = _mm(dkv, wts["w_mem_kv"], "bwd_mem_kv_d", tb=True, out_dtype=F32)
    _, gs["norm_mem"] = _norm_bwd(mem, sm["norm_mem"][l], dnm, None, BF16, "bwd_norm_mem")
    w_seg = _split_w_in(wts["w_in"])
    dh, dws = None, []
    for i, (dz, nm_) in enumerate(((dzc, "conv"), (dza, "att"), (dzq, "memq"), (dzg, "gate"))):
        blk = 1152 if nm_ == "att" else 1024
        dh = _mm(dz, w_seg[i], f"bwd_in_{nm_}_d", out_dtype=F32, acc=dh, tk=blk)
        dws.append(_mm(dz, s["h"], f"bwd_in_{nm_}_w", ta=True, out_dtype=F32, tm=blk))
    gb["w_in"] = jnp.concatenate(dws, axis=0)
    dx, gs["norm_mix_pre"] = _norm_bwd(s["x"], sm["norm_mix_pre"][l], dh, dx1, F32, "bwd_norm_mix_pre")
    return dx, gb, gs


def _local_step(x, mem, target, wts, sm):
    t = x.shape[0]
    bias_tabs, bucket_idx = [], []
    for g, (_, dil) in enumerate(ATT_PATTERNS):
        idx = _bucket_index(_att_tiles(t // dil)[1], dil)
        bucket_idx.append(jnp.asarray(idx))
        bias_tabs.append(_bias_table(sm["rel_bias"][:, g * HEADS:(g + 1) * HEADS], bucket_idx[g], f"bias_table_d{dil}"))
    h = _norm_plain(x, sm["norm_mix_pre"][0], "norm_first")
    saved, nms = [], []
    for l in range(DEPTH):
        nm = _norm_plain(mem, sm["norm_mem"][l], "norm_mem")
        x, h, s = _layer_fwd(l, x, h, nm, wts[l], sm, bias_tabs)
        saved.append(s)
        nms.append(nm)
    loss, dx = _loss_head(x, target, "loss_head")
    gbig, gsmall = [None] * DEPTH, {}
    for l in reversed(range(DEPTH)):
        dx, gb, gs = _layer_bwd(l, dx, saved[l], nms[l], mem, wts[l], sm, bias_tabs, bucket_idx)
        gbig[l] = gb
        for n, v in gs.items():
            gsmall.setdefault(n, {})[l] = v
    small = {}
    for n, _ in SMALL:
        small[n] = gsmall[n][0] + gsmall[n][1] if n == "rel_bias" else jnp.stack([gsmall[n][0], gsmall[n][1]])
    return loss, dx, gbig, small


def kernel(x, mem, rel_bias, norm_mix_pre, w_in, b_gate, conv_dw, conv_dw_bias, conv_ln_g, conv_ln_b, w_conv_out, w_att_out, norm_mem, w_mem_kv, w_mem_out, w_out, norm_mix_post, norm_ffn_pre, w_ffn_in, w_ffn_out, norm_ffn_post, loss_target, m_rel_bias, m_norm_mix_pre, m_w_in, m_b_gate, m_conv_dw, m_conv_dw_bias, m_conv_ln_g, m_conv_ln_b, m_w_conv_out, m_w_att_out, m_norm_mem, m_w_mem_kv, m_w_mem_out, m_w_out, m_norm_mix_post, m_norm_ffn_pre, m_w_ffn_in, m_w_ffn_out, m_norm_ffn_post, v_rel_bias, v_norm_mix_pre, v_w_in, v_b_gate, v_conv_dw, v_conv_dw_bias, v_conv_ln_g, v_conv_ln_b, v_w_conv_out, v_w_att_out, v_norm_mem, v_w_mem_kv, v_w_mem_out, v_w_out, v_norm_mix_post, v_norm_ffn_pre, v_w_ffn_in, v_w_ffn_out, v_norm_ffn_post):
    args = dict(locals())
    w = {n: args[n] for n in TWIN_WEIGHTS}
    m = {n: args["m_" + n] for n in TWIN_WEIGHTS}
    v = {n: args["v_" + n] for n in TWIN_WEIGHTS}
    sm = {n: w[n] for n, _ in SMALL}

    wts = [_full_weights(_all_gather(_pack_layer(w, l), "gather_weights")) for l in range(DEPTH)]
    loss, dx, gbig, gsmall = _local_step(x[0], mem[0], loss_target[0], wts, sm)
    loss = lax.psum(loss, ("x", "y", "c"))

    g_layers = [_unpack_blocks(_reduce_scatter(_scatter_layout(gbig[l]), "layer")) for l in range(DEPTH)]
    small_shapes = [s for _, s in SMALL]
    g_small = _unpack(_sum_slots(_all_gather(_pack([gsmall[n] for n, _ in SMALL], SMALL_ROWS, F32), "gather_small"),
                                 F32, "sum_small"), small_shapes)
    grads = {n: jnp.stack([g_layers[0][i], g_layers[1][i]]) for i, (n, _, _) in enumerate(BIG)}
    grads.update({n: g_small[i] for i, (n, _) in enumerate(SMALL)})

    delta, new_m, new_v = {}, {}, {}
    for n in TWIN_WEIGHTS:
        delta[n], new_m[n], new_v[n] = _adamw(w[n], grads[n], m[n], v[n], f"adamw_{n}")

    return (loss, dx[None], *[grads[n] for n in TWIN_WEIGHTS], *[delta[n] for n in TWIN_WEIGHTS],
            *[new_m[n] for n in TWIN_WEIGHTS], *[new_v[n] for n in TWIN_WEIGHTS])
```

```python
import functools

import numpy as np
import jax
import jax.numpy as jnp
from jax import lax
from jax.experimental import pallas as pl
from jax.experimental.pallas import tpu as pltpu

F32 = jnp.float32
BF16 = jnp.bfloat16

N_DEV = 8
D = 1024
DEPTH = 2
CONV_W = 512
CONV_K = 31
CONV_PAD = 16
ATT_PATTERNS = ((128, 1), (512, 4), (2048, 16))
ATT_RADIUS = 64
HEADS = 4
HEAD_DIM = 64
ATT_GW = HEADS * HEAD_DIM
ATT_W = 3 * ATT_GW
MEM_HEADS = 4
MEM_HD = 128
MEM_W = 512
N_MEM = 256
FFN_H = 2816
NUM_BUCKETS = 32
MAX_DISTANCE = 1024
RMS_EPS = 1e-6
LN_EPS = 1e-5
NEG_INF = -1e30
SEG = (2 * CONV_W, 3 * ATT_W, MEM_W, 3 * D)
ADAM_LR, ADAM_B1, ADAM_B2, ADAM_EPS, ADAM_WD, ADAM_STEP = 0.001, 0.9, 0.999, 1e-08, 0.01, 10

VMEM_LIMIT_V7X = 48 * 1024 * 1024
MESH = pl.DeviceIdType.MESH


def _cp(n_axes):
    return pltpu.CompilerParams(dimension_semantics=("arbitrary",) * n_axes, vmem_limit_bytes=VMEM_LIMIT_V7X)


def _row(v):
    return v.reshape(1, -1)


def _mm(a, b, name, *, ta=False, tb=False, out_dtype=BF16, acc=None, tm=1024, tn=1024, tk=1024):
    kdim, m = a.shape if ta else a.shape[::-1]
    n, kb = b.shape if tb else b.shape[::-1]
    assert kb == kdim and a.dtype == BF16 and b.dtype == BF16
    tm, tn, tk = min(tm, m), min(tn, n), min(tk, kdim)
    assert m % tm == 0 and n % tn == 0 and kdim % tk == 0, (name, m, n, kdim, tm, tn, tk)
    nk = kdim // tk
    a_spec = pl.BlockSpec((tk, tm), lambda i, j, k: (k, i)) if ta else pl.BlockSpec((tm, tk), lambda i, j, k: (i, k))
    b_spec = pl.BlockSpec((tn, tk), lambda i, j, k: (j, k)) if tb else pl.BlockSpec((tk, tn), lambda i, j, k: (k, j))
    o_spec = pl.BlockSpec((tm, tn), lambda i, j, k: (i, j))
    dims = (((0 if ta else 1,), (1 if tb else 0,)), ((), ()))
    has_acc = acc is not None
    use_scratch = nk > 1 and out_dtype != F32
    assert not has_acc or out_dtype == F32

    def body(*refs):
        a_ref, b_ref = refs[0], refs[1]
        o_ref = refs[2 + has_acc]
        p = lax.dot_general(a_ref[...], b_ref[...], dims, preferred_element_type=F32)
        if nk == 1:
            if has_acc:
                p = p + refs[2][...]
            o_ref[...] = p.astype(o_ref.dtype)
            return
        k = pl.program_id(2)
        t_ref = refs[3 + has_acc] if use_scratch else o_ref

        @pl.when(k == 0)
        def _():
            t_ref[...] = (p + refs[2][...]) if has_acc else p

        @pl.when(k > 0)
        def _():
            t_ref[...] += p

        if use_scratch:
            @pl.when(k == nk - 1)
            def _():
                o_ref[...] = t_ref[...].astype(o_ref.dtype)

    return pl.pallas_call(
        body, name=name, grid=(m // tm, n // tn, nk),
        in_specs=[a_spec, b_spec] + ([o_spec] if has_acc else []), out_specs=o_spec,
        out_shape=jax.ShapeDtypeStruct((m, n), out_dtype),
        scratch_shapes=[pltpu.VMEM((tm, tn), F32)] if use_scratch else [],
        compiler_params=_cp(3),
    )(*((a, b, acc) if has_acc else (a, b)))


def _rms_val(v, g):
    return v * lax.rsqrt(jnp.mean(v * v, axis=-1, keepdims=True) + RMS_EPS) * g


def _norm_plain(v, g, name, tr=512):
    t, d = v.shape
    tr = min(tr, t)

    def body(v_ref, g_ref, o_ref):
        o_ref[...] = _rms_val(v_ref[...], g_ref[...]).astype(BF16)

    return pl.pallas_call(
        body, name=name, grid=(t // tr,),
        in_specs=[pl.BlockSpec((tr, d), lambda i: (i, 0)), pl.BlockSpec((1, d), lambda i: (0, 0))],
        out_specs=pl.BlockSpec((tr, d), lambda i: (i, 0)),
        out_shape=jax.ShapeDtypeStruct((t, d), BF16), compiler_params=_cp(1),
    )(v, _row(g))


def _norm_res(xres, v, g, g_next, name, tr=512):
    t, d = v.shape
    tr = min(tr, t)
    two = g_next is not None

    def body(*refs):
        x_ref, v_ref, g_ref = refs[:3]
        xn = x_ref[...] + _rms_val(v_ref[...], g_ref[...])
        refs[3 + two][...] = xn
        if two:
            refs[5][...] = _rms_val(xn, refs[3][...]).astype(BF16)

    row = pl.BlockSpec((tr, d), lambda i: (i, 0))
    vec = pl.BlockSpec((1, d), lambda i: (0, 0))
    outs = pl.pallas_call(
        body, name=name, grid=(t // tr,),
        in_specs=[row, row, vec] + ([vec] if two else []),
        out_specs=[row, row] if two else [row],
        out_shape=[jax.ShapeDtypeStruct((t, d), F32)] + ([jax.ShapeDtypeStruct((t, d), BF16)] if two else []),
        compiler_params=_cp(1),
    )(*((xres, v, _row(g)) + ((_row(g_next),) if two else ())))
    return (outs[0], outs[1]) if two else (outs[0], None)


def _norm_bwd(v, g, dout, dres, out_dtype, name, tr=512):
    t, d = v.shape
    tr = min(tr, t)
    nt = t // tr
    has_res = dres is not None

    def body(*refs):
        v_ref, g_ref, do_ref = refs[:3]
        dv_ref, dg_ref = refs[3 + has_res], refs[4 + has_res]
        i = pl.program_id(0)
        vv = v_ref[...]
        dy = do_ref[...].astype(F32)
        r = lax.rsqrt(jnp.mean(vv * vv, axis=-1, keepdims=True) + RMS_EPS)
        vhat = vv * r
        dvh = dy * g_ref[...]
        dv = r * (dvh - vhat * jnp.mean(dvh * vhat, axis=-1, keepdims=True))
        if has_res:
            dv = dv + refs[3][...]
        dv_ref[...] = dv.astype(dv_ref.dtype)
        part = jnp.sum(dy * vhat, axis=0, keepdims=True)

        @pl.when(i == 0)
        def _():
            dg_ref[...] = jnp.zeros_like(dg_ref)

        dg_ref[0:1, :] += part

    row = pl.BlockSpec((tr, d), lambda i: (i, 0))
    dv, dg = pl.pallas_call(
        body, name=name, grid=(nt,),
        in_specs=[row, pl.BlockSpec((1, d), lambda i: (0, 0)), row] + ([row] if has_res else []),
        out_specs=[row, pl.BlockSpec((8, d), lambda i: (0, 0))],
        out_shape=[jax.ShapeDtypeStruct((t, d), out_dtype), jax.ShapeDtypeStruct((8, d), F32)],
        compiler_params=_cp(1),
    )(*((v, _row(g), dout) + ((dres,) if has_res else ())))
    return dv, dg[0]


def _loss_head(y, target, name, tr=512):
    t, d = y.shape
    tr = min(tr, t)
    nt = t // tr

    def body(y_ref, t_ref, dy_ref, l_ref):
        i = pl.program_id(0)
        err = y_ref[...] - t_ref[...]
        dy_ref[...] = err * (1.0 / d)

        @pl.when(i == 0)
        def _():
            l_ref[...] = jnp.zeros_like(l_ref)

        l_ref[...] += jnp.sum(err * err) * (0.5 / d)

    row = pl.BlockSpec((tr, d), lambda i: (i, 0))
    dy, l = pl.pallas_call(
        body, name=name, grid=(nt,), in_specs=[row, row],
        out_specs=[row, pl.BlockSpec((8, 128), lambda i: (0, 0))],
        out_shape=[jax.ShapeDtypeStruct((t, d), F32), jax.ShapeDtypeStruct((8, 128), F32)],
        compiler_params=_cp(1),
    )(y, target)
    return l[0, 0], dy


def _sigmoid(v):
    return 1.0 / (1.0 + jnp.exp(-v))


def _halo_specs(tq, width, nt, halo):
    per = tq // halo
    last = nt * per - 1
    main = pl.BlockSpec((tq, width), lambda i: (i, 0))
    prev = pl.BlockSpec((halo, width), lambda i: (jnp.maximum(i * per - 1, 0), 0))
    nxt = pl.BlockSpec((halo, width), lambda i: (jnp.minimum((i + 1) * per, last), 0))
    return prev, main, nxt


def _glu_window(zp, zm, zn, i, nt):
    def glu(z):
        z = z.astype(F32)
        return z[:, :CONV_W] * _sigmoid(z[:, CONV_W:])

    up = jnp.where(i > 0, glu(zp), 0.0)
    un = jnp.where(i < nt - 1, glu(zn), 0.0)
    return jnp.concatenate([up, glu(zm), un], axis=0)


def _shifted(win, shift, rows):
    if shift == 0:
        return win[:rows]
    return pltpu.roll(win, win.shape[0] - shift, 0)[:rows]


def _conv_fwd(zc, w, b, ln_g, ln_b, name, tq=256):
    t = zc.shape[0]
    tq = min(tq, t)
    nt = t // tq

    def body(zp_ref, zm_ref, zn_ref, w_ref, b_ref, g_ref, bb_ref, y_ref, c_ref):
        i = pl.program_id(0)
        win = _glu_window(zp_ref[...], zm_ref[...], zn_ref[...], i, nt)
        wv = w_ref[...]
        y = jnp.zeros((tq, CONV_W), F32) + b_ref[...]
        for k in range(CONV_K):
            y = y + _shifted(win, k + 1, tq) * wv[k:k + 1, :]
        y_ref[...] = y
        mu = jnp.mean(y, axis=-1, keepdims=True)
        yc = y - mu
        ln = yc * lax.rsqrt(jnp.mean(yc * yc, axis=-1, keepdims=True) + LN_EPS) * g_ref[...] + bb_ref[...]
        c_ref[...] = (ln * _sigmoid(ln)).astype(BF16)

    vec = pl.BlockSpec((1, CONV_W), lambda i: (0, 0))
    out = pl.BlockSpec((tq, CONV_W), lambda i: (i, 0))
    return pl.pallas_call(
        body, name=name, grid=(nt,),
        in_specs=[*_halo_specs(tq, 2 * CONV_W, nt, CONV_PAD), pl.BlockSpec((32, CONV_W), lambda i: (0, 0)), vec, vec, vec],
        out_specs=[out, out],
        out_shape=[jax.ShapeDtypeStruct((t, CONV_W), F32), jax.ShapeDtypeStruct((t, CONV_W), BF16)],
        compiler_params=_cp(1),
    )(zc, zc, zc, w, _row(b), _row(ln_g), _row(ln_b))


def _conv_bwd(zc, y, dc, w, ln_g, ln_b, name, tq=256):
    t = zc.shape[0]
    tq = min(tq, t)
    nt = t // tq
    rows = tq + 2 * CONV_PAD

    def body(zp_ref, zm_ref, zn_ref, yp_ref, ym_ref, yn_ref, dp_ref, dm_ref, dn_ref, w_ref, g_ref, bb_ref,
             dz_ref, dw_ref, db_ref, dg_ref, dbb_ref):
        i = pl.program_id(0)
        uwin = _glu_window(zp_ref[...], zm_ref[...], zn_ref[...], i, nt)
        ywin = jnp.concatenate([yp_ref[...], ym_ref[...], yn_ref[...]], axis=0)
        dcw = jnp.concatenate([dp_ref[...], dm_ref[...], dn_ref[...]], axis=0).astype(F32)
        mu = jnp.mean(ywin, axis=-1, keepdims=True)
        yc = ywin - mu
        rstd = lax.rsqrt(jnp.mean(yc * yc, axis=-1, keepdims=True) + LN_EPS)
        yhat = yc * rstd
        ln = yhat * g_ref[...] + bb_ref[...]
        sg = _sigmoid(ln)
        dl = dcw * (sg * (1.0 + ln * (1.0 - sg)))
        ridx = lax.broadcasted_iota(jnp.int32, (rows, 1), 0)
        inside = ((ridx >= CONV_PAD) | (i > 0)) & ((ridx < tq + CONV_PAD) | (i < nt - 1))
        dl = jnp.where(inside, dl, 0.0)
        dyh = dl * g_ref[...]
        dy = rstd * (dyh - jnp.mean(dyh, axis=-1, keepdims=True) - yhat * jnp.mean(dyh * yhat, axis=-1, keepdims=True))
        dy = jnp.where(inside, dy, 0.0)
        main = slice(CONV_PAD, CONV_PAD + tq)
        dlm, yhm, dym = dl[main], yhat[main], dy[main]

        @pl.when(i == 0)
        def _():
            dw_ref[...] = jnp.zeros_like(dw_ref)
            db_ref[...] = jnp.zeros_like(db_ref)
            dg_ref[...] = jnp.zeros_like(dg_ref)
            dbb_ref[...] = jnp.zeros_like(dbb_ref)

        dg_ref[0:1, :] += jnp.sum(dlm * yhm, axis=0, keepdims=True)
        dbb_ref[0:1, :] += jnp.sum(dlm, axis=0, keepdims=True)
        db_ref[0:1, :] += jnp.sum(dym, axis=0, keepdims=True)
        wv = w_ref[...]
        du = jnp.zeros((tq, CONV_W), F32)
        for k in range(CONV_K):
            du = du + _shifted(dy, 2 * CONV_PAD - 1 - k, tq) * wv[k:k + 1, :]
            dw_ref[k:k + 1, :] += jnp.sum(dym * _shifted(uwin, k + 1, tq), axis=0, keepdims=True)
        zm = zm_ref[...].astype(F32)
        a, gt = zm[:, :CONV_W], zm[:, CONV_W:]
        sgt = _sigmoid(gt)
        dz_ref[:, :CONV_W] = (du * sgt).astype(BF16)
        dz_ref[:, CONV_W:] = (du * a * sgt * (1.0 - sgt)).astype(BF16)

    vec = pl.BlockSpec((1, CONV_W), lambda i: (0, 0))
    acc = pl.BlockSpec((8, CONV_W), lambda i: (0, 0))
    dz, dw, db, dg, dbb = pl.pallas_call(
        body, name=name, grid=(nt,),
        in_specs=[*_halo_specs(tq, 2 * CONV_W, nt, CONV_PAD), *_halo_specs(tq, CONV_W, nt, CONV_PAD),
                  *_halo_specs(tq, CONV_W, nt, CONV_PAD), pl.BlockSpec((32, CONV_W), lambda i: (0, 0)), vec, vec],
        out_specs=[pl.BlockSpec((tq, 2 * CONV_W), lambda i: (i, 0)), pl.BlockSpec((32, CONV_W), lambda i: (0, 0)), acc, acc, acc],
        out_shape=[jax.ShapeDtypeStruct((t, 2 * CONV_W), BF16), jax.ShapeDtypeStruct((32, CONV_W), F32),
                   jax.ShapeDtypeStruct((8, CONV_W), F32), jax.ShapeDtypeStruct((8, CONV_W), F32),
                   jax.ShapeDtypeStruct((8, CONV_W), F32)],
        compiler_params=_cp(1),
    )(zc, zc, zc, y, y, y, dc, dc, dc, w, _row(ln_g), _row(ln_b))
    return dz, dw, db[0], dg[0], dbb[0]


def _att_tiles(seq_len):
    tq = min(256, seq_len)
    return tq, min(128, tq)


def _t5_bucket_np(rel):
    nb = NUM_BUCKETS // 2
    max_exact = nb // 2
    ret = np.where(rel > 0, nb, 0)
    n = np.abs(rel)
    nf = np.maximum(n, 1).astype(np.float32)
    large = max_exact + (np.log(nf / np.float32(max_exact)) / np.float32(np.log(MAX_DISTANCE / max_exact))
                         * np.float32(nb - max_exact)).astype(np.int32)
    large = np.minimum(large, nb - 1)
    return ret + np.where(n < max_exact, n, large)


def _bucket_index(sb, dilation):
    off = np.arange(sb + 2 * ATT_RADIUS)[None, :] - ATT_RADIUS - np.arange(sb)[:, None]
    idx = _t5_bucket_np(off * dilation).astype(np.int32)
    return np.where(np.abs(off) <= ATT_RADIUS, idx, -1).astype(np.int32)


def _bias_table(tab, idx, name):
    sb, w = idx.shape

    def body(tab_ref, idx_ref, o_ref):
        ix = idx_ref[...]
        for h in range(HEADS):
            acc = jnp.full((sb, w), NEG_INF, F32)
            for b in range(NUM_BUCKETS):
                acc = jnp.where(ix == b, tab_ref[b, h], acc)
            o_ref[h] = acc

    return pl.pallas_call(
        body, name=name, out_shape=jax.ShapeDtypeStruct((HEADS, sb, w), F32),
        in_specs=[pl.BlockSpec(memory_space=pltpu.SMEM), pl.BlockSpec(memory_space=pltpu.VMEM)],
        out_specs=pl.BlockSpec(memory_space=pltpu.VMEM),
    )(tab, idx)


def _qkv_specs(tq, per, last_blk, cols, tile):
    def main(col):
        return pl.BlockSpec((tq, ATT_GW), lambda *g: (tile(*g), col))

    def prev(col):
        return pl.BlockSpec((ATT_RADIUS, ATT_GW), lambda *g: (jnp.maximum(tile(*g) * per - 1, 0), col))

    def nxt(col):
        return pl.BlockSpec((ATT_RADIUS, ATT_GW), lambda *g: (jnp.minimum((tile(*g) + 1) * per, last_blk), col))

    cq, ck, cv = cols
    return [main(cq), prev(ck), main(ck), nxt(ck), prev(cv), main(cv), nxt(cv)]


def _att_fwd(q, k, v, bias, seq_len, name, cols=(0, 0, 0)):
    t = q.shape[0]
    tq, sb = _att_tiles(seq_len)
    nbl = seq_len // tq
    nt = t // tq
    per = tq // ATT_RADIUS
    w = sb + 2 * ATT_RADIUS
    last_blk = t // ATT_RADIUS - 1

    def body(q_ref, kp_ref, kc_ref, kn_ref, vp_ref, vc_ref, vn_ref, b_ref, o_ref, l_ref):
        n = pl.program_id(0) % nbl
        kw = jnp.concatenate([kp_ref[...], kc_ref[...], kn_ref[...]], axis=0)
        vw = jnp.concatenate([vp_ref[...], vc_ref[...], vn_ref[...]], axis=0)
        kpos = n * tq - ATT_RADIUS + lax.broadcasted_iota(jnp.int32, (1, tq + 2 * ATT_RADIUS), 1)
        valid = (kpos >= 0) & (kpos < seq_len)
        for j in range(tq // sb):
            qj = q_ref[j * sb:(j + 1) * sb, :]
            kj, vj, okj = kw[j * sb:j * sb + w], vw[j * sb:j * sb + w], valid[:, j * sb:j * sb + w]
            outs, lses = [], []
            for h in range(HEADS):
                hs = slice(h * HEAD_DIM, (h + 1) * HEAD_DIM)
                s = lax.dot_general(qj[:, hs], kj[:, hs], (((1,), (1,)), ((), ())), preferred_element_type=F32)
                s = jnp.where(okj, s * (HEAD_DIM ** -0.5) + b_ref[h], NEG_INF)
                m = jnp.max(s, axis=-1, keepdims=True)
                e = jnp.exp(s - m)
                den = jnp.sum(e, axis=-1, keepdims=True)
                o = jnp.dot(e.astype(BF16), vj[:, hs], preferred_element_type=F32) / den
                outs.append(o)
                lses.append(jnp.broadcast_to(m + jnp.log(den), (sb, HEAD_DIM)))
            o_ref[j * sb:(j + 1) * sb, :] = jnp.concatenate(outs, axis=1).astype(BF16)
            l_ref[j * sb:(j + 1) * sb, :] = jnp.concatenate(lses, axis=1)

    main = pl.BlockSpec((tq, ATT_GW), lambda i: (i, 0))
    return pl.pallas_call(
        body, name=name, grid=(nt,),
        in_specs=[*_qkv_specs(tq, per, last_blk, cols, lambda i: i), pl.BlockSpec((HEADS, sb, w), lambda i: (0, 0, 0))],
        out_specs=[main, main],
        out_shape=[jax.ShapeDtypeStruct((t, ATT_GW), BF16), jax.ShapeDtypeStruct((t, ATT_GW), F32)],
        compiler_params=_cp(1),
    )(q, k, k, k, v, v, v, bias)


PERM_ROWS = 512
LANES = 128


def _spec_rm(d, tr, width, col=0):
    return pl.BlockSpec((d, tr // d, width), lambda i: (0, i, col))


def _rm_view(a, d):
    return a.reshape(d, a.shape[0] // d, a.shape[1])


def _gather_residues(scr, val, d, out_ref, col0):
    tr, w = val.shape
    for c in range(w // LANES):
        scr[...] = val[:, c * LANES:(c + 1) * LANES].astype(F32)
        for r in range(d):
            out_ref[r, :, col0 + c * LANES:col0 + (c + 1) * LANES] = scr[pl.ds(r, tr // d, stride=d), :].astype(out_ref.dtype)


def _scatter_residues(scr, ref, d):
    w = ref.shape[2]
    cols = []
    for c in range(w // LANES):
        for r in range(d):
            scr[pl.ds(r, scr.shape[0] // d, stride=d), :] = ref[r, :, c * LANES:(c + 1) * LANES].astype(F32)
        cols.append(scr[...])
    return jnp.concatenate(cols, axis=1)


def _to_residue_major(srcs, d, name, tr=PERM_ROWS):
    t = srcs[0][0].shape[0]
    n = len(srcs)

    def body(*refs):
        scr = refs[2 * n]
        for k in range(n):
            _gather_residues(scr, refs[k][...], d, refs[n + k], 0)

    outs = pl.pallas_call(
        body, name=name, grid=(t // tr,),
        in_specs=[pl.BlockSpec((tr, ATT_GW), functools.partial(lambda i, col: (i, col), col=col)) for _, col in srcs],
        out_specs=[_spec_rm(d, tr, ATT_GW)] * n,
        out_shape=[jax.ShapeDtypeStruct((d, t // d, ATT_GW), a.dtype) for a, _ in srcs],
        scratch_shapes=[pltpu.VMEM((tr, LANES), F32)], compiler_params=_cp(1),
    )(*[a for a, _ in srcs])
    return [o.reshape(t, ATT_GW) for o in outs]


def _att_combine(outs, lses, name, tr=PERM_ROWS):
    t = outs[0].shape[0]
    dils = [d for _, d in ATT_PATTERNS]

    def body(o0, o1, o2, l0, l1, l2, a_ref, lse_ref, scr):
        os_, ls_ = [], []
        for o_ref, l_ref, d in zip((o0, o1, o2), (l0, l1, l2), dils):
            if d == 1:
                os_.append(o_ref[...].astype(F32))
                ls_.append(l_ref[...])
            else:
                os_.append(_scatter_residues(scr, o_ref, d))
                ls_.append(_scatter_residues(scr, l_ref, d))
        la, lb, lc = ls_
        m = jnp.maximum(jnp.maximum(la, lb), lc)
        ea, eb, ec = jnp.exp(la - m), jnp.exp(lb - m), jnp.exp(lc - m)
        den = ea + eb + ec
        a_ref[...] = ((ea * os_[0] + eb * os_[1] + ec * os_[2]) / den).astype(BF16)
        lse_ref[...] = m + jnp.log(den)

    row = pl.BlockSpec((tr, ATT_GW), lambda i: (i, 0))
    specs = [row if d == 1 else _spec_rm(d, tr, ATT_GW) for d in dils]
    views = lambda arrs: [a if d == 1 else _rm_view(a, d) for a, d in zip(arrs, dils)]
    return pl.pallas_call(
        body, name=name, grid=(t // tr,), in_specs=specs * 2, out_specs=[row, row],
        out_shape=[jax.ShapeDtypeStruct((t, ATT_GW), BF16), jax.ShapeDtypeStruct((t, ATT_GW), F32)],
        scratch_shapes=[pltpu.VMEM((tr, LANES), F32)], compiler_params=_cp(1),
    )(*views(outs), *views(lses))


def _assemble_dza(dqkv, name, tr=PERM_ROWS):
    t = dqkv[0][0].shape[0]
    dils = [d for _, d in ATT_PATTERNS]

    def body(*refs):
        o_ref, scr = refs[9], refs[10]
        for g, d in enumerate(dils):
            for c in range(3):
                ref = refs[g * 3 + c]
                val = ref[...] if d == 1 else _scatter_residues(scr, ref, d).astype(BF16)
                o_ref[:, (c * 3 + g) * ATT_GW:(c * 3 + g + 1) * ATT_GW] = val

    row = pl.BlockSpec((tr, ATT_GW), lambda i: (i, 0))
    specs = [row if d == 1 else _spec_rm(d, tr, ATT_GW) for d in dils for _ in range(3)]
    args = [a if d == 1 else _rm_view(a, d) for trio, d in zip(dqkv, dils) for a in trio]
    return pl.pallas_call(
        body, name=name, grid=(t // tr,), in_specs=specs, out_specs=pl.BlockSpec((tr, 3 * ATT_W), lambda i: (i, 0)),
        out_shape=jax.ShapeDtypeStruct((t, 3 * ATT_W), BF16),
        scratch_shapes=[pltpu.VMEM((tr, LANES), F32)], compiler_params=_cp(1),
    )(*args)


def _att_bwd(q, k, v, bias, o, do, lse, seq_len, name, cols=(0, 0, 0)):
    t = q.shape[0]
    tq, sb = _att_tiles(seq_len)
    nbl = seq_len // tq
    n_res = t // seq_len
    per = tq // ATT_RADIUS
    w = sb + 2 * ATT_RADIUS
    last_blk = t // ATT_RADIUS - 1
    acc_rows = 2 * tq + ATT_RADIUS
    scale = HEAD_DIM ** -0.5

    def body(q_ref, kp_ref, kc_ref, kn_ref, vp_ref, vc_ref, vn_ref, b_ref, o_ref, do_ref, l_ref,
             dq_ref, dk_ref, dv_ref, db_ref, ak_ref, av_ref):
        r, n = pl.program_id(0), pl.program_id(1)

        @pl.when((r == 0) & (n == 0))
        def _():
            db_ref[...] = jnp.zeros_like(db_ref)

        @pl.when(n == 0)
        def _():
            ak_ref[...] = jnp.zeros_like(ak_ref)
            av_ref[...] = jnp.zeros_like(av_ref)

        @pl.when(n < nbl)
        def _():
            kw = jnp.concatenate([kp_ref[...], kc_ref[...], kn_ref[...]], axis=0)
            vw = jnp.concatenate([vp_ref[...], vc_ref[...], vn_ref[...]], axis=0)
            kpos = n * tq - ATT_RADIUS + lax.broadcasted_iota(jnp.int32, (1, tq + 2 * ATT_RADIUS), 1)
            valid = (kpos >= 0) & (kpos < seq_len)
            for j in range(tq // sb):
                rows = slice(j * sb, (j + 1) * sb)
                qj, doj = q_ref[rows, :], do_ref[rows, :]
                dd = doj.astype(F32) * o_ref[rows, :].astype(F32)
                lj = l_ref[rows, :]
                kj, vj, okj = kw[j * sb:j * sb + w], vw[j * sb:j * sb + w], valid[:, j * sb:j * sb + w]
                dqs, dks, dvs = [], [], []
                for h in range(HEADS):
                    hs = slice(h * HEAD_DIM, (h + 1) * HEAD_DIM)
                    s = lax.dot_general(qj[:, hs], kj[:, hs], (((1,), (1,)), ((), ())), preferred_element_type=F32)
                    s = jnp.where(okj, s * scale + b_ref[h], NEG_INF)
                    p = jnp.exp(s - lj[:, h * HEAD_DIM:h * HEAD_DIM + 1])
                    dp = lax.dot_general(doj[:, hs], vj[:, hs], (((1,), (1,)), ((), ())), preferred_element_type=F32)
                    ds = p * (dp - jnp.sum(dd[:, hs], axis=-1, keepdims=True))
                    db_ref[h] += ds
                    dsb = ds.astype(BF16)
                    dqs.append(jnp.dot(dsb, kj[:, hs], preferred_element_type=F32) * scale)
                    dks.append(lax.dot_general(dsb, qj[:, hs], (((0,), (0,)), ((), ())), preferred_element_type=F32) * scale)
                    dvs.append(lax.dot_general(p.astype(BF16), doj[:, hs], (((0,), (0,)), ((), ())), preferred_element_type=F32))
                dq_ref[rows, :] = jnp.concatenate(dqs, axis=1).astype(BF16)
                a0 = j * sb + tq - ATT_RADIUS
                ak_ref[a0:a0 + w, :] += jnp.concatenate(dks, axis=1)
                av_ref[a0:a0 + w, :] += jnp.concatenate(dvs, axis=1)

        dk_ref[...] = ak_ref[0:tq, :].astype(BF16)
        dv_ref[...] = av_ref[0:tq, :].astype(BF16)
        keep = tq + ATT_RADIUS
        nk, nv = ak_ref[tq:acc_rows, :], av_ref[tq:acc_rows, :]
        ak_ref[0:keep, :] = nk
        av_ref[0:keep, :] = nv
        ak_ref[keep:acc_rows, :] = jnp.zeros((acc_rows - keep, ATT_GW), F32)
        av_ref[keep:acc_rows, :] = jnp.zeros((acc_rows - keep, ATT_GW), F32)

    def tile(r, n):
        return r * nbl + jnp.minimum(n, nbl - 1)

    main = pl.BlockSpec((tq, ATT_GW), lambda r, n: (tile(r, n), 0))
    lag = pl.BlockSpec((tq, ATT_GW), lambda r, n: (r * nbl + jnp.maximum(n - 1, 0), 0))
    bspec = pl.BlockSpec((HEADS, sb, w), lambda r, n: (0, 0, 0))
    return pl.pallas_call(
        body, name=name, grid=(n_res, nbl + 1),
        in_specs=[*_qkv_specs(tq, per, last_blk, cols, tile), bspec, main, main, main],
        out_specs=[main, lag, lag, bspec],
        out_shape=[jax.ShapeDtypeStruct((t, ATT_GW), BF16)] * 3 + [jax.ShapeDtypeStruct((HEADS, sb, w), F32)],
        scratch_shapes=[pltpu.VMEM((acc_rows, ATT_GW), F32), pltpu.VMEM((acc_rows, ATT_GW), F32)],
        compiler_params=_cp(2),
    )(q, k, k, k, v, v, v, bias, o, do, lse)


def _bias_grad(db, idx, name):
    _, sb, w = db.shape

    def body(db_ref, idx_ref, o_ref):
        ix = idx_ref[...]
        lane = lax.broadcasted_iota(jnp.int32, (1, 128), 1)
        rows = []
        for h in range(HEADS):
            d = db_ref[h]
            acc = jnp.zeros((1, 128), F32)
            for b in range(NUM_BUCKETS):
                acc = acc + jnp.where(lane == b, jnp.sum(jnp.where(ix == b, d, 0.0)), 0.0)
            rows.append(acc)
        o_ref[...] = jnp.concatenate(rows + [jnp.zeros((8 - HEADS, 128), F32)], axis=0)

    return pl.pallas_call(
        body, name=name, out_shape=jax.ShapeDtypeStruct((8, 128), F32),
        in_specs=[pl.BlockSpec(memory_space=pltpu.VMEM), pl.BlockSpec(memory_space=pltpu.VMEM)],
        out_specs=pl.BlockSpec(memory_space=pltpu.VMEM),
    )(db, idx)


def _mem_fwd(zq, kv, name, tq=512):
    t = zq.shape[0]
    tq = min(tq, t)
    scale = MEM_HD ** -0.5

    def body(q_ref, kv_ref, o_ref):
        outs = []
        for h in range(MEM_HEADS):
            hs = slice(h * MEM_HD, (h + 1) * MEM_HD)
            kh = kv_ref[:, h * MEM_HD:(h + 1) * MEM_HD]
            vh = kv_ref[:, MEM_W + h * MEM_HD:MEM_W + (h + 1) * MEM_HD]
            s = lax.dot_general(q_ref[:, hs], kh, (((1,), (1,)), ((), ())), preferred_element_type=F32) * scale
            e = jnp.exp(s - jnp.max(s, axis=-1, keepdims=True))
            den = jnp.sum(e, axis=-1, keepdims=True)
            outs.append(jnp.dot(e.astype(BF16), vh, preferred_element_type=F32) / den)
        o_ref[...] = jnp.concatenate(outs, axis=1).astype(BF16)

    row = pl.BlockSpec((tq, MEM_W), lambda i: (i, 0))
    return pl.pallas_call(
        body, name=name, grid=(t // tq,),
        in_specs=[row, pl.BlockSpec((N_MEM, 2 * MEM_W), lambda i: (0, 0))], out_specs=row,
        out_shape=jax.ShapeDtypeStruct((t, MEM_W), BF16), compiler_params=_cp(1),
    )(zq, kv)


def _mem_bwd(zq, kv, do, name, tq=512):
    t = zq.shape[0]
    tq = min(tq, t)
    scale = MEM_HD ** -0.5

    def body(q_ref, kv_ref, do_ref, dq_ref, dkv_ref):
        @pl.when(pl.program_id(0) == 0)
        def _():
            dkv_ref[...] = jnp.zeros_like(dkv_ref)

        dqs = []
        for h in range(MEM_HEADS):
            hs = slice(h * MEM_HD, (h + 1) * MEM_HD)
            vs = slice(MEM_W + h * MEM_HD, MEM_W + (h + 1) * MEM_HD)
            qh, doh = q_ref[:, hs], do_ref[:, hs]
            kh, vh = kv_ref[:, hs], kv_ref[:, vs]
            s = lax.dot_general(qh, kh, (((1,), (1,)), ((), ())), preferred_element_type=F32) * scale
            e = jnp.exp(s - jnp.max(s, axis=-1, keepdims=True))
            p = e / jnp.sum(e, axis=-1, keepdims=True)
            dp = lax.dot_general(doh, vh, (((1,), (1,)), ((), ())), preferred_element_type=F32)
            ds = p * (dp - jnp.sum(p * dp, axis=-1, keepdims=True))
            dsb = ds.astype(BF16)
            dqs.append(jnp.dot(dsb, kh, preferred_element_type=F32) * scale)
            dkv_ref[:, hs] += lax.dot_general(dsb, qh, (((0,), (0,)), ((), ())), preferred_element_type=F32) * scale
            dkv_ref[:, vs] += lax.dot_general(p.astype(BF16), doh, (((0,), (0,)), ((), ())), preferred_element_type=F32)
        dq_ref[...] = jnp.concatenate(dqs, axis=1).astype(BF16)

    row = pl.BlockSpec((tq, MEM_W), lambda i: (i, 0))
    full = pl.BlockSpec((N_MEM, 2 * MEM_W), lambda i: (0, 0))
    return pl.pallas_call(
        body, name=name, grid=(t // tq,), in_specs=[row, full, row], out_specs=[row, full],
        out_shape=[jax.ShapeDtypeStruct((t, MEM_W), BF16), jax.ShapeDtypeStruct((N_MEM, 2 * MEM_W), F32)],
        compiler_params=_cp(1),
    )(zq, kv, do)


def _gate_fwd(zg, bg, ys, name, tr=256):
    t = zg.shape[0]
    tr = min(tr, t)

    def body(z_ref, b_ref, y0, y1, y2, o_ref):
        g = _sigmoid(z_ref[...].astype(F32) + b_ref[...])
        o_ref[...] = (g[:, :D] * y0[...].astype(F32) + g[:, D:2 * D] * y1[...].astype(F32)
                      + g[:, 2 * D:] * y2[...].astype(F32)).astype(BF16)

    row = pl.BlockSpec((tr, D), lambda i: (i, 0))
    return pl.pallas_call(
        body, name=name, grid=(t // tr,),
        in_specs=[pl.BlockSpec((tr, 3 * D), lambda i: (i, 0)), pl.BlockSpec((1, 3 * D), lambda i: (0, 0)), row, row, row],
        out_specs=row, out_shape=jax.ShapeDtypeStruct((t, D), BF16), compiler_params=_cp(1),
    )(zg, _row(bg), *ys)


def _gate_bwd(zg, bg, ys, dmerged, name, tr=256):
    t = zg.shape[0]
    tr = min(tr, t)

    def body(z_ref, b_ref, y0, y1, y2, dm_ref, d0, d1, d2, dz_ref, db_ref):
        @pl.when(pl.program_id(0) == 0)
        def _():
            db_ref[...] = jnp.zeros_like(db_ref)

        g = _sigmoid(z_ref[...].astype(F32) + b_ref[...])
        dm = dm_ref[...].astype(F32)
        for i, (y_ref, d_ref) in enumerate(((y0, d0), (y1, d1), (y2, d2))):
            gi = g[:, i * D:(i + 1) * D]
            d_ref[...] = (dm * gi).astype(BF16)
            dz = dm * y_ref[...].astype(F32) * gi * (1.0 - gi)
            dz_ref[:, i * D:(i + 1) * D] = dz.astype(BF16)
            db_ref[0:1, i * D:(i + 1) * D] += jnp.sum(dz, axis=0, keepdims=True)

    row = pl.BlockSpec((tr, D), lambda i: (i, 0))
    wide = pl.BlockSpec((tr, 3 * D), lambda i: (i, 0))
    d0, d1, d2, dz, db = pl.pallas_call(
        body, name=name, grid=(t // tr,),
        in_specs=[wide, pl.BlockSpec((1, 3 * D), lambda i: (0, 0)), row, row, row, row],
        out_specs=[row, row, row, wide, pl.BlockSpec((8, 3 * D), lambda i: (0, 0))],
        out_shape=[jax.ShapeDtypeStruct((t, D), BF16)] * 3 + [jax.ShapeDtypeStruct((t, 3 * D), BF16),
                                                              jax.ShapeDtypeStruct((8, 3 * D), F32)],
        compiler_params=_cp(1),
    )(zg, _row(bg), *ys, dmerged)
    return (d0, d1, d2), dz, db[0]


def _swiglu_fwd(gu, name, tr=256):
    t = gu.shape[0]
    tr = min(tr, t)

    def body(gu_ref, o_ref):
        g = gu_ref[:, :FFN_H].astype(F32)
        o_ref[...] = (g * _sigmoid(g) * gu_ref[:, FFN_H:].astype(F32)).astype(BF16)

    return pl.pallas_call(
        body, name=name, grid=(t // tr,), in_specs=[pl.BlockSpec((tr, 2 * FFN_H), lambda i: (i, 0))],
        out_specs=pl.BlockSpec((tr, FFN_H), lambda i: (i, 0)),
        out_shape=jax.ShapeDtypeStruct((t, FFN_H), BF16), compiler_params=_cp(1),
    )(gu)


def _swiglu_bwd(gu, dact, name, tr=256):
    t = gu.shape[0]
    tr = min(tr, t)

    def body(gu_ref, da_ref, o_ref):
        g = gu_ref[:, :FFN_H].astype(F32)
        u = gu_ref[:, FFN_H:].astype(F32)
        da = da_ref[...].astype(F32)
        sg = _sigmoid(g)
        o_ref[:, :FFN_H] = (da * u * (sg * (1.0 + g * (1.0 - sg)))).astype(BF16)
        o_ref[:, FFN_H:] = (da * g * sg).astype(BF16)

    wide = pl.BlockSpec((tr, 2 * FFN_H), lambda i: (i, 0))
    return pl.pallas_call(
        body, name=name, grid=(t // tr,), in_specs=[wide, pl.BlockSpec((tr, FFN_H), lambda i: (i, 0))],
        out_specs=wide, out_shape=jax.ShapeDtypeStruct((t, 2 * FFN_H), BF16), compiler_params=_cp(1),
    )(gu, dact)


def _adamw(w, g, m, v, name):
    shape = w.shape
    w, g, m, v = (a.reshape(-1, shape[-1]) for a in (w, g, m, v))
    r, d = w.shape
    tr = next((c for c in (512, 352, 256, 128, 64, 32, 16, 8) if r % c == 0), r)

    def body(w_ref, g_ref, m_ref, v_ref, d_ref, nm_ref, nv_ref):
        gg = g_ref[...]
        m2 = ADAM_B1 * m_ref[...] + (1.0 - ADAM_B1) * gg
        v2 = ADAM_B2 * v_ref[...] + (1.0 - ADAM_B2) * (gg * gg)
        m_hat = m2 / (1.0 - ADAM_B1 ** ADAM_STEP)
        v_hat = v2 / (1.0 - ADAM_B2 ** ADAM_STEP)
        d_ref[...] = -ADAM_LR * (m_hat / (jnp.sqrt(v_hat) + ADAM_EPS) + ADAM_WD * w_ref[...])
        nm_ref[...] = m2
        nv_ref[...] = v2

    row = pl.BlockSpec((tr, d), lambda i: (i, 0))
    outs = pl.pallas_call(
        body, name=name, grid=(r // tr,), in_specs=[row] * 4, out_specs=[row] * 3,
        out_shape=[jax.ShapeDtypeStruct((r, d), F32)] * 3, compiler_params=_cp(1),
    )(w, g, m, v)
    return [o.reshape(shape) for o in outs]


def _slab_tile(rows):
    return next((c for c in range(min(rows, 512), 15, -16) if rows % c == 0), rows)


def _sum_slots(parts, out_dtype, name):
    n, r, d = parts.shape
    tr = _slab_tile(r)

    def body(p_ref, o_ref):
        acc = p_ref[0].astype(F32)
        for s in range(1, n):
            acc = acc + p_ref[s].astype(F32)
        o_ref[...] = acc.astype(o_ref.dtype)

    return pl.pallas_call(
        body, name=name, grid=(r // tr,), in_specs=[pl.BlockSpec((n, tr, d), lambda i: (0, i, 0))],
        out_specs=pl.BlockSpec((tr, d), lambda i: (i, 0)),
        out_shape=jax.ShapeDtypeStruct((r, d), out_dtype), compiler_params=_cp(1),
    )(parts)


def _sum_pairs(mine, theirs, name):
    n, r, d = mine.shape
    tr = _slab_tile(r)

    def body(a_ref, b_ref, o_ref):
        o_ref[...] = (a_ref[...].astype(F32) + b_ref[...].astype(F32)).astype(o_ref.dtype)

    blk = pl.BlockSpec((1, tr, d), lambda s, i: (s, i, 0))
    return pl.pallas_call(
        body, name=name, grid=(n, r // tr), in_specs=[blk, blk], out_specs=blk,
        out_shape=jax.ShapeDtypeStruct((n, r, d), mine.dtype), compiler_params=_cp(2),
    )(mine, theirs)


def _place():
    return lax.axis_index("x"), lax.axis_index("y"), lax.axis_index("c")


def _all_gather(shard, name):
    r, d = shard.shape

    def body(x_ref, out_ref, send_sems, recv_sems, local_sem):
        x, y, c = _place()
        me, sibling = (x, y, c), (x, y, 1 - c)
        chips = [(1 - x, y), (x, 1 - y), (1 - x, 1 - y)]

        def slot(px, py, pc):
            return out_ref.at[4 * px + 2 * py + pc]

        def copy(k, block, to, src=None):
            return pltpu.make_async_remote_copy(
                src_ref=slot(*block) if src is None else src, dst_ref=slot(*block),
                send_sem=send_sems.at[k], recv_sem=recv_sems.at[k], device_id=to, device_id_type=MESH)

        mine = pltpu.make_async_copy(x_ref, slot(*me), local_sem)
        mine.start()
        first = [copy(0, me, sibling, src=x_ref)]
        first += [copy(1 + j, me, (*chip, c), src=x_ref) for j, chip in enumerate(chips)]
        for cp in first:
            cp.start()
        passed = [copy(4 + j, (*chip, c), sibling) for j, chip in enumerate(chips)]
        for j, chip in enumerate(chips):
            copy(1 + j, (*chip, c), me).wait_recv()
            passed[j].start()
        copy(0, sibling, me).wait_recv()
        for j, chip in enumerate(chips):
            copy(4 + j, (*chip, 1 - c), me).wait_recv()
        for cp in first + passed:
            cp.wait_send()
        mine.wait()

    return pl.pallas_call(
        body, name=name, out_shape=jax.ShapeDtypeStruct((N_DEV, r, d), shard.dtype),
        in_specs=[pl.BlockSpec(memory_space=pl.ANY)], out_specs=pl.BlockSpec(memory_space=pl.ANY),
        scratch_shapes=[pltpu.SemaphoreType.DMA((7,)), pltpu.SemaphoreType.DMA((7,)), pltpu.SemaphoreType.DMA],
    )(shard)


def _swap_core(parts, name):
    n, _, r, d = parts.shape

    def body(p_ref, out_ref, send_sem, recv_sem):
        x, y, c = _place()
        cp = pltpu.make_async_remote_copy(src_ref=p_ref.at[:, 1 - c], dst_ref=out_ref, send_sem=send_sem, recv_sem=recv_sem,
                                          device_id=(x, y, 1 - c), device_id_type=MESH)
        cp.start()
        cp.wait()

    return pl.pallas_call(
        body, name=name, out_shape=jax.ShapeDtypeStruct((n, r, d), parts.dtype),
        in_specs=[pl.BlockSpec(memory_space=pl.ANY)], out_specs=pl.BlockSpec(memory_space=pl.ANY),
        scratch_shapes=[pltpu.SemaphoreType.DMA, pltpu.SemaphoreType.DMA],
    )(parts)


def _swap_chips(parts, name):
    n, r, d = parts.shape

    def body(p_ref, out_ref, send_sems, recv_sems, local_sem):
        x, y, c = _place()
        chips = [(1 - x, y), (x, 1 - y), (1 - x, 1 - y)]
        mine = pltpu.make_async_copy(p_ref.at[2 * x + y], out_ref.at[2 * x + y], local_sem)
        mine.start()
        sends = [pltpu.make_async_remote_copy(src_ref=p_ref.at[2 * cx + cy], dst_ref=out_ref.at[2 * x + y],
                                              send_sem=send_sems.at[j], recv_sem=recv_sems.at[j],
                                              device_id=(cx, cy, c), device_id_type=MESH)
                 for j, (cx, cy) in enumerate(chips)]
        for cp in sends:
            cp.start()
        for j, (cx, cy) in enumerate(chips):
            pltpu.make_async_remote_copy(src_ref=p_ref.at[2 * x + y], dst_ref=out_ref.at[2 * cx + cy],
                                         send_sem=send_sems.at[j], recv_sem=recv_sems.at[j],
                                         device_id=(cx, cy, c), device_id_type=MESH).wait_recv()
        for cp in sends:
            cp.wait_send()
        mine.wait()

    return pl.pallas_call(
        body, name=name, out_shape=jax.ShapeDtypeStruct((n, r, d), parts.dtype),
        in_specs=[pl.BlockSpec(memory_space=pl.ANY)], out_specs=pl.BlockSpec(memory_space=pl.ANY),
        scratch_shapes=[pltpu.SemaphoreType.DMA((3,)), pltpu.SemaphoreType.DMA((3,)), pltpu.SemaphoreType.DMA],
    )(parts)


def _reduce_scatter(contrib, tag):
    _, r, d = contrib.shape
    by_chip = contrib.reshape(4, 2, r, d)
    got = _swap_core(by_chip, f"rs_core_{tag}")
    mine = lax.dynamic_index_in_dim(by_chip, lax.axis_index("c"), axis=1, keepdims=False)
    chip_sum = _sum_pairs(mine, got, f"rs_pair_{tag}")
    return _sum_slots(_swap_chips(chip_sum, f"rs_chip_{tag}"), F32, f"rs_sum_{tag}")


HBM_SPEC = pl.BlockSpec(memory_space=pltpu.HBM)
SEM_SPEC = pl.BlockSpec(memory_space=pltpu.SEMAPHORE)
SPLIT_PARAMS = pltpu.CompilerParams(has_side_effects=pltpu.SideEffectType.DATAFLOW_SIDE_EFFECTING)


def _peers():
    x, y, c = _place()
    flip = lambda v, bit: 1 - v if bit else v
    return 4 * x + 2 * y + c, [((flip(x, k >> 2 & 1), flip(y, k >> 1 & 1), flip(c, k & 1))) for k in range(1, N_DEV)]


def _exchange_start(src, all_gather, name):
    r, d = src.shape[-2:]

    def body(src_ref, land_ref, send_sems, recv_sems, src_thru, land_thru, token):
        me, peers = _peers()
        for k, (px, py, pc) in enumerate(peers):
            part = src_ref if all_gather else src_ref.at[4 * px + 2 * py + pc]
            pltpu.make_async_remote_copy(src_ref=part, dst_ref=land_ref.at[me], send_sem=send_sems.at[k],
                                         recv_sem=recv_sems.at[k], device_id=(px, py, pc), device_id_type=MESH).start()
        token[...] = jnp.zeros_like(token)

    land = lax.empty((N_DEV, r, d), src.dtype)
    return pl.pallas_call(
        body, name=name,
        out_shape=(pltpu.SemaphoreType.DMA((N_DEV - 1,)), pltpu.SemaphoreType.DMA((N_DEV - 1,)), pltpu.HBM(src.shape, src.dtype),
                   pltpu.HBM(land.shape, land.dtype), jax.ShapeDtypeStruct((8, LANES), F32)),
        in_specs=(HBM_SPEC, HBM_SPEC), out_specs=(SEM_SPEC, SEM_SPEC, HBM_SPEC, HBM_SPEC, pl.BlockSpec(memory_space=pltpu.VMEM)),
        input_output_aliases={0: 2, 1: 3}, compiler_params=SPLIT_PARAMS,
    )(pltpu.with_memory_space_constraint(src, pltpu.HBM), pltpu.with_memory_space_constraint(land, pltpu.HBM))


def _exchange_wait(started, after, all_gather, name):
    send_sems, recv_sems, src_thru, land_thru, _ = started

    def body(src_ref, land_ref, send_sems, recv_sems, after_ref, src_out, land_out):
        me, peers = _peers()
        for k, (px, py, pc) in enumerate(peers):
            part = src_ref if all_gather else src_ref.at[4 * px + 2 * py + pc]
            cp = pltpu.make_async_remote_copy(src_ref=part, dst_ref=land_ref.at[4 * px + 2 * py + pc], send_sem=send_sems.at[k],
                                              recv_sem=recv_sems.at[k], device_id=(px, py, pc), device_id_type=MESH)
            cp.wait_send()
            cp.wait_recv()

    return pl.pallas_call(
        body, name=name,
        out_shape=(pltpu.HBM(src_thru.shape, src_thru.dtype), pltpu.HBM(land_thru.shape, land_thru.dtype)),
        in_specs=(HBM_SPEC, HBM_SPEC, SEM_SPEC, SEM_SPEC, pl.BlockSpec(memory_space=pl.ANY)), out_specs=(HBM_SPEC, HBM_SPEC),
        input_output_aliases={0: 0, 1: 1}, compiler_params=SPLIT_PARAMS,
    )(src_thru, land_thru, send_sems, recv_sems, after)


def _own_slot(land, mine):
    x, y, c = _place()
    return lax.dynamic_update_slice(land, mine[None], (4 * x + 2 * y + c, 0, 0))


BIG = (("w_in", (D, 864), 1), ("w_conv_out", (CONV_W, 128), 1), ("w_att_out", (ATT_GW, 128), 1), ("w_mem_kv", (128, D), 0),
       ("w_mem_out", (MEM_W, 128), 1), ("w_out", (128, D), 0), ("w_ffn_in", (D, 704), 1), ("w_ffn_out", (352, D), 0),
       ("conv_dw", (CONV_K, 64), 1))
SMALL = (("rel_bias", (NUM_BUCKETS, 12)), ("norm_mix_pre", (DEPTH, D)), ("b_gate", (DEPTH, 3 * D)),
         ("conv_dw_bias", (DEPTH, CONV_W)), ("conv_ln_g", (DEPTH, CONV_W)), ("conv_ln_b", (DEPTH, CONV_W)),
         ("norm_mem", (DEPTH, D)), ("norm_mix_post", (DEPTH, D)), ("norm_ffn_pre", (DEPTH, D)), ("norm_ffn_post", (DEPTH, D)))
TWIN_WEIGHTS = ("rel_bias", "norm_mix_pre", "w_in", "b_gate", "conv_dw", "conv_dw_bias", "conv_ln_g", "conv_ln_b", "w_conv_out",
                "w_att_out", "norm_mem", "w_mem_kv", "w_mem_out", "w_out", "norm_mix_post", "norm_ffn_pre", "w_ffn_in",
                "w_ffn_out", "norm_ffn_post")


BIG_INFO = {n: (s, a) for n, s, a in BIG}
GROUPS = {"mix": ("w_in", "w_conv_out", "w_att_out", "w_mem_kv", "w_mem_out", "w_out", "conv_dw"), "ffn": ("w_ffn_in", "w_ffn_out")}
TILE_BYTES_PER_LANE = 32


def _rows_of(shape, dtype):
    tile = TILE_BYTES_PER_LANE // jnp.dtype(dtype).itemsize
    return -(-int(np.prod(shape)) // (D * tile)) * tile


def _row_form(shape, axis):
    return tuple(shape) if axis == 0 else (shape[1], shape[0])


def _as_slab_rows(p, lead=()):
    shape = p.shape[len(lead):]
    rows = _rows_of(shape, p.dtype)
    if shape == (rows, D):
        return p
    n = int(np.prod(shape))
    p = jnp.pad(p.reshape(*lead, n), [(0, 0)] * len(lead) + [(0, rows * D - n)])
    return p.reshape(*lead, rows, D)


def _from_slab_rows(rows, shape, lead=()):
    if rows.shape[len(lead):] == tuple(shape):
        return rows
    n = int(np.prod(shape))
    return rows.reshape(*lead, -1)[..., :n].reshape(*lead, *shape)


def _pack(pieces, dtype):
    return jnp.concatenate([_as_slab_rows(p.astype(dtype)) for p in pieces], axis=0)


def _unpack(slab, shapes, packed_as):
    out, r0 = [], 0
    for s in shapes:
        rows = _rows_of(s, packed_as)
        out.append(_from_slab_rows(slab[r0:r0 + rows], s))
        r0 += rows
    return out


def _pack_group(blocks, layer, group):
    return _pack([blocks[n][layer] if BIG_INFO[n][1] == 0 else blocks[n][layer].T for n in GROUPS[group]], BF16)


def _full_weights(gathered, group):
    out, r0 = {}, 0
    for name in GROUPS[group]:
        r, k = _row_form(*BIG_INFO[name])
        rows = _rows_of((r, k), BF16)
        out[name] = _from_slab_rows(gathered[:, r0:r0 + rows], (r, k), lead=(N_DEV,)).reshape(N_DEV * r, k)
        r0 += rows
    return out


def _scatter_layout(grads, group):
    parts = []
    for name in GROUPS[group]:
        r, k = _row_form(*BIG_INFO[name])
        parts.append(_as_slab_rows(grads[name].astype(BF16).reshape(N_DEV, r, k), lead=(N_DEV,)))
    return jnp.concatenate(parts, axis=1)


def _unpack_blocks(slab, group):
    out = _unpack(slab, [_row_form(*BIG_INFO[n]) for n in GROUPS[group]], BF16)
    return {n: b if BIG_INFO[n][1] == 0 else b.T for n, b in zip(GROUPS[group], out)}


def _split_w_in(w_in):
    edges = np.cumsum((0,) + SEG)
    return [w_in[edges[i]:edges[i + 1]] for i in range(4)]


def _conv_taps(conv_dw):
    return jnp.pad(conv_dw.astype(F32).T, ((0, 32 - CONV_K), (0, 0)))


def _layer_fwd(l, x, h, nm, get_w, sm, bias_tabs):
    t = x.shape[0]
    wts = get_w(l, "mix", h)
    w_seg = _split_w_in(wts["w_in"])
    zc = _mm(h, w_seg[0], "fwd_in_conv", tb=True)
    za = _mm(h, w_seg[1], "fwd_in_att", tb=True, tn=1152)
    zq = _mm(h, w_seg[2], "fwd_in_memq", tb=True)
    zg = _mm(h, w_seg[3], "fwd_in_gate", tb=True)
    y, cact = _conv_fwd(zc, _conv_taps(wts["conv_dw"]), sm["conv_dw_bias"][l], sm["conv_ln_g"][l], sm["conv_ln_b"][l],
                        "conv_fwd")
    qkv, outs, lses = [], [], []
    for g, (_, dil) in enumerate(ATT_PATTERNS):
        cols = tuple(i * 3 + g for i in range(3))
        if dil == 1:
            trio, at = (za, za, za), cols
        else:
            trio, at = _to_residue_major([(za, c) for c in cols], dil, f"qkv_residue_d{dil}"), (0, 0, 0)
        o, ls = _att_fwd(*trio, bias_tabs[g], t // dil, f"att_fwd_d{dil}", cols=at)
        qkv.append((trio, at))
        outs.append(o)
        lses.append(ls)
    att, lse = _att_combine(outs, lses, "att_combine")
    kv = _mm(nm, wts["w_mem_kv"], "fwd_mem_kv")
    om = _mem_fwd(zq, kv, "mem_fwd")
    ys = (_mm(cact, wts["w_conv_out"], "fwd_conv_out", tb=True, tm=2048),
          _mm(att, wts["w_att_out"], "fwd_att_out", tb=True, tm=2048),
          _mm(om, wts["w_mem_out"], "fwd_mem_out", tb=True, tm=2048))
    merged = _gate_fwd(zg, sm["b_gate"][l], ys, "gate_fwd")
    ymix = _mm(merged, wts["w_out"], "fwd_out", out_dtype=F32)
    x1, h2 = _norm_res(x, ymix, sm["norm_mix_post"][l], sm["norm_ffn_pre"][l], "norm_mix_post")
    wff = get_w(l, "ffn", h2)
    gu = _mm(h2, wff["w_ffn_in"], "fwd_ffn_in", tb=True, tn=1408)
    act = _swiglu_fwd(gu, "swiglu_fwd")
    f = _mm(act, wff["w_ffn_out"], "fwd_ffn_out", out_dtype=F32, tk=1408)
    g_next = sm["norm_mix_pre"][l + 1] if l + 1 < DEPTH else None
    x2, h_next = _norm_res(x1, f, sm["norm_ffn_post"][l], g_next, "norm_ffn_post" if g_next is not None else "norm_ffn_last")
    saved = dict(x=x, h=h, zc=zc, zq=zq, zg=zg, y=y, cact=cact, qkv=qkv, att=att, lse=lse, kv=kv, om=om, ys=ys,
                 merged=merged, ymix=ymix, x1=x1, h2=h2, gu=gu, act=act, f=f)
    return x2, h_next, saved


def _after(value, *tokens):
    tokens = [t for t in tokens if t is not None]
    return lax.optimization_barrier((value, *tokens))[0] if tokens else value


def _layer_bwd(l, dx2, s, nm, mem, get_w, put_g, sm, bias_tabs, bucket_idx):
    t = dx2.shape[0]
    wts, wff = get_w(l, "mix", None), get_w(l, "ffn", None)
    gb, gf, gs = {}, {}, {}
    df, gs["norm_ffn_post"] = _norm_bwd(s["f"], sm["norm_ffn_post"][l], dx2, None, BF16, "bwd_norm_ffn_post")
    dact = _mm(df, wff["w_ffn_out"], "bwd_ffn_out_d", tb=True, tn=1408)
    gf["w_ffn_out"] = _mm(s["act"], df, "bwd_ffn_out_w", ta=True, tm=1408)
    dgu = _swiglu_bwd(s["gu"], dact, "swiglu_bwd")
    dh2 = _mm(dgu, wff["w_ffn_in"], "bwd_ffn_in_d", out_dtype=F32, tk=1408)
    gf["w_ffn_in"] = _mm(dgu, s["h2"], "bwd_ffn_in_w", ta=True, tm=1408)
    tok = put_g(l, "ffn", gf)
    dx1, gs["norm_ffn_pre"] = _norm_bwd(s["x1"], _after(sm["norm_ffn_pre"][l], tok), dh2, dx2, F32, "bwd_norm_ffn_pre")
    dymix, gs["norm_mix_post"] = _norm_bwd(s["ymix"], sm["norm_mix_post"][l], dx1, None, BF16, "bwd_norm_mix_post")
    dmerged = _mm(dymix, wts["w_out"], "bwd_out_d", tb=True)
    gb["w_out"] = _mm(s["merged"], dymix, "bwd_out_w", ta=True)
    dys, dzg, gs["b_gate"] = _gate_bwd(s["zg"], sm["b_gate"][l], s["ys"], dmerged, "gate_bwd")
    dcact = _mm(dys[0], wts["w_conv_out"], "bwd_conv_out_d", tm=2048)
    gb["w_conv_out"] = _mm(dys[0], s["cact"], "bwd_conv_out_w", ta=True)
    dzc, dconv, gs["conv_dw_bias"], gs["conv_ln_g"], gs["conv_ln_b"] = _conv_bwd(
        s["zc"], s["y"], dcact, _conv_taps(wts["conv_dw"]), sm["conv_ln_g"][l], sm["conv_ln_b"][l], "conv_bwd")
    gb["conv_dw"] = dconv[:CONV_K].T
    datt = _mm(dys[1], wts["w_att_out"], "bwd_att_out_d", tm=2048)
    gb["w_att_out"] = _mm(dys[1], s["att"], "bwd_att_out_w", ta=True)
    dqkv, rel = [], []
    for g, (_, dil) in enumerate(ATT_PATTERNS):
        trio, at = s["qkv"][g]
        if dil == 1:
            o_r, do_r, lse_r = s["att"], datt, s["lse"]
        else:
            o_r, do_r, lse_r = _to_residue_major([(s["att"], 0), (datt, 0), (s["lse"], 0)], dil, f"att_bwd_residue_d{dil}")
        dq, dk, dv, dbias = _att_bwd(*trio, bias_tabs[g], o_r, do_r, lse_r, t // dil, f"att_bwd_d{dil}", cols=at)
        dqkv.append((dq, dk, dv))
        rel.append(_bias_grad(dbias, bucket_idx[g], f"bias_grad_d{dil}")[:HEADS, :NUM_BUCKETS])
    dza = _assemble_dza(dqkv, "att_bwd_assemble")
    gs["rel_bias"] = jnp.concatenate(rel, axis=0).T
    dom = _mm(dys[2], wts["w_mem_out"], "bwd_mem_out_d", tm=2048)
    gb["w_mem_out"] = _mm(dys[2], s["om"], "bwd_mem_out_w", ta=True)
    dzq, dkv = _mem_bwd(s["zq"], s["kv"], dom, "mem_bwd")
    dkv = dkv.astype(BF16)
    gb["w_mem_kv"] = _mm(nm, dkv, "bwd_mem_kv_w", ta=True)
    dnm = _mm(dkv, wts["w_mem_kv"], "bwd_mem_kv_d", tb=True, out_dtype=F32)
    _, gs["norm_mem"] = _norm_bwd(mem, sm["norm_mem"][l], dnm, None, BF16, "bwd_norm_mem")
    w_seg = _split_w_in(wts["w_in"])
    dh, dws = None, []
    for i, (dz, nm_) in enumerate(((dzc, "conv"), (dza, "att"), (dzq, "memq"), (dzg, "gate"))):
        blk = 1152 if nm_ == "att" else 1024
        dh = _mm(dz, w_seg[i], f"bwd_in_{nm_}_d", out_dtype=F32, acc=dh, tk=blk)
        dws.append(_mm(dz, s["h"], f"bwd_in_{nm_}_w", ta=True, tm=blk))
    gb["w_in"] = jnp.concatenate(dws, axis=0)
    tok = put_g(l, "mix", gb)
    dx, gs["norm_mix_pre"] = _norm_bwd(s["x"], _after(sm["norm_mix_pre"][l], tok), dh, dx1, F32, "bwd_norm_mix_pre")
    return dx, gs


def _local_step(x, mem, target, sm, get_w, put_g, tokens=()):
    t = x.shape[0]
    bias_tabs, bucket_idx = [], []
    for g, (_, dil) in enumerate(ATT_PATTERNS):
        idx = _bucket_index(_att_tiles(t // dil)[1], dil)
        bucket_idx.append(jnp.asarray(idx))
        bias_tabs.append(_bias_table(sm["rel_bias"][:, g * HEADS:(g + 1) * HEADS], bucket_idx[g], f"bias_table_d{dil}"))
    h = _norm_plain(x, _after(sm["norm_mix_pre"][0], *tokens), "norm_first")
    saved, nms = [], []
    for l in range(DEPTH):
        nm = _norm_plain(mem, sm["norm_mem"][l], "norm_mem")
        x, h, s = _layer_fwd(l, x, h, nm, get_w, sm, bias_tabs)
        saved.append(s)
        nms.append(nm)
    loss, dx = _loss_head(x, target, "loss_head")
    gsmall = {}
    for l in reversed(range(DEPTH)):
        dx, gs = _layer_bwd(l, dx, saved[l], nms[l], mem, get_w, put_g, sm, bias_tabs, bucket_idx)
        for n, v in gs.items():
            gsmall.setdefault(n, {})[l] = v
    small = {}
    for n, _ in SMALL:
        small[n] = gsmall[n][0] + gsmall[n][1] if n == "rel_bias" else jnp.stack([gsmall[n][0], gsmall[n][1]])
    return loss, dx, small


def kernel(x, mem, rel_bias, norm_mix_pre, w_in, b_gate, conv_dw, conv_dw_bias, conv_ln_g, conv_ln_b, w_conv_out, w_att_out, norm_mem, w_mem_kv, w_mem_out, w_out, norm_mix_post, norm_ffn_pre, w_ffn_in, w_ffn_out, norm_ffn_post, loss_target, m_rel_bias, m_norm_mix_pre, m_w_in, m_b_gate, m_conv_dw, m_conv_dw_bias, m_conv_ln_g, m_conv_ln_b, m_w_conv_out, m_w_att_out, m_norm_mem, m_w_mem_kv, m_w_mem_out, m_w_out, m_norm_mix_post, m_norm_ffn_pre, m_w_ffn_in, m_w_ffn_out, m_norm_ffn_post, v_rel_bias, v_norm_mix_pre, v_w_in, v_b_gate, v_conv_dw, v_conv_dw_bias, v_conv_ln_g, v_conv_ln_b, v_w_conv_out, v_w_att_out, v_norm_mem, v_w_mem_kv, v_w_mem_out, v_w_out, v_norm_mix_post, v_norm_ffn_pre, v_w_ffn_in, v_w_ffn_out, v_norm_ffn_post):
    args = dict(locals())
    w = {n: args[n] for n in TWIN_WEIGHTS}
    m = {n: args["m_" + n] for n in TWIN_WEIGHTS}
    v = {n: args["v_" + n] for n in TWIN_WEIGHTS}
    sm = {n: w[n] for n, _ in SMALL}

    first = (0, "mix")
    keys = [(l, grp) for l in range(DEPTH) for grp in GROUPS]
    shard = {(l, grp): _pack_group(w, l, grp) for l, grp in keys}
    weights = {first: _full_weights(_all_gather(shard[first], "gather_l0_mix"), "mix")}
    ag = {k: _exchange_start(_after(shard[k], weights[first]["w_out"]), True, f"ag_start_l{k[0]}_{k[1]}")
          for k in keys if k != first}

    def get_w(l, grp, after):
        if (l, grp) not in weights:
            mine, land = _exchange_wait(ag[l, grp], after, True, f"ag_wait_l{l}_{grp}")
            weights[l, grp] = _full_weights(_own_slot(land, mine), grp)
        return weights[l, grp]

    rs, last = {}, {}

    def put_g(l, grp, grads):
        contrib = _scatter_layout(grads, grp)
        if (l, grp) == first:
            last[first] = contrib
            return None
        rs[l, grp] = _exchange_start(contrib, False, f"rs_start_l{l}_{grp}")
        return rs[l, grp][4]

    loss, dx, gsmall = _local_step(x[0], mem[0], loss_target[0], sm, get_w, put_g, [st[4] for st in ag.values()])
    loss = lax.psum(loss, ("x", "y", "c"))

    xi, yi, ci = _place()
    me = 4 * xi + 2 * yi + ci
    g_slab = {first: _reduce_scatter(last[first], "l0_mix")}
    for k, st in rs.items():
        contrib, land = _exchange_wait(st, dx, False, f"rs_wait_l{k[0]}_{k[1]}")
        own = lax.dynamic_index_in_dim(contrib, me, axis=0, keepdims=False)
        g_slab[k] = _sum_slots(_own_slot(land, own), F32, f"rs_sum_l{k[0]}_{k[1]}")
    g_layers = [{**_unpack_blocks(g_slab[l, "mix"], "mix"), **_unpack_blocks(g_slab[l, "ffn"], "ffn")} for l in range(DEPTH)]
    small_shapes = [s for _, s in SMALL]
    g_small = _unpack(_sum_slots(_all_gather(_pack([gsmall[n] for n, _ in SMALL], F32), "gather_small"), F32, "sum_small"),
                      small_shapes, F32)
    grads = {n: jnp.stack([g_layers[0][n], g_layers[1][n]]) for n, _, _ in BIG}
    grads.update({n: g_small[i] for i, (n, _) in enumerate(SMALL)})

    delta, new_m, new_v = {}, {}, {}
    for n in TWIN_WEIGHTS:
        delta[n], new_m[n], new_v[n] = _adamw(w[n], grads[n], m[n], v[n], f"adamw_{n}")

    return (loss, dx[None], *[grads[n] for n in TWIN_WEIGHTS], *[delta[n] for n in TWIN_WEIGHTS],
            *[new_m[n] for n in TWIN_WEIGHTS], *[new_v[n] for n in TWIN_WEIGHTS])
```

```python
import functools

import numpy as np
import jax
import jax.numpy as jnp
from jax import lax
from jax.experimental import pallas as pl
from jax.experimental.pallas import tpu as pltpu

F32 = jnp.float32
BF16 = jnp.bfloat16

N_DEV = 8
D = 1024
DEPTH = 2
CONV_W = 512
CONV_K = 31
CONV_PAD = 16
ATT_PATTERNS = ((128, 1), (512, 4), (2048, 16))
ATT_RADIUS = 64
HEADS = 4
HEAD_DIM = 64
ATT_GW = HEADS * HEAD_DIM
ATT_W = 3 * ATT_GW
MEM_HEADS = 4
MEM_HD = 128
MEM_W = 512
N_MEM = 256
FFN_H = 2816
NUM_BUCKETS = 32
MAX_DISTANCE = 1024
RMS_EPS = 1e-6
LN_EPS = 1e-5
NEG_INF = -1e30
SEG = (2 * CONV_W, 3 * ATT_W, MEM_W, 3 * D)
ADAM_LR, ADAM_B1, ADAM_B2, ADAM_EPS, ADAM_WD, ADAM_STEP = 0.001, 0.9, 0.999, 1e-08, 0.01, 10

VMEM_LIMIT_V7X = 48 * 1024 * 1024
MESH = pl.DeviceIdType.MESH


def _cp(n_axes):
    return pltpu.CompilerParams(dimension_semantics=("arbitrary",) * n_axes, vmem_limit_bytes=VMEM_LIMIT_V7X)


def _row(v):
    return v.reshape(1, -1)


def _mm(a, b, name, *, ta=False, tb=False, out_dtype=BF16, acc=None, tm=1024, tn=1024, tk=1024):
    kdim, m = a.shape if ta else a.shape[::-1]
    n, kb = b.shape if tb else b.shape[::-1]
    assert kb == kdim and a.dtype == BF16 and b.dtype == BF16
    tm, tn, tk = min(tm, m), min(tn, n), min(tk, kdim)
    assert m % tm == 0 and n % tn == 0 and kdim % tk == 0, (name, m, n, kdim, tm, tn, tk)
    nk = kdim // tk
    a_spec = pl.BlockSpec((tk, tm), lambda i, j, k: (k, i)) if ta else pl.BlockSpec((tm, tk), lambda i, j, k: (i, k))
    b_spec = pl.BlockSpec((tn, tk), lambda i, j, k: (j, k)) if tb else pl.BlockSpec((tk, tn), lambda i, j, k: (k, j))
    o_spec = pl.BlockSpec((tm, tn), lambda i, j, k: (i, j))
    dims = (((0 if ta else 1,), (1 if tb else 0,)), ((), ()))
    has_acc = acc is not None
    use_scratch = nk > 1 and out_dtype != F32
    assert not has_acc or out_dtype == F32

    def body(*refs):
        a_ref, b_ref = refs[0], refs[1]
        o_ref = refs[2 + has_acc]
        p = lax.dot_general(a_ref[...], b_ref[...], dims, preferred_element_type=F32)
        if nk == 1:
            if has_acc:
                p = p + refs[2][...]
            o_ref[...] = p.astype(o_ref.dtype)
            return
        k = pl.program_id(2)
        t_ref = refs[3 + has_acc] if use_scratch else o_ref

        @pl.when(k == 0)
        def _():
            t_ref[...] = (p + refs[2][...]) if has_acc else p

        @pl.when(k > 0)
        def _():
            t_ref[...] += p

        if use_scratch:
            @pl.when(k == nk - 1)
            def _():
                o_ref[...] = t_ref[...].astype(o_ref.dtype)

    return pl.pallas_call(
        body, name=name, grid=(m // tm, n // tn, nk),
        in_specs=[a_spec, b_spec] + ([o_spec] if has_acc else []), out_specs=o_spec,
        out_shape=jax.ShapeDtypeStruct((m, n), out_dtype),
        scratch_shapes=[pltpu.VMEM((tm, tn), F32)] if use_scratch else [],
        compiler_params=_cp(3),
    )(*((a, b, acc) if has_acc else (a, b)))


def _rms_val(v, g):
    return v * lax.rsqrt(jnp.mean(v * v, axis=-1, keepdims=True) + RMS_EPS) * g


def _norm_plain(v, g, name, tr=512):
    t, d = v.shape
    tr = min(tr, t)

    def body(v_ref, g_ref, o_ref):
        o_ref[...] = _rms_val(v_ref[...], g_ref[...]).astype(BF16)

    return pl.pallas_call(
        body, name=name, grid=(t // tr,),
        in_specs=[pl.BlockSpec((tr, d), lambda i: (i, 0)), pl.BlockSpec((1, d), lambda i: (0, 0))],
        out_specs=pl.BlockSpec((tr, d), lambda i: (i, 0)),
        out_shape=jax.ShapeDtypeStruct((t, d), BF16), compiler_params=_cp(1),
    )(v, _row(g))


def _norm_res(xres, v, g, g_next, name, tr=512):
    t, d = v.shape
    tr = min(tr, t)
    two = g_next is not None

    def body(*refs):
        x_ref, v_ref, g_ref = refs[:3]
        xn = x_ref[...] + _rms_val(v_ref[...], g_ref[...])
        refs[3 + two][...] = xn
        if two:
            refs[5][...] = _rms_val(xn, refs[3][...]).astype(BF16)

    row = pl.BlockSpec((tr, d), lambda i: (i, 0))
    vec = pl.BlockSpec((1, d), lambda i: (0, 0))
    outs = pl.pallas_call(
        body, name=name, grid=(t // tr,),
        in_specs=[row, row, vec] + ([vec] if two else []),
        out_specs=[row, row] if two else [row],
        out_shape=[jax.ShapeDtypeStruct((t, d), F32)] + ([jax.ShapeDtypeStruct((t, d), BF16)] if two else []),
        compiler_params=_cp(1),
    )(*((xres, v, _row(g)) + ((_row(g_next),) if two else ())))
    return (outs[0], outs[1]) if two else (outs[0], None)


def _norm_bwd(v, g, dout, dres, out_dtype, name, tr=512):
    t, d = v.shape
    tr = min(tr, t)
    nt = t // tr
    has_res = dres is not None

    def body(*refs):
        v_ref, g_ref, do_ref = refs[:3]
        dv_ref, dg_ref = refs[3 + has_res], refs[4 + has_res]
        i = pl.program_id(0)
        vv = v_ref[...]
        dy = do_ref[...].astype(F32)
        r = lax.rsqrt(jnp.mean(vv * vv, axis=-1, keepdims=True) + RMS_EPS)
        vhat = vv * r
        dvh = dy * g_ref[...]
        dv = r * (dvh - vhat * jnp.mean(dvh * vhat, axis=-1, keepdims=True))
        if has_res:
            dv = dv + refs[3][...]
        dv_ref[...] = dv.astype(dv_ref.dtype)
        part = jnp.sum(dy * vhat, axis=0, keepdims=True)

        @pl.when(i == 0)
        def _():
            dg_ref[...] = jnp.zeros_like(dg_ref)

        dg_ref[0:1, :] += part

    row = pl.BlockSpec((tr, d), lambda i: (i, 0))
    dv, dg = pl.pallas_call(
        body, name=name, grid=(nt,),
        in_specs=[row, pl.BlockSpec((1, d), lambda i: (0, 0)), row] + ([row] if has_res else []),
        out_specs=[row, pl.BlockSpec((8, d), lambda i: (0, 0))],
        out_shape=[jax.ShapeDtypeStruct((t, d), out_dtype), jax.ShapeDtypeStruct((8, d), F32)],
        compiler_params=_cp(1),
    )(*((v, _row(g), dout) + ((dres,) if has_res else ())))
    return dv, dg[0]


def _loss_head(y, target, name, tr=512):
    t, d = y.shape
    tr = min(tr, t)
    nt = t // tr

    def body(y_ref, t_ref, dy_ref, l_ref):
        i = pl.program_id(0)
        err = y_ref[...] - t_ref[...]
        dy_ref[...] = err * (1.0 / d)

        @pl.when(i == 0)
        def _():
            l_ref[...] = jnp.zeros_like(l_ref)

        l_ref[...] += jnp.sum(err * err) * (0.5 / d)

    row = pl.BlockSpec((tr, d), lambda i: (i, 0))
    dy, l = pl.pallas_call(
        body, name=name, grid=(nt,), in_specs=[row, row],
        out_specs=[row, pl.BlockSpec((8, 128), lambda i: (0, 0))],
        out_shape=[jax.ShapeDtypeStruct((t, d), F32), jax.ShapeDtypeStruct((8, 128), F32)],
        compiler_params=_cp(1),
    )(y, target)
    return l[0, 0], dy


def _sigmoid(v):
    return 1.0 / (1.0 + jnp.exp(-v))


def _halo_specs(tq, width, nt, halo):
    per = tq // halo
    last = nt * per - 1
    main = pl.BlockSpec((tq, width), lambda i: (i, 0))
    prev = pl.BlockSpec((halo, width), lambda i: (jnp.maximum(i * per - 1, 0), 0))
    nxt = pl.BlockSpec((halo, width), lambda i: (jnp.minimum((i + 1) * per, last), 0))
    return prev, main, nxt


def _glu_window(zp, zm, zn, i, nt):
    def glu(z):
        z = z.astype(F32)
        return z[:, :CONV_W] * _sigmoid(z[:, CONV_W:])

    up = jnp.where(i > 0, glu(zp), 0.0)
    un = jnp.where(i < nt - 1, glu(zn), 0.0)
    return jnp.concatenate([up, glu(zm), un], axis=0)


def _shifted(win, shift, rows):
    if shift == 0:
        return win[:rows]
    return pltpu.roll(win, win.shape[0] - shift, 0)[:rows]


def _conv_fwd(zc, w, b, ln_g, ln_b, name, tq=256):
    t = zc.shape[0]
    tq = min(tq, t)
    nt = t // tq

    def body(zp_ref, zm_ref, zn_ref, w_ref, b_ref, g_ref, bb_ref, y_ref, c_ref):
        i = pl.program_id(0)
        win = _glu_window(zp_ref[...], zm_ref[...], zn_ref[...], i, nt)
        wv = w_ref[...]
        y = jnp.zeros((tq, CONV_W), F32) + b_ref[...]
        for k in range(CONV_K):
            y = y + _shifted(win, k + 1, tq) * wv[k:k + 1, :]
        y_ref[...] = y
        mu = jnp.mean(y, axis=-1, keepdims=True)
        yc = y - mu
        ln = yc * lax.rsqrt(jnp.mean(yc * yc, axis=-1, keepdims=True) + LN_EPS) * g_ref[...] + bb_ref[...]
        c_ref[...] = (ln * _sigmoid(ln)).astype(BF16)

    vec = pl.BlockSpec((1, CONV_W), lambda i: (0, 0))
    out = pl.BlockSpec((tq, CONV_W), lambda i: (i, 0))
    return pl.pallas_call(
        body, name=name, grid=(nt,),
        in_specs=[*_halo_specs(tq, 2 * CONV_W, nt, CONV_PAD), pl.BlockSpec((32, CONV_W), lambda i: (0, 0)), vec, vec, vec],
        out_specs=[out, out],
        out_shape=[jax.ShapeDtypeStruct((t, CONV_W), F32), jax.ShapeDtypeStruct((t, CONV_W), BF16)],
        compiler_params=_cp(1),
    )(zc, zc, zc, w, _row(b), _row(ln_g), _row(ln_b))


def _conv_bwd(zc, y, dc, w, ln_g, ln_b, name, tq=256):
    t = zc.shape[0]
    tq = min(tq, t)
    nt = t // tq
    rows = tq + 2 * CONV_PAD

    def body(zp_ref, zm_ref, zn_ref, yp_ref, ym_ref, yn_ref, dp_ref, dm_ref, dn_ref, w_ref, g_ref, bb_ref,
             dz_ref, dw_ref, db_ref, dg_ref, dbb_ref):
        i = pl.program_id(0)
        uwin = _glu_window(zp_ref[...], zm_ref[...], zn_ref[...], i, nt)
        ywin = jnp.concatenate([yp_ref[...], ym_ref[...], yn_ref[...]], axis=0)
        dcw = jnp.concatenate([dp_ref[...], dm_ref[...], dn_ref[...]], axis=0).astype(F32)
        mu = jnp.mean(ywin, axis=-1, keepdims=True)
        yc = ywin - mu
        rstd = lax.rsqrt(jnp.mean(yc * yc, axis=-1, keepdims=True) + LN_EPS)
        yhat = yc * rstd
        ln = yhat * g_ref[...] + bb_ref[...]
        sg = _sigmoid(ln)
        dl = dcw * (sg * (1.0 + ln * (1.0 - sg)))
        ridx = lax.broadcasted_iota(jnp.int32, (rows, 1), 0)
        inside = ((ridx >= CONV_PAD) | (i > 0)) & ((ridx < tq + CONV_PAD) | (i < nt - 1))
        dl = jnp.where(inside, dl, 0.0)
        dyh = dl * g_ref[...]
        dy = rstd * (dyh - jnp.mean(dyh, axis=-1, keepdims=True) - yhat * jnp.mean(dyh * yhat, axis=-1, keepdims=True))
        dy = jnp.where(inside, dy, 0.0)
        main = slice(CONV_PAD, CONV_PAD + tq)
        dlm, yhm, dym = dl[main], yhat[main], dy[main]

        @pl.when(i == 0)
        def _():
            dw_ref[...] = jnp.zeros_like(dw_ref)
            db_ref[...] = jnp.zeros_like(db_ref)
            dg_ref[...] = jnp.zeros_like(dg_ref)
            dbb_ref[...] = jnp.zeros_like(dbb_ref)

        dg_ref[0:1, :] += jnp.sum(dlm * yhm, axis=0, keepdims=True)
        dbb_ref[0:1, :] += jnp.sum(dlm, axis=0, keepdims=True)
        db_ref[0:1, :] += jnp.sum(dym, axis=0, keepdims=True)
        wv = w_ref[...]
        du = jnp.zeros((tq, CONV_W), F32)
        for k in range(CONV_K):
            du = du + _shifted(dy, 2 * CONV_PAD - 1 - k, tq) * wv[k:k + 1, :]
            dw_ref[k:k + 1, :] += jnp.sum(dym * _shifted(uwin, k + 1, tq), axis=0, keepdims=True)
        zm = zm_ref[...].astype(F32)
        a, gt = zm[:, :CONV_W], zm[:, CONV_W:]
        sgt = _sigmoid(gt)
        dz_ref[:, :CONV_W] = (du * sgt).astype(BF16)
        dz_ref[:, CONV_W:] = (du * a * sgt * (1.0 - sgt)).astype(BF16)

    vec = pl.BlockSpec((1, CONV_W), lambda i: (0, 0))
    acc = pl.BlockSpec((8, CONV_W), lambda i: (0, 0))
    dz, dw, db, dg, dbb = pl.pallas_call(
        body, name=name, grid=(nt,),
        in_specs=[*_halo_specs(tq, 2 * CONV_W, nt, CONV_PAD), *_halo_specs(tq, CONV_W, nt, CONV_PAD),
                  *_halo_specs(tq, CONV_W, nt, CONV_PAD), pl.BlockSpec((32, CONV_W), lambda i: (0, 0)), vec, vec],
        out_specs=[pl.BlockSpec((tq, 2 * CONV_W), lambda i: (i, 0)), pl.BlockSpec((32, CONV_W), lambda i: (0, 0)), acc, acc, acc],
        out_shape=[jax.ShapeDtypeStruct((t, 2 * CONV_W), BF16), jax.ShapeDtypeStruct((32, CONV_W), F32),
                   jax.ShapeDtypeStruct((8, CONV_W), F32), jax.ShapeDtypeStruct((8, CONV_W), F32),
                   jax.ShapeDtypeStruct((8, CONV_W), F32)],
        compiler_params=_cp(1),
    )(zc, zc, zc, y, y, y, dc, dc, dc, w, _row(ln_g), _row(ln_b))
    return dz, dw, db[0], dg[0], dbb[0]


def _att_tiles(seq_len):
    tq = min(256, seq_len)
    return tq, min(128, tq)


def _t5_bucket_np(rel):
    nb = NUM_BUCKETS // 2
    max_exact = nb // 2
    ret = np.where(rel > 0, nb, 0)
    n = np.abs(rel)
    nf = np.maximum(n, 1).astype(np.float32)
    large = max_exact + (np.log(nf / np.float32(max_exact)) / np.float32(np.log(MAX_DISTANCE / max_exact))
                         * np.float32(nb - max_exact)).astype(np.int32)
    large = np.minimum(large, nb - 1)
    return ret + np.where(n < max_exact, n, large)


def _bucket_index(sb, dilation):
    off = np.arange(sb + 2 * ATT_RADIUS)[None, :] - ATT_RADIUS - np.arange(sb)[:, None]
    idx = _t5_bucket_np(off * dilation).astype(np.int32)
    return np.where(np.abs(off) <= ATT_RADIUS, idx, -1).astype(np.int32)


def _bias_table(tab, idx, name):
    sb, w = idx.shape

    def body(tab_ref, idx_ref, o_ref):
        ix = idx_ref[...]
        for h in range(HEADS):
            acc = jnp.full((sb, w), NEG_INF, F32)
            for b in range(NUM_BUCKETS):
                acc = jnp.where(ix == b, tab_ref[b, h], acc)
            o_ref[h] = acc

    return pl.pallas_call(
        body, name=name, out_shape=jax.ShapeDtypeStruct((HEADS, sb, w), F32),
        in_specs=[pl.BlockSpec(memory_space=pltpu.SMEM), pl.BlockSpec(memory_space=pltpu.VMEM)],
        out_specs=pl.BlockSpec(memory_space=pltpu.VMEM),
    )(tab, idx)


def _qkv_specs(tq, per, last_blk, cols, tile):
    def main(col):
        return pl.BlockSpec((tq, ATT_GW), lambda *g: (tile(*g), col))

    def prev(col):
        return pl.BlockSpec((ATT_RADIUS, ATT_GW), lambda *g: (jnp.maximum(tile(*g) * per - 1, 0), col))

    def nxt(col):
        return pl.BlockSpec((ATT_RADIUS, ATT_GW), lambda *g: (jnp.minimum((tile(*g) + 1) * per, last_blk), col))

    cq, ck, cv = cols
    return [main(cq), prev(ck), main(ck), nxt(ck), prev(cv), main(cv), nxt(cv)]


def _att_fwd(q, k, v, bias, seq_len, name, cols=(0, 0, 0)):
    t = q.shape[0]
    tq, sb = _att_tiles(seq_len)
    nbl = seq_len // tq
    nt = t // tq
    per = tq // ATT_RADIUS
    w = sb + 2 * ATT_RADIUS
    last_blk = t // ATT_RADIUS - 1

    def body(q_ref, kp_ref, kc_ref, kn_ref, vp_ref, vc_ref, vn_ref, b_ref, o_ref, l_ref):
        n = pl.program_id(0) % nbl
        kw = jnp.concatenate([kp_ref[...], kc_ref[...], kn_ref[...]], axis=0)
        vw = jnp.concatenate([vp_ref[...], vc_ref[...], vn_ref[...]], axis=0)
        kpos = n * tq - ATT_RADIUS + lax.broadcasted_iota(jnp.int32, (1, tq + 2 * ATT_RADIUS), 1)
        valid = (kpos >= 0) & (kpos < seq_len)
        for j in range(tq // sb):
            qj = q_ref[j * sb:(j + 1) * sb, :]
            kj, vj, okj = kw[j * sb:j * sb + w], vw[j * sb:j * sb + w], valid[:, j * sb:j * sb + w]
            outs, lses = [], []
            for h in range(HEADS):
                hs = slice(h * HEAD_DIM, (h + 1) * HEAD_DIM)
                s = lax.dot_general(qj[:, hs], kj[:, hs], (((1,), (1,)), ((), ())), preferred_element_type=F32)
                s = jnp.where(okj, s * (HEAD_DIM ** -0.5) + b_ref[h], NEG_INF)
                m = jnp.max(s, axis=-1, keepdims=True)
                e = jnp.exp(s - m)
                den = jnp.sum(e, axis=-1, keepdims=True)
                o = jnp.dot(e.astype(BF16), vj[:, hs], preferred_element_type=F32) / den
                outs.append(o)
                lses.append(jnp.broadcast_to(m + jnp.log(den), (sb, HEAD_DIM)))
            o_ref[j * sb:(j + 1) * sb, :] = jnp.concatenate(outs, axis=1).astype(BF16)
            l_ref[j * sb:(j + 1) * sb, :] = jnp.concatenate(lses, axis=1)

    main = pl.BlockSpec((tq, ATT_GW), lambda i: (i, 0))
    return pl.pallas_call(
        body, name=name, grid=(nt,),
        in_specs=[*_qkv_specs(tq, per, last_blk, cols, lambda i: i), pl.BlockSpec((HEADS, sb, w), lambda i: (0, 0, 0))],
        out_specs=[main, main],
        out_shape=[jax.ShapeDtypeStruct((t, ATT_GW), BF16), jax.ShapeDtypeStruct((t, ATT_GW), F32)],
        compiler_params=_cp(1),
    )(q, k, k, k, v, v, v, bias)


PERM_ROWS = 512
LANES = 128


def _spec_rm(d, tr, width, col=0):
    return pl.BlockSpec((d, tr // d, width), lambda i: (0, i, col))


def _rm_view(a, d):
    return a.reshape(d, a.shape[0] // d, a.shape[1])


def _gather_residues(scr, val, d, out_ref, col0):
    tr, w = val.shape
    for c in range(w // LANES):
        scr[...] = val[:, c * LANES:(c + 1) * LANES].astype(F32)
        for r in range(d):
            out_ref[r, :, col0 + c * LANES:col0 + (c + 1) * LANES] = scr[pl.ds(r, tr // d, stride=d), :].astype(out_ref.dtype)


def _scatter_residues(scr, ref, d):
    w = ref.shape[2]
    cols = []
    for c in range(w // LANES):
        for r in range(d):
            scr[pl.ds(r, scr.shape[0] // d, stride=d), :] = ref[r, :, c * LANES:(c + 1) * LANES].astype(F32)
        cols.append(scr[...])
    return jnp.concatenate(cols, axis=1)


def _to_residue_major(srcs, d, name, tr=PERM_ROWS):
    t = srcs[0][0].shape[0]
    n = len(srcs)

    def body(*refs):
        scr = refs[2 * n]
        for k in range(n):
            _gather_residues(scr, refs[k][...], d, refs[n + k], 0)

    outs = pl.pallas_call(
        body, name=name, grid=(t // tr,),
        in_specs=[pl.BlockSpec((tr, ATT_GW), functools.partial(lambda i, col: (i, col), col=col)) for _, col in srcs],
        out_specs=[_spec_rm(d, tr, ATT_GW)] * n,
        out_shape=[jax.ShapeDtypeStruct((d, t // d, ATT_GW), a.dtype) for a, _ in srcs],
        scratch_shapes=[pltpu.VMEM((tr, LANES), F32)], compiler_params=_cp(1),
    )(*[a for a, _ in srcs])
    return [o.reshape(t, ATT_GW) for o in outs]


def _att_combine(outs, lses, name, tr=PERM_ROWS):
    t = outs[0].shape[0]
    dils = [d for _, d in ATT_PATTERNS]

    def body(o0, o1, o2, l0, l1, l2, a_ref, lse_ref, scr):
        os_, ls_ = [], []
        for o_ref, l_ref, d in zip((o0, o1, o2), (l0, l1, l2), dils):
            if d == 1:
                os_.append(o_ref[...].astype(F32))
                ls_.append(l_ref[...])
            else:
                os_.append(_scatter_residues(scr, o_ref, d))
                ls_.append(_scatter_residues(scr, l_ref, d))
        la, lb, lc = ls_
        m = jnp.maximum(jnp.maximum(la, lb), lc)
        ea, eb, ec = jnp.exp(la - m), jnp.exp(lb - m), jnp.exp(lc - m)
        den = ea + eb + ec
        a_ref[...] = ((ea * os_[0] + eb * os_[1] + ec * os_[2]) / den).astype(BF16)
        lse_ref[...] = m + jnp.log(den)

    row = pl.BlockSpec((tr, ATT_GW), lambda i: (i, 0))
    specs = [row if d == 1 else _spec_rm(d, tr, ATT_GW) for d in dils]
    views = lambda arrs: [a if d == 1 else _rm_view(a, d) for a, d in zip(arrs, dils)]
    return pl.pallas_call(
        body, name=name, grid=(t // tr,), in_specs=specs * 2, out_specs=[row, row],
        out_shape=[jax.ShapeDtypeStruct((t, ATT_GW), BF16), jax.ShapeDtypeStruct((t, ATT_GW), F32)],
        scratch_shapes=[pltpu.VMEM((tr, LANES), F32)], compiler_params=_cp(1),
    )(*views(outs), *views(lses))


def _assemble_dza(dqkv, name, tr=PERM_ROWS):
    t = dqkv[0][0].shape[0]
    dils = [d for _, d in ATT_PATTERNS]

    def body(*refs):
        o_ref, scr = refs[9], refs[10]
        for g, d in enumerate(dils):
            for c in range(3):
                ref = refs[g * 3 + c]
                val = ref[...] if d == 1 else _scatter_residues(scr, ref, d).astype(BF16)
                o_ref[:, (c * 3 + g) * ATT_GW:(c * 3 + g + 1) * ATT_GW] = val

    row = pl.BlockSpec((tr, ATT_GW), lambda i: (i, 0))
    specs = [row if d == 1 else _spec_rm(d, tr, ATT_GW) for d in dils for _ in range(3)]
    args = [a if d == 1 else _rm_view(a, d) for trio, d in zip(dqkv, dils) for a in trio]
    return pl.pallas_call(
        body, name=name, grid=(t // tr,), in_specs=specs, out_specs=pl.BlockSpec((tr, 3 * ATT_W), lambda i: (i, 0)),
        out_shape=jax.ShapeDtypeStruct((t, 3 * ATT_W), BF16),
        scratch_shapes=[pltpu.VMEM((tr, LANES), F32)], compiler_params=_cp(1),
    )(*args)


def _att_bwd(q, k, v, bias, o, do, lse, seq_len, name, cols=(0, 0, 0)):
    t = q.shape[0]
    tq, sb = _att_tiles(seq_len)
    nbl = seq_len // tq
    n_res = t // seq_len
    per = tq // ATT_RADIUS
    w = sb + 2 * ATT_RADIUS
    last_blk = t // ATT_RADIUS - 1
    acc_rows = 2 * tq + ATT_RADIUS
    scale = HEAD_DIM ** -0.5

    def body(q_ref, kp_ref, kc_ref, kn_ref, vp_ref, vc_ref, vn_ref, b_ref, o_ref, do_ref, l_ref,
             dq_ref, dk_ref, dv_ref, db_ref, ak_ref, av_ref):
        r, n = pl.program_id(0), pl.program_id(1)

        @pl.when((r == 0) & (n == 0))
        def _():
            db_ref[...] = jnp.zeros_like(db_ref)

        @pl.when(n == 0)
        def _():
            ak_ref[...] = jnp.zeros_like(ak_ref)
            av_ref[...] = jnp.zeros_like(av_ref)

        @pl.when(n < nbl)
        def _():
            kw = jnp.concatenate([kp_ref[...], kc_ref[...], kn_ref[...]], axis=0)
            vw = jnp.concatenate([vp_ref[...], vc_ref[...], vn_ref[...]], axis=0)
            kpos = n * tq - ATT_RADIUS + lax.broadcasted_iota(jnp.int32, (1, tq + 2 * ATT_RADIUS), 1)
            valid = (kpos >= 0) & (kpos < seq_len)
            for j in range(tq // sb):
                rows = slice(j * sb, (j + 1) * sb)
                qj, doj = q_ref[rows, :], do_ref[rows, :]
                dd = doj.astype(F32) * o_ref[rows, :].astype(F32)
                lj = l_ref[rows, :]
                kj, vj, okj = kw[j * sb:j * sb + w], vw[j * sb:j * sb + w], valid[:, j * sb:j * sb + w]
                dqs, dks, dvs = [], [], []
                for h in range(HEADS):
                    hs = slice(h * HEAD_DIM, (h + 1) * HEAD_DIM)
                    s = lax.dot_general(qj[:, hs], kj[:, hs], (((1,), (1,)), ((), ())), preferred_element_type=F32)
                    s = jnp.where(okj, s * scale + b_ref[h], NEG_INF)
                    p = jnp.exp(s - lj[:, h * HEAD_DIM:h * HEAD_DIM + 1])
                    dp = lax.dot_general(doj[:, hs], vj[:, hs], (((1,), (1,)), ((), ())), preferred_element_type=F32)
                    ds = p * (dp - jnp.sum(dd[:, hs], axis=-1, keepdims=True))
                    db_ref[h] += ds
                    dsb = ds.astype(BF16)
                    dqs.append(jnp.dot(dsb, kj[:, hs], preferred_element_type=F32) * scale)
                    dks.append(lax.dot_general(dsb, qj[:, hs], (((0,), (0,)), ((), ())), preferred_element_type=F32) * scale)
                    dvs.append(lax.dot_general(p.astype(BF16), doj[:, hs], (((0,), (0,)), ((), ())), preferred_element_type=F32))
                dq_ref[rows, :] = jnp.concatenate(dqs, axis=1).astype(BF16)
                a0 = j * sb + tq - ATT_RADIUS
                ak_ref[a0:a0 + w, :] += jnp.concatenate(dks, axis=1)
                av_ref[a0:a0 + w, :] += jnp.concatenate(dvs, axis=1)

        dk_ref[...] = ak_ref[0:tq, :].astype(BF16)
        dv_ref[...] = av_ref[0:tq, :].astype(BF16)
        keep = tq + ATT_RADIUS
        nk, nv = ak_ref[tq:acc_rows, :], av_ref[tq:acc_rows, :]
        ak_ref[0:keep, :] = nk
        av_ref[0:keep, :] = nv
        ak_ref[keep:acc_rows, :] = jnp.zeros((acc_rows - keep, ATT_GW), F32)
        av_ref[keep:acc_rows, :] = jnp.zeros((acc_rows - keep, ATT_GW), F32)

    def tile(r, n):
        return r * nbl + jnp.minimum(n, nbl - 1)

    main = pl.BlockSpec((tq, ATT_GW), lambda r, n: (tile(r, n), 0))
    lag = pl.BlockSpec((tq, ATT_GW), lambda r, n: (r * nbl + jnp.maximum(n - 1, 0), 0))
    bspec = pl.BlockSpec((HEADS, sb, w), lambda r, n: (0, 0, 0))
    return pl.pallas_call(
        body, name=name, grid=(n_res, nbl + 1),
        in_specs=[*_qkv_specs(tq, per, last_blk, cols, tile), bspec, main, main, main],
        out_specs=[main, lag, lag, bspec],
        out_shape=[jax.ShapeDtypeStruct((t, ATT_GW), BF16)] * 3 + [jax.ShapeDtypeStruct((HEADS, sb, w), F32)],
        scratch_shapes=[pltpu.VMEM((acc_rows, ATT_GW), F32), pltpu.VMEM((acc_rows, ATT_GW), F32)],
        compiler_params=_cp(2),
    )(q, k, k, k, v, v, v, bias, o, do, lse)


def _bias_grad(db, idx, name):
    _, sb, w = db.shape

    def body(db_ref, idx_ref, o_ref):
        ix = idx_ref[...]
        lane = lax.broadcasted_iota(jnp.int32, (1, 128), 1)
        rows = []
        for h in range(HEADS):
            d = db_ref[h]
            acc = jnp.zeros((1, 128), F32)
            for b in range(NUM_BUCKETS):
                acc = acc + jnp.where(lane == b, jnp.sum(jnp.where(ix == b, d, 0.0)), 0.0)
            rows.append(acc)
        o_ref[...] = jnp.concatenate(rows + [jnp.zeros((8 - HEADS, 128), F32)], axis=0)

    return pl.pallas_call(
        body, name=name, out_shape=jax.ShapeDtypeStruct((8, 128), F32),
        in_specs=[pl.BlockSpec(memory_space=pltpu.VMEM), pl.BlockSpec(memory_space=pltpu.VMEM)],
        out_specs=pl.BlockSpec(memory_space=pltpu.VMEM),
    )(db, idx)


def _mem_fwd(zq, kv, name, tq=512):
    t = zq.shape[0]
    tq = min(tq, t)
    scale = MEM_HD ** -0.5

    def body(q_ref, kv_ref, o_ref):
        outs = []
        for h in range(MEM_HEADS):
            hs = slice(h * MEM_HD, (h + 1) * MEM_HD)
            kh = kv_ref[:, h * MEM_HD:(h + 1) * MEM_HD]
            vh = kv_ref[:, MEM_W + h * MEM_HD:MEM_W + (h + 1) * MEM_HD]
            s = lax.dot_general(q_ref[:, hs], kh, (((1,), (1,)), ((), ())), preferred_element_type=F32) * scale
            e = jnp.exp(s - jnp.max(s, axis=-1, keepdims=True))
            den = jnp.sum(e, axis=-1, keepdims=True)
            outs.append(jnp.dot(e.astype(BF16), vh, preferred_element_type=F32) / den)
        o_ref[...] = jnp.concatenate(outs, axis=1).astype(BF16)

    row = pl.BlockSpec((tq, MEM_W), lambda i: (i, 0))
    return pl.pallas_call(
        body, name=name, grid=(t // tq,),
        in_specs=[row, pl.BlockSpec((N_MEM, 2 * MEM_W), lambda i: (0, 0))], out_specs=row,
        out_shape=jax.ShapeDtypeStruct((t, MEM_W), BF16), compiler_params=_cp(1),
    )(zq, kv)


def _mem_bwd(zq, kv, do, name, tq=512):
    t = zq.shape[0]
    tq = min(tq, t)
    scale = MEM_HD ** -0.5

    def body(q_ref, kv_ref, do_ref, dq_ref, dkv_ref):
        @pl.when(pl.program_id(0) == 0)
        def _():
            dkv_ref[...] = jnp.zeros_like(dkv_ref)

        dqs = []
        for h in range(MEM_HEADS):
            hs = slice(h * MEM_HD, (h + 1) * MEM_HD)
            vs = slice(MEM_W + h * MEM_HD, MEM_W + (h + 1) * MEM_HD)
            qh, doh = q_ref[:, hs], do_ref[:, hs]
            kh, vh = kv_ref[:, hs], kv_ref[:, vs]
            s = lax.dot_general(qh, kh, (((1,), (1,)), ((), ())), preferred_element_type=F32) * scale
            e = jnp.exp(s - jnp.max(s, axis=-1, keepdims=True))
            p = e / jnp.sum(e, axis=-1, keepdims=True)
            dp = lax.dot_general(doh, vh, (((1,), (1,)), ((), ())), preferred_element_type=F32)
            ds = p * (dp - jnp.sum(p * dp, axis=-1, keepdims=True))
            dsb = ds.astype(BF16)
            dqs.append(jnp.dot(dsb, kh, preferred_element_type=F32) * scale)
            dkv_ref[:, hs] += lax.dot_general(dsb, qh, (((0,), (0,)), ((), ())), preferred_element_type=F32) * scale
            dkv_ref[:, vs] += lax.dot_general(p.astype(BF16), doh, (((0,), (0,)), ((), ())), preferred_element_type=F32)
        dq_ref[...] = jnp.concatenate(dqs, axis=1).astype(BF16)

    row = pl.BlockSpec((tq, MEM_W), lambda i: (i, 0))
    full = pl.BlockSpec((N_MEM, 2 * MEM_W), lambda i: (0, 0))
    return pl.pallas_call(
        body, name=name, grid=(t // tq,), in_specs=[row, full, row], out_specs=[row, full],
        out_shape=[jax.ShapeDtypeStruct((t, MEM_W), BF16), jax.ShapeDtypeStruct((N_MEM, 2 * MEM_W), F32)],
        compiler_params=_cp(1),
    )(zq, kv, do)


def _gate_fwd(zg, bg, ys, name, tr=256):
    t = zg.shape[0]
    tr = min(tr, t)

    def body(z_ref, b_ref, y0, y1, y2, o_ref):
        g = _sigmoid(z_ref[...].astype(F32) + b_ref[...])
        o_ref[...] = (g[:, :D] * y0[...].astype(F32) + g[:, D:2 * D] * y1[...].astype(F32)
                      + g[:, 2 * D:] * y2[...].astype(F32)).astype(BF16)

    row = pl.BlockSpec((tr, D), lambda i: (i, 0))
    return pl.pallas_call(
        body, name=name, grid=(t // tr,),
        in_specs=[pl.BlockSpec((tr, 3 * D), lambda i: (i, 0)), pl.BlockSpec((1, 3 * D), lambda i: (0, 0)), row, row, row],
        out_specs=row, out_shape=jax.ShapeDtypeStruct((t, D), BF16), compiler_params=_cp(1),
    )(zg, _row(bg), *ys)


def _gate_bwd(zg, bg, ys, dmerged, name, tr=256):
    t = zg.shape[0]
    tr = min(tr, t)

    def body(z_ref, b_ref, y0, y1, y2, dm_ref, d0, d1, d2, dz_ref, db_ref):
        @pl.when(pl.program_id(0) == 0)
        def _():
            db_ref[...] = jnp.zeros_like(db_ref)

        g = _sigmoid(z_ref[...].astype(F32) + b_ref[...])
        dm = dm_ref[...].astype(F32)
        for i, (y_ref, d_ref) in enumerate(((y0, d0), (y1, d1), (y2, d2))):
            gi = g[:, i * D:(i + 1) * D]
            d_ref[...] = (dm * gi).astype(BF16)
            dz = dm * y_ref[...].astype(F32) * gi * (1.0 - gi)
            dz_ref[:, i * D:(i + 1) * D] = dz.astype(BF16)
            db_ref[0:1, i * D:(i + 1) * D] += jnp.sum(dz, axis=0, keepdims=True)

    row = pl.BlockSpec((tr, D), lambda i: (i, 0))
    wide = pl.BlockSpec((tr, 3 * D), lambda i: (i, 0))
    d0, d1, d2, dz, db = pl.pallas_call(
        body, name=name, grid=(t // tr,),
        in_specs=[wide, pl.BlockSpec((1, 3 * D), lambda i: (0, 0)), row, row, row, row],
        out_specs=[row, row, row, wide, pl.BlockSpec((8, 3 * D), lambda i: (0, 0))],
        out_shape=[jax.ShapeDtypeStruct((t, D), BF16)] * 3 + [jax.ShapeDtypeStruct((t, 3 * D), BF16),
                                                              jax.ShapeDtypeStruct((8, 3 * D), F32)],
        compiler_params=_cp(1),
    )(zg, _row(bg), *ys, dmerged)
    return (d0, d1, d2), dz, db[0]


def _swiglu_fwd(gu, name, tr=256):
    t = gu.shape[0]
    tr = min(tr, t)

    def body(gu_ref, o_ref):
        g = gu_ref[:, :FFN_H].astype(F32)
        o_ref[...] = (g * _sigmoid(g) * gu_ref[:, FFN_H:].astype(F32)).astype(BF16)

    return pl.pallas_call(
        body, name=name, grid=(t // tr,), in_specs=[pl.BlockSpec((tr, 2 * FFN_H), lambda i: (i, 0))],
        out_specs=pl.BlockSpec((tr, FFN_H), lambda i: (i, 0)),
        out_shape=jax.ShapeDtypeStruct((t, FFN_H), BF16), compiler_params=_cp(1),
    )(gu)


def _swiglu_bwd(gu, dact, name, tr=256):
    t = gu.shape[0]
    tr = min(tr, t)

    def body(gu_ref, da_ref, o_ref):
        g = gu_ref[:, :FFN_H].astype(F32)
        u = gu_ref[:, FFN_H:].astype(F32)
        da = da_ref[...].astype(F32)
        sg = _sigmoid(g)
        o_ref[:, :FFN_H] = (da * u * (sg * (1.0 + g * (1.0 - sg)))).astype(BF16)
        o_ref[:, FFN_H:] = (da * g * sg).astype(BF16)

    wide = pl.BlockSpec((tr, 2 * FFN_H), lambda i: (i, 0))
    return pl.pallas_call(
        body, name=name, grid=(t // tr,), in_specs=[wide, pl.BlockSpec((tr, FFN_H), lambda i: (i, 0))],
        out_specs=wide, out_shape=jax.ShapeDtypeStruct((t, 2 * FFN_H), BF16), compiler_params=_cp(1),
    )(gu, dact)


def _adamw(w, g, m, v, name):
    shape = w.shape
    w, g, m, v = (a.reshape(-1, shape[-1]) for a in (w, g, m, v))
    r, d = w.shape
    tr = next((c for c in (512, 352, 256, 128, 64, 32, 16, 8) if r % c == 0), r)

    def body(w_ref, g_ref, m_ref, v_ref, d_ref, nm_ref, nv_ref):
        gg = g_ref[...]
        m2 = ADAM_B1 * m_ref[...] + (1.0 - ADAM_B1) * gg
        v2 = ADAM_B2 * v_ref[...] + (1.0 - ADAM_B2) * (gg * gg)
        m_hat = m2 / (1.0 - ADAM_B1 ** ADAM_STEP)
        v_hat = v2 / (1.0 - ADAM_B2 ** ADAM_STEP)
        d_ref[...] = -ADAM_LR * (m_hat / (jnp.sqrt(v_hat) + ADAM_EPS) + ADAM_WD * w_ref[...])
        nm_ref[...] = m2
        nv_ref[...] = v2

    row = pl.BlockSpec((tr, d), lambda i: (i, 0))
    outs = pl.pallas_call(
        body, name=name, grid=(r // tr,), in_specs=[row] * 4, out_specs=[row] * 3,
        out_shape=[jax.ShapeDtypeStruct((r, d), F32)] * 3, compiler_params=_cp(1),
    )(w, g, m, v)
    return [o.reshape(shape) for o in outs]


def _slab_tile(rows):
    return next((c for c in range(min(rows, 512), 15, -16) if rows % c == 0), rows)


def _sum_slots(parts, out_dtype, name):
    n, r, d = parts.shape
    tr = _slab_tile(r)

    def body(p_ref, o_ref):
        acc = p_ref[0].astype(F32)
        for s in range(1, n):
            acc = acc + p_ref[s].astype(F32)
        o_ref[...] = acc.astype(o_ref.dtype)

    return pl.pallas_call(
        body, name=name, grid=(r // tr,), in_specs=[pl.BlockSpec((n, tr, d), lambda i: (0, i, 0))],
        out_specs=pl.BlockSpec((tr, d), lambda i: (i, 0)),
        out_shape=jax.ShapeDtypeStruct((r, d), out_dtype), compiler_params=_cp(1),
    )(parts)


def _place():
    return lax.axis_index("x"), lax.axis_index("y"), lax.axis_index("c")


def _all_gather(shard, name):
    r, d = shard.shape

    def body(x_ref, out_ref, send_sems, recv_sems, local_sem):
        x, y, c = _place()
        me, sibling = (x, y, c), (x, y, 1 - c)
        chips = [(1 - x, y), (x, 1 - y), (1 - x, 1 - y)]

        def slot(px, py, pc):
            return out_ref.at[4 * px + 2 * py + pc]

        def copy(k, block, to, src=None):
            return pltpu.make_async_remote_copy(
                src_ref=slot(*block) if src is None else src, dst_ref=slot(*block),
                send_sem=send_sems.at[k], recv_sem=recv_sems.at[k], device_id=to, device_id_type=MESH)

        mine = pltpu.make_async_copy(x_ref, slot(*me), local_sem)
        mine.start()
        first = [copy(0, me, sibling, src=x_ref)]
        first += [copy(1 + j, me, (*chip, c), src=x_ref) for j, chip in enumerate(chips)]
        for cp in first:
            cp.start()
        passed = [copy(4 + j, (*chip, c), sibling) for j, chip in enumerate(chips)]
        for j, chip in enumerate(chips):
            copy(1 + j, (*chip, c), me).wait_recv()
            passed[j].start()
        copy(0, sibling, me).wait_recv()
        for j, chip in enumerate(chips):
            copy(4 + j, (*chip, 1 - c), me).wait_recv()
        for cp in first + passed:
            cp.wait_send()
        mine.wait()

    return pl.pallas_call(
        body, name=name, out_shape=jax.ShapeDtypeStruct((N_DEV, r, d), shard.dtype),
        in_specs=[pl.BlockSpec(memory_space=pl.ANY)], out_specs=pl.BlockSpec(memory_space=pl.ANY),
        scratch_shapes=[pltpu.SemaphoreType.DMA((7,)), pltpu.SemaphoreType.DMA((7,)), pltpu.SemaphoreType.DMA],
    )(shard)


HBM_SPEC = pl.BlockSpec(memory_space=pltpu.HBM)
SEM_SPEC = pl.BlockSpec(memory_space=pltpu.SEMAPHORE)
SPLIT_PARAMS = pltpu.CompilerParams(has_side_effects=pltpu.SideEffectType.DATAFLOW_SIDE_EFFECTING)


def _peers():
    x, y, c = _place()
    flip = lambda v, bit: 1 - v if bit else v
    return 4 * x + 2 * y + c, [((flip(x, k >> 2 & 1), flip(y, k >> 1 & 1), flip(c, k & 1))) for k in range(1, N_DEV)]


def _exchange_start(src, all_gather, name):
    r, d = src.shape[-2:]

    def body(src_ref, land_ref, send_sems, recv_sems, src_thru, land_thru, token):
        me, peers = _peers()
        for k, (px, py, pc) in enumerate(peers):
            part = src_ref if all_gather else src_ref.at[4 * px + 2 * py + pc]
            pltpu.make_async_remote_copy(src_ref=part, dst_ref=land_ref.at[me], send_sem=send_sems.at[k],
                                         recv_sem=recv_sems.at[k], device_id=(px, py, pc), device_id_type=MESH).start()
        token[...] = jnp.zeros_like(token)

    land = lax.empty((N_DEV, r, d), src.dtype)
    return pl.pallas_call(
        body, name=name,
        out_shape=(pltpu.SemaphoreType.DMA((N_DEV - 1,)), pltpu.SemaphoreType.DMA((N_DEV - 1,)), pltpu.HBM(src.shape, src.dtype),
                   pltpu.HBM(land.shape, land.dtype), jax.ShapeDtypeStruct((8, LANES), F32)),
        in_specs=(HBM_SPEC, HBM_SPEC), out_specs=(SEM_SPEC, SEM_SPEC, HBM_SPEC, HBM_SPEC, pl.BlockSpec(memory_space=pltpu.VMEM)),
        input_output_aliases={0: 2, 1: 3}, compiler_params=SPLIT_PARAMS,
    )(pltpu.with_memory_space_constraint(src, pltpu.HBM), pltpu.with_memory_space_constraint(land, pltpu.HBM))


def _exchange_wait(started, after, all_gather, name):
    send_sems, recv_sems, src_thru, land_thru, _ = started

    def body(src_ref, land_ref, send_sems, recv_sems, after_ref, src_out, land_out):
        me, peers = _peers()
        for k, (px, py, pc) in enumerate(peers):
            part = src_ref if all_gather else src_ref.at[4 * px + 2 * py + pc]
            cp = pltpu.make_async_remote_copy(src_ref=part, dst_ref=land_ref.at[4 * px + 2 * py + pc], send_sem=send_sems.at[k],
                                              recv_sem=recv_sems.at[k], device_id=(px, py, pc), device_id_type=MESH)
            cp.wait_send()
            cp.wait_recv()

    return pl.pallas_call(
        body, name=name,
        out_shape=(pltpu.HBM(src_thru.shape, src_thru.dtype), pltpu.HBM(land_thru.shape, land_thru.dtype)),
        in_specs=(HBM_SPEC, HBM_SPEC, SEM_SPEC, SEM_SPEC, pl.BlockSpec(memory_space=pl.ANY)), out_specs=(HBM_SPEC, HBM_SPEC),
        input_output_aliases={0: 0, 1: 1}, compiler_params=SPLIT_PARAMS,
    )(src_thru, land_thru, send_sems, recv_sems, after)


def _own_slot(land, mine):
    x, y, c = _place()
    return lax.dynamic_update_slice(land, mine[None], (4 * x + 2 * y + c, 0, 0))


BIG = (("w_in", (D, 864), 1), ("w_conv_out", (CONV_W, 128), 1), ("w_att_out", (ATT_GW, 128), 1), ("w_mem_kv", (128, D), 0),
       ("w_mem_out", (MEM_W, 128), 1), ("w_out", (128, D), 0), ("w_ffn_in", (D, 704), 1), ("w_ffn_out", (352, D), 0),
       ("conv_dw", (CONV_K, 64), 1))
SMALL = (("rel_bias", (NUM_BUCKETS, 12)), ("norm_mix_pre", (DEPTH, D)), ("b_gate", (DEPTH, 3 * D)),
         ("conv_dw_bias", (DEPTH, CONV_W)), ("conv_ln_g", (DEPTH, CONV_W)), ("conv_ln_b", (DEPTH, CONV_W)),
         ("norm_mem", (DEPTH, D)), ("norm_mix_post", (DEPTH, D)), ("norm_ffn_pre", (DEPTH, D)), ("norm_ffn_post", (DEPTH, D)))
TWIN_WEIGHTS = ("rel_bias", "norm_mix_pre", "w_in", "b_gate", "conv_dw", "conv_dw_bias", "conv_ln_g", "conv_ln_b", "w_conv_out",
                "w_att_out", "norm_mem", "w_mem_kv", "w_mem_out", "w_out", "norm_mix_post", "norm_ffn_pre", "w_ffn_in",
                "w_ffn_out", "norm_ffn_post")


BIG_INFO = {n: (s, a) for n, s, a in BIG}
GROUPS = {"mix": ("w_in", "w_conv_out", "w_att_out", "w_mem_kv", "w_mem_out", "w_out", "conv_dw"), "ffn": ("w_ffn_in", "w_ffn_out")}
TILE_BYTES_PER_LANE = 32


def _rows_of(shape, dtype):
    tile = TILE_BYTES_PER_LANE // jnp.dtype(dtype).itemsize
    return -(-int(np.prod(shape)) // (D * tile)) * tile


def _row_form(shape, axis):
    return tuple(shape) if axis == 0 else (shape[1], shape[0])


def _as_slab_rows(p, lead=()):
    shape = p.shape[len(lead):]
    rows = _rows_of(shape, p.dtype)
    if shape == (rows, D):
        return p
    n = int(np.prod(shape))
    p = jnp.pad(p.reshape(*lead, n), [(0, 0)] * len(lead) + [(0, rows * D - n)])
    return p.reshape(*lead, rows, D)


def _from_slab_rows(rows, shape, lead=()):
    if rows.shape[len(lead):] == tuple(shape):
        return rows
    n = int(np.prod(shape))
    return rows.reshape(*lead, -1)[..., :n].reshape(*lead, *shape)


def _pack(pieces, dtype):
    return jnp.concatenate([_as_slab_rows(p.astype(dtype)) for p in pieces], axis=0)


def _unpack(slab, shapes, packed_as):
    out, r0 = [], 0
    for s in shapes:
        rows = _rows_of(s, packed_as)
        out.append(_from_slab_rows(slab[r0:r0 + rows], s))
        r0 += rows
    return out


def _pack_group(blocks, layer, group):
    return _pack([blocks[n][layer] if BIG_INFO[n][1] == 0 else blocks[n][layer].T for n in GROUPS[group]], BF16)


def _full_weights(gathered, group):
    out, r0 = {}, 0
    for name in GROUPS[group]:
        r, k = _row_form(*BIG_INFO[name])
        rows = _rows_of((r, k), BF16)
        out[name] = _from_slab_rows(gathered[:, r0:r0 + rows], (r, k), lead=(N_DEV,)).reshape(N_DEV * r, k)
        r0 += rows
    return out


def _scatter_layout(grads, group):
    parts = []
    for name in GROUPS[group]:
        r, k = _row_form(*BIG_INFO[name])
        parts.append(_as_slab_rows(grads[name].astype(BF16).reshape(N_DEV, r, k), lead=(N_DEV,)))
    return jnp.concatenate(parts, axis=1)


def _unpack_blocks(slab, group):
    out = _unpack(slab, [_row_form(*BIG_INFO[n]) for n in GROUPS[group]], BF16)
    return {n: b if BIG_INFO[n][1] == 0 else b.T for n, b in zip(GROUPS[group], out)}


def _split_w_in(w_in):
    edges = np.cumsum((0,) + SEG)
    return [w_in[edges[i]:edges[i + 1]] for i in range(4)]


def _conv_taps(conv_dw):
    return jnp.pad(conv_dw.astype(F32).T, ((0, 32 - CONV_K), (0, 0)))


def _layer_fwd(l, x, h, nm, get_w, sm, bias_tabs):
    t = x.shape[0]
    wts = get_w(l, "mix", h)
    w_seg = _split_w_in(wts["w_in"])
    zc = _mm(h, w_seg[0], "fwd_in_conv", tb=True)
    za = _mm(h, w_seg[1], "fwd_in_att", tb=True, tn=1152)
    zq = _mm(h, w_seg[2], "fwd_in_memq", tb=True)
    zg = _mm(h, w_seg[3], "fwd_in_gate", tb=True)
    y, cact = _conv_fwd(zc, _conv_taps(wts["conv_dw"]), sm["conv_dw_bias"][l], sm["conv_ln_g"][l], sm["conv_ln_b"][l],
                        "conv_fwd")
    qkv, outs, lses = [], [], []
    for g, (_, dil) in enumerate(ATT_PATTERNS):
        cols = tuple(i * 3 + g for i in range(3))
        if dil == 1:
            trio, at = (za, za, za), cols
        else:
            trio, at = _to_residue_major([(za, c) for c in cols], dil, f"qkv_residue_d{dil}"), (0, 0, 0)
        o, ls = _att_fwd(*trio, bias_tabs[g], t // dil, f"att_fwd_d{dil}", cols=at)
        qkv.append((trio, at))
        outs.append(o)
        lses.append(ls)
    att, lse = _att_combine(outs, lses, "att_combine")
    kv = _mm(nm, wts["w_mem_kv"], "fwd_mem_kv")
    om = _mem_fwd(zq, kv, "mem_fwd")
    ys = (_mm(cact, wts["w_conv_out"], "fwd_conv_out", tb=True, tm=2048),
          _mm(att, wts["w_att_out"], "fwd_att_out", tb=True, tm=2048),
          _mm(om, wts["w_mem_out"], "fwd_mem_out", tb=True, tm=2048))
    merged = _gate_fwd(zg, sm["b_gate"][l], ys, "gate_fwd")
    ymix = _mm(merged, wts["w_out"], "fwd_out", out_dtype=F32)
    x1, h2 = _norm_res(x, ymix, sm["norm_mix_post"][l], sm["norm_ffn_pre"][l], "norm_mix_post")
    wff = get_w(l, "ffn", h2)
    gu = _mm(h2, wff["w_ffn_in"], "fwd_ffn_in", tb=True, tn=1408)
    act = _swiglu_fwd(gu, "swiglu_fwd")
    f = _mm(act, wff["w_ffn_out"], "fwd_ffn_out", out_dtype=F32, tk=1408)
    g_next = sm["norm_mix_pre"][l + 1] if l + 1 < DEPTH else None
    x2, h_next = _norm_res(x1, f, sm["norm_ffn_post"][l], g_next, "norm_ffn_post" if g_next is not None else "norm_ffn_last")
    saved = dict(x=x, h=h, zc=zc, zq=zq, zg=zg, y=y, cact=cact, qkv=qkv, att=att, lse=lse, kv=kv, om=om, ys=ys,
                 merged=merged, ymix=ymix, x1=x1, h2=h2, gu=gu, act=act, f=f)
    return x2, h_next, saved


def _after(value, *tokens):
    for t in tokens:
        if t is not None:
            value = value + t[0, 0].astype(value.dtype)
    return value


def _layer_bwd(l, dx2, s, nm, mem, get_w, put_g, sm, bias_tabs, bucket_idx):
    t = dx2.shape[0]
    wts, wff = get_w(l, "mix", None), get_w(l, "ffn", None)
    gb, gf, gs = {}, {}, {}
    df, gs["norm_ffn_post"] = _norm_bwd(s["f"], sm["norm_ffn_post"][l], dx2, None, BF16, "bwd_norm_ffn_post")
    dact = _mm(df, wff["w_ffn_out"], "bwd_ffn_out_d", tb=True, tn=1408)
    gf["w_ffn_out"] = _mm(s["act"], df, "bwd_ffn_out_w", ta=True, tm=1408)
    dgu = _swiglu_bwd(s["gu"], dact, "swiglu_bwd")
    dh2 = _mm(dgu, wff["w_ffn_in"], "bwd_ffn_in_d", out_dtype=F32, tk=1408)
    gf["w_ffn_in"] = _mm(dgu, s["h2"], "bwd_ffn_in_w", ta=True, tm=1408)
    tok = put_g(l, "ffn", gf)
    dx1, gs["norm_ffn_pre"] = _norm_bwd(s["x1"], _after(sm["norm_ffn_pre"][l], tok), dh2, dx2, F32, "bwd_norm_ffn_pre")
    dymix, gs["norm_mix_post"] = _norm_bwd(s["ymix"], sm["norm_mix_post"][l], dx1, None, BF16, "bwd_norm_mix_post")
    dmerged = _mm(dymix, wts["w_out"], "bwd_out_d", tb=True)
    gb["w_out"] = _mm(s["merged"], dymix, "bwd_out_w", ta=True)
    dys, dzg, gs["b_gate"] = _gate_bwd(s["zg"], sm["b_gate"][l], s["ys"], dmerged, "gate_bwd")
    dcact = _mm(dys[0], wts["w_conv_out"], "bwd_conv_out_d", tm=2048)
    gb["w_conv_out"] = _mm(dys[0], s["cact"], "bwd_conv_out_w", ta=True)
    dzc, dconv, gs["conv_dw_bias"], gs["conv_ln_g"], gs["conv_ln_b"] = _conv_bwd(
        s["zc"], s["y"], dcact, _conv_taps(wts["conv_dw"]), sm["conv_ln_g"][l], sm["conv_ln_b"][l], "conv_bwd")
    gb["conv_dw"] = dconv[:CONV_K].T
    datt = _mm(dys[1], wts["w_att_out"], "bwd_att_out_d", tm=2048)
    gb["w_att_out"] = _mm(dys[1], s["att"], "bwd_att_out_w", ta=True)
    dqkv, rel = [], []
    for g, (_, dil) in enumerate(ATT_PATTERNS):
        trio, at = s["qkv"][g]
        if dil == 1:
            o_r, do_r, lse_r = s["att"], datt, s["lse"]
        else:
            o_r, do_r, lse_r = _to_residue_major([(s["att"], 0), (datt, 0), (s["lse"], 0)], dil, f"att_bwd_residue_d{dil}")
        dq, dk, dv, dbias = _att_bwd(*trio, bias_tabs[g], o_r, do_r, lse_r, t // dil, f"att_bwd_d{dil}", cols=at)
        dqkv.append((dq, dk, dv))
        rel.append(_bias_grad(dbias, bucket_idx[g], f"bias_grad_d{dil}")[:HEADS, :NUM_BUCKETS])
    dza = _assemble_dza(dqkv, "att_bwd_assemble")
    gs["rel_bias"] = jnp.concatenate(rel, axis=0).T
    dom = _mm(dys[2], wts["w_mem_out"], "bwd_mem_out_d", tm=2048)
    gb["w_mem_out"] = _mm(dys[2], s["om"], "bwd_mem_out_w", ta=True)
    dzq, dkv = _mem_bwd(s["zq"], s["kv"], dom, "mem_bwd")
    dkv = dkv.astype(BF16)
    gb["w_mem_kv"] = _mm(nm, dkv, "bwd_mem_kv_w", ta=True)
    dnm = _mm(dkv, wts["w_mem_kv"], "bwd_mem_kv_d", tb=True, out_dtype=F32)
    _, gs["norm_mem"] = _norm_bwd(mem, sm["norm_mem"][l], dnm, None, BF16, "bwd_norm_mem")
    segs = ((dzc, "conv", 1024), (dza, "att", 1152), (dzq, "memq", 1024), (dzg, "gate", 1024))
    gb["w_in"] = jnp.concatenate([_mm(dz, s["h"], f"bwd_in_{nm_}_w", ta=True, tm=blk) for dz, nm_, blk in segs], axis=0)
    tok = put_g(l, "mix", gb)
    w_seg = _split_w_in(wts["w_in"])
    dh = None
    for i, (dz, nm_, blk) in enumerate(segs):
        dh = _mm(dz, _after(w_seg[i], tok) if i == 0 else w_seg[i], f"bwd_in_{nm_}_d", out_dtype=F32, acc=dh, tk=blk)
    dx, gs["norm_mix_pre"] = _norm_bwd(s["x"], sm["norm_mix_pre"][l], dh, dx1, F32, "bwd_norm_mix_pre")
    return dx, gs


def _local_step(x, mem, target, sm, get_w, put_g, tokens=()):
    t = x.shape[0]
    bias_tabs, bucket_idx = [], []
    for g, (_, dil) in enumerate(ATT_PATTERNS):
        idx = _bucket_index(_att_tiles(t // dil)[1], dil)
        bucket_idx.append(jnp.asarray(idx))
        bias_tabs.append(_bias_table(sm["rel_bias"][:, g * HEADS:(g + 1) * HEADS], bucket_idx[g], f"bias_table_d{dil}"))
    h = _norm_plain(x, _after(sm["norm_mix_pre"][0], *tokens), "norm_first")
    saved, nms = [], []
    for l in range(DEPTH):
        nm = _norm_plain(mem, sm["norm_mem"][l], "norm_mem")
        x, h, s = _layer_fwd(l, x, h, nm, get_w, sm, bias_tabs)
        saved.append(s)
        nms.append(nm)
    loss, dx = _loss_head(x, target, "loss_head")
    gsmall = {}
    for l in reversed(range(DEPTH)):
        dx, gs = _layer_bwd(l, dx, saved[l], nms[l], mem, get_w, put_g, sm, bias_tabs, bucket_idx)
        for n, v in gs.items():
            gsmall.setdefault(n, {})[l] = v
    small = {}
    for n, _ in SMALL:
        small[n] = gsmall[n][0] + gsmall[n][1] if n == "rel_bias" else jnp.stack([gsmall[n][0], gsmall[n][1]])
    return loss, dx, small


def kernel(x, mem, rel_bias, norm_mix_pre, w_in, b_gate, conv_dw, conv_dw_bias, conv_ln_g, conv_ln_b, w_conv_out, w_att_out, norm_mem, w_mem_kv, w_mem_out, w_out, norm_mix_post, norm_ffn_pre, w_ffn_in, w_ffn_out, norm_ffn_post, loss_target, m_rel_bias, m_norm_mix_pre, m_w_in, m_b_gate, m_conv_dw, m_conv_dw_bias, m_conv_ln_g, m_conv_ln_b, m_w_conv_out, m_w_att_out, m_norm_mem, m_w_mem_kv, m_w_mem_out, m_w_out, m_norm_mix_post, m_norm_ffn_pre, m_w_ffn_in, m_w_ffn_out, m_norm_ffn_post, v_rel_bias, v_norm_mix_pre, v_w_in, v_b_gate, v_conv_dw, v_conv_dw_bias, v_conv_ln_g, v_conv_ln_b, v_w_conv_out, v_w_att_out, v_norm_mem, v_w_mem_kv, v_w_mem_out, v_w_out, v_norm_mix_post, v_norm_ffn_pre, v_w_ffn_in, v_w_ffn_out, v_norm_ffn_post):
    args = dict(locals())
    w = {n: args[n] for n in TWIN_WEIGHTS}
    m = {n: args["m_" + n] for n in TWIN_WEIGHTS}
    v = {n: args["v_" + n] for n in TWIN_WEIGHTS}
    sm = {n: w[n] for n, _ in SMALL}

    first = (0, "mix")
    keys = [(l, grp) for l in range(DEPTH) for grp in GROUPS]
    shard = {(l, grp): _pack_group(w, l, grp) for l, grp in keys}
    weights = {first: _full_weights(_all_gather(shard[first], "gather_l0_mix"), "mix")}
    settled = weights[first]["w_out"][:1, :1] * 0
    ag = {k: _exchange_start(_after(shard[k], settled), True, f"ag_start_l{k[0]}_{k[1]}") for k in keys if k != first}

    def get_w(l, grp, after):
        if (l, grp) not in weights:
            mine, land = _exchange_wait(ag[l, grp], after, True, f"ag_wait_l{l}_{grp}")
            weights[l, grp] = _full_weights(_own_slot(land, mine), grp)
        return weights[l, grp]

    rs = {}

    def put_g(l, grp, grads):
        rs[l, grp] = _exchange_start(_scatter_layout(grads, grp), False, f"rs_start_l{l}_{grp}")
        return rs[l, grp][4]

    loss, dx, gsmall = _local_step(x[0], mem[0], loss_target[0], sm, get_w, put_g, [st[4] for st in ag.values()])
    loss = lax.psum(loss, ("x", "y", "c"))

    xi, yi, ci = _place()
    me = 4 * xi + 2 * yi + ci
    g_slab = {}
    for k, st in rs.items():
        contrib, land = _exchange_wait(st, dx, False, f"rs_wait_l{k[0]}_{k[1]}")
        own = lax.dynamic_index_in_dim(contrib, me, axis=0, keepdims=False)
        g_slab[k] = _sum_slots(_own_slot(land, own), F32, f"rs_sum_l{k[0]}_{k[1]}")
    g_layers = [{**_unpack_blocks(g_slab[l, "mix"], "mix"), **_unpack_blocks(g_slab[l, "ffn"], "ffn")} for l in range(DEPTH)]
    small_shapes = [s for _, s in SMALL]
    g_small = _unpack(_sum_slots(_all_gather(_pack([gsmall[n] for n, _ in SMALL], F32), "gather_small"), F32, "sum_small"),
                      small_shapes, F32)
    grads = {n: jnp.stack([g_layers[0][n], g_layers[1][n]]) for n, _, _ in BIG}
    grads.update({n: g_small[i] for i, (n, _) in enumerate(SMALL)})

    delta, new_m, new_v = {}, {}, {}
    for n in TWIN_WEIGHTS:
        delta[n], new_m[n], new_v[n] = _adamw(w[n], grads[n], m[n], v[n], f"adamw_{n}")

    return (loss, dx[None], *[grads[n] for n in TWIN_WEIGHTS], *[delta[n] for n in TWIN_WEIGHTS],
            *[new_m[n] for n in TWIN_WEIGHTS], *[new_v[n] for n in TWIN_WEIGHTS])
```

```python
import functools

import numpy as np
import jax
import jax.numpy as jnp
from jax import lax
from jax.experimental import pallas as pl
from jax.experimental.pallas import tpu as pltpu

F32 = jnp.float32
BF16 = jnp.bfloat16

N_DEV = 8
D = 1024
DEPTH = 2
CONV_W = 512
CONV_K = 31
CONV_PAD = 16
ATT_PATTERNS = ((128, 1), (512, 4), (2048, 16))
ATT_RADIUS = 64
HEADS = 4
HEAD_DIM = 64
ATT_GW = HEADS * HEAD_DIM
ATT_W = 3 * ATT_GW
MEM_HEADS = 4
MEM_HD = 128
MEM_W = 512
N_MEM = 256
FFN_H = 2816
NUM_BUCKETS = 32
MAX_DISTANCE = 1024
RMS_EPS = 1e-6
LN_EPS = 1e-5
NEG_INF = -1e30
SEG = (2 * CONV_W, 3 * ATT_W, MEM_W, 3 * D)
ADAM_LR, ADAM_B1, ADAM_B2, ADAM_EPS, ADAM_WD, ADAM_STEP = 0.001, 0.9, 0.999, 1e-08, 0.01, 10

VMEM_LIMIT_V7X = 48 * 1024 * 1024
MESH = pl.DeviceIdType.MESH


def _cp(n_axes):
    return pltpu.CompilerParams(dimension_semantics=("arbitrary",) * n_axes, vmem_limit_bytes=VMEM_LIMIT_V7X)


def _row(v):
    return v.reshape(1, -1)


def _mm(a, b, name, *, ta=False, tb=False, out_dtype=BF16, tm=2048, tn=1024, tk=2048):
    kdim, m = a.shape if ta else a.shape[::-1]
    n, kb = b.shape if tb else b.shape[::-1]
    assert kb == kdim and a.dtype == BF16 and b.dtype == BF16
    tm, tn, tk = min(tm, m), min(tn, n), min(tk, kdim)
    assert m % tm == 0 and n % tn == 0 and kdim % tk == 0, (name, m, n, kdim, tm, tn, tk)
    nk = kdim // tk
    a_spec = pl.BlockSpec((tk, tm), lambda i, j, k: (k, i)) if ta else pl.BlockSpec((tm, tk), lambda i, j, k: (i, k))
    b_spec = pl.BlockSpec((tn, tk), lambda i, j, k: (j, k)) if tb else pl.BlockSpec((tk, tn), lambda i, j, k: (k, j))
    o_spec = pl.BlockSpec((tm, tn), lambda i, j, k: (i, j))
    dims = (((0 if ta else 1,), (1 if tb else 0,)), ((), ()))
    use_scratch = nk > 1 and out_dtype != F32

    def body(*refs):
        a_ref, b_ref, o_ref = refs[:3]
        p = lax.dot_general(a_ref[...], b_ref[...], dims, preferred_element_type=F32)
        if nk == 1:
            o_ref[...] = p.astype(o_ref.dtype)
            return
        k = pl.program_id(2)
        t_ref = refs[3] if use_scratch else o_ref

        @pl.when(k == 0)
        def _():
            t_ref[...] = p

        @pl.when(k > 0)
        def _():
            t_ref[...] += p

        if use_scratch:
            @pl.when(k == nk - 1)
            def _():
                o_ref[...] = t_ref[...].astype(o_ref.dtype)

    return pl.pallas_call(
        body, name=name, grid=(m // tm, n // tn, nk), in_specs=[a_spec, b_spec], out_specs=o_spec,
        out_shape=jax.ShapeDtypeStruct((m, n), out_dtype),
        scratch_shapes=[pltpu.VMEM((tm, tn), F32)] if use_scratch else [],
        compiler_params=_cp(3),
    )(a, b)


def _rms_bwd_val(v, g, dy):
    r = lax.rsqrt(jnp.mean(v * v, axis=-1, keepdims=True) + RMS_EPS)
    vhat = v * r
    dvh = dy * g
    dv = r * (dvh - vhat * jnp.mean(dvh * vhat, axis=-1, keepdims=True))
    return dv, jnp.sum(dy * vhat, axis=0, keepdims=True)


def _mm_sum_norm_bwd(pairs, v, g, dres, name, *, tm=512):
    m, n = v.shape
    tm = min(tm, m)
    starts, specs, total = [], [], 0
    for a, b, tk in pairs:
        assert a.shape == (m, b.shape[0]) and b.shape[1] == n and a.shape[1] % tk == 0 and a.dtype == b.dtype == BF16
        nk = a.shape[1] // tk
        starts.append((total, nk))
        step = functools.partial(lambda k, s0, nk_: jnp.clip(k - s0, 0, nk_ - 1), s0=total, nk_=nk)
        specs.append(pl.BlockSpec((tm, tk), functools.partial(lambda i, k, st: (i, st(k)), st=step)))
        specs.append(pl.BlockSpec((tk, n), functools.partial(lambda i, k, st: (st(k), 0), st=step)))
        total += nk
    np_ = 2 * len(pairs)

    def body(*refs):
        v_ref, g_ref, dres_ref, dv_ref, dg_ref, acc = refs[np_:np_ + 6]
        i, k = pl.program_id(0), pl.program_id(1)
        for p, (s0, nk) in enumerate(starts):
            @pl.when((k >= s0) & (k < s0 + nk))
            def _(p=p, s0=s0):
                part = jnp.dot(refs[2 * p][...], refs[2 * p + 1][...], preferred_element_type=F32)
                if s0 == 0:
                    @pl.when(k == 0)
                    def _():
                        acc[...] = part

                    @pl.when(k > 0)
                    def _():
                        acc[...] += part
                else:
                    acc[...] += part

        @pl.when((i == 0) & (k == 0))
        def _():
            dg_ref[...] = jnp.zeros_like(dg_ref)

        @pl.when(k == total - 1)
        def _():
            dv, dg = _rms_bwd_val(v_ref[...], g_ref[...], acc[...])
            dv_ref[...] = dv + dres_ref[...]
            dg_ref[0:1, :] += dg

    row = pl.BlockSpec((tm, n), lambda i, k: (i, 0))
    dv, dg = pl.pallas_call(
        body, name=name, grid=(m // tm, total),
        in_specs=specs + [row, pl.BlockSpec((1, n), lambda i, k: (0, 0)), row],
        out_specs=[row, pl.BlockSpec((8, n), lambda i, k: (0, 0))],
        out_shape=[jax.ShapeDtypeStruct((m, n), F32), jax.ShapeDtypeStruct((8, n), F32)],
        scratch_shapes=[pltpu.VMEM((tm, n), F32)], compiler_params=_cp(2),
    )(*[op for a, b, _ in pairs for op in (a, b)], v, _row(g), dres)
    return dv, dg[0]


def _mm_swiglu_fwd(h, w, name, *, tm=1024, tn=1408):
    m, kdim = h.shape
    hid = w.shape[0] // 2
    tm = min(tm, m)
    nj = hid // tn
    dims = (((1,), (1,)), ((), ()))

    def body(h_ref, wg_ref, wu_ref, g_ref, u_ref, a_ref):
        g = lax.dot_general(h_ref[...], wg_ref[...], dims, preferred_element_type=F32)
        u = lax.dot_general(h_ref[...], wu_ref[...], dims, preferred_element_type=F32)
        g_ref[...] = g.astype(BF16)
        u_ref[...] = u.astype(BF16)
        a_ref[...] = (g * _sigmoid(g) * u).astype(BF16)

    out = pl.BlockSpec((tm, tn), lambda i, j: (i, j))
    return pl.pallas_call(
        body, name=name, grid=(m // tm, nj),
        in_specs=[pl.BlockSpec((tm, kdim), lambda i, j: (i, 0)), pl.BlockSpec((tn, kdim), lambda i, j: (j, 0)),
                  pl.BlockSpec((tn, kdim), lambda i, j: (j + nj, 0))],
        out_specs=[out, out, out], out_shape=[jax.ShapeDtypeStruct((m, hid), BF16)] * 3, compiler_params=_cp(2),
    )(h, w, w)


def _mm_swiglu_bwd(df, w, g, u, name, *, tm=1024, tn=1408):
    m, kdim = df.shape
    hid = w.shape[0]
    tm = min(tm, m)

    def body(df_ref, w_ref, g_ref, u_ref, dg_ref, du_ref):
        da = lax.dot_general(df_ref[...], w_ref[...], (((1,), (1,)), ((), ())), preferred_element_type=F32)
        gg, uu = g_ref[...].astype(F32), u_ref[...].astype(F32)
        sg = _sigmoid(gg)
        dg_ref[...] = (da * uu * (sg * (1.0 + gg * (1.0 - sg)))).astype(BF16)
        du_ref[...] = (da * gg * sg).astype(BF16)

    blk = pl.BlockSpec((tm, tn), lambda i, j: (i, j))
    return pl.pallas_call(
        body, name=name, grid=(m // tm, hid // tn),
        in_specs=[pl.BlockSpec((tm, kdim), lambda i, j: (i, 0)), pl.BlockSpec((tn, kdim), lambda i, j: (j, 0)), blk, blk],
        out_specs=[blk, blk], out_shape=[jax.ShapeDtypeStruct((m, hid), BF16)] * 2, compiler_params=_cp(2),
    )(df, w, g, u)


def _mm_norm_res(a, b, xres, g, g_next, name, *, tm=512):
    m, kdim = a.shape
    n = b.shape[1]
    tm = min(tm, m)
    two = g_next is not None

    def body(*refs):
        a_ref, b_ref, x_ref, g_ref = refs[:4]
        y_ref, xn_ref = refs[4 + two], refs[5 + two]
        y = jnp.dot(a_ref[...], b_ref[...], preferred_element_type=F32)
        y_ref[...] = y
        xn = x_ref[...] + _rms_val(y, g_ref[...])
        xn_ref[...] = xn
        if two:
            refs[7][...] = _rms_val(xn, refs[4][...]).astype(BF16)

    row = pl.BlockSpec((tm, n), lambda i: (i, 0))
    vec = pl.BlockSpec((1, n), lambda i: (0, 0))
    outs = pl.pallas_call(
        body, name=name, grid=(m // tm,),
        in_specs=[pl.BlockSpec((tm, kdim), lambda i: (i, 0)), pl.BlockSpec((kdim, n), lambda i: (0, 0)), row, vec] + ([vec] if two else []),
        out_specs=[row, row] + ([row] if two else []),
        out_shape=[jax.ShapeDtypeStruct((m, n), F32)] * 2 + ([jax.ShapeDtypeStruct((m, n), BF16)] if two else []),
        compiler_params=_cp(1),
    )(*((a, b, xres, _row(g)) + ((_row(g_next),) if two else ())))
    return (outs[0], outs[1], outs[2]) if two else (outs[0], outs[1], None)


def _rms_val(v, g):
    return v * lax.rsqrt(jnp.mean(v * v, axis=-1, keepdims=True) + RMS_EPS) * g


def _norm_plain(v, g, name, tr=512):
    t, d = v.shape
    tr = min(tr, t)

    def body(v_ref, g_ref, o_ref):
        o_ref[...] = _rms_val(v_ref[...], g_ref[...]).astype(BF16)

    return pl.pallas_call(
        body, name=name, grid=(t // tr,),
        in_specs=[pl.BlockSpec((tr, d), lambda i: (i, 0)), pl.BlockSpec((1, d), lambda i: (0, 0))],
        out_specs=pl.BlockSpec((tr, d), lambda i: (i, 0)),
        out_shape=jax.ShapeDtypeStruct((t, d), BF16), compiler_params=_cp(1),
    )(v, _row(g))


def _norm_bwd(v, g, dout, dres, out_dtype, name, tr=512):
    t, d = v.shape
    tr = min(tr, t)
    nt = t // tr
    has_res = dres is not None

    def body(*refs):
        v_ref, g_ref, do_ref = refs[:3]
        dv_ref, dg_ref = refs[3 + has_res], refs[4 + has_res]
        i = pl.program_id(0)
        vv = v_ref[...]
        dy = do_ref[...].astype(F32)
        r = lax.rsqrt(jnp.mean(vv * vv, axis=-1, keepdims=True) + RMS_EPS)
        vhat = vv * r
        dvh = dy * g_ref[...]
        dv = r * (dvh - vhat * jnp.mean(dvh * vhat, axis=-1, keepdims=True))
        if has_res:
            dv = dv + refs[3][...]
        dv_ref[...] = dv.astype(dv_ref.dtype)
        part = jnp.sum(dy * vhat, axis=0, keepdims=True)

        @pl.when(i == 0)
        def _():
            dg_ref[...] = jnp.zeros_like(dg_ref)

        dg_ref[0:1, :] += part

    row = pl.BlockSpec((tr, d), lambda i: (i, 0))
    dv, dg = pl.pallas_call(
        body, name=name, grid=(nt,),
        in_specs=[row, pl.BlockSpec((1, d), lambda i: (0, 0)), row] + ([row] if has_res else []),
        out_specs=[row, pl.BlockSpec((8, d), lambda i: (0, 0))],
        out_shape=[jax.ShapeDtypeStruct((t, d), out_dtype), jax.ShapeDtypeStruct((8, d), F32)],
        compiler_params=_cp(1),
    )(*((v, _row(g), dout) + ((dres,) if has_res else ())))
    return dv, dg[0]


def _loss_head(y, target, name, tr=512):
    t, d = y.shape
    tr = min(tr, t)
    nt = t // tr

    def body(y_ref, t_ref, dy_ref, l_ref):
        i = pl.program_id(0)
        err = y_ref[...] - t_ref[...]
        dy_ref[...] = err * (1.0 / d)

        @pl.when(i == 0)
        def _():
            l_ref[...] = jnp.zeros_like(l_ref)

        l_ref[...] += jnp.sum(err * err) * (0.5 / d)

    row = pl.BlockSpec((tr, d), lambda i: (i, 0))
    dy, l = pl.pallas_call(
        body, name=name, grid=(nt,), in_specs=[row, row],
        out_specs=[row, pl.BlockSpec((8, 128), lambda i: (0, 0))],
        out_shape=[jax.ShapeDtypeStruct((t, d), F32), jax.ShapeDtypeStruct((8, 128), F32)],
        compiler_params=_cp(1),
    )(y, target)
    return l[0, 0], dy


def _sigmoid(v):
    return 1.0 / (1.0 + jnp.exp(-v))


def _halo_specs(tq, width, nt, halo):
    per = tq // halo
    last = nt * per - 1
    main = pl.BlockSpec((tq, width), lambda i: (i, 0))
    prev = pl.BlockSpec((halo, width), lambda i: (jnp.maximum(i * per - 1, 0), 0))
    nxt = pl.BlockSpec((halo, width), lambda i: (jnp.minimum((i + 1) * per, last), 0))
    return prev, main, nxt


def _glu_window(zp, zm, zn, i, nt):
    def glu(z):
        z = z.astype(F32)
        return z[:, :CONV_W] * _sigmoid(z[:, CONV_W:])

    up = jnp.where(i > 0, glu(zp), 0.0)
    un = jnp.where(i < nt - 1, glu(zn), 0.0)
    return jnp.concatenate([up, glu(zm), un], axis=0)


def _shifted(win, shift, rows):
    if shift == 0:
        return win[:rows]
    return pltpu.roll(win, win.shape[0] - shift, 0)[:rows]


def _conv_fwd(zc, w, b, ln_g, ln_b, name, tq=256):
    t = zc.shape[0]
    tq = min(tq, t)
    nt = t // tq

    def body(zp_ref, zm_ref, zn_ref, w_ref, b_ref, g_ref, bb_ref, y_ref, c_ref):
        i = pl.program_id(0)
        win = _glu_window(zp_ref[...], zm_ref[...], zn_ref[...], i, nt)
        wv = w_ref[...]
        y = jnp.zeros((tq, CONV_W), F32) + b_ref[...]
        for k in range(CONV_K):
            y = y + _shifted(win, k + 1, tq) * wv[k:k + 1, :]
        y_ref[...] = y
        mu = jnp.mean(y, axis=-1, keepdims=True)
        yc = y - mu
        ln = yc * lax.rsqrt(jnp.mean(yc * yc, axis=-1, keepdims=True) + LN_EPS) * g_ref[...] + bb_ref[...]
        c_ref[...] = (ln * _sigmoid(ln)).astype(BF16)

    vec = pl.BlockSpec((1, CONV_W), lambda i: (0, 0))
    out = pl.BlockSpec((tq, CONV_W), lambda i: (i, 0))
    return pl.pallas_call(
        body, name=name, grid=(nt,),
        in_specs=[*_halo_specs(tq, 2 * CONV_W, nt, CONV_PAD), pl.BlockSpec((32, CONV_W), lambda i: (0, 0)), vec, vec, vec],
        out_specs=[out, out],
        out_shape=[jax.ShapeDtypeStruct((t, CONV_W), F32), jax.ShapeDtypeStruct((t, CONV_W), BF16)],
        compiler_params=_cp(1),
    )(zc, zc, zc, w, _row(b), _row(ln_g), _row(ln_b))


def _conv_bwd(zc, y, dc, w, ln_g, ln_b, name, tq=256):
    t = zc.shape[0]
    tq = min(tq, t)
    nt = t // tq
    rows = tq + 2 * CONV_PAD

    def body(zp_ref, zm_ref, zn_ref, yp_ref, ym_ref, yn_ref, dp_ref, dm_ref, dn_ref, w_ref, g_ref, bb_ref,
             dz_ref, dw_ref, db_ref, dg_ref, dbb_ref):
        i = pl.program_id(0)
        uwin = _glu_window(zp_ref[...], zm_ref[...], zn_ref[...], i, nt)
        ywin = jnp.concatenate([yp_ref[...], ym_ref[...], yn_ref[...]], axis=0)
        dcw = jnp.concatenate([dp_ref[...], dm_ref[...], dn_ref[...]], axis=0).astype(F32)
        mu = jnp.mean(ywin, axis=-1, keepdims=True)
        yc = ywin - mu
        rstd = lax.rsqrt(jnp.mean(yc * yc, axis=-1, keepdims=True) + LN_EPS)
        yhat = yc * rstd
        ln = yhat * g_ref[...] + bb_ref[...]
        sg = _sigmoid(ln)
        dl = dcw * (sg * (1.0 + ln * (1.0 - sg)))
        ridx = lax.broadcasted_iota(jnp.int32, (rows, 1), 0)
        inside = ((ridx >= CONV_PAD) | (i > 0)) & ((ridx < tq + CONV_PAD) | (i < nt - 1))
        dl = jnp.where(inside, dl, 0.0)
        dyh = dl * g_ref[...]
        dy = rstd * (dyh - jnp.mean(dyh, axis=-1, keepdims=True) - yhat * jnp.mean(dyh * yhat, axis=-1, keepdims=True))
        dy = jnp.where(inside, dy, 0.0)
        main = slice(CONV_PAD, CONV_PAD + tq)
        dlm, yhm, dym = dl[main], yhat[main], dy[main]

        @pl.when(i == 0)
        def _():
            dw_ref[...] = jnp.zeros_like(dw_ref)
            db_ref[...] = jnp.zeros_like(db_ref)
            dg_ref[...] = jnp.zeros_like(dg_ref)
            dbb_ref[...] = jnp.zeros_like(dbb_ref)

        dg_ref[0:1, :] += jnp.sum(dlm * yhm, axis=0, keepdims=True)
        dbb_ref[0:1, :] += jnp.sum(dlm, axis=0, keepdims=True)
        db_ref[0:1, :] += jnp.sum(dym, axis=0, keepdims=True)
        wv = w_ref[...]
        du = jnp.zeros((tq, CONV_W), F32)
        for k in range(CONV_K):
            du = du + _shifted(dy, 2 * CONV_PAD - 1 - k, tq) * wv[k:k + 1, :]
            dw_ref[k:k + 1, :] += jnp.sum(dym * _shifted(uwin, k + 1, tq), axis=0, keepdims=True)
        zm = zm_ref[...].astype(F32)
        a, gt = zm[:, :CONV_W], zm[:, CONV_W:]
        sgt = _sigmoid(gt)
        dz_ref[:, :CONV_W] = (du * sgt).astype(BF16)
        dz_ref[:, CONV_W:] = (du * a * sgt * (1.0 - sgt)).astype(BF16)

    vec = pl.BlockSpec((1, CONV_W), lambda i: (0, 0))
    acc = pl.BlockSpec((8, CONV_W), lambda i: (0, 0))
    dz, dw, db, dg, dbb = pl.pallas_call(
        body, name=name, grid=(nt,),
        in_specs=[*_halo_specs(tq, 2 * CONV_W, nt, CONV_PAD), *_halo_specs(tq, CONV_W, nt, CONV_PAD),
                  *_halo_specs(tq, CONV_W, nt, CONV_PAD), pl.BlockSpec((32, CONV_W), lambda i: (0, 0)), vec, vec],
        out_specs=[pl.BlockSpec((tq, 2 * CONV_W), lambda i: (i, 0)), pl.BlockSpec((32, CONV_W), lambda i: (0, 0)), acc, acc, acc],
        out_shape=[jax.ShapeDtypeStruct((t, 2 * CONV_W), BF16), jax.ShapeDtypeStruct((32, CONV_W), F32),
                   jax.ShapeDtypeStruct((8, CONV_W), F32), jax.ShapeDtypeStruct((8, CONV_W), F32),
                   jax.ShapeDtypeStruct((8, CONV_W), F32)],
        compiler_params=_cp(1),
    )(zc, zc, zc, y, y, y, dc, dc, dc, w, _row(ln_g), _row(ln_b))
    return dz, dw, db[0], dg[0], dbb[0]


def _att_tiles(seq_len):
    tq = min(256, seq_len)
    return tq, min(128, tq)


def _t5_bucket_np(rel):
    nb = NUM_BUCKETS // 2
    max_exact = nb // 2
    ret = np.where(rel > 0, nb, 0)
    n = np.abs(rel)
    nf = np.maximum(n, 1).astype(np.float32)
    large = max_exact + (np.log(nf / np.float32(max_exact)) / np.float32(np.log(MAX_DISTANCE / max_exact))
                         * np.float32(nb - max_exact)).astype(np.int32)
    large = np.minimum(large, nb - 1)
    return ret + np.where(n < max_exact, n, large)


def _bucket_index(sb, dilation):
    off = np.arange(sb + 2 * ATT_RADIUS)[None, :] - ATT_RADIUS - np.arange(sb)[:, None]
    idx = _t5_bucket_np(off * dilation).astype(np.int32)
    return np.where(np.abs(off) <= ATT_RADIUS, idx, -1).astype(np.int32)


def _bias_table(tab, idx, name):
    sb, w = idx.shape

    def body(tab_ref, idx_ref, o_ref):
        ix = idx_ref[...]
        for h in range(HEADS):
            acc = jnp.full((sb, w), NEG_INF, F32)
            for b in range(NUM_BUCKETS):
                acc = jnp.where(ix == b, tab_ref[b, h], acc)
            o_ref[h] = acc

    return pl.pallas_call(
        body, name=name, out_shape=jax.ShapeDtypeStruct((HEADS, sb, w), F32),
        in_specs=[pl.BlockSpec(memory_space=pltpu.SMEM), pl.BlockSpec(memory_space=pltpu.VMEM)],
        out_specs=pl.BlockSpec(memory_space=pltpu.VMEM),
    )(tab, idx)


def _qkv_specs(tq, per, last_blk, cols, tile):
    def main(col):
        return pl.BlockSpec((tq, ATT_GW), lambda *g: (tile(*g), col))

    def prev(col):
        return pl.BlockSpec((ATT_RADIUS, ATT_GW), lambda *g: (jnp.maximum(tile(*g) * per - 1, 0), col))

    def nxt(col):
        return pl.BlockSpec((ATT_RADIUS, ATT_GW), lambda *g: (jnp.minimum((tile(*g) + 1) * per, last_blk), col))

    cq, ck, cv = cols
    return [main(cq), prev(ck), main(ck), nxt(ck), prev(cv), main(cv), nxt(cv)]


def _att_fwd(q, k, v, bias, seq_len, name, cols=(0, 0, 0)):
    t = q.shape[0]
    tq, sb = _att_tiles(seq_len)
    nbl = seq_len // tq
    nt = t // tq
    per = tq // ATT_RADIUS
    w = sb + 2 * ATT_RADIUS
    last_blk = t // ATT_RADIUS - 1

    def body(q_ref, kp_ref, kc_ref, kn_ref, vp_ref, vc_ref, vn_ref, b_ref, o_ref, l_ref):
        n = pl.program_id(0) % nbl
        kw = jnp.concatenate([kp_ref[...], kc_ref[...], kn_ref[...]], axis=0)
        vw = jnp.concatenate([vp_ref[...], vc_ref[...], vn_ref[...]], axis=0)
        kpos = n * tq - ATT_RADIUS + lax.broadcasted_iota(jnp.int32, (1, tq + 2 * ATT_RADIUS), 1)
        valid = (kpos >= 0) & (kpos < seq_len)
        for j in range(tq // sb):
            qj = q_ref[j * sb:(j + 1) * sb, :]
            kj, vj, okj = kw[j * sb:j * sb + w], vw[j * sb:j * sb + w], valid[:, j * sb:j * sb + w]
            outs, lses = [], []
            for h in range(HEADS):
                hs = slice(h * HEAD_DIM, (h + 1) * HEAD_DIM)
                s = lax.dot_general(qj[:, hs], kj[:, hs], (((1,), (1,)), ((), ())), preferred_element_type=F32)
                s = jnp.where(okj, s * (HEAD_DIM ** -0.5) + b_ref[h], NEG_INF)
                m = jnp.max(s, axis=-1, keepdims=True)
                e = jnp.exp(s - m)
                den = jnp.sum(e, axis=-1, keepdims=True)
                o = jnp.dot(e.astype(BF16), vj[:, hs], preferred_element_type=F32) / den
                outs.append(o)
                lses.append(jnp.broadcast_to(m + jnp.log(den), (sb, HEAD_DIM)))
            o_ref[j * sb:(j + 1) * sb, :] = jnp.concatenate(outs, axis=1).astype(BF16)
            l_ref[j * sb:(j + 1) * sb, :] = jnp.concatenate(lses, axis=1)

    main = pl.BlockSpec((tq, ATT_GW), lambda i: (i, 0))
    return pl.pallas_call(
        body, name=name, grid=(nt,),
        in_specs=[*_qkv_specs(tq, per, last_blk, cols, lambda i: i), pl.BlockSpec((HEADS, sb, w), lambda i: (0, 0, 0))],
        out_specs=[main, main],
        out_shape=[jax.ShapeDtypeStruct((t, ATT_GW), BF16), jax.ShapeDtypeStruct((t, ATT_GW), F32)],
        compiler_params=_cp(1),
    )(q, k, k, k, v, v, v, bias)


PERM_ROWS = 512
LANES = 128


def _spec_rm(d, tr, width, col=0):
    return pl.BlockSpec((d, tr // d, width), lambda i: (0, i, col))


def _rm_view(a, d):
    return a.reshape(d, a.shape[0] // d, a.shape[1])


def _gather_residues(scr, val, d, out_ref, col0):
    tr, w = val.shape
    for c in range(w // LANES):
        scr[...] = val[:, c * LANES:(c + 1) * LANES].astype(F32)
        for r in range(d):
            out_ref[r, :, col0 + c * LANES:col0 + (c + 1) * LANES] = scr[pl.ds(r, tr // d, stride=d), :].astype(out_ref.dtype)


def _scatter_residues(scr, ref, d):
    w = ref.shape[2]
    cols = []
    for c in range(w // LANES):
        for r in range(d):
            scr[pl.ds(r, scr.shape[0] // d, stride=d), :] = ref[r, :, c * LANES:(c + 1) * LANES].astype(F32)
        cols.append(scr[...])
    return jnp.concatenate(cols, axis=1)


def _to_residue_major(srcs, d, name, tr=PERM_ROWS):
    t = srcs[0][0].shape[0]
    n = len(srcs)

    def body(*refs):
        scr = refs[2 * n]
        for k in range(n):
            _gather_residues(scr, refs[k][...], d, refs[n + k], 0)

    outs = pl.pallas_call(
        body, name=name, grid=(t // tr,),
        in_specs=[pl.BlockSpec((tr, ATT_GW), functools.partial(lambda i, col: (i, col), col=col)) for _, col in srcs],
        out_specs=[_spec_rm(d, tr, ATT_GW)] * n,
        out_shape=[jax.ShapeDtypeStruct((d, t // d, ATT_GW), a.dtype) for a, _ in srcs],
        scratch_shapes=[pltpu.VMEM((tr, LANES), F32)], compiler_params=_cp(1),
    )(*[a for a, _ in srcs])
    return [o.reshape(t, ATT_GW) for o in outs]


def _att_combine(outs, lses, name, tr=PERM_ROWS):
    t = outs[0].shape[0]
    dils = [d for _, d in ATT_PATTERNS]

    def body(o0, o1, o2, l0, l1, l2, a_ref, lse_ref, scr):
        os_, ls_ = [], []
        for o_ref, l_ref, d in zip((o0, o1, o2), (l0, l1, l2), dils):
            if d == 1:
                os_.append(o_ref[...].astype(F32))
                ls_.append(l_ref[...])
            else:
                os_.append(_scatter_residues(scr, o_ref, d))
                ls_.append(_scatter_residues(scr, l_ref, d))
        la, lb, lc = ls_
        m = jnp.maximum(jnp.maximum(la, lb), lc)
        ea, eb, ec = jnp.exp(la - m), jnp.exp(lb - m), jnp.exp(lc - m)
        den = ea + eb + ec
        a_ref[...] = ((ea * os_[0] + eb * os_[1] + ec * os_[2]) / den).astype(BF16)
        lse_ref[...] = m + jnp.log(den)

    row = pl.BlockSpec((tr, ATT_GW), lambda i: (i, 0))
    specs = [row if d == 1 else _spec_rm(d, tr, ATT_GW) for d in dils]
    views = lambda arrs: [a if d == 1 else _rm_view(a, d) for a, d in zip(arrs, dils)]
    return pl.pallas_call(
        body, name=name, grid=(t // tr,), in_specs=specs * 2, out_specs=[row, row],
        out_shape=[jax.ShapeDtypeStruct((t, ATT_GW), BF16), jax.ShapeDtypeStruct((t, ATT_GW), F32)],
        scratch_shapes=[pltpu.VMEM((tr, LANES), F32)], compiler_params=_cp(1),
    )(*views(outs), *views(lses))


def _assemble_dza(dqkv, name, tr=PERM_ROWS):
    t = dqkv[0][0].shape[0]
    dils = [d for _, d in ATT_PATTERNS]

    def body(*refs):
        o_ref, scr = refs[9], refs[10]
        for g, d in enumerate(dils):
            for c in range(3):
                ref = refs[g * 3 + c]
                val = ref[...] if d == 1 else _scatter_residues(scr, ref, d).astype(BF16)
                o_ref[:, (c * 3 + g) * ATT_GW:(c * 3 + g + 1) * ATT_GW] = val

    row = pl.BlockSpec((tr, ATT_GW), lambda i: (i, 0))
    specs = [row if d == 1 else _spec_rm(d, tr, ATT_GW) for d in dils for _ in range(3)]
    args = [a if d == 1 else _rm_view(a, d) for trio, d in zip(dqkv, dils) for a in trio]
    return pl.pallas_call(
        body, name=name, grid=(t // tr,), in_specs=specs, out_specs=pl.BlockSpec((tr, 3 * ATT_W), lambda i: (i, 0)),
        out_shape=jax.ShapeDtypeStruct((t, 3 * ATT_W), BF16),
        scratch_shapes=[pltpu.VMEM((tr, LANES), F32)], compiler_params=_cp(1),
    )(*args)


def _att_bwd(q, k, v, bias, o, do, lse, seq_len, name, cols=(0, 0, 0)):
    t = q.shape[0]
    tq, sb = _att_tiles(seq_len)
    nbl = seq_len // tq
    n_res = t // seq_len
    per = tq // ATT_RADIUS
    w = sb + 2 * ATT_RADIUS
    last_blk = t // ATT_RADIUS - 1
    acc_rows = 2 * tq + ATT_RADIUS
    scale = HEAD_DIM ** -0.5

    def body(q_ref, kp_ref, kc_ref, kn_ref, vp_ref, vc_ref, vn_ref, b_ref, o_ref, do_ref, l_ref,
             dq_ref, dk_ref, dv_ref, db_ref, ak_ref, av_ref):
        r, n = pl.program_id(0), pl.program_id(1)

        @pl.when((r == 0) & (n == 0))
        def _():
            db_ref[...] = jnp.zeros_like(db_ref)

        @pl.when(n == 0)
        def _():
            ak_ref[...] = jnp.zeros_like(ak_ref)
            av_ref[...] = jnp.zeros_like(av_ref)

        @pl.when(n < nbl)
        def _():
            kw = jnp.concatenate([kp_ref[...], kc_ref[...], kn_ref[...]], axis=0)
            vw = jnp.concatenate([vp_ref[...], vc_ref[...], vn_ref[...]], axis=0)
            kpos = n * tq - ATT_RADIUS + lax.broadcasted_iota(jnp.int32, (1, tq + 2 * ATT_RADIUS), 1)
            valid = (kpos >= 0) & (kpos < seq_len)
            for j in range(tq // sb):
                rows = slice(j * sb, (j + 1) * sb)
                qj, doj = q_ref[rows, :], do_ref[rows, :]
                dd = doj.astype(F32) * o_ref[rows, :].astype(F32)
                lj = l_ref[rows, :]
                kj, vj, okj = kw[j * sb:j * sb + w], vw[j * sb:j * sb + w], valid[:, j * sb:j * sb + w]
                dqs, dks, dvs = [], [], []
                for h in range(HEADS):
                    hs = slice(h * HEAD_DIM, (h + 1) * HEAD_DIM)
                    s = lax.dot_general(qj[:, hs], kj[:, hs], (((1,), (1,)), ((), ())), preferred_element_type=F32)
                    s = jnp.where(okj, s * scale + b_ref[h], NEG_INF)
                    p = jnp.exp(s - lj[:, h * HEAD_DIM:h * HEAD_DIM + 1])
                    dp = lax.dot_general(doj[:, hs], vj[:, hs], (((1,), (1,)), ((), ())), preferred_element_type=F32)
                    ds = p * (dp - jnp.sum(dd[:, hs], axis=-1, keepdims=True))
                    db_ref[h] += ds
                    dsb = ds.astype(BF16)
                    dqs.append(jnp.dot(dsb, kj[:, hs], preferred_element_type=F32) * scale)
                    dks.append(lax.dot_general(dsb, qj[:, hs], (((0,), (0,)), ((), ())), preferred_element_type=F32) * scale)
                    dvs.append(lax.dot_general(p.astype(BF16), doj[:, hs], (((0,), (0,)), ((), ())), preferred_element_type=F32))
                dq_ref[rows, :] = jnp.concatenate(dqs, axis=1).astype(BF16)
                a0 = j * sb + tq - ATT_RADIUS
                ak_ref[a0:a0 + w, :] += jnp.concatenate(dks, axis=1)
                av_ref[a0:a0 + w, :] += jnp.concatenate(dvs, axis=1)

        dk_ref[...] = ak_ref[0:tq, :].astype(BF16)
        dv_ref[...] = av_ref[0:tq, :].astype(BF16)
        keep = tq + ATT_RADIUS
        nk, nv = ak_ref[tq:acc_rows, :], av_ref[tq:acc_rows, :]
        ak_ref[0:keep, :] = nk
        av_ref[0:keep, :] = nv
        ak_ref[keep:acc_rows, :] = jnp.zeros((acc_rows - keep, ATT_GW), F32)
        av_ref[keep:acc_rows, :] = jnp.zeros((acc_rows - keep, ATT_GW), F32)

    def tile(r, n):
        return r * nbl + jnp.minimum(n, nbl - 1)

    main = pl.BlockSpec((tq, ATT_GW), lambda r, n: (tile(r, n), 0))
    lag = pl.BlockSpec((tq, ATT_GW), lambda r, n: (r * nbl + jnp.maximum(n - 1, 0), 0))
    bspec = pl.BlockSpec((HEADS, sb, w), lambda r, n: (0, 0, 0))
    return pl.pallas_call(
        body, name=name, grid=(n_res, nbl + 1),
        in_specs=[*_qkv_specs(tq, per, last_blk, cols, tile), bspec, main, main, main],
        out_specs=[main, lag, lag, bspec],
        out_shape=[jax.ShapeDtypeStruct((t, ATT_GW), BF16)] * 3 + [jax.ShapeDtypeStruct((HEADS, sb, w), F32)],
        scratch_shapes=[pltpu.VMEM((acc_rows, ATT_GW), F32), pltpu.VMEM((acc_rows, ATT_GW), F32)],
        compiler_params=_cp(2),
    )(q, k, k, k, v, v, v, bias, o, do, lse)


def _bias_grad(db, idx, name):
    _, sb, w = db.shape

    def body(db_ref, idx_ref, o_ref):
        ix = idx_ref[...]
        lane = lax.broadcasted_iota(jnp.int32, (1, 128), 1)
        rows = []
        for h in range(HEADS):
            d = db_ref[h]
            acc = jnp.zeros((1, 128), F32)
            for b in range(NUM_BUCKETS):
                acc = acc + jnp.where(lane == b, jnp.sum(jnp.where(ix == b, d, 0.0)), 0.0)
            rows.append(acc)
        o_ref[...] = jnp.concatenate(rows + [jnp.zeros((8 - HEADS, 128), F32)], axis=0)

    return pl.pallas_call(
        body, name=name, out_shape=jax.ShapeDtypeStruct((8, 128), F32),
        in_specs=[pl.BlockSpec(memory_space=pltpu.VMEM), pl.BlockSpec(memory_space=pltpu.VMEM)],
        out_specs=pl.BlockSpec(memory_space=pltpu.VMEM),
    )(db, idx)


def _mem_fwd(zq, kv, name, tq=512):
    t = zq.shape[0]
    tq = min(tq, t)
    scale = MEM_HD ** -0.5

    def body(q_ref, kv_ref, o_ref):
        outs = []
        for h in range(MEM_HEADS):
            hs = slice(h * MEM_HD, (h + 1) * MEM_HD)
            kh = kv_ref[:, h * MEM_HD:(h + 1) * MEM_HD]
            vh = kv_ref[:, MEM_W + h * MEM_HD:MEM_W + (h + 1) * MEM_HD]
            s = lax.dot_general(q_ref[:, hs], kh, (((1,), (1,)), ((), ())), preferred_element_type=F32) * scale
            e = jnp.exp(s - jnp.max(s, axis=-1, keepdims=True))
            den = jnp.sum(e, axis=-1, keepdims=True)
            outs.append(jnp.dot(e.astype(BF16), vh, preferred_element_type=F32) / den)
        o_ref[...] = jnp.concatenate(outs, axis=1).astype(BF16)

    row = pl.BlockSpec((tq, MEM_W), lambda i: (i, 0))
    return pl.pallas_call(
        body, name=name, grid=(t // tq,),
        in_specs=[row, pl.BlockSpec((N_MEM, 2 * MEM_W), lambda i: (0, 0))], out_specs=row,
        out_shape=jax.ShapeDtypeStruct((t, MEM_W), BF16), compiler_params=_cp(1),
    )(zq, kv)


def _mem_bwd(zq, kv, do, name, tq=512):
    t = zq.shape[0]
    tq = min(tq, t)
    scale = MEM_HD ** -0.5

    def body(q_ref, kv_ref, do_ref, dq_ref, dkv_ref):
        @pl.when(pl.program_id(0) == 0)
        def _():
            dkv_ref[...] = jnp.zeros_like(dkv_ref)

        dqs = []
        for h in range(MEM_HEADS):
            hs = slice(h * MEM_HD, (h + 1) * MEM_HD)
            vs = slice(MEM_W + h * MEM_HD, MEM_W + (h + 1) * MEM_HD)
            qh, doh = q_ref[:, hs], do_ref[:, hs]
            kh, vh = kv_ref[:, hs], kv_ref[:, vs]
            s = lax.dot_general(qh, kh, (((1,), (1,)), ((), ())), preferred_element_type=F32) * scale
            e = jnp.exp(s - jnp.max(s, axis=-1, keepdims=True))
            p = e / jnp.sum(e, axis=-1, keepdims=True)
            dp = lax.dot_general(doh, vh, (((1,), (1,)), ((), ())), preferred_element_type=F32)
            ds = p * (dp - jnp.sum(p * dp, axis=-1, keepdims=True))
            dsb = ds.astype(BF16)
            dqs.append(jnp.dot(dsb, kh, preferred_element_type=F32) * scale)
            dkv_ref[:, hs] += lax.dot_general(dsb, qh, (((0,), (0,)), ((), ())), preferred_element_type=F32) * scale
            dkv_ref[:, vs] += lax.dot_general(p.astype(BF16), doh, (((0,), (0,)), ((), ())), preferred_element_type=F32)
        dq_ref[...] = jnp.concatenate(dqs, axis=1).astype(BF16)

    row = pl.BlockSpec((tq, MEM_W), lambda i: (i, 0))
    full = pl.BlockSpec((N_MEM, 2 * MEM_W), lambda i: (0, 0))
    return pl.pallas_call(
        body, name=name, grid=(t // tq,), in_specs=[row, full, row], out_specs=[row, full],
        out_shape=[jax.ShapeDtypeStruct((t, MEM_W), BF16), jax.ShapeDtypeStruct((N_MEM, 2 * MEM_W), F32)],
        compiler_params=_cp(1),
    )(zq, kv, do)


def _gate_fwd(zg, bg, ys, name, tr=256):
    t = zg.shape[0]
    tr = min(tr, t)

    def body(z_ref, b_ref, y0, y1, y2, o_ref):
        g = _sigmoid(z_ref[...].astype(F32) + b_ref[...])
        o_ref[...] = (g[:, :D] * y0[...].astype(F32) + g[:, D:2 * D] * y1[...].astype(F32)
                      + g[:, 2 * D:] * y2[...].astype(F32)).astype(BF16)

    row = pl.BlockSpec((tr, D), lambda i: (i, 0))
    return pl.pallas_call(
        body, name=name, grid=(t // tr,),
        in_specs=[pl.BlockSpec((tr, 3 * D), lambda i: (i, 0)), pl.BlockSpec((1, 3 * D), lambda i: (0, 0)), row, row, row],
        out_specs=row, out_shape=jax.ShapeDtypeStruct((t, D), BF16), compiler_params=_cp(1),
    )(zg, _row(bg), *ys)


def _gate_bwd(zg, bg, ys, dmerged, name, tr=256):
    t = zg.shape[0]
    tr = min(tr, t)

    def body(z_ref, b_ref, y0, y1, y2, dm_ref, d0, d1, d2, dz_ref, db_ref):
        @pl.when(pl.program_id(0) == 0)
        def _():
            db_ref[...] = jnp.zeros_like(db_ref)

        g = _sigmoid(z_ref[...].astype(F32) + b_ref[...])
        dm = dm_ref[...].astype(F32)
        for i, (y_ref, d_ref) in enumerate(((y0, d0), (y1, d1), (y2, d2))):
            gi = g[:, i * D:(i + 1) * D]
            d_ref[...] = (dm * gi).astype(BF16)
            dz = dm * y_ref[...].astype(F32) * gi * (1.0 - gi)
            dz_ref[:, i * D:(i + 1) * D] = dz.astype(BF16)
            db_ref[0:1, i * D:(i + 1) * D] += jnp.sum(dz, axis=0, keepdims=True)

    row = pl.BlockSpec((tr, D), lambda i: (i, 0))
    wide = pl.BlockSpec((tr, 3 * D), lambda i: (i, 0))
    d0, d1, d2, dz, db = pl.pallas_call(
        body, name=name, grid=(t // tr,),
        in_specs=[wide, pl.BlockSpec((1, 3 * D), lambda i: (0, 0)), row, row, row, row],
        out_specs=[row, row, row, wide, pl.BlockSpec((8, 3 * D), lambda i: (0, 0))],
        out_shape=[jax.ShapeDtypeStruct((t, D), BF16)] * 3 + [jax.ShapeDtypeStruct((t, 3 * D), BF16),
                                                              jax.ShapeDtypeStruct((8, 3 * D), F32)],
        compiler_params=_cp(1),
    )(zg, _row(bg), *ys, dmerged)
    return (d0, d1, d2), dz, db[0]


def _adamw(w, g, m, v, name):
    shape = w.shape
    w, g, m, v = (a.reshape(-1, shape[-1]) for a in (w, g, m, v))
    r, d = w.shape
    tr = next((c for c in (512, 352, 256, 128, 64, 32, 16, 8) if r % c == 0), r)

    def body(w_ref, g_ref, m_ref, v_ref, d_ref, nm_ref, nv_ref):
        gg = g_ref[...]
        m2 = ADAM_B1 * m_ref[...] + (1.0 - ADAM_B1) * gg
        v2 = ADAM_B2 * v_ref[...] + (1.0 - ADAM_B2) * (gg * gg)
        m_hat = m2 / (1.0 - ADAM_B1 ** ADAM_STEP)
        v_hat = v2 / (1.0 - ADAM_B2 ** ADAM_STEP)
        d_ref[...] = -ADAM_LR * (m_hat / (jnp.sqrt(v_hat) + ADAM_EPS) + ADAM_WD * w_ref[...])
        nm_ref[...] = m2
        nv_ref[...] = v2

    row = pl.BlockSpec((tr, d), lambda i: (i, 0))
    outs = pl.pallas_call(
        body, name=name, grid=(r // tr,), in_specs=[row] * 4, out_specs=[row] * 3,
        out_shape=[jax.ShapeDtypeStruct((r, d), F32)] * 3, compiler_params=_cp(1),
    )(w, g, m, v)
    return [o.reshape(shape) for o in outs]


def _slab_tile(rows):
    return next((c for c in range(min(rows, 512), 15, -16) if rows % c == 0), rows)


def _sum_slots(parts, out_dtype, name):
    n, r, d = parts.shape
    tr = _slab_tile(r)

    def body(p_ref, o_ref):
        acc = p_ref[0].astype(F32)
        for s in range(1, n):
            acc = acc + p_ref[s].astype(F32)
        o_ref[...] = acc.astype(o_ref.dtype)

    return pl.pallas_call(
        body, name=name, grid=(r // tr,), in_specs=[pl.BlockSpec((n, tr, d), lambda i: (0, i, 0))],
        out_specs=pl.BlockSpec((tr, d), lambda i: (i, 0)),
        out_shape=jax.ShapeDtypeStruct((r, d), out_dtype), compiler_params=_cp(1),
    )(parts)


def _place():
    return lax.axis_index("x"), lax.axis_index("y"), lax.axis_index("c")


def _all_gather(shard, name):
    r, d = shard.shape

    def body(x_ref, out_ref, send_sems, recv_sems, local_sem):
        x, y, c = _place()
        me, sibling = (x, y, c), (x, y, 1 - c)
        chips = [(1 - x, y), (x, 1 - y), (1 - x, 1 - y)]

        def slot(px, py, pc):
            return out_ref.at[4 * px + 2 * py + pc]

        def copy(k, block, to, src=None):
            return pltpu.make_async_remote_copy(
                src_ref=slot(*block) if src is None else src, dst_ref=slot(*block),
                send_sem=send_sems.at[k], recv_sem=recv_sems.at[k], device_id=to, device_id_type=MESH)

        mine = pltpu.make_async_copy(x_ref, slot(*me), local_sem)
        mine.start()
        first = [copy(0, me, sibling, src=x_ref)]
        first += [copy(1 + j, me, (*chip, c), src=x_ref) for j, chip in enumerate(chips)]
        for cp in first:
            cp.start()
        passed = [copy(4 + j, (*chip, c), sibling) for j, chip in enumerate(chips)]
        for j, chip in enumerate(chips):
            copy(1 + j, (*chip, c), me).wait_recv()
            passed[j].start()
        copy(0, sibling, me).wait_recv()
        for j, chip in enumerate(chips):
            copy(4 + j, (*chip, 1 - c), me).wait_recv()
        for cp in first + passed:
            cp.wait_send()
        mine.wait()

    return pl.pallas_call(
        body, name=name, out_shape=jax.ShapeDtypeStruct((N_DEV, r, d), shard.dtype),
        in_specs=[pl.BlockSpec(memory_space=pl.ANY)], out_specs=pl.BlockSpec(memory_space=pl.ANY),
        scratch_shapes=[pltpu.SemaphoreType.DMA((7,)), pltpu.SemaphoreType.DMA((7,)), pltpu.SemaphoreType.DMA],
    )(shard)


HBM_SPEC = pl.BlockSpec(memory_space=pltpu.HBM)
SEM_SPEC = pl.BlockSpec(memory_space=pltpu.SEMAPHORE)
SPLIT_PARAMS = pltpu.CompilerParams(has_side_effects=pltpu.SideEffectType.DATAFLOW_SIDE_EFFECTING)


def _peers():
    x, y, c = _place()
    flip = lambda v, bit: 1 - v if bit else v
    return 4 * x + 2 * y + c, [((flip(x, k >> 2 & 1), flip(y, k >> 1 & 1), flip(c, k & 1))) for k in range(1, N_DEV)]


def _exchange_start(src, all_gather, name):
    r, d = src.shape[-2:]

    def body(src_ref, land_ref, send_sems, recv_sems, src_thru, land_thru, token):
        me, peers = _peers()
        for k, (px, py, pc) in enumerate(peers):
            part = src_ref if all_gather else src_ref.at[4 * px + 2 * py + pc]
            pltpu.make_async_remote_copy(src_ref=part, dst_ref=land_ref.at[me], send_sem=send_sems.at[k],
                                         recv_sem=recv_sems.at[k], device_id=(px, py, pc), device_id_type=MESH).start()
        token[...] = jnp.zeros_like(token)

    land = lax.empty((N_DEV, r, d), src.dtype)
    return pl.pallas_call(
        body, name=name,
        out_shape=(pltpu.SemaphoreType.DMA((N_DEV - 1,)), pltpu.SemaphoreType.DMA((N_DEV - 1,)), pltpu.HBM(src.shape, src.dtype),
                   pltpu.HBM(land.shape, land.dtype), jax.ShapeDtypeStruct((8, LANES), F32)),
        in_specs=(HBM_SPEC, HBM_SPEC), out_specs=(SEM_SPEC, SEM_SPEC, HBM_SPEC, HBM_SPEC, pl.BlockSpec(memory_space=pltpu.VMEM)),
        input_output_aliases={0: 2, 1: 3}, compiler_params=SPLIT_PARAMS,
    )(pltpu.with_memory_space_constraint(src, pltpu.HBM), pltpu.with_memory_space_constraint(land, pltpu.HBM))


def _exchange_wait(started, after, all_gather, name):
    send_sems, recv_sems, src_thru, land_thru, _ = started

    def body(src_ref, land_ref, send_sems, recv_sems, after_ref, src_out, land_out):
        me, peers = _peers()
        for k, (px, py, pc) in enumerate(peers):
            part = src_ref if all_gather else src_ref.at[4 * px + 2 * py + pc]
            cp = pltpu.make_async_remote_copy(src_ref=part, dst_ref=land_ref.at[4 * px + 2 * py + pc], send_sem=send_sems.at[k],
                                              recv_sem=recv_sems.at[k], device_id=(px, py, pc), device_id_type=MESH)
            cp.wait_send()
            cp.wait_recv()

    return pl.pallas_call(
        body, name=name,
        out_shape=(pltpu.HBM(src_thru.shape, src_thru.dtype), pltpu.HBM(land_thru.shape, land_thru.dtype)),
        in_specs=(HBM_SPEC, HBM_SPEC, SEM_SPEC, SEM_SPEC, pl.BlockSpec(memory_space=pl.ANY)), out_specs=(HBM_SPEC, HBM_SPEC),
        input_output_aliases={0: 0, 1: 1}, compiler_params=SPLIT_PARAMS,
    )(src_thru, land_thru, send_sems, recv_sems, after)


def _own_slot(land, mine):
    x, y, c = _place()
    return lax.dynamic_update_slice(land, mine[None], (4 * x + 2 * y + c, 0, 0))


BIG = (("w_in", (D, 864), 1), ("w_conv_out", (CONV_W, 128), 1), ("w_att_out", (ATT_GW, 128), 1), ("w_mem_kv", (128, D), 0),
       ("w_mem_out", (MEM_W, 128), 1), ("w_out", (128, D), 0), ("w_ffn_in", (D, 704), 1), ("w_ffn_out", (352, D), 0),
       ("conv_dw", (CONV_K, 64), 1))
SMALL = (("rel_bias", (NUM_BUCKETS, 12)), ("norm_mix_pre", (DEPTH, D)), ("b_gate", (DEPTH, 3 * D)),
         ("conv_dw_bias", (DEPTH, CONV_W)), ("conv_ln_g", (DEPTH, CONV_W)), ("conv_ln_b", (DEPTH, CONV_W)),
         ("norm_mem", (DEPTH, D)), ("norm_mix_post", (DEPTH, D)), ("norm_ffn_pre", (DEPTH, D)), ("norm_ffn_post", (DEPTH, D)))
TWIN_WEIGHTS = ("rel_bias", "norm_mix_pre", "w_in", "b_gate", "conv_dw", "conv_dw_bias", "conv_ln_g", "conv_ln_b", "w_conv_out",
                "w_att_out", "norm_mem", "w_mem_kv", "w_mem_out", "w_out", "norm_mix_post", "norm_ffn_pre", "w_ffn_in",
                "w_ffn_out", "norm_ffn_post")


BIG_INFO = {n: (s, a) for n, s, a in BIG}
GROUPS = {"mix": ("w_in", "w_conv_out", "w_att_out", "w_mem_kv", "w_mem_out", "w_out", "conv_dw"), "ffn": ("w_ffn_in", "w_ffn_out")}
TILE_BYTES_PER_LANE = 32


def _rows_of(shape, dtype):
    tile = TILE_BYTES_PER_LANE // jnp.dtype(dtype).itemsize
    return -(-int(np.prod(shape)) // (D * tile)) * tile


def _row_form(shape, axis):
    return tuple(shape) if axis == 0 else (shape[1], shape[0])


def _as_slab_rows(p, lead=()):
    shape = p.shape[len(lead):]
    rows = _rows_of(shape, p.dtype)
    if shape == (rows, D):
        return p
    n = int(np.prod(shape))
    p = jnp.pad(p.reshape(*lead, n), [(0, 0)] * len(lead) + [(0, rows * D - n)])
    return p.reshape(*lead, rows, D)


def _from_slab_rows(rows, shape, lead=()):
    if rows.shape[len(lead):] == tuple(shape):
        return rows
    n = int(np.prod(shape))
    return rows.reshape(*lead, -1)[..., :n].reshape(*lead, *shape)


def _pack(pieces, dtype):
    return jnp.concatenate([_as_slab_rows(p.astype(dtype)) for p in pieces], axis=0)


def _unpack(slab, shapes, packed_as):
    out, r0 = [], 0
    for s in shapes:
        rows = _rows_of(s, packed_as)
        out.append(_from_slab_rows(slab[r0:r0 + rows], s))
        r0 += rows
    return out


def _pack_group(blocks, layer, group):
    return _pack([blocks[n][layer] if BIG_INFO[n][1] == 0 else blocks[n][layer].T for n in GROUPS[group]], BF16)


def _full_weights(gathered, group):
    out, r0 = {}, 0
    for name in GROUPS[group]:
        r, k = _row_form(*BIG_INFO[name])
        rows = _rows_of((r, k), BF16)
        out[name] = _from_slab_rows(gathered[:, r0:r0 + rows], (r, k), lead=(N_DEV,)).reshape(N_DEV * r, k)
        r0 += rows
    return out


def _scatter_layout(grads, group):
    parts = []
    for name in GROUPS[group]:
        r, k = _row_form(*BIG_INFO[name])
        parts.append(_as_slab_rows(grads[name].astype(BF16).reshape(N_DEV, r, k), lead=(N_DEV,)))
    return jnp.concatenate(parts, axis=1)


def _unpack_blocks(slab, group):
    out = _unpack(slab, [_row_form(*BIG_INFO[n]) for n in GROUPS[group]], BF16)
    return {n: b if BIG_INFO[n][1] == 0 else b.T for n, b in zip(GROUPS[group], out)}


def _split_w_in(w_in):
    edges = np.cumsum((0,) + SEG)
    return [w_in[edges[i]:edges[i + 1]] for i in range(4)]


def _conv_taps(conv_dw):
    return jnp.pad(conv_dw.astype(F32).T, ((0, 32 - CONV_K), (0, 0)))


def _layer_fwd(l, x, h, nm, get_w, sm, bias_tabs):
    t = x.shape[0]
    wts = get_w(l, "mix", h)
    w_seg = _split_w_in(wts["w_in"])
    zc = _mm(h, w_seg[0], "fwd_in_conv", tb=True)
    za = _mm(h, w_seg[1], "fwd_in_att", tb=True, tn=1152)
    zq = _mm(h, w_seg[2], "fwd_in_memq", tb=True)
    zg = _mm(h, w_seg[3], "fwd_in_gate", tb=True)
    y, cact = _conv_fwd(zc, _conv_taps(wts["conv_dw"]), sm["conv_dw_bias"][l], sm["conv_ln_g"][l], sm["conv_ln_b"][l],
                        "conv_fwd")
    qkv, outs, lses = [], [], []
    for g, (_, dil) in enumerate(ATT_PATTERNS):
        cols = tuple(i * 3 + g for i in range(3))
        if dil == 1:
            trio, at = (za, za, za), cols
        else:
            trio, at = _to_residue_major([(za, c) for c in cols], dil, f"qkv_residue_d{dil}"), (0, 0, 0)
        o, ls = _att_fwd(*trio, bias_tabs[g], t // dil, f"att_fwd_d{dil}", cols=at)
        qkv.append((trio, at))
        outs.append(o)
        lses.append(ls)
    att, lse = _att_combine(outs, lses, "att_combine")
    kv = _mm(nm, wts["w_mem_kv"], "fwd_mem_kv")
    om = _mem_fwd(zq, kv, "mem_fwd")
    ys = (_mm(cact, wts["w_conv_out"], "fwd_conv_out", tb=True, tm=2048),
          _mm(att, wts["w_att_out"], "fwd_att_out", tb=True, tm=2048),
          _mm(om, wts["w_mem_out"], "fwd_mem_out", tb=True, tm=2048))
    merged = _gate_fwd(zg, sm["b_gate"][l], ys, "gate_fwd")
    ymix, x1, h2 = _mm_norm_res(merged, wts["w_out"], x, sm["norm_mix_post"][l], sm["norm_ffn_pre"][l], "fwd_out_norm")
    wff = get_w(l, "ffn", h2)
    gff, uff, act = _mm_swiglu_fwd(h2, wff["w_ffn_in"], "fwd_ffn_in_swiglu")
    g_next = sm["norm_mix_pre"][l + 1] if l + 1 < DEPTH else None
    f, x2, h_next = _mm_norm_res(act, wff["w_ffn_out"], x1, sm["norm_ffn_post"][l], g_next,
                                 "fwd_ffn_out_norm" if g_next is not None else "fwd_ffn_out_last")
    saved = dict(x=x, h=h, zc=zc, zq=zq, zg=zg, y=y, cact=cact, qkv=qkv, att=att, lse=lse, kv=kv, om=om, ys=ys,
                 merged=merged, ymix=ymix, x1=x1, h2=h2, gff=gff, uff=uff, act=act, f=f)
    return x2, h_next, saved


def _after(value, *tokens):
    for t in tokens:
        if t is not None:
            value = value + t[0, 0].astype(value.dtype)
    return value


def _layer_bwd(l, dx2, s, nm, mem, get_w, put_g, sm, bias_tabs, bucket_idx):
    t = dx2.shape[0]
    wts, wff = get_w(l, "mix", None), get_w(l, "ffn", None)
    gb, gf, gs = {}, {}, {}
    df, gs["norm_ffn_post"] = _norm_bwd(s["f"], sm["norm_ffn_post"][l], dx2, None, BF16, "bwd_norm_ffn_post")
    dg, du = _mm_swiglu_bwd(df, wff["w_ffn_out"], s["gff"], s["uff"], "bwd_ffn_out_d_swiglu")
    gf["w_ffn_out"] = _mm(s["act"], df, "bwd_ffn_out_w", ta=True, tm=1408)
    gf["w_ffn_in"] = jnp.concatenate([_mm(dg, s["h2"], "bwd_ffn_in_w_gate", ta=True, tm=1408),
                                      _mm(du, s["h2"], "bwd_ffn_in_w_up", ta=True, tm=1408)], axis=0)
    tok = put_g(l, "ffn", gf)
    dx1, gs["norm_ffn_pre"] = _mm_sum_norm_bwd(
        [(dg, wff["w_ffn_in"][:FFN_H], FFN_H // 2), (du, wff["w_ffn_in"][FFN_H:], FFN_H // 2)], s["x1"],
        _after(sm["norm_ffn_pre"][l], tok), dx2, "bwd_ffn_in_d_norm")
    dymix, gs["norm_mix_post"] = _norm_bwd(s["ymix"], sm["norm_mix_post"][l], dx1, None, BF16, "bwd_norm_mix_post")
    dmerged = _mm(dymix, wts["w_out"], "bwd_out_d", tb=True)
    gb["w_out"] = _mm(s["merged"], dymix, "bwd_out_w", ta=True)
    dys, dzg, gs["b_gate"] = _gate_bwd(s["zg"], sm["b_gate"][l], s["ys"], dmerged, "gate_bwd")
    dcact = _mm(dys[0], wts["w_conv_out"], "bwd_conv_out_d", tm=2048)
    gb["w_conv_out"] = _mm(dys[0], s["cact"], "bwd_conv_out_w", ta=True)
    dzc, dconv, gs["conv_dw_bias"], gs["conv_ln_g"], gs["conv_ln_b"] = _conv_bwd(
        s["zc"], s["y"], dcact, _conv_taps(wts["conv_dw"]), sm["conv_ln_g"][l], sm["conv_ln_b"][l], "conv_bwd")
    gb["conv_dw"] = dconv[:CONV_K].T
    datt = _mm(dys[1], wts["w_att_out"], "bwd_att_out_d", tm=2048)
    gb["w_att_out"] = _mm(dys[1], s["att"], "bwd_att_out_w", ta=True)
    dqkv, rel = [], []
    for g, (_, dil) in enumerate(ATT_PATTERNS):
        trio, at = s["qkv"][g]
        if dil == 1:
            o_r, do_r, lse_r = s["att"], datt, s["lse"]
        else:
            o_r, do_r, lse_r = _to_residue_major([(s["att"], 0), (datt, 0), (s["lse"], 0)], dil, f"att_bwd_residue_d{dil}")
        dq, dk, dv, dbias = _att_bwd(*trio, bias_tabs[g], o_r, do_r, lse_r, t // dil, f"att_bwd_d{dil}", cols=at)
        dqkv.append((dq, dk, dv))
        rel.append(_bias_grad(dbias, bucket_idx[g], f"bias_grad_d{dil}")[:HEADS, :NUM_BUCKETS])
    dza = _assemble_dza(dqkv, "att_bwd_assemble")
    gs["rel_bias"] = jnp.concatenate(rel, axis=0).T
    dom = _mm(dys[2], wts["w_mem_out"], "bwd_mem_out_d", tm=2048)
    gb["w_mem_out"] = _mm(dys[2], s["om"], "bwd_mem_out_w", ta=True)
    dzq, dkv = _mem_bwd(s["zq"], s["kv"], dom, "mem_bwd")
    dkv = dkv.astype(BF16)
    gb["w_mem_kv"] = _mm(nm, dkv, "bwd_mem_kv_w", ta=True)
    dnm = _mm(dkv, wts["w_mem_kv"], "bwd_mem_kv_d", tb=True, out_dtype=F32)
    _, gs["norm_mem"] = _norm_bwd(mem, sm["norm_mem"][l], dnm, None, BF16, "bwd_norm_mem")
    segs = ((dzc, "conv", 1024), (dza, "att", 1152), (dzq, "memq", 512), (dzg, "gate", 1024))
    gb["w_in"] = jnp.concatenate([_mm(dz, s["h"], f"bwd_in_{nm_}_w", ta=True, tm=min(blk, 1152), tk=2048)
                                  for dz, nm_, blk in segs], axis=0)
    tok = put_g(l, "mix", gb)
    w_seg = _split_w_in(wts["w_in"])
    w_seg[0] = _after(w_seg[0], tok)
    dx, gs["norm_mix_pre"] = _mm_sum_norm_bwd([(dz, w_seg[i], blk) for i, (dz, _, blk) in enumerate(segs)], s["x"],
                                              sm["norm_mix_pre"][l], dx1, "bwd_in_d_norm")
    return dx, gs


def _local_step(x, mem, target, sm, get_w, put_g, tokens=()):
    t = x.shape[0]
    bias_tabs, bucket_idx = [], []
    for g, (_, dil) in enumerate(ATT_PATTERNS):
        idx = _bucket_index(_att_tiles(t // dil)[1], dil)
        bucket_idx.append(jnp.asarray(idx))
        bias_tabs.append(_bias_table(sm["rel_bias"][:, g * HEADS:(g + 1) * HEADS], bucket_idx[g], f"bias_table_d{dil}"))
    h = _norm_plain(x, _after(sm["norm_mix_pre"][0], *tokens), "norm_first")
    saved, nms = [], []
    for l in range(DEPTH):
        nm = _norm_plain(mem, sm["norm_mem"][l], "norm_mem")
        x, h, s = _layer_fwd(l, x, h, nm, get_w, sm, bias_tabs)
        saved.append(s)
        nms.append(nm)
    loss, dx = _loss_head(x, target, "loss_head")
    gsmall = {}
    for l in reversed(range(DEPTH)):
        dx, gs = _layer_bwd(l, dx, saved[l], nms[l], mem, get_w, put_g, sm, bias_tabs, bucket_idx)
        for n, v in gs.items():
            gsmall.setdefault(n, {})[l] = v
    small = {}
    for n, _ in SMALL:
        small[n] = gsmall[n][0] + gsmall[n][1] if n == "rel_bias" else jnp.stack([gsmall[n][0], gsmall[n][1]])
    return loss, dx, small


def kernel(x, mem, rel_bias, norm_mix_pre, w_in, b_gate, conv_dw, conv_dw_bias, conv_ln_g, conv_ln_b, w_conv_out, w_att_out, norm_mem, w_mem_kv, w_mem_out, w_out, norm_mix_post, norm_ffn_pre, w_ffn_in, w_ffn_out, norm_ffn_post, loss_target, m_rel_bias, m_norm_mix_pre, m_w_in, m_b_gate, m_conv_dw, m_conv_dw_bias, m_conv_ln_g, m_conv_ln_b, m_w_conv_out, m_w_att_out, m_norm_mem, m_w_mem_kv, m_w_mem_out, m_w_out, m_norm_mix_post, m_norm_ffn_pre, m_w_ffn_in, m_w_ffn_out, m_norm_ffn_post, v_rel_bias, v_norm_mix_pre, v_w_in, v_b_gate, v_conv_dw, v_conv_dw_bias, v_conv_ln_g, v_conv_ln_b, v_w_conv_out, v_w_att_out, v_norm_mem, v_w_mem_kv, v_w_mem_out, v_w_out, v_norm_mix_post, v_norm_ffn_pre, v_w_ffn_in, v_w_ffn_out, v_norm_ffn_post):
    args = dict(locals())
    w = {n: args[n] for n in TWIN_WEIGHTS}
    m = {n: args["m_" + n] for n in TWIN_WEIGHTS}
    v = {n: args["v_" + n] for n in TWIN_WEIGHTS}
    sm = {n: w[n] for n, _ in SMALL}

    first = (0, "mix")
    keys = [(l, grp) for l in range(DEPTH) for grp in GROUPS]
    shard = {(l, grp): _pack_group(w, l, grp) for l, grp in keys}
    weights = {first: _full_weights(_all_gather(shard[first], "gather_l0_mix"), "mix")}
    settled = weights[first]["w_out"][:1, :1] * 0
    ag = {k: _exchange_start(_after(shard[k], settled), True, f"ag_start_l{k[0]}_{k[1]}") for k in keys if k != first}

    def get_w(l, grp, after):
        if (l, grp) not in weights:
            mine, land = _exchange_wait(ag[l, grp], after, True, f"ag_wait_l{l}_{grp}")
            weights[l, grp] = _full_weights(_own_slot(land, mine), grp)
        return weights[l, grp]

    rs = {}

    def put_g(l, grp, grads):
        rs[l, grp] = _exchange_start(_scatter_layout(grads, grp), False, f"rs_start_l{l}_{grp}")
        return rs[l, grp][4]

    loss, dx, gsmall = _local_step(x[0], mem[0], loss_target[0], sm, get_w, put_g, [st[4] for st in ag.values()])
    loss = lax.psum(loss, ("x", "y", "c"))

    xi, yi, ci = _place()
    me = 4 * xi + 2 * yi + ci
    g_slab = {}
    for k, st in rs.items():
        contrib, land = _exchange_wait(st, dx, False, f"rs_wait_l{k[0]}_{k[1]}")
        own = lax.dynamic_index_in_dim(contrib, me, axis=0, keepdims=False)
        g_slab[k] = _sum_slots(_own_slot(land, own), F32, f"rs_sum_l{k[0]}_{k[1]}")
    g_layers = [{**_unpack_blocks(g_slab[l, "mix"], "mix"), **_unpack_blocks(g_slab[l, "ffn"], "ffn")} for l in range(DEPTH)]
    small_shapes = [s for _, s in SMALL]
    g_small = _unpack(_sum_slots(_all_gather(_pack([gsmall[n] for n, _ in SMALL], F32), "gather_small"), F32, "sum_small"),
                      small_shapes, F32)
    grads = {n: jnp.stack([g_layers[0][n], g_layers[1][n]]) for n, _, _ in BIG}
    grads.update({n: g_small[i] for i, (n, _) in enumerate(SMALL)})

    delta, new_m, new_v = {}, {}, {}
    for n in TWIN_WEIGHTS:
        delta[n], new_m[n], new_v[n] = _adamw(w[n], grads[n], m[n], v[n], f"adamw_{n}")

    return (loss, dx[None], *[grads[n] for n in TWIN_WEIGHTS], *[delta[n] for n in TWIN_WEIGHTS],
            *[new_m[n] for n in TWIN_WEIGHTS], *[new_v[n] for n in TWIN_WEIGHTS])
```

```python
import functools

import numpy as np
import jax
import jax.numpy as jnp
from jax import lax
from jax.experimental import pallas as pl
from jax.experimental.pallas import tpu as pltpu

F32 = jnp.float32
BF16 = jnp.bfloat16

N_DEV = 8
D = 1024
DEPTH = 2
CONV_W = 512
CONV_K = 31
CONV_PAD = 16
ATT_PATTERNS = ((128, 1), (512, 4), (2048, 16))
ATT_RADIUS = 64
HEADS = 4
HEAD_DIM = 64
ATT_GW = HEADS * HEAD_DIM
ATT_W = 3 * ATT_GW
MEM_HEADS = 4
MEM_HD = 128
MEM_W = 512
N_MEM = 256
FFN_H = 2816
NUM_BUCKETS = 32
MAX_DISTANCE = 1024
RMS_EPS = 1e-6
LN_EPS = 1e-5
NEG_INF = -1e30
SEG = (2 * CONV_W, 3 * ATT_W, MEM_W, 3 * D)
ADAM_LR, ADAM_B1, ADAM_B2, ADAM_EPS, ADAM_WD, ADAM_STEP = 0.001, 0.9, 0.999, 1e-08, 0.01, 10

VMEM_LIMIT_V7X = 56 * 1024 * 1024
MESH = pl.DeviceIdType.MESH


def _cp(n_axes):
    return pltpu.CompilerParams(dimension_semantics=("arbitrary",) * n_axes, vmem_limit_bytes=VMEM_LIMIT_V7X)


def _row(v):
    return v.reshape(1, -1)


def _mm(a, b, name, *, ta=False, tb=False, out_dtype=BF16, tm=2048, tn=1024, tk=2048):
    kdim, m = a.shape if ta else a.shape[::-1]
    n, kb = b.shape if tb else b.shape[::-1]
    assert kb == kdim and a.dtype == BF16 and b.dtype == BF16
    tm, tn, tk = min(tm, m), min(tn, n), min(tk, kdim)
    assert m % tm == 0 and n % tn == 0 and kdim % tk == 0, (name, m, n, kdim, tm, tn, tk)
    nk = kdim // tk
    a_spec = pl.BlockSpec((tk, tm), lambda i, j, k: (k, i)) if ta else pl.BlockSpec((tm, tk), lambda i, j, k: (i, k))
    b_spec = pl.BlockSpec((tn, tk), lambda i, j, k: (j, k)) if tb else pl.BlockSpec((tk, tn), lambda i, j, k: (k, j))
    o_spec = pl.BlockSpec((tm, tn), lambda i, j, k: (i, j))
    dims = (((0 if ta else 1,), (1 if tb else 0,)), ((), ()))
    use_scratch = nk > 1 and out_dtype != F32

    def body(*refs):
        a_ref, b_ref, o_ref = refs[:3]
        p = lax.dot_general(a_ref[...], b_ref[...], dims, preferred_element_type=F32)
        if nk == 1:
            o_ref[...] = p.astype(o_ref.dtype)
            return
        k = pl.program_id(2)
        t_ref = refs[3] if use_scratch else o_ref

        @pl.when(k == 0)
        def _():
            t_ref[...] = p

        @pl.when(k > 0)
        def _():
            t_ref[...] += p

        if use_scratch:
            @pl.when(k == nk - 1)
            def _():
                o_ref[...] = t_ref[...].astype(o_ref.dtype)

    return pl.pallas_call(
        body, name=name, grid=(m // tm, n // tn, nk), in_specs=[a_spec, b_spec], out_specs=o_spec,
        out_shape=jax.ShapeDtypeStruct((m, n), out_dtype),
        scratch_shapes=[pltpu.VMEM((tm, tn), F32)] if use_scratch else [],
        compiler_params=_cp(3),
    )(a, b)


def _rms_bwd_val(v, g, dy):
    r = lax.rsqrt(jnp.mean(v * v, axis=-1, keepdims=True) + RMS_EPS)
    vhat = v * r
    dvh = dy * g
    dv = r * (dvh - vhat * jnp.mean(dvh * vhat, axis=-1, keepdims=True))
    return dv, jnp.sum(dy * vhat, axis=0, keepdims=True)


def _mm_sum(pairs, name, *, norm=None, tm=1024):
    m, n = pairs[0][0].shape[0], pairs[0][1].shape[1]
    tm = min(tm, m)
    starts, specs, total = [], [], 0
    for a, b, tk in pairs:
        assert a.shape == (m, b.shape[0]) and b.shape[1] == n and a.shape[1] % tk == 0 and a.dtype == b.dtype == BF16
        nk = a.shape[1] // tk
        starts.append((total, nk))
        step = functools.partial(lambda k, s0, nk_: jnp.clip(k - s0, 0, nk_ - 1), s0=total, nk_=nk)
        specs.append(pl.BlockSpec((tm, tk), functools.partial(lambda i, k, st: (i, st(k)), st=step)))
        specs.append(pl.BlockSpec((tk, n), functools.partial(lambda i, k, st: (st(k), 0), st=step)))
        total += nk
    np_ = 2 * len(pairs)
    row = pl.BlockSpec((tm, n), lambda i, k: (i, 0))

    def accumulate(refs, acc, k):
        for p, (s0, nk) in enumerate(starts):
            @pl.when((k >= s0) & (k < s0 + nk))
            def _(p=p, s0=s0):
                part = jnp.dot(refs[2 * p][...], refs[2 * p + 1][...], preferred_element_type=F32)
                if s0 == 0:
                    @pl.when(k == 0)
                    def _():
                        acc[...] = part

                    @pl.when(k > 0)
                    def _():
                        acc[...] += part
                else:
                    acc[...] += part

    operands = [op for a, b, _ in pairs for op in (a, b)]
    if norm is None:
        def plain(*refs):
            accumulate(refs, refs[np_], pl.program_id(1))

        return pl.pallas_call(
            plain, name=name, grid=(m // tm, total), in_specs=specs, out_specs=row,
            out_shape=jax.ShapeDtypeStruct((m, n), F32), compiler_params=_cp(2),
        )(*operands)

    v, g, dres = norm

    def body(*refs):
        v_ref, g_ref, dres_ref, dv_ref, dg_ref, acc = refs[np_:np_ + 6]
        i, k = pl.program_id(0), pl.program_id(1)
        accumulate(refs, acc, k)

        @pl.when((i == 0) & (k == 0))
        def _():
            dg_ref[...] = jnp.zeros_like(dg_ref)

        @pl.when(k == total - 1)
        def _():
            dv, dg = _rms_bwd_val(v_ref[...], g_ref[...], acc[...])
            dv_ref[...] = dv + dres_ref[...]
            dg_ref[0:1, :] += dg

    once = pl.BlockSpec((tm, n), lambda i, k: (i, 0), pipeline_mode=pl.Buffered(1))
    dv, dg = pl.pallas_call(
        body, name=name, grid=(m // tm, total),
        in_specs=specs + [once, pl.BlockSpec((1, n), lambda i, k: (0, 0)), once],
        out_specs=[row, pl.BlockSpec((8, n), lambda i, k: (0, 0))],
        out_shape=[jax.ShapeDtypeStruct((m, n), F32), jax.ShapeDtypeStruct((8, n), F32)],
        scratch_shapes=[pltpu.VMEM((tm, n), F32)], compiler_params=_cp(2),
    )(*operands, v, _row(g), dres)
    return dv, dg[0]


def _mm_swiglu_fwd(h, w, name, *, tm=1024, tn=1408):
    m, kdim = h.shape
    hid = w.shape[0] // 2
    tm = min(tm, m)
    nj = hid // tn
    dims = (((1,), (1,)), ((), ()))

    def body(h_ref, wg_ref, wu_ref, g_ref, u_ref, a_ref):
        g = lax.dot_general(h_ref[...], wg_ref[...], dims, preferred_element_type=F32)
        u = lax.dot_general(h_ref[...], wu_ref[...], dims, preferred_element_type=F32)
        g_ref[...] = g.astype(BF16)
        u_ref[...] = u.astype(BF16)
        a_ref[...] = (g * _sigmoid(g) * u).astype(BF16)

    out = pl.BlockSpec((tm, tn), lambda i, j: (i, j))
    return pl.pallas_call(
        body, name=name, grid=(m // tm, nj),
        in_specs=[pl.BlockSpec((tm, kdim), lambda i, j: (i, 0)), pl.BlockSpec((tn, kdim), lambda i, j: (j, 0)),
                  pl.BlockSpec((tn, kdim), lambda i, j: (j + nj, 0))],
        out_specs=[out, out, out], out_shape=[jax.ShapeDtypeStruct((m, hid), BF16)] * 3, compiler_params=_cp(2),
    )(h, w, w)


def _mm_swiglu_bwd(df, w, g, u, name, *, tm=1024, tn=1408):
    m, kdim = df.shape
    hid = w.shape[0]
    tm = min(tm, m)

    def body(df_ref, w_ref, g_ref, u_ref, dg_ref, du_ref):
        da = lax.dot_general(df_ref[...], w_ref[...], (((1,), (1,)), ((), ())), preferred_element_type=F32)
        gg, uu = g_ref[...].astype(F32), u_ref[...].astype(F32)
        sg = _sigmoid(gg)
        dg_ref[...] = (da * uu * (sg * (1.0 + gg * (1.0 - sg)))).astype(BF16)
        du_ref[...] = (da * gg * sg).astype(BF16)

    blk = pl.BlockSpec((tm, tn), lambda i, j: (i, j))
    return pl.pallas_call(
        body, name=name, grid=(m // tm, hid // tn),
        in_specs=[pl.BlockSpec((tm, kdim), lambda i, j: (i, 0)), pl.BlockSpec((tn, kdim), lambda i, j: (j, 0)), blk, blk],
        out_specs=[blk, blk], out_shape=[jax.ShapeDtypeStruct((m, hid), BF16)] * 2, compiler_params=_cp(2),
    )(df, w, g, u)


def _mm_norm_res(a, b, xres, g, g_next, name, *, tm=512):
    m, kdim = a.shape
    n = b.shape[1]
    tm = min(tm, m)
    two = g_next is not None

    def body(*refs):
        a_ref, b_ref, x_ref, g_ref = refs[:4]
        y_ref, xn_ref = refs[4 + two], refs[5 + two]
        y = jnp.dot(a_ref[...], b_ref[...], preferred_element_type=F32)
        y_ref[...] = y
        xn = x_ref[...] + _rms_val(y, g_ref[...])
        xn_ref[...] = xn
        if two:
            refs[7][...] = _rms_val(xn, refs[4][...]).astype(BF16)

    row = pl.BlockSpec((tm, n), lambda i: (i, 0))
    vec = pl.BlockSpec((1, n), lambda i: (0, 0))
    outs = pl.pallas_call(
        body, name=name, grid=(m // tm,),
        in_specs=[pl.BlockSpec((tm, kdim), lambda i: (i, 0)), pl.BlockSpec((kdim, n), lambda i: (0, 0)), row, vec] + ([vec] if two else []),
        out_specs=[row, row] + ([row] if two else []),
        out_shape=[jax.ShapeDtypeStruct((m, n), F32)] * 2 + ([jax.ShapeDtypeStruct((m, n), BF16)] if two else []),
        compiler_params=_cp(1),
    )(*((a, b, xres, _row(g)) + ((_row(g_next),) if two else ())))
    return (outs[0], outs[1], outs[2]) if two else (outs[0], outs[1], None)


def _rms_val(v, g):
    return v * lax.rsqrt(jnp.mean(v * v, axis=-1, keepdims=True) + RMS_EPS) * g


def _norm_plain(v, g, name, tr=512):
    t, d = v.shape
    tr = min(tr, t)

    def body(v_ref, g_ref, o_ref):
        o_ref[...] = _rms_val(v_ref[...], g_ref[...]).astype(BF16)

    return pl.pallas_call(
        body, name=name, grid=(t // tr,),
        in_specs=[pl.BlockSpec((tr, d), lambda i: (i, 0)), pl.BlockSpec((1, d), lambda i: (0, 0))],
        out_specs=pl.BlockSpec((tr, d), lambda i: (i, 0)),
        out_shape=jax.ShapeDtypeStruct((t, d), BF16), compiler_params=_cp(1),
    )(v, _row(g))


def _norm_bwd(v, g, dout, dres, out_dtype, name, tr=512):
    t, d = v.shape
    tr = min(tr, t)
    nt = t // tr
    has_res = dres is not None

    def body(*refs):
        v_ref, g_ref, do_ref = refs[:3]
        dv_ref, dg_ref = refs[3 + has_res], refs[4 + has_res]
        i = pl.program_id(0)
        vv = v_ref[...]
        dy = do_ref[...].astype(F32)
        r = lax.rsqrt(jnp.mean(vv * vv, axis=-1, keepdims=True) + RMS_EPS)
        vhat = vv * r
        dvh = dy * g_ref[...]
        dv = r * (dvh - vhat * jnp.mean(dvh * vhat, axis=-1, keepdims=True))
        if has_res:
            dv = dv + refs[3][...]
        dv_ref[...] = dv.astype(dv_ref.dtype)
        part = jnp.sum(dy * vhat, axis=0, keepdims=True)

        @pl.when(i == 0)
        def _():
            dg_ref[...] = jnp.zeros_like(dg_ref)

        dg_ref[0:1, :] += part

    row = pl.BlockSpec((tr, d), lambda i: (i, 0))
    dv, dg = pl.pallas_call(
        body, name=name, grid=(nt,),
        in_specs=[row, pl.BlockSpec((1, d), lambda i: (0, 0)), row] + ([row] if has_res else []),
        out_specs=[row, pl.BlockSpec((8, d), lambda i: (0, 0))],
        out_shape=[jax.ShapeDtypeStruct((t, d), out_dtype), jax.ShapeDtypeStruct((8, d), F32)],
        compiler_params=_cp(1),
    )(*((v, _row(g), dout) + ((dres,) if has_res else ())))
    return dv, dg[0]


def _loss_head(y, target, name, tr=512):
    t, d = y.shape
    tr = min(tr, t)
    nt = t // tr

    def body(y_ref, t_ref, dy_ref, l_ref):
        i = pl.program_id(0)
        err = y_ref[...] - t_ref[...]
        dy_ref[...] = err * (1.0 / d)

        @pl.when(i == 0)
        def _():
            l_ref[...] = jnp.zeros_like(l_ref)

        l_ref[...] += jnp.sum(err * err) * (0.5 / d)

    row = pl.BlockSpec((tr, d), lambda i: (i, 0))
    dy, l = pl.pallas_call(
        body, name=name, grid=(nt,), in_specs=[row, row],
        out_specs=[row, pl.BlockSpec((8, 128), lambda i: (0, 0))],
        out_shape=[jax.ShapeDtypeStruct((t, d), F32), jax.ShapeDtypeStruct((8, 128), F32)],
        compiler_params=_cp(1),
    )(y, target)
    return l[0, 0], dy


def _sigmoid(v):
    return 1.0 / (1.0 + jnp.exp(-v))


def _halo_specs(tq, width, nt, halo):
    per = tq // halo
    last = nt * per - 1
    main = pl.BlockSpec((tq, width), lambda i: (i, 0))
    prev = pl.BlockSpec((halo, width), lambda i: (jnp.maximum(i * per - 1, 0), 0))
    nxt = pl.BlockSpec((halo, width), lambda i: (jnp.minimum((i + 1) * per, last), 0))
    return prev, main, nxt


def _glu_window(zp, zm, zn, i, nt):
    def glu(z):
        z = z.astype(F32)
        return z[:, :CONV_W] * _sigmoid(z[:, CONV_W:])

    up = jnp.where(i > 0, glu(zp), 0.0)
    un = jnp.where(i < nt - 1, glu(zn), 0.0)
    return jnp.concatenate([up, glu(zm), un], axis=0)


def _shifted(win, shift, rows):
    if shift == 0:
        return win[:rows]
    return pltpu.roll(win, win.shape[0] - shift, 0)[:rows]


def _conv_fwd(zc, w, b, ln_g, ln_b, name, tq=512):
    t = zc.shape[0]
    tq = min(tq, t)
    nt = t // tq

    def body(zp_ref, zm_ref, zn_ref, w_ref, b_ref, g_ref, bb_ref, y_ref, c_ref):
        i = pl.program_id(0)
        win = _glu_window(zp_ref[...], zm_ref[...], zn_ref[...], i, nt)
        wv = w_ref[...]
        y = jnp.zeros((tq, CONV_W), F32) + b_ref[...]
        for k in range(CONV_K):
            y = y + _shifted(win, k + 1, tq) * wv[k:k + 1, :]
        y_ref[...] = y
        mu = jnp.mean(y, axis=-1, keepdims=True)
        yc = y - mu
        ln = yc * lax.rsqrt(jnp.mean(yc * yc, axis=-1, keepdims=True) + LN_EPS) * g_ref[...] + bb_ref[...]
        c_ref[...] = (ln * _sigmoid(ln)).astype(BF16)

    vec = pl.BlockSpec((1, CONV_W), lambda i: (0, 0))
    out = pl.BlockSpec((tq, CONV_W), lambda i: (i, 0))
    return pl.pallas_call(
        body, name=name, grid=(nt,),
        in_specs=[*_halo_specs(tq, 2 * CONV_W, nt, CONV_PAD), pl.BlockSpec((32, CONV_W), lambda i: (0, 0)), vec, vec, vec],
        out_specs=[out, out],
        out_shape=[jax.ShapeDtypeStruct((t, CONV_W), F32), jax.ShapeDtypeStruct((t, CONV_W), BF16)],
        compiler_params=_cp(1),
    )(zc, zc, zc, w, _row(b), _row(ln_g), _row(ln_b))


def _conv_bwd(zc, y, dc, w, ln_g, ln_b, name, tq=256):
    t = zc.shape[0]
    tq = min(tq, t)
    nt = t // tq
    rows = tq + 2 * CONV_PAD

    def body(zp_ref, zm_ref, zn_ref, yp_ref, ym_ref, yn_ref, dp_ref, dm_ref, dn_ref, w_ref, g_ref, bb_ref,
             dz_ref, dw_ref, db_ref, dg_ref, dbb_ref):
        i = pl.program_id(0)
        uwin = _glu_window(zp_ref[...], zm_ref[...], zn_ref[...], i, nt)
        ywin = jnp.concatenate([yp_ref[...], ym_ref[...], yn_ref[...]], axis=0)
        dcw = jnp.concatenate([dp_ref[...], dm_ref[...], dn_ref[...]], axis=0).astype(F32)
        mu = jnp.mean(ywin, axis=-1, keepdims=True)
        yc = ywin - mu
        rstd = lax.rsqrt(jnp.mean(yc * yc, axis=-1, keepdims=True) + LN_EPS)
        yhat = yc * rstd
        ln = yhat * g_ref[...] + bb_ref[...]
        sg = _sigmoid(ln)
        dl = dcw * (sg * (1.0 + ln * (1.0 - sg)))
        ridx = lax.broadcasted_iota(jnp.int32, (rows, 1), 0)
        inside = ((ridx >= CONV_PAD) | (i > 0)) & ((ridx < tq + CONV_PAD) | (i < nt - 1))
        dl = jnp.where(inside, dl, 0.0)
        dyh = dl * g_ref[...]
        dy = rstd * (dyh - jnp.mean(dyh, axis=-1, keepdims=True) - yhat * jnp.mean(dyh * yhat, axis=-1, keepdims=True))
        dy = jnp.where(inside, dy, 0.0)
        main = slice(CONV_PAD, CONV_PAD + tq)
        dlm, yhm, dym = dl[main], yhat[main], dy[main]

        @pl.when(i == 0)
        def _():
            dw_ref[...] = jnp.zeros_like(dw_ref)
            db_ref[...] = jnp.zeros_like(db_ref)
            dg_ref[...] = jnp.zeros_like(dg_ref)
            dbb_ref[...] = jnp.zeros_like(dbb_ref)

        dg_ref[0:1, :] += jnp.sum(dlm * yhm, axis=0, keepdims=True)
        dbb_ref[0:1, :] += jnp.sum(dlm, axis=0, keepdims=True)
        db_ref[0:1, :] += jnp.sum(dym, axis=0, keepdims=True)
        wv = w_ref[...]
        du = jnp.zeros((tq, CONV_W), F32)
        for k in range(CONV_K):
            du = du + _shifted(dy, 2 * CONV_PAD - 1 - k, tq) * wv[k:k + 1, :]
            dw_ref[k:k + 1, :] += jnp.sum(dym * _shifted(uwin, k + 1, tq), axis=0, keepdims=True)
        zm = zm_ref[...].astype(F32)
        a, gt = zm[:, :CONV_W], zm[:, CONV_W:]
        sgt = _sigmoid(gt)
        dz_ref[:, :CONV_W] = (du * sgt).astype(BF16)
        dz_ref[:, CONV_W:] = (du * a * sgt * (1.0 - sgt)).astype(BF16)

    vec = pl.BlockSpec((1, CONV_W), lambda i: (0, 0))
    acc = pl.BlockSpec((8, CONV_W), lambda i: (0, 0))
    dz, dw, db, dg, dbb = pl.pallas_call(
        body, name=name, grid=(nt,),
        in_specs=[*_halo_specs(tq, 2 * CONV_W, nt, CONV_PAD), *_halo_specs(tq, CONV_W, nt, CONV_PAD),
                  *_halo_specs(tq, CONV_W, nt, CONV_PAD), pl.BlockSpec((32, CONV_W), lambda i: (0, 0)), vec, vec],
        out_specs=[pl.BlockSpec((tq, 2 * CONV_W), lambda i: (i, 0)), pl.BlockSpec((32, CONV_W), lambda i: (0, 0)), acc, acc, acc],
        out_shape=[jax.ShapeDtypeStruct((t, 2 * CONV_W), BF16), jax.ShapeDtypeStruct((32, CONV_W), F32),
                   jax.ShapeDtypeStruct((8, CONV_W), F32), jax.ShapeDtypeStruct((8, CONV_W), F32),
                   jax.ShapeDtypeStruct((8, CONV_W), F32)],
        compiler_params=_cp(1),
    )(zc, zc, zc, y, y, y, dc, dc, dc, w, _row(ln_g), _row(ln_b))
    return dz, dw, db[0], dg[0], dbb[0]


def _att_tiles(seq_len):
    tq = min(256, seq_len)
    return tq, min(128, tq)


def _t5_bucket_np(rel):
    nb = NUM_BUCKETS // 2
    max_exact = nb // 2
    ret = np.where(rel > 0, nb, 0)
    n = np.abs(rel)
    nf = np.maximum(n, 1).astype(np.float32)
    large = max_exact + (np.log(nf / np.float32(max_exact)) / np.float32(np.log(MAX_DISTANCE / max_exact))
                         * np.float32(nb - max_exact)).astype(np.int32)
    large = np.minimum(large, nb - 1)
    return ret + np.where(n < max_exact, n, large)


def _bucket_index(sb, dilation):
    off = np.arange(sb + 2 * ATT_RADIUS)[None, :] - ATT_RADIUS - np.arange(sb)[:, None]
    idx = _t5_bucket_np(off * dilation).astype(np.int32)
    return np.where(np.abs(off) <= ATT_RADIUS, idx, -1).astype(np.int32)


def _bias_table(tab, idx, name):
    sb, w = idx.shape

    def body(tab_ref, idx_ref, o_ref):
        ix = idx_ref[...]
        for h in range(HEADS):
            acc = jnp.full((sb, w), NEG_INF, F32)
            for b in range(NUM_BUCKETS):
                acc = jnp.where(ix == b, tab_ref[b, h], acc)
            o_ref[h] = acc

    return pl.pallas_call(
        body, name=name, out_shape=jax.ShapeDtypeStruct((HEADS, sb, w), F32),
        in_specs=[pl.BlockSpec(memory_space=pltpu.SMEM), pl.BlockSpec(memory_space=pltpu.VMEM)],
        out_specs=pl.BlockSpec(memory_space=pltpu.VMEM),
    )(tab, idx)


def _qkv_specs(tq, per, last_blk, cols, tile):
    def main(col):
        return pl.BlockSpec((tq, ATT_GW), lambda *g: (tile(*g), col))

    def prev(col):
        return pl.BlockSpec((ATT_RADIUS, ATT_GW), lambda *g: (jnp.maximum(tile(*g) * per - 1, 0), col))

    def nxt(col):
        return pl.BlockSpec((ATT_RADIUS, ATT_GW), lambda *g: (jnp.minimum((tile(*g) + 1) * per, last_blk), col))

    cq, ck, cv = cols
    return [main(cq), prev(ck), main(ck), nxt(ck), prev(cv), main(cv), nxt(cv)]


def _att_fwd(q, k, v, bias, seq_len, name, cols=(0, 0, 0)):
    t = q.shape[0]
    tq, sb = _att_tiles(seq_len)
    nbl = seq_len // tq
    nt = t // tq
    per = tq // ATT_RADIUS
    w = sb + 2 * ATT_RADIUS
    last_blk = t // ATT_RADIUS - 1

    def body(q_ref, kp_ref, kc_ref, kn_ref, vp_ref, vc_ref, vn_ref, b_ref, o_ref, l_ref):
        n = pl.program_id(0) % nbl
        kw = jnp.concatenate([kp_ref[...], kc_ref[...], kn_ref[...]], axis=0)
        vw = jnp.concatenate([vp_ref[...], vc_ref[...], vn_ref[...]], axis=0)
        kpos = n * tq - ATT_RADIUS + lax.broadcasted_iota(jnp.int32, (1, tq + 2 * ATT_RADIUS), 1)
        valid = (kpos >= 0) & (kpos < seq_len)
        for j in range(tq // sb):
            qj = q_ref[j * sb:(j + 1) * sb, :]
            kj, vj, okj = kw[j * sb:j * sb + w], vw[j * sb:j * sb + w], valid[:, j * sb:j * sb + w]
            outs, lses = [], []
            for h in range(HEADS):
                hs = slice(h * HEAD_DIM, (h + 1) * HEAD_DIM)
                s = lax.dot_general(qj[:, hs], kj[:, hs], (((1,), (1,)), ((), ())), preferred_element_type=F32)
                s = jnp.where(okj, s * (HEAD_DIM ** -0.5) + b_ref[h], NEG_INF)
                m = jnp.max(s, axis=-1, keepdims=True)
                e = jnp.exp(s - m)
                den = jnp.sum(e, axis=-1, keepdims=True)
                o = jnp.dot(e.astype(BF16), vj[:, hs], preferred_element_type=F32) / den
                outs.append(o)
                lses.append(jnp.broadcast_to(m + jnp.log(den), (sb, HEAD_DIM)))
            o_ref[j * sb:(j + 1) * sb, :] = jnp.concatenate(outs, axis=1).astype(BF16)
            l_ref[j * sb:(j + 1) * sb, :] = jnp.concatenate(lses, axis=1)

    main = pl.BlockSpec((tq, ATT_GW), lambda i: (i, 0))
    return pl.pallas_call(
        body, name=name, grid=(nt,),
        in_specs=[*_qkv_specs(tq, per, last_blk, cols, lambda i: i), pl.BlockSpec((HEADS, sb, w), lambda i: (0, 0, 0))],
        out_specs=[main, main],
        out_shape=[jax.ShapeDtypeStruct((t, ATT_GW), BF16), jax.ShapeDtypeStruct((t, ATT_GW), F32)],
        compiler_params=_cp(1),
    )(q, k, k, k, v, v, v, bias)


PERM_ROWS = 2048
LANES = 128


def _spec_rm(d, tr, width, col=0):
    return pl.BlockSpec((d, tr // d, width), lambda i: (0, i, col))


def _rm_view(a, d):
    return a.reshape(d, a.shape[0] // d, a.shape[1])


def _gather_residues(scr, val, d, out_ref, col0):
    tr, w = val.shape
    for c in range(w // LANES):
        scr[...] = val[:, c * LANES:(c + 1) * LANES].astype(F32)
        for r in range(d):
            out_ref[r, :, col0 + c * LANES:col0 + (c + 1) * LANES] = scr[pl.ds(r, tr // d, stride=d), :].astype(out_ref.dtype)


def _scatter_residues(scr, ref, d):
    w = ref.shape[2]
    cols = []
    for c in range(w // LANES):
        for r in range(d):
            scr[pl.ds(r, scr.shape[0] // d, stride=d), :] = ref[r, :, c * LANES:(c + 1) * LANES].astype(F32)
        cols.append(scr[...])
    return jnp.concatenate(cols, axis=1)


def _to_residue_major(srcs, d, name, tr=PERM_ROWS):
    t = srcs[0][0].shape[0]
    tr = min(tr, t)
    n = len(srcs)

    def body(*refs):
        scr = refs[2 * n]
        for k in range(n):
            _gather_residues(scr, refs[k][...], d, refs[n + k], 0)

    outs = pl.pallas_call(
        body, name=name, grid=(t // tr,),
        in_specs=[pl.BlockSpec((tr, ATT_GW), functools.partial(lambda i, col: (i, col), col=col)) for _, col in srcs],
        out_specs=[_spec_rm(d, tr, ATT_GW)] * n,
        out_shape=[jax.ShapeDtypeStruct((d, t // d, ATT_GW), a.dtype) for a, _ in srcs],
        scratch_shapes=[pltpu.VMEM((tr, LANES), F32)], compiler_params=_cp(1),
    )(*[a for a, _ in srcs])
    return [o.reshape(t, ATT_GW) for o in outs]


def _att_combine(outs, lses, name, tr=PERM_ROWS):
    t = outs[0].shape[0]
    tr = min(tr, t)
    dils = [d for _, d in ATT_PATTERNS]

    def body(o0, o1, o2, l0, l1, l2, a_ref, lse_ref, scr):
        os_, ls_ = [], []
        for o_ref, l_ref, d in zip((o0, o1, o2), (l0, l1, l2), dils):
            if d == 1:
                os_.append(o_ref[...].astype(F32))
                ls_.append(l_ref[...])
            else:
                os_.append(_scatter_residues(scr, o_ref, d))
                ls_.append(_scatter_residues(scr, l_ref, d))
        la, lb, lc = ls_
        m = jnp.maximum(jnp.maximum(la, lb), lc)
        ea, eb, ec = jnp.exp(la - m), jnp.exp(lb - m), jnp.exp(lc - m)
        den = ea + eb + ec
        a_ref[...] = ((ea * os_[0] + eb * os_[1] + ec * os_[2]) / den).astype(BF16)
        lse_ref[...] = m + jnp.log(den)

    row = pl.BlockSpec((tr, ATT_GW), lambda i: (i, 0))
    specs = [row if d == 1 else _spec_rm(d, tr, ATT_GW) for d in dils]
    views = lambda arrs: [a if d == 1 else _rm_view(a, d) for a, d in zip(arrs, dils)]
    return pl.pallas_call(
        body, name=name, grid=(t // tr,), in_specs=specs * 2, out_specs=[row, row],
        out_shape=[jax.ShapeDtypeStruct((t, ATT_GW), BF16), jax.ShapeDtypeStruct((t, ATT_GW), F32)],
        scratch_shapes=[pltpu.VMEM((tr, LANES), F32)], compiler_params=_cp(1),
    )(*views(outs), *views(lses))


def _assemble_dza(dqkv, name, tr=PERM_ROWS):
    t = dqkv[0][0].shape[0]
    tr = min(tr, t)
    dils = [d for _, d in ATT_PATTERNS]

    def body(*refs):
        o_ref, scr = refs[9], refs[10]
        for g, d in enumerate(dils):
            for c in range(3):
                ref = refs[g * 3 + c]
                val = ref[...] if d == 1 else _scatter_residues(scr, ref, d).astype(BF16)
                o_ref[:, (c * 3 + g) * ATT_GW:(c * 3 + g + 1) * ATT_GW] = val

    row = pl.BlockSpec((tr, ATT_GW), lambda i: (i, 0))
    specs = [row if d == 1 else _spec_rm(d, tr, ATT_GW) for d in dils for _ in range(3)]
    args = [a if d == 1 else _rm_view(a, d) for trio, d in zip(dqkv, dils) for a in trio]
    return pl.pallas_call(
        body, name=name, grid=(t // tr,), in_specs=specs, out_specs=pl.BlockSpec((tr, 3 * ATT_W), lambda i: (i, 0)),
        out_shape=jax.ShapeDtypeStruct((t, 3 * ATT_W), BF16),
        scratch_shapes=[pltpu.VMEM((tr, LANES), F32)], compiler_params=_cp(1),
    )(*args)


def _att_bwd(q, k, v, bias, o, do, lse, seq_len, name, cols=(0, 0, 0)):
    t = q.shape[0]
    tq, sb = _att_tiles(seq_len)
    nbl = seq_len // tq
    n_res = t // seq_len
    per = tq // ATT_RADIUS
    w = sb + 2 * ATT_RADIUS
    last_blk = t // ATT_RADIUS - 1
    acc_rows = 2 * tq + ATT_RADIUS
    scale = HEAD_DIM ** -0.5

    def body(q_ref, kp_ref, kc_ref, kn_ref, vp_ref, vc_ref, vn_ref, b_ref, o_ref, do_ref, l_ref,
             dq_ref, dk_ref, dv_ref, db_ref, ak_ref, av_ref):
        r, n = pl.program_id(0), pl.program_id(1)

        @pl.when((r == 0) & (n == 0))
        def _():
            db_ref[...] = jnp.zeros_like(db_ref)

        @pl.when(n == 0)
        def _():
            ak_ref[...] = jnp.zeros_like(ak_ref)
            av_ref[...] = jnp.zeros_like(av_ref)

        @pl.when(n < nbl)
        def _():
            kw = jnp.concatenate([kp_ref[...], kc_ref[...], kn_ref[...]], axis=0)
            vw = jnp.concatenate([vp_ref[...], vc_ref[...], vn_ref[...]], axis=0)
            kpos = n * tq - ATT_RADIUS + lax.broadcasted_iota(jnp.int32, (1, tq + 2 * ATT_RADIUS), 1)
            valid = (kpos >= 0) & (kpos < seq_len)
            for j in range(tq // sb):
                rows = slice(j * sb, (j + 1) * sb)
                qj, doj = q_ref[rows, :], do_ref[rows, :]
                dd = doj.astype(F32) * o_ref[rows, :].astype(F32)
                lj = l_ref[rows, :]
                kj, vj, okj = kw[j * sb:j * sb + w], vw[j * sb:j * sb + w], valid[:, j * sb:j * sb + w]
                dqs, dks, dvs = [], [], []
                for h in range(HEADS):
                    hs = slice(h * HEAD_DIM, (h + 1) * HEAD_DIM)
                    s = lax.dot_general(qj[:, hs], kj[:, hs], (((1,), (1,)), ((), ())), preferred_element_type=F32)
                    s = jnp.where(okj, s * scale + b_ref[h], NEG_INF)
                    p = jnp.exp(s - lj[:, h * HEAD_DIM:h * HEAD_DIM + 1])
                    dp = lax.dot_general(doj[:, hs], vj[:, hs], (((1,), (1,)), ((), ())), preferred_element_type=F32)
                    ds = p * (dp - jnp.sum(dd[:, hs], axis=-1, keepdims=True))
                    db_ref[h] += ds
                    dsb = ds.astype(BF16)
                    dqs.append(jnp.dot(dsb, kj[:, hs], preferred_element_type=F32) * scale)
                    dks.append(lax.dot_general(dsb, qj[:, hs], (((0,), (0,)), ((), ())), preferred_element_type=F32) * scale)
                    dvs.append(lax.dot_general(p.astype(BF16), doj[:, hs], (((0,), (0,)), ((), ())), preferred_element_type=F32))
                dq_ref[rows, :] = jnp.concatenate(dqs, axis=1).astype(BF16)
                a0 = j * sb + tq - ATT_RADIUS
                ak_ref[a0:a0 + w, :] += jnp.concatenate(dks, axis=1)
                av_ref[a0:a0 + w, :] += jnp.concatenate(dvs, axis=1)

        dk_ref[...] = ak_ref[0:tq, :].astype(BF16)
        dv_ref[...] = av_ref[0:tq, :].astype(BF16)
        keep = tq + ATT_RADIUS
        nk, nv = ak_ref[tq:acc_rows, :], av_ref[tq:acc_rows, :]
        ak_ref[0:keep, :] = nk
        av_ref[0:keep, :] = nv
        ak_ref[keep:acc_rows, :] = jnp.zeros((acc_rows - keep, ATT_GW), F32)
        av_ref[keep:acc_rows, :] = jnp.zeros((acc_rows - keep, ATT_GW), F32)

    def tile(r, n):
        return r * nbl + jnp.minimum(n, nbl - 1)

    main = pl.BlockSpec((tq, ATT_GW), lambda r, n: (tile(r, n), 0))
    lag = pl.BlockSpec((tq, ATT_GW), lambda r, n: (r * nbl + jnp.maximum(n - 1, 0), 0))
    bspec = pl.BlockSpec((HEADS, sb, w), lambda r, n: (0, 0, 0))
    return pl.pallas_call(
        body, name=name, grid=(n_res, nbl + 1),
        in_specs=[*_qkv_specs(tq, per, last_blk, cols, tile), bspec, main, main, main],
        out_specs=[main, lag, lag, bspec],
        out_shape=[jax.ShapeDtypeStruct((t, ATT_GW), BF16)] * 3 + [jax.ShapeDtypeStruct((HEADS, sb, w), F32)],
        scratch_shapes=[pltpu.VMEM((acc_rows, ATT_GW), F32), pltpu.VMEM((acc_rows, ATT_GW), F32)],
        compiler_params=_cp(2),
    )(q, k, k, k, v, v, v, bias, o, do, lse)


def _bias_grad(db, idx, name):
    _, sb, w = db.shape

    def body(db_ref, idx_ref, o_ref):
        ix = idx_ref[...]
        lane = lax.broadcasted_iota(jnp.int32, (1, 128), 1)
        rows = []
        for h in range(HEADS):
            d = db_ref[h]
            acc = jnp.zeros((1, 128), F32)
            for b in range(NUM_BUCKETS):
                acc = acc + jnp.where(lane == b, jnp.sum(jnp.where(ix == b, d, 0.0)), 0.0)
            rows.append(acc)
        o_ref[...] = jnp.concatenate(rows + [jnp.zeros((8 - HEADS, 128), F32)], axis=0)

    return pl.pallas_call(
        body, name=name, out_shape=jax.ShapeDtypeStruct((8, 128), F32),
        in_specs=[pl.BlockSpec(memory_space=pltpu.VMEM), pl.BlockSpec(memory_space=pltpu.VMEM)],
        out_specs=pl.BlockSpec(memory_space=pltpu.VMEM),
    )(db, idx)


def _mem_fwd(zq, kv, name, tq=512):
    t = zq.shape[0]
    tq = min(tq, t)
    scale = MEM_HD ** -0.5

    def body(q_ref, kv_ref, o_ref):
        outs = []
        for h in range(MEM_HEADS):
            hs = slice(h * MEM_HD, (h + 1) * MEM_HD)
            kh = kv_ref[:, h * MEM_HD:(h + 1) * MEM_HD]
            vh = kv_ref[:, MEM_W + h * MEM_HD:MEM_W + (h + 1) * MEM_HD]
            s = lax.dot_general(q_ref[:, hs], kh, (((1,), (1,)), ((), ())), preferred_element_type=F32) * scale
            e = jnp.exp(s - jnp.max(s, axis=-1, keepdims=True))
            den = jnp.sum(e, axis=-1, keepdims=True)
            outs.append(jnp.dot(e.astype(BF16), vh, preferred_element_type=F32) / den)
        o_ref[...] = jnp.concatenate(outs, axis=1).astype(BF16)

    row = pl.BlockSpec((tq, MEM_W), lambda i: (i, 0))
    return pl.pallas_call(
        body, name=name, grid=(t // tq,),
        in_specs=[row, pl.BlockSpec((N_MEM, 2 * MEM_W), lambda i: (0, 0))], out_specs=row,
        out_shape=jax.ShapeDtypeStruct((t, MEM_W), BF16), compiler_params=_cp(1),
    )(zq, kv)


def _mem_bwd(zq, kv, do, name, tq=512):
    t = zq.shape[0]
    tq = min(tq, t)
    scale = MEM_HD ** -0.5

    def body(q_ref, kv_ref, do_ref, dq_ref, dkv_ref):
        @pl.when(pl.program_id(0) == 0)
        def _():
            dkv_ref[...] = jnp.zeros_like(dkv_ref)

        dqs = []
        for h in range(MEM_HEADS):
            hs = slice(h * MEM_HD, (h + 1) * MEM_HD)
            vs = slice(MEM_W + h * MEM_HD, MEM_W + (h + 1) * MEM_HD)
            qh, doh = q_ref[:, hs], do_ref[:, hs]
            kh, vh = kv_ref[:, hs], kv_ref[:, vs]
            s = lax.dot_general(qh, kh, (((1,), (1,)), ((), ())), preferred_element_type=F32) * scale
            e = jnp.exp(s - jnp.max(s, axis=-1, keepdims=True))
            p = e / jnp.sum(e, axis=-1, keepdims=True)
            dp = lax.dot_general(doh, vh, (((1,), (1,)), ((), ())), preferred_element_type=F32)
            ds = p * (dp - jnp.sum(p * dp, axis=-1, keepdims=True))
            dsb = ds.astype(BF16)
            dqs.append(jnp.dot(dsb, kh, preferred_element_type=F32) * scale)
            dkv_ref[:, hs] += lax.dot_general(dsb, qh, (((0,), (0,)), ((), ())), preferred_element_type=F32) * scale
            dkv_ref[:, vs] += lax.dot_general(p.astype(BF16), doh, (((0,), (0,)), ((), ())), preferred_element_type=F32)
        dq_ref[...] = jnp.concatenate(dqs, axis=1).astype(BF16)

    row = pl.BlockSpec((tq, MEM_W), lambda i: (i, 0))
    full = pl.BlockSpec((N_MEM, 2 * MEM_W), lambda i: (0, 0))
    return pl.pallas_call(
        body, name=name, grid=(t // tq,), in_specs=[row, full, row], out_specs=[row, full],
        out_shape=[jax.ShapeDtypeStruct((t, MEM_W), BF16), jax.ShapeDtypeStruct((N_MEM, 2 * MEM_W), F32)],
        compiler_params=_cp(1),
    )(zq, kv, do)


def _gate_fwd(zg, bg, ys, name, tr=256):
    t = zg.shape[0]
    tr = min(tr, t)

    def body(z_ref, b_ref, y0, y1, y2, o_ref):
        g = _sigmoid(z_ref[...].astype(F32) + b_ref[...])
        o_ref[...] = (g[:, :D] * y0[...].astype(F32) + g[:, D:2 * D] * y1[...].astype(F32)
                      + g[:, 2 * D:] * y2[...].astype(F32)).astype(BF16)

    row = pl.BlockSpec((tr, D), lambda i: (i, 0))
    return pl.pallas_call(
        body, name=name, grid=(t // tr,),
        in_specs=[pl.BlockSpec((tr, 3 * D), lambda i: (i, 0)), pl.BlockSpec((1, 3 * D), lambda i: (0, 0)), row, row, row],
        out_specs=row, out_shape=jax.ShapeDtypeStruct((t, D), BF16), compiler_params=_cp(1),
    )(zg, _row(bg), *ys)


def _gate_bwd(zg, bg, ys, dmerged, name, tr=256):
    t = zg.shape[0]
    tr = min(tr, t)

    def body(z_ref, b_ref, y0, y1, y2, dm_ref, d0, d1, d2, dz_ref, db_ref):
        @pl.when(pl.program_id(0) == 0)
        def _():
            db_ref[...] = jnp.zeros_like(db_ref)

        g = _sigmoid(z_ref[...].astype(F32) + b_ref[...])
        dm = dm_ref[...].astype(F32)
        for i, (y_ref, d_ref) in enumerate(((y0, d0), (y1, d1), (y2, d2))):
            gi = g[:, i * D:(i + 1) * D]
            d_ref[...] = (dm * gi).astype(BF16)
            dz = dm * y_ref[...].astype(F32) * gi * (1.0 - gi)
            dz_ref[:, i * D:(i + 1) * D] = dz.astype(BF16)
            db_ref[0:1, i * D:(i + 1) * D] += jnp.sum(dz, axis=0, keepdims=True)

    row = pl.BlockSpec((tr, D), lambda i: (i, 0))
    wide = pl.BlockSpec((tr, 3 * D), lambda i: (i, 0))
    d0, d1, d2, dz, db = pl.pallas_call(
        body, name=name, grid=(t // tr,),
        in_specs=[wide, pl.BlockSpec((1, 3 * D), lambda i: (0, 0)), row, row, row, row],
        out_specs=[row, row, row, wide, pl.BlockSpec((8, 3 * D), lambda i: (0, 0))],
        out_shape=[jax.ShapeDtypeStruct((t, D), BF16)] * 3 + [jax.ShapeDtypeStruct((t, 3 * D), BF16),
                                                              jax.ShapeDtypeStruct((8, 3 * D), F32)],
        compiler_params=_cp(1),
    )(zg, _row(bg), *ys, dmerged)
    return (d0, d1, d2), dz, db[0]


def _adamw(w, g, m, v, name):
    shape = w.shape
    w, g, m, v = (a.reshape(-1, shape[-1]) for a in (w, g, m, v))
    r, d = w.shape
    tr = next((c for c in (512, 352, 256, 128, 64, 32, 16, 8) if r % c == 0), r)

    def body(w_ref, g_ref, m_ref, v_ref, d_ref, nm_ref, nv_ref):
        gg = g_ref[...]
        m2 = ADAM_B1 * m_ref[...] + (1.0 - ADAM_B1) * gg
        v2 = ADAM_B2 * v_ref[...] + (1.0 - ADAM_B2) * (gg * gg)
        m_hat = m2 / (1.0 - ADAM_B1 ** ADAM_STEP)
        v_hat = v2 / (1.0 - ADAM_B2 ** ADAM_STEP)
        d_ref[...] = -ADAM_LR * (m_hat / (jnp.sqrt(v_hat) + ADAM_EPS) + ADAM_WD * w_ref[...])
        nm_ref[...] = m2
        nv_ref[...] = v2

    row = pl.BlockSpec((tr, d), lambda i: (i, 0))
    outs = pl.pallas_call(
        body, name=name, grid=(r // tr,), in_specs=[row] * 4, out_specs=[row] * 3,
        out_shape=[jax.ShapeDtypeStruct((r, d), F32)] * 3, compiler_params=_cp(1),
    )(w, g, m, v)
    return [o.reshape(shape) for o in outs]


def _slab_tile(rows):
    return next((c for c in range(min(rows, 512), 15, -16) if rows % c == 0), rows)


def _sum_slots(parts, out_dtype, name):
    n, r, d = parts.shape
    tr = _slab_tile(r)

    def body(p_ref, o_ref):
        acc = p_ref[0].astype(F32)
        for s in range(1, n):
            acc = acc + p_ref[s].astype(F32)
        o_ref[...] = acc.astype(o_ref.dtype)

    return pl.pallas_call(
        body, name=name, grid=(r // tr,), in_specs=[pl.BlockSpec((n, tr, d), lambda i: (0, i, 0))],
        out_specs=pl.BlockSpec((tr, d), lambda i: (i, 0)),
        out_shape=jax.ShapeDtypeStruct((r, d), out_dtype), compiler_params=_cp(1),
    )(parts)


def _place():
    return lax.axis_index("x"), lax.axis_index("y"), lax.axis_index("c")


def _all_gather(shard, name):
    r, d = shard.shape

    def body(x_ref, out_ref, send_sems, recv_sems, local_sem):
        x, y, c = _place()
        me, sibling = (x, y, c), (x, y, 1 - c)
        chips = [(1 - x, y), (x, 1 - y), (1 - x, 1 - y)]

        def slot(px, py, pc):
            return out_ref.at[4 * px + 2 * py + pc]

        def copy(k, block, to, src=None):
            return pltpu.make_async_remote_copy(
                src_ref=slot(*block) if src is None else src, dst_ref=slot(*block),
                send_sem=send_sems.at[k], recv_sem=recv_sems.at[k], device_id=to, device_id_type=MESH)

        mine = pltpu.make_async_copy(x_ref, slot(*me), local_sem)
        mine.start()
        first = [copy(0, me, sibling, src=x_ref)]
        first += [copy(1 + j, me, (*chip, c), src=x_ref) for j, chip in enumerate(chips)]
        for cp in first:
            cp.start()
        passed = [copy(4 + j, (*chip, c), sibling) for j, chip in enumerate(chips)]
        for j, chip in enumerate(chips):
            copy(1 + j, (*chip, c), me).wait_recv()
            passed[j].start()
        copy(0, sibling, me).wait_recv()
        for j, chip in enumerate(chips):
            copy(4 + j, (*chip, 1 - c), me).wait_recv()
        for cp in first + passed:
            cp.wait_send()
        mine.wait()

    return pl.pallas_call(
        body, name=name, out_shape=jax.ShapeDtypeStruct((N_DEV, r, d), shard.dtype),
        in_specs=[pl.BlockSpec(memory_space=pl.ANY)], out_specs=pl.BlockSpec(memory_space=pl.ANY),
        scratch_shapes=[pltpu.SemaphoreType.DMA((7,)), pltpu.SemaphoreType.DMA((7,)), pltpu.SemaphoreType.DMA],
    )(shard)


HBM_SPEC = pl.BlockSpec(memory_space=pltpu.HBM)
SEM_SPEC = pl.BlockSpec(memory_space=pltpu.SEMAPHORE)
SPLIT_PARAMS = pltpu.CompilerParams(has_side_effects=pltpu.SideEffectType.DATAFLOW_SIDE_EFFECTING)


def _peers():
    x, y, c = _place()
    flip = lambda v, bit: 1 - v if bit else v
    return 4 * x + 2 * y + c, [((flip(x, k >> 2 & 1), flip(y, k >> 1 & 1), flip(c, k & 1))) for k in range(1, N_DEV)]


def _exchange_start(src, all_gather, name):
    r, d = src.shape[-2:]

    def body(src_ref, land_ref, send_sems, recv_sems, src_thru, land_thru, token):
        me, peers = _peers()
        for k, (px, py, pc) in enumerate(peers):
            part = src_ref if all_gather else src_ref.at[4 * px + 2 * py + pc]
            pltpu.make_async_remote_copy(src_ref=part, dst_ref=land_ref.at[me], send_sem=send_sems.at[k],
                                         recv_sem=recv_sems.at[k], device_id=(px, py, pc), device_id_type=MESH).start()
        token[...] = jnp.zeros_like(token)

    land = lax.empty((N_DEV, r, d), src.dtype)
    return pl.pallas_call(
        body, name=name,
        out_shape=(pltpu.SemaphoreType.DMA((N_DEV - 1,)), pltpu.SemaphoreType.DMA((N_DEV - 1,)), pltpu.HBM(src.shape, src.dtype),
                   pltpu.HBM(land.shape, land.dtype), jax.ShapeDtypeStruct((8, LANES), F32)),
        in_specs=(HBM_SPEC, HBM_SPEC), out_specs=(SEM_SPEC, SEM_SPEC, HBM_SPEC, HBM_SPEC, pl.BlockSpec(memory_space=pltpu.VMEM)),
        input_output_aliases={0: 2, 1: 3}, compiler_params=SPLIT_PARAMS,
    )(pltpu.with_memory_space_constraint(src, pltpu.HBM), pltpu.with_memory_space_constraint(land, pltpu.HBM))


def _exchange_wait(started, after, all_gather, name):
    send_sems, recv_sems, src_thru, land_thru, _ = started

    def body(src_ref, land_ref, send_sems, recv_sems, after_ref, src_out, land_out):
        me, peers = _peers()
        for k, (px, py, pc) in enumerate(peers):
            part = src_ref if all_gather else src_ref.at[4 * px + 2 * py + pc]
            cp = pltpu.make_async_remote_copy(src_ref=part, dst_ref=land_ref.at[4 * px + 2 * py + pc], send_sem=send_sems.at[k],
                                              recv_sem=recv_sems.at[k], device_id=(px, py, pc), device_id_type=MESH)
            cp.wait_send()
            cp.wait_recv()

    return pl.pallas_call(
        body, name=name,
        out_shape=(pltpu.HBM(src_thru.shape, src_thru.dtype), pltpu.HBM(land_thru.shape, land_thru.dtype)),
        in_specs=(HBM_SPEC, HBM_SPEC, SEM_SPEC, SEM_SPEC, pl.BlockSpec(memory_space=pl.ANY)), out_specs=(HBM_SPEC, HBM_SPEC),
        input_output_aliases={0: 0, 1: 1}, compiler_params=SPLIT_PARAMS,
    )(src_thru, land_thru, send_sems, recv_sems, after)


def _own_slot(land, mine):
    x, y, c = _place()
    return lax.dynamic_update_slice(land, mine[None], (4 * x + 2 * y + c, 0, 0))


BIG = (("w_in", (D, 864), 1), ("w_conv_out", (CONV_W, 128), 1), ("w_att_out", (ATT_GW, 128), 1), ("w_mem_kv", (128, D), 0),
       ("w_mem_out", (MEM_W, 128), 1), ("w_out", (128, D), 0), ("w_ffn_in", (D, 704), 1), ("w_ffn_out", (352, D), 0),
       ("conv_dw", (CONV_K, 64), 1))
SMALL = (("rel_bias", (NUM_BUCKETS, 12)), ("norm_mix_pre", (DEPTH, D)), ("b_gate", (DEPTH, 3 * D)),
         ("conv_dw_bias", (DEPTH, CONV_W)), ("conv_ln_g", (DEPTH, CONV_W)), ("conv_ln_b", (DEPTH, CONV_W)),
         ("norm_mem", (DEPTH, D)), ("norm_mix_post", (DEPTH, D)), ("norm_ffn_pre", (DEPTH, D)), ("norm_ffn_post", (DEPTH, D)))
TWIN_WEIGHTS = ("rel_bias", "norm_mix_pre", "w_in", "b_gate", "conv_dw", "conv_dw_bias", "conv_ln_g", "conv_ln_b", "w_conv_out",
                "w_att_out", "norm_mem", "w_mem_kv", "w_mem_out", "w_out", "norm_mix_post", "norm_ffn_pre", "w_ffn_in",
                "w_ffn_out", "norm_ffn_post")


BIG_INFO = {n: (s, a) for n, s, a in BIG}
GROUPS = {"mix": ("w_in", "w_conv_out", "w_att_out", "w_mem_kv", "w_mem_out", "w_out", "conv_dw"), "ffn": ("w_ffn_in", "w_ffn_out")}
TILE_BYTES_PER_LANE = 32


def _rows_of(shape, dtype):
    tile = TILE_BYTES_PER_LANE // jnp.dtype(dtype).itemsize
    return -(-int(np.prod(shape)) // (D * tile)) * tile


def _row_form(shape, axis):
    return tuple(shape) if axis == 0 else (shape[1], shape[0])


def _as_slab_rows(p, lead=()):
    shape = p.shape[len(lead):]
    rows = _rows_of(shape, p.dtype)
    if shape == (rows, D):
        return p
    n = int(np.prod(shape))
    p = jnp.pad(p.reshape(*lead, n), [(0, 0)] * len(lead) + [(0, rows * D - n)])
    return p.reshape(*lead, rows, D)


def _from_slab_rows(rows, shape, lead=()):
    if rows.shape[len(lead):] == tuple(shape):
        return rows
    n = int(np.prod(shape))
    return rows.reshape(*lead, -1)[..., :n].reshape(*lead, *shape)


def _pack(pieces, dtype):
    return jnp.concatenate([_as_slab_rows(p.astype(dtype)) for p in pieces], axis=0)


def _unpack(slab, shapes, packed_as):
    out, r0 = [], 0
    for s in shapes:
        rows = _rows_of(s, packed_as)
        out.append(_from_slab_rows(slab[r0:r0 + rows], s))
        r0 += rows
    return out


def _pack_group(blocks, layer, group):
    return _pack([blocks[n][layer] if BIG_INFO[n][1] == 0 else blocks[n][layer].T for n in GROUPS[group]], BF16)


def _full_weights(gathered, group):
    out, r0 = {}, 0
    for name in GROUPS[group]:
        r, k = _row_form(*BIG_INFO[name])
        rows = _rows_of((r, k), BF16)
        out[name] = _from_slab_rows(gathered[:, r0:r0 + rows], (r, k), lead=(N_DEV,)).reshape(N_DEV * r, k)
        r0 += rows
    return out


def _scatter_layout(grads, group):
    parts = []
    for name in GROUPS[group]:
        r, k = _row_form(*BIG_INFO[name])
        parts.append(_as_slab_rows(grads[name].astype(BF16).reshape(N_DEV, r, k), lead=(N_DEV,)))
    return jnp.concatenate(parts, axis=1)


def _unpack_blocks(slab, group):
    out = _unpack(slab, [_row_form(*BIG_INFO[n]) for n in GROUPS[group]], BF16)
    return {n: b if BIG_INFO[n][1] == 0 else b.T for n, b in zip(GROUPS[group], out)}


def _split_w_in(w_in):
    edges = np.cumsum((0,) + SEG)
    return [w_in[edges[i]:edges[i + 1]] for i in range(4)]


def _conv_taps(conv_dw):
    return jnp.pad(conv_dw.astype(F32).T, ((0, 32 - CONV_K), (0, 0)))


def _layer_fwd(l, x, h, nm, get_w, sm, bias_tabs):
    t = x.shape[0]
    wts = get_w(l, "mix", h)
    w_seg = _split_w_in(wts["w_in"])
    zc = _mm(h, w_seg[0], "fwd_in_conv", tb=True)
    za = _mm(h, w_seg[1], "fwd_in_att", tb=True, tn=1152)
    zq = _mm(h, w_seg[2], "fwd_in_memq", tb=True)
    zg = _mm(h, w_seg[3], "fwd_in_gate", tb=True)
    y, cact = _conv_fwd(zc, _conv_taps(wts["conv_dw"]), sm["conv_dw_bias"][l], sm["conv_ln_g"][l], sm["conv_ln_b"][l],
                        "conv_fwd")
    qkv, outs, lses = [], [], []
    for g, (_, dil) in enumerate(ATT_PATTERNS):
        cols = tuple(i * 3 + g for i in range(3))
        if dil == 1:
            trio, at = (za, za, za), cols
        else:
            trio, at = _to_residue_major([(za, c) for c in cols], dil, f"qkv_residue_d{dil}"), (0, 0, 0)
        o, ls = _att_fwd(*trio, bias_tabs[g], t // dil, f"att_fwd_d{dil}", cols=at)
        qkv.append((trio, at))
        outs.append(o)
        lses.append(ls)
    att, lse = _att_combine(outs, lses, "att_combine")
    kv = _mm(nm, wts["w_mem_kv"], "fwd_mem_kv")
    om = _mem_fwd(zq, kv, "mem_fwd")
    ys = (_mm(cact, wts["w_conv_out"], "fwd_conv_out", tb=True, tm=2048),
          _mm(att, wts["w_att_out"], "fwd_att_out", tb=True, tm=2048),
          _mm(om, wts["w_mem_out"], "fwd_mem_out", tb=True, tm=2048))
    merged = _gate_fwd(zg, sm["b_gate"][l], ys, "gate_fwd")
    ymix, x1, h2 = _mm_norm_res(merged, wts["w_out"], x, sm["norm_mix_post"][l], sm["norm_ffn_pre"][l], "fwd_out_norm")
    wff = get_w(l, "ffn", h2)
    gff, uff, act = _mm_swiglu_fwd(h2, wff["w_ffn_in"], "fwd_ffn_in_swiglu")
    g_next = sm["norm_mix_pre"][l + 1] if l + 1 < DEPTH else None
    f, x2, h_next = _mm_norm_res(act, wff["w_ffn_out"], x1, sm["norm_ffn_post"][l], g_next,
                                 "fwd_ffn_out_norm" if g_next is not None else "fwd_ffn_out_last")
    saved = dict(x=x, h=h, zc=zc, zq=zq, zg=zg, y=y, cact=cact, qkv=qkv, att=att, lse=lse, kv=kv, om=om, ys=ys,
                 merged=merged, ymix=ymix, x1=x1, h2=h2, gff=gff, uff=uff, act=act, f=f)
    return x2, h_next, saved


def _after(value, *tokens):
    for t in tokens:
        if t is not None:
            value = value + t[0, 0].astype(value.dtype)
    return value


def _layer_bwd(l, dx2, s, nm, mem, get_w, put_g, sm, bias_tabs, bucket_idx):
    t = dx2.shape[0]
    wts, wff = get_w(l, "mix", None), get_w(l, "ffn", None)
    gb, gf, gs = {}, {}, {}
    df, gs["norm_ffn_post"] = _norm_bwd(s["f"], sm["norm_ffn_post"][l], dx2, None, BF16, "bwd_norm_ffn_post")
    dg, du = _mm_swiglu_bwd(df, wff["w_ffn_out"], s["gff"], s["uff"], "bwd_ffn_out_d_swiglu")
    gf["w_ffn_out"] = _mm(s["act"], df, "bwd_ffn_out_w", ta=True, tm=1408)
    gf["w_ffn_in"] = jnp.concatenate([_mm(dg, s["h2"], "bwd_ffn_in_w_gate", ta=True, tm=1408),
                                      _mm(du, s["h2"], "bwd_ffn_in_w_up", ta=True, tm=1408)], axis=0)
    tok = put_g(l, "ffn", gf)
    dh2 = _mm_sum([(dg, _after(wff["w_ffn_in"][:FFN_H], tok), FFN_H // 2), (du, wff["w_ffn_in"][FFN_H:], FFN_H // 2)],
                  "bwd_ffn_in_d")
    dx1, gs["norm_ffn_pre"] = _norm_bwd(s["x1"], sm["norm_ffn_pre"][l], dh2, dx2, F32, "bwd_norm_ffn_pre")
    dymix, gs["norm_mix_post"] = _norm_bwd(s["ymix"], sm["norm_mix_post"][l], dx1, None, BF16, "bwd_norm_mix_post")
    dmerged = _mm(dymix, wts["w_out"], "bwd_out_d", tb=True)
    gb["w_out"] = _mm(s["merged"], dymix, "bwd_out_w", ta=True)
    dys, dzg, gs["b_gate"] = _gate_bwd(s["zg"], sm["b_gate"][l], s["ys"], dmerged, "gate_bwd")
    dcact = _mm(dys[0], wts["w_conv_out"], "bwd_conv_out_d", tm=2048)
    gb["w_conv_out"] = _mm(dys[0], s["cact"], "bwd_conv_out_w", ta=True)
    dzc, dconv, gs["conv_dw_bias"], gs["conv_ln_g"], gs["conv_ln_b"] = _conv_bwd(
        s["zc"], s["y"], dcact, _conv_taps(wts["conv_dw"]), sm["conv_ln_g"][l], sm["conv_ln_b"][l], "conv_bwd")
    gb["conv_dw"] = dconv[:CONV_K].T
    datt = _mm(dys[1], wts["w_att_out"], "bwd_att_out_d", tm=2048)
    gb["w_att_out"] = _mm(dys[1], s["att"], "bwd_att_out_w", ta=True)
    dqkv, rel = [], []
    for g, (_, dil) in enumerate(ATT_PATTERNS):
        trio, at = s["qkv"][g]
        if dil == 1:
            o_r, do_r, lse_r = s["att"], datt, s["lse"]
        else:
            o_r, do_r, lse_r = _to_residue_major([(s["att"], 0), (datt, 0), (s["lse"], 0)], dil, f"att_bwd_residue_d{dil}")
        dq, dk, dv, dbias = _att_bwd(*trio, bias_tabs[g], o_r, do_r, lse_r, t // dil, f"att_bwd_d{dil}", cols=at)
        dqkv.append((dq, dk, dv))
        rel.append(_bias_grad(dbias, bucket_idx[g], f"bias_grad_d{dil}")[:HEADS, :NUM_BUCKETS])
    dza = _assemble_dza(dqkv, "att_bwd_assemble")
    gs["rel_bias"] = jnp.concatenate(rel, axis=0).T
    dom = _mm(dys[2], wts["w_mem_out"], "bwd_mem_out_d", tm=2048)
    gb["w_mem_out"] = _mm(dys[2], s["om"], "bwd_mem_out_w", ta=True)
    dzq, dkv = _mem_bwd(s["zq"], s["kv"], dom, "mem_bwd")
    dkv = dkv.astype(BF16)
    gb["w_mem_kv"] = _mm(nm, dkv, "bwd_mem_kv_w", ta=True)
    dnm = _mm(dkv, wts["w_mem_kv"], "bwd_mem_kv_d", tb=True, out_dtype=F32)
    _, gs["norm_mem"] = _norm_bwd(mem, sm["norm_mem"][l], dnm, None, BF16, "bwd_norm_mem")
    segs = ((dzc, "conv", 1024), (dza, "att", 1152), (dzq, "memq", 512), (dzg, "gate", 1024))
    gb["w_in"] = jnp.concatenate([_mm(dz, s["h"], f"bwd_in_{nm_}_w", ta=True, tm=min(blk, 1152), tk=2048)
                                  for dz, nm_, blk in segs], axis=0)
    tok = put_g(l, "mix", gb)
    w_seg = _split_w_in(wts["w_in"])
    w_seg[0] = _after(w_seg[0], tok)
    dx, gs["norm_mix_pre"] = _mm_sum([(dz, w_seg[i], blk) for i, (dz, _, blk) in enumerate(segs)], "bwd_in_d_norm",
                                     norm=(s["x"], sm["norm_mix_pre"][l], dx1))
    return dx, gs


def _local_step(x, mem, target, sm, get_w, put_g, tokens=()):
    t = x.shape[0]
    bias_tabs, bucket_idx = [], []
    for g, (_, dil) in enumerate(ATT_PATTERNS):
        idx = _bucket_index(_att_tiles(t // dil)[1], dil)
        bucket_idx.append(jnp.asarray(idx))
        bias_tabs.append(_bias_table(sm["rel_bias"][:, g * HEADS:(g + 1) * HEADS], bucket_idx[g], f"bias_table_d{dil}"))
    h = _norm_plain(x, _after(sm["norm_mix_pre"][0], *tokens), "norm_first")
    saved, nms = [], []
    for l in range(DEPTH):
        nm = _norm_plain(mem, sm["norm_mem"][l], "norm_mem")
        x, h, s = _layer_fwd(l, x, h, nm, get_w, sm, bias_tabs)
        saved.append(s)
        nms.append(nm)
    loss, dx = _loss_head(x, target, "loss_head")
    gsmall = {}
    for l in reversed(range(DEPTH)):
        dx, gs = _layer_bwd(l, dx, saved[l], nms[l], mem, get_w, put_g, sm, bias_tabs, bucket_idx)
        for n, v in gs.items():
            gsmall.setdefault(n, {})[l] = v
    small = {}
    for n, _ in SMALL:
        small[n] = gsmall[n][0] + gsmall[n][1] if n == "rel_bias" else jnp.stack([gsmall[n][0], gsmall[n][1]])
    return loss, dx, small


def kernel(x, mem, rel_bias, norm_mix_pre, w_in, b_gate, conv_dw, conv_dw_bias, conv_ln_g, conv_ln_b, w_conv_out, w_att_out, norm_mem, w_mem_kv, w_mem_out, w_out, norm_mix_post, norm_ffn_pre, w_ffn_in, w_ffn_out, norm_ffn_post, loss_target, m_rel_bias, m_norm_mix_pre, m_w_in, m_b_gate, m_conv_dw, m_conv_dw_bias, m_conv_ln_g, m_conv_ln_b, m_w_conv_out, m_w_att_out, m_norm_mem, m_w_mem_kv, m_w_mem_out, m_w_out, m_norm_mix_post, m_norm_ffn_pre, m_w_ffn_in, m_w_ffn_out, m_norm_ffn_post, v_rel_bias, v_norm_mix_pre, v_w_in, v_b_gate, v_conv_dw, v_conv_dw_bias, v_conv_ln_g, v_conv_ln_b, v_w_conv_out, v_w_att_out, v_norm_mem, v_w_mem_kv, v_w_mem_out, v_w_out, v_norm_mix_post, v_norm_ffn_pre, v_w_ffn_in, v_w_ffn_out, v_norm_ffn_post):
    args = dict(locals())
    w = {n: args[n] for n in TWIN_WEIGHTS}
    m = {n: args["m_" + n] for n in TWIN_WEIGHTS}
    v = {n: args["v_" + n] for n in TWIN_WEIGHTS}
    sm = {n: w[n] for n, _ in SMALL}

    first = (0, "mix")
    keys = [(l, grp) for l in range(DEPTH) for grp in GROUPS]
    shard = {(l, grp): _pack_group(w, l, grp) for l, grp in keys}
    weights = {first: _full_weights(_all_gather(shard[first], "gather_l0_mix"), "mix")}
    settled = weights[first]["w_out"][:1, :1] * 0
    ag = {k: _exchange_start(_after(shard[k], settled), True, f"ag_start_l{k[0]}_{k[1]}") for k in keys if k != first}

    def get_w(l, grp, after):
        if (l, grp) not in weights:
            mine, land = _exchange_wait(ag[l, grp], after, True, f"ag_wait_l{l}_{grp}")
            weights[l, grp] = _full_weights(_own_slot(land, mine), grp)
        return weights[l, grp]

    rs = {}

    def put_g(l, grp, grads):
        rs[l, grp] = _exchange_start(_scatter_layout(grads, grp), False, f"rs_start_l{l}_{grp}")
        return rs[l, grp][4]

    loss, dx, gsmall = _local_step(x[0], mem[0], loss_target[0], sm, get_w, put_g, [st[4] for st in ag.values()])
    loss = lax.psum(loss, ("x", "y", "c"))

    xi, yi, ci = _place()
    me = 4 * xi + 2 * yi + ci
    g_slab = {}
    for k, st in rs.items():
        contrib, land = _exchange_wait(st, dx, False, f"rs_wait_l{k[0]}_{k[1]}")
        own = lax.dynamic_index_in_dim(contrib, me, axis=0, keepdims=False)
        g_slab[k] = _sum_slots(_own_slot(land, own), F32, f"rs_sum_l{k[0]}_{k[1]}")
    g_layers = [{**_unpack_blocks(g_slab[l, "mix"], "mix"), **_unpack_blocks(g_slab[l, "ffn"], "ffn")} for l in range(DEPTH)]
    small_shapes = [s for _, s in SMALL]
    g_small = _unpack(_sum_slots(_all_gather(_pack([gsmall[n] for n, _ in SMALL], F32), "gather_small"), F32, "sum_small"),
                      small_shapes, F32)
    grads = {n: jnp.stack([g_layers[0][n], g_layers[1][n]]) for n, _, _ in BIG}
    grads.update({n: g_small[i] for i, (n, _) in enumerate(SMALL)})

    delta, new_m, new_v = {}, {}, {}
    for n in TWIN_WEIGHTS:
        delta[n], new_m[n], new_v[n] = _adamw(w[n], grads[n], m[n], v[n], f"adamw_{n}")

    return (loss, dx[None], *[grads[n] for n in TWIN_WEIGHTS], *[delta[n] for n in TWIN_WEIGHTS],
            *[new_m[n] for n in TWIN_WEIGHTS], *[new_v[n] for n in TWIN_WEIGHTS])
```

```python
import functools

import numpy as np
import jax
import jax.numpy as jnp
from jax import lax
from jax.experimental import pallas as pl
from jax.experimental.pallas import tpu as pltpu

F32 = jnp.float32
BF16 = jnp.bfloat16

N_DEV = 8
D = 1024
DEPTH = 2
CONV_W = 512
CONV_K = 31
CONV_PAD = 16
ATT_PATTERNS = ((128, 1), (512, 4), (2048, 16))
ATT_RADIUS = 64
HEADS = 4
HEAD_DIM = 64
ATT_GW = HEADS * HEAD_DIM
ATT_W = 3 * ATT_GW
MEM_HEADS = 4
MEM_HD = 128
MEM_W = 512
N_MEM = 256
FFN_H = 2816
NUM_BUCKETS = 32
MAX_DISTANCE = 1024
RMS_EPS = 1e-6
LN_EPS = 1e-5
NEG_INF = -1e30
SEG = (2 * CONV_W, 3 * ATT_W, MEM_W, 3 * D)
ADAM_LR, ADAM_B1, ADAM_B2, ADAM_EPS, ADAM_WD, ADAM_STEP = 0.001, 0.9, 0.999, 1e-08, 0.01, 10

VMEM_LIMIT_V7X = 56 * 1024 * 1024
MESH = pl.DeviceIdType.MESH


def _cp(n_axes):
    return pltpu.CompilerParams(dimension_semantics=("arbitrary",) * n_axes, vmem_limit_bytes=VMEM_LIMIT_V7X)


def _row(v):
    return v.reshape(1, -1)


def _mm(a, b, name, *, ta=False, tb=False, out_dtype=BF16, tm=2048, tn=1024, tk=2048):
    kdim, m = a.shape if ta else a.shape[::-1]
    n, kb = b.shape if tb else b.shape[::-1]
    assert kb == kdim and a.dtype == BF16 and b.dtype == BF16
    tm, tn, tk = min(tm, m), min(tn, n), min(tk, kdim)
    assert m % tm == 0 and n % tn == 0 and kdim % tk == 0, (name, m, n, kdim, tm, tn, tk)
    nk = kdim // tk
    a_spec = pl.BlockSpec((tk, tm), lambda i, j, k: (k, i)) if ta else pl.BlockSpec((tm, tk), lambda i, j, k: (i, k))
    b_spec = pl.BlockSpec((tn, tk), lambda i, j, k: (j, k)) if tb else pl.BlockSpec((tk, tn), lambda i, j, k: (k, j))
    o_spec = pl.BlockSpec((tm, tn), lambda i, j, k: (i, j))
    dims = (((0 if ta else 1,), (1 if tb else 0,)), ((), ()))
    use_scratch = nk > 1 and out_dtype != F32

    def body(*refs):
        a_ref, b_ref, o_ref = refs[:3]
        p = lax.dot_general(a_ref[...], b_ref[...], dims, preferred_element_type=F32)
        if nk == 1:
            o_ref[...] = p.astype(o_ref.dtype)
            return
        k = pl.program_id(2)
        t_ref = refs[3] if use_scratch else o_ref

        @pl.when(k == 0)
        def _():
            t_ref[...] = p

        @pl.when(k > 0)
        def _():
            t_ref[...] += p

        if use_scratch:
            @pl.when(k == nk - 1)
            def _():
                o_ref[...] = t_ref[...].astype(o_ref.dtype)

    return pl.pallas_call(
        body, name=name, grid=(m // tm, n // tn, nk), in_specs=[a_spec, b_spec], out_specs=o_spec,
        out_shape=jax.ShapeDtypeStruct((m, n), out_dtype),
        scratch_shapes=[pltpu.VMEM((tm, tn), F32)] if use_scratch else [],
        compiler_params=_cp(3),
    )(a, b)


def _rms_bwd_val(v, g, dy):
    r = lax.rsqrt(jnp.mean(v * v, axis=-1, keepdims=True) + RMS_EPS)
    vhat = v * r
    dvh = dy * g
    dv = r * (dvh - vhat * jnp.mean(dvh * vhat, axis=-1, keepdims=True))
    return dv, jnp.sum(dy * vhat, axis=0, keepdims=True)


def _mm_sum(pairs, name, *, norm=None, tm=1024):
    m, n = pairs[0][0].shape[0], pairs[0][1].shape[1]
    tm = min(tm, m)
    starts, specs, total = [], [], 0
    for a, b, tk in pairs:
        assert a.shape == (m, b.shape[0]) and b.shape[1] == n and a.shape[1] % tk == 0 and a.dtype == b.dtype == BF16
        nk = a.shape[1] // tk
        starts.append((total, nk))
        step = functools.partial(lambda k, s0, nk_: jnp.clip(k - s0, 0, nk_ - 1), s0=total, nk_=nk)
        specs.append(pl.BlockSpec((tm, tk), functools.partial(lambda i, k, st: (i, st(k)), st=step)))
        specs.append(pl.BlockSpec((tk, n), functools.partial(lambda i, k, st: (st(k), 0), st=step)))
        total += nk
    np_ = 2 * len(pairs)
    row = pl.BlockSpec((tm, n), lambda i, k: (i, 0))

    def accumulate(refs, acc, k):
        for p, (s0, nk) in enumerate(starts):
            @pl.when((k >= s0) & (k < s0 + nk))
            def _(p=p, s0=s0):
                part = jnp.dot(refs[2 * p][...], refs[2 * p + 1][...], preferred_element_type=F32)
                if s0 == 0:
                    @pl.when(k == 0)
                    def _():
                        acc[...] = part

                    @pl.when(k > 0)
                    def _():
                        acc[...] += part
                else:
                    acc[...] += part

    operands = [op for a, b, _ in pairs for op in (a, b)]
    if norm is None:
        def plain(*refs):
            accumulate(refs, refs[np_], pl.program_id(1))

        return pl.pallas_call(
            plain, name=name, grid=(m // tm, total), in_specs=specs, out_specs=row,
            out_shape=jax.ShapeDtypeStruct((m, n), F32), compiler_params=_cp(2),
        )(*operands)

    v, g, dres = norm

    def body(*refs):
        v_ref, g_ref, dres_ref, dv_ref, dg_ref, acc = refs[np_:np_ + 6]
        i, k = pl.program_id(0), pl.program_id(1)
        accumulate(refs, acc, k)

        @pl.when((i == 0) & (k == 0))
        def _():
            dg_ref[...] = jnp.zeros_like(dg_ref)

        @pl.when(k == total - 1)
        def _():
            dv, dg = _rms_bwd_val(v_ref[...], g_ref[...], acc[...])
            dv_ref[...] = dv + dres_ref[...]
            dg_ref[0:1, :] += dg

    once = pl.BlockSpec((tm, n), lambda i, k: (i, 0), pipeline_mode=pl.Buffered(1))
    dv, dg = pl.pallas_call(
        body, name=name, grid=(m // tm, total),
        in_specs=specs + [once, pl.BlockSpec((1, n), lambda i, k: (0, 0)), once],
        out_specs=[row, pl.BlockSpec((8, n), lambda i, k: (0, 0))],
        out_shape=[jax.ShapeDtypeStruct((m, n), F32), jax.ShapeDtypeStruct((8, n), F32)],
        scratch_shapes=[pltpu.VMEM((tm, n), F32)], compiler_params=_cp(2),
    )(*operands, v, _row(g), dres)
    return dv, dg[0]


def _mm_swiglu_fwd(h, w, name, *, tm=1024, tn=1408):
    m, kdim = h.shape
    hid = w.shape[0] // 2
    tm = min(tm, m)
    nj = hid // tn
    dims = (((1,), (1,)), ((), ()))

    def body(h_ref, wg_ref, wu_ref, g_ref, u_ref, a_ref):
        g = lax.dot_general(h_ref[...], wg_ref[...], dims, preferred_element_type=F32)
        u = lax.dot_general(h_ref[...], wu_ref[...], dims, preferred_element_type=F32)
        g_ref[...] = g.astype(BF16)
        u_ref[...] = u.astype(BF16)
        a_ref[...] = (g * _sigmoid(g) * u).astype(BF16)

    out = pl.BlockSpec((tm, tn), lambda i, j: (i, j))
    return pl.pallas_call(
        body, name=name, grid=(m // tm, nj),
        in_specs=[pl.BlockSpec((tm, kdim), lambda i, j: (i, 0)), pl.BlockSpec((tn, kdim), lambda i, j: (j, 0)),
                  pl.BlockSpec((tn, kdim), lambda i, j: (j + nj, 0))],
        out_specs=[out, out, out], out_shape=[jax.ShapeDtypeStruct((m, hid), BF16)] * 3, compiler_params=_cp(2),
    )(h, w, w)


def _mm_swiglu_bwd(df, w, g, u, name, *, tm=1024, tn=1408):
    m, kdim = df.shape
    hid = w.shape[0]
    tm = min(tm, m)

    def body(df_ref, w_ref, g_ref, u_ref, dg_ref, du_ref):
        da = lax.dot_general(df_ref[...], w_ref[...], (((1,), (1,)), ((), ())), preferred_element_type=F32)
        gg, uu = g_ref[...].astype(F32), u_ref[...].astype(F32)
        sg = _sigmoid(gg)
        dg_ref[...] = (da * uu * (sg * (1.0 + gg * (1.0 - sg)))).astype(BF16)
        du_ref[...] = (da * gg * sg).astype(BF16)

    blk = pl.BlockSpec((tm, tn), lambda i, j: (i, j))
    return pl.pallas_call(
        body, name=name, grid=(m // tm, hid // tn),
        in_specs=[pl.BlockSpec((tm, kdim), lambda i, j: (i, 0)), pl.BlockSpec((tn, kdim), lambda i, j: (j, 0)), blk, blk],
        out_specs=[blk, blk], out_shape=[jax.ShapeDtypeStruct((m, hid), BF16)] * 2, compiler_params=_cp(2),
    )(df, w, g, u)


def _mm_norm_res(a, b, xres, g, g_next, name, *, tm=512):
    m, kdim = a.shape
    n = b.shape[1]
    tm = min(tm, m)
    two = g_next is not None

    def body(*refs):
        a_ref, b_ref, x_ref, g_ref = refs[:4]
        y_ref, xn_ref = refs[4 + two], refs[5 + two]
        y = jnp.dot(a_ref[...], b_ref[...], preferred_element_type=F32)
        y_ref[...] = y
        xn = x_ref[...] + _rms_val(y, g_ref[...])
        xn_ref[...] = xn
        if two:
            refs[7][...] = _rms_val(xn, refs[4][...]).astype(BF16)

    row = pl.BlockSpec((tm, n), lambda i: (i, 0))
    vec = pl.BlockSpec((1, n), lambda i: (0, 0))
    outs = pl.pallas_call(
        body, name=name, grid=(m // tm,),
        in_specs=[pl.BlockSpec((tm, kdim), lambda i: (i, 0)), pl.BlockSpec((kdim, n), lambda i: (0, 0)), row, vec] + ([vec] if two else []),
        out_specs=[row, row] + ([row] if two else []),
        out_shape=[jax.ShapeDtypeStruct((m, n), F32)] * 2 + ([jax.ShapeDtypeStruct((m, n), BF16)] if two else []),
        compiler_params=_cp(1),
    )(*((a, b, xres, _row(g)) + ((_row(g_next),) if two else ())))
    return (outs[0], outs[1], outs[2]) if two else (outs[0], outs[1], None)


def _rms_val(v, g):
    return v * lax.rsqrt(jnp.mean(v * v, axis=-1, keepdims=True) + RMS_EPS) * g


def _norm_plain(v, g, name, tr=512):
    t, d = v.shape
    tr = min(tr, t)

    def body(v_ref, g_ref, o_ref):
        o_ref[...] = _rms_val(v_ref[...], g_ref[...]).astype(BF16)

    return pl.pallas_call(
        body, name=name, grid=(t // tr,),
        in_specs=[pl.BlockSpec((tr, d), lambda i: (i, 0)), pl.BlockSpec((1, d), lambda i: (0, 0))],
        out_specs=pl.BlockSpec((tr, d), lambda i: (i, 0)),
        out_shape=jax.ShapeDtypeStruct((t, d), BF16), compiler_params=_cp(1),
    )(v, _row(g))


def _norm_bwd(v, g, dout, dres, out_dtype, name, tr=512):
    t, d = v.shape
    tr = min(tr, t)
    nt = t // tr
    has_res = dres is not None

    def body(*refs):
        v_ref, g_ref, do_ref = refs[:3]
        dv_ref, dg_ref = refs[3 + has_res], refs[4 + has_res]
        i = pl.program_id(0)
        vv = v_ref[...]
        dy = do_ref[...].astype(F32)
        r = lax.rsqrt(jnp.mean(vv * vv, axis=-1, keepdims=True) + RMS_EPS)
        vhat = vv * r
        dvh = dy * g_ref[...]
        dv = r * (dvh - vhat * jnp.mean(dvh * vhat, axis=-1, keepdims=True))
        if has_res:
            dv = dv + refs[3][...]
        dv_ref[...] = dv.astype(dv_ref.dtype)
        part = jnp.sum(dy * vhat, axis=0, keepdims=True)

        @pl.when(i == 0)
        def _():
            dg_ref[...] = jnp.zeros_like(dg_ref)

        dg_ref[0:1, :] += part

    row = pl.BlockSpec((tr, d), lambda i: (i, 0))
    dv, dg = pl.pallas_call(
        body, name=name, grid=(nt,),
        in_specs=[row, pl.BlockSpec((1, d), lambda i: (0, 0)), row] + ([row] if has_res else []),
        out_specs=[row, pl.BlockSpec((8, d), lambda i: (0, 0))],
        out_shape=[jax.ShapeDtypeStruct((t, d), out_dtype), jax.ShapeDtypeStruct((8, d), F32)],
        compiler_params=_cp(1),
    )(*((v, _row(g), dout) + ((dres,) if has_res else ())))
    return dv, dg[0]


def _loss_head(y, target, name, tr=512):
    t, d = y.shape
    tr = min(tr, t)
    nt = t // tr

    def body(y_ref, t_ref, dy_ref, l_ref):
        i = pl.program_id(0)
        err = y_ref[...] - t_ref[...]
        dy_ref[...] = err * (1.0 / d)

        @pl.when(i == 0)
        def _():
            l_ref[...] = jnp.zeros_like(l_ref)

        l_ref[...] += jnp.sum(err * err) * (0.5 / d)

    row = pl.BlockSpec((tr, d), lambda i: (i, 0))
    dy, l = pl.pallas_call(
        body, name=name, grid=(nt,), in_specs=[row, row],
        out_specs=[row, pl.BlockSpec((8, 128), lambda i: (0, 0))],
        out_shape=[jax.ShapeDtypeStruct((t, d), F32), jax.ShapeDtypeStruct((8, 128), F32)],
        compiler_params=_cp(1),
    )(y, target)
    return l[0, 0], dy


def _sigmoid(v):
    return 1.0 / (1.0 + jnp.exp(-v))


def _halo_specs(tq, width, nt, halo):
    per = tq // halo
    last = nt * per - 1
    main = pl.BlockSpec((tq, width), lambda i: (i, 0))
    prev = pl.BlockSpec((halo, width), lambda i: (jnp.maximum(i * per - 1, 0), 0))
    nxt = pl.BlockSpec((halo, width), lambda i: (jnp.minimum((i + 1) * per, last), 0))
    return prev, main, nxt


def _glu_window(zp, zm, zn, i, nt):
    def glu(z):
        z = z.astype(F32)
        return z[:, :CONV_W] * _sigmoid(z[:, CONV_W:])

    up = jnp.where(i > 0, glu(zp), 0.0)
    un = jnp.where(i < nt - 1, glu(zn), 0.0)
    return jnp.concatenate([up, glu(zm), un], axis=0)


def _shifted(win, shift, rows):
    if shift == 0:
        return win[:rows]
    return pltpu.roll(win, win.shape[0] - shift, 0)[:rows]


def _conv_fwd(zc, w, b, ln_g, ln_b, name, tq=512):
    t = zc.shape[0]
    tq = min(tq, t)
    nt = t // tq

    def body(zp_ref, zm_ref, zn_ref, w_ref, b_ref, g_ref, bb_ref, y_ref, c_ref):
        i = pl.program_id(0)
        win = _glu_window(zp_ref[...], zm_ref[...], zn_ref[...], i, nt)
        wv = w_ref[...]
        y = jnp.zeros((tq, CONV_W), F32) + b_ref[...]
        for k in range(CONV_K):
            y = y + _shifted(win, k + 1, tq) * wv[k:k + 1, :]
        y_ref[...] = y
        mu = jnp.mean(y, axis=-1, keepdims=True)
        yc = y - mu
        ln = yc * lax.rsqrt(jnp.mean(yc * yc, axis=-1, keepdims=True) + LN_EPS) * g_ref[...] + bb_ref[...]
        c_ref[...] = (ln * _sigmoid(ln)).astype(BF16)

    vec = pl.BlockSpec((1, CONV_W), lambda i: (0, 0))
    out = pl.BlockSpec((tq, CONV_W), lambda i: (i, 0))
    return pl.pallas_call(
        body, name=name, grid=(nt,),
        in_specs=[*_halo_specs(tq, 2 * CONV_W, nt, CONV_PAD), pl.BlockSpec((32, CONV_W), lambda i: (0, 0)), vec, vec, vec],
        out_specs=[out, out],
        out_shape=[jax.ShapeDtypeStruct((t, CONV_W), F32), jax.ShapeDtypeStruct((t, CONV_W), BF16)],
        compiler_params=_cp(1),
    )(zc, zc, zc, w, _row(b), _row(ln_g), _row(ln_b))


def _conv_bwd(zc, y, dc, w, ln_g, ln_b, name, tq=256):
    t = zc.shape[0]
    tq = min(tq, t)
    nt = t // tq
    rows = tq + 2 * CONV_PAD

    def body(zp_ref, zm_ref, zn_ref, yp_ref, ym_ref, yn_ref, dp_ref, dm_ref, dn_ref, w_ref, g_ref, bb_ref,
             dz_ref, dw_ref, db_ref, dg_ref, dbb_ref):
        i = pl.program_id(0)
        uwin = _glu_window(zp_ref[...], zm_ref[...], zn_ref[...], i, nt)
        ywin = jnp.concatenate([yp_ref[...], ym_ref[...], yn_ref[...]], axis=0)
        dcw = jnp.concatenate([dp_ref[...], dm_ref[...], dn_ref[...]], axis=0).astype(F32)
        mu = jnp.mean(ywin, axis=-1, keepdims=True)
        yc = ywin - mu
        rstd = lax.rsqrt(jnp.mean(yc * yc, axis=-1, keepdims=True) + LN_EPS)
        yhat = yc * rstd
        ln = yhat * g_ref[...] + bb_ref[...]
        sg = _sigmoid(ln)
        dl = dcw * (sg * (1.0 + ln * (1.0 - sg)))
        ridx = lax.broadcasted_iota(jnp.int32, (rows, 1), 0)
        inside = ((ridx >= CONV_PAD) | (i > 0)) & ((ridx < tq + CONV_PAD) | (i < nt - 1))
        dl = jnp.where(inside, dl, 0.0)
        dyh = dl * g_ref[...]
        dy = rstd * (dyh - jnp.mean(dyh, axis=-1, keepdims=True) - yhat * jnp.mean(dyh * yhat, axis=-1, keepdims=True))
        dy = jnp.where(inside, dy, 0.0)
        main = slice(CONV_PAD, CONV_PAD + tq)
        dlm, yhm, dym = dl[main], yhat[main], dy[main]

        @pl.when(i == 0)
        def _():
            dw_ref[...] = jnp.zeros_like(dw_ref)
            db_ref[...] = jnp.zeros_like(db_ref)
            dg_ref[...] = jnp.zeros_like(dg_ref)
            dbb_ref[...] = jnp.zeros_like(dbb_ref)

        dg_ref[0:1, :] += jnp.sum(dlm * yhm, axis=0, keepdims=True)
        dbb_ref[0:1, :] += jnp.sum(dlm, axis=0, keepdims=True)
        db_ref[0:1, :] += jnp.sum(dym, axis=0, keepdims=True)
        wv = w_ref[...]
        du = jnp.zeros((tq, CONV_W), F32)
        for k in range(CONV_K):
            du = du + _shifted(dy, 2 * CONV_PAD - 1 - k, tq) * wv[k:k + 1, :]
            dw_ref[k:k + 1, :] += jnp.sum(dym * _shifted(uwin, k + 1, tq), axis=0, keepdims=True)
        zm = zm_ref[...].astype(F32)
        a, gt = zm[:, :CONV_W], zm[:, CONV_W:]
        sgt = _sigmoid(gt)
        dz_ref[:, :CONV_W] = (du * sgt).astype(BF16)
        dz_ref[:, CONV_W:] = (du * a * sgt * (1.0 - sgt)).astype(BF16)

    vec = pl.BlockSpec((1, CONV_W), lambda i: (0, 0))
    acc = pl.BlockSpec((8, CONV_W), lambda i: (0, 0))
    dz, dw, db, dg, dbb = pl.pallas_call(
        body, name=name, grid=(nt,),
        in_specs=[*_halo_specs(tq, 2 * CONV_W, nt, CONV_PAD), *_halo_specs(tq, CONV_W, nt, CONV_PAD),
                  *_halo_specs(tq, CONV_W, nt, CONV_PAD), pl.BlockSpec((32, CONV_W), lambda i: (0, 0)), vec, vec],
        out_specs=[pl.BlockSpec((tq, 2 * CONV_W), lambda i: (i, 0)), pl.BlockSpec((32, CONV_W), lambda i: (0, 0)), acc, acc, acc],
        out_shape=[jax.ShapeDtypeStruct((t, 2 * CONV_W), BF16), jax.ShapeDtypeStruct((32, CONV_W), F32),
                   jax.ShapeDtypeStruct((8, CONV_W), F32), jax.ShapeDtypeStruct((8, CONV_W), F32),
                   jax.ShapeDtypeStruct((8, CONV_W), F32)],
        compiler_params=_cp(1),
    )(zc, zc, zc, y, y, y, dc, dc, dc, w, _row(ln_g), _row(ln_b))
    return dz, dw, db[0], dg[0], dbb[0]


def _att_tiles(seq_len):
    tq = min(256, seq_len)
    return tq, min(128, tq)


def _t5_bucket_np(rel):
    nb = NUM_BUCKETS // 2
    max_exact = nb // 2
    ret = np.where(rel > 0, nb, 0)
    n = np.abs(rel)
    nf = np.maximum(n, 1).astype(np.float32)
    large = max_exact + (np.log(nf / np.float32(max_exact)) / np.float32(np.log(MAX_DISTANCE / max_exact))
                         * np.float32(nb - max_exact)).astype(np.int32)
    large = np.minimum(large, nb - 1)
    return ret + np.where(n < max_exact, n, large)


def _bucket_index(sb, dilation):
    off = np.arange(sb + 2 * ATT_RADIUS)[None, :] - ATT_RADIUS - np.arange(sb)[:, None]
    idx = _t5_bucket_np(off * dilation).astype(np.int32)
    return np.where(np.abs(off) <= ATT_RADIUS, idx, -1).astype(np.int32)


def _bias_table(tab, idx, name):
    sb, w = idx.shape

    def body(tab_ref, idx_ref, o_ref):
        ix = idx_ref[...]
        for h in range(HEADS):
            acc = jnp.full((sb, w), NEG_INF, F32)
            for b in range(NUM_BUCKETS):
                acc = jnp.where(ix == b, tab_ref[b, h], acc)
            o_ref[h] = acc

    return pl.pallas_call(
        body, name=name, out_shape=jax.ShapeDtypeStruct((HEADS, sb, w), F32),
        in_specs=[pl.BlockSpec(memory_space=pltpu.SMEM), pl.BlockSpec(memory_space=pltpu.VMEM)],
        out_specs=pl.BlockSpec(memory_space=pltpu.VMEM),
    )(tab, idx)


def _qkv_specs(tq, per, last_blk, cols, tile):
    def main(col):
        return pl.BlockSpec((tq, ATT_GW), lambda *g: (tile(*g), col))

    def prev(col):
        return pl.BlockSpec((ATT_RADIUS, ATT_GW), lambda *g: (jnp.maximum(tile(*g) * per - 1, 0), col))

    def nxt(col):
        return pl.BlockSpec((ATT_RADIUS, ATT_GW), lambda *g: (jnp.minimum((tile(*g) + 1) * per, last_blk), col))

    cq, ck, cv = cols
    return [main(cq), prev(ck), main(ck), nxt(ck), prev(cv), main(cv), nxt(cv)]


def _att_fwd(q, k, v, bias, seq_len, name, cols=(0, 0, 0)):
    t = q.shape[0]
    tq, sb = _att_tiles(seq_len)
    nbl = seq_len // tq
    nt = t // tq
    per = tq // ATT_RADIUS
    w = sb + 2 * ATT_RADIUS
    last_blk = t // ATT_RADIUS - 1

    def body(q_ref, kp_ref, kc_ref, kn_ref, vp_ref, vc_ref, vn_ref, b_ref, o_ref, l_ref):
        n = pl.program_id(0) % nbl
        kw = jnp.concatenate([kp_ref[...], kc_ref[...], kn_ref[...]], axis=0)
        vw = jnp.concatenate([vp_ref[...], vc_ref[...], vn_ref[...]], axis=0)
        kpos = n * tq - ATT_RADIUS + lax.broadcasted_iota(jnp.int32, (1, tq + 2 * ATT_RADIUS), 1)
        valid = (kpos >= 0) & (kpos < seq_len)
        for j in range(tq // sb):
            qj = q_ref[j * sb:(j + 1) * sb, :]
            kj, vj, okj = kw[j * sb:j * sb + w], vw[j * sb:j * sb + w], valid[:, j * sb:j * sb + w]
            outs, lses = [], []
            for h in range(HEADS):
                hs = slice(h * HEAD_DIM, (h + 1) * HEAD_DIM)
                s = lax.dot_general(qj[:, hs], kj[:, hs], (((1,), (1,)), ((), ())), preferred_element_type=F32)
                s = jnp.where(okj, s * (HEAD_DIM ** -0.5) + b_ref[h], NEG_INF)
                m = jnp.max(s, axis=-1, keepdims=True)
                e = jnp.exp(s - m)
                den = jnp.sum(e, axis=-1, keepdims=True)
                o = jnp.dot(e.astype(BF16), vj[:, hs], preferred_element_type=F32) / den
                outs.append(o)
                lses.append(jnp.broadcast_to(m + jnp.log(den), (sb, HEAD_DIM)))
            o_ref[j * sb:(j + 1) * sb, :] = jnp.concatenate(outs, axis=1).astype(BF16)
            l_ref[j * sb:(j + 1) * sb, :] = jnp.concatenate(lses, axis=1)

    main = pl.BlockSpec((tq, ATT_GW), lambda i: (i, 0))
    return pl.pallas_call(
        body, name=name, grid=(nt,),
        in_specs=[*_qkv_specs(tq, per, last_blk, cols, lambda i: i), pl.BlockSpec((HEADS, sb, w), lambda i: (0, 0, 0))],
        out_specs=[main, main],
        out_shape=[jax.ShapeDtypeStruct((t, ATT_GW), BF16), jax.ShapeDtypeStruct((t, ATT_GW), F32)],
        compiler_params=_cp(1),
    )(q, k, k, k, v, v, v, bias)


PERM_ROWS = 2048
LANES = 128


def _spec_rm(d, tr, width, col=0):
    return pl.BlockSpec((d, tr // d, width), lambda i: (0, i, col))


def _rm_view(a, d):
    return a.reshape(d, a.shape[0] // d, a.shape[1])


def _gather_residues(scr, val, d, out_ref, col0):
    tr, w = val.shape
    for c in range(w // LANES):
        scr[...] = val[:, c * LANES:(c + 1) * LANES].astype(F32)
        for r in range(d):
            out_ref[r, :, col0 + c * LANES:col0 + (c + 1) * LANES] = scr[pl.ds(r, tr // d, stride=d), :].astype(out_ref.dtype)


def _scatter_residues(scr, ref, d):
    w = ref.shape[2]
    cols = []
    for c in range(w // LANES):
        for r in range(d):
            scr[pl.ds(r, scr.shape[0] // d, stride=d), :] = ref[r, :, c * LANES:(c + 1) * LANES].astype(F32)
        cols.append(scr[...])
    return jnp.concatenate(cols, axis=1)


def _to_residue_major(srcs, d, name, tr=PERM_ROWS):
    t = srcs[0][0].shape[0]
    tr = min(tr, t)
    n = len(srcs)

    def body(*refs):
        scr = refs[2 * n]
        for k in range(n):
            _gather_residues(scr, refs[k][...], d, refs[n + k], 0)

    outs = pl.pallas_call(
        body, name=name, grid=(t // tr,),
        in_specs=[pl.BlockSpec((tr, ATT_GW), functools.partial(lambda i, col: (i, col), col=col)) for _, col in srcs],
        out_specs=[_spec_rm(d, tr, ATT_GW)] * n,
        out_shape=[jax.ShapeDtypeStruct((d, t // d, ATT_GW), a.dtype) for a, _ in srcs],
        scratch_shapes=[pltpu.VMEM((tr, LANES), F32)], compiler_params=_cp(1),
    )(*[a for a, _ in srcs])
    return [o.reshape(t, ATT_GW) for o in outs]


def _att_combine(outs, lses, name, tr=PERM_ROWS):
    t = outs[0].shape[0]
    tr = min(tr, t)
    dils = [d for _, d in ATT_PATTERNS]

    def body(o0, o1, o2, l0, l1, l2, a_ref, lse_ref, scr):
        os_, ls_ = [], []
        for o_ref, l_ref, d in zip((o0, o1, o2), (l0, l1, l2), dils):
            if d == 1:
                os_.append(o_ref[...].astype(F32))
                ls_.append(l_ref[...])
            else:
                os_.append(_scatter_residues(scr, o_ref, d))
                ls_.append(_scatter_residues(scr, l_ref, d))
        la, lb, lc = ls_
        m = jnp.maximum(jnp.maximum(la, lb), lc)
        ea, eb, ec = jnp.exp(la - m), jnp.exp(lb - m), jnp.exp(lc - m)
        den = ea + eb + ec
        a_ref[...] = ((ea * os_[0] + eb * os_[1] + ec * os_[2]) / den).astype(BF16)
        lse_ref[...] = m + jnp.log(den)

    row = pl.BlockSpec((tr, ATT_GW), lambda i: (i, 0))
    specs = [row if d == 1 else _spec_rm(d, tr, ATT_GW) for d in dils]
    views = lambda arrs: [a if d == 1 else _rm_view(a, d) for a, d in zip(arrs, dils)]
    return pl.pallas_call(
        body, name=name, grid=(t // tr,), in_specs=specs * 2, out_specs=[row, row],
        out_shape=[jax.ShapeDtypeStruct((t, ATT_GW), BF16), jax.ShapeDtypeStruct((t, ATT_GW), F32)],
        scratch_shapes=[pltpu.VMEM((tr, LANES), F32)], compiler_params=_cp(1),
    )(*views(outs), *views(lses))


def _assemble_dza(dqkv, name, tr=PERM_ROWS):
    t = dqkv[0][0].shape[0]
    tr = min(tr, t)
    dils = [d for _, d in ATT_PATTERNS]

    def body(*refs):
        o_ref, scr = refs[9], refs[10]
        for g, d in enumerate(dils):
            for c in range(3):
                ref = refs[g * 3 + c]
                val = ref[...] if d == 1 else _scatter_residues(scr, ref, d).astype(BF16)
                o_ref[:, (c * 3 + g) * ATT_GW:(c * 3 + g + 1) * ATT_GW] = val

    row = pl.BlockSpec((tr, ATT_GW), lambda i: (i, 0))
    specs = [row if d == 1 else _spec_rm(d, tr, ATT_GW) for d in dils for _ in range(3)]
    args = [a if d == 1 else _rm_view(a, d) for trio, d in zip(dqkv, dils) for a in trio]
    return pl.pallas_call(
        body, name=name, grid=(t // tr,), in_specs=specs, out_specs=pl.BlockSpec((tr, 3 * ATT_W), lambda i: (i, 0)),
        out_shape=jax.ShapeDtypeStruct((t, 3 * ATT_W), BF16),
        scratch_shapes=[pltpu.VMEM((tr, LANES), F32)], compiler_params=_cp(1),
    )(*args)


def _att_bwd(q, k, v, bias, o, do, lse, seq_len, name, cols=(0, 0, 0)):
    t = q.shape[0]
    tq, sb = _att_tiles(seq_len)
    nbl = seq_len // tq
    n_res = t // seq_len
    per = tq // ATT_RADIUS
    w = sb + 2 * ATT_RADIUS
    last_blk = t // ATT_RADIUS - 1
    acc_cols = 2 * tq - sb + w
    scale = HEAD_DIM ** -0.5

    def body(q_ref, kp_ref, kc_ref, kn_ref, vp_ref, vc_ref, vn_ref, b_ref, o_ref, do_ref, l_ref,
             dq_ref, dk_ref, dv_ref, db_ref, ak_ref, av_ref):
        r, n = pl.program_id(0), pl.program_id(1)

        @pl.when((r == 0) & (n == 0))
        def _():
            db_ref[...] = jnp.zeros_like(db_ref)

        @pl.when(n == 0)
        def _():
            ak_ref[...] = jnp.zeros_like(ak_ref)
            av_ref[...] = jnp.zeros_like(av_ref)

        @pl.when(n < nbl)
        def _():
            kw = jnp.concatenate([kp_ref[...], kc_ref[...], kn_ref[...]], axis=0)
            vw = jnp.concatenate([vp_ref[...], vc_ref[...], vn_ref[...]], axis=0)
            kpos = n * tq - ATT_RADIUS + lax.broadcasted_iota(jnp.int32, (1, tq + 2 * ATT_RADIUS), 1)
            valid = (kpos >= 0) & (kpos < seq_len)
            for j in range(tq // sb):
                rows = slice(j * sb, (j + 1) * sb)
                qj, doj = q_ref[rows, :], do_ref[rows, :]
                dd = doj.astype(F32) * o_ref[rows, :].astype(F32)
                lj = l_ref[rows, :]
                kj, vj, okj = kw[j * sb:j * sb + w], vw[j * sb:j * sb + w], valid[:, j * sb:j * sb + w]
                dqs, dks, dvs, dss = [], [], [], []
                for h in range(HEADS):
                    hs = slice(h * HEAD_DIM, (h + 1) * HEAD_DIM)
                    s = lax.dot_general(qj[:, hs], kj[:, hs], (((1,), (1,)), ((), ())), preferred_element_type=F32)
                    s = jnp.where(okj, s * scale + b_ref[h], NEG_INF)
                    p = jnp.exp(s - lj[:, h * HEAD_DIM:h * HEAD_DIM + 1])
                    dp = lax.dot_general(doj[:, hs], vj[:, hs], (((1,), (1,)), ((), ())), preferred_element_type=F32)
                    ds = p * (dp - jnp.sum(dd[:, hs], axis=-1, keepdims=True))
                    dss.append(ds)
                    dsb = ds.astype(BF16)
                    dqs.append(jnp.dot(dsb, kj[:, hs], preferred_element_type=F32) * scale)
                    dks.append(lax.dot_general(qj[:, hs], dsb, (((0,), (0,)), ((), ())), preferred_element_type=F32) * scale)
                    dvs.append(lax.dot_general(doj[:, hs], p.astype(BF16), (((0,), (0,)), ((), ())), preferred_element_type=F32))
                dq_ref[rows, :] = jnp.concatenate(dqs, axis=1).astype(BF16)
                c0 = tq + j * sb
                ak_ref[:, c0:c0 + w] += jnp.concatenate(dks, axis=0)
                av_ref[:, c0:c0 + w] += jnp.concatenate(dvs, axis=0)
                for h in range(HEADS):
                    db_ref[h] += dss[h]

        dk_ref[...] = ak_ref[:, ATT_RADIUS:ATT_RADIUS + tq].T.astype(BF16)
        dv_ref[...] = av_ref[:, ATT_RADIUS:ATT_RADIUS + tq].T.astype(BF16)
        keep = acc_cols - tq
        nk, nv = ak_ref[:, tq:acc_cols], av_ref[:, tq:acc_cols]
        ak_ref[:, 0:keep] = nk
        av_ref[:, 0:keep] = nv
        ak_ref[:, keep:acc_cols] = jnp.zeros((ATT_GW, tq), F32)
        av_ref[:, keep:acc_cols] = jnp.zeros((ATT_GW, tq), F32)

    def tile(r, n):
        return r * nbl + jnp.minimum(n, nbl - 1)

    main = pl.BlockSpec((tq, ATT_GW), lambda r, n: (tile(r, n), 0))
    lag = pl.BlockSpec((tq, ATT_GW), lambda r, n: (r * nbl + jnp.maximum(n - 1, 0), 0))
    bspec = pl.BlockSpec((HEADS, sb, w), lambda r, n: (0, 0, 0))
    return pl.pallas_call(
        body, name=name, grid=(n_res, nbl + 1),
        in_specs=[*_qkv_specs(tq, per, last_blk, cols, tile), bspec, main, main, main],
        out_specs=[main, lag, lag, bspec],
        out_shape=[jax.ShapeDtypeStruct((t, ATT_GW), BF16)] * 3 + [jax.ShapeDtypeStruct((HEADS, sb, w), F32)],
        scratch_shapes=[pltpu.VMEM((ATT_GW, acc_cols), F32), pltpu.VMEM((ATT_GW, acc_cols), F32)],
        compiler_params=_cp(2),
    )(q, k, k, k, v, v, v, bias, o, do, lse)


def _bias_grad(db, idx, name):
    _, sb, w = db.shape

    def body(db_ref, idx_ref, o_ref):
        ix = idx_ref[...]
        lane = lax.broadcasted_iota(jnp.int32, (1, 128), 1)
        rows = []
        for h in range(HEADS):
            d = db_ref[h]
            acc = jnp.zeros((1, 128), F32)
            for b in range(NUM_BUCKETS):
                acc = acc + jnp.where(lane == b, jnp.sum(jnp.where(ix == b, d, 0.0)), 0.0)
            rows.append(acc)
        o_ref[...] = jnp.concatenate(rows + [jnp.zeros((8 - HEADS, 128), F32)], axis=0)

    return pl.pallas_call(
        body, name=name, out_shape=jax.ShapeDtypeStruct((8, 128), F32),
        in_specs=[pl.BlockSpec(memory_space=pltpu.VMEM), pl.BlockSpec(memory_space=pltpu.VMEM)],
        out_specs=pl.BlockSpec(memory_space=pltpu.VMEM),
    )(db, idx)


def _mem_fwd(zq, kv, name, tq=512):
    t = zq.shape[0]
    tq = min(tq, t)
    scale = MEM_HD ** -0.5

    def body(q_ref, kv_ref, o_ref):
        outs = []
        for h in range(MEM_HEADS):
            hs = slice(h * MEM_HD, (h + 1) * MEM_HD)
            kh = kv_ref[:, h * MEM_HD:(h + 1) * MEM_HD]
            vh = kv_ref[:, MEM_W + h * MEM_HD:MEM_W + (h + 1) * MEM_HD]
            s = lax.dot_general(q_ref[:, hs], kh, (((1,), (1,)), ((), ())), preferred_element_type=F32) * scale
            e = jnp.exp(s - jnp.max(s, axis=-1, keepdims=True))
            den = jnp.sum(e, axis=-1, keepdims=True)
            outs.append(jnp.dot(e.astype(BF16), vh, preferred_element_type=F32) / den)
        o_ref[...] = jnp.concatenate(outs, axis=1).astype(BF16)

    row = pl.BlockSpec((tq, MEM_W), lambda i: (i, 0))
    return pl.pallas_call(
        body, name=name, grid=(t // tq,),
        in_specs=[row, pl.BlockSpec((N_MEM, 2 * MEM_W), lambda i: (0, 0))], out_specs=row,
        out_shape=jax.ShapeDtypeStruct((t, MEM_W), BF16), compiler_params=_cp(1),
    )(zq, kv)


def _mem_bwd(zq, kv, do, name, tq=512):
    t = zq.shape[0]
    tq = min(tq, t)
    scale = MEM_HD ** -0.5

    def body(q_ref, kv_ref, do_ref, dq_ref, dkv_ref):
        @pl.when(pl.program_id(0) == 0)
        def _():
            dkv_ref[...] = jnp.zeros_like(dkv_ref)

        dqs = []
        for h in range(MEM_HEADS):
            hs = slice(h * MEM_HD, (h + 1) * MEM_HD)
            vs = slice(MEM_W + h * MEM_HD, MEM_W + (h + 1) * MEM_HD)
            qh, doh = q_ref[:, hs], do_ref[:, hs]
            kh, vh = kv_ref[:, hs], kv_ref[:, vs]
            s = lax.dot_general(qh, kh, (((1,), (1,)), ((), ())), preferred_element_type=F32) * scale
            e = jnp.exp(s - jnp.max(s, axis=-1, keepdims=True))
            p = e / jnp.sum(e, axis=-1, keepdims=True)
            dp = lax.dot_general(doh, vh, (((1,), (1,)), ((), ())), preferred_element_type=F32)
            ds = p * (dp - jnp.sum(p * dp, axis=-1, keepdims=True))
            dsb = ds.astype(BF16)
            dqs.append(jnp.dot(dsb, kh, preferred_element_type=F32) * scale)
            dkv_ref[:, hs] += lax.dot_general(dsb, qh, (((0,), (0,)), ((), ())), preferred_element_type=F32) * scale
            dkv_ref[:, vs] += lax.dot_general(p.astype(BF16), doh, (((0,), (0,)), ((), ())), preferred_element_type=F32)
        dq_ref[...] = jnp.concatenate(dqs, axis=1).astype(BF16)

    row = pl.BlockSpec((tq, MEM_W), lambda i: (i, 0))
    full = pl.BlockSpec((N_MEM, 2 * MEM_W), lambda i: (0, 0))
    return pl.pallas_call(
        body, name=name, grid=(t // tq,), in_specs=[row, full, row], out_specs=[row, full],
        out_shape=[jax.ShapeDtypeStruct((t, MEM_W), BF16), jax.ShapeDtypeStruct((N_MEM, 2 * MEM_W), F32)],
        compiler_params=_cp(1),
    )(zq, kv, do)


def _gate_fwd(zg, bg, ys, name, tr=256):
    t = zg.shape[0]
    tr = min(tr, t)

    def body(z_ref, b_ref, y0, y1, y2, o_ref):
        g = _sigmoid(z_ref[...].astype(F32) + b_ref[...])
        o_ref[...] = (g[:, :D] * y0[...].astype(F32) + g[:, D:2 * D] * y1[...].astype(F32)
                      + g[:, 2 * D:] * y2[...].astype(F32)).astype(BF16)

    row = pl.BlockSpec((tr, D), lambda i: (i, 0))
    return pl.pallas_call(
        body, name=name, grid=(t // tr,),
        in_specs=[pl.BlockSpec((tr, 3 * D), lambda i: (i, 0)), pl.BlockSpec((1, 3 * D), lambda i: (0, 0)), row, row, row],
        out_specs=row, out_shape=jax.ShapeDtypeStruct((t, D), BF16), compiler_params=_cp(1),
    )(zg, _row(bg), *ys)


def _gate_bwd(zg, bg, ys, dmerged, name, tr=256):
    t = zg.shape[0]
    tr = min(tr, t)

    def body(z_ref, b_ref, y0, y1, y2, dm_ref, d0, d1, d2, dz_ref, db_ref):
        @pl.when(pl.program_id(0) == 0)
        def _():
            db_ref[...] = jnp.zeros_like(db_ref)

        g = _sigmoid(z_ref[...].astype(F32) + b_ref[...])
        dm = dm_ref[...].astype(F32)
        for i, (y_ref, d_ref) in enumerate(((y0, d0), (y1, d1), (y2, d2))):
            gi = g[:, i * D:(i + 1) * D]
            d_ref[...] = (dm * gi).astype(BF16)
            dz = dm * y_ref[...].astype(F32) * gi * (1.0 - gi)
            dz_ref[:, i * D:(i + 1) * D] = dz.astype(BF16)
            db_ref[0:1, i * D:(i + 1) * D] += jnp.sum(dz, axis=0, keepdims=True)

    row = pl.BlockSpec((tr, D), lambda i: (i, 0))
    wide = pl.BlockSpec((tr, 3 * D), lambda i: (i, 0))
    d0, d1, d2, dz, db = pl.pallas_call(
        body, name=name, grid=(t // tr,),
        in_specs=[wide, pl.BlockSpec((1, 3 * D), lambda i: (0, 0)), row, row, row, row],
        out_specs=[row, row, row, wide, pl.BlockSpec((8, 3 * D), lambda i: (0, 0))],
        out_shape=[jax.ShapeDtypeStruct((t, D), BF16)] * 3 + [jax.ShapeDtypeStruct((t, 3 * D), BF16),
                                                              jax.ShapeDtypeStruct((8, 3 * D), F32)],
        compiler_params=_cp(1),
    )(zg, _row(bg), *ys, dmerged)
    return (d0, d1, d2), dz, db[0]


def _adamw(w, g, m, v, name):
    shape = w.shape
    w, g, m, v = (a.reshape(-1, shape[-1]) for a in (w, g, m, v))
    r, d = w.shape
    tr = next((c for c in (512, 352, 256, 128, 64, 32, 16, 8) if r % c == 0), r)

    def body(w_ref, g_ref, m_ref, v_ref, d_ref, nm_ref, nv_ref):
        gg = g_ref[...]
        m2 = ADAM_B1 * m_ref[...] + (1.0 - ADAM_B1) * gg
        v2 = ADAM_B2 * v_ref[...] + (1.0 - ADAM_B2) * (gg * gg)
        m_hat = m2 / (1.0 - ADAM_B1 ** ADAM_STEP)
        v_hat = v2 / (1.0 - ADAM_B2 ** ADAM_STEP)
        d_ref[...] = -ADAM_LR * (m_hat / (jnp.sqrt(v_hat) + ADAM_EPS) + ADAM_WD * w_ref[...])
        nm_ref[...] = m2
        nv_ref[...] = v2

    row = pl.BlockSpec((tr, d), lambda i: (i, 0))
    outs = pl.pallas_call(
        body, name=name, grid=(r // tr,), in_specs=[row] * 4, out_specs=[row] * 3,
        out_shape=[jax.ShapeDtypeStruct((r, d), F32)] * 3, compiler_params=_cp(1),
    )(w, g, m, v)
    return [o.reshape(shape) for o in outs]


def _slab_tile(rows):
    return next((c for c in range(min(rows, 512), 15, -16) if rows % c == 0), rows)


def _sum_slots(parts, out_dtype, name):
    n, r, d = parts.shape
    tr = _slab_tile(r)

    def body(p_ref, o_ref):
        acc = p_ref[0].astype(F32)
        for s in range(1, n):
            acc = acc + p_ref[s].astype(F32)
        o_ref[...] = acc.astype(o_ref.dtype)

    return pl.pallas_call(
        body, name=name, grid=(r // tr,), in_specs=[pl.BlockSpec((n, tr, d), lambda i: (0, i, 0))],
        out_specs=pl.BlockSpec((tr, d), lambda i: (i, 0)),
        out_shape=jax.ShapeDtypeStruct((r, d), out_dtype), compiler_params=_cp(1),
    )(parts)


def _place():
    return lax.axis_index("x"), lax.axis_index("y"), lax.axis_index("c")


def _all_gather(shard, name):
    r, d = shard.shape

    def body(x_ref, out_ref, send_sems, recv_sems, local_sem):
        x, y, c = _place()
        me, sibling = (x, y, c), (x, y, 1 - c)
        chips = [(1 - x, y), (x, 1 - y), (1 - x, 1 - y)]

        def slot(px, py, pc):
            return out_ref.at[4 * px + 2 * py + pc]

        def copy(k, block, to, src=None):
            return pltpu.make_async_remote_copy(
                src_ref=slot(*block) if src is None else src, dst_ref=slot(*block),
                send_sem=send_sems.at[k], recv_sem=recv_sems.at[k], device_id=to, device_id_type=MESH)

        mine = pltpu.make_async_copy(x_ref, slot(*me), local_sem)
        mine.start()
        first = [copy(0, me, sibling, src=x_ref)]
        first += [copy(1 + j, me, (*chip, c), src=x_ref) for j, chip in enumerate(chips)]
        for cp in first:
            cp.start()
        passed = [copy(4 + j, (*chip, c), sibling) for j, chip in enumerate(chips)]
        for j, chip in enumerate(chips):
            copy(1 + j, (*chip, c), me).wait_recv()
            passed[j].start()
        copy(0, sibling, me).wait_recv()
        for j, chip in enumerate(chips):
            copy(4 + j, (*chip, 1 - c), me).wait_recv()
        for cp in first + passed:
            cp.wait_send()
        mine.wait()

    return pl.pallas_call(
        body, name=name, out_shape=jax.ShapeDtypeStruct((N_DEV, r, d), shard.dtype),
        in_specs=[pl.BlockSpec(memory_space=pl.ANY)], out_specs=pl.BlockSpec(memory_space=pl.ANY),
        scratch_shapes=[pltpu.SemaphoreType.DMA((7,)), pltpu.SemaphoreType.DMA((7,)), pltpu.SemaphoreType.DMA],
    )(shard)


HBM_SPEC = pl.BlockSpec(memory_space=pltpu.HBM)
SEM_SPEC = pl.BlockSpec(memory_space=pltpu.SEMAPHORE)
SPLIT_PARAMS = pltpu.CompilerParams(has_side_effects=pltpu.SideEffectType.DATAFLOW_SIDE_EFFECTING)


def _peers():
    x, y, c = _place()
    flip = lambda v, bit: 1 - v if bit else v
    return 4 * x + 2 * y + c, [((flip(x, k >> 2 & 1), flip(y, k >> 1 & 1), flip(c, k & 1))) for k in range(1, N_DEV)]


def _exchange_start(src, all_gather, name):
    r, d = src.shape[-2:]

    def body(src_ref, land_ref, send_sems, recv_sems, src_thru, land_thru, token):
        me, peers = _peers()
        for k, (px, py, pc) in enumerate(peers):
            part = src_ref if all_gather else src_ref.at[4 * px + 2 * py + pc]
            pltpu.make_async_remote_copy(src_ref=part, dst_ref=land_ref.at[me], send_sem=send_sems.at[k],
                                         recv_sem=recv_sems.at[k], device_id=(px, py, pc), device_id_type=MESH).start()
        token[...] = jnp.zeros_like(token)

    land = lax.empty((N_DEV, r, d), src.dtype)
    return pl.pallas_call(
        body, name=name,
        out_shape=(pltpu.SemaphoreType.DMA((N_DEV - 1,)), pltpu.SemaphoreType.DMA((N_DEV - 1,)), pltpu.HBM(src.shape, src.dtype),
                   pltpu.HBM(land.shape, land.dtype), jax.ShapeDtypeStruct((8, LANES), F32)),
        in_specs=(HBM_SPEC, HBM_SPEC), out_specs=(SEM_SPEC, SEM_SPEC, HBM_SPEC, HBM_SPEC, pl.BlockSpec(memory_space=pltpu.VMEM)),
        input_output_aliases={0: 2, 1: 3}, compiler_params=SPLIT_PARAMS,
    )(pltpu.with_memory_space_constraint(src, pltpu.HBM), pltpu.with_memory_space_constraint(land, pltpu.HBM))


def _exchange_wait(started, after, all_gather, name):
    send_sems, recv_sems, src_thru, land_thru, _ = started

    def body(src_ref, land_ref, send_sems, recv_sems, after_ref, src_out, land_out):
        me, peers = _peers()
        for k, (px, py, pc) in enumerate(peers):
            part = src_ref if all_gather else src_ref.at[4 * px + 2 * py + pc]
            cp = pltpu.make_async_remote_copy(src_ref=part, dst_ref=land_ref.at[4 * px + 2 * py + pc], send_sem=send_sems.at[k],
                                              recv_sem=recv_sems.at[k], device_id=(px, py, pc), device_id_type=MESH)
            cp.wait_send()
            cp.wait_recv()

    return pl.pallas_call(
        body, name=name,
        out_shape=(pltpu.HBM(src_thru.shape, src_thru.dtype), pltpu.HBM(land_thru.shape, land_thru.dtype)),
        in_specs=(HBM_SPEC, HBM_SPEC, SEM_SPEC, SEM_SPEC, pl.BlockSpec(memory_space=pl.ANY)), out_specs=(HBM_SPEC, HBM_SPEC),
        input_output_aliases={0: 0, 1: 1}, compiler_params=SPLIT_PARAMS,
    )(src_thru, land_thru, send_sems, recv_sems, after)


def _own_slot(land, mine):
    x, y, c = _place()
    return lax.dynamic_update_slice(land, mine[None], (4 * x + 2 * y + c, 0, 0))


BIG = (("w_in", (D, 864), 1), ("w_conv_out", (CONV_W, 128), 1), ("w_att_out", (ATT_GW, 128), 1), ("w_mem_kv", (128, D), 0),
       ("w_mem_out", (MEM_W, 128), 1), ("w_out", (128, D), 0), ("w_ffn_in", (D, 704), 1), ("w_ffn_out", (352, D), 0),
       ("conv_dw", (CONV_K, 64), 1))
SMALL = (("rel_bias", (NUM_BUCKETS, 12)), ("norm_mix_pre", (DEPTH, D)), ("b_gate", (DEPTH, 3 * D)),
         ("conv_dw_bias", (DEPTH, CONV_W)), ("conv_ln_g", (DEPTH, CONV_W)), ("conv_ln_b", (DEPTH, CONV_W)),
         ("norm_mem", (DEPTH, D)), ("norm_mix_post", (DEPTH, D)), ("norm_ffn_pre", (DEPTH, D)), ("norm_ffn_post", (DEPTH, D)))
TWIN_WEIGHTS = ("rel_bias", "norm_mix_pre", "w_in", "b_gate", "conv_dw", "conv_dw_bias", "conv_ln_g", "conv_ln_b", "w_conv_out",
                "w_att_out", "norm_mem", "w_mem_kv", "w_mem_out", "w_out", "norm_mix_post", "norm_ffn_pre", "w_ffn_in",
                "w_ffn_out", "norm_ffn_post")


BIG_INFO = {n: (s, a) for n, s, a in BIG}
GROUPS = {"mix": ("w_in", "w_conv_out", "w_att_out", "w_mem_kv", "w_mem_out", "w_out", "conv_dw"), "ffn": ("w_ffn_in", "w_ffn_out")}
TILE_BYTES_PER_LANE = 32


def _rows_of(shape, dtype):
    tile = TILE_BYTES_PER_LANE // jnp.dtype(dtype).itemsize
    return -(-int(np.prod(shape)) // (D * tile)) * tile


def _row_form(shape, axis):
    return tuple(shape) if axis == 0 else (shape[1], shape[0])


def _as_slab_rows(p, lead=()):
    shape = p.shape[len(lead):]
    rows = _rows_of(shape, p.dtype)
    if shape == (rows, D):
        return p
    n = int(np.prod(shape))
    p = jnp.pad(p.reshape(*lead, n), [(0, 0)] * len(lead) + [(0, rows * D - n)])
    return p.reshape(*lead, rows, D)


def _from_slab_rows(rows, shape, lead=()):
    if rows.shape[len(lead):] == tuple(shape):
        return rows
    n = int(np.prod(shape))
    return rows.reshape(*lead, -1)[..., :n].reshape(*lead, *shape)


def _pack(pieces, dtype):
    return jnp.concatenate([_as_slab_rows(p.astype(dtype)) for p in pieces], axis=0)


def _unpack(slab, shapes, packed_as):
    out, r0 = [], 0
    for s in shapes:
        rows = _rows_of(s, packed_as)
        out.append(_from_slab_rows(slab[r0:r0 + rows], s))
        r0 += rows
    return out


def _pack_group(blocks, layer, group):
    return _pack([blocks[n][layer] if BIG_INFO[n][1] == 0 else blocks[n][layer].T for n in GROUPS[group]], BF16)


def _full_weights(gathered, group):
    out, r0 = {}, 0
    for name in GROUPS[group]:
        r, k = _row_form(*BIG_INFO[name])
        rows = _rows_of((r, k), BF16)
        out[name] = _from_slab_rows(gathered[:, r0:r0 + rows], (r, k), lead=(N_DEV,)).reshape(N_DEV * r, k)
        r0 += rows
    return out


def _scatter_layout(grads, group):
    parts = []
    for name in GROUPS[group]:
        r, k = _row_form(*BIG_INFO[name])
        parts.append(_as_slab_rows(grads[name].astype(BF16).reshape(N_DEV, r, k), lead=(N_DEV,)))
    return jnp.concatenate(parts, axis=1)


def _unpack_blocks(slab, group):
    out = _unpack(slab, [_row_form(*BIG_INFO[n]) for n in GROUPS[group]], BF16)
    return {n: b if BIG_INFO[n][1] == 0 else b.T for n, b in zip(GROUPS[group], out)}


def _split_w_in(w_in):
    edges = np.cumsum((0,) + SEG)
    return [w_in[edges[i]:edges[i + 1]] for i in range(4)]


def _conv_taps(conv_dw):
    return jnp.pad(conv_dw.astype(F32).T, ((0, 32 - CONV_K), (0, 0)))


def _layer_fwd(l, x, h, nm, get_w, sm, bias_tabs):
    t = x.shape[0]
    wts = get_w(l, "mix", h)
    w_seg = _split_w_in(wts["w_in"])
    zc = _mm(h, w_seg[0], "fwd_in_conv", tb=True)
    za = _mm(h, w_seg[1], "fwd_in_att", tb=True, tn=1152)
    zq = _mm(h, w_seg[2], "fwd_in_memq", tb=True)
    zg = _mm(h, w_seg[3], "fwd_in_gate", tb=True)
    y, cact = _conv_fwd(zc, _conv_taps(wts["conv_dw"]), sm["conv_dw_bias"][l], sm["conv_ln_g"][l], sm["conv_ln_b"][l],
                        "conv_fwd")
    qkv, outs, lses = [], [], []
    for g, (_, dil) in enumerate(ATT_PATTERNS):
        cols = tuple(i * 3 + g for i in range(3))
        if dil == 1:
            trio, at = (za, za, za), cols
        else:
            trio, at = _to_residue_major([(za, c) for c in cols], dil, f"qkv_residue_d{dil}"), (0, 0, 0)
        o, ls = _att_fwd(*trio, bias_tabs[g], t // dil, f"att_fwd_d{dil}", cols=at)
        qkv.append((trio, at))
        outs.append(o)
        lses.append(ls)
    att, lse = _att_combine(outs, lses, "att_combine")
    kv = _mm(nm, wts["w_mem_kv"], "fwd_mem_kv")
    om = _mem_fwd(zq, kv, "mem_fwd")
    ys = (_mm(cact, wts["w_conv_out"], "fwd_conv_out", tb=True, tm=2048),
          _mm(att, wts["w_att_out"], "fwd_att_out", tb=True, tm=2048),
          _mm(om, wts["w_mem_out"], "fwd_mem_out", tb=True, tm=2048))
    merged = _gate_fwd(zg, sm["b_gate"][l], ys, "gate_fwd")
    ymix, x1, h2 = _mm_norm_res(merged, wts["w_out"], x, sm["norm_mix_post"][l], sm["norm_ffn_pre"][l], "fwd_out_norm")
    wff = get_w(l, "ffn", h2)
    gff, uff, act = _mm_swiglu_fwd(h2, wff["w_ffn_in"], "fwd_ffn_in_swiglu")
    g_next = sm["norm_mix_pre"][l + 1] if l + 1 < DEPTH else None
    f, x2, h_next = _mm_norm_res(act, wff["w_ffn_out"], x1, sm["norm_ffn_post"][l], g_next,
                                 "fwd_ffn_out_norm" if g_next is not None else "fwd_ffn_out_last")
    saved = dict(x=x, h=h, zc=zc, zq=zq, zg=zg, y=y, cact=cact, qkv=qkv, att=att, lse=lse, kv=kv, om=om, ys=ys,
                 merged=merged, ymix=ymix, x1=x1, h2=h2, gff=gff, uff=uff, act=act, f=f)
    return x2, h_next, saved


def _after(value, *tokens):
    for t in tokens:
        if t is not None:
            value = value + t[0, 0].astype(value.dtype)
    return value


def _layer_bwd(l, dx2, s, nm, mem, get_w, put_g, sm, bias_tabs, bucket_idx):
    t = dx2.shape[0]
    wts, wff = get_w(l, "mix", None), get_w(l, "ffn", None)
    gb, gf, gs = {}, {}, {}
    df, gs["norm_ffn_post"] = _norm_bwd(s["f"], sm["norm_ffn_post"][l], dx2, None, BF16, "bwd_norm_ffn_post")
    dg, du = _mm_swiglu_bwd(df, wff["w_ffn_out"], s["gff"], s["uff"], "bwd_ffn_out_d_swiglu")
    gf["w_ffn_out"] = _mm(s["act"], df, "bwd_ffn_out_w", ta=True, tm=1408)
    gf["w_ffn_in"] = jnp.concatenate([_mm(dg, s["h2"], "bwd_ffn_in_w_gate", ta=True, tm=1408),
                                      _mm(du, s["h2"], "bwd_ffn_in_w_up", ta=True, tm=1408)], axis=0)
    tok = put_g(l, "ffn", gf)
    dh2 = _mm_sum([(dg, _after(wff["w_ffn_in"][:FFN_H], tok), FFN_H // 2), (du, wff["w_ffn_in"][FFN_H:], FFN_H // 2)],
                  "bwd_ffn_in_d")
    dx1, gs["norm_ffn_pre"] = _norm_bwd(s["x1"], sm["norm_ffn_pre"][l], dh2, dx2, F32, "bwd_norm_ffn_pre")
    dymix, gs["norm_mix_post"] = _norm_bwd(s["ymix"], sm["norm_mix_post"][l], dx1, None, BF16, "bwd_norm_mix_post")
    dmerged = _mm(dymix, wts["w_out"], "bwd_out_d", tb=True)
    gb["w_out"] = _mm(s["merged"], dymix, "bwd_out_w", ta=True)
    dys, dzg, gs["b_gate"] = _gate_bwd(s["zg"], sm["b_gate"][l], s["ys"], dmerged, "gate_bwd")
    dcact = _mm(dys[0], wts["w_conv_out"], "bwd_conv_out_d", tm=2048)
    gb["w_conv_out"] = _mm(dys[0], s["cact"], "bwd_conv_out_w", ta=True)
    dzc, dconv, gs["conv_dw_bias"], gs["conv_ln_g"], gs["conv_ln_b"] = _conv_bwd(
        s["zc"], s["y"], dcact, _conv_taps(wts["conv_dw"]), sm["conv_ln_g"][l], sm["conv_ln_b"][l], "conv_bwd")
    gb["conv_dw"] = dconv[:CONV_K].T
    datt = _mm(dys[1], wts["w_att_out"], "bwd_att_out_d", tm=2048)
    gb["w_att_out"] = _mm(dys[1], s["att"], "bwd_att_out_w", ta=True)
    dqkv, rel = [], []
    for g, (_, dil) in enumerate(ATT_PATTERNS):
        trio, at = s["qkv"][g]
        if dil == 1:
            o_r, do_r, lse_r = s["att"], datt, s["lse"]
        else:
            o_r, do_r, lse_r = _to_residue_major([(s["att"], 0), (datt, 0), (s["lse"], 0)], dil, f"att_bwd_residue_d{dil}")
        dq, dk, dv, dbias = _att_bwd(*trio, bias_tabs[g], o_r, do_r, lse_r, t // dil, f"att_bwd_d{dil}", cols=at)
        dqkv.append((dq, dk, dv))
        rel.append(_bias_grad(dbias, bucket_idx[g], f"bias_grad_d{dil}")[:HEADS, :NUM_BUCKETS])
    dza = _assemble_dza(dqkv, "att_bwd_assemble")
    gs["rel_bias"] = jnp.concatenate(rel, axis=0).T
    dom = _mm(dys[2], wts["w_mem_out"], "bwd_mem_out_d", tm=2048)
    gb["w_mem_out"] = _mm(dys[2], s["om"], "bwd_mem_out_w", ta=True)
    dzq, dkv = _mem_bwd(s["zq"], s["kv"], dom, "mem_bwd")
    dkv = dkv.astype(BF16)
    gb["w_mem_kv"] = _mm(nm, dkv, "bwd_mem_kv_w", ta=True)
    dnm = _mm(dkv, wts["w_mem_kv"], "bwd_mem_kv_d", tb=True, out_dtype=F32)
    _, gs["norm_mem"] = _norm_bwd(mem, sm["norm_mem"][l], dnm, None, BF16, "bwd_norm_mem")
    segs = ((dzc, "conv", 1024), (dza, "att", 1152), (dzq, "memq", 512), (dzg, "gate", 1024))
    gb["w_in"] = jnp.concatenate([_mm(dz, s["h"], f"bwd_in_{nm_}_w", ta=True, tm=min(blk, 1152), tk=2048)
                                  for dz, nm_, blk in segs], axis=0)
    tok = put_g(l, "mix", gb)
    w_seg = _split_w_in(wts["w_in"])
    w_seg[0] = _after(w_seg[0], tok)
    dx, gs["norm_mix_pre"] = _mm_sum([(dz, w_seg[i], blk) for i, (dz, _, blk) in enumerate(segs)], "bwd_in_d_norm",
                                     norm=(s["x"], sm["norm_mix_pre"][l], dx1))
    return dx, gs


def _local_step(x, mem, target, sm, get_w, put_g, tokens=()):
    t = x.shape[0]
    bias_tabs, bucket_idx = [], []
    for g, (_, dil) in enumerate(ATT_PATTERNS):
        idx = _bucket_index(_att_tiles(t // dil)[1], dil)
        bucket_idx.append(jnp.asarray(idx))
        bias_tabs.append(_bias_table(sm["rel_bias"][:, g * HEADS:(g + 1) * HEADS], bucket_idx[g], f"bias_table_d{dil}"))
    h = _norm_plain(x, _after(sm["norm_mix_pre"][0], *tokens), "norm_first")
    saved, nms = [], []
    for l in range(DEPTH):
        nm = _norm_plain(mem, sm["norm_mem"][l], "norm_mem")
        x, h, s = _layer_fwd(l, x, h, nm, get_w, sm, bias_tabs)
        saved.append(s)
        nms.append(nm)
    loss, dx = _loss_head(x, target, "loss_head")
    gsmall = {}
    for l in reversed(range(DEPTH)):
        dx, gs = _layer_bwd(l, dx, saved[l], nms[l], mem, get_w, put_g, sm, bias_tabs, bucket_idx)
        for n, v in gs.items():
            gsmall.setdefault(n, {})[l] = v
    small = {}
    for n, _ in SMALL:
        small[n] = gsmall[n][0] + gsmall[n][1] if n == "rel_bias" else jnp.stack([gsmall[n][0], gsmall[n][1]])
    return loss, dx, small


def kernel(x, mem, rel_bias, norm_mix_pre, w_in, b_gate, conv_dw, conv_dw_bias, conv_ln_g, conv_ln_b, w_conv_out, w_att_out, norm_mem, w_mem_kv, w_mem_out, w_out, norm_mix_post, norm_ffn_pre, w_ffn_in, w_ffn_out, norm_ffn_post, loss_target, m_rel_bias, m_norm_mix_pre, m_w_in, m_b_gate, m_conv_dw, m_conv_dw_bias, m_conv_ln_g, m_conv_ln_b, m_w_conv_out, m_w_att_out, m_norm_mem, m_w_mem_kv, m_w_mem_out, m_w_out, m_norm_mix_post, m_norm_ffn_pre, m_w_ffn_in, m_w_ffn_out, m_norm_ffn_post, v_rel_bias, v_norm_mix_pre, v_w_in, v_b_gate, v_conv_dw, v_conv_dw_bias, v_conv_ln_g, v_conv_ln_b, v_w_conv_out, v_w_att_out, v_norm_mem, v_w_mem_kv, v_w_mem_out, v_w_out, v_norm_mix_post, v_norm_ffn_pre, v_w_ffn_in, v_w_ffn_out, v_norm_ffn_post):
    args = dict(locals())
    w = {n: args[n] for n in TWIN_WEIGHTS}
    m = {n: args["m_" + n] for n in TWIN_WEIGHTS}
    v = {n: args["v_" + n] for n in TWIN_WEIGHTS}
    sm = {n: w[n] for n, _ in SMALL}

    first = (0, "mix")
    keys = [(l, grp) for l in range(DEPTH) for grp in GROUPS]
    shard = {(l, grp): _pack_group(w, l, grp) for l, grp in keys}
    weights = {first: _full_weights(_all_gather(shard[first], "gather_l0_mix"), "mix")}
    settled = weights[first]["w_out"][:1, :1] * 0
    ag = {k: _exchange_start(_after(shard[k], settled), True, f"ag_start_l{k[0]}_{k[1]}") for k in keys if k != first}

    def get_w(l, grp, after):
        if (l, grp) not in weights:
            mine, land = _exchange_wait(ag[l, grp], after, True, f"ag_wait_l{l}_{grp}")
            weights[l, grp] = _full_weights(_own_slot(land, mine), grp)
        return weights[l, grp]

    rs = {}

    def put_g(l, grp, grads):
        rs[l, grp] = _exchange_start(_scatter_layout(grads, grp), False, f"rs_start_l{l}_{grp}")
        return rs[l, grp][4]

    loss, dx, gsmall = _local_step(x[0], mem[0], loss_target[0], sm, get_w, put_g, [st[4] for st in ag.values()])
    loss = lax.psum(loss, ("x", "y", "c"))

    xi, yi, ci = _place()
    me = 4 * xi + 2 * yi + ci
    g_slab = {}
    for k, st in rs.items():
        contrib, land = _exchange_wait(st, dx, False, f"rs_wait_l{k[0]}_{k[1]}")
        own = lax.dynamic_index_in_dim(contrib, me, axis=0, keepdims=False)
        g_slab[k] = _sum_slots(_own_slot(land, own), F32, f"rs_sum_l{k[0]}_{k[1]}")
    g_layers = [{**_unpack_blocks(g_slab[l, "mix"], "mix"), **_unpack_blocks(g_slab[l, "ffn"], "ffn")} for l in range(DEPTH)]
    small_shapes = [s for _, s in SMALL]
    g_small = _unpack(_sum_slots(_all_gather(_pack([gsmall[n] for n, _ in SMALL], F32), "gather_small"), F32, "sum_small"),
                      small_shapes, F32)
    grads = {n: jnp.stack([g_layers[0][n], g_layers[1][n]]) for n, _, _ in BIG}
    grads.update({n: g_small[i] for i, (n, _) in enumerate(SMALL)})

    delta, new_m, new_v = {}, {}, {}
    for n in TWIN_WEIGHTS:
        delta[n], new_m[n], new_v[n] = _adamw(w[n], grads[n], m[n], v[n], f"adamw_{n}")

    return (loss, dx[None], *[grads[n] for n in TWIN_WEIGHTS], *[delta[n] for n in TWIN_WEIGHTS],
            *[new_m[n] for n in TWIN_WEIGHTS], *[new_v[n] for n in TWIN_WEIGHTS])
```

```python
import functools

import numpy as np
import jax
import jax.numpy as jnp
from jax import lax
from jax.experimental import pallas as pl
from jax.experimental.pallas import tpu as pltpu

F32 = jnp.float32
BF16 = jnp.bfloat16

N_DEV = 8
D = 1024
DEPTH = 2
CONV_W = 512
CONV_K = 31
CONV_PAD = 16
ATT_PATTERNS = ((128, 1), (512, 4), (2048, 16))
ATT_RADIUS = 64
HEADS = 4
HEAD_DIM = 64
ATT_GW = HEADS * HEAD_DIM
ATT_W = 3 * ATT_GW
MEM_HEADS = 4
MEM_HD = 128
MEM_W = 512
N_MEM = 256
FFN_H = 2816
NUM_BUCKETS = 32
MAX_DISTANCE = 1024
RMS_EPS = 1e-6
LN_EPS = 1e-5
NEG_INF = -1e30
SEG = (2 * CONV_W, 3 * ATT_W, MEM_W, 3 * D)
ADAM_LR, ADAM_B1, ADAM_B2, ADAM_EPS, ADAM_WD, ADAM_STEP = 0.001, 0.9, 0.999, 1e-08, 0.01, 10

VMEM_LIMIT_V7X = 56 * 1024 * 1024
MESH = pl.DeviceIdType.MESH


def _cp(n_axes):
    return pltpu.CompilerParams(dimension_semantics=("arbitrary",) * n_axes, vmem_limit_bytes=VMEM_LIMIT_V7X)


def _row(v):
    return v.reshape(1, -1)


def _mm(a, b, name, *, ta=False, tb=False, out_dtype=BF16, tm=2048, tn=1024, tk=2048):
    kdim, m = a.shape if ta else a.shape[::-1]
    n, kb = b.shape if tb else b.shape[::-1]
    assert kb == kdim and a.dtype == BF16 and b.dtype == BF16
    tm, tn, tk = min(tm, m), min(tn, n), min(tk, kdim)
    assert m % tm == 0 and n % tn == 0 and kdim % tk == 0, (name, m, n, kdim, tm, tn, tk)
    nk = kdim // tk
    a_spec = pl.BlockSpec((tk, tm), lambda i, j, k: (k, i)) if ta else pl.BlockSpec((tm, tk), lambda i, j, k: (i, k))
    b_spec = pl.BlockSpec((tn, tk), lambda i, j, k: (j, k)) if tb else pl.BlockSpec((tk, tn), lambda i, j, k: (k, j))
    o_spec = pl.BlockSpec((tm, tn), lambda i, j, k: (i, j))
    dims = (((0 if ta else 1,), (1 if tb else 0,)), ((), ()))
    use_scratch = nk > 1 and out_dtype != F32

    def body(*refs):
        a_ref, b_ref, o_ref = refs[:3]
        p = lax.dot_general(a_ref[...], b_ref[...], dims, preferred_element_type=F32)
        if nk == 1:
            o_ref[...] = p.astype(o_ref.dtype)
            return
        k = pl.program_id(2)
        t_ref = refs[3] if use_scratch else o_ref

        @pl.when(k == 0)
        def _():
            t_ref[...] = p

        @pl.when(k > 0)
        def _():
            t_ref[...] += p

        if use_scratch:
            @pl.when(k == nk - 1)
            def _():
                o_ref[...] = t_ref[...].astype(o_ref.dtype)

    return pl.pallas_call(
        body, name=name, grid=(m // tm, n // tn, nk), in_specs=[a_spec, b_spec], out_specs=o_spec,
        out_shape=jax.ShapeDtypeStruct((m, n), out_dtype),
        scratch_shapes=[pltpu.VMEM((tm, tn), F32)] if use_scratch else [],
        compiler_params=_cp(3),
    )(a, b)


EPILOGUE_COLS = 512
EPILOGUE_ROWS = 512


def _chunks(size, step):
    return [(c, min(c + step, size)) for c in range(0, size, step)]


def _rms_bwd_val(v, g, dy):
    r = lax.rsqrt(jnp.mean(v * v, axis=-1, keepdims=True) + RMS_EPS)
    vhat = v * r
    dvh = dy * g
    dv = r * (dvh - vhat * jnp.mean(dvh * vhat, axis=-1, keepdims=True))
    return dv, jnp.sum(dy * vhat, axis=0, keepdims=True)


def _mm_sum(pairs, name, *, norm=None, tm=1024):
    m, n = pairs[0][0].shape[0], pairs[0][1].shape[1]
    tm = min(tm, m)
    starts, specs, total = [], [], 0
    for a, b, tk in pairs:
        assert a.shape == (m, b.shape[0]) and b.shape[1] == n and a.shape[1] % tk == 0 and a.dtype == b.dtype == BF16
        nk = a.shape[1] // tk
        starts.append((total, nk))
        step = functools.partial(lambda k, s0, nk_: jnp.clip(k - s0, 0, nk_ - 1), s0=total, nk_=nk)
        specs.append(pl.BlockSpec((tm, tk), functools.partial(lambda i, k, st: (i, st(k)), st=step)))
        specs.append(pl.BlockSpec((tk, n), functools.partial(lambda i, k, st: (st(k), 0), st=step)))
        total += nk
    np_ = 2 * len(pairs)
    row = pl.BlockSpec((tm, n), lambda i, k: (i, 0))

    def accumulate(refs, acc, k, upto=total):
        for p, (s0, nk) in enumerate(starts):
            @pl.when((k >= s0) & (k < min(s0 + nk, upto)))
            def _(p=p, s0=s0):
                part = jnp.dot(refs[2 * p][...], refs[2 * p + 1][...], preferred_element_type=F32)
                if s0 == 0:
                    @pl.when(k == 0)
                    def _():
                        acc[...] = part

                    @pl.when(k > 0)
                    def _():
                        acc[...] += part
                else:
                    acc[...] += part

    operands = [op for a, b, _ in pairs for op in (a, b)]
    if norm is None:
        def plain(*refs):
            accumulate(refs, refs[np_], pl.program_id(1))

        return pl.pallas_call(
            plain, name=name, grid=(m // tm, total), in_specs=specs, out_specs=row,
            out_shape=jax.ShapeDtypeStruct((m, n), F32), compiler_params=_cp(2),
        )(*operands)

    v, g, dres = norm

    def body(*refs):
        v_ref, g_ref, dres_ref, dv_ref, dg_ref, acc = refs[np_:np_ + 6]
        i, k = pl.program_id(0), pl.program_id(1)
        accumulate(refs, acc, k, upto=total - 1)

        @pl.when((i == 0) & (k == 0))
        def _():
            dg_ref[...] = jnp.zeros_like(dg_ref)

        @pl.when(k == total - 1)
        def _():
            a_ref, b_ref = refs[np_ - 2], refs[np_ - 1]
            for r0, r1 in _chunks(tm, EPILOGUE_ROWS):
                dy = jnp.dot(a_ref[r0:r1, :], b_ref[...], preferred_element_type=F32)
                if total > 1:
                    dy = dy + acc[r0:r1, :]
                dv, dg = _rms_bwd_val(v_ref[r0:r1, :], g_ref[...], dy)
                dv_ref[r0:r1, :] = dv + dres_ref[r0:r1, :]
                dg_ref[0:1, :] += dg

    once = pl.BlockSpec((tm, n), lambda i, k: (i, 0), pipeline_mode=pl.Buffered(1))
    dv, dg = pl.pallas_call(
        body, name=name, grid=(m // tm, total),
        in_specs=specs + [once, pl.BlockSpec((1, n), lambda i, k: (0, 0)), once],
        out_specs=[row, pl.BlockSpec((8, n), lambda i, k: (0, 0))],
        out_shape=[jax.ShapeDtypeStruct((m, n), F32), jax.ShapeDtypeStruct((8, n), F32)],
        scratch_shapes=[pltpu.VMEM((tm, n), F32)], compiler_params=_cp(2),
    )(*operands, v, _row(g), dres)
    return dv, dg[0]


def _mm_swiglu_fwd(h, w, name, *, tm=1024, tn=1408):
    m, kdim = h.shape
    hid = w.shape[0] // 2
    tm = min(tm, m)
    nj = hid // tn
    dims = (((1,), (1,)), ((), ()))

    def body(h_ref, wg_ref, wu_ref, g_ref, u_ref, a_ref):
        hv = h_ref[...]
        for c0, c1 in _chunks(tn, EPILOGUE_COLS):
            g = lax.dot_general(hv, wg_ref[c0:c1, :], dims, preferred_element_type=F32)
            u = lax.dot_general(hv, wu_ref[c0:c1, :], dims, preferred_element_type=F32)
            g_ref[:, c0:c1] = g.astype(BF16)
            u_ref[:, c0:c1] = u.astype(BF16)
            a_ref[:, c0:c1] = (g * _sigmoid(g) * u).astype(BF16)

    out = pl.BlockSpec((tm, tn), lambda i, j: (i, j))
    return pl.pallas_call(
        body, name=name, grid=(m // tm, nj),
        in_specs=[pl.BlockSpec((tm, kdim), lambda i, j: (i, 0)), pl.BlockSpec((tn, kdim), lambda i, j: (j, 0)),
                  pl.BlockSpec((tn, kdim), lambda i, j: (j + nj, 0))],
        out_specs=[out, out, out], out_shape=[jax.ShapeDtypeStruct((m, hid), BF16)] * 3, compiler_params=_cp(2),
    )(h, w, w)


def _mm_swiglu_bwd(df, w, g, u, name, *, tm=1024, tn=1408):
    m, kdim = df.shape
    hid = w.shape[0]
    tm = min(tm, m)

    def body(df_ref, w_ref, g_ref, u_ref, dg_ref, du_ref):
        dfv = df_ref[...]
        for c0, c1 in _chunks(tn, EPILOGUE_COLS):
            da = lax.dot_general(dfv, w_ref[c0:c1, :], (((1,), (1,)), ((), ())), preferred_element_type=F32)
            gg, uu = g_ref[:, c0:c1].astype(F32), u_ref[:, c0:c1].astype(F32)
            sg = _sigmoid(gg)
            dg_ref[:, c0:c1] = (da * uu * (sg * (1.0 + gg * (1.0 - sg)))).astype(BF16)
            du_ref[:, c0:c1] = (da * gg * sg).astype(BF16)

    blk = pl.BlockSpec((tm, tn), lambda i, j: (i, j))
    return pl.pallas_call(
        body, name=name, grid=(m // tm, hid // tn),
        in_specs=[pl.BlockSpec((tm, kdim), lambda i, j: (i, 0)), pl.BlockSpec((tn, kdim), lambda i, j: (j, 0)), blk, blk],
        out_specs=[blk, blk], out_shape=[jax.ShapeDtypeStruct((m, hid), BF16)] * 2, compiler_params=_cp(2),
    )(df, w, g, u)


def _mm_norm_res(a, b, xres, g, g_next, name, *, tm=1024):
    m, kdim = a.shape
    n = b.shape[1]
    tm = min(tm, m)
    two = g_next is not None

    def body(*refs):
        a_ref, b_ref, x_ref, g_ref = refs[:4]
        y_ref, xn_ref = refs[4 + two], refs[5 + two]
        for r0, r1 in _chunks(tm, EPILOGUE_ROWS):
            y = jnp.dot(a_ref[r0:r1, :], b_ref[...], preferred_element_type=F32)
            y_ref[r0:r1, :] = y.astype(BF16)
            xn = x_ref[r0:r1, :] + _rms_val(y, g_ref[...])
            xn_ref[r0:r1, :] = xn
            if two:
                refs[7][r0:r1, :] = _rms_val(xn, refs[4][...]).astype(BF16)

    row = pl.BlockSpec((tm, n), lambda i: (i, 0))
    vec = pl.BlockSpec((1, n), lambda i: (0, 0))
    outs = pl.pallas_call(
        body, name=name, grid=(m // tm,),
        in_specs=[pl.BlockSpec((tm, kdim), lambda i: (i, 0)), pl.BlockSpec((kdim, n), lambda i: (0, 0)), row, vec] + ([vec] if two else []),
        out_specs=[row, row] + ([row] if two else []),
        out_shape=[jax.ShapeDtypeStruct((m, n), BF16), jax.ShapeDtypeStruct((m, n), F32)]
        + ([jax.ShapeDtypeStruct((m, n), BF16)] if two else []),
        compiler_params=_cp(1),
    )(*((a, b, xres, _row(g)) + ((_row(g_next),) if two else ())))
    return (outs[0], outs[1], outs[2]) if two else (outs[0], outs[1], None)


def _rms_val(v, g):
    return v * lax.rsqrt(jnp.mean(v * v, axis=-1, keepdims=True) + RMS_EPS) * g


def _norm_plain(v, g, name, tr=512):
    t, d = v.shape
    tr = min(tr, t)

    def body(v_ref, g_ref, o_ref):
        o_ref[...] = _rms_val(v_ref[...], g_ref[...]).astype(BF16)

    return pl.pallas_call(
        body, name=name, grid=(t // tr,),
        in_specs=[pl.BlockSpec((tr, d), lambda i: (i, 0)), pl.BlockSpec((1, d), lambda i: (0, 0))],
        out_specs=pl.BlockSpec((tr, d), lambda i: (i, 0)),
        out_shape=jax.ShapeDtypeStruct((t, d), BF16), compiler_params=_cp(1),
    )(v, _row(g))


def _norm_bwd(v, g, dout, dres, out_dtype, name, tr=512):
    t, d = v.shape
    tr = min(tr, t)
    nt = t // tr
    has_res = dres is not None

    def body(*refs):
        v_ref, g_ref, do_ref = refs[:3]
        dv_ref, dg_ref = refs[3 + has_res], refs[4 + has_res]
        i = pl.program_id(0)
        vv = v_ref[...].astype(F32)
        dy = do_ref[...].astype(F32)
        r = lax.rsqrt(jnp.mean(vv * vv, axis=-1, keepdims=True) + RMS_EPS)
        vhat = vv * r
        dvh = dy * g_ref[...]
        dv = r * (dvh - vhat * jnp.mean(dvh * vhat, axis=-1, keepdims=True))
        if has_res:
            dv = dv + refs[3][...]
        dv_ref[...] = dv.astype(dv_ref.dtype)
        part = jnp.sum(dy * vhat, axis=0, keepdims=True)

        @pl.when(i == 0)
        def _():
            dg_ref[...] = jnp.zeros_like(dg_ref)

        dg_ref[0:1, :] += part

    row = pl.BlockSpec((tr, d), lambda i: (i, 0))
    dv, dg = pl.pallas_call(
        body, name=name, grid=(nt,),
        in_specs=[row, pl.BlockSpec((1, d), lambda i: (0, 0)), row] + ([row] if has_res else []),
        out_specs=[row, pl.BlockSpec((8, d), lambda i: (0, 0))],
        out_shape=[jax.ShapeDtypeStruct((t, d), out_dtype), jax.ShapeDtypeStruct((8, d), F32)],
        compiler_params=_cp(1),
    )(*((v, _row(g), dout) + ((dres,) if has_res else ())))
    return dv, dg[0]


def _loss_head(y, target, name, tr=512):
    t, d = y.shape
    tr = min(tr, t)
    nt = t // tr

    def body(y_ref, t_ref, dy_ref, l_ref):
        i = pl.program_id(0)
        err = y_ref[...] - t_ref[...]
        dy_ref[...] = err * (1.0 / d)

        @pl.when(i == 0)
        def _():
            l_ref[...] = jnp.zeros_like(l_ref)

        l_ref[...] += jnp.sum(err * err) * (0.5 / d)

    row = pl.BlockSpec((tr, d), lambda i: (i, 0))
    dy, l = pl.pallas_call(
        body, name=name, grid=(nt,), in_specs=[row, row],
        out_specs=[row, pl.BlockSpec((8, 128), lambda i: (0, 0))],
        out_shape=[jax.ShapeDtypeStruct((t, d), F32), jax.ShapeDtypeStruct((8, 128), F32)],
        compiler_params=_cp(1),
    )(y, target)
    return l[0, 0], dy


def _sigmoid(v):
    return 1.0 / (1.0 + jnp.exp(-v))


def _halo_specs(tq, width, nt, halo):
    per = tq // halo
    last = nt * per - 1
    main = pl.BlockSpec((tq, width), lambda i: (i, 0))
    prev = pl.BlockSpec((halo, width), lambda i: (jnp.maximum(i * per - 1, 0), 0))
    nxt = pl.BlockSpec((halo, width), lambda i: (jnp.minimum((i + 1) * per, last), 0))
    return prev, main, nxt


def _glu_window(zp, zm, zn, i, nt):
    def glu(z):
        z = z.astype(F32)
        return z[:, :CONV_W] * _sigmoid(z[:, CONV_W:])

    up = jnp.where(i > 0, glu(zp), 0.0)
    un = jnp.where(i < nt - 1, glu(zn), 0.0)
    return jnp.concatenate([up, glu(zm), un], axis=0)


def _shifted(win, shift, rows):
    if shift == 0:
        return win[:rows]
    return pltpu.roll(win, win.shape[0] - shift, 0)[:rows]


def _conv_fwd(zc, w, b, ln_g, ln_b, name, tq=512):
    t = zc.shape[0]
    tq = min(tq, t)
    nt = t // tq

    def body(zp_ref, zm_ref, zn_ref, w_ref, b_ref, g_ref, bb_ref, y_ref, c_ref):
        i = pl.program_id(0)
        win = _glu_window(zp_ref[...], zm_ref[...], zn_ref[...], i, nt)
        wv = w_ref[...]
        y = jnp.zeros((tq, CONV_W), F32) + b_ref[...]
        for k in range(CONV_K):
            y = y + _shifted(win, k + 1, tq) * wv[k:k + 1, :]
        y_ref[...] = y
        mu = jnp.mean(y, axis=-1, keepdims=True)
        yc = y - mu
        ln = yc * lax.rsqrt(jnp.mean(yc * yc, axis=-1, keepdims=True) + LN_EPS) * g_ref[...] + bb_ref[...]
        c_ref[...] = (ln * _sigmoid(ln)).astype(BF16)

    vec = pl.BlockSpec((1, CONV_W), lambda i: (0, 0))
    out = pl.BlockSpec((tq, CONV_W), lambda i: (i, 0))
    return pl.pallas_call(
        body, name=name, grid=(nt,),
        in_specs=[*_halo_specs(tq, 2 * CONV_W, nt, CONV_PAD), pl.BlockSpec((32, CONV_W), lambda i: (0, 0)), vec, vec, vec],
        out_specs=[out, out],
        out_shape=[jax.ShapeDtypeStruct((t, CONV_W), F32), jax.ShapeDtypeStruct((t, CONV_W), BF16)],
        compiler_params=_cp(1),
    )(zc, zc, zc, w, _row(b), _row(ln_g), _row(ln_b))


def _conv_bwd(zc, y, dc, w, ln_g, ln_b, name, tq=256):
    t = zc.shape[0]
    tq = min(tq, t)
    nt = t // tq
    rows = tq + 2 * CONV_PAD

    def body(zp_ref, zm_ref, zn_ref, yp_ref, ym_ref, yn_ref, dp_ref, dm_ref, dn_ref, w_ref, g_ref, bb_ref,
             dz_ref, dw_ref, db_ref, dg_ref, dbb_ref):
        i = pl.program_id(0)
        uwin = _glu_window(zp_ref[...], zm_ref[...], zn_ref[...], i, nt)
        ywin = jnp.concatenate([yp_ref[...], ym_ref[...], yn_ref[...]], axis=0)
        dcw = jnp.concatenate([dp_ref[...], dm_ref[...], dn_ref[...]], axis=0).astype(F32)
        mu = jnp.mean(ywin, axis=-1, keepdims=True)
        yc = ywin - mu
        rstd = lax.rsqrt(jnp.mean(yc * yc, axis=-1, keepdims=True) + LN_EPS)
        yhat = yc * rstd
        ln = yhat * g_ref[...] + bb_ref[...]
        sg = _sigmoid(ln)
        dl = dcw * (sg * (1.0 + ln * (1.0 - sg)))
        ridx = lax.broadcasted_iota(jnp.int32, (rows, 1), 0)
        inside = ((ridx >= CONV_PAD) | (i > 0)) & ((ridx < tq + CONV_PAD) | (i < nt - 1))
        dl = jnp.where(inside, dl, 0.0)
        dyh = dl * g_ref[...]
        dy = rstd * (dyh - jnp.mean(dyh, axis=-1, keepdims=True) - yhat * jnp.mean(dyh * yhat, axis=-1, keepdims=True))
        dy = jnp.where(inside, dy, 0.0)
        main = slice(CONV_PAD, CONV_PAD + tq)
        dlm, yhm, dym = dl[main], yhat[main], dy[main]

        @pl.when(i == 0)
        def _():
            dw_ref[...] = jnp.zeros_like(dw_ref)
            db_ref[...] = jnp.zeros_like(db_ref)
            dg_ref[...] = jnp.zeros_like(dg_ref)
            dbb_ref[...] = jnp.zeros_like(dbb_ref)

        dg_ref[0:1, :] += jnp.sum(dlm * yhm, axis=0, keepdims=True)
        dbb_ref[0:1, :] += jnp.sum(dlm, axis=0, keepdims=True)
        db_ref[0:1, :] += jnp.sum(dym, axis=0, keepdims=True)
        wv = w_ref[...]
        du = jnp.zeros((tq, CONV_W), F32)
        for k in range(CONV_K):
            du = du + _shifted(dy, 2 * CONV_PAD - 1 - k, tq) * wv[k:k + 1, :]
            dw_ref[k:k + 1, :] += jnp.sum(dym * _shifted(uwin, k + 1, tq), axis=0, keepdims=True)
        zm = zm_ref[...].astype(F32)
        a, gt = zm[:, :CONV_W], zm[:, CONV_W:]
        sgt = _sigmoid(gt)
        dz_ref[:, :CONV_W] = (du * sgt).astype(BF16)
        dz_ref[:, CONV_W:] = (du * a * sgt * (1.0 - sgt)).astype(BF16)

    vec = pl.BlockSpec((1, CONV_W), lambda i: (0, 0))
    acc = pl.BlockSpec((8, CONV_W), lambda i: (0, 0))
    dz, dw, db, dg, dbb = pl.pallas_call(
        body, name=name, grid=(nt,),
        in_specs=[*_halo_specs(tq, 2 * CONV_W, nt, CONV_PAD), *_halo_specs(tq, CONV_W, nt, CONV_PAD),
                  *_halo_specs(tq, CONV_W, nt, CONV_PAD), pl.BlockSpec((32, CONV_W), lambda i: (0, 0)), vec, vec],
        out_specs=[pl.BlockSpec((tq, 2 * CONV_W), lambda i: (i, 0)), pl.BlockSpec((32, CONV_W), lambda i: (0, 0)), acc, acc, acc],
        out_shape=[jax.ShapeDtypeStruct((t, 2 * CONV_W), BF16), jax.ShapeDtypeStruct((32, CONV_W), F32),
                   jax.ShapeDtypeStruct((8, CONV_W), F32), jax.ShapeDtypeStruct((8, CONV_W), F32),
                   jax.ShapeDtypeStruct((8, CONV_W), F32)],
        compiler_params=_cp(1),
    )(zc, zc, zc, y, y, y, dc, dc, dc, w, _row(ln_g), _row(ln_b))
    return dz, dw, db[0], dg[0], dbb[0]


def _att_tiles(seq_len):
    tq = min(256, seq_len)
    return tq, min(128, tq)


def _t5_bucket_np(rel):
    nb = NUM_BUCKETS // 2
    max_exact = nb // 2
    ret = np.where(rel > 0, nb, 0)
    n = np.abs(rel)
    nf = np.maximum(n, 1).astype(np.float32)
    large = max_exact + (np.log(nf / np.float32(max_exact)) / np.float32(np.log(MAX_DISTANCE / max_exact))
                         * np.float32(nb - max_exact)).astype(np.int32)
    large = np.minimum(large, nb - 1)
    return ret + np.where(n < max_exact, n, large)


def _bucket_index(sb, dilation):
    off = np.arange(sb + 2 * ATT_RADIUS)[None, :] - ATT_RADIUS - np.arange(sb)[:, None]
    idx = _t5_bucket_np(off * dilation).astype(np.int32)
    return np.where(np.abs(off) <= ATT_RADIUS, idx, -1).astype(np.int32)


def _bias_table(tab, idx, name):
    sb, w = idx.shape

    def body(tab_ref, idx_ref, o_ref):
        ix = idx_ref[...]
        for h in range(HEADS):
            acc = jnp.full((sb, w), NEG_INF, F32)
            for b in range(NUM_BUCKETS):
                acc = jnp.where(ix == b, tab_ref[b, h], acc)
            o_ref[h] = acc

    return pl.pallas_call(
        body, name=name, out_shape=jax.ShapeDtypeStruct((HEADS, sb, w), F32),
        in_specs=[pl.BlockSpec(memory_space=pltpu.SMEM), pl.BlockSpec(memory_space=pltpu.VMEM)],
        out_specs=pl.BlockSpec(memory_space=pltpu.VMEM),
    )(tab, idx)


def _qkv_specs(tq, per, last_blk, cols, tile):
    def main(col):
        return pl.BlockSpec((tq, ATT_GW), lambda *g: (tile(*g), col))

    def prev(col):
        return pl.BlockSpec((ATT_RADIUS, ATT_GW), lambda *g: (jnp.maximum(tile(*g) * per - 1, 0), col))

    def nxt(col):
        return pl.BlockSpec((ATT_RADIUS, ATT_GW), lambda *g: (jnp.minimum((tile(*g) + 1) * per, last_blk), col))

    cq, ck, cv = cols
    return [main(cq), prev(ck), main(ck), nxt(ck), prev(cv), main(cv), nxt(cv)]


def _att_fwd(q, k, v, bias, seq_len, name, cols=(0, 0, 0)):
    t = q.shape[0]
    tq, sb = _att_tiles(seq_len)
    nbl = seq_len // tq
    nt = t // tq
    per = tq // ATT_RADIUS
    w = sb + 2 * ATT_RADIUS
    last_blk = t // ATT_RADIUS - 1

    def body(q_ref, kp_ref, kc_ref, kn_ref, vp_ref, vc_ref, vn_ref, b_ref, o_ref, l_ref):
        n = pl.program_id(0) % nbl
        kw = jnp.concatenate([kp_ref[...], kc_ref[...], kn_ref[...]], axis=0)
        vw = jnp.concatenate([vp_ref[...], vc_ref[...], vn_ref[...]], axis=0)
        kpos = n * tq - ATT_RADIUS + lax.broadcasted_iota(jnp.int32, (1, tq + 2 * ATT_RADIUS), 1)
        valid = (kpos >= 0) & (kpos < seq_len)
        for j in range(tq // sb):
            qj = q_ref[j * sb:(j + 1) * sb, :]
            kj, vj, okj = kw[j * sb:j * sb + w], vw[j * sb:j * sb + w], valid[:, j * sb:j * sb + w]
            outs, lses = [], []
            for h in range(HEADS):
                hs = slice(h * HEAD_DIM, (h + 1) * HEAD_DIM)
                s = lax.dot_general(qj[:, hs], kj[:, hs], (((1,), (1,)), ((), ())), preferred_element_type=F32)
                s = jnp.where(okj, s * (HEAD_DIM ** -0.5) + b_ref[h], NEG_INF)
                m = jnp.max(s, axis=-1, keepdims=True)
                e = jnp.exp(s - m)
                den = jnp.sum(e, axis=-1, keepdims=True)
                o = jnp.dot(e.astype(BF16), vj[:, hs], preferred_element_type=F32) / den
                outs.append(o)
                lses.append(jnp.broadcast_to(m + jnp.log(den), (sb, HEAD_DIM)))
            o_ref[j * sb:(j + 1) * sb, :] = jnp.concatenate(outs, axis=1).astype(BF16)
            l_ref[j * sb:(j + 1) * sb, :] = jnp.concatenate(lses, axis=1)

    main = pl.BlockSpec((tq, ATT_GW), lambda i: (i, 0))
    return pl.pallas_call(
        body, name=name, grid=(nt,),
        in_specs=[*_qkv_specs(tq, per, last_blk, cols, lambda i: i), pl.BlockSpec((HEADS, sb, w), lambda i: (0, 0, 0))],
        out_specs=[main, main],
        out_shape=[jax.ShapeDtypeStruct((t, ATT_GW), BF16), jax.ShapeDtypeStruct((t, ATT_GW), F32)],
        compiler_params=_cp(1),
    )(q, k, k, k, v, v, v, bias)


PERM_ROWS = 2048
LANES = 128


def _spec_rm(d, tr, width, col=0):
    return pl.BlockSpec((d, tr // d, width), lambda i: (0, i, col))


def _rm_view(a, d):
    return a.reshape(d, a.shape[0] // d, a.shape[1])


def _gather_residues(scr, val, d, out_ref, col0):
    tr, w = val.shape
    for c in range(w // LANES):
        scr[...] = val[:, c * LANES:(c + 1) * LANES].astype(F32)
        for r in range(d):
            out_ref[r, :, col0 + c * LANES:col0 + (c + 1) * LANES] = scr[pl.ds(r, tr // d, stride=d), :].astype(out_ref.dtype)


def _scatter_residues(scr, ref, d):
    w = ref.shape[2]
    cols = []
    for c in range(w // LANES):
        for r in range(d):
            scr[pl.ds(r, scr.shape[0] // d, stride=d), :] = ref[r, :, c * LANES:(c + 1) * LANES].astype(F32)
        cols.append(scr[...])
    return jnp.concatenate(cols, axis=1)


def _to_residue_major(srcs, d, name, tr=PERM_ROWS):
    t = srcs[0][0].shape[0]
    tr = min(tr, t)
    n = len(srcs)

    def body(*refs):
        scr = refs[2 * n]
        for k in range(n):
            _gather_residues(scr, refs[k][...], d, refs[n + k], 0)

    outs = pl.pallas_call(
        body, name=name, grid=(t // tr,),
        in_specs=[pl.BlockSpec((tr, ATT_GW), functools.partial(lambda i, col: (i, col), col=col)) for _, col in srcs],
        out_specs=[_spec_rm(d, tr, ATT_GW)] * n,
        out_shape=[jax.ShapeDtypeStruct((d, t // d, ATT_GW), a.dtype) for a, _ in srcs],
        scratch_shapes=[pltpu.VMEM((tr, LANES), F32)], compiler_params=_cp(1),
    )(*[a for a, _ in srcs])
    return [o.reshape(t, ATT_GW) for o in outs]


def _att_combine(outs, lses, name, tr=PERM_ROWS):
    t = outs[0].shape[0]
    tr = min(tr, t)
    dils = [d for _, d in ATT_PATTERNS]

    def body(o0, o1, o2, l0, l1, l2, a_ref, lse_ref, scr):
        os_, ls_ = [], []
        for o_ref, l_ref, d in zip((o0, o1, o2), (l0, l1, l2), dils):
            if d == 1:
                os_.append(o_ref[...].astype(F32))
                ls_.append(l_ref[...])
            else:
                os_.append(_scatter_residues(scr, o_ref, d))
                ls_.append(_scatter_residues(scr, l_ref, d))
        la, lb, lc = ls_
        m = jnp.maximum(jnp.maximum(la, lb), lc)
        ea, eb, ec = jnp.exp(la - m), jnp.exp(lb - m), jnp.exp(lc - m)
        den = ea + eb + ec
        a_ref[...] = ((ea * os_[0] + eb * os_[1] + ec * os_[2]) / den).astype(BF16)
        lse_ref[...] = m + jnp.log(den)

    row = pl.BlockSpec((tr, ATT_GW), lambda i: (i, 0))
    specs = [row if d == 1 else _spec_rm(d, tr, ATT_GW) for d in dils]
    views = lambda arrs: [a if d == 1 else _rm_view(a, d) for a, d in zip(arrs, dils)]
    return pl.pallas_call(
        body, name=name, grid=(t // tr,), in_specs=specs * 2, out_specs=[row, row],
        out_shape=[jax.ShapeDtypeStruct((t, ATT_GW), BF16), jax.ShapeDtypeStruct((t, ATT_GW), F32)],
        scratch_shapes=[pltpu.VMEM((tr, LANES), F32)], compiler_params=_cp(1),
    )(*views(outs), *views(lses))


def _assemble_dza(dqkv, name, tr=PERM_ROWS):
    t = dqkv[0][0].shape[0]
    tr = min(tr, t)
    dils = [d for _, d in ATT_PATTERNS]

    def body(*refs):
        o_ref, scr = refs[9], refs[10]
        for g, d in enumerate(dils):
            for c in range(3):
                ref = refs[g * 3 + c]
                val = ref[...] if d == 1 else _scatter_residues(scr, ref, d).astype(BF16)
                o_ref[:, (c * 3 + g) * ATT_GW:(c * 3 + g + 1) * ATT_GW] = val

    row = pl.BlockSpec((tr, ATT_GW), lambda i: (i, 0))
    specs = [row if d == 1 else _spec_rm(d, tr, ATT_GW) for d in dils for _ in range(3)]
    args = [a if d == 1 else _rm_view(a, d) for trio, d in zip(dqkv, dils) for a in trio]
    return pl.pallas_call(
        body, name=name, grid=(t // tr,), in_specs=specs, out_specs=pl.BlockSpec((tr, 3 * ATT_W), lambda i: (i, 0)),
        out_shape=jax.ShapeDtypeStruct((t, 3 * ATT_W), BF16),
        scratch_shapes=[pltpu.VMEM((tr, LANES), F32)], compiler_params=_cp(1),
    )(*args)


def _att_bwd(q, k, v, bias, o, do, lse, seq_len, name, cols=(0, 0, 0)):
    t = q.shape[0]
    tq, sb = _att_tiles(seq_len)
    nbl = seq_len // tq
    n_res = t // seq_len
    per = tq // ATT_RADIUS
    w = sb + 2 * ATT_RADIUS
    last_blk = t // ATT_RADIUS - 1
    acc_cols = 2 * tq - sb + w
    scale = HEAD_DIM ** -0.5

    def body(q_ref, kp_ref, kc_ref, kn_ref, vp_ref, vc_ref, vn_ref, b_ref, o_ref, do_ref, l_ref,
             dq_ref, dk_ref, dv_ref, db_ref, ak_ref, av_ref):
        r, n = pl.program_id(0), pl.program_id(1)

        @pl.when((r == 0) & (n == 0))
        def _():
            db_ref[...] = jnp.zeros_like(db_ref)

        @pl.when(n == 0)
        def _():
            ak_ref[...] = jnp.zeros_like(ak_ref)
            av_ref[...] = jnp.zeros_like(av_ref)

        @pl.when(n < nbl)
        def _():
            kw = jnp.concatenate([kp_ref[...], kc_ref[...], kn_ref[...]], axis=0)
            vw = jnp.concatenate([vp_ref[...], vc_ref[...], vn_ref[...]], axis=0)
            kpos = n * tq - ATT_RADIUS + lax.broadcasted_iota(jnp.int32, (1, tq + 2 * ATT_RADIUS), 1)
            valid = (kpos >= 0) & (kpos < seq_len)
            for j in range(tq // sb):
                rows = slice(j * sb, (j + 1) * sb)
                qj, doj = q_ref[rows, :], do_ref[rows, :]
                dd = doj.astype(F32) * o_ref[rows, :].astype(F32)
                lj = l_ref[rows, :]
                kj, vj, okj = kw[j * sb:j * sb + w], vw[j * sb:j * sb + w], valid[:, j * sb:j * sb + w]
                dqs, dks, dvs, dss = [], [], [], []
                for h in range(HEADS):
                    hs = slice(h * HEAD_DIM, (h + 1) * HEAD_DIM)
                    s = lax.dot_general(qj[:, hs], kj[:, hs], (((1,), (1,)), ((), ())), preferred_element_type=F32)
                    s = jnp.where(okj, s * scale + b_ref[h], NEG_INF)
                    p = jnp.exp(s - lj[:, h * HEAD_DIM:h * HEAD_DIM + 1])
                    dp = lax.dot_general(doj[:, hs], vj[:, hs], (((1,), (1,)), ((), ())), preferred_element_type=F32)
                    ds = p * (dp - jnp.sum(dd[:, hs], axis=-1, keepdims=True))
                    dss.append(ds)
                    dsb = ds.astype(BF16)
                    dqs.append(jnp.dot(dsb, kj[:, hs], preferred_element_type=F32) * scale)
                    dks.append(lax.dot_general(qj[:, hs], dsb, (((0,), (0,)), ((), ())), preferred_element_type=F32) * scale)
                    dvs.append(lax.dot_general(doj[:, hs], p.astype(BF16), (((0,), (0,)), ((), ())), preferred_element_type=F32))
                dq_ref[rows, :] = jnp.concatenate(dqs, axis=1).astype(BF16)
                c0 = tq + j * sb
                ak_ref[:, c0:c0 + w] += jnp.concatenate(dks, axis=0)
                av_ref[:, c0:c0 + w] += jnp.concatenate(dvs, axis=0)
                for h in range(HEADS):
                    db_ref[h] += dss[h]

        dk_ref[...] = ak_ref[:, ATT_RADIUS:ATT_RADIUS + tq].T.astype(BF16)
        dv_ref[...] = av_ref[:, ATT_RADIUS:ATT_RADIUS + tq].T.astype(BF16)
        keep = acc_cols - tq
        nk, nv = ak_ref[:, tq:acc_cols], av_ref[:, tq:acc_cols]
        ak_ref[:, 0:keep] = nk
        av_ref[:, 0:keep] = nv
        ak_ref[:, keep:acc_cols] = jnp.zeros((ATT_GW, tq), F32)
        av_ref[:, keep:acc_cols] = jnp.zeros((ATT_GW, tq), F32)

    def tile(r, n):
        return r * nbl + jnp.minimum(n, nbl - 1)

    main = pl.BlockSpec((tq, ATT_GW), lambda r, n: (tile(r, n), 0))
    lag = pl.BlockSpec((tq, ATT_GW), lambda r, n: (r * nbl + jnp.maximum(n - 1, 0), 0))
    bspec = pl.BlockSpec((HEADS, sb, w), lambda r, n: (0, 0, 0))
    return pl.pallas_call(
        body, name=name, grid=(n_res, nbl + 1),
        in_specs=[*_qkv_specs(tq, per, last_blk, cols, tile), bspec, main, main, main],
        out_specs=[main, lag, lag, bspec],
        out_shape=[jax.ShapeDtypeStruct((t, ATT_GW), BF16)] * 3 + [jax.ShapeDtypeStruct((HEADS, sb, w), F32)],
        scratch_shapes=[pltpu.VMEM((ATT_GW, acc_cols), F32), pltpu.VMEM((ATT_GW, acc_cols), F32)],
        compiler_params=_cp(2),
    )(q, k, k, k, v, v, v, bias, o, do, lse)


def _bias_grad(db, idx, name):
    _, sb, w = db.shape

    def body(db_ref, idx_ref, o_ref):
        ix = idx_ref[...]
        lane = lax.broadcasted_iota(jnp.int32, (1, 128), 1)
        rows = []
        for h in range(HEADS):
            d = db_ref[h]
            acc = jnp.zeros((1, 128), F32)
            for b in range(NUM_BUCKETS):
                acc = acc + jnp.where(lane == b, jnp.sum(jnp.where(ix == b, d, 0.0)), 0.0)
            rows.append(acc)
        o_ref[...] = jnp.concatenate(rows + [jnp.zeros((8 - HEADS, 128), F32)], axis=0)

    return pl.pallas_call(
        body, name=name, out_shape=jax.ShapeDtypeStruct((8, 128), F32),
        in_specs=[pl.BlockSpec(memory_space=pltpu.VMEM), pl.BlockSpec(memory_space=pltpu.VMEM)],
        out_specs=pl.BlockSpec(memory_space=pltpu.VMEM),
    )(db, idx)


def _mem_fwd(zq, kv, name, tq=512):
    t = zq.shape[0]
    tq = min(tq, t)
    scale = MEM_HD ** -0.5

    def body(q_ref, kv_ref, o_ref):
        outs = []
        for h in range(MEM_HEADS):
            hs = slice(h * MEM_HD, (h + 1) * MEM_HD)
            kh = kv_ref[:, h * MEM_HD:(h + 1) * MEM_HD]
            vh = kv_ref[:, MEM_W + h * MEM_HD:MEM_W + (h + 1) * MEM_HD]
            s = lax.dot_general(q_ref[:, hs], kh, (((1,), (1,)), ((), ())), preferred_element_type=F32) * scale
            e = jnp.exp(s - jnp.max(s, axis=-1, keepdims=True))
            den = jnp.sum(e, axis=-1, keepdims=True)
            outs.append(jnp.dot(e.astype(BF16), vh, preferred_element_type=F32) / den)
        o_ref[...] = jnp.concatenate(outs, axis=1).astype(BF16)

    row = pl.BlockSpec((tq, MEM_W), lambda i: (i, 0))
    return pl.pallas_call(
        body, name=name, grid=(t // tq,),
        in_specs=[row, pl.BlockSpec((N_MEM, 2 * MEM_W), lambda i: (0, 0))], out_specs=row,
        out_shape=jax.ShapeDtypeStruct((t, MEM_W), BF16), compiler_params=_cp(1),
    )(zq, kv)


def _mem_bwd(zq, kv, do, name, tq=512):
    t = zq.shape[0]
    tq = min(tq, t)
    scale = MEM_HD ** -0.5

    def body(q_ref, kv_ref, do_ref, dq_ref, dkv_ref):
        @pl.when(pl.program_id(0) == 0)
        def _():
            dkv_ref[...] = jnp.zeros_like(dkv_ref)

        dqs = []
        for h in range(MEM_HEADS):
            hs = slice(h * MEM_HD, (h + 1) * MEM_HD)
            vs = slice(MEM_W + h * MEM_HD, MEM_W + (h + 1) * MEM_HD)
            qh, doh = q_ref[:, hs], do_ref[:, hs]
            kh, vh = kv_ref[:, hs], kv_ref[:, vs]
            s = lax.dot_general(qh, kh, (((1,), (1,)), ((), ())), preferred_element_type=F32) * scale
            e = jnp.exp(s - jnp.max(s, axis=-1, keepdims=True))
            p = e / jnp.sum(e, axis=-1, keepdims=True)
            dp = lax.dot_general(doh, vh, (((1,), (1,)), ((), ())), preferred_element_type=F32)
            ds = p * (dp - jnp.sum(p * dp, axis=-1, keepdims=True))
            dsb = ds.astype(BF16)
            dqs.append(jnp.dot(dsb, kh, preferred_element_type=F32) * scale)
            dkv_ref[:, hs] += lax.dot_general(dsb, qh, (((0,), (0,)), ((), ())), preferred_element_type=F32) * scale
            dkv_ref[:, vs] += lax.dot_general(p.astype(BF16), doh, (((0,), (0,)), ((), ())), preferred_element_type=F32)
        dq_ref[...] = jnp.concatenate(dqs, axis=1).astype(BF16)

    row = pl.BlockSpec((tq, MEM_W), lambda i: (i, 0))
    full = pl.BlockSpec((N_MEM, 2 * MEM_W), lambda i: (0, 0))
    return pl.pallas_call(
        body, name=name, grid=(t // tq,), in_specs=[row, full, row], out_specs=[row, full],
        out_shape=[jax.ShapeDtypeStruct((t, MEM_W), BF16), jax.ShapeDtypeStruct((N_MEM, 2 * MEM_W), F32)],
        compiler_params=_cp(1),
    )(zq, kv, do)


NT = (((1,), (1,)), ((), ()))


def _branches_fwd(acts, ws, zg, bg, name, tr=512):
    t = zg.shape[0]
    tr = min(tr, t)

    def body(a0, a1, a2, w0, w1, w2, z_ref, b_ref, y0, y1, y2, o_ref):
        merged = None
        for i, (a_ref, w_ref, y_ref) in enumerate(((a0, w0, y0), (a1, w1, y1), (a2, w2, y2))):
            y = lax.dot_general(a_ref[...], w_ref[...], NT, preferred_element_type=F32)
            y_ref[...] = y.astype(BF16)
            g = _sigmoid(z_ref[:, i * D:(i + 1) * D].astype(F32) + b_ref[:, i * D:(i + 1) * D])
            merged = g * y if merged is None else merged + g * y
        o_ref[...] = merged.astype(BF16)

    row = pl.BlockSpec((tr, D), lambda i: (i, 0))
    outs = pl.pallas_call(
        body, name=name, grid=(t // tr,),
        in_specs=[pl.BlockSpec((tr, a.shape[1]), lambda i: (i, 0)) for a in acts]
        + [pl.BlockSpec(w.shape, lambda i: (0, 0)) for w in ws]
        + [pl.BlockSpec((tr, 3 * D), lambda i: (i, 0)), pl.BlockSpec((1, 3 * D), lambda i: (0, 0))],
        out_specs=[row] * 4, out_shape=[jax.ShapeDtypeStruct((t, D), BF16)] * 4, compiler_params=_cp(1),
    )(*acts, *ws, zg, _row(bg))
    return tuple(outs[:3]), outs[3]


def _branches_bwd(dymix, w_out, ws, zg, bg, ys, name, tr=512):
    t = zg.shape[0]
    tr = min(tr, t)

    def body(dy_ref, wo_ref, w0, w1, w2, z_ref, b_ref, y0, y1, y2, d0, d1, d2, dz_ref, db_ref, da0, da1, da2):
        @pl.when(pl.program_id(0) == 0)
        def _():
            db_ref[...] = jnp.zeros_like(db_ref)

        dm = lax.dot_general(dy_ref[...], wo_ref[...], NT, preferred_element_type=F32)
        for i, (y_ref, w_ref, d_ref, da_ref) in enumerate(((y0, w0, d0, da0), (y1, w1, d1, da1), (y2, w2, d2, da2))):
            gi = _sigmoid(z_ref[:, i * D:(i + 1) * D].astype(F32) + b_ref[:, i * D:(i + 1) * D])
            dy = (dm * gi).astype(BF16)
            d_ref[...] = dy
            dz = dm * y_ref[...].astype(F32) * gi * (1.0 - gi)
            dz_ref[:, i * D:(i + 1) * D] = dz.astype(BF16)
            db_ref[0:1, i * D:(i + 1) * D] += jnp.sum(dz, axis=0, keepdims=True)
            da_ref[...] = jnp.dot(dy, w_ref[...], preferred_element_type=F32).astype(BF16)

    row = pl.BlockSpec((tr, D), lambda i: (i, 0))
    wide = pl.BlockSpec((tr, 3 * D), lambda i: (i, 0))
    outs = pl.pallas_call(
        body, name=name, grid=(t // tr,),
        in_specs=[row, pl.BlockSpec(w_out.shape, lambda i: (0, 0))] + [pl.BlockSpec(w.shape, lambda i: (0, 0)) for w in ws]
        + [wide, pl.BlockSpec((1, 3 * D), lambda i: (0, 0)), row, row, row],
        out_specs=[row, row, row, wide, pl.BlockSpec((8, 3 * D), lambda i: (0, 0))]
        + [pl.BlockSpec((tr, w.shape[1]), lambda i: (i, 0)) for w in ws],
        out_shape=[jax.ShapeDtypeStruct((t, D), BF16)] * 3
        + [jax.ShapeDtypeStruct((t, 3 * D), BF16), jax.ShapeDtypeStruct((8, 3 * D), F32)]
        + [jax.ShapeDtypeStruct((t, w.shape[1]), BF16) for w in ws],
        compiler_params=_cp(1),
    )(dymix, w_out, *ws, zg, _row(bg), *ys)
    return tuple(outs[:3]), outs[3], outs[4][0], tuple(outs[5:])


def _adamw(w, g, m, v, name):
    shape = w.shape
    w, g, m, v = (a.reshape(-1, shape[-1]) for a in (w, g, m, v))
    r, d = w.shape
    tr = next((c for c in (512, 352, 256, 128, 64, 32, 16, 8) if r % c == 0), r)

    def body(w_ref, g_ref, m_ref, v_ref, d_ref, nm_ref, nv_ref):
        gg = g_ref[...]
        m2 = ADAM_B1 * m_ref[...] + (1.0 - ADAM_B1) * gg
        v2 = ADAM_B2 * v_ref[...] + (1.0 - ADAM_B2) * (gg * gg)
        m_hat = m2 / (1.0 - ADAM_B1 ** ADAM_STEP)
        v_hat = v2 / (1.0 - ADAM_B2 ** ADAM_STEP)
        d_ref[...] = -ADAM_LR * (m_hat / (jnp.sqrt(v_hat) + ADAM_EPS) + ADAM_WD * w_ref[...])
        nm_ref[...] = m2
        nv_ref[...] = v2

    row = pl.BlockSpec((tr, d), lambda i: (i, 0))
    outs = pl.pallas_call(
        body, name=name, grid=(r // tr,), in_specs=[row] * 4, out_specs=[row] * 3,
        out_shape=[jax.ShapeDtypeStruct((r, d), F32)] * 3, compiler_params=_cp(1),
    )(w, g, m, v)
    return [o.reshape(shape) for o in outs]


def _slab_tile(rows):
    return next((c for c in range(min(rows, 512), 15, -16) if rows % c == 0), rows)


def _sum_slots(parts, out_dtype, name):
    n, r, d = parts.shape
    tr = _slab_tile(r)

    def body(p_ref, o_ref):
        acc = p_ref[0].astype(F32)
        for s in range(1, n):
            acc = acc + p_ref[s].astype(F32)
        o_ref[...] = acc.astype(o_ref.dtype)

    return pl.pallas_call(
        body, name=name, grid=(r // tr,), in_specs=[pl.BlockSpec((n, tr, d), lambda i: (0, i, 0))],
        out_specs=pl.BlockSpec((tr, d), lambda i: (i, 0)),
        out_shape=jax.ShapeDtypeStruct((r, d), out_dtype), compiler_params=_cp(1),
    )(parts)


def _place():
    return lax.axis_index("x"), lax.axis_index("y"), lax.axis_index("c")


def _all_gather(shard, name):
    r, d = shard.shape

    def body(x_ref, out_ref, send_sems, recv_sems, local_sem):
        x, y, c = _place()
        me, sibling = (x, y, c), (x, y, 1 - c)
        chips = [(1 - x, y), (x, 1 - y), (1 - x, 1 - y)]

        def slot(px, py, pc):
            return out_ref.at[4 * px + 2 * py + pc]

        def copy(k, block, to, src=None):
            return pltpu.make_async_remote_copy(
                src_ref=slot(*block) if src is None else src, dst_ref=slot(*block),
                send_sem=send_sems.at[k], recv_sem=recv_sems.at[k], device_id=to, device_id_type=MESH)

        mine = pltpu.make_async_copy(x_ref, slot(*me), local_sem)
        mine.start()
        first = [copy(0, me, sibling, src=x_ref)]
        first += [copy(1 + j, me, (*chip, c), src=x_ref) for j, chip in enumerate(chips)]
        for cp in first:
            cp.start()
        passed = [copy(4 + j, (*chip, c), sibling) for j, chip in enumerate(chips)]
        for j, chip in enumerate(chips):
            copy(1 + j, (*chip, c), me).wait_recv()
            passed[j].start()
        copy(0, sibling, me).wait_recv()
        for j, chip in enumerate(chips):
            copy(4 + j, (*chip, 1 - c), me).wait_recv()
        for cp in first + passed:
            cp.wait_send()
        mine.wait()

    return pl.pallas_call(
        body, name=name, out_shape=jax.ShapeDtypeStruct((N_DEV, r, d), shard.dtype),
        in_specs=[pl.BlockSpec(memory_space=pl.ANY)], out_specs=pl.BlockSpec(memory_space=pl.ANY),
        scratch_shapes=[pltpu.SemaphoreType.DMA((7,)), pltpu.SemaphoreType.DMA((7,)), pltpu.SemaphoreType.DMA],
    )(shard)


HBM_SPEC = pl.BlockSpec(memory_space=pltpu.HBM)
SEM_SPEC = pl.BlockSpec(memory_space=pltpu.SEMAPHORE)
SPLIT_PARAMS = pltpu.CompilerParams(has_side_effects=pltpu.SideEffectType.DATAFLOW_SIDE_EFFECTING)


def _peers():
    x, y, c = _place()
    flip = lambda v, bit: 1 - v if bit else v
    return 4 * x + 2 * y + c, [((flip(x, k >> 2 & 1), flip(y, k >> 1 & 1), flip(c, k & 1))) for k in range(1, N_DEV)]


def _exchange_start(src, all_gather, name):
    r, d = src.shape[-2:]

    def body(src_ref, land_ref, send_sems, recv_sems, src_thru, land_thru, token):
        me, peers = _peers()
        for k, (px, py, pc) in enumerate(peers):
            part = src_ref if all_gather else src_ref.at[4 * px + 2 * py + pc]
            pltpu.make_async_remote_copy(src_ref=part, dst_ref=land_ref.at[me], send_sem=send_sems.at[k],
                                         recv_sem=recv_sems.at[k], device_id=(px, py, pc), device_id_type=MESH).start()
        token[...] = jnp.zeros_like(token)

    land = lax.empty((N_DEV, r, d), src.dtype)
    return pl.pallas_call(
        body, name=name,
        out_shape=(pltpu.SemaphoreType.DMA((N_DEV - 1,)), pltpu.SemaphoreType.DMA((N_DEV - 1,)), pltpu.HBM(src.shape, src.dtype),
                   pltpu.HBM(land.shape, land.dtype), jax.ShapeDtypeStruct((8, LANES), F32)),
        in_specs=(HBM_SPEC, HBM_SPEC), out_specs=(SEM_SPEC, SEM_SPEC, HBM_SPEC, HBM_SPEC, pl.BlockSpec(memory_space=pltpu.VMEM)),
        input_output_aliases={0: 2, 1: 3}, compiler_params=SPLIT_PARAMS,
    )(pltpu.with_memory_space_constraint(src, pltpu.HBM), pltpu.with_memory_space_constraint(land, pltpu.HBM))


def _exchange_wait(started, after, all_gather, name):
    send_sems, recv_sems, src_thru, land_thru, _ = started

    def body(src_ref, land_ref, send_sems, recv_sems, after_ref, src_out, land_out):
        me, peers = _peers()
        for k, (px, py, pc) in enumerate(peers):
            part = src_ref if all_gather else src_ref.at[4 * px + 2 * py + pc]
            cp = pltpu.make_async_remote_copy(src_ref=part, dst_ref=land_ref.at[4 * px + 2 * py + pc], send_sem=send_sems.at[k],
                                              recv_sem=recv_sems.at[k], device_id=(px, py, pc), device_id_type=MESH)
            cp.wait_send()
            cp.wait_recv()

    return pl.pallas_call(
        body, name=name,
        out_shape=(pltpu.HBM(src_thru.shape, src_thru.dtype), pltpu.HBM(land_thru.shape, land_thru.dtype)),
        in_specs=(HBM_SPEC, HBM_SPEC, SEM_SPEC, SEM_SPEC, pl.BlockSpec(memory_space=pl.ANY)), out_specs=(HBM_SPEC, HBM_SPEC),
        input_output_aliases={0: 0, 1: 1}, compiler_params=SPLIT_PARAMS,
    )(src_thru, land_thru, send_sems, recv_sems, after)


def _own_slot(land, mine):
    x, y, c = _place()
    return lax.dynamic_update_slice(land, mine[None], (4 * x + 2 * y + c, 0, 0))


BIG = (("w_in", (D, 864), 1), ("w_conv_out", (CONV_W, 128), 1), ("w_att_out", (ATT_GW, 128), 1), ("w_mem_kv", (128, D), 0),
       ("w_mem_out", (MEM_W, 128), 1), ("w_out", (128, D), 0), ("w_ffn_in", (D, 704), 1), ("w_ffn_out", (352, D), 0),
       ("conv_dw", (CONV_K, 64), 1))
SMALL = (("rel_bias", (NUM_BUCKETS, 12)), ("norm_mix_pre", (DEPTH, D)), ("b_gate", (DEPTH, 3 * D)),
         ("conv_dw_bias", (DEPTH, CONV_W)), ("conv_ln_g", (DEPTH, CONV_W)), ("conv_ln_b", (DEPTH, CONV_W)),
         ("norm_mem", (DEPTH, D)), ("norm_mix_post", (DEPTH, D)), ("norm_ffn_pre", (DEPTH, D)), ("norm_ffn_post", (DEPTH, D)))
TWIN_WEIGHTS = ("rel_bias", "norm_mix_pre", "w_in", "b_gate", "conv_dw", "conv_dw_bias", "conv_ln_g", "conv_ln_b", "w_conv_out",
                "w_att_out", "norm_mem", "w_mem_kv", "w_mem_out", "w_out", "norm_mix_post", "norm_ffn_pre", "w_ffn_in",
                "w_ffn_out", "norm_ffn_post")


BIG_INFO = {n: (s, a) for n, s, a in BIG}
GROUPS = {"mix": ("w_in", "w_conv_out", "w_att_out", "w_mem_kv", "w_mem_out", "w_out", "conv_dw"), "ffn": ("w_ffn_in", "w_ffn_out")}
TILE_BYTES_PER_LANE = 32


def _rows_of(shape, dtype):
    tile = TILE_BYTES_PER_LANE // jnp.dtype(dtype).itemsize
    return -(-int(np.prod(shape)) // (D * tile)) * tile


def _row_form(shape, axis):
    return tuple(shape) if axis == 0 else (shape[1], shape[0])


def _as_slab_rows(p, lead=()):
    shape = p.shape[len(lead):]
    rows = _rows_of(shape, p.dtype)
    if shape == (rows, D):
        return p
    n = int(np.prod(shape))
    p = jnp.pad(p.reshape(*lead, n), [(0, 0)] * len(lead) + [(0, rows * D - n)])
    return p.reshape(*lead, rows, D)


def _from_slab_rows(rows, shape, lead=()):
    if rows.shape[len(lead):] == tuple(shape):
        return rows
    n = int(np.prod(shape))
    return rows.reshape(*lead, -1)[..., :n].reshape(*lead, *shape)


def _pack(pieces, dtype):
    return jnp.concatenate([_as_slab_rows(p.astype(dtype)) for p in pieces], axis=0)


def _unpack(slab, shapes, packed_as):
    out, r0 = [], 0
    for s in shapes:
        rows = _rows_of(s, packed_as)
        out.append(_from_slab_rows(slab[r0:r0 + rows], s))
        r0 += rows
    return out


def _pack_group(blocks, layer, group):
    return _pack([blocks[n][layer] if BIG_INFO[n][1] == 0 else blocks[n][layer].T for n in GROUPS[group]], BF16)


def _full_weights(gathered, group):
    out, r0 = {}, 0
    for name in GROUPS[group]:
        r, k = _row_form(*BIG_INFO[name])
        rows = _rows_of((r, k), BF16)
        out[name] = _from_slab_rows(gathered[:, r0:r0 + rows], (r, k), lead=(N_DEV,)).reshape(N_DEV * r, k)
        r0 += rows
    return out


def _scatter_layout(grads, group):
    parts = []
    for name in GROUPS[group]:
        r, k = _row_form(*BIG_INFO[name])
        parts.append(_as_slab_rows(grads[name].astype(BF16).reshape(N_DEV, r, k), lead=(N_DEV,)))
    return jnp.concatenate(parts, axis=1)


def _unpack_blocks(slab, group):
    out = _unpack(slab, [_row_form(*BIG_INFO[n]) for n in GROUPS[group]], BF16)
    return {n: b if BIG_INFO[n][1] == 0 else b.T for n, b in zip(GROUPS[group], out)}


def _split_w_in(w_in):
    edges = np.cumsum((0,) + SEG)
    return [w_in[edges[i]:edges[i + 1]] for i in range(4)]


def _conv_taps(conv_dw):
    return jnp.pad(conv_dw.astype(F32).T, ((0, 32 - CONV_K), (0, 0)))


def _layer_fwd(l, x, h, nm, get_w, sm, bias_tabs):
    t = x.shape[0]
    wts = get_w(l, "mix", h)
    w_seg = _split_w_in(wts["w_in"])
    zc = _mm(h, w_seg[0], "fwd_in_conv", tb=True)
    za = _mm(h, w_seg[1], "fwd_in_att", tb=True, tn=1152)
    zq = _mm(h, w_seg[2], "fwd_in_memq", tb=True)
    zg = _mm(h, w_seg[3], "fwd_in_gate", tb=True)
    y, cact = _conv_fwd(zc, _conv_taps(wts["conv_dw"]), sm["conv_dw_bias"][l], sm["conv_ln_g"][l], sm["conv_ln_b"][l],
                        "conv_fwd")
    qkv, outs, lses = [], [], []
    for g, (_, dil) in enumerate(ATT_PATTERNS):
        cols = tuple(i * 3 + g for i in range(3))
        if dil == 1:
            trio, at = (za, za, za), cols
        else:
            trio, at = _to_residue_major([(za, c) for c in cols], dil, f"qkv_residue_d{dil}"), (0, 0, 0)
        o, ls = _att_fwd(*trio, bias_tabs[g], t // dil, f"att_fwd_d{dil}", cols=at)
        qkv.append((trio, at))
        outs.append(o)
        lses.append(ls)
    att, lse = _att_combine(outs, lses, "att_combine")
    kv = _mm(nm, wts["w_mem_kv"], "fwd_mem_kv")
    om = _mem_fwd(zq, kv, "mem_fwd")
    w_br = (wts["w_conv_out"], wts["w_att_out"], wts["w_mem_out"])
    ys, merged = _branches_fwd((cact, att, om), w_br, zg, sm["b_gate"][l], "branches_fwd")
    ymix, x1, h2 = _mm_norm_res(merged, wts["w_out"], x, sm["norm_mix_post"][l], sm["norm_ffn_pre"][l], "fwd_out_norm")
    wff = get_w(l, "ffn", h2)
    gff, uff, act = _mm_swiglu_fwd(h2, wff["w_ffn_in"], "fwd_ffn_in_swiglu")
    g_next = sm["norm_mix_pre"][l + 1] if l + 1 < DEPTH else None
    f, x2, h_next = _mm_norm_res(act, wff["w_ffn_out"], x1, sm["norm_ffn_post"][l], g_next,
                                 "fwd_ffn_out_norm" if g_next is not None else "fwd_ffn_out_last")
    saved = dict(x=x, h=h, zc=zc, zq=zq, zg=zg, y=y, cact=cact, qkv=qkv, att=att, lse=lse, kv=kv, om=om, ys=ys,
                 merged=merged, ymix=ymix, x1=x1, h2=h2, gff=gff, uff=uff, act=act, f=f)
    return x2, h_next, saved


def _after(value, *tokens):
    for t in tokens:
        if t is not None:
            value = value + t[0, 0].astype(value.dtype)
    return value


def _layer_bwd(l, dx2, s, nm, mem, get_w, put_g, sm, bias_tabs, bucket_idx):
    t = dx2.shape[0]
    wts, wff = get_w(l, "mix", None), get_w(l, "ffn", None)
    gb, gf, gs = {}, {}, {}
    df, gs["norm_ffn_post"] = _norm_bwd(s["f"], sm["norm_ffn_post"][l], dx2, None, BF16, "bwd_norm_ffn_post")
    dg, du = _mm_swiglu_bwd(df, wff["w_ffn_out"], s["gff"], s["uff"], "bwd_ffn_out_d_swiglu")
    gf["w_ffn_out"] = _mm(s["act"], df, "bwd_ffn_out_w", ta=True, tm=1408)
    gf["w_ffn_in"] = jnp.concatenate([_mm(dg, s["h2"], "bwd_ffn_in_w_gate", ta=True, tm=1408),
                                      _mm(du, s["h2"], "bwd_ffn_in_w_up", ta=True, tm=1408)], axis=0)
    tok = put_g(l, "ffn", gf)
    dh2 = _mm_sum([(dg, _after(wff["w_ffn_in"][:FFN_H], tok), FFN_H // 2), (du, wff["w_ffn_in"][FFN_H:], FFN_H // 2)],
                  "bwd_ffn_in_d")
    dx1, gs["norm_ffn_pre"] = _norm_bwd(s["x1"], sm["norm_ffn_pre"][l], dh2, dx2, F32, "bwd_norm_ffn_pre")
    dymix, gs["norm_mix_post"] = _norm_bwd(s["ymix"], sm["norm_mix_post"][l], dx1, None, BF16, "bwd_norm_mix_post")
    gb["w_out"] = _mm(s["merged"], dymix, "bwd_out_w", ta=True)
    w_br = (wts["w_conv_out"], wts["w_att_out"], wts["w_mem_out"])
    dys, dzg, gs["b_gate"], (dcact, datt, dom) = _branches_bwd(dymix, wts["w_out"], w_br, s["zg"], sm["b_gate"][l], s["ys"],
                                                               "branches_bwd")
    gb["w_conv_out"] = _mm(dys[0], s["cact"], "bwd_conv_out_w", ta=True)
    dzc, dconv, gs["conv_dw_bias"], gs["conv_ln_g"], gs["conv_ln_b"] = _conv_bwd(
        s["zc"], s["y"], dcact, _conv_taps(wts["conv_dw"]), sm["conv_ln_g"][l], sm["conv_ln_b"][l], "conv_bwd")
    gb["conv_dw"] = dconv[:CONV_K].T
    gb["w_att_out"] = _mm(dys[1], s["att"], "bwd_att_out_w", ta=True)
    dqkv, rel = [], []
    for g, (_, dil) in enumerate(ATT_PATTERNS):
        trio, at = s["qkv"][g]
        if dil == 1:
            o_r, do_r, lse_r = s["att"], datt, s["lse"]
        else:
            o_r, do_r, lse_r = _to_residue_major([(s["att"], 0), (datt, 0), (s["lse"], 0)], dil, f"att_bwd_residue_d{dil}")
        dq, dk, dv, dbias = _att_bwd(*trio, bias_tabs[g], o_r, do_r, lse_r, t // dil, f"att_bwd_d{dil}", cols=at)
        dqkv.append((dq, dk, dv))
        rel.append(_bias_grad(dbias, bucket_idx[g], f"bias_grad_d{dil}")[:HEADS, :NUM_BUCKETS])
    dza = _assemble_dza(dqkv, "att_bwd_assemble")
    gs["rel_bias"] = jnp.concatenate(rel, axis=0).T
    gb["w_mem_out"] = _mm(dys[2], s["om"], "bwd_mem_out_w", ta=True)
    dzq, dkv = _mem_bwd(s["zq"], s["kv"], dom, "mem_bwd")
    dkv = dkv.astype(BF16)
    gb["w_mem_kv"] = _mm(nm, dkv, "bwd_mem_kv_w", ta=True)
    dnm = _mm(dkv, wts["w_mem_kv"], "bwd_mem_kv_d", tb=True, out_dtype=F32)
    _, gs["norm_mem"] = _norm_bwd(mem, sm["norm_mem"][l], dnm, None, BF16, "bwd_norm_mem")
    segs = ((dzc, "conv", 1024), (dza, "att", 1152), (dzq, "memq", 512), (dzg, "gate", 1024))
    gb["w_in"] = jnp.concatenate([_mm(dz, s["h"], f"bwd_in_{nm_}_w", ta=True, tm=min(blk, 1152), tk=2048)
                                  for dz, nm_, blk in segs], axis=0)
    tok = put_g(l, "mix", gb)
    w_seg = _split_w_in(wts["w_in"])
    w_seg[0] = _after(w_seg[0], tok)
    dx, gs["norm_mix_pre"] = _mm_sum([(dz, w_seg[i], blk) for i, (dz, _, blk) in enumerate(segs)], "bwd_in_d_norm",
                                     norm=(s["x"], sm["norm_mix_pre"][l], dx1))
    return dx, gs


def _local_step(x, mem, target, sm, get_w, put_g, tokens=()):
    t = x.shape[0]
    bias_tabs, bucket_idx = [], []
    for g, (_, dil) in enumerate(ATT_PATTERNS):
        idx = _bucket_index(_att_tiles(t // dil)[1], dil)
        bucket_idx.append(jnp.asarray(idx))
        bias_tabs.append(_bias_table(sm["rel_bias"][:, g * HEADS:(g + 1) * HEADS], bucket_idx[g], f"bias_table_d{dil}"))
    h = _norm_plain(x, _after(sm["norm_mix_pre"][0], *tokens), "norm_first")
    saved, nms = [], []
    for l in range(DEPTH):
        nm = _norm_plain(mem, sm["norm_mem"][l], "norm_mem")
        x, h, s = _layer_fwd(l, x, h, nm, get_w, sm, bias_tabs)
        saved.append(s)
        nms.append(nm)
    loss, dx = _loss_head(x, target, "loss_head")
    gsmall = {}
    for l in reversed(range(DEPTH)):
        dx, gs = _layer_bwd(l, dx, saved[l], nms[l], mem, get_w, put_g, sm, bias_tabs, bucket_idx)
        for n, v in gs.items():
            gsmall.setdefault(n, {})[l] = v
    small = {}
    for n, _ in SMALL:
        small[n] = gsmall[n][0] + gsmall[n][1] if n == "rel_bias" else jnp.stack([gsmall[n][0], gsmall[n][1]])
    return loss, dx, small


def kernel(x, mem, rel_bias, norm_mix_pre, w_in, b_gate, conv_dw, conv_dw_bias, conv_ln_g, conv_ln_b, w_conv_out, w_att_out, norm_mem, w_mem_kv, w_mem_out, w_out, norm_mix_post, norm_ffn_pre, w_ffn_in, w_ffn_out, norm_ffn_post, loss_target, m_rel_bias, m_norm_mix_pre, m_w_in, m_b_gate, m_conv_dw, m_conv_dw_bias, m_conv_ln_g, m_conv_ln_b, m_w_conv_out, m_w_att_out, m_norm_mem, m_w_mem_kv, m_w_mem_out, m_w_out, m_norm_mix_post, m_norm_ffn_pre, m_w_ffn_in, m_w_ffn_out, m_norm_ffn_post, v_rel_bias, v_norm_mix_pre, v_w_in, v_b_gate, v_conv_dw, v_conv_dw_bias, v_conv_ln_g, v_conv_ln_b, v_w_conv_out, v_w_att_out, v_norm_mem, v_w_mem_kv, v_w_mem_out, v_w_out, v_norm_mix_post, v_norm_ffn_pre, v_w_ffn_in, v_w_ffn_out, v_norm_ffn_post):
    args = dict(locals())
    w = {n: args[n] for n in TWIN_WEIGHTS}
    m = {n: args["m_" + n] for n in TWIN_WEIGHTS}
    v = {n: args["v_" + n] for n in TWIN_WEIGHTS}
    sm = {n: w[n] for n, _ in SMALL}

    first = (0, "mix")
    keys = [(l, grp) for l in range(DEPTH) for grp in GROUPS]
    shard = {(l, grp): _pack_group(w, l, grp) for l, grp in keys}
    weights = {first: _full_weights(_all_gather(shard[first], "gather_l0_mix"), "mix")}
    settled = weights[first]["w_out"][:1, :1] * 0
    ag = {k: _exchange_start(_after(shard[k], settled), True, f"ag_start_l{k[0]}_{k[1]}") for k in keys if k != first}

    def get_w(l, grp, after):
        if (l, grp) not in weights:
            mine, land = _exchange_wait(ag[l, grp], after, True, f"ag_wait_l{l}_{grp}")
            weights[l, grp] = _full_weights(_own_slot(land, mine), grp)
        return weights[l, grp]

    rs = {}

    def put_g(l, grp, grads):
        rs[l, grp] = _exchange_start(_scatter_layout(grads, grp), False, f"rs_start_l{l}_{grp}")
        return rs[l, grp][4]

    loss, dx, gsmall = _local_step(x[0], mem[0], loss_target[0], sm, get_w, put_g, [st[4] for st in ag.values()])
    loss = lax.psum(loss, ("x", "y", "c"))

    xi, yi, ci = _place()
    me = 4 * xi + 2 * yi + ci
    g_slab = {}
    for k, st in rs.items():
        contrib, land = _exchange_wait(st, dx, False, f"rs_wait_l{k[0]}_{k[1]}")
        own = lax.dynamic_index_in_dim(contrib, me, axis=0, keepdims=False)
        g_slab[k] = _sum_slots(_own_slot(land, own), F32, f"rs_sum_l{k[0]}_{k[1]}")
    g_layers = [{**_unpack_blocks(g_slab[l, "mix"], "mix"), **_unpack_blocks(g_slab[l, "ffn"], "ffn")} for l in range(DEPTH)]
    small_shapes = [s for _, s in SMALL]
    g_small = _unpack(_sum_slots(_all_gather(_pack([gsmall[n] for n, _ in SMALL], F32), "gather_small"), F32, "sum_small"),
                      small_shapes, F32)
    grads = {n: jnp.stack([g_layers[0][n], g_layers[1][n]]) for n, _, _ in BIG}
    grads.update({n: g_small[i] for i, (n, _) in enumerate(SMALL)})

    delta, new_m, new_v = {}, {}, {}
    for n in TWIN_WEIGHTS:
        delta[n], new_m[n], new_v[n] = _adamw(w[n], grads[n], m[n], v[n], f"adamw_{n}")

    return (loss, dx[None], *[grads[n] for n in TWIN_WEIGHTS], *[delta[n] for n in TWIN_WEIGHTS],
            *[new_m[n] for n in TWIN_WEIGHTS], *[new_v[n] for n in TWIN_WEIGHTS])
```

```python
import functools

import numpy as np
import jax
import jax.numpy as jnp
from jax import lax
from jax.experimental import pallas as pl
from jax.experimental.pallas import tpu as pltpu

F32 = jnp.float32
BF16 = jnp.bfloat16

N_DEV = 8
D = 1024
DEPTH = 2
CONV_W = 512
CONV_K = 31
CONV_PAD = 16
ATT_PATTERNS = ((128, 1), (512, 4), (2048, 16))
ATT_RADIUS = 64
HEADS = 4
HEAD_DIM = 64
ATT_GW = HEADS * HEAD_DIM
ATT_W = 3 * ATT_GW
MEM_HEADS = 4
MEM_HD = 128
MEM_W = 512
N_MEM = 256
FFN_H = 2816
NUM_BUCKETS = 32
MAX_DISTANCE = 1024
RMS_EPS = 1e-6
LN_EPS = 1e-5
NEG_INF = -1e30
SEG = (2 * CONV_W, 3 * ATT_W, MEM_W, 3 * D)
ADAM_LR, ADAM_B1, ADAM_B2, ADAM_EPS, ADAM_WD, ADAM_STEP = 0.001, 0.9, 0.999, 1e-08, 0.01, 10

VMEM_LIMIT_V7X = 56 * 1024 * 1024
MESH = pl.DeviceIdType.MESH


def _cp(n_axes):
    return pltpu.CompilerParams(dimension_semantics=("arbitrary",) * n_axes, vmem_limit_bytes=VMEM_LIMIT_V7X)


def _row(v):
    return v.reshape(1, -1)


def _mm(a, b, name, *, ta=False, tb=False, out_dtype=BF16, tm=2048, tn=1024, tk=2048):
    kdim, m = a.shape if ta else a.shape[::-1]
    n, kb = b.shape if tb else b.shape[::-1]
    assert kb == kdim and a.dtype == BF16 and b.dtype == BF16
    tm, tn, tk = min(tm, m), min(tn, n), min(tk, kdim)
    assert m % tm == 0 and n % tn == 0 and kdim % tk == 0, (name, m, n, kdim, tm, tn, tk)
    nk = kdim // tk
    a_spec = pl.BlockSpec((tk, tm), lambda i, j, k: (k, i)) if ta else pl.BlockSpec((tm, tk), lambda i, j, k: (i, k))
    b_spec = pl.BlockSpec((tn, tk), lambda i, j, k: (j, k)) if tb else pl.BlockSpec((tk, tn), lambda i, j, k: (k, j))
    o_spec = pl.BlockSpec((tm, tn), lambda i, j, k: (i, j))
    dims = (((0 if ta else 1,), (1 if tb else 0,)), ((), ()))
    use_scratch = nk > 1 and out_dtype != F32

    def body(*refs):
        a_ref, b_ref, o_ref = refs[:3]
        p = lax.dot_general(a_ref[...], b_ref[...], dims, preferred_element_type=F32)
        if nk == 1:
            o_ref[...] = p.astype(o_ref.dtype)
            return
        k = pl.program_id(2)
        t_ref = refs[3] if use_scratch else o_ref

        @pl.when(k == 0)
        def _():
            t_ref[...] = p

        @pl.when(k > 0)
        def _():
            t_ref[...] += p

        if use_scratch:
            @pl.when(k == nk - 1)
            def _():
                o_ref[...] = t_ref[...].astype(o_ref.dtype)

    return pl.pallas_call(
        body, name=name, grid=(m // tm, n // tn, nk), in_specs=[a_spec, b_spec], out_specs=o_spec,
        out_shape=jax.ShapeDtypeStruct((m, n), out_dtype),
        scratch_shapes=[pltpu.VMEM((tm, tn), F32)] if use_scratch else [],
        compiler_params=_cp(3),
    )(a, b)


EPILOGUE_COLS = 512
EPILOGUE_ROWS = 512


def _chunks(size, step):
    return [(c, min(c + step, size)) for c in range(0, size, step)]


def _rms_bwd_val(v, g, dy):
    r = lax.rsqrt(jnp.mean(v * v, axis=-1, keepdims=True) + RMS_EPS)
    vhat = v * r
    dvh = dy * g
    dv = r * (dvh - vhat * jnp.mean(dvh * vhat, axis=-1, keepdims=True))
    return dv, jnp.sum(dy * vhat, axis=0, keepdims=True)


def _mm_sum(pairs, name, *, norm=None, tm=1024):
    m, n = pairs[0][0].shape[0], pairs[0][1].shape[1]
    tm = min(tm, m)
    starts, specs, total = [], [], 0
    for a, b, tk in pairs:
        assert a.shape == (m, b.shape[0]) and b.shape[1] == n and a.shape[1] % tk == 0 and a.dtype == b.dtype == BF16
        nk = a.shape[1] // tk
        starts.append((total, nk))
        step = functools.partial(lambda k, s0, nk_: jnp.clip(k - s0, 0, nk_ - 1), s0=total, nk_=nk)
        specs.append(pl.BlockSpec((tm, tk), functools.partial(lambda i, k, st: (i, st(k)), st=step)))
        specs.append(pl.BlockSpec((tk, n), functools.partial(lambda i, k, st: (st(k), 0), st=step)))
        total += nk
    np_ = 2 * len(pairs)
    row = pl.BlockSpec((tm, n), lambda i, k: (i, 0))

    def accumulate(refs, acc, k, upto=total):
        for p, (s0, nk) in enumerate(starts):
            @pl.when((k >= s0) & (k < min(s0 + nk, upto)))
            def _(p=p, s0=s0):
                part = jnp.dot(refs[2 * p][...], refs[2 * p + 1][...], preferred_element_type=F32)
                if s0 == 0:
                    @pl.when(k == 0)
                    def _():
                        acc[...] = part

                    @pl.when(k > 0)
                    def _():
                        acc[...] += part
                else:
                    acc[...] += part

    operands = [op for a, b, _ in pairs for op in (a, b)]
    if norm is None:
        def plain(*refs):
            accumulate(refs, refs[np_], pl.program_id(1))

        return pl.pallas_call(
            plain, name=name, grid=(m // tm, total), in_specs=specs, out_specs=row,
            out_shape=jax.ShapeDtypeStruct((m, n), F32), compiler_params=_cp(2),
        )(*operands)

    v, g, dres = norm

    def body(*refs):
        v_ref, g_ref, dres_ref, dv_ref, dg_ref, acc = refs[np_:np_ + 6]
        i, k = pl.program_id(0), pl.program_id(1)
        accumulate(refs, acc, k, upto=total - 1)

        @pl.when((i == 0) & (k == 0))
        def _():
            dg_ref[...] = jnp.zeros_like(dg_ref)

        @pl.when(k == total - 1)
        def _():
            a_ref, b_ref = refs[np_ - 2], refs[np_ - 1]
            for r0, r1 in _chunks(tm, EPILOGUE_ROWS):
                dy = jnp.dot(a_ref[r0:r1, :], b_ref[...], preferred_element_type=F32)
                if total > 1:
                    dy = dy + acc[r0:r1, :]
                dv, dg = _rms_bwd_val(v_ref[r0:r1, :], g_ref[...], dy)
                dv_ref[r0:r1, :] = dv + dres_ref[r0:r1, :]
                dg_ref[0:1, :] += dg

    once = pl.BlockSpec((tm, n), lambda i, k: (i, 0), pipeline_mode=pl.Buffered(1))
    dv, dg = pl.pallas_call(
        body, name=name, grid=(m // tm, total),
        in_specs=specs + [once, pl.BlockSpec((1, n), lambda i, k: (0, 0)), once],
        out_specs=[row, pl.BlockSpec((8, n), lambda i, k: (0, 0))],
        out_shape=[jax.ShapeDtypeStruct((m, n), F32), jax.ShapeDtypeStruct((8, n), F32)],
        scratch_shapes=[pltpu.VMEM((tm, n), F32)], compiler_params=_cp(2),
    )(*operands, v, _row(g), dres)
    return dv, dg[0]


def _mm_swiglu_fwd(h, w, name, *, tm=1024, tn=1408):
    m, kdim = h.shape
    hid = w.shape[0] // 2
    tm = min(tm, m)
    nj = hid // tn
    dims = (((1,), (1,)), ((), ()))

    def body(h_ref, wg_ref, wu_ref, g_ref, u_ref, a_ref):
        hv = h_ref[...]
        for c0, c1 in _chunks(tn, EPILOGUE_COLS):
            g = lax.dot_general(hv, wg_ref[c0:c1, :], dims, preferred_element_type=F32)
            u = lax.dot_general(hv, wu_ref[c0:c1, :], dims, preferred_element_type=F32)
            g_ref[:, c0:c1] = g.astype(BF16)
            u_ref[:, c0:c1] = u.astype(BF16)
            a_ref[:, c0:c1] = (g * _sigmoid(g) * u).astype(BF16)

    out = pl.BlockSpec((tm, tn), lambda i, j: (i, j))
    return pl.pallas_call(
        body, name=name, grid=(m // tm, nj),
        in_specs=[pl.BlockSpec((tm, kdim), lambda i, j: (i, 0)), pl.BlockSpec((tn, kdim), lambda i, j: (j, 0)),
                  pl.BlockSpec((tn, kdim), lambda i, j: (j + nj, 0))],
        out_specs=[out, out, out], out_shape=[jax.ShapeDtypeStruct((m, hid), BF16)] * 3, compiler_params=_cp(2),
    )(h, w, w)


def _mm_swiglu_bwd(df, w, g, u, name, *, tm=1024, tn=1408):
    m, kdim = df.shape
    hid = w.shape[0]
    tm = min(tm, m)

    def body(df_ref, w_ref, g_ref, u_ref, dg_ref, du_ref):
        dfv = df_ref[...]
        for c0, c1 in _chunks(tn, EPILOGUE_COLS):
            da = lax.dot_general(dfv, w_ref[c0:c1, :], (((1,), (1,)), ((), ())), preferred_element_type=F32)
            gg, uu = g_ref[:, c0:c1].astype(F32), u_ref[:, c0:c1].astype(F32)
            sg = _sigmoid(gg)
            dg_ref[:, c0:c1] = (da * uu * (sg * (1.0 + gg * (1.0 - sg)))).astype(BF16)
            du_ref[:, c0:c1] = (da * gg * sg).astype(BF16)

    blk = pl.BlockSpec((tm, tn), lambda i, j: (i, j))
    return pl.pallas_call(
        body, name=name, grid=(m // tm, hid // tn),
        in_specs=[pl.BlockSpec((tm, kdim), lambda i, j: (i, 0)), pl.BlockSpec((tn, kdim), lambda i, j: (j, 0)), blk, blk],
        out_specs=[blk, blk], out_shape=[jax.ShapeDtypeStruct((m, hid), BF16)] * 2, compiler_params=_cp(2),
    )(df, w, g, u)


def _mm_norm_res(a, b, xres, g, g_next, name, *, tm=1024):
    m, kdim = a.shape
    n = b.shape[1]
    tm = min(tm, m)
    two = g_next is not None

    def body(*refs):
        a_ref, b_ref, x_ref, g_ref = refs[:4]
        y_ref, xn_ref = refs[4 + two], refs[5 + two]
        for r0, r1 in _chunks(tm, EPILOGUE_ROWS):
            y = jnp.dot(a_ref[r0:r1, :], b_ref[...], preferred_element_type=F32)
            y_ref[r0:r1, :] = y.astype(BF16)
            xn = x_ref[r0:r1, :] + _rms_val(y, g_ref[...])
            xn_ref[r0:r1, :] = xn
            if two:
                refs[7][r0:r1, :] = _rms_val(xn, refs[4][...]).astype(BF16)

    row = pl.BlockSpec((tm, n), lambda i: (i, 0))
    vec = pl.BlockSpec((1, n), lambda i: (0, 0))
    outs = pl.pallas_call(
        body, name=name, grid=(m // tm,),
        in_specs=[pl.BlockSpec((tm, kdim), lambda i: (i, 0)), pl.BlockSpec((kdim, n), lambda i: (0, 0)), row, vec] + ([vec] if two else []),
        out_specs=[row, row] + ([row] if two else []),
        out_shape=[jax.ShapeDtypeStruct((m, n), BF16), jax.ShapeDtypeStruct((m, n), F32)]
        + ([jax.ShapeDtypeStruct((m, n), BF16)] if two else []),
        compiler_params=_cp(1),
    )(*((a, b, xres, _row(g)) + ((_row(g_next),) if two else ())))
    return (outs[0], outs[1], outs[2]) if two else (outs[0], outs[1], None)


def _rms_val(v, g):
    return v * lax.rsqrt(jnp.mean(v * v, axis=-1, keepdims=True) + RMS_EPS) * g


def _norm_plain(v, g, name, tr=512):
    t, d = v.shape
    tr = min(tr, t)

    def body(v_ref, g_ref, o_ref):
        o_ref[...] = _rms_val(v_ref[...], g_ref[...]).astype(BF16)

    return pl.pallas_call(
        body, name=name, grid=(t // tr,),
        in_specs=[pl.BlockSpec((tr, d), lambda i: (i, 0)), pl.BlockSpec((1, d), lambda i: (0, 0))],
        out_specs=pl.BlockSpec((tr, d), lambda i: (i, 0)),
        out_shape=jax.ShapeDtypeStruct((t, d), BF16), compiler_params=_cp(1),
    )(v, _row(g))


def _norm_bwd(v, g, dout, dres, out_dtype, name, tr=512):
    t, d = v.shape
    tr = min(tr, t)
    nt = t // tr
    has_res = dres is not None

    def body(*refs):
        v_ref, g_ref, do_ref = refs[:3]
        dv_ref, dg_ref = refs[3 + has_res], refs[4 + has_res]
        i = pl.program_id(0)
        vv = v_ref[...].astype(F32)
        dy = do_ref[...].astype(F32)
        r = lax.rsqrt(jnp.mean(vv * vv, axis=-1, keepdims=True) + RMS_EPS)
        vhat = vv * r
        dvh = dy * g_ref[...]
        dv = r * (dvh - vhat * jnp.mean(dvh * vhat, axis=-1, keepdims=True))
        if has_res:
            dv = dv + refs[3][...]
        dv_ref[...] = dv.astype(dv_ref.dtype)
        part = jnp.sum(dy * vhat, axis=0, keepdims=True)

        @pl.when(i == 0)
        def _():
            dg_ref[...] = jnp.zeros_like(dg_ref)

        dg_ref[0:1, :] += part

    row = pl.BlockSpec((tr, d), lambda i: (i, 0))
    dv, dg = pl.pallas_call(
        body, name=name, grid=(nt,),
        in_specs=[row, pl.BlockSpec((1, d), lambda i: (0, 0)), row] + ([row] if has_res else []),
        out_specs=[row, pl.BlockSpec((8, d), lambda i: (0, 0))],
        out_shape=[jax.ShapeDtypeStruct((t, d), out_dtype), jax.ShapeDtypeStruct((8, d), F32)],
        compiler_params=_cp(1),
    )(*((v, _row(g), dout) + ((dres,) if has_res else ())))
    return dv, dg[0]


def _loss_head(y, target, name, tr=512):
    t, d = y.shape
    tr = min(tr, t)
    nt = t // tr

    def body(y_ref, t_ref, dy_ref, l_ref):
        i = pl.program_id(0)
        err = y_ref[...] - t_ref[...]
        dy_ref[...] = err * (1.0 / d)

        @pl.when(i == 0)
        def _():
            l_ref[...] = jnp.zeros_like(l_ref)

        l_ref[...] += jnp.sum(err * err) * (0.5 / d)

    row = pl.BlockSpec((tr, d), lambda i: (i, 0))
    dy, l = pl.pallas_call(
        body, name=name, grid=(nt,), in_specs=[row, row],
        out_specs=[row, pl.BlockSpec((8, 128), lambda i: (0, 0))],
        out_shape=[jax.ShapeDtypeStruct((t, d), F32), jax.ShapeDtypeStruct((8, 128), F32)],
        compiler_params=_cp(1),
    )(y, target)
    return l[0, 0], dy


def _sigmoid(v):
    return 1.0 / (1.0 + jnp.exp(-v))


def _halo_specs(tq, width, nt, halo):
    per = tq // halo
    last = nt * per - 1
    main = pl.BlockSpec((tq, width), lambda i: (i, 0))
    prev = pl.BlockSpec((halo, width), lambda i: (jnp.maximum(i * per - 1, 0), 0))
    nxt = pl.BlockSpec((halo, width), lambda i: (jnp.minimum((i + 1) * per, last), 0))
    return prev, main, nxt


def _glu_window(zp, zm, zn, i, nt):
    def glu(z):
        z = z.astype(F32)
        return z[:, :CONV_W] * _sigmoid(z[:, CONV_W:])

    up = jnp.where(i > 0, glu(zp), 0.0)
    un = jnp.where(i < nt - 1, glu(zn), 0.0)
    return jnp.concatenate([up, glu(zm), un], axis=0)


def _shifted(win, shift, rows):
    if shift == 0:
        return win[:rows]
    return pltpu.roll(win, win.shape[0] - shift, 0)[:rows]


def _conv_fwd(zc, w, b, ln_g, ln_b, name, tq=512):
    t = zc.shape[0]
    tq = min(tq, t)
    nt = t // tq

    def body(zp_ref, zm_ref, zn_ref, w_ref, b_ref, g_ref, bb_ref, y_ref, c_ref):
        i = pl.program_id(0)
        win = _glu_window(zp_ref[...], zm_ref[...], zn_ref[...], i, nt)
        wv = w_ref[...]
        y = jnp.zeros((tq, CONV_W), F32) + b_ref[...]
        for k in range(CONV_K):
            y = y + _shifted(win, k + 1, tq) * wv[k:k + 1, :]
        y_ref[...] = y
        mu = jnp.mean(y, axis=-1, keepdims=True)
        yc = y - mu
        ln = yc * lax.rsqrt(jnp.mean(yc * yc, axis=-1, keepdims=True) + LN_EPS) * g_ref[...] + bb_ref[...]
        c_ref[...] = (ln * _sigmoid(ln)).astype(BF16)

    vec = pl.BlockSpec((1, CONV_W), lambda i: (0, 0))
    out = pl.BlockSpec((tq, CONV_W), lambda i: (i, 0))
    return pl.pallas_call(
        body, name=name, grid=(nt,),
        in_specs=[*_halo_specs(tq, 2 * CONV_W, nt, CONV_PAD), pl.BlockSpec((32, CONV_W), lambda i: (0, 0)), vec, vec, vec],
        out_specs=[out, out],
        out_shape=[jax.ShapeDtypeStruct((t, CONV_W), F32), jax.ShapeDtypeStruct((t, CONV_W), BF16)],
        compiler_params=_cp(1),
    )(zc, zc, zc, w, _row(b), _row(ln_g), _row(ln_b))


def _conv_bwd(zc, y, dc, w, ln_g, ln_b, name, tq=256):
    t = zc.shape[0]
    tq = min(tq, t)
    nt = t // tq
    rows = tq + 2 * CONV_PAD

    def body(zp_ref, zm_ref, zn_ref, yp_ref, ym_ref, yn_ref, dp_ref, dm_ref, dn_ref, w_ref, g_ref, bb_ref,
             dz_ref, dw_ref, db_ref, dg_ref, dbb_ref):
        i = pl.program_id(0)
        uwin = _glu_window(zp_ref[...], zm_ref[...], zn_ref[...], i, nt)
        ywin = jnp.concatenate([yp_ref[...], ym_ref[...], yn_ref[...]], axis=0)
        dcw = jnp.concatenate([dp_ref[...], dm_ref[...], dn_ref[...]], axis=0).astype(F32)
        mu = jnp.mean(ywin, axis=-1, keepdims=True)
        yc = ywin - mu
        rstd = lax.rsqrt(jnp.mean(yc * yc, axis=-1, keepdims=True) + LN_EPS)
        yhat = yc * rstd
        ln = yhat * g_ref[...] + bb_ref[...]
        sg = _sigmoid(ln)
        dl = dcw * (sg * (1.0 + ln * (1.0 - sg)))
        ridx = lax.broadcasted_iota(jnp.int32, (rows, 1), 0)
        inside = ((ridx >= CONV_PAD) | (i > 0)) & ((ridx < tq + CONV_PAD) | (i < nt - 1))
        dl = jnp.where(inside, dl, 0.0)
        dyh = dl * g_ref[...]
        dy = rstd * (dyh - jnp.mean(dyh, axis=-1, keepdims=True) - yhat * jnp.mean(dyh * yhat, axis=-1, keepdims=True))
        dy = jnp.where(inside, dy, 0.0)
        main = slice(CONV_PAD, CONV_PAD + tq)
        dlm, yhm, dym = dl[main], yhat[main], dy[main]

        @pl.when(i == 0)
        def _():
            dw_ref[...] = jnp.zeros_like(dw_ref)
            db_ref[...] = jnp.zeros_like(db_ref)
            dg_ref[...] = jnp.zeros_like(dg_ref)
            dbb_ref[...] = jnp.zeros_like(dbb_ref)

        dg_ref[0:1, :] += jnp.sum(dlm * yhm, axis=0, keepdims=True)
        dbb_ref[0:1, :] += jnp.sum(dlm, axis=0, keepdims=True)
        db_ref[0:1, :] += jnp.sum(dym, axis=0, keepdims=True)
        wv = w_ref[...]
        du = jnp.zeros((tq, CONV_W), F32)
        for k in range(CONV_K):
            du = du + _shifted(dy, 2 * CONV_PAD - 1 - k, tq) * wv[k:k + 1, :]
            dw_ref[k:k + 1, :] += jnp.sum(dym * _shifted(uwin, k + 1, tq), axis=0, keepdims=True)
        zm = zm_ref[...].astype(F32)
        a, gt = zm[:, :CONV_W], zm[:, CONV_W:]
        sgt = _sigmoid(gt)
        dz_ref[:, :CONV_W] = (du * sgt).astype(BF16)
        dz_ref[:, CONV_W:] = (du * a * sgt * (1.0 - sgt)).astype(BF16)

    vec = pl.BlockSpec((1, CONV_W), lambda i: (0, 0))
    acc = pl.BlockSpec((8, CONV_W), lambda i: (0, 0))
    dz, dw, db, dg, dbb = pl.pallas_call(
        body, name=name, grid=(nt,),
        in_specs=[*_halo_specs(tq, 2 * CONV_W, nt, CONV_PAD), *_halo_specs(tq, CONV_W, nt, CONV_PAD),
                  *_halo_specs(tq, CONV_W, nt, CONV_PAD), pl.BlockSpec((32, CONV_W), lambda i: (0, 0)), vec, vec],
        out_specs=[pl.BlockSpec((tq, 2 * CONV_W), lambda i: (i, 0)), pl.BlockSpec((32, CONV_W), lambda i: (0, 0)), acc, acc, acc],
        out_shape=[jax.ShapeDtypeStruct((t, 2 * CONV_W), BF16), jax.ShapeDtypeStruct((32, CONV_W), F32),
                   jax.ShapeDtypeStruct((8, CONV_W), F32), jax.ShapeDtypeStruct((8, CONV_W), F32),
                   jax.ShapeDtypeStruct((8, CONV_W), F32)],
        compiler_params=_cp(1),
    )(zc, zc, zc, y, y, y, dc, dc, dc, w, _row(ln_g), _row(ln_b))
    return dz, dw, db[0], dg[0], dbb[0]


def _att_tiles(seq_len):
    tq = min(256, seq_len)
    return tq, min(128, tq)


def _t5_bucket_np(rel):
    nb = NUM_BUCKETS // 2
    max_exact = nb // 2
    ret = np.where(rel > 0, nb, 0)
    n = np.abs(rel)
    nf = np.maximum(n, 1).astype(np.float32)
    large = max_exact + (np.log(nf / np.float32(max_exact)) / np.float32(np.log(MAX_DISTANCE / max_exact))
                         * np.float32(nb - max_exact)).astype(np.int32)
    large = np.minimum(large, nb - 1)
    return ret + np.where(n < max_exact, n, large)


def _bucket_index(sb, dilation):
    off = np.arange(sb + 2 * ATT_RADIUS)[None, :] - ATT_RADIUS - np.arange(sb)[:, None]
    idx = _t5_bucket_np(off * dilation).astype(np.int32)
    return np.where(np.abs(off) <= ATT_RADIUS, idx, -1).astype(np.int32)


def _bias_table(tab, idx, name):
    sb, w = idx.shape

    def body(tab_ref, idx_ref, o_ref):
        ix = idx_ref[...]
        for h in range(HEADS):
            acc = jnp.full((sb, w), NEG_INF, F32)
            for b in range(NUM_BUCKETS):
                acc = jnp.where(ix == b, tab_ref[b, h], acc)
            o_ref[h] = acc

    return pl.pallas_call(
        body, name=name, out_shape=jax.ShapeDtypeStruct((HEADS, sb, w), F32),
        in_specs=[pl.BlockSpec(memory_space=pltpu.SMEM), pl.BlockSpec(memory_space=pltpu.VMEM)],
        out_specs=pl.BlockSpec(memory_space=pltpu.VMEM),
    )(tab, idx)


def _qkv_specs(tq, per, last_blk, cols, tile):
    def main(col):
        return pl.BlockSpec((tq, ATT_GW), lambda *g: (tile(*g), col))

    def prev(col):
        return pl.BlockSpec((ATT_RADIUS, ATT_GW), lambda *g: (jnp.maximum(tile(*g) * per - 1, 0), col))

    def nxt(col):
        return pl.BlockSpec((ATT_RADIUS, ATT_GW), lambda *g: (jnp.minimum((tile(*g) + 1) * per, last_blk), col))

    cq, ck, cv = cols
    return [main(cq), prev(ck), main(ck), nxt(ck), prev(cv), main(cv), nxt(cv)]


def _att_fwd(q, k, v, bias, seq_len, name, cols=(0, 0, 0)):
    t = q.shape[0]
    tq, sb = _att_tiles(seq_len)
    nbl = seq_len // tq
    nt = t // tq
    per = tq // ATT_RADIUS
    w = sb + 2 * ATT_RADIUS
    last_blk = t // ATT_RADIUS - 1

    def body(q_ref, kp_ref, kc_ref, kn_ref, vp_ref, vc_ref, vn_ref, b_ref, o_ref, l_ref):
        n = pl.program_id(0) % nbl
        kw = jnp.concatenate([kp_ref[...], kc_ref[...], kn_ref[...]], axis=0)
        vw = jnp.concatenate([vp_ref[...], vc_ref[...], vn_ref[...]], axis=0)
        kpos = n * tq - ATT_RADIUS + lax.broadcasted_iota(jnp.int32, (1, tq + 2 * ATT_RADIUS), 1)
        valid = (kpos >= 0) & (kpos < seq_len)
        for j in range(tq // sb):
            qj = q_ref[j * sb:(j + 1) * sb, :]
            kj, vj, okj = kw[j * sb:j * sb + w], vw[j * sb:j * sb + w], valid[:, j * sb:j * sb + w]
            heads = [slice(h * HEAD_DIM, (h + 1) * HEAD_DIM) for h in range(HEADS)]
            ss = [lax.dot_general(qj[:, hs], kj[:, hs], NT, preferred_element_type=F32) for hs in heads]
            es, dens, lses = [], [], []
            for h in range(HEADS):
                s = jnp.where(okj, ss[h] * (HEAD_DIM ** -0.5) + b_ref[h], NEG_INF)
                m = jnp.max(s, axis=-1, keepdims=True)
                e = jnp.exp(s - m)
                den = jnp.sum(e, axis=-1, keepdims=True)
                es.append(e.astype(BF16))
                dens.append(den)
                lses.append(jnp.broadcast_to(m + jnp.log(den), (sb, HEAD_DIM)))
            outs = [jnp.dot(es[h], vj[:, hs], preferred_element_type=F32) / dens[h] for h, hs in enumerate(heads)]
            o_ref[j * sb:(j + 1) * sb, :] = jnp.concatenate(outs, axis=1).astype(BF16)
            l_ref[j * sb:(j + 1) * sb, :] = jnp.concatenate(lses, axis=1)

    main = pl.BlockSpec((tq, ATT_GW), lambda i: (i, 0))
    return pl.pallas_call(
        body, name=name, grid=(nt,),
        in_specs=[*_qkv_specs(tq, per, last_blk, cols, lambda i: i), pl.BlockSpec((HEADS, sb, w), lambda i: (0, 0, 0))],
        out_specs=[main, main],
        out_shape=[jax.ShapeDtypeStruct((t, ATT_GW), BF16), jax.ShapeDtypeStruct((t, ATT_GW), F32)],
        compiler_params=_cp(1),
    )(q, k, k, k, v, v, v, bias)


PERM_ROWS = 2048
LANES = 128


def _spec_rm(d, tr, width, col=0):
    return pl.BlockSpec((d, tr // d, width), lambda i: (0, i, col))


def _rm_view(a, d):
    return a.reshape(d, a.shape[0] // d, a.shape[1])


def _gather_residues(scr, val, d, out_ref, col0):
    tr, w = val.shape
    for c in range(w // LANES):
        scr[...] = val[:, c * LANES:(c + 1) * LANES].astype(F32)
        for r in range(d):
            out_ref[r, :, col0 + c * LANES:col0 + (c + 1) * LANES] = scr[pl.ds(r, tr // d, stride=d), :].astype(out_ref.dtype)


def _scatter_residues(scr, ref, d):
    w = ref.shape[2]
    cols = []
    for c in range(w // LANES):
        for r in range(d):
            scr[pl.ds(r, scr.shape[0] // d, stride=d), :] = ref[r, :, c * LANES:(c + 1) * LANES].astype(F32)
        cols.append(scr[...])
    return jnp.concatenate(cols, axis=1)


def _to_residue_major(srcs, d, name, tr=PERM_ROWS):
    t = srcs[0][0].shape[0]
    tr = min(tr, t)
    n = len(srcs)

    def body(*refs):
        scr = refs[2 * n]
        for k in range(n):
            _gather_residues(scr, refs[k][...], d, refs[n + k], 0)

    outs = pl.pallas_call(
        body, name=name, grid=(t // tr,),
        in_specs=[pl.BlockSpec((tr, ATT_GW), functools.partial(lambda i, col: (i, col), col=col)) for _, col in srcs],
        out_specs=[_spec_rm(d, tr, ATT_GW)] * n,
        out_shape=[jax.ShapeDtypeStruct((d, t // d, ATT_GW), a.dtype) for a, _ in srcs],
        scratch_shapes=[pltpu.VMEM((tr, LANES), F32)], compiler_params=_cp(1),
    )(*[a for a, _ in srcs])
    return [o.reshape(t, ATT_GW) for o in outs]


def _att_combine(outs, lses, name, tr=PERM_ROWS):
    t = outs[0].shape[0]
    tr = min(tr, t)
    dils = [d for _, d in ATT_PATTERNS]

    def body(o0, o1, o2, l0, l1, l2, a_ref, lse_ref, scr):
        os_, ls_ = [], []
        for o_ref, l_ref, d in zip((o0, o1, o2), (l0, l1, l2), dils):
            if d == 1:
                os_.append(o_ref[...].astype(F32))
                ls_.append(l_ref[...])
            else:
                os_.append(_scatter_residues(scr, o_ref, d))
                ls_.append(_scatter_residues(scr, l_ref, d))
        la, lb, lc = ls_
        m = jnp.maximum(jnp.maximum(la, lb), lc)
        ea, eb, ec = jnp.exp(la - m), jnp.exp(lb - m), jnp.exp(lc - m)
        den = ea + eb + ec
        a_ref[...] = ((ea * os_[0] + eb * os_[1] + ec * os_[2]) / den).astype(BF16)
        lse_ref[...] = m + jnp.log(den)

    row = pl.BlockSpec((tr, ATT_GW), lambda i: (i, 0))
    specs = [row if d == 1 else _spec_rm(d, tr, ATT_GW) for d in dils]
    views = lambda arrs: [a if d == 1 else _rm_view(a, d) for a, d in zip(arrs, dils)]
    return pl.pallas_call(
        body, name=name, grid=(t // tr,), in_specs=specs * 2, out_specs=[row, row],
        out_shape=[jax.ShapeDtypeStruct((t, ATT_GW), BF16), jax.ShapeDtypeStruct((t, ATT_GW), F32)],
        scratch_shapes=[pltpu.VMEM((tr, LANES), F32)], compiler_params=_cp(1),
    )(*views(outs), *views(lses))


def _assemble_dza(dqkv, name, tr=PERM_ROWS):
    t = dqkv[0][0].shape[0]
    tr = min(tr, t)
    dils = [d for _, d in ATT_PATTERNS]

    def body(*refs):
        o_ref, scr = refs[9], refs[10]
        for g, d in enumerate(dils):
            for c in range(3):
                ref = refs[g * 3 + c]
                val = ref[...] if d == 1 else _scatter_residues(scr, ref, d).astype(BF16)
                o_ref[:, (c * 3 + g) * ATT_GW:(c * 3 + g + 1) * ATT_GW] = val

    row = pl.BlockSpec((tr, ATT_GW), lambda i: (i, 0))
    specs = [row if d == 1 else _spec_rm(d, tr, ATT_GW) for d in dils for _ in range(3)]
    args = [a if d == 1 else _rm_view(a, d) for trio, d in zip(dqkv, dils) for a in trio]
    return pl.pallas_call(
        body, name=name, grid=(t // tr,), in_specs=specs, out_specs=pl.BlockSpec((tr, 3 * ATT_W), lambda i: (i, 0)),
        out_shape=jax.ShapeDtypeStruct((t, 3 * ATT_W), BF16),
        scratch_shapes=[pltpu.VMEM((tr, LANES), F32)], compiler_params=_cp(1),
    )(*args)


def _att_bwd(q, k, v, bias, o, do, lse, seq_len, name, cols=(0, 0, 0)):
    t = q.shape[0]
    tq, sb = _att_tiles(seq_len)
    nbl = seq_len // tq
    n_res = t // seq_len
    per = tq // ATT_RADIUS
    w = sb + 2 * ATT_RADIUS
    last_blk = t // ATT_RADIUS - 1
    acc_cols = 2 * tq - sb + w
    scale = HEAD_DIM ** -0.5

    def body(q_ref, kp_ref, kc_ref, kn_ref, vp_ref, vc_ref, vn_ref, b_ref, o_ref, do_ref, l_ref,
             dq_ref, dk_ref, dv_ref, db_ref, ak_ref, av_ref):
        r, n = pl.program_id(0), pl.program_id(1)

        @pl.when((r == 0) & (n == 0))
        def _():
            db_ref[...] = jnp.zeros_like(db_ref)

        @pl.when(n == 0)
        def _():
            ak_ref[...] = jnp.zeros_like(ak_ref)
            av_ref[...] = jnp.zeros_like(av_ref)

        @pl.when(n < nbl)
        def _():
            kw = jnp.concatenate([kp_ref[...], kc_ref[...], kn_ref[...]], axis=0)
            vw = jnp.concatenate([vp_ref[...], vc_ref[...], vn_ref[...]], axis=0)
            kpos = n * tq - ATT_RADIUS + lax.broadcasted_iota(jnp.int32, (1, tq + 2 * ATT_RADIUS), 1)
            valid = (kpos >= 0) & (kpos < seq_len)
            for j in range(tq // sb):
                rows = slice(j * sb, (j + 1) * sb)
                qj, doj = q_ref[rows, :], do_ref[rows, :]
                dd = doj.astype(F32) * o_ref[rows, :].astype(F32)
                lj = l_ref[rows, :]
                kj, vj, okj = kw[j * sb:j * sb + w], vw[j * sb:j * sb + w], valid[:, j * sb:j * sb + w]
                heads = [slice(h * HEAD_DIM, (h + 1) * HEAD_DIM) for h in range(HEADS)]
                ss = [lax.dot_general(qj[:, hs], kj[:, hs], NT, preferred_element_type=F32) for hs in heads]
                dps = [lax.dot_general(doj[:, hs], vj[:, hs], NT, preferred_element_type=F32) for hs in heads]
                ps, dss, dsbs = [], [], []
                for h, hs in enumerate(heads):
                    s = jnp.where(okj, ss[h] * scale + b_ref[h], NEG_INF)
                    p = jnp.exp(s - lj[:, h * HEAD_DIM:h * HEAD_DIM + 1])
                    ds = p * (dps[h] - jnp.sum(dd[:, hs], axis=-1, keepdims=True))
                    ps.append(p.astype(BF16))
                    dss.append(ds)
                    dsbs.append(ds.astype(BF16))
                dqs = [jnp.dot(dsbs[h], kj[:, hs], preferred_element_type=F32) * scale for h, hs in enumerate(heads)]
                dks = [lax.dot_general(qj[:, hs], dsbs[h], (((0,), (0,)), ((), ())), preferred_element_type=F32) * scale
                       for h, hs in enumerate(heads)]
                dvs = [lax.dot_general(doj[:, hs], ps[h], (((0,), (0,)), ((), ())), preferred_element_type=F32)
                       for h, hs in enumerate(heads)]
                dq_ref[rows, :] = jnp.concatenate(dqs, axis=1).astype(BF16)
                c0 = tq + j * sb
                ak_ref[:, c0:c0 + w] += jnp.concatenate(dks, axis=0)
                av_ref[:, c0:c0 + w] += jnp.concatenate(dvs, axis=0)
                for h in range(HEADS):
                    db_ref[h] += dss[h]

        dk_ref[...] = ak_ref[:, ATT_RADIUS:ATT_RADIUS + tq].T.astype(BF16)
        dv_ref[...] = av_ref[:, ATT_RADIUS:ATT_RADIUS + tq].T.astype(BF16)
        keep = acc_cols - tq
        nk, nv = ak_ref[:, tq:acc_cols], av_ref[:, tq:acc_cols]
        ak_ref[:, 0:keep] = nk
        av_ref[:, 0:keep] = nv
        ak_ref[:, keep:acc_cols] = jnp.zeros((ATT_GW, tq), F32)
        av_ref[:, keep:acc_cols] = jnp.zeros((ATT_GW, tq), F32)

    def tile(r, n):
        return r * nbl + jnp.minimum(n, nbl - 1)

    main = pl.BlockSpec((tq, ATT_GW), lambda r, n: (tile(r, n), 0))
    lag = pl.BlockSpec((tq, ATT_GW), lambda r, n: (r * nbl + jnp.maximum(n - 1, 0), 0))
    bspec = pl.BlockSpec((HEADS, sb, w), lambda r, n: (0, 0, 0))
    return pl.pallas_call(
        body, name=name, grid=(n_res, nbl + 1),
        in_specs=[*_qkv_specs(tq, per, last_blk, cols, tile), bspec, main, main, main],
        out_specs=[main, lag, lag, bspec],
        out_shape=[jax.ShapeDtypeStruct((t, ATT_GW), BF16)] * 3 + [jax.ShapeDtypeStruct((HEADS, sb, w), F32)],
        scratch_shapes=[pltpu.VMEM((ATT_GW, acc_cols), F32), pltpu.VMEM((ATT_GW, acc_cols), F32)],
        compiler_params=_cp(2),
    )(q, k, k, k, v, v, v, bias, o, do, lse)


def _bias_grad(db, idx, name):
    _, sb, w = db.shape

    def body(db_ref, idx_ref, o_ref):
        ix = idx_ref[...]
        lane = lax.broadcasted_iota(jnp.int32, (1, 128), 1)
        rows = []
        for h in range(HEADS):
            d = db_ref[h]
            acc = jnp.zeros((1, 128), F32)
            for b in range(NUM_BUCKETS):
                acc = acc + jnp.where(lane == b, jnp.sum(jnp.where(ix == b, d, 0.0)), 0.0)
            rows.append(acc)
        o_ref[...] = jnp.concatenate(rows + [jnp.zeros((8 - HEADS, 128), F32)], axis=0)

    return pl.pallas_call(
        body, name=name, out_shape=jax.ShapeDtypeStruct((8, 128), F32),
        in_specs=[pl.BlockSpec(memory_space=pltpu.VMEM), pl.BlockSpec(memory_space=pltpu.VMEM)],
        out_specs=pl.BlockSpec(memory_space=pltpu.VMEM),
    )(db, idx)


def _mem_fwd(zq, kv, name, tq=512):
    t = zq.shape[0]
    tq = min(tq, t)
    scale = MEM_HD ** -0.5

    def body(q_ref, kv_ref, o_ref):
        outs = []
        for h in range(MEM_HEADS):
            hs = slice(h * MEM_HD, (h + 1) * MEM_HD)
            kh = kv_ref[:, h * MEM_HD:(h + 1) * MEM_HD]
            vh = kv_ref[:, MEM_W + h * MEM_HD:MEM_W + (h + 1) * MEM_HD]
            s = lax.dot_general(q_ref[:, hs], kh, (((1,), (1,)), ((), ())), preferred_element_type=F32) * scale
            e = jnp.exp(s - jnp.max(s, axis=-1, keepdims=True))
            den = jnp.sum(e, axis=-1, keepdims=True)
            outs.append(jnp.dot(e.astype(BF16), vh, preferred_element_type=F32) / den)
        o_ref[...] = jnp.concatenate(outs, axis=1).astype(BF16)

    row = pl.BlockSpec((tq, MEM_W), lambda i: (i, 0))
    return pl.pallas_call(
        body, name=name, grid=(t // tq,),
        in_specs=[row, pl.BlockSpec((N_MEM, 2 * MEM_W), lambda i: (0, 0))], out_specs=row,
        out_shape=jax.ShapeDtypeStruct((t, MEM_W), BF16), compiler_params=_cp(1),
    )(zq, kv)


def _mem_bwd(zq, kv, do, name, tq=512):
    t = zq.shape[0]
    tq = min(tq, t)
    nt = t // tq
    scale = MEM_HD ** -0.5

    def body(q_ref, kv_ref, do_ref, dq_ref, out_ref, dkv_ref):
        @pl.when(pl.program_id(0) == 0)
        def _():
            dkv_ref[...] = jnp.zeros_like(dkv_ref)

        ks = [slice(h * MEM_HD, (h + 1) * MEM_HD) for h in range(MEM_HEADS)]
        vs = [slice(MEM_W + h * MEM_HD, MEM_W + (h + 1) * MEM_HD) for h in range(MEM_HEADS)]
        ss = [lax.dot_general(q_ref[:, k], kv_ref[:, k], NT, preferred_element_type=F32) * scale for k in ks]
        dps = [lax.dot_general(do_ref[:, k], kv_ref[:, v], NT, preferred_element_type=F32) for k, v in zip(ks, vs)]
        pbs, dsbs = [], []
        for s, dp in zip(ss, dps):
            e = jnp.exp(s - jnp.max(s, axis=-1, keepdims=True))
            p = e / jnp.sum(e, axis=-1, keepdims=True)
            ds = p * (dp - jnp.sum(p * dp, axis=-1, keepdims=True))
            pbs.append(p.astype(BF16))
            dsbs.append(ds.astype(BF16))
        dq_ref[...] = jnp.concatenate([jnp.dot(dsb, kv_ref[:, k], preferred_element_type=F32) * scale
                                       for dsb, k in zip(dsbs, ks)], axis=1).astype(BF16)
        for dsb, pb, k, v in zip(dsbs, pbs, ks, vs):
            dkv_ref[k, :] += jnp.dot(q_ref[:, k].T, dsb, preferred_element_type=F32) * scale
            dkv_ref[v, :] += jnp.dot(do_ref[:, k].T, pb, preferred_element_type=F32)

        @pl.when(pl.program_id(0) == nt - 1)
        def _():
            out_ref[...] = dkv_ref[...].T

    row = pl.BlockSpec((tq, MEM_W), lambda i: (i, 0))
    full = pl.BlockSpec((N_MEM, 2 * MEM_W), lambda i: (0, 0))
    return pl.pallas_call(
        body, name=name, grid=(nt,), in_specs=[row, full, row], out_specs=[row, full],
        out_shape=[jax.ShapeDtypeStruct((t, MEM_W), BF16), jax.ShapeDtypeStruct((N_MEM, 2 * MEM_W), F32)],
        scratch_shapes=[pltpu.VMEM((2 * MEM_W, N_MEM), F32)], compiler_params=_cp(1),
    )(zq, kv, do)


NT = (((1,), (1,)), ((), ()))


def _branches_fwd(acts, ws, zg, bg, name, tr=512):
    t = zg.shape[0]
    tr = min(tr, t)

    def body(a0, a1, a2, w0, w1, w2, z_ref, b_ref, y0, y1, y2, o_ref):
        merged = None
        for i, (a_ref, w_ref, y_ref) in enumerate(((a0, w0, y0), (a1, w1, y1), (a2, w2, y2))):
            y = lax.dot_general(a_ref[...], w_ref[...], NT, preferred_element_type=F32)
            y_ref[...] = y.astype(BF16)
            g = _sigmoid(z_ref[:, i * D:(i + 1) * D].astype(F32) + b_ref[:, i * D:(i + 1) * D])
            merged = g * y if merged is None else merged + g * y
        o_ref[...] = merged.astype(BF16)

    row = pl.BlockSpec((tr, D), lambda i: (i, 0))
    outs = pl.pallas_call(
        body, name=name, grid=(t // tr,),
        in_specs=[pl.BlockSpec((tr, a.shape[1]), lambda i: (i, 0)) for a in acts]
        + [pl.BlockSpec(w.shape, lambda i: (0, 0)) for w in ws]
        + [pl.BlockSpec((tr, 3 * D), lambda i: (i, 0)), pl.BlockSpec((1, 3 * D), lambda i: (0, 0))],
        out_specs=[row] * 4, out_shape=[jax.ShapeDtypeStruct((t, D), BF16)] * 4, compiler_params=_cp(1),
    )(*acts, *ws, zg, _row(bg))
    return tuple(outs[:3]), outs[3]


def _branches_bwd(dymix, w_out, ws, zg, bg, ys, name, tr=512):
    t = zg.shape[0]
    tr = min(tr, t)

    def body(dy_ref, wo_ref, w0, w1, w2, z_ref, b_ref, y0, y1, y2, d0, d1, d2, dz_ref, db_ref, da0, da1, da2):
        @pl.when(pl.program_id(0) == 0)
        def _():
            db_ref[...] = jnp.zeros_like(db_ref)

        dm = lax.dot_general(dy_ref[...], wo_ref[...], NT, preferred_element_type=F32)
        for i, (y_ref, w_ref, d_ref, da_ref) in enumerate(((y0, w0, d0, da0), (y1, w1, d1, da1), (y2, w2, d2, da2))):
            gi = _sigmoid(z_ref[:, i * D:(i + 1) * D].astype(F32) + b_ref[:, i * D:(i + 1) * D])
            dy = (dm * gi).astype(BF16)
            d_ref[...] = dy
            dz = dm * y_ref[...].astype(F32) * gi * (1.0 - gi)
            dz_ref[:, i * D:(i + 1) * D] = dz.astype(BF16)
            db_ref[0:1, i * D:(i + 1) * D] += jnp.sum(dz, axis=0, keepdims=True)
            da_ref[...] = jnp.dot(dy, w_ref[...], preferred_element_type=F32).astype(BF16)

    row = pl.BlockSpec((tr, D), lambda i: (i, 0))
    wide = pl.BlockSpec((tr, 3 * D), lambda i: (i, 0))
    outs = pl.pallas_call(
        body, name=name, grid=(t // tr,),
        in_specs=[row, pl.BlockSpec(w_out.shape, lambda i: (0, 0))] + [pl.BlockSpec(w.shape, lambda i: (0, 0)) for w in ws]
        + [wide, pl.BlockSpec((1, 3 * D), lambda i: (0, 0)), row, row, row],
        out_specs=[row, row, row, wide, pl.BlockSpec((8, 3 * D), lambda i: (0, 0))]
        + [pl.BlockSpec((tr, w.shape[1]), lambda i: (i, 0)) for w in ws],
        out_shape=[jax.ShapeDtypeStruct((t, D), BF16)] * 3
        + [jax.ShapeDtypeStruct((t, 3 * D), BF16), jax.ShapeDtypeStruct((8, 3 * D), F32)]
        + [jax.ShapeDtypeStruct((t, w.shape[1]), BF16) for w in ws],
        compiler_params=_cp(1),
    )(dymix, w_out, *ws, zg, _row(bg), *ys)
    return tuple(outs[:3]), outs[3], outs[4][0], tuple(outs[5:])


def _adamw(w, g, m, v, name):
    shape = w.shape
    w, g, m, v = (a.reshape(-1, shape[-1]) for a in (w, g, m, v))
    r, d = w.shape
    tr = next((c for c in (512, 352, 256, 128, 64, 32, 16, 8) if r % c == 0), r)

    def body(w_ref, g_ref, m_ref, v_ref, d_ref, nm_ref, nv_ref):
        gg = g_ref[...]
        m2 = ADAM_B1 * m_ref[...] + (1.0 - ADAM_B1) * gg
        v2 = ADAM_B2 * v_ref[...] + (1.0 - ADAM_B2) * (gg * gg)
        m_hat = m2 / (1.0 - ADAM_B1 ** ADAM_STEP)
        v_hat = v2 / (1.0 - ADAM_B2 ** ADAM_STEP)
        d_ref[...] = -ADAM_LR * (m_hat / (jnp.sqrt(v_hat) + ADAM_EPS) + ADAM_WD * w_ref[...])
        nm_ref[...] = m2
        nv_ref[...] = v2

    row = pl.BlockSpec((tr, d), lambda i: (i, 0))
    outs = pl.pallas_call(
        body, name=name, grid=(r // tr,), in_specs=[row] * 4, out_specs=[row] * 3,
        out_shape=[jax.ShapeDtypeStruct((r, d), F32)] * 3, compiler_params=_cp(1),
    )(w, g, m, v)
    return [o.reshape(shape) for o in outs]


def _slab_tile(rows):
    return next((c for c in range(min(rows, 512), 15, -16) if rows % c == 0), rows)


def _sum_slots(parts, out_dtype, name):
    n, r, d = parts.shape
    tr = _slab_tile(r)

    def body(p_ref, o_ref):
        acc = p_ref[0].astype(F32)
        for s in range(1, n):
            acc = acc + p_ref[s].astype(F32)
        o_ref[...] = acc.astype(o_ref.dtype)

    return pl.pallas_call(
        body, name=name, grid=(r // tr,), in_specs=[pl.BlockSpec((n, tr, d), lambda i: (0, i, 0))],
        out_specs=pl.BlockSpec((tr, d), lambda i: (i, 0)),
        out_shape=jax.ShapeDtypeStruct((r, d), out_dtype), compiler_params=_cp(1),
    )(parts)


def _place():
    return lax.axis_index("x"), lax.axis_index("y"), lax.axis_index("c")


def _all_gather(shard, name):
    r, d = shard.shape

    def body(x_ref, out_ref, send_sems, recv_sems, local_sem):
        x, y, c = _place()
        me, sibling = (x, y, c), (x, y, 1 - c)
        chips = [(1 - x, y), (x, 1 - y), (1 - x, 1 - y)]

        def slot(px, py, pc):
            return out_ref.at[4 * px + 2 * py + pc]

        def copy(k, block, to, src=None):
            return pltpu.make_async_remote_copy(
                src_ref=slot(*block) if src is None else src, dst_ref=slot(*block),
                send_sem=send_sems.at[k], recv_sem=recv_sems.at[k], device_id=to, device_id_type=MESH)

        mine = pltpu.make_async_copy(x_ref, slot(*me), local_sem)
        mine.start()
        first = [copy(0, me, sibling, src=x_ref)]
        first += [copy(1 + j, me, (*chip, c), src=x_ref) for j, chip in enumerate(chips)]
        for cp in first:
            cp.start()
        passed = [copy(4 + j, (*chip, c), sibling) for j, chip in enumerate(chips)]
        for j, chip in enumerate(chips):
            copy(1 + j, (*chip, c), me).wait_recv()
            passed[j].start()
        copy(0, sibling, me).wait_recv()
        for j, chip in enumerate(chips):
            copy(4 + j, (*chip, 1 - c), me).wait_recv()
        for cp in first + passed:
            cp.wait_send()
        mine.wait()

    return pl.pallas_call(
        body, name=name, out_shape=jax.ShapeDtypeStruct((N_DEV, r, d), shard.dtype),
        in_specs=[pl.BlockSpec(memory_space=pl.ANY)], out_specs=pl.BlockSpec(memory_space=pl.ANY),
        scratch_shapes=[pltpu.SemaphoreType.DMA((7,)), pltpu.SemaphoreType.DMA((7,)), pltpu.SemaphoreType.DMA],
    )(shard)


HBM_SPEC = pl.BlockSpec(memory_space=pltpu.HBM)
SEM_SPEC = pl.BlockSpec(memory_space=pltpu.SEMAPHORE)
SPLIT_PARAMS = pltpu.CompilerParams(has_side_effects=pltpu.SideEffectType.DATAFLOW_SIDE_EFFECTING)


def _peers():
    x, y, c = _place()
    flip = lambda v, bit: 1 - v if bit else v
    return 4 * x + 2 * y + c, [((flip(x, k >> 2 & 1), flip(y, k >> 1 & 1), flip(c, k & 1))) for k in range(1, N_DEV)]


def _exchange_start(src, all_gather, name):
    r, d = src.shape[-2:]

    def body(src_ref, land_ref, send_sems, recv_sems, src_thru, land_thru, token):
        me, peers = _peers()
        for k, (px, py, pc) in enumerate(peers):
            part = src_ref if all_gather else src_ref.at[4 * px + 2 * py + pc]
            pltpu.make_async_remote_copy(src_ref=part, dst_ref=land_ref.at[me], send_sem=send_sems.at[k],
                                         recv_sem=recv_sems.at[k], device_id=(px, py, pc), device_id_type=MESH).start()
        token[...] = jnp.zeros_like(token)

    land = lax.empty((N_DEV, r, d), src.dtype)
    return pl.pallas_call(
        body, name=name,
        out_shape=(pltpu.SemaphoreType.DMA((N_DEV - 1,)), pltpu.SemaphoreType.DMA((N_DEV - 1,)), pltpu.HBM(src.shape, src.dtype),
                   pltpu.HBM(land.shape, land.dtype), jax.ShapeDtypeStruct((8, LANES), F32)),
        in_specs=(HBM_SPEC, HBM_SPEC), out_specs=(SEM_SPEC, SEM_SPEC, HBM_SPEC, HBM_SPEC, pl.BlockSpec(memory_space=pltpu.VMEM)),
        input_output_aliases={0: 2, 1: 3}, compiler_params=SPLIT_PARAMS,
    )(pltpu.with_memory_space_constraint(src, pltpu.HBM), pltpu.with_memory_space_constraint(land, pltpu.HBM))


def _exchange_wait(started, after, all_gather, name):
    send_sems, recv_sems, src_thru, land_thru, _ = started

    def body(src_ref, land_ref, send_sems, recv_sems, after_ref, src_out, land_out):
        me, peers = _peers()
        for k, (px, py, pc) in enumerate(peers):
            part = src_ref if all_gather else src_ref.at[4 * px + 2 * py + pc]
            cp = pltpu.make_async_remote_copy(src_ref=part, dst_ref=land_ref.at[4 * px + 2 * py + pc], send_sem=send_sems.at[k],
                                              recv_sem=recv_sems.at[k], device_id=(px, py, pc), device_id_type=MESH)
            cp.wait_send()
            cp.wait_recv()

    return pl.pallas_call(
        body, name=name,
        out_shape=(pltpu.HBM(src_thru.shape, src_thru.dtype), pltpu.HBM(land_thru.shape, land_thru.dtype)),
        in_specs=(HBM_SPEC, HBM_SPEC, SEM_SPEC, SEM_SPEC, pl.BlockSpec(memory_space=pl.ANY)), out_specs=(HBM_SPEC, HBM_SPEC),
        input_output_aliases={0: 0, 1: 1}, compiler_params=SPLIT_PARAMS,
    )(src_thru, land_thru, send_sems, recv_sems, after)


def _own_slot(land, mine):
    x, y, c = _place()
    return lax.dynamic_update_slice(land, mine[None], (4 * x + 2 * y + c, 0, 0))


BIG = (("w_in", (D, 864), 1), ("w_conv_out", (CONV_W, 128), 1), ("w_att_out", (ATT_GW, 128), 1), ("w_mem_kv", (128, D), 0),
       ("w_mem_out", (MEM_W, 128), 1), ("w_out", (128, D), 0), ("w_ffn_in", (D, 704), 1), ("w_ffn_out", (352, D), 0),
       ("conv_dw", (CONV_K, 64), 1))
SMALL = (("rel_bias", (NUM_BUCKETS, 12)), ("norm_mix_pre", (DEPTH, D)), ("b_gate", (DEPTH, 3 * D)),
         ("conv_dw_bias", (DEPTH, CONV_W)), ("conv_ln_g", (DEPTH, CONV_W)), ("conv_ln_b", (DEPTH, CONV_W)),
         ("norm_mem", (DEPTH, D)), ("norm_mix_post", (DEPTH, D)), ("norm_ffn_pre", (DEPTH, D)), ("norm_ffn_post", (DEPTH, D)))
TWIN_WEIGHTS = ("rel_bias", "norm_mix_pre", "w_in", "b_gate", "conv_dw", "conv_dw_bias", "conv_ln_g", "conv_ln_b", "w_conv_out",
                "w_att_out", "norm_mem", "w_mem_kv", "w_mem_out", "w_out", "norm_mix_post", "norm_ffn_pre", "w_ffn_in",
                "w_ffn_out", "norm_ffn_post")


BIG_INFO = {n: (s, a) for n, s, a in BIG}
GROUPS = {"mix": ("w_in", "w_conv_out", "w_att_out", "w_mem_kv", "w_mem_out", "w_out", "conv_dw"), "ffn": ("w_ffn_in", "w_ffn_out")}
TILE_BYTES_PER_LANE = 32


def _rows_of(shape, dtype):
    tile = TILE_BYTES_PER_LANE // jnp.dtype(dtype).itemsize
    return -(-int(np.prod(shape)) // (D * tile)) * tile


def _row_form(shape, axis):
    return tuple(shape) if axis == 0 else (shape[1], shape[0])


def _as_slab_rows(p, lead=()):
    shape = p.shape[len(lead):]
    rows = _rows_of(shape, p.dtype)
    if shape == (rows, D):
        return p
    n = int(np.prod(shape))
    p = jnp.pad(p.reshape(*lead, n), [(0, 0)] * len(lead) + [(0, rows * D - n)])
    return p.reshape(*lead, rows, D)


def _from_slab_rows(rows, shape, lead=()):
    if rows.shape[len(lead):] == tuple(shape):
        return rows
    n = int(np.prod(shape))
    return rows.reshape(*lead, -1)[..., :n].reshape(*lead, *shape)


def _pack(pieces, dtype):
    return jnp.concatenate([_as_slab_rows(p.astype(dtype)) for p in pieces], axis=0)


def _unpack(slab, shapes, packed_as):
    out, r0 = [], 0
    for s in shapes:
        rows = _rows_of(s, packed_as)
        out.append(_from_slab_rows(slab[r0:r0 + rows], s))
        r0 += rows
    return out


def _pack_group(blocks, layer, group):
    return _pack([blocks[n][layer] if BIG_INFO[n][1] == 0 else blocks[n][layer].T for n in GROUPS[group]], BF16)


def _full_weights(gathered, group):
    out, r0 = {}, 0
    for name in GROUPS[group]:
        r, k = _row_form(*BIG_INFO[name])
        rows = _rows_of((r, k), BF16)
        out[name] = _from_slab_rows(gathered[:, r0:r0 + rows], (r, k), lead=(N_DEV,)).reshape(N_DEV * r, k)
        r0 += rows
    return out


def _scatter_layout(grads, group):
    parts = []
    for name in GROUPS[group]:
        r, k = _row_form(*BIG_INFO[name])
        parts.append(_as_slab_rows(grads[name].astype(BF16).reshape(N_DEV, r, k), lead=(N_DEV,)))
    return jnp.concatenate(parts, axis=1)


def _unpack_blocks(slab, group):
    out = _unpack(slab, [_row_form(*BIG_INFO[n]) for n in GROUPS[group]], BF16)
    return {n: b if BIG_INFO[n][1] == 0 else b.T for n, b in zip(GROUPS[group], out)}


def _split_w_in(w_in):
    edges = np.cumsum((0,) + SEG)
    return [w_in[edges[i]:edges[i + 1]] for i in range(4)]


def _conv_taps(conv_dw):
    return jnp.pad(conv_dw.astype(F32).T, ((0, 32 - CONV_K), (0, 0)))


def _layer_fwd(l, x, h, nm, get_w, sm, bias_tabs):
    t = x.shape[0]
    wts = get_w(l, "mix", h)
    w_seg = _split_w_in(wts["w_in"])
    zc = _mm(h, w_seg[0], "fwd_in_conv", tb=True)
    za = _mm(h, w_seg[1], "fwd_in_att", tb=True, tn=1152)
    zq = _mm(h, w_seg[2], "fwd_in_memq", tb=True)
    zg = _mm(h, w_seg[3], "fwd_in_gate", tb=True)
    y, cact = _conv_fwd(zc, _conv_taps(wts["conv_dw"]), sm["conv_dw_bias"][l], sm["conv_ln_g"][l], sm["conv_ln_b"][l],
                        "conv_fwd")
    qkv, outs, lses = [], [], []
    for g, (_, dil) in enumerate(ATT_PATTERNS):
        cols = tuple(i * 3 + g for i in range(3))
        if dil == 1:
            trio, at = (za, za, za), cols
        else:
            trio, at = _to_residue_major([(za, c) for c in cols], dil, f"qkv_residue_d{dil}"), (0, 0, 0)
        o, ls = _att_fwd(*trio, bias_tabs[g], t // dil, f"att_fwd_d{dil}", cols=at)
        qkv.append((trio, at))
        outs.append(o)
        lses.append(ls)
    att, lse = _att_combine(outs, lses, "att_combine")
    kv = _mm(nm, wts["w_mem_kv"], "fwd_mem_kv")
    om = _mem_fwd(zq, kv, "mem_fwd")
    w_br = (wts["w_conv_out"], wts["w_att_out"], wts["w_mem_out"])
    ys, merged = _branches_fwd((cact, att, om), w_br, zg, sm["b_gate"][l], "branches_fwd")
    ymix, x1, h2 = _mm_norm_res(merged, wts["w_out"], x, sm["norm_mix_post"][l], sm["norm_ffn_pre"][l], "fwd_out_norm")
    wff = get_w(l, "ffn", h2)
    gff, uff, act = _mm_swiglu_fwd(h2, wff["w_ffn_in"], "fwd_ffn_in_swiglu")
    g_next = sm["norm_mix_pre"][l + 1] if l + 1 < DEPTH else None
    f, x2, h_next = _mm_norm_res(act, wff["w_ffn_out"], x1, sm["norm_ffn_post"][l], g_next,
                                 "fwd_ffn_out_norm" if g_next is not None else "fwd_ffn_out_last")
    saved = dict(x=x, h=h, zc=zc, zq=zq, zg=zg, y=y, cact=cact, qkv=qkv, att=att, lse=lse, kv=kv, om=om, ys=ys,
                 merged=merged, ymix=ymix, x1=x1, h2=h2, gff=gff, uff=uff, act=act, f=f)
    return x2, h_next, saved


def _after(value, *tokens):
    for t in tokens:
        if t is not None:
            value = value + t[0, 0].astype(value.dtype)
    return value


def _layer_bwd(l, dx2, s, nm, mem, get_w, put_g, sm, bias_tabs, bucket_idx):
    t = dx2.shape[0]
    wts, wff = get_w(l, "mix", None), get_w(l, "ffn", None)
    gb, gf, gs = {}, {}, {}
    df, gs["norm_ffn_post"] = _norm_bwd(s["f"], sm["norm_ffn_post"][l], dx2, None, BF16, "bwd_norm_ffn_post")
    dg, du = _mm_swiglu_bwd(df, wff["w_ffn_out"], s["gff"], s["uff"], "bwd_ffn_out_d_swiglu")
    gf["w_ffn_out"] = _mm(s["act"], df, "bwd_ffn_out_w", ta=True, tm=1408)
    gf["w_ffn_in"] = jnp.concatenate([_mm(dg, s["h2"], "bwd_ffn_in_w_gate", ta=True, tm=1408),
                                      _mm(du, s["h2"], "bwd_ffn_in_w_up", ta=True, tm=1408)], axis=0)
    tok = put_g(l, "ffn", gf)
    dh2 = _mm_sum([(dg, _after(wff["w_ffn_in"][:FFN_H], tok), FFN_H // 2), (du, wff["w_ffn_in"][FFN_H:], FFN_H // 2)],
                  "bwd_ffn_in_d")
    dx1, gs["norm_ffn_pre"] = _norm_bwd(s["x1"], sm["norm_ffn_pre"][l], dh2, dx2, F32, "bwd_norm_ffn_pre")
    dymix, gs["norm_mix_post"] = _norm_bwd(s["ymix"], sm["norm_mix_post"][l], dx1, None, BF16, "bwd_norm_mix_post")
    gb["w_out"] = _mm(s["merged"], dymix, "bwd_out_w", ta=True)
    w_br = (wts["w_conv_out"], wts["w_att_out"], wts["w_mem_out"])
    dys, dzg, gs["b_gate"], (dcact, datt, dom) = _branches_bwd(dymix, wts["w_out"], w_br, s["zg"], sm["b_gate"][l], s["ys"],
                                                               "branches_bwd")
    gb["w_conv_out"] = _mm(dys[0], s["cact"], "bwd_conv_out_w", ta=True)
    dzc, dconv, gs["conv_dw_bias"], gs["conv_ln_g"], gs["conv_ln_b"] = _conv_bwd(
        s["zc"], s["y"], dcact, _conv_taps(wts["conv_dw"]), sm["conv_ln_g"][l], sm["conv_ln_b"][l], "conv_bwd")
    gb["conv_dw"] = dconv[:CONV_K].T
    gb["w_att_out"] = _mm(dys[1], s["att"], "bwd_att_out_w", ta=True)
    dqkv, rel = [], []
    for g, (_, dil) in enumerate(ATT_PATTERNS):
        trio, at = s["qkv"][g]
        if dil == 1:
            o_r, do_r, lse_r = s["att"], datt, s["lse"]
        else:
            o_r, do_r, lse_r = _to_residue_major([(s["att"], 0), (datt, 0), (s["lse"], 0)], dil, f"att_bwd_residue_d{dil}")
        dq, dk, dv, dbias = _att_bwd(*trio, bias_tabs[g], o_r, do_r, lse_r, t // dil, f"att_bwd_d{dil}", cols=at)
        dqkv.append((dq, dk, dv))
        rel.append(_bias_grad(dbias, bucket_idx[g], f"bias_grad_d{dil}")[:HEADS, :NUM_BUCKETS])
    dza = _assemble_dza(dqkv, "att_bwd_assemble")
    gs["rel_bias"] = jnp.concatenate(rel, axis=0).T
    gb["w_mem_out"] = _mm(dys[2], s["om"], "bwd_mem_out_w", ta=True)
    dzq, dkv = _mem_bwd(s["zq"], s["kv"], dom, "mem_bwd")
    dkv = dkv.astype(BF16)
    gb["w_mem_kv"] = _mm(nm, dkv, "bwd_mem_kv_w", ta=True)
    dnm = _mm(dkv, wts["w_mem_kv"], "bwd_mem_kv_d", tb=True, out_dtype=F32)
    _, gs["norm_mem"] = _norm_bwd(mem, sm["norm_mem"][l], dnm, None, BF16, "bwd_norm_mem")
    segs = ((dzc, "conv", 1024), (dza, "att", 1152), (dzq, "memq", 512), (dzg, "gate", 1024))
    gb["w_in"] = jnp.concatenate([_mm(dz, s["h"], f"bwd_in_{nm_}_w", ta=True, tm=min(blk, 1152), tk=2048)
                                  for dz, nm_, blk in segs], axis=0)
    tok = put_g(l, "mix", gb)
    w_seg = _split_w_in(wts["w_in"])
    w_seg[0] = _after(w_seg[0], tok)
    dx, gs["norm_mix_pre"] = _mm_sum([(dz, w_seg[i], blk) for i, (dz, _, blk) in enumerate(segs)], "bwd_in_d_norm",
                                     norm=(s["x"], sm["norm_mix_pre"][l], dx1))
    return dx, gs


def _local_step(x, mem, target, sm, get_w, put_g, tokens=()):
    t = x.shape[0]
    bias_tabs, bucket_idx = [], []
    for g, (_, dil) in enumerate(ATT_PATTERNS):
        idx = _bucket_index(_att_tiles(t // dil)[1], dil)
        bucket_idx.append(jnp.asarray(idx))
        bias_tabs.append(_bias_table(sm["rel_bias"][:, g * HEADS:(g + 1) * HEADS], bucket_idx[g], f"bias_table_d{dil}"))
    h = _norm_plain(x, _after(sm["norm_mix_pre"][0], *tokens), "norm_first")
    saved, nms = [], []
    for l in range(DEPTH):
        nm = _norm_plain(mem, sm["norm_mem"][l], "norm_mem")
        x, h, s = _layer_fwd(l, x, h, nm, get_w, sm, bias_tabs)
        saved.append(s)
        nms.append(nm)
    loss, dx = _loss_head(x, target, "loss_head")
    gsmall = {}
    for l in reversed(range(DEPTH)):
        dx, gs = _layer_bwd(l, dx, saved[l], nms[l], mem, get_w, put_g, sm, bias_tabs, bucket_idx)
        for n, v in gs.items():
            gsmall.setdefault(n, {})[l] = v
    small = {}
    for n, _ in SMALL:
        small[n] = gsmall[n][0] + gsmall[n][1] if n == "rel_bias" else jnp.stack([gsmall[n][0], gsmall[n][1]])
    return loss, dx, small


def kernel(x, mem, rel_bias, norm_mix_pre, w_in, b_gate, conv_dw, conv_dw_bias, conv_ln_g, conv_ln_b, w_conv_out, w_att_out, norm_mem, w_mem_kv, w_mem_out, w_out, norm_mix_post, norm_ffn_pre, w_ffn_in, w_ffn_out, norm_ffn_post, loss_target, m_rel_bias, m_norm_mix_pre, m_w_in, m_b_gate, m_conv_dw, m_conv_dw_bias, m_conv_ln_g, m_conv_ln_b, m_w_conv_out, m_w_att_out, m_norm_mem, m_w_mem_kv, m_w_mem_out, m_w_out, m_norm_mix_post, m_norm_ffn_pre, m_w_ffn_in, m_w_ffn_out, m_norm_ffn_post, v_rel_bias, v_norm_mix_pre, v_w_in, v_b_gate, v_conv_dw, v_conv_dw_bias, v_conv_ln_g, v_conv_ln_b, v_w_conv_out, v_w_att_out, v_norm_mem, v_w_mem_kv, v_w_mem_out, v_w_out, v_norm_mix_post, v_norm_ffn_pre, v_w_ffn_in, v_w_ffn_out, v_norm_ffn_post):
    args = dict(locals())
    w = {n: args[n] for n in TWIN_WEIGHTS}
    m = {n: args["m_" + n] for n in TWIN_WEIGHTS}
    v = {n: args["v_" + n] for n in TWIN_WEIGHTS}
    sm = {n: w[n] for n, _ in SMALL}

    first = (0, "mix")
    keys = [(l, grp) for l in range(DEPTH) for grp in GROUPS]
    shard = {(l, grp): _pack_group(w, l, grp) for l, grp in keys}
    weights = {first: _full_weights(_all_gather(shard[first], "gather_l0_mix"), "mix")}
    settled = weights[first]["w_out"][:1, :1] * 0
    ag = {k: _exchange_start(_after(shard[k], settled), True, f"ag_start_l{k[0]}_{k[1]}") for k in keys if k != first}

    def get_w(l, grp, after):
        if (l, grp) not in weights:
            mine, land = _exchange_wait(ag[l, grp], after, True, f"ag_wait_l{l}_{grp}")
            weights[l, grp] = _full_weights(_own_slot(land, mine), grp)
        return weights[l, grp]

    rs = {}

    def put_g(l, grp, grads):
        rs[l, grp] = _exchange_start(_scatter_layout(grads, grp), False, f"rs_start_l{l}_{grp}")
        return rs[l, grp][4]

    loss, dx, gsmall = _local_step(x[0], mem[0], loss_target[0], sm, get_w, put_g, [st[4] for st in ag.values()])
    loss = lax.psum(loss, ("x", "y", "c"))

    xi, yi, ci = _place()
    me = 4 * xi + 2 * yi + ci
    g_slab = {}
    for k, st in rs.items():
        contrib, land = _exchange_wait(st, dx, False, f"rs_wait_l{k[0]}_{k[1]}")
        own = lax.dynamic_index_in_dim(contrib, me, axis=0, keepdims=False)
        g_slab[k] = _sum_slots(_own_slot(land, own), F32, f"rs_sum_l{k[0]}_{k[1]}")
    g_layers = [{**_unpack_blocks(g_slab[l, "mix"], "mix"), **_unpack_blocks(g_slab[l, "ffn"], "ffn")} for l in range(DEPTH)]
    small_shapes = [s for _, s in SMALL]
    g_small = _unpack(_sum_slots(_all_gather(_pack([gsmall[n] for n, _ in SMALL], F32), "gather_small"), F32, "sum_small"),
                      small_shapes, F32)
    grads = {n: jnp.stack([g_layers[0][n], g_layers[1][n]]) for n, _, _ in BIG}
    grads.update({n: g_small[i] for i, (n, _) in enumerate(SMALL)})

    delta, new_m, new_v = {}, {}, {}
    for n in TWIN_WEIGHTS:
        delta[n], new_m[n], new_v[n] = _adamw(w[n], grads[n], m[n], v[n], f"adamw_{n}")

    return (loss, dx[None], *[grads[n] for n in TWIN_WEIGHTS], *[delta[n] for n in TWIN_WEIGHTS],
            *[new_m[n] for n in TWIN_WEIGHTS], *[new_v[n] for n in TWIN_WEIGHTS])
```

```python
import functools

import numpy as np
import jax
import jax.numpy as jnp
from jax import lax
from jax.experimental import pallas as pl
from jax.experimental.pallas import tpu as pltpu

F32 = jnp.float32
BF16 = jnp.bfloat16

N_DEV = 8
D = 1024
DEPTH = 2
CONV_W = 512
CONV_K = 31
CONV_PAD = 16
ATT_PATTERNS = ((128, 1), (512, 4), (2048, 16))
ATT_RADIUS = 64
HEADS = 4
HEAD_DIM = 64
ATT_GW = HEADS * HEAD_DIM
ATT_W = 3 * ATT_GW
MEM_HEADS = 4
MEM_HD = 128
MEM_W = 512
N_MEM = 256
FFN_H = 2816
NUM_BUCKETS = 32
MAX_DISTANCE = 1024
RMS_EPS = 1e-6
LN_EPS = 1e-5
NEG_INF = -1e30
SEG = (2 * CONV_W, 3 * ATT_W, MEM_W, 3 * D)
ADAM_LR, ADAM_B1, ADAM_B2, ADAM_EPS, ADAM_WD, ADAM_STEP = 0.001, 0.9, 0.999, 1e-08, 0.01, 10

VMEM_LIMIT_V7X = 56 * 1024 * 1024
MESH = pl.DeviceIdType.MESH


def _cp(n_axes):
    return pltpu.CompilerParams(dimension_semantics=("arbitrary",) * n_axes, vmem_limit_bytes=VMEM_LIMIT_V7X)


def _row(v):
    return v.reshape(1, -1)


def _mm(a, b, name, *, ta=False, tb=False, out_dtype=BF16, tm=2048, tn=1024, tk=2048):
    kdim, m = a.shape if ta else a.shape[::-1]
    n, kb = b.shape if tb else b.shape[::-1]
    assert kb == kdim and a.dtype == BF16 and b.dtype == BF16
    tm, tn, tk = min(tm, m), min(tn, n), min(tk, kdim)
    assert m % tm == 0 and n % tn == 0 and kdim % tk == 0, (name, m, n, kdim, tm, tn, tk)
    nk = kdim // tk
    a_spec = pl.BlockSpec((tk, tm), lambda i, j, k: (k, i)) if ta else pl.BlockSpec((tm, tk), lambda i, j, k: (i, k))
    b_spec = pl.BlockSpec((tn, tk), lambda i, j, k: (j, k)) if tb else pl.BlockSpec((tk, tn), lambda i, j, k: (k, j))
    o_spec = pl.BlockSpec((tm, tn), lambda i, j, k: (i, j))
    dims = (((0 if ta else 1,), (1 if tb else 0,)), ((), ()))
    use_scratch = nk > 1 and out_dtype != F32

    def body(*refs):
        a_ref, b_ref, o_ref = refs[:3]
        p = lax.dot_general(a_ref[...], b_ref[...], dims, preferred_element_type=F32)
        if nk == 1:
            o_ref[...] = p.astype(o_ref.dtype)
            return
        k = pl.program_id(2)
        t_ref = refs[3] if use_scratch else o_ref

        @pl.when(k == 0)
        def _():
            t_ref[...] = p

        @pl.when(k > 0)
        def _():
            t_ref[...] += p

        if use_scratch:
            @pl.when(k == nk - 1)
            def _():
                o_ref[...] = t_ref[...].astype(o_ref.dtype)

    return pl.pallas_call(
        body, name=name, grid=(m // tm, n // tn, nk), in_specs=[a_spec, b_spec], out_specs=o_spec,
        out_shape=jax.ShapeDtypeStruct((m, n), out_dtype),
        scratch_shapes=[pltpu.VMEM((tm, tn), F32)] if use_scratch else [],
        compiler_params=_cp(3),
    )(a, b)


EPILOGUE_COLS = 512
EPILOGUE_ROWS = 512


def _chunks(size, step):
    return [(c, min(c + step, size)) for c in range(0, size, step)]


def _rms_bwd_val(v, g, dy):
    r = lax.rsqrt(jnp.mean(v * v, axis=-1, keepdims=True) + RMS_EPS)
    vhat = v * r
    dvh = dy * g
    dv = r * (dvh - vhat * jnp.mean(dvh * vhat, axis=-1, keepdims=True))
    return dv, jnp.sum(dy * vhat, axis=0, keepdims=True)


def _mm_sum(pairs, name, *, norm=None, tm=1024):
    m, n = pairs[0][0].shape[0], pairs[0][1].shape[1]
    tm = min(tm, m)
    starts, specs, total = [], [], 0
    for a, b, tk in pairs:
        assert a.shape == (m, b.shape[0]) and b.shape[1] == n and a.shape[1] % tk == 0 and a.dtype == b.dtype == BF16
        nk = a.shape[1] // tk
        starts.append((total, nk))
        step = functools.partial(lambda k, s0, nk_: jnp.clip(k - s0, 0, nk_ - 1), s0=total, nk_=nk)
        specs.append(pl.BlockSpec((tm, tk), functools.partial(lambda i, k, st: (i, st(k)), st=step)))
        specs.append(pl.BlockSpec((tk, n), functools.partial(lambda i, k, st: (st(k), 0), st=step)))
        total += nk
    np_ = 2 * len(pairs)
    row = pl.BlockSpec((tm, n), lambda i, k: (i, 0))

    def accumulate(refs, acc, k, upto=total):
        for p, (s0, nk) in enumerate(starts):
            @pl.when((k >= s0) & (k < min(s0 + nk, upto)))
            def _(p=p, s0=s0):
                part = jnp.dot(refs[2 * p][...], refs[2 * p + 1][...], preferred_element_type=F32)
                if s0 == 0:
                    @pl.when(k == 0)
                    def _():
                        acc[...] = part

                    @pl.when(k > 0)
                    def _():
                        acc[...] += part
                else:
                    acc[...] += part

    operands = [op for a, b, _ in pairs for op in (a, b)]
    if norm is None:
        def plain(*refs):
            accumulate(refs, refs[np_], pl.program_id(1))

        return pl.pallas_call(
            plain, name=name, grid=(m // tm, total), in_specs=specs, out_specs=row,
            out_shape=jax.ShapeDtypeStruct((m, n), F32), compiler_params=_cp(2),
        )(*operands)

    v, g, dres = norm

    def body(*refs):
        v_ref, g_ref, dres_ref, dv_ref, dg_ref, acc = refs[np_:np_ + 6]
        i, k = pl.program_id(0), pl.program_id(1)
        accumulate(refs, acc, k, upto=total - 1)

        @pl.when((i == 0) & (k == 0))
        def _():
            dg_ref[...] = jnp.zeros_like(dg_ref)

        @pl.when(k == total - 1)
        def _():
            a_ref, b_ref = refs[np_ - 2], refs[np_ - 1]
            for r0, r1 in _chunks(tm, EPILOGUE_ROWS):
                dy = jnp.dot(a_ref[r0:r1, :], b_ref[...], preferred_element_type=F32)
                if total > 1:
                    dy = dy + acc[r0:r1, :]
                dv, dg = _rms_bwd_val(v_ref[r0:r1, :], g_ref[...], dy)
                dv_ref[r0:r1, :] = dv + dres_ref[r0:r1, :]
                dg_ref[0:1, :] += dg

    once = pl.BlockSpec((tm, n), lambda i, k: (i, 0), pipeline_mode=pl.Buffered(1))
    dv, dg = pl.pallas_call(
        body, name=name, grid=(m // tm, total),
        in_specs=specs + [once, pl.BlockSpec((1, n), lambda i, k: (0, 0)), once],
        out_specs=[row, pl.BlockSpec((8, n), lambda i, k: (0, 0))],
        out_shape=[jax.ShapeDtypeStruct((m, n), F32), jax.ShapeDtypeStruct((8, n), F32)],
        scratch_shapes=[pltpu.VMEM((tm, n), F32)], compiler_params=_cp(2),
    )(*operands, v, _row(g), dres)
    return dv, dg[0]


def _mm_swiglu_fwd(h, w, name, *, tm=1024, tn=1408):
    m, kdim = h.shape
    hid = w.shape[0] // 2
    tm = min(tm, m)
    nj = hid // tn
    dims = (((1,), (1,)), ((), ()))

    def body(h_ref, wg_ref, wu_ref, g_ref, u_ref, a_ref):
        hv = h_ref[...]
        for c0, c1 in _chunks(tn, EPILOGUE_COLS):
            g = lax.dot_general(hv, wg_ref[c0:c1, :], dims, preferred_element_type=F32)
            u = lax.dot_general(hv, wu_ref[c0:c1, :], dims, preferred_element_type=F32)
            g_ref[:, c0:c1] = g.astype(BF16)
            u_ref[:, c0:c1] = u.astype(BF16)
            a_ref[:, c0:c1] = (g * _sigmoid(g) * u).astype(BF16)

    out = pl.BlockSpec((tm, tn), lambda i, j: (i, j))
    return pl.pallas_call(
        body, name=name, grid=(m // tm, nj),
        in_specs=[pl.BlockSpec((tm, kdim), lambda i, j: (i, 0)), pl.BlockSpec((tn, kdim), lambda i, j: (j, 0)),
                  pl.BlockSpec((tn, kdim), lambda i, j: (j + nj, 0))],
        out_specs=[out, out, out], out_shape=[jax.ShapeDtypeStruct((m, hid), BF16)] * 3, compiler_params=_cp(2),
    )(h, w, w)


def _mm_swiglu_bwd(df, w, g, u, name, *, tm=1024, tn=1408):
    m, kdim = df.shape
    hid = w.shape[0]
    tm = min(tm, m)

    def body(df_ref, w_ref, g_ref, u_ref, dg_ref, du_ref):
        dfv = df_ref[...]
        for c0, c1 in _chunks(tn, EPILOGUE_COLS):
            da = lax.dot_general(dfv, w_ref[c0:c1, :], (((1,), (1,)), ((), ())), preferred_element_type=F32)
            gg, uu = g_ref[:, c0:c1].astype(F32), u_ref[:, c0:c1].astype(F32)
            sg = _sigmoid(gg)
            dg_ref[:, c0:c1] = (da * uu * (sg * (1.0 + gg * (1.0 - sg)))).astype(BF16)
            du_ref[:, c0:c1] = (da * gg * sg).astype(BF16)

    blk = pl.BlockSpec((tm, tn), lambda i, j: (i, j))
    return pl.pallas_call(
        body, name=name, grid=(m // tm, hid // tn),
        in_specs=[pl.BlockSpec((tm, kdim), lambda i, j: (i, 0)), pl.BlockSpec((tn, kdim), lambda i, j: (j, 0)), blk, blk],
        out_specs=[blk, blk], out_shape=[jax.ShapeDtypeStruct((m, hid), BF16)] * 2, compiler_params=_cp(2),
    )(df, w, g, u)


def _mm_norm_res(a, b, xres, g, g_next, name, *, tm=1024):
    m, kdim = a.shape
    n = b.shape[1]
    tm = min(tm, m)
    two = g_next is not None

    def body(*refs):
        a_ref, b_ref, x_ref, g_ref = refs[:4]
        y_ref, xn_ref = refs[4 + two], refs[5 + two]
        for r0, r1 in _chunks(tm, EPILOGUE_ROWS):
            y = jnp.dot(a_ref[r0:r1, :], b_ref[...], preferred_element_type=F32)
            y_ref[r0:r1, :] = y.astype(BF16)
            xn = x_ref[r0:r1, :] + _rms_val(y, g_ref[...])
            xn_ref[r0:r1, :] = xn
            if two:
                refs[7][r0:r1, :] = _rms_val(xn, refs[4][...]).astype(BF16)

    row = pl.BlockSpec((tm, n), lambda i: (i, 0))
    vec = pl.BlockSpec((1, n), lambda i: (0, 0))
    outs = pl.pallas_call(
        body, name=name, grid=(m // tm,),
        in_specs=[pl.BlockSpec((tm, kdim), lambda i: (i, 0)), pl.BlockSpec((kdim, n), lambda i: (0, 0)), row, vec] + ([vec] if two else []),
        out_specs=[row, row] + ([row] if two else []),
        out_shape=[jax.ShapeDtypeStruct((m, n), BF16), jax.ShapeDtypeStruct((m, n), F32)]
        + ([jax.ShapeDtypeStruct((m, n), BF16)] if two else []),
        compiler_params=_cp(1),
    )(*((a, b, xres, _row(g)) + ((_row(g_next),) if two else ())))
    return (outs[0], outs[1], outs[2]) if two else (outs[0], outs[1], None)


def _rms_val(v, g):
    return v * lax.rsqrt(jnp.mean(v * v, axis=-1, keepdims=True) + RMS_EPS) * g


def _norm_plain(v, g, name, tr=512):
    t, d = v.shape
    tr = min(tr, t)

    def body(v_ref, g_ref, o_ref):
        o_ref[...] = _rms_val(v_ref[...], g_ref[...]).astype(BF16)

    return pl.pallas_call(
        body, name=name, grid=(t // tr,),
        in_specs=[pl.BlockSpec((tr, d), lambda i: (i, 0)), pl.BlockSpec((1, d), lambda i: (0, 0))],
        out_specs=pl.BlockSpec((tr, d), lambda i: (i, 0)),
        out_shape=jax.ShapeDtypeStruct((t, d), BF16), compiler_params=_cp(1),
    )(v, _row(g))


def _norm_bwd(v, g, dout, dres, out_dtype, name, tr=512):
    t, d = v.shape
    tr = min(tr, t)
    nt = t // tr
    has_res = dres is not None

    def body(*refs):
        v_ref, g_ref, do_ref = refs[:3]
        dv_ref, dg_ref = refs[3 + has_res], refs[4 + has_res]
        i = pl.program_id(0)
        vv = v_ref[...].astype(F32)
        dy = do_ref[...].astype(F32)
        r = lax.rsqrt(jnp.mean(vv * vv, axis=-1, keepdims=True) + RMS_EPS)
        vhat = vv * r
        dvh = dy * g_ref[...]
        dv = r * (dvh - vhat * jnp.mean(dvh * vhat, axis=-1, keepdims=True))
        if has_res:
            dv = dv + refs[3][...]
        dv_ref[...] = dv.astype(dv_ref.dtype)
        part = jnp.sum(dy * vhat, axis=0, keepdims=True)

        @pl.when(i == 0)
        def _():
            dg_ref[...] = jnp.zeros_like(dg_ref)

        dg_ref[0:1, :] += part

    row = pl.BlockSpec((tr, d), lambda i: (i, 0))
    dv, dg = pl.pallas_call(
        body, name=name, grid=(nt,),
        in_specs=[row, pl.BlockSpec((1, d), lambda i: (0, 0)), row] + ([row] if has_res else []),
        out_specs=[row, pl.BlockSpec((8, d), lambda i: (0, 0))],
        out_shape=[jax.ShapeDtypeStruct((t, d), out_dtype), jax.ShapeDtypeStruct((8, d), F32)],
        compiler_params=_cp(1),
    )(*((v, _row(g), dout) + ((dres,) if has_res else ())))
    return dv, dg[0]


def _loss_head(y, target, name, tr=512):
    t, d = y.shape
    tr = min(tr, t)
    nt = t // tr

    def body(y_ref, t_ref, dy_ref, l_ref):
        i = pl.program_id(0)
        err = y_ref[...] - t_ref[...]
        dy_ref[...] = err * (1.0 / d)

        @pl.when(i == 0)
        def _():
            l_ref[...] = jnp.zeros_like(l_ref)

        l_ref[...] += jnp.sum(err * err) * (0.5 / d)

    row = pl.BlockSpec((tr, d), lambda i: (i, 0))
    dy, l = pl.pallas_call(
        body, name=name, grid=(nt,), in_specs=[row, row],
        out_specs=[row, pl.BlockSpec((8, 128), lambda i: (0, 0))],
        out_shape=[jax.ShapeDtypeStruct((t, d), F32), jax.ShapeDtypeStruct((8, 128), F32)],
        compiler_params=_cp(1),
    )(y, target)
    return l[0, 0], dy


def _sigmoid(v):
    return 1.0 / (1.0 + jnp.exp(-v))


def _halo_specs(tq, width, nt, halo):
    per = tq // halo
    last = nt * per - 1
    main = pl.BlockSpec((tq, width), lambda i: (i, 0))
    prev = pl.BlockSpec((halo, width), lambda i: (jnp.maximum(i * per - 1, 0), 0))
    nxt = pl.BlockSpec((halo, width), lambda i: (jnp.minimum((i + 1) * per, last), 0))
    return prev, main, nxt


def _glu_window(zp, zm, zn, i, nt):
    def glu(z):
        z = z.astype(F32)
        return z[:, :CONV_W] * _sigmoid(z[:, CONV_W:])

    up = jnp.where(i > 0, glu(zp), 0.0)
    un = jnp.where(i < nt - 1, glu(zn), 0.0)
    return jnp.concatenate([up, glu(zm), un], axis=0)


def _shifted(win, shift, rows):
    if shift == 0:
        return win[:rows]
    return pltpu.roll(win, win.shape[0] - shift, 0)[:rows]


def _conv_fwd(zc, w, b, ln_g, ln_b, name, tq=512):
    t = zc.shape[0]
    tq = min(tq, t)
    nt = t // tq

    def body(zp_ref, zm_ref, zn_ref, w_ref, b_ref, g_ref, bb_ref, y_ref, c_ref):
        i = pl.program_id(0)
        win = _glu_window(zp_ref[...], zm_ref[...], zn_ref[...], i, nt)
        wv = w_ref[...]
        y = jnp.zeros((tq, CONV_W), F32) + b_ref[...]
        for k in range(CONV_K):
            y = y + _shifted(win, k + 1, tq) * wv[k:k + 1, :]
        y_ref[...] = y
        mu = jnp.mean(y, axis=-1, keepdims=True)
        yc = y - mu
        ln = yc * lax.rsqrt(jnp.mean(yc * yc, axis=-1, keepdims=True) + LN_EPS) * g_ref[...] + bb_ref[...]
        c_ref[...] = (ln * _sigmoid(ln)).astype(BF16)

    vec = pl.BlockSpec((1, CONV_W), lambda i: (0, 0))
    out = pl.BlockSpec((tq, CONV_W), lambda i: (i, 0))
    return pl.pallas_call(
        body, name=name, grid=(nt,),
        in_specs=[*_halo_specs(tq, 2 * CONV_W, nt, CONV_PAD), pl.BlockSpec((32, CONV_W), lambda i: (0, 0)), vec, vec, vec],
        out_specs=[out, out],
        out_shape=[jax.ShapeDtypeStruct((t, CONV_W), F32), jax.ShapeDtypeStruct((t, CONV_W), BF16)],
        compiler_params=_cp(1),
    )(zc, zc, zc, w, _row(b), _row(ln_g), _row(ln_b))


def _conv_bwd(zc, y, dc, w, ln_g, ln_b, name, tq=256):
    t = zc.shape[0]
    tq = min(tq, t)
    nt = t // tq
    rows = tq + 2 * CONV_PAD

    def body(zp_ref, zm_ref, zn_ref, yp_ref, ym_ref, yn_ref, dp_ref, dm_ref, dn_ref, w_ref, g_ref, bb_ref,
             dz_ref, dw_ref, db_ref, dg_ref, dbb_ref):
        i = pl.program_id(0)
        uwin = _glu_window(zp_ref[...], zm_ref[...], zn_ref[...], i, nt)
        ywin = jnp.concatenate([yp_ref[...], ym_ref[...], yn_ref[...]], axis=0)
        dcw = jnp.concatenate([dp_ref[...], dm_ref[...], dn_ref[...]], axis=0).astype(F32)
        mu = jnp.mean(ywin, axis=-1, keepdims=True)
        yc = ywin - mu
        rstd = lax.rsqrt(jnp.mean(yc * yc, axis=-1, keepdims=True) + LN_EPS)
        yhat = yc * rstd
        ln = yhat * g_ref[...] + bb_ref[...]
        sg = _sigmoid(ln)
        dl = dcw * (sg * (1.0 + ln * (1.0 - sg)))
        ridx = lax.broadcasted_iota(jnp.int32, (rows, 1), 0)
        inside = ((ridx >= CONV_PAD) | (i > 0)) & ((ridx < tq + CONV_PAD) | (i < nt - 1))
        dl = jnp.where(inside, dl, 0.0)
        dyh = dl * g_ref[...]
        dy = rstd * (dyh - jnp.mean(dyh, axis=-1, keepdims=True) - yhat * jnp.mean(dyh * yhat, axis=-1, keepdims=True))
        dy = jnp.where(inside, dy, 0.0)
        main = slice(CONV_PAD, CONV_PAD + tq)
        dlm, yhm, dym = dl[main], yhat[main], dy[main]

        @pl.when(i == 0)
        def _():
            dw_ref[...] = jnp.zeros_like(dw_ref)
            db_ref[...] = jnp.zeros_like(db_ref)
            dg_ref[...] = jnp.zeros_like(dg_ref)
            dbb_ref[...] = jnp.zeros_like(dbb_ref)

        dg_ref[0:1, :] += jnp.sum(dlm * yhm, axis=0, keepdims=True)
        dbb_ref[0:1, :] += jnp.sum(dlm, axis=0, keepdims=True)
        db_ref[0:1, :] += jnp.sum(dym, axis=0, keepdims=True)
        wv = w_ref[...]
        du = jnp.zeros((tq, CONV_W), F32)
        for k in range(CONV_K):
            du = du + _shifted(dy, 2 * CONV_PAD - 1 - k, tq) * wv[k:k + 1, :]
            dw_ref[k:k + 1, :] += jnp.sum(dym * _shifted(uwin, k + 1, tq), axis=0, keepdims=True)
        zm = zm_ref[...].astype(F32)
        a, gt = zm[:, :CONV_W], zm[:, CONV_W:]
        sgt = _sigmoid(gt)
        dz_ref[:, :CONV_W] = (du * sgt).astype(BF16)
        dz_ref[:, CONV_W:] = (du * a * sgt * (1.0 - sgt)).astype(BF16)

    vec = pl.BlockSpec((1, CONV_W), lambda i: (0, 0))
    acc = pl.BlockSpec((8, CONV_W), lambda i: (0, 0))
    dz, dw, db, dg, dbb = pl.pallas_call(
        body, name=name, grid=(nt,),
        in_specs=[*_halo_specs(tq, 2 * CONV_W, nt, CONV_PAD), *_halo_specs(tq, CONV_W, nt, CONV_PAD),
                  *_halo_specs(tq, CONV_W, nt, CONV_PAD), pl.BlockSpec((32, CONV_W), lambda i: (0, 0)), vec, vec],
        out_specs=[pl.BlockSpec((tq, 2 * CONV_W), lambda i: (i, 0)), pl.BlockSpec((32, CONV_W), lambda i: (0, 0)), acc, acc, acc],
        out_shape=[jax.ShapeDtypeStruct((t, 2 * CONV_W), BF16), jax.ShapeDtypeStruct((32, CONV_W), F32),
                   jax.ShapeDtypeStruct((8, CONV_W), F32), jax.ShapeDtypeStruct((8, CONV_W), F32),
                   jax.ShapeDtypeStruct((8, CONV_W), F32)],
        compiler_params=_cp(1),
    )(zc, zc, zc, y, y, y, dc, dc, dc, w, _row(ln_g), _row(ln_b))
    return dz, dw, db[0], dg[0], dbb[0]


def _att_tiles(seq_len):
    tq = min(256, seq_len)
    return tq, min(128, tq)


def _t5_bucket_np(rel):
    nb = NUM_BUCKETS // 2
    max_exact = nb // 2
    ret = np.where(rel > 0, nb, 0)
    n = np.abs(rel)
    nf = np.maximum(n, 1).astype(np.float32)
    large = max_exact + (np.log(nf / np.float32(max_exact)) / np.float32(np.log(MAX_DISTANCE / max_exact))
                         * np.float32(nb - max_exact)).astype(np.int32)
    large = np.minimum(large, nb - 1)
    return ret + np.where(n < max_exact, n, large)


def _bucket_index(sb, dilation):
    off = np.arange(sb + 2 * ATT_RADIUS)[None, :] - ATT_RADIUS - np.arange(sb)[:, None]
    idx = _t5_bucket_np(off * dilation).astype(np.int32)
    return np.where(np.abs(off) <= ATT_RADIUS, idx, -1).astype(np.int32)


def _bias_table(tab, idx, name):
    sb, w = idx.shape

    def body(tab_ref, idx_ref, o_ref):
        ix = idx_ref[...]
        for h in range(HEADS):
            acc = jnp.full((sb, w), NEG_INF, F32)
            for b in range(NUM_BUCKETS):
                acc = jnp.where(ix == b, tab_ref[b, h], acc)
            o_ref[h] = acc

    return pl.pallas_call(
        body, name=name, out_shape=jax.ShapeDtypeStruct((HEADS, sb, w), F32),
        in_specs=[pl.BlockSpec(memory_space=pltpu.SMEM), pl.BlockSpec(memory_space=pltpu.VMEM)],
        out_specs=pl.BlockSpec(memory_space=pltpu.VMEM),
    )(tab, idx)


def _qkv_specs(tq, per, last_blk, cols, tile):
    def main(col):
        return pl.BlockSpec((tq, ATT_GW), lambda *g: (tile(*g), col))

    def prev(col):
        return pl.BlockSpec((ATT_RADIUS, ATT_GW), lambda *g: (jnp.maximum(tile(*g) * per - 1, 0), col))

    def nxt(col):
        return pl.BlockSpec((ATT_RADIUS, ATT_GW), lambda *g: (jnp.minimum((tile(*g) + 1) * per, last_blk), col))

    cq, ck, cv = cols
    return [main(cq), prev(ck), main(ck), nxt(ck), prev(cv), main(cv), nxt(cv)]


def _att_fwd(q, k, v, bias, seq_len, name, cols=(0, 0, 0)):
    t = q.shape[0]
    tq, sb = _att_tiles(seq_len)
    nbl = seq_len // tq
    nt = t // tq
    per = tq // ATT_RADIUS
    w = sb + 2 * ATT_RADIUS
    last_blk = t // ATT_RADIUS - 1

    def body(q_ref, kp_ref, kc_ref, kn_ref, vp_ref, vc_ref, vn_ref, b_ref, o_ref, l_ref):
        n = pl.program_id(0) % nbl
        kw = jnp.concatenate([kp_ref[...], kc_ref[...], kn_ref[...]], axis=0)
        vw = jnp.concatenate([vp_ref[...], vc_ref[...], vn_ref[...]], axis=0)
        kpos = n * tq - ATT_RADIUS + lax.broadcasted_iota(jnp.int32, (1, tq + 2 * ATT_RADIUS), 1)
        valid = (kpos >= 0) & (kpos < seq_len)
        for j in range(tq // sb):
            qj = q_ref[j * sb:(j + 1) * sb, :]
            kj, vj, okj = kw[j * sb:j * sb + w], vw[j * sb:j * sb + w], valid[:, j * sb:j * sb + w]
            outs, lses = [], []
            for h in range(HEADS):
                hs = slice(h * HEAD_DIM, (h + 1) * HEAD_DIM)
                s = lax.dot_general(qj[:, hs], kj[:, hs], NT, preferred_element_type=F32)
                s = jnp.where(okj, s * (HEAD_DIM ** -0.5) + b_ref[h], NEG_INF)
                m = jnp.max(s, axis=-1, keepdims=True)
                e = jnp.exp(s - m)
                den = jnp.sum(e, axis=-1, keepdims=True)
                outs.append(jnp.dot(e.astype(BF16), vj[:, hs], preferred_element_type=F32) / den)
                lses.append(jnp.broadcast_to(m + jnp.log(den), (sb, HEAD_DIM)))
            o_ref[j * sb:(j + 1) * sb, :] = jnp.concatenate(outs, axis=1).astype(BF16)
            l_ref[j * sb:(j + 1) * sb, :] = jnp.concatenate(lses, axis=1)

    main = pl.BlockSpec((tq, ATT_GW), lambda i: (i, 0))
    return pl.pallas_call(
        body, name=name, grid=(nt,),
        in_specs=[*_qkv_specs(tq, per, last_blk, cols, lambda i: i), pl.BlockSpec((HEADS, sb, w), lambda i: (0, 0, 0))],
        out_specs=[main, main],
        out_shape=[jax.ShapeDtypeStruct((t, ATT_GW), BF16), jax.ShapeDtypeStruct((t, ATT_GW), F32)],
        compiler_params=_cp(1),
    )(q, k, k, k, v, v, v, bias)


PERM_ROWS = 2048
LANES = 128


def _spec_rm(d, tr, width, col=0):
    return pl.BlockSpec((d, tr // d, width), lambda i: (0, i, col))


def _rm_view(a, d):
    return a.reshape(d, a.shape[0] // d, a.shape[1])


def _gather_residues(scr, val, d, out_ref, col0):
    tr, w = val.shape
    for c in range(w // LANES):
        scr[...] = val[:, c * LANES:(c + 1) * LANES].astype(F32)
        for r in range(d):
            out_ref[r, :, col0 + c * LANES:col0 + (c + 1) * LANES] = scr[pl.ds(r, tr // d, stride=d), :].astype(out_ref.dtype)


def _scatter_residues(scr, ref, d):
    w = ref.shape[2]
    cols = []
    for c in range(w // LANES):
        for r in range(d):
            scr[pl.ds(r, scr.shape[0] // d, stride=d), :] = ref[r, :, c * LANES:(c + 1) * LANES].astype(F32)
        cols.append(scr[...])
    return jnp.concatenate(cols, axis=1)


def _to_residue_major(srcs, d, name, tr=PERM_ROWS):
    t = srcs[0][0].shape[0]
    tr = min(tr, t)
    n = len(srcs)

    def body(*refs):
        scr = refs[2 * n]
        for k in range(n):
            _gather_residues(scr, refs[k][...], d, refs[n + k], 0)

    outs = pl.pallas_call(
        body, name=name, grid=(t // tr,),
        in_specs=[pl.BlockSpec((tr, ATT_GW), functools.partial(lambda i, col: (i, col), col=col)) for _, col in srcs],
        out_specs=[_spec_rm(d, tr, ATT_GW)] * n,
        out_shape=[jax.ShapeDtypeStruct((d, t // d, ATT_GW), a.dtype) for a, _ in srcs],
        scratch_shapes=[pltpu.VMEM((tr, LANES), F32)], compiler_params=_cp(1),
    )(*[a for a, _ in srcs])
    return [o.reshape(t, ATT_GW) for o in outs]


def _att_combine(outs, lses, name, tr=PERM_ROWS):
    t = outs[0].shape[0]
    tr = min(tr, t)
    dils = [d for _, d in ATT_PATTERNS]

    def body(o0, o1, o2, l0, l1, l2, a_ref, lse_ref, scr):
        os_, ls_ = [], []
        for o_ref, l_ref, d in zip((o0, o1, o2), (l0, l1, l2), dils):
            if d == 1:
                os_.append(o_ref[...].astype(F32))
                ls_.append(l_ref[...])
            else:
                os_.append(_scatter_residues(scr, o_ref, d))
                ls_.append(_scatter_residues(scr, l_ref, d))
        la, lb, lc = ls_
        m = jnp.maximum(jnp.maximum(la, lb), lc)
        ea, eb, ec = jnp.exp(la - m), jnp.exp(lb - m), jnp.exp(lc - m)
        den = ea + eb + ec
        a_ref[...] = ((ea * os_[0] + eb * os_[1] + ec * os_[2]) / den).astype(BF16)
        lse_ref[...] = m + jnp.log(den)

    row = pl.BlockSpec((tr, ATT_GW), lambda i: (i, 0))
    specs = [row if d == 1 else _spec_rm(d, tr, ATT_GW) for d in dils]
    views = lambda arrs: [a if d == 1 else _rm_view(a, d) for a, d in zip(arrs, dils)]
    return pl.pallas_call(
        body, name=name, grid=(t // tr,), in_specs=specs * 2, out_specs=[row, row],
        out_shape=[jax.ShapeDtypeStruct((t, ATT_GW), BF16), jax.ShapeDtypeStruct((t, ATT_GW), F32)],
        scratch_shapes=[pltpu.VMEM((tr, LANES), F32)], compiler_params=_cp(1),
    )(*views(outs), *views(lses))


def _assemble_dza(dqkv, name, tr=PERM_ROWS):
    t = dqkv[0][0].shape[0]
    tr = min(tr, t)
    dils = [d for _, d in ATT_PATTERNS]

    def body(*refs):
        o_ref, scr = refs[9], refs[10]
        for g, d in enumerate(dils):
            for c in range(3):
                ref = refs[g * 3 + c]
                val = ref[...] if d == 1 else _scatter_residues(scr, ref, d).astype(BF16)
                o_ref[:, (c * 3 + g) * ATT_GW:(c * 3 + g + 1) * ATT_GW] = val

    row = pl.BlockSpec((tr, ATT_GW), lambda i: (i, 0))
    specs = [row if d == 1 else _spec_rm(d, tr, ATT_GW) for d in dils for _ in range(3)]
    args = [a if d == 1 else _rm_view(a, d) for trio, d in zip(dqkv, dils) for a in trio]
    return pl.pallas_call(
        body, name=name, grid=(t // tr,), in_specs=specs, out_specs=pl.BlockSpec((tr, 3 * ATT_W), lambda i: (i, 0)),
        out_shape=jax.ShapeDtypeStruct((t, 3 * ATT_W), BF16),
        scratch_shapes=[pltpu.VMEM((tr, LANES), F32)], compiler_params=_cp(1),
    )(*args)


def _att_delta(o, do, name, tr=1024):
    t = o.shape[0]
    tr = min(tr, t)

    def body(o_ref, do_ref, d_ref):
        dd = o_ref[...].astype(F32) * do_ref[...].astype(F32)
        d_ref[...] = jnp.concatenate(
            [jnp.broadcast_to(jnp.sum(dd[:, h * HEAD_DIM:(h + 1) * HEAD_DIM], axis=-1, keepdims=True), (tr, HEAD_DIM))
             for h in range(HEADS)], axis=1)

    row = pl.BlockSpec((tr, ATT_GW), lambda i: (i, 0))
    return pl.pallas_call(
        body, name=name, grid=(t // tr,), in_specs=[row, row], out_specs=row,
        out_shape=jax.ShapeDtypeStruct((t, ATT_GW), F32), compiler_params=_cp(1),
    )(o, do)


def _att_bwd(q, k, v, bias, delta, do, lse, seq_len, name, cols=(0, 0, 0)):
    t = q.shape[0]
    tq, sb = _att_tiles(seq_len)
    nbl = seq_len // tq
    n_res = t // seq_len
    per = tq // ATT_RADIUS
    w = sb + 2 * ATT_RADIUS
    last_blk = t // ATT_RADIUS - 1
    acc_cols = 2 * tq - sb + w
    scale = HEAD_DIM ** -0.5

    def body(q_ref, kp_ref, kc_ref, kn_ref, vp_ref, vc_ref, vn_ref, b_ref, dl_ref, do_ref, l_ref,
             dq_ref, dk_ref, dv_ref, db_ref, ak_ref, av_ref):
        r, n = pl.program_id(0), pl.program_id(1)

        @pl.when((r == 0) & (n == 0))
        def _():
            db_ref[...] = jnp.zeros_like(db_ref)

        @pl.when(n == 0)
        def _():
            ak_ref[...] = jnp.zeros_like(ak_ref)
            av_ref[...] = jnp.zeros_like(av_ref)

        @pl.when(n < nbl)
        def _():
            kw = jnp.concatenate([kp_ref[...], kc_ref[...], kn_ref[...]], axis=0)
            vw = jnp.concatenate([vp_ref[...], vc_ref[...], vn_ref[...]], axis=0)
            kpos = n * tq - ATT_RADIUS + lax.broadcasted_iota(jnp.int32, (1, tq + 2 * ATT_RADIUS), 1)
            valid = (kpos >= 0) & (kpos < seq_len)
            for j in range(tq // sb):
                rows = slice(j * sb, (j + 1) * sb)
                qj, doj = q_ref[rows, :], do_ref[rows, :]
                dlj = dl_ref[rows, :]
                lj = l_ref[rows, :]
                kj, vj, okj = kw[j * sb:j * sb + w], vw[j * sb:j * sb + w], valid[:, j * sb:j * sb + w]
                heads = [slice(h * HEAD_DIM, (h + 1) * HEAD_DIM) for h in range(HEADS)]
                ss = [lax.dot_general(qj[:, hs], kj[:, hs], NT, preferred_element_type=F32) for hs in heads]
                dps = [lax.dot_general(doj[:, hs], vj[:, hs], NT, preferred_element_type=F32) for hs in heads]
                ps, dss, dsbs = [], [], []
                for h, hs in enumerate(heads):
                    s = jnp.where(okj, ss[h] * scale + b_ref[h], NEG_INF)
                    p = jnp.exp(s - lj[:, h * HEAD_DIM:h * HEAD_DIM + 1])
                    ds = p * (dps[h] - dlj[:, h * HEAD_DIM:h * HEAD_DIM + 1])
                    ps.append(p.astype(BF16))
                    dss.append(ds)
                    dsbs.append(ds.astype(BF16))
                dqs = [jnp.dot(dsbs[h], kj[:, hs], preferred_element_type=F32) * scale for h, hs in enumerate(heads)]
                dks = [lax.dot_general(qj[:, hs], dsbs[h], (((0,), (0,)), ((), ())), preferred_element_type=F32) * scale
                       for h, hs in enumerate(heads)]
                dvs = [lax.dot_general(doj[:, hs], ps[h], (((0,), (0,)), ((), ())), preferred_element_type=F32)
                       for h, hs in enumerate(heads)]
                dq_ref[rows, :] = jnp.concatenate(dqs, axis=1).astype(BF16)
                c0 = tq + j * sb
                ak_ref[:, c0:c0 + w] += jnp.concatenate(dks, axis=0)
                av_ref[:, c0:c0 + w] += jnp.concatenate(dvs, axis=0)
                for h in range(HEADS):
                    db_ref[h] += dss[h]

        dk_ref[...] = ak_ref[:, ATT_RADIUS:ATT_RADIUS + tq].T.astype(BF16)
        dv_ref[...] = av_ref[:, ATT_RADIUS:ATT_RADIUS + tq].T.astype(BF16)
        keep = acc_cols - tq
        nk, nv = ak_ref[:, tq:acc_cols], av_ref[:, tq:acc_cols]
        ak_ref[:, 0:keep] = nk
        av_ref[:, 0:keep] = nv
        ak_ref[:, keep:acc_cols] = jnp.zeros((ATT_GW, tq), F32)
        av_ref[:, keep:acc_cols] = jnp.zeros((ATT_GW, tq), F32)

    def tile(r, n):
        return r * nbl + jnp.minimum(n, nbl - 1)

    main = pl.BlockSpec((tq, ATT_GW), lambda r, n: (tile(r, n), 0))
    lag = pl.BlockSpec((tq, ATT_GW), lambda r, n: (r * nbl + jnp.maximum(n - 1, 0), 0))
    bspec = pl.BlockSpec((HEADS, sb, w), lambda r, n: (0, 0, 0))
    return pl.pallas_call(
        body, name=name, grid=(n_res, nbl + 1),
        in_specs=[*_qkv_specs(tq, per, last_blk, cols, tile), bspec, main, main, main],
        out_specs=[main, lag, lag, bspec],
        out_shape=[jax.ShapeDtypeStruct((t, ATT_GW), BF16)] * 3 + [jax.ShapeDtypeStruct((HEADS, sb, w), F32)],
        scratch_shapes=[pltpu.VMEM((ATT_GW, acc_cols), F32), pltpu.VMEM((ATT_GW, acc_cols), F32)],
        compiler_params=_cp(2),
    )(q, k, k, k, v, v, v, bias, delta, do, lse)


def _bias_grad(db, idx, name):
    _, sb, w = db.shape

    def body(db_ref, idx_ref, o_ref):
        ix = idx_ref[...]
        lane = lax.broadcasted_iota(jnp.int32, (1, 128), 1)
        rows = []
        for h in range(HEADS):
            d = db_ref[h]
            acc = jnp.zeros((1, 128), F32)
            for b in range(NUM_BUCKETS):
                acc = acc + jnp.where(lane == b, jnp.sum(jnp.where(ix == b, d, 0.0)), 0.0)
            rows.append(acc)
        o_ref[...] = jnp.concatenate(rows + [jnp.zeros((8 - HEADS, 128), F32)], axis=0)

    return pl.pallas_call(
        body, name=name, out_shape=jax.ShapeDtypeStruct((8, 128), F32),
        in_specs=[pl.BlockSpec(memory_space=pltpu.VMEM), pl.BlockSpec(memory_space=pltpu.VMEM)],
        out_specs=pl.BlockSpec(memory_space=pltpu.VMEM),
    )(db, idx)


def _mem_fwd(zq, kv, name, tq=512):
    t = zq.shape[0]
    tq = min(tq, t)
    scale = MEM_HD ** -0.5

    def body(q_ref, kv_ref, o_ref):
        outs = []
        for h in range(MEM_HEADS):
            hs = slice(h * MEM_HD, (h + 1) * MEM_HD)
            kh = kv_ref[:, h * MEM_HD:(h + 1) * MEM_HD]
            vh = kv_ref[:, MEM_W + h * MEM_HD:MEM_W + (h + 1) * MEM_HD]
            s = lax.dot_general(q_ref[:, hs], kh, (((1,), (1,)), ((), ())), preferred_element_type=F32) * scale
            e = jnp.exp(s - jnp.max(s, axis=-1, keepdims=True))
            den = jnp.sum(e, axis=-1, keepdims=True)
            outs.append(jnp.dot(e.astype(BF16), vh, preferred_element_type=F32) / den)
        o_ref[...] = jnp.concatenate(outs, axis=1).astype(BF16)

    row = pl.BlockSpec((tq, MEM_W), lambda i: (i, 0))
    return pl.pallas_call(
        body, name=name, grid=(t // tq,),
        in_specs=[row, pl.BlockSpec((N_MEM, 2 * MEM_W), lambda i: (0, 0))], out_specs=row,
        out_shape=jax.ShapeDtypeStruct((t, MEM_W), BF16), compiler_params=_cp(1),
    )(zq, kv)


def _mem_bwd(zq, kv, do, name, tq=512):
    t = zq.shape[0]
    tq = min(tq, t)
    nt = t // tq
    scale = MEM_HD ** -0.5

    def body(q_ref, kv_ref, do_ref, dq_ref, out_ref, dkv_ref):
        @pl.when(pl.program_id(0) == 0)
        def _():
            dkv_ref[...] = jnp.zeros_like(dkv_ref)

        ks = [slice(h * MEM_HD, (h + 1) * MEM_HD) for h in range(MEM_HEADS)]
        vs = [slice(MEM_W + h * MEM_HD, MEM_W + (h + 1) * MEM_HD) for h in range(MEM_HEADS)]
        ss = [lax.dot_general(q_ref[:, k], kv_ref[:, k], NT, preferred_element_type=F32) * scale for k in ks]
        dps = [lax.dot_general(do_ref[:, k], kv_ref[:, v], NT, preferred_element_type=F32) for k, v in zip(ks, vs)]
        pbs, dsbs = [], []
        for s, dp in zip(ss, dps):
            e = jnp.exp(s - jnp.max(s, axis=-1, keepdims=True))
            p = e / jnp.sum(e, axis=-1, keepdims=True)
            ds = p * (dp - jnp.sum(p * dp, axis=-1, keepdims=True))
            pbs.append(p.astype(BF16))
            dsbs.append(ds.astype(BF16))
        dq_ref[...] = jnp.concatenate([jnp.dot(dsb, kv_ref[:, k], preferred_element_type=F32) * scale
                                       for dsb, k in zip(dsbs, ks)], axis=1).astype(BF16)
        for dsb, pb, k, v in zip(dsbs, pbs, ks, vs):
            dkv_ref[k, :] += jnp.dot(q_ref[:, k].T, dsb, preferred_element_type=F32) * scale
            dkv_ref[v, :] += jnp.dot(do_ref[:, k].T, pb, preferred_element_type=F32)

        @pl.when(pl.program_id(0) == nt - 1)
        def _():
            out_ref[...] = dkv_ref[...].T

    row = pl.BlockSpec((tq, MEM_W), lambda i: (i, 0))
    full = pl.BlockSpec((N_MEM, 2 * MEM_W), lambda i: (0, 0))
    return pl.pallas_call(
        body, name=name, grid=(nt,), in_specs=[row, full, row], out_specs=[row, full],
        out_shape=[jax.ShapeDtypeStruct((t, MEM_W), BF16), jax.ShapeDtypeStruct((N_MEM, 2 * MEM_W), F32)],
        scratch_shapes=[pltpu.VMEM((2 * MEM_W, N_MEM), F32)], compiler_params=_cp(1),
    )(zq, kv, do)


NT = (((1,), (1,)), ((), ()))


def _branches_fwd(acts, ws, zg, bg, name, tr=512):
    t = zg.shape[0]
    tr = min(tr, t)

    def body(a0, a1, a2, w0, w1, w2, z_ref, b_ref, y0, y1, y2, o_ref):
        merged = None
        for i, (a_ref, w_ref, y_ref) in enumerate(((a0, w0, y0), (a1, w1, y1), (a2, w2, y2))):
            y = lax.dot_general(a_ref[...], w_ref[...], NT, preferred_element_type=F32)
            y_ref[...] = y.astype(BF16)
            g = _sigmoid(z_ref[:, i * D:(i + 1) * D].astype(F32) + b_ref[:, i * D:(i + 1) * D])
            merged = g * y if merged is None else merged + g * y
        o_ref[...] = merged.astype(BF16)

    row = pl.BlockSpec((tr, D), lambda i: (i, 0))
    outs = pl.pallas_call(
        body, name=name, grid=(t // tr,),
        in_specs=[pl.BlockSpec((tr, a.shape[1]), lambda i: (i, 0)) for a in acts]
        + [pl.BlockSpec(w.shape, lambda i: (0, 0)) for w in ws]
        + [pl.BlockSpec((tr, 3 * D), lambda i: (i, 0)), pl.BlockSpec((1, 3 * D), lambda i: (0, 0))],
        out_specs=[row] * 4, out_shape=[jax.ShapeDtypeStruct((t, D), BF16)] * 4, compiler_params=_cp(1),
    )(*acts, *ws, zg, _row(bg))
    return tuple(outs[:3]), outs[3]


def _branches_bwd(dymix, w_out, ws, zg, bg, ys, name, tr=512):
    t = zg.shape[0]
    tr = min(tr, t)

    def body(dy_ref, wo_ref, w0, w1, w2, z_ref, b_ref, y0, y1, y2, d0, d1, d2, dz_ref, db_ref, da0, da1, da2):
        @pl.when(pl.program_id(0) == 0)
        def _():
            db_ref[...] = jnp.zeros_like(db_ref)

        dm = lax.dot_general(dy_ref[...], wo_ref[...], NT, preferred_element_type=F32)
        for i, (y_ref, w_ref, d_ref, da_ref) in enumerate(((y0, w0, d0, da0), (y1, w1, d1, da1), (y2, w2, d2, da2))):
            gi = _sigmoid(z_ref[:, i * D:(i + 1) * D].astype(F32) + b_ref[:, i * D:(i + 1) * D])
            dy = (dm * gi).astype(BF16)
            d_ref[...] = dy
            dz = dm * y_ref[...].astype(F32) * gi * (1.0 - gi)
            dz_ref[:, i * D:(i + 1) * D] = dz.astype(BF16)
            db_ref[0:1, i * D:(i + 1) * D] += jnp.sum(dz, axis=0, keepdims=True)
            da_ref[...] = jnp.dot(dy, w_ref[...], preferred_element_type=F32).astype(BF16)

    row = pl.BlockSpec((tr, D), lambda i: (i, 0))
    wide = pl.BlockSpec((tr, 3 * D), lambda i: (i, 0))
    outs = pl.pallas_call(
        body, name=name, grid=(t // tr,),
        in_specs=[row, pl.BlockSpec(w_out.shape, lambda i: (0, 0))] + [pl.BlockSpec(w.shape, lambda i: (0, 0)) for w in ws]
        + [wide, pl.BlockSpec((1, 3 * D), lambda i: (0, 0)), row, row, row],
        out_specs=[row, row, row, wide, pl.BlockSpec((8, 3 * D), lambda i: (0, 0))]
        + [pl.BlockSpec((tr, w.shape[1]), lambda i: (i, 0)) for w in ws],
        out_shape=[jax.ShapeDtypeStruct((t, D), BF16)] * 3
        + [jax.ShapeDtypeStruct((t, 3 * D), BF16), jax.ShapeDtypeStruct((8, 3 * D), F32)]
        + [jax.ShapeDtypeStruct((t, w.shape[1]), BF16) for w in ws],
        compiler_params=_cp(1),
    )(dymix, w_out, *ws, zg, _row(bg), *ys)
    return tuple(outs[:3]), outs[3], outs[4][0], tuple(outs[5:])


def _adamw(w, g, m, v, name):
    shape = w.shape
    w, g, m, v = (a.reshape(-1, shape[-1]) for a in (w, g, m, v))
    r, d = w.shape
    tr = next((c for c in (512, 352, 256, 128, 64, 32, 16, 8) if r % c == 0), r)

    def body(w_ref, g_ref, m_ref, v_ref, d_ref, nm_ref, nv_ref):
        gg = g_ref[...]
        m2 = ADAM_B1 * m_ref[...] + (1.0 - ADAM_B1) * gg
        v2 = ADAM_B2 * v_ref[...] + (1.0 - ADAM_B2) * (gg * gg)
        m_hat = m2 / (1.0 - ADAM_B1 ** ADAM_STEP)
        v_hat = v2 / (1.0 - ADAM_B2 ** ADAM_STEP)
        d_ref[...] = -ADAM_LR * (m_hat / (jnp.sqrt(v_hat) + ADAM_EPS) + ADAM_WD * w_ref[...])
        nm_ref[...] = m2
        nv_ref[...] = v2

    row = pl.BlockSpec((tr, d), lambda i: (i, 0))
    outs = pl.pallas_call(
        body, name=name, grid=(r // tr,), in_specs=[row] * 4, out_specs=[row] * 3,
        out_shape=[jax.ShapeDtypeStruct((r, d), F32)] * 3, compiler_params=_cp(1),
    )(w, g, m, v)
    return [o.reshape(shape) for o in outs]


def _slab_tile(rows):
    return next((c for c in range(min(rows, 512), 15, -16) if rows % c == 0), rows)


def _sum_slots(parts, out_dtype, name):
    n, r, d = parts.shape
    tr = _slab_tile(r)

    def body(p_ref, o_ref):
        acc = p_ref[0].astype(F32)
        for s in range(1, n):
            acc = acc + p_ref[s].astype(F32)
        o_ref[...] = acc.astype(o_ref.dtype)

    return pl.pallas_call(
        body, name=name, grid=(r // tr,), in_specs=[pl.BlockSpec((n, tr, d), lambda i: (0, i, 0))],
        out_specs=pl.BlockSpec((tr, d), lambda i: (i, 0)),
        out_shape=jax.ShapeDtypeStruct((r, d), out_dtype), compiler_params=_cp(1),
    )(parts)


def _place():
    return lax.axis_index("x"), lax.axis_index("y"), lax.axis_index("c")


def _all_gather(shard, name):
    r, d = shard.shape

    def body(x_ref, out_ref, send_sems, recv_sems, local_sem):
        x, y, c = _place()
        me, sibling = (x, y, c), (x, y, 1 - c)
        chips = [(1 - x, y), (x, 1 - y), (1 - x, 1 - y)]

        def slot(px, py, pc):
            return out_ref.at[4 * px + 2 * py + pc]

        def copy(k, block, to, src=None):
            return pltpu.make_async_remote_copy(
                src_ref=slot(*block) if src is None else src, dst_ref=slot(*block),
                send_sem=send_sems.at[k], recv_sem=recv_sems.at[k], device_id=to, device_id_type=MESH)

        mine = pltpu.make_async_copy(x_ref, slot(*me), local_sem)
        mine.start()
        first = [copy(0, me, sibling, src=x_ref)]
        first += [copy(1 + j, me, (*chip, c), src=x_ref) for j, chip in enumerate(chips)]
        for cp in first:
            cp.start()
        passed = [copy(4 + j, (*chip, c), sibling) for j, chip in enumerate(chips)]
        for j, chip in enumerate(chips):
            copy(1 + j, (*chip, c), me).wait_recv()
            passed[j].start()
        copy(0, sibling, me).wait_recv()
        for j, chip in enumerate(chips):
            copy(4 + j, (*chip, 1 - c), me).wait_recv()
        for cp in first + passed:
            cp.wait_send()
        mine.wait()

    return pl.pallas_call(
        body, name=name, out_shape=jax.ShapeDtypeStruct((N_DEV, r, d), shard.dtype),
        in_specs=[pl.BlockSpec(memory_space=pl.ANY)], out_specs=pl.BlockSpec(memory_space=pl.ANY),
        scratch_shapes=[pltpu.SemaphoreType.DMA((7,)), pltpu.SemaphoreType.DMA((7,)), pltpu.SemaphoreType.DMA],
    )(shard)


HBM_SPEC = pl.BlockSpec(memory_space=pltpu.HBM)
SEM_SPEC = pl.BlockSpec(memory_space=pltpu.SEMAPHORE)
SPLIT_PARAMS = pltpu.CompilerParams(has_side_effects=pltpu.SideEffectType.DATAFLOW_SIDE_EFFECTING)


def _peers():
    x, y, c = _place()
    flip = lambda v, bit: 1 - v if bit else v
    return 4 * x + 2 * y + c, [((flip(x, k >> 2 & 1), flip(y, k >> 1 & 1), flip(c, k & 1))) for k in range(1, N_DEV)]


def _exchange_start(src, all_gather, name):
    r, d = src.shape[-2:]

    def body(src_ref, land_ref, send_sems, recv_sems, src_thru, land_thru, token):
        me, peers = _peers()
        for k, (px, py, pc) in enumerate(peers):
            part = src_ref if all_gather else src_ref.at[4 * px + 2 * py + pc]
            pltpu.make_async_remote_copy(src_ref=part, dst_ref=land_ref.at[me], send_sem=send_sems.at[k],
                                         recv_sem=recv_sems.at[k], device_id=(px, py, pc), device_id_type=MESH).start()
        token[...] = jnp.zeros_like(token)

    land = lax.empty((N_DEV, r, d), src.dtype)
    return pl.pallas_call(
        body, name=name,
        out_shape=(pltpu.SemaphoreType.DMA((N_DEV - 1,)), pltpu.SemaphoreType.DMA((N_DEV - 1,)), pltpu.HBM(src.shape, src.dtype),
                   pltpu.HBM(land.shape, land.dtype), jax.ShapeDtypeStruct((8, LANES), F32)),
        in_specs=(HBM_SPEC, HBM_SPEC), out_specs=(SEM_SPEC, SEM_SPEC, HBM_SPEC, HBM_SPEC, pl.BlockSpec(memory_space=pltpu.VMEM)),
        input_output_aliases={0: 2, 1: 3}, compiler_params=SPLIT_PARAMS,
    )(pltpu.with_memory_space_constraint(src, pltpu.HBM), pltpu.with_memory_space_constraint(land, pltpu.HBM))


def _exchange_wait(started, after, all_gather, name):
    send_sems, recv_sems, src_thru, land_thru, _ = started

    def body(src_ref, land_ref, send_sems, recv_sems, after_ref, src_out, land_out):
        me, peers = _peers()
        for k, (px, py, pc) in enumerate(peers):
            part = src_ref if all_gather else src_ref.at[4 * px + 2 * py + pc]
            cp = pltpu.make_async_remote_copy(src_ref=part, dst_ref=land_ref.at[4 * px + 2 * py + pc], send_sem=send_sems.at[k],
                                              recv_sem=recv_sems.at[k], device_id=(px, py, pc), device_id_type=MESH)
            cp.wait_send()
            cp.wait_recv()

    return pl.pallas_call(
        body, name=name,
        out_shape=(pltpu.HBM(src_thru.shape, src_thru.dtype), pltpu.HBM(land_thru.shape, land_thru.dtype)),
        in_specs=(HBM_SPEC, HBM_SPEC, SEM_SPEC, SEM_SPEC, pl.BlockSpec(memory_space=pl.ANY)), out_specs=(HBM_SPEC, HBM_SPEC),
        input_output_aliases={0: 0, 1: 1}, compiler_params=SPLIT_PARAMS,
    )(src_thru, land_thru, send_sems, recv_sems, after)


def _own_slot(land, mine):
    x, y, c = _place()
    return lax.dynamic_update_slice(land, mine[None], (4 * x + 2 * y + c, 0, 0))


BIG = (("w_in", (D, 864), 1), ("w_conv_out", (CONV_W, 128), 1), ("w_att_out", (ATT_GW, 128), 1), ("w_mem_kv", (128, D), 0),
       ("w_mem_out", (MEM_W, 128), 1), ("w_out", (128, D), 0), ("w_ffn_in", (D, 704), 1), ("w_ffn_out", (352, D), 0),
       ("conv_dw", (CONV_K, 64), 1))
SMALL = (("rel_bias", (NUM_BUCKETS, 12)), ("norm_mix_pre", (DEPTH, D)), ("b_gate", (DEPTH, 3 * D)),
         ("conv_dw_bias", (DEPTH, CONV_W)), ("conv_ln_g", (DEPTH, CONV_W)), ("conv_ln_b", (DEPTH, CONV_W)),
         ("norm_mem", (DEPTH, D)), ("norm_mix_post", (DEPTH, D)), ("norm_ffn_pre", (DEPTH, D)), ("norm_ffn_post", (DEPTH, D)))
TWIN_WEIGHTS = ("rel_bias", "norm_mix_pre", "w_in", "b_gate", "conv_dw", "conv_dw_bias", "conv_ln_g", "conv_ln_b", "w_conv_out",
                "w_att_out", "norm_mem", "w_mem_kv", "w_mem_out", "w_out", "norm_mix_post", "norm_ffn_pre", "w_ffn_in",
                "w_ffn_out", "norm_ffn_post")


BIG_INFO = {n: (s, a) for n, s, a in BIG}
GROUPS = {"mix": ("w_in", "w_conv_out", "w_att_out", "w_mem_kv", "w_mem_out", "w_out", "conv_dw"), "ffn": ("w_ffn_in", "w_ffn_out")}
TILE_BYTES_PER_LANE = 32


def _rows_of(shape, dtype):
    tile = TILE_BYTES_PER_LANE // jnp.dtype(dtype).itemsize
    return -(-int(np.prod(shape)) // (D * tile)) * tile


def _row_form(shape, axis):
    return tuple(shape) if axis == 0 else (shape[1], shape[0])


def _as_slab_rows(p, lead=()):
    shape = p.shape[len(lead):]
    rows = _rows_of(shape, p.dtype)
    if shape == (rows, D):
        return p
    n = int(np.prod(shape))
    p = jnp.pad(p.reshape(*lead, n), [(0, 0)] * len(lead) + [(0, rows * D - n)])
    return p.reshape(*lead, rows, D)


def _from_slab_rows(rows, shape, lead=()):
    if rows.shape[len(lead):] == tuple(shape):
        return rows
    n = int(np.prod(shape))
    return rows.reshape(*lead, -1)[..., :n].reshape(*lead, *shape)


def _pack(pieces, dtype):
    return jnp.concatenate([_as_slab_rows(p.astype(dtype)) for p in pieces], axis=0)


def _unpack(slab, shapes, packed_as):
    out, r0 = [], 0
    for s in shapes:
        rows = _rows_of(s, packed_as)
        out.append(_from_slab_rows(slab[r0:r0 + rows], s))
        r0 += rows
    return out


def _pack_group(blocks, layer, group):
    return _pack([blocks[n][layer] if BIG_INFO[n][1] == 0 else blocks[n][layer].T for n in GROUPS[group]], BF16)


def _full_weights(gathered, group):
    out, r0 = {}, 0
    for name in GROUPS[group]:
        r, k = _row_form(*BIG_INFO[name])
        rows = _rows_of((r, k), BF16)
        out[name] = _from_slab_rows(gathered[:, r0:r0 + rows], (r, k), lead=(N_DEV,)).reshape(N_DEV * r, k)
        r0 += rows
    return out


def _scatter_layout(grads, group):
    parts = []
    for name in GROUPS[group]:
        r, k = _row_form(*BIG_INFO[name])
        parts.append(_as_slab_rows(grads[name].astype(BF16).reshape(N_DEV, r, k), lead=(N_DEV,)))
    return jnp.concatenate(parts, axis=1)


def _unpack_blocks(slab, group):
    out = _unpack(slab, [_row_form(*BIG_INFO[n]) for n in GROUPS[group]], BF16)
    return {n: b if BIG_INFO[n][1] == 0 else b.T for n, b in zip(GROUPS[group], out)}


def _split_w_in(w_in):
    edges = np.cumsum((0,) + SEG)
    return [w_in[edges[i]:edges[i + 1]] for i in range(4)]


def _conv_taps(conv_dw):
    return jnp.pad(conv_dw.astype(F32).T, ((0, 32 - CONV_K), (0, 0)))


def _layer_fwd(l, x, h, nm, get_w, sm, bias_tabs):
    t = x.shape[0]
    wts = get_w(l, "mix", h)
    w_seg = _split_w_in(wts["w_in"])
    zc = _mm(h, w_seg[0], "fwd_in_conv", tb=True)
    za = _mm(h, w_seg[1], "fwd_in_att", tb=True, tn=1152)
    zq = _mm(h, w_seg[2], "fwd_in_memq", tb=True)
    zg = _mm(h, w_seg[3], "fwd_in_gate", tb=True)
    y, cact = _conv_fwd(zc, _conv_taps(wts["conv_dw"]), sm["conv_dw_bias"][l], sm["conv_ln_g"][l], sm["conv_ln_b"][l],
                        "conv_fwd")
    qkv, outs, lses = [], [], []
    for g, (_, dil) in enumerate(ATT_PATTERNS):
        cols = tuple(i * 3 + g for i in range(3))
        if dil == 1:
            trio, at = (za, za, za), cols
        else:
            trio, at = _to_residue_major([(za, c) for c in cols], dil, f"qkv_residue_d{dil}"), (0, 0, 0)
        o, ls = _att_fwd(*trio, bias_tabs[g], t // dil, f"att_fwd_d{dil}", cols=at)
        qkv.append((trio, at))
        outs.append(o)
        lses.append(ls)
    att, lse = _att_combine(outs, lses, "att_combine")
    kv = _mm(nm, wts["w_mem_kv"], "fwd_mem_kv")
    om = _mem_fwd(zq, kv, "mem_fwd")
    w_br = (wts["w_conv_out"], wts["w_att_out"], wts["w_mem_out"])
    ys, merged = _branches_fwd((cact, att, om), w_br, zg, sm["b_gate"][l], "branches_fwd")
    ymix, x1, h2 = _mm_norm_res(merged, wts["w_out"], x, sm["norm_mix_post"][l], sm["norm_ffn_pre"][l], "fwd_out_norm")
    wff = get_w(l, "ffn", h2)
    gff, uff, act = _mm_swiglu_fwd(h2, wff["w_ffn_in"], "fwd_ffn_in_swiglu")
    g_next = sm["norm_mix_pre"][l + 1] if l + 1 < DEPTH else None
    f, x2, h_next = _mm_norm_res(act, wff["w_ffn_out"], x1, sm["norm_ffn_post"][l], g_next,
                                 "fwd_ffn_out_norm" if g_next is not None else "fwd_ffn_out_last")
    saved = dict(x=x, h=h, zc=zc, zq=zq, zg=zg, y=y, cact=cact, qkv=qkv, att=att, lse=lse, kv=kv, om=om, ys=ys,
                 merged=merged, ymix=ymix, x1=x1, h2=h2, gff=gff, uff=uff, act=act, f=f)
    return x2, h_next, saved


def _after(value, *tokens):
    for t in tokens:
        if t is not None:
            value = value + t[0, 0].astype(value.dtype)
    return value


def _layer_bwd(l, dx2, s, nm, mem, get_w, put_g, sm, bias_tabs, bucket_idx):
    t = dx2.shape[0]
    wts, wff = get_w(l, "mix", None), get_w(l, "ffn", None)
    gb, gf, gs = {}, {}, {}
    df, gs["norm_ffn_post"] = _norm_bwd(s["f"], sm["norm_ffn_post"][l], dx2, None, BF16, "bwd_norm_ffn_post")
    dg, du = _mm_swiglu_bwd(df, wff["w_ffn_out"], s["gff"], s["uff"], "bwd_ffn_out_d_swiglu")
    gf["w_ffn_out"] = _mm(s["act"], df, "bwd_ffn_out_w", ta=True, tm=1408)
    gf["w_ffn_in"] = jnp.concatenate([_mm(dg, s["h2"], "bwd_ffn_in_w_gate", ta=True, tm=1408),
                                      _mm(du, s["h2"], "bwd_ffn_in_w_up", ta=True, tm=1408)], axis=0)
    tok = put_g(l, "ffn", gf)
    dh2 = _mm_sum([(dg, _after(wff["w_ffn_in"][:FFN_H], tok), FFN_H // 2), (du, wff["w_ffn_in"][FFN_H:], FFN_H // 2)],
                  "bwd_ffn_in_d")
    dx1, gs["norm_ffn_pre"] = _norm_bwd(s["x1"], sm["norm_ffn_pre"][l], dh2, dx2, F32, "bwd_norm_ffn_pre")
    dymix, gs["norm_mix_post"] = _norm_bwd(s["ymix"], sm["norm_mix_post"][l], dx1, None, BF16, "bwd_norm_mix_post")
    gb["w_out"] = _mm(s["merged"], dymix, "bwd_out_w", ta=True)
    w_br = (wts["w_conv_out"], wts["w_att_out"], wts["w_mem_out"])
    dys, dzg, gs["b_gate"], (dcact, datt, dom) = _branches_bwd(dymix, wts["w_out"], w_br, s["zg"], sm["b_gate"][l], s["ys"],
                                                               "branches_bwd")
    gb["w_conv_out"] = _mm(dys[0], s["cact"], "bwd_conv_out_w", ta=True)
    dzc, dconv, gs["conv_dw_bias"], gs["conv_ln_g"], gs["conv_ln_b"] = _conv_bwd(
        s["zc"], s["y"], dcact, _conv_taps(wts["conv_dw"]), sm["conv_ln_g"][l], sm["conv_ln_b"][l], "conv_bwd")
    gb["conv_dw"] = dconv[:CONV_K].T
    gb["w_att_out"] = _mm(dys[1], s["att"], "bwd_att_out_w", ta=True)
    delta = _att_delta(s["att"], datt, "att_delta")
    dqkv, rel = [], []
    for g, (_, dil) in enumerate(ATT_PATTERNS):
        trio, at = s["qkv"][g]
        if dil == 1:
            dl_r, do_r, lse_r = delta, datt, s["lse"]
        else:
            dl_r, do_r, lse_r = _to_residue_major([(delta, 0), (datt, 0), (s["lse"], 0)], dil, f"att_bwd_residue_d{dil}")
        dq, dk, dv, dbias = _att_bwd(*trio, bias_tabs[g], dl_r, do_r, lse_r, t // dil, f"att_bwd_d{dil}", cols=at)
        dqkv.append((dq, dk, dv))
        rel.append(_bias_grad(dbias, bucket_idx[g], f"bias_grad_d{dil}")[:HEADS, :NUM_BUCKETS])
    dza = _assemble_dza(dqkv, "att_bwd_assemble")
    gs["rel_bias"] = jnp.concatenate(rel, axis=0).T
    gb["w_mem_out"] = _mm(dys[2], s["om"], "bwd_mem_out_w", ta=True)
    dzq, dkv = _mem_bwd(s["zq"], s["kv"], dom, "mem_bwd")
    dkv = dkv.astype(BF16)
    gb["w_mem_kv"] = _mm(nm, dkv, "bwd_mem_kv_w", ta=True)
    dnm = _mm(dkv, wts["w_mem_kv"], "bwd_mem_kv_d", tb=True, out_dtype=F32)
    _, gs["norm_mem"] = _norm_bwd(mem, sm["norm_mem"][l], dnm, None, BF16, "bwd_norm_mem")
    segs = ((dzc, "conv", 1024), (dza, "att", 1152), (dzq, "memq", 512), (dzg, "gate", 1024))
    gb["w_in"] = jnp.concatenate([_mm(dz, s["h"], f"bwd_in_{nm_}_w", ta=True, tm=min(blk, 1152), tk=2048)
                                  for dz, nm_, blk in segs], axis=0)
    tok = put_g(l, "mix", gb)
    w_seg = _split_w_in(wts["w_in"])
    w_seg[0] = _after(w_seg[0], tok)
    dx, gs["norm_mix_pre"] = _mm_sum([(dz, w_seg[i], blk) for i, (dz, _, blk) in enumerate(segs)], "bwd_in_d_norm",
                                     norm=(s["x"], sm["norm_mix_pre"][l], dx1))
    return dx, gs


def _local_step(x, mem, target, sm, get_w, put_g, tokens=()):
    t = x.shape[0]
    bias_tabs, bucket_idx = [], []
    for g, (_, dil) in enumerate(ATT_PATTERNS):
        idx = _bucket_index(_att_tiles(t // dil)[1], dil)
        bucket_idx.append(jnp.asarray(idx))
        bias_tabs.append(_bias_table(sm["rel_bias"][:, g * HEADS:(g + 1) * HEADS], bucket_idx[g], f"bias_table_d{dil}"))
    h = _norm_plain(x, _after(sm["norm_mix_pre"][0], *tokens), "norm_first")
    saved, nms = [], []
    for l in range(DEPTH):
        nm = _norm_plain(mem, sm["norm_mem"][l], "norm_mem")
        x, h, s = _layer_fwd(l, x, h, nm, get_w, sm, bias_tabs)
        saved.append(s)
        nms.append(nm)
    loss, dx = _loss_head(x, target, "loss_head")
    gsmall = {}
    for l in reversed(range(DEPTH)):
        dx, gs = _layer_bwd(l, dx, saved[l], nms[l], mem, get_w, put_g, sm, bias_tabs, bucket_idx)
        for n, v in gs.items():
            gsmall.setdefault(n, {})[l] = v
    small = {}
    for n, _ in SMALL:
        small[n] = gsmall[n][0] + gsmall[n][1] if n == "rel_bias" else jnp.stack([gsmall[n][0], gsmall[n][1]])
    return loss, dx, small


def kernel(x, mem, rel_bias, norm_mix_pre, w_in, b_gate, conv_dw, conv_dw_bias, conv_ln_g, conv_ln_b, w_conv_out, w_att_out, norm_mem, w_mem_kv, w_mem_out, w_out, norm_mix_post, norm_ffn_pre, w_ffn_in, w_ffn_out, norm_ffn_post, loss_target, m_rel_bias, m_norm_mix_pre, m_w_in, m_b_gate, m_conv_dw, m_conv_dw_bias, m_conv_ln_g, m_conv_ln_b, m_w_conv_out, m_w_att_out, m_norm_mem, m_w_mem_kv, m_w_mem_out, m_w_out, m_norm_mix_post, m_norm_ffn_pre, m_w_ffn_in, m_w_ffn_out, m_norm_ffn_post, v_rel_bias, v_norm_mix_pre, v_w_in, v_b_gate, v_conv_dw, v_conv_dw_bias, v_conv_ln_g, v_conv_ln_b, v_w_conv_out, v_w_att_out, v_norm_mem, v_w_mem_kv, v_w_mem_out, v_w_out, v_norm_mix_post, v_norm_ffn_pre, v_w_ffn_in, v_w_ffn_out, v_norm_ffn_post):
    args = dict(locals())
    w = {n: args[n] for n in TWIN_WEIGHTS}
    m = {n: args["m_" + n] for n in TWIN_WEIGHTS}
    v = {n: args["v_" + n] for n in TWIN_WEIGHTS}
    sm = {n: w[n] for n, _ in SMALL}

    first = (0, "mix")
    keys = [(l, grp) for l in range(DEPTH) for grp in GROUPS]
    shard = {(l, grp): _pack_group(w, l, grp) for l, grp in keys}
    weights = {first: _full_weights(_all_gather(shard[first], "gather_l0_mix"), "mix")}
    settled = weights[first]["w_out"][:1, :1] * 0
    ag = {k: _exchange_start(_after(shard[k], settled), True, f"ag_start_l{k[0]}_{k[1]}") for k in keys if k != first}

    def get_w(l, grp, after):
        if (l, grp) not in weights:
            mine, land = _exchange_wait(ag[l, grp], after, True, f"ag_wait_l{l}_{grp}")
            weights[l, grp] = _full_weights(_own_slot(land, mine), grp)
        return weights[l, grp]

    rs = {}

    def put_g(l, grp, grads):
        rs[l, grp] = _exchange_start(_scatter_layout(grads, grp), False, f"rs_start_l{l}_{grp}")
        return rs[l, grp][4]

    loss, dx, gsmall = _local_step(x[0], mem[0], loss_target[0], sm, get_w, put_g, [st[4] for st in ag.values()])
    loss = lax.psum(loss, ("x", "y", "c"))

    xi, yi, ci = _place()
    me = 4 * xi + 2 * yi + ci
    g_slab = {}
    for k, st in rs.items():
        contrib, land = _exchange_wait(st, dx, False, f"rs_wait_l{k[0]}_{k[1]}")
        own = lax.dynamic_index_in_dim(contrib, me, axis=0, keepdims=False)
        g_slab[k] = _sum_slots(_own_slot(land, own), F32, f"rs_sum_l{k[0]}_{k[1]}")
    g_layers = [{**_unpack_blocks(g_slab[l, "mix"], "mix"), **_unpack_blocks(g_slab[l, "ffn"], "ffn")} for l in range(DEPTH)]
    small_shapes = [s for _, s in SMALL]
    g_small = _unpack(_sum_slots(_all_gather(_pack([gsmall[n] for n, _ in SMALL], F32), "gather_small"), F32, "sum_small"),
                      small_shapes, F32)
    grads = {n: jnp.stack([g_layers[0][n], g_layers[1][n]]) for n, _, _ in BIG}
    grads.update({n: g_small[i] for i, (n, _) in enumerate(SMALL)})

    delta, new_m, new_v = {}, {}, {}
    for n in TWIN_WEIGHTS:
        delta[n], new_m[n], new_v[n] = _adamw(w[n], grads[n], m[n], v[n], f"adamw_{n}")

    return (loss, dx[None], *[grads[n] for n in TWIN_WEIGHTS], *[delta[n] for n in TWIN_WEIGHTS],
            *[new_m[n] for n in TWIN_WEIGHTS], *[new_v[n] for n in TWIN_WEIGHTS])
```

```python
import functools

import numpy as np
import jax
import jax.numpy as jnp
from jax import lax
from jax.experimental import pallas as pl
from jax.experimental.pallas import tpu as pltpu

F32 = jnp.float32
BF16 = jnp.bfloat16

N_DEV = 8
D = 1024
DEPTH = 2
CONV_W = 512
CONV_K = 31
CONV_PAD = 16
ATT_PATTERNS = ((128, 1), (512, 4), (2048, 16))
ATT_RADIUS = 64
HEADS = 4
HEAD_DIM = 64
ATT_GW = HEADS * HEAD_DIM
ATT_W = 3 * ATT_GW
MEM_HEADS = 4
MEM_HD = 128
MEM_W = 512
N_MEM = 256
FFN_H = 2816
NUM_BUCKETS = 32
MAX_DISTANCE = 1024
RMS_EPS = 1e-6
LN_EPS = 1e-5
NEG_INF = -1e30
SEG = (2 * CONV_W, 3 * ATT_W, MEM_W, 3 * D)
ADAM_LR, ADAM_B1, ADAM_B2, ADAM_EPS, ADAM_WD, ADAM_STEP = 0.001, 0.9, 0.999, 1e-08, 0.01, 10

VMEM_LIMIT_V7X = 56 * 1024 * 1024
MESH = pl.DeviceIdType.MESH


def _cp(n_axes):
    return pltpu.CompilerParams(dimension_semantics=("arbitrary",) * n_axes, vmem_limit_bytes=VMEM_LIMIT_V7X)


def _row(v):
    return v.reshape(1, -1)


def _mm(a, b, name, *, ta=False, tb=False, out_dtype=BF16, tm=2048, tn=1024, tk=2048):
    kdim, m = a.shape if ta else a.shape[::-1]
    n, kb = b.shape if tb else b.shape[::-1]
    assert kb == kdim and a.dtype == BF16 and b.dtype == BF16
    tm, tn, tk = min(tm, m), min(tn, n), min(tk, kdim)
    assert m % tm == 0 and n % tn == 0 and kdim % tk == 0, (name, m, n, kdim, tm, tn, tk)
    nk = kdim // tk
    a_spec = pl.BlockSpec((tk, tm), lambda i, j, k: (k, i)) if ta else pl.BlockSpec((tm, tk), lambda i, j, k: (i, k))
    b_spec = pl.BlockSpec((tn, tk), lambda i, j, k: (j, k)) if tb else pl.BlockSpec((tk, tn), lambda i, j, k: (k, j))
    o_spec = pl.BlockSpec((tm, tn), lambda i, j, k: (i, j))
    dims = (((0 if ta else 1,), (1 if tb else 0,)), ((), ()))
    use_scratch = nk > 1 and out_dtype != F32

    def body(*refs):
        a_ref, b_ref, o_ref = refs[:3]
        p = lax.dot_general(a_ref[...], b_ref[...], dims, preferred_element_type=F32)
        if nk == 1:
            o_ref[...] = p.astype(o_ref.dtype)
            return
        k = pl.program_id(2)
        t_ref = refs[3] if use_scratch else o_ref

        @pl.when(k == 0)
        def _():
            t_ref[...] = p

        @pl.when(k > 0)
        def _():
            t_ref[...] += p

        if use_scratch:
            @pl.when(k == nk - 1)
            def _():
                o_ref[...] = t_ref[...].astype(o_ref.dtype)

    return pl.pallas_call(
        body, name=name, grid=(m // tm, n // tn, nk), in_specs=[a_spec, b_spec], out_specs=o_spec,
        out_shape=jax.ShapeDtypeStruct((m, n), out_dtype),
        scratch_shapes=[pltpu.VMEM((tm, tn), F32)] if use_scratch else [],
        compiler_params=_cp(3),
    )(a, b)


EPILOGUE_COLS = 512
EPILOGUE_ROWS = 512


def _chunks(size, step):
    return [(c, min(c + step, size)) for c in range(0, size, step)]


def _rms_bwd_val(v, g, dy):
    r = lax.rsqrt(jnp.mean(v * v, axis=-1, keepdims=True) + RMS_EPS)
    vhat = v * r
    dvh = dy * g
    dv = r * (dvh - vhat * jnp.mean(dvh * vhat, axis=-1, keepdims=True))
    return dv, jnp.sum(dy * vhat, axis=0, keepdims=True)


def _mm_sum(pairs, name, *, norm=None, tm=1024):
    m, n = pairs[0][0].shape[0], pairs[0][1].shape[1]
    tm = min(tm, m)
    starts, specs, total = [], [], 0
    for a, b, tk in pairs:
        assert a.shape == (m, b.shape[0]) and b.shape[1] == n and a.shape[1] % tk == 0 and a.dtype == b.dtype == BF16
        nk = a.shape[1] // tk
        starts.append((total, nk))
        step = functools.partial(lambda k, s0, nk_: jnp.clip(k - s0, 0, nk_ - 1), s0=total, nk_=nk)
        specs.append(pl.BlockSpec((tm, tk), functools.partial(lambda i, k, st: (i, st(k)), st=step)))
        specs.append(pl.BlockSpec((tk, n), functools.partial(lambda i, k, st: (st(k), 0), st=step)))
        total += nk
    np_ = 2 * len(pairs)
    row = pl.BlockSpec((tm, n), lambda i, k: (i, 0))

    def accumulate(refs, acc, k, upto=total):
        for p, (s0, nk) in enumerate(starts):
            @pl.when((k >= s0) & (k < min(s0 + nk, upto)))
            def _(p=p, s0=s0):
                part = jnp.dot(refs[2 * p][...], refs[2 * p + 1][...], preferred_element_type=F32)
                if s0 == 0:
                    @pl.when(k == 0)
                    def _():
                        acc[...] = part

                    @pl.when(k > 0)
                    def _():
                        acc[...] += part
                else:
                    acc[...] += part

    operands = [op for a, b, _ in pairs for op in (a, b)]
    if norm is None:
        def plain(*refs):
            accumulate(refs, refs[np_], pl.program_id(1))

        return pl.pallas_call(
            plain, name=name, grid=(m // tm, total), in_specs=specs, out_specs=row,
            out_shape=jax.ShapeDtypeStruct((m, n), F32), compiler_params=_cp(2),
        )(*operands)

    v, g, dres = norm

    def body(*refs):
        v_ref, g_ref, dres_ref, dv_ref, dg_ref, acc = refs[np_:np_ + 6]
        i, k = pl.program_id(0), pl.program_id(1)
        accumulate(refs, acc, k, upto=total - 1)

        @pl.when((i == 0) & (k == 0))
        def _():
            dg_ref[...] = jnp.zeros_like(dg_ref)

        @pl.when(k == total - 1)
        def _():
            a_ref, b_ref = refs[np_ - 2], refs[np_ - 1]
            for r0, r1 in _chunks(tm, EPILOGUE_ROWS):
                dy = jnp.dot(a_ref[r0:r1, :], b_ref[...], preferred_element_type=F32)
                if total > 1:
                    dy = dy + acc[r0:r1, :]
                dv, dg = _rms_bwd_val(v_ref[r0:r1, :], g_ref[...], dy)
                dv_ref[r0:r1, :] = dv + dres_ref[r0:r1, :]
                dg_ref[0:1, :] += dg

    once = pl.BlockSpec((tm, n), lambda i, k: (i, 0), pipeline_mode=pl.Buffered(1))
    dv, dg = pl.pallas_call(
        body, name=name, grid=(m // tm, total),
        in_specs=specs + [once, pl.BlockSpec((1, n), lambda i, k: (0, 0)), once],
        out_specs=[row, pl.BlockSpec((8, n), lambda i, k: (0, 0))],
        out_shape=[jax.ShapeDtypeStruct((m, n), F32), jax.ShapeDtypeStruct((8, n), F32)],
        scratch_shapes=[pltpu.VMEM((tm, n), F32)], compiler_params=_cp(2),
    )(*operands, v, _row(g), dres)
    return dv, dg[0]


def _mm_swiglu_fwd(h, w, name, *, tm=1024, tn=1408):
    m, kdim = h.shape
    hid = w.shape[0] // 2
    tm = min(tm, m)
    nj = hid // tn
    dims = (((1,), (1,)), ((), ()))

    def body(h_ref, wg_ref, wu_ref, g_ref, u_ref, a_ref):
        hv = h_ref[...]
        for c0, c1 in _chunks(tn, EPILOGUE_COLS):
            g = lax.dot_general(hv, wg_ref[c0:c1, :], dims, preferred_element_type=F32)
            u = lax.dot_general(hv, wu_ref[c0:c1, :], dims, preferred_element_type=F32)
            gb, ub = g.astype(BF16), u.astype(BF16)
            g_ref[:, c0:c1] = gb
            u_ref[:, c0:c1] = ub
            a_ref[:, c0:c1] = gb * _sigmoid(gb) * ub

    out = pl.BlockSpec((tm, tn), lambda i, j: (i, j))
    return pl.pallas_call(
        body, name=name, grid=(m // tm, nj),
        in_specs=[pl.BlockSpec((tm, kdim), lambda i, j: (i, 0)), pl.BlockSpec((tn, kdim), lambda i, j: (j, 0)),
                  pl.BlockSpec((tn, kdim), lambda i, j: (j + nj, 0))],
        out_specs=[out, out, out], out_shape=[jax.ShapeDtypeStruct((m, hid), BF16)] * 3, compiler_params=_cp(2),
    )(h, w, w)


def _mm_swiglu_bwd(df, w, g, u, name, *, tm=1024, tn=1408):
    m, kdim = df.shape
    hid = w.shape[0]
    tm = min(tm, m)

    def body(df_ref, w_ref, g_ref, u_ref, dg_ref, du_ref):
        dfv = df_ref[...]
        for c0, c1 in _chunks(tn, EPILOGUE_COLS):
            da = lax.dot_general(dfv, w_ref[c0:c1, :], (((1,), (1,)), ((), ())), preferred_element_type=F32)
            gg, uu, dab = g_ref[:, c0:c1], u_ref[:, c0:c1], da.astype(BF16)
            sg = _sigmoid(gg)
            dg_ref[:, c0:c1] = dab * uu * (sg * (1.0 + gg * (1.0 - sg)))
            du_ref[:, c0:c1] = dab * gg * sg

    blk = pl.BlockSpec((tm, tn), lambda i, j: (i, j))
    return pl.pallas_call(
        body, name=name, grid=(m // tm, hid // tn),
        in_specs=[pl.BlockSpec((tm, kdim), lambda i, j: (i, 0)), pl.BlockSpec((tn, kdim), lambda i, j: (j, 0)), blk, blk],
        out_specs=[blk, blk], out_shape=[jax.ShapeDtypeStruct((m, hid), BF16)] * 2, compiler_params=_cp(2),
    )(df, w, g, u)


def _mm_norm_res(a, b, xres, g, g_next, name, *, tm=1024):
    m, kdim = a.shape
    n = b.shape[1]
    tm = min(tm, m)
    two = g_next is not None

    def body(*refs):
        a_ref, b_ref, x_ref, g_ref = refs[:4]
        y_ref, xn_ref = refs[4 + two], refs[5 + two]
        for r0, r1 in _chunks(tm, EPILOGUE_ROWS):
            y = jnp.dot(a_ref[r0:r1, :], b_ref[...], preferred_element_type=F32)
            y_ref[r0:r1, :] = y.astype(BF16)
            xn = x_ref[r0:r1, :] + _rms_val(y, g_ref[...])
            xn_ref[r0:r1, :] = xn
            if two:
                refs[7][r0:r1, :] = _rms_val(xn, refs[4][...]).astype(BF16)

    row = pl.BlockSpec((tm, n), lambda i: (i, 0))
    vec = pl.BlockSpec((1, n), lambda i: (0, 0))
    outs = pl.pallas_call(
        body, name=name, grid=(m // tm,),
        in_specs=[pl.BlockSpec((tm, kdim), lambda i: (i, 0)), pl.BlockSpec((kdim, n), lambda i: (0, 0)), row, vec] + ([vec] if two else []),
        out_specs=[row, row] + ([row] if two else []),
        out_shape=[jax.ShapeDtypeStruct((m, n), BF16), jax.ShapeDtypeStruct((m, n), F32)]
        + ([jax.ShapeDtypeStruct((m, n), BF16)] if two else []),
        compiler_params=_cp(1),
    )(*((a, b, xres, _row(g)) + ((_row(g_next),) if two else ())))
    return (outs[0], outs[1], outs[2]) if two else (outs[0], outs[1], None)


def _rms_val(v, g):
    return v * lax.rsqrt(jnp.mean(v * v, axis=-1, keepdims=True) + RMS_EPS) * g


def _norm_plain(v, g, name, tr=512):
    t, d = v.shape
    tr = min(tr, t)

    def body(v_ref, g_ref, o_ref):
        o_ref[...] = _rms_val(v_ref[...], g_ref[...]).astype(BF16)

    return pl.pallas_call(
        body, name=name, grid=(t // tr,),
        in_specs=[pl.BlockSpec((tr, d), lambda i: (i, 0)), pl.BlockSpec((1, d), lambda i: (0, 0))],
        out_specs=pl.BlockSpec((tr, d), lambda i: (i, 0)),
        out_shape=jax.ShapeDtypeStruct((t, d), BF16), compiler_params=_cp(1),
    )(v, _row(g))


def _norm_bwd(v, g, dout, dres, out_dtype, name, tr=512):
    t, d = v.shape
    tr = min(tr, t)
    nt = t // tr
    has_res = dres is not None

    def body(*refs):
        v_ref, g_ref, do_ref = refs[:3]
        dv_ref, dg_ref = refs[3 + has_res], refs[4 + has_res]
        i = pl.program_id(0)
        vv = v_ref[...].astype(F32)
        dy = do_ref[...].astype(F32)
        r = lax.rsqrt(jnp.mean(vv * vv, axis=-1, keepdims=True) + RMS_EPS)
        vhat = vv * r
        dvh = dy * g_ref[...]
        dv = r * (dvh - vhat * jnp.mean(dvh * vhat, axis=-1, keepdims=True))
        if has_res:
            dv = dv + refs[3][...]
        dv_ref[...] = dv.astype(dv_ref.dtype)
        part = jnp.sum(dy * vhat, axis=0, keepdims=True)

        @pl.when(i == 0)
        def _():
            dg_ref[...] = jnp.zeros_like(dg_ref)

        dg_ref[0:1, :] += part

    row = pl.BlockSpec((tr, d), lambda i: (i, 0))
    dv, dg = pl.pallas_call(
        body, name=name, grid=(nt,),
        in_specs=[row, pl.BlockSpec((1, d), lambda i: (0, 0)), row] + ([row] if has_res else []),
        out_specs=[row, pl.BlockSpec((8, d), lambda i: (0, 0))],
        out_shape=[jax.ShapeDtypeStruct((t, d), out_dtype), jax.ShapeDtypeStruct((8, d), F32)],
        compiler_params=_cp(1),
    )(*((v, _row(g), dout) + ((dres,) if has_res else ())))
    return dv, dg[0]


def _norm_bwd_pair(v1, g1, dout, dres, v2, g2, name, tr=512):
    t, d = v1.shape
    tr = min(tr, t)

    def body(v1_ref, g1_ref, do_ref, dr_ref, v2_ref, g2_ref, d1_ref, d2_ref, dg1_ref, dg2_ref):
        @pl.when(pl.program_id(0) == 0)
        def _():
            dg1_ref[...] = jnp.zeros_like(dg1_ref)
            dg2_ref[...] = jnp.zeros_like(dg2_ref)

        dv1, dg1 = _rms_bwd_val(v1_ref[...].astype(F32), g1_ref[...], do_ref[...].astype(F32))
        d1 = dv1 + dr_ref[...]
        d1_ref[...] = d1
        dv2, dg2 = _rms_bwd_val(v2_ref[...].astype(F32), g2_ref[...], d1)
        d2_ref[...] = dv2.astype(BF16)
        dg1_ref[0:1, :] += dg1
        dg2_ref[0:1, :] += dg2

    row = pl.BlockSpec((tr, d), lambda i: (i, 0))
    vec = pl.BlockSpec((1, d), lambda i: (0, 0))
    acc = pl.BlockSpec((8, d), lambda i: (0, 0))
    d1, d2, dg1, dg2 = pl.pallas_call(
        body, name=name, grid=(t // tr,), in_specs=[row, vec, row, row, row, vec], out_specs=[row, row, acc, acc],
        out_shape=[jax.ShapeDtypeStruct((t, d), F32), jax.ShapeDtypeStruct((t, d), BF16),
                   jax.ShapeDtypeStruct((8, d), F32), jax.ShapeDtypeStruct((8, d), F32)],
        compiler_params=_cp(1),
    )(v1, _row(g1), dout, dres, v2, _row(g2))
    return d1, d2, dg1[0], dg2[0]


def _loss_head(y, target, name, tr=512):
    t, d = y.shape
    tr = min(tr, t)
    nt = t // tr

    def body(y_ref, t_ref, dy_ref, l_ref):
        i = pl.program_id(0)
        err = y_ref[...] - t_ref[...]
        dy_ref[...] = err * (1.0 / d)

        @pl.when(i == 0)
        def _():
            l_ref[...] = jnp.zeros_like(l_ref)

        l_ref[...] += jnp.sum(err * err) * (0.5 / d)

    row = pl.BlockSpec((tr, d), lambda i: (i, 0))
    dy, l = pl.pallas_call(
        body, name=name, grid=(nt,), in_specs=[row, row],
        out_specs=[row, pl.BlockSpec((8, 128), lambda i: (0, 0))],
        out_shape=[jax.ShapeDtypeStruct((t, d), F32), jax.ShapeDtypeStruct((8, 128), F32)],
        compiler_params=_cp(1),
    )(y, target)
    return l[0, 0], dy


def _sigmoid(v):
    return 1.0 / (1.0 + jnp.exp(-v))


def _halo_specs(tq, width, nt, halo):
    per = tq // halo
    last = nt * per - 1
    main = pl.BlockSpec((tq, width), lambda i: (i, 0))
    prev = pl.BlockSpec((halo, width), lambda i: (jnp.maximum(i * per - 1, 0), 0))
    nxt = pl.BlockSpec((halo, width), lambda i: (jnp.minimum((i + 1) * per, last), 0))
    return prev, main, nxt


def _glu_window(zp, zm, zn, i, nt):
    def glu(z):
        z = z.astype(F32)
        return z[:, :CONV_W] * _sigmoid(z[:, CONV_W:])

    up = jnp.where(i > 0, glu(zp), 0.0)
    un = jnp.where(i < nt - 1, glu(zn), 0.0)
    return jnp.concatenate([up, glu(zm), un], axis=0)


def _shifted(win, shift, rows):
    if shift == 0:
        return win[:rows]
    return pltpu.roll(win, win.shape[0] - shift, 0)[:rows]


def _conv_fwd(zc, w, b, ln_g, ln_b, name, tq=512):
    t = zc.shape[0]
    tq = min(tq, t)
    nt = t // tq

    def body(zp_ref, zm_ref, zn_ref, w_ref, b_ref, g_ref, bb_ref, y_ref, c_ref):
        i = pl.program_id(0)
        win = _glu_window(zp_ref[...], zm_ref[...], zn_ref[...], i, nt)
        wv = w_ref[...]
        y = jnp.zeros((tq, CONV_W), F32) + b_ref[...]
        for k in range(CONV_K):
            y = y + _shifted(win, k + 1, tq) * wv[k:k + 1, :]
        y_ref[...] = y
        mu = jnp.mean(y, axis=-1, keepdims=True)
        yc = y - mu
        ln = yc * lax.rsqrt(jnp.mean(yc * yc, axis=-1, keepdims=True) + LN_EPS) * g_ref[...] + bb_ref[...]
        c_ref[...] = (ln * _sigmoid(ln)).astype(BF16)

    vec = pl.BlockSpec((1, CONV_W), lambda i: (0, 0))
    out = pl.BlockSpec((tq, CONV_W), lambda i: (i, 0))
    return pl.pallas_call(
        body, name=name, grid=(nt,),
        in_specs=[*_halo_specs(tq, 2 * CONV_W, nt, CONV_PAD), pl.BlockSpec((32, CONV_W), lambda i: (0, 0)), vec, vec, vec],
        out_specs=[out, out],
        out_shape=[jax.ShapeDtypeStruct((t, CONV_W), F32), jax.ShapeDtypeStruct((t, CONV_W), BF16)],
        compiler_params=_cp(1),
    )(zc, zc, zc, w, _row(b), _row(ln_g), _row(ln_b))


def _conv_bwd(zc, y, dc, w, ln_g, ln_b, name, tq=256):
    t = zc.shape[0]
    tq = min(tq, t)
    nt = t // tq
    rows = tq + 2 * CONV_PAD

    def body(zp_ref, zm_ref, zn_ref, yp_ref, ym_ref, yn_ref, dp_ref, dm_ref, dn_ref, w_ref, g_ref, bb_ref,
             dz_ref, dw_ref, db_ref, dg_ref, dbb_ref):
        i = pl.program_id(0)
        uwin = _glu_window(zp_ref[...], zm_ref[...], zn_ref[...], i, nt)
        ywin = jnp.concatenate([yp_ref[...], ym_ref[...], yn_ref[...]], axis=0)
        dcw = jnp.concatenate([dp_ref[...], dm_ref[...], dn_ref[...]], axis=0).astype(F32)
        mu = jnp.mean(ywin, axis=-1, keepdims=True)
        yc = ywin - mu
        rstd = lax.rsqrt(jnp.mean(yc * yc, axis=-1, keepdims=True) + LN_EPS)
        yhat = yc * rstd
        ln = yhat * g_ref[...] + bb_ref[...]
        sg = _sigmoid(ln)
        dl = dcw * (sg * (1.0 + ln * (1.0 - sg)))
        ridx = lax.broadcasted_iota(jnp.int32, (rows, 1), 0)
        inside = ((ridx >= CONV_PAD) | (i > 0)) & ((ridx < tq + CONV_PAD) | (i < nt - 1))
        dl = jnp.where(inside, dl, 0.0)
        dyh = dl * g_ref[...]
        dy = rstd * (dyh - jnp.mean(dyh, axis=-1, keepdims=True) - yhat * jnp.mean(dyh * yhat, axis=-1, keepdims=True))
        dy = jnp.where(inside, dy, 0.0)
        main = slice(CONV_PAD, CONV_PAD + tq)
        dlm, yhm, dym = dl[main], yhat[main], dy[main]

        @pl.when(i == 0)
        def _():
            dw_ref[...] = jnp.zeros_like(dw_ref)
            db_ref[...] = jnp.zeros_like(db_ref)
            dg_ref[...] = jnp.zeros_like(dg_ref)
            dbb_ref[...] = jnp.zeros_like(dbb_ref)

        dg_ref[0:1, :] += jnp.sum(dlm * yhm, axis=0, keepdims=True)
        dbb_ref[0:1, :] += jnp.sum(dlm, axis=0, keepdims=True)
        db_ref[0:1, :] += jnp.sum(dym, axis=0, keepdims=True)
        wv = w_ref[...]
        du = jnp.zeros((tq, CONV_W), F32)
        for k in range(CONV_K):
            du = du + _shifted(dy, 2 * CONV_PAD - 1 - k, tq) * wv[k:k + 1, :]
            dw_ref[k:k + 1, :] += jnp.sum(dym * _shifted(uwin, k + 1, tq), axis=0, keepdims=True)
        zm = zm_ref[...].astype(F32)
        a, gt = zm[:, :CONV_W], zm[:, CONV_W:]
        sgt = _sigmoid(gt)
        dz_ref[:, :CONV_W] = (du * sgt).astype(BF16)
        dz_ref[:, CONV_W:] = (du * a * sgt * (1.0 - sgt)).astype(BF16)

    vec = pl.BlockSpec((1, CONV_W), lambda i: (0, 0))
    acc = pl.BlockSpec((8, CONV_W), lambda i: (0, 0))
    dz, dw, db, dg, dbb = pl.pallas_call(
        body, name=name, grid=(nt,),
        in_specs=[*_halo_specs(tq, 2 * CONV_W, nt, CONV_PAD), *_halo_specs(tq, CONV_W, nt, CONV_PAD),
                  *_halo_specs(tq, CONV_W, nt, CONV_PAD), pl.BlockSpec((32, CONV_W), lambda i: (0, 0)), vec, vec],
        out_specs=[pl.BlockSpec((tq, 2 * CONV_W), lambda i: (i, 0)), pl.BlockSpec((32, CONV_W), lambda i: (0, 0)), acc, acc, acc],
        out_shape=[jax.ShapeDtypeStruct((t, 2 * CONV_W), BF16), jax.ShapeDtypeStruct((32, CONV_W), F32),
                   jax.ShapeDtypeStruct((8, CONV_W), F32), jax.ShapeDtypeStruct((8, CONV_W), F32),
                   jax.ShapeDtypeStruct((8, CONV_W), F32)],
        compiler_params=_cp(1),
    )(zc, zc, zc, y, y, y, dc, dc, dc, w, _row(ln_g), _row(ln_b))
    return dz, dw, db[0], dg[0], dbb[0]


def _att_tiles(seq_len):
    tq = min(256, seq_len)
    return tq, min(128, tq)


def _t5_bucket_np(rel):
    nb = NUM_BUCKETS // 2
    max_exact = nb // 2
    ret = np.where(rel > 0, nb, 0)
    n = np.abs(rel)
    nf = np.maximum(n, 1).astype(np.float32)
    large = max_exact + (np.log(nf / np.float32(max_exact)) / np.float32(np.log(MAX_DISTANCE / max_exact))
                         * np.float32(nb - max_exact)).astype(np.int32)
    large = np.minimum(large, nb - 1)
    return ret + np.where(n < max_exact, n, large)


def _bucket_index(sb, dilation):
    off = np.arange(sb + 2 * ATT_RADIUS)[None, :] - ATT_RADIUS - np.arange(sb)[:, None]
    idx = _t5_bucket_np(off * dilation).astype(np.int32)
    return np.where(np.abs(off) <= ATT_RADIUS, idx, -1).astype(np.int32)


def _bias_table(tab, idx, name):
    sb, w = idx.shape

    def body(tab_ref, idx_ref, o_ref):
        ix = idx_ref[...]
        for h in range(HEADS):
            acc = jnp.full((sb, w), NEG_INF, F32)
            for b in range(NUM_BUCKETS):
                acc = jnp.where(ix == b, tab_ref[b, h], acc)
            o_ref[h] = acc

    return pl.pallas_call(
        body, name=name, out_shape=jax.ShapeDtypeStruct((HEADS, sb, w), F32),
        in_specs=[pl.BlockSpec(memory_space=pltpu.SMEM), pl.BlockSpec(memory_space=pltpu.VMEM)],
        out_specs=pl.BlockSpec(memory_space=pltpu.VMEM),
    )(tab, idx)


def _qkv_specs(tq, per, last_blk, cols, tile):
    def main(col):
        return pl.BlockSpec((tq, ATT_GW), lambda *g: (tile(*g), col))

    def prev(col):
        return pl.BlockSpec((ATT_RADIUS, ATT_GW), lambda *g: (jnp.maximum(tile(*g) * per - 1, 0), col))

    def nxt(col):
        return pl.BlockSpec((ATT_RADIUS, ATT_GW), lambda *g: (jnp.minimum((tile(*g) + 1) * per, last_blk), col))

    cq, ck, cv = cols
    return [main(cq), prev(ck), main(ck), nxt(ck), prev(cv), main(cv), nxt(cv)]


def _att_fwd(q, k, v, bias, seq_len, name, cols=(0, 0, 0)):
    t = q.shape[0]
    tq, sb = _att_tiles(seq_len)
    nbl = seq_len // tq
    nt = t // tq
    per = tq // ATT_RADIUS
    w = sb + 2 * ATT_RADIUS
    last_blk = t // ATT_RADIUS - 1

    def body(q_ref, kp_ref, kc_ref, kn_ref, vp_ref, vc_ref, vn_ref, b_ref, o_ref, l_ref):
        n = pl.program_id(0) % nbl
        kw = jnp.concatenate([kp_ref[...], kc_ref[...], kn_ref[...]], axis=0)
        vw = jnp.concatenate([vp_ref[...], vc_ref[...], vn_ref[...]], axis=0)
        kpos = n * tq - ATT_RADIUS + lax.broadcasted_iota(jnp.int32, (1, tq + 2 * ATT_RADIUS), 1)
        valid = (kpos >= 0) & (kpos < seq_len)
        for j in range(tq // sb):
            qj = q_ref[j * sb:(j + 1) * sb, :]
            kj, vj, okj = kw[j * sb:j * sb + w], vw[j * sb:j * sb + w], valid[:, j * sb:j * sb + w]
            outs, lses = [], []
            for h in range(HEADS):
                hs = slice(h * HEAD_DIM, (h + 1) * HEAD_DIM)
                s = lax.dot_general(qj[:, hs], kj[:, hs], NT, preferred_element_type=F32)
                s = jnp.where(okj, s * (HEAD_DIM ** -0.5) + b_ref[h], NEG_INF)
                m = jnp.max(s, axis=-1, keepdims=True)
                e = jnp.exp(s - m)
                den = jnp.sum(e, axis=-1, keepdims=True)
                outs.append(jnp.dot(e.astype(BF16), vj[:, hs], preferred_element_type=F32) / den)
                lses.append(jnp.broadcast_to(m + jnp.log(den), (sb, HEAD_DIM)))
            o_ref[j * sb:(j + 1) * sb, :] = jnp.concatenate(outs, axis=1).astype(BF16)
            l_ref[j * sb:(j + 1) * sb, :] = jnp.concatenate(lses, axis=1)

    main = pl.BlockSpec((tq, ATT_GW), lambda i: (i, 0))
    return pl.pallas_call(
        body, name=name, grid=(nt,),
        in_specs=[*_qkv_specs(tq, per, last_blk, cols, lambda i: i), pl.BlockSpec((HEADS, sb, w), lambda i: (0, 0, 0))],
        out_specs=[main, main],
        out_shape=[jax.ShapeDtypeStruct((t, ATT_GW), BF16), jax.ShapeDtypeStruct((t, ATT_GW), F32)],
        compiler_params=_cp(1),
    )(q, k, k, k, v, v, v, bias)


PERM_ROWS = 2048
LANES = 128


def _spec_rm(d, tr, width, col=0):
    return pl.BlockSpec((d, tr // d, width), lambda i: (0, i, col))


def _rm_view(a, d):
    return a.reshape(d, a.shape[0] // d, a.shape[1])


def _gather_residues(scr, val, d, out_ref, col0):
    tr, w = val.shape
    for c in range(w // LANES):
        scr[...] = val[:, c * LANES:(c + 1) * LANES].astype(F32)
        for r in range(d):
            out_ref[r, :, col0 + c * LANES:col0 + (c + 1) * LANES] = scr[pl.ds(r, tr // d, stride=d), :].astype(out_ref.dtype)


def _scatter_residues(scr, ref, d):
    w = ref.shape[2]
    cols = []
    for c in range(w // LANES):
        for r in range(d):
            scr[pl.ds(r, scr.shape[0] // d, stride=d), :] = ref[r, :, c * LANES:(c + 1) * LANES].astype(F32)
        cols.append(scr[...])
    return jnp.concatenate(cols, axis=1)


def _to_residue_major(srcs, d, name, tr=PERM_ROWS):
    t = srcs[0][0].shape[0]
    tr = min(tr, t)
    n = len(srcs)

    def body(*refs):
        scr = refs[2 * n]
        for k in range(n):
            _gather_residues(scr, refs[k][...], d, refs[n + k], 0)

    outs = pl.pallas_call(
        body, name=name, grid=(t // tr,),
        in_specs=[pl.BlockSpec((tr, ATT_GW), functools.partial(lambda i, col: (i, col), col=col)) for _, col in srcs],
        out_specs=[_spec_rm(d, tr, ATT_GW)] * n,
        out_shape=[jax.ShapeDtypeStruct((d, t // d, ATT_GW), a.dtype) for a, _ in srcs],
        scratch_shapes=[pltpu.VMEM((tr, LANES), F32)], compiler_params=_cp(1),
    )(*[a for a, _ in srcs])
    return [o.reshape(t, ATT_GW) for o in outs]


def _att_combine(outs, lses, name, tr=PERM_ROWS):
    t = outs[0].shape[0]
    tr = min(tr, t)
    dils = [d for _, d in ATT_PATTERNS]

    def body(o0, o1, o2, l0, l1, l2, a_ref, lse_ref, scr):
        os_, ls_ = [], []
        for o_ref, l_ref, d in zip((o0, o1, o2), (l0, l1, l2), dils):
            if d == 1:
                os_.append(o_ref[...].astype(F32))
                ls_.append(l_ref[...])
            else:
                os_.append(_scatter_residues(scr, o_ref, d))
                ls_.append(_scatter_residues(scr, l_ref, d))
        la, lb, lc = ls_
        m = jnp.maximum(jnp.maximum(la, lb), lc)
        ea, eb, ec = jnp.exp(la - m), jnp.exp(lb - m), jnp.exp(lc - m)
        den = ea + eb + ec
        a_ref[...] = ((ea * os_[0] + eb * os_[1] + ec * os_[2]) / den).astype(BF16)
        lse_ref[...] = m + jnp.log(den)

    row = pl.BlockSpec((tr, ATT_GW), lambda i: (i, 0))
    specs = [row if d == 1 else _spec_rm(d, tr, ATT_GW) for d in dils]
    views = lambda arrs: [a if d == 1 else _rm_view(a, d) for a, d in zip(arrs, dils)]
    return pl.pallas_call(
        body, name=name, grid=(t // tr,), in_specs=specs * 2, out_specs=[row, row],
        out_shape=[jax.ShapeDtypeStruct((t, ATT_GW), BF16), jax.ShapeDtypeStruct((t, ATT_GW), F32)],
        scratch_shapes=[pltpu.VMEM((tr, LANES), F32)], compiler_params=_cp(1),
    )(*views(outs), *views(lses))


def _assemble_dza(dqkv, name, tr=PERM_ROWS):
    t = dqkv[0][0].shape[0]
    tr = min(tr, t)
    dils = [d for _, d in ATT_PATTERNS]

    def body(*refs):
        o_ref, scr = refs[9], refs[10]
        for g, d in enumerate(dils):
            for c in range(3):
                ref = refs[g * 3 + c]
                val = ref[...] if d == 1 else _scatter_residues(scr, ref, d).astype(BF16)
                o_ref[:, (c * 3 + g) * ATT_GW:(c * 3 + g + 1) * ATT_GW] = val

    row = pl.BlockSpec((tr, ATT_GW), lambda i: (i, 0))
    specs = [row if d == 1 else _spec_rm(d, tr, ATT_GW) for d in dils for _ in range(3)]
    args = [a if d == 1 else _rm_view(a, d) for trio, d in zip(dqkv, dils) for a in trio]
    return pl.pallas_call(
        body, name=name, grid=(t // tr,), in_specs=specs, out_specs=pl.BlockSpec((tr, 3 * ATT_W), lambda i: (i, 0)),
        out_shape=jax.ShapeDtypeStruct((t, 3 * ATT_W), BF16),
        scratch_shapes=[pltpu.VMEM((tr, LANES), F32)], compiler_params=_cp(1),
    )(*args)


def _att_delta(o, do, name, tr=1024):
    t = o.shape[0]
    tr = min(tr, t)

    def body(o_ref, do_ref, d_ref):
        dd = o_ref[...].astype(F32) * do_ref[...].astype(F32)
        d_ref[...] = jnp.concatenate(
            [jnp.broadcast_to(jnp.sum(dd[:, h * HEAD_DIM:(h + 1) * HEAD_DIM], axis=-1, keepdims=True), (tr, HEAD_DIM))
             for h in range(HEADS)], axis=1)

    row = pl.BlockSpec((tr, ATT_GW), lambda i: (i, 0))
    return pl.pallas_call(
        body, name=name, grid=(t // tr,), in_specs=[row, row], out_specs=row,
        out_shape=jax.ShapeDtypeStruct((t, ATT_GW), F32), compiler_params=_cp(1),
    )(o, do)


def _att_bwd(q, k, v, bias, delta, do, lse, seq_len, name, cols=(0, 0, 0)):
    t = q.shape[0]
    tq, sb = _att_tiles(seq_len)
    nbl = seq_len // tq
    n_res = t // seq_len
    per = tq // ATT_RADIUS
    w = sb + 2 * ATT_RADIUS
    last_blk = t // ATT_RADIUS - 1
    acc_cols = 2 * tq - sb + w
    scale = HEAD_DIM ** -0.5

    def body(q_ref, kp_ref, kc_ref, kn_ref, vp_ref, vc_ref, vn_ref, b_ref, dl_ref, do_ref, l_ref,
             dq_ref, dk_ref, dv_ref, db_ref, ak_ref, av_ref):
        r, n = pl.program_id(0), pl.program_id(1)

        @pl.when((r == 0) & (n == 0))
        def _():
            db_ref[...] = jnp.zeros_like(db_ref)

        @pl.when(n == 0)
        def _():
            ak_ref[...] = jnp.zeros_like(ak_ref)
            av_ref[...] = jnp.zeros_like(av_ref)

        @pl.when(n < nbl)
        def _():
            kw = jnp.concatenate([kp_ref[...], kc_ref[...], kn_ref[...]], axis=0)
            vw = jnp.concatenate([vp_ref[...], vc_ref[...], vn_ref[...]], axis=0)
            kpos = n * tq - ATT_RADIUS + lax.broadcasted_iota(jnp.int32, (1, tq + 2 * ATT_RADIUS), 1)
            valid = (kpos >= 0) & (kpos < seq_len)
            for j in range(tq // sb):
                rows = slice(j * sb, (j + 1) * sb)
                qj, doj = q_ref[rows, :], do_ref[rows, :]
                dlj = dl_ref[rows, :]
                lj = l_ref[rows, :]
                kj, vj, okj = kw[j * sb:j * sb + w], vw[j * sb:j * sb + w], valid[:, j * sb:j * sb + w]
                heads = [slice(h * HEAD_DIM, (h + 1) * HEAD_DIM) for h in range(HEADS)]
                ss = [lax.dot_general(qj[:, hs], kj[:, hs], NT, preferred_element_type=F32) for hs in heads]
                dps = [lax.dot_general(doj[:, hs], vj[:, hs], NT, preferred_element_type=F32) for hs in heads]
                ps, dss, dsbs = [], [], []
                for h, hs in enumerate(heads):
                    s = jnp.where(okj, ss[h] * scale + b_ref[h], NEG_INF)
                    p = jnp.exp(s - lj[:, h * HEAD_DIM:h * HEAD_DIM + 1])
                    ds = p * (dps[h] - dlj[:, h * HEAD_DIM:h * HEAD_DIM + 1])
                    ps.append(p.astype(BF16))
                    dss.append(ds)
                    dsbs.append(ds.astype(BF16))
                dqs = [jnp.dot(dsbs[h], kj[:, hs], preferred_element_type=F32) * scale for h, hs in enumerate(heads)]
                dks = [lax.dot_general(qj[:, hs], dsbs[h], (((0,), (0,)), ((), ())), preferred_element_type=F32) * scale
                       for h, hs in enumerate(heads)]
                dvs = [lax.dot_general(doj[:, hs], ps[h], (((0,), (0,)), ((), ())), preferred_element_type=F32)
                       for h, hs in enumerate(heads)]
                dq_ref[rows, :] = jnp.concatenate(dqs, axis=1).astype(BF16)
                c0 = tq + j * sb
                ak_ref[:, c0:c0 + w] += jnp.concatenate(dks, axis=0)
                av_ref[:, c0:c0 + w] += jnp.concatenate(dvs, axis=0)
                for h in range(HEADS):
                    db_ref[h] += dss[h]

        dk_ref[...] = ak_ref[:, ATT_RADIUS:ATT_RADIUS + tq].T.astype(BF16)
        dv_ref[...] = av_ref[:, ATT_RADIUS:ATT_RADIUS + tq].T.astype(BF16)
        keep = acc_cols - tq
        nk, nv = ak_ref[:, tq:acc_cols], av_ref[:, tq:acc_cols]
        ak_ref[:, 0:keep] = nk
        av_ref[:, 0:keep] = nv
        ak_ref[:, keep:acc_cols] = jnp.zeros((ATT_GW, tq), F32)
        av_ref[:, keep:acc_cols] = jnp.zeros((ATT_GW, tq), F32)

    def tile(r, n):
        return r * nbl + jnp.minimum(n, nbl - 1)

    main = pl.BlockSpec((tq, ATT_GW), lambda r, n: (tile(r, n), 0))
    lag = pl.BlockSpec((tq, ATT_GW), lambda r, n: (r * nbl + jnp.maximum(n - 1, 0), 0))
    bspec = pl.BlockSpec((HEADS, sb, w), lambda r, n: (0, 0, 0))
    return pl.pallas_call(
        body, name=name, grid=(n_res, nbl + 1),
        in_specs=[*_qkv_specs(tq, per, last_blk, cols, tile), bspec, main, main, main],
        out_specs=[main, lag, lag, bspec],
        out_shape=[jax.ShapeDtypeStruct((t, ATT_GW), BF16)] * 3 + [jax.ShapeDtypeStruct((HEADS, sb, w), F32)],
        scratch_shapes=[pltpu.VMEM((ATT_GW, acc_cols), F32), pltpu.VMEM((ATT_GW, acc_cols), F32)],
        compiler_params=_cp(2),
    )(q, k, k, k, v, v, v, bias, delta, do, lse)


def _bias_grad(db, idx, name):
    _, sb, w = db.shape

    def body(db_ref, idx_ref, o_ref):
        ix = idx_ref[...]
        lane = lax.broadcasted_iota(jnp.int32, (1, 128), 1)
        rows = []
        for h in range(HEADS):
            d = db_ref[h]
            acc = jnp.zeros((1, 128), F32)
            for b in range(NUM_BUCKETS):
                acc = acc + jnp.where(lane == b, jnp.sum(jnp.where(ix == b, d, 0.0)), 0.0)
            rows.append(acc)
        o_ref[...] = jnp.concatenate(rows + [jnp.zeros((8 - HEADS, 128), F32)], axis=0)

    return pl.pallas_call(
        body, name=name, out_shape=jax.ShapeDtypeStruct((8, 128), F32),
        in_specs=[pl.BlockSpec(memory_space=pltpu.VMEM), pl.BlockSpec(memory_space=pltpu.VMEM)],
        out_specs=pl.BlockSpec(memory_space=pltpu.VMEM),
    )(db, idx)


def _mem_fwd(zq, kv, name, tq=512):
    t = zq.shape[0]
    tq = min(tq, t)
    scale = MEM_HD ** -0.5

    def body(q_ref, kv_ref, o_ref):
        outs = []
        for h in range(MEM_HEADS):
            hs = slice(h * MEM_HD, (h + 1) * MEM_HD)
            kh = kv_ref[:, h * MEM_HD:(h + 1) * MEM_HD]
            vh = kv_ref[:, MEM_W + h * MEM_HD:MEM_W + (h + 1) * MEM_HD]
            s = lax.dot_general(q_ref[:, hs], kh, (((1,), (1,)), ((), ())), preferred_element_type=F32) * scale
            e = jnp.exp(s - jnp.max(s, axis=-1, keepdims=True))
            den = jnp.sum(e, axis=-1, keepdims=True)
            outs.append(jnp.dot(e.astype(BF16), vh, preferred_element_type=F32) / den)
        o_ref[...] = jnp.concatenate(outs, axis=1).astype(BF16)

    row = pl.BlockSpec((tq, MEM_W), lambda i: (i, 0))
    return pl.pallas_call(
        body, name=name, grid=(t // tq,),
        in_specs=[row, pl.BlockSpec((N_MEM, 2 * MEM_W), lambda i: (0, 0))], out_specs=row,
        out_shape=jax.ShapeDtypeStruct((t, MEM_W), BF16), compiler_params=_cp(1),
    )(zq, kv)


def _mem_bwd(zq, kv, do, name, tq=512):
    t = zq.shape[0]
    tq = min(tq, t)
    nt = t // tq
    scale = MEM_HD ** -0.5

    def body(q_ref, kv_ref, do_ref, dq_ref, out_ref, dkv_ref):
        @pl.when(pl.program_id(0) == 0)
        def _():
            dkv_ref[...] = jnp.zeros_like(dkv_ref)

        ks = [slice(h * MEM_HD, (h + 1) * MEM_HD) for h in range(MEM_HEADS)]
        vs = [slice(MEM_W + h * MEM_HD, MEM_W + (h + 1) * MEM_HD) for h in range(MEM_HEADS)]
        ss = [lax.dot_general(q_ref[:, k], kv_ref[:, k], NT, preferred_element_type=F32) * scale for k in ks]
        dps = [lax.dot_general(do_ref[:, k], kv_ref[:, v], NT, preferred_element_type=F32) for k, v in zip(ks, vs)]
        pbs, dsbs = [], []
        for s, dp in zip(ss, dps):
            e = jnp.exp(s - jnp.max(s, axis=-1, keepdims=True))
            p = e / jnp.sum(e, axis=-1, keepdims=True)
            ds = p * (dp - jnp.sum(p * dp, axis=-1, keepdims=True))
            pbs.append(p.astype(BF16))
            dsbs.append(ds.astype(BF16))
        dq_ref[...] = jnp.concatenate([jnp.dot(dsb, kv_ref[:, k], preferred_element_type=F32) * scale
                                       for dsb, k in zip(dsbs, ks)], axis=1).astype(BF16)
        for dsb, pb, k, v in zip(dsbs, pbs, ks, vs):
            dkv_ref[k, :] += jnp.dot(q_ref[:, k].T, dsb, preferred_element_type=F32) * scale
            dkv_ref[v, :] += jnp.dot(do_ref[:, k].T, pb, preferred_element_type=F32)

        @pl.when(pl.program_id(0) == nt - 1)
        def _():
            out_ref[...] = dkv_ref[...].T

    row = pl.BlockSpec((tq, MEM_W), lambda i: (i, 0))
    full = pl.BlockSpec((N_MEM, 2 * MEM_W), lambda i: (0, 0))
    return pl.pallas_call(
        body, name=name, grid=(nt,), in_specs=[row, full, row], out_specs=[row, full],
        out_shape=[jax.ShapeDtypeStruct((t, MEM_W), BF16), jax.ShapeDtypeStruct((N_MEM, 2 * MEM_W), F32)],
        scratch_shapes=[pltpu.VMEM((2 * MEM_W, N_MEM), F32)], compiler_params=_cp(1),
    )(zq, kv, do)


NT = (((1,), (1,)), ((), ()))


def _branches_fwd(acts, ws, zg, bg, name, tr=512):
    t = zg.shape[0]
    tr = min(tr, t)

    def body(a0, a1, a2, w0, w1, w2, z_ref, b_ref, y0, y1, y2, o_ref):
        merged = None
        for i, (a_ref, w_ref, y_ref) in enumerate(((a0, w0, y0), (a1, w1, y1), (a2, w2, y2))):
            y = lax.dot_general(a_ref[...], w_ref[...], NT, preferred_element_type=F32)
            y_ref[...] = y.astype(BF16)
            g = _sigmoid(z_ref[:, i * D:(i + 1) * D].astype(F32) + b_ref[:, i * D:(i + 1) * D])
            merged = g * y if merged is None else merged + g * y
        o_ref[...] = merged.astype(BF16)

    row = pl.BlockSpec((tr, D), lambda i: (i, 0))
    outs = pl.pallas_call(
        body, name=name, grid=(t // tr,),
        in_specs=[pl.BlockSpec((tr, a.shape[1]), lambda i: (i, 0)) for a in acts]
        + [pl.BlockSpec(w.shape, lambda i: (0, 0)) for w in ws]
        + [pl.BlockSpec((tr, 3 * D), lambda i: (i, 0)), pl.BlockSpec((1, 3 * D), lambda i: (0, 0))],
        out_specs=[row] * 4, out_shape=[jax.ShapeDtypeStruct((t, D), BF16)] * 4, compiler_params=_cp(1),
    )(*acts, *ws, zg, _row(bg))
    return tuple(outs[:3]), outs[3]


def _branches_bwd(dymix, w_out, ws, zg, bg, ys, name, tr=512):
    t = zg.shape[0]
    tr = min(tr, t)

    def body(dy_ref, wo_ref, w0, w1, w2, z_ref, b_ref, y0, y1, y2, d0, d1, d2, dz_ref, db_ref, da0, da1, da2):
        @pl.when(pl.program_id(0) == 0)
        def _():
            db_ref[...] = jnp.zeros_like(db_ref)

        dm = lax.dot_general(dy_ref[...], wo_ref[...], NT, preferred_element_type=F32)
        for i, (y_ref, w_ref, d_ref, da_ref) in enumerate(((y0, w0, d0, da0), (y1, w1, d1, da1), (y2, w2, d2, da2))):
            gi = _sigmoid(z_ref[:, i * D:(i + 1) * D].astype(F32) + b_ref[:, i * D:(i + 1) * D])
            dy = (dm * gi).astype(BF16)
            d_ref[...] = dy
            dz = dm * y_ref[...].astype(F32) * gi * (1.0 - gi)
            dz_ref[:, i * D:(i + 1) * D] = dz.astype(BF16)
            db_ref[0:1, i * D:(i + 1) * D] += jnp.sum(dz, axis=0, keepdims=True)
            da_ref[...] = jnp.dot(dy, w_ref[...], preferred_element_type=F32).astype(BF16)

    row = pl.BlockSpec((tr, D), lambda i: (i, 0))
    wide = pl.BlockSpec((tr, 3 * D), lambda i: (i, 0))
    outs = pl.pallas_call(
        body, name=name, grid=(t // tr,),
        in_specs=[row, pl.BlockSpec(w_out.shape, lambda i: (0, 0))] + [pl.BlockSpec(w.shape, lambda i: (0, 0)) for w in ws]
        + [wide, pl.BlockSpec((1, 3 * D), lambda i: (0, 0)), row, row, row],
        out_specs=[row, row, row, wide, pl.BlockSpec((8, 3 * D), lambda i: (0, 0))]
        + [pl.BlockSpec((tr, w.shape[1]), lambda i: (i, 0)) for w in ws],
        out_shape=[jax.ShapeDtypeStruct((t, D), BF16)] * 3
        + [jax.ShapeDtypeStruct((t, 3 * D), BF16), jax.ShapeDtypeStruct((8, 3 * D), F32)]
        + [jax.ShapeDtypeStruct((t, w.shape[1]), BF16) for w in ws],
        compiler_params=_cp(1),
    )(dymix, w_out, *ws, zg, _row(bg), *ys)
    return tuple(outs[:3]), outs[3], outs[4][0], tuple(outs[5:])


def _adamw(w, g, m, v, name):
    shape = w.shape
    w, g, m, v = (a.reshape(-1, shape[-1]) for a in (w, g, m, v))
    r, d = w.shape
    tr = next((c for c in (512, 352, 256, 128, 64, 32, 16, 8) if r % c == 0), r)

    def body(w_ref, g_ref, m_ref, v_ref, d_ref, nm_ref, nv_ref):
        gg = g_ref[...]
        m2 = ADAM_B1 * m_ref[...] + (1.0 - ADAM_B1) * gg
        v2 = ADAM_B2 * v_ref[...] + (1.0 - ADAM_B2) * (gg * gg)
        m_hat = m2 / (1.0 - ADAM_B1 ** ADAM_STEP)
        v_hat = v2 / (1.0 - ADAM_B2 ** ADAM_STEP)
        d_ref[...] = -ADAM_LR * (m_hat / (jnp.sqrt(v_hat) + ADAM_EPS) + ADAM_WD * w_ref[...])
        nm_ref[...] = m2
        nv_ref[...] = v2

    row = pl.BlockSpec((tr, d), lambda i: (i, 0))
    outs = pl.pallas_call(
        body, name=name, grid=(r // tr,), in_specs=[row] * 4, out_specs=[row] * 3,
        out_shape=[jax.ShapeDtypeStruct((r, d), F32)] * 3, compiler_params=_cp(1),
    )(w, g, m, v)
    return [o.reshape(shape) for o in outs]


def _slab_tile(rows):
    return next((c for c in range(min(rows, 512), 15, -16) if rows % c == 0), rows)


def _sum_slots(parts, out_dtype, name):
    n, r, d = parts.shape
    tr = _slab_tile(r)

    def body(p_ref, o_ref):
        acc = p_ref[0].astype(F32)
        for s in range(1, n):
            acc = acc + p_ref[s].astype(F32)
        o_ref[...] = acc.astype(o_ref.dtype)

    return pl.pallas_call(
        body, name=name, grid=(r // tr,), in_specs=[pl.BlockSpec((n, tr, d), lambda i: (0, i, 0))],
        out_specs=pl.BlockSpec((tr, d), lambda i: (i, 0)),
        out_shape=jax.ShapeDtypeStruct((r, d), out_dtype), compiler_params=_cp(1),
    )(parts)


def _place():
    return lax.axis_index("x"), lax.axis_index("y"), lax.axis_index("c")


def _all_gather(shard, name):
    r, d = shard.shape

    def body(x_ref, out_ref, send_sems, recv_sems, local_sem):
        x, y, c = _place()
        me, sibling = (x, y, c), (x, y, 1 - c)
        chips = [(1 - x, y), (x, 1 - y), (1 - x, 1 - y)]

        def slot(px, py, pc):
            return out_ref.at[4 * px + 2 * py + pc]

        def copy(k, block, to, src=None):
            return pltpu.make_async_remote_copy(
                src_ref=slot(*block) if src is None else src, dst_ref=slot(*block),
                send_sem=send_sems.at[k], recv_sem=recv_sems.at[k], device_id=to, device_id_type=MESH)

        mine = pltpu.make_async_copy(x_ref, slot(*me), local_sem)
        mine.start()
        first = [copy(0, me, sibling, src=x_ref)]
        first += [copy(1 + j, me, (*chip, c), src=x_ref) for j, chip in enumerate(chips)]
        for cp in first:
            cp.start()
        passed = [copy(4 + j, (*chip, c), sibling) for j, chip in enumerate(chips)]
        for j, chip in enumerate(chips):
            copy(1 + j, (*chip, c), me).wait_recv()
            passed[j].start()
        copy(0, sibling, me).wait_recv()
        for j, chip in enumerate(chips):
            copy(4 + j, (*chip, 1 - c), me).wait_recv()
        for cp in first + passed:
            cp.wait_send()
        mine.wait()

    return pl.pallas_call(
        body, name=name, out_shape=jax.ShapeDtypeStruct((N_DEV, r, d), shard.dtype),
        in_specs=[pl.BlockSpec(memory_space=pl.ANY)], out_specs=pl.BlockSpec(memory_space=pl.ANY),
        scratch_shapes=[pltpu.SemaphoreType.DMA((7,)), pltpu.SemaphoreType.DMA((7,)), pltpu.SemaphoreType.DMA],
    )(shard)


HBM_SPEC = pl.BlockSpec(memory_space=pltpu.HBM)
SEM_SPEC = pl.BlockSpec(memory_space=pltpu.SEMAPHORE)
SPLIT_PARAMS = pltpu.CompilerParams(has_side_effects=pltpu.SideEffectType.DATAFLOW_SIDE_EFFECTING)


def _peers():
    x, y, c = _place()
    flip = lambda v, bit: 1 - v if bit else v
    return 4 * x + 2 * y + c, [((flip(x, k >> 2 & 1), flip(y, k >> 1 & 1), flip(c, k & 1))) for k in range(1, N_DEV)]


def _exchange_start(src, all_gather, name):
    r, d = src.shape[-2:]

    def body(src_ref, land_ref, send_sems, recv_sems, src_thru, land_thru, token):
        me, peers = _peers()
        for k, (px, py, pc) in enumerate(peers):
            part = src_ref if all_gather else src_ref.at[4 * px + 2 * py + pc]
            pltpu.make_async_remote_copy(src_ref=part, dst_ref=land_ref.at[me], send_sem=send_sems.at[k],
                                         recv_sem=recv_sems.at[k], device_id=(px, py, pc), device_id_type=MESH).start()
        token[...] = jnp.zeros_like(token)

    land = lax.empty((N_DEV, r, d), src.dtype)
    return pl.pallas_call(
        body, name=name,
        out_shape=(pltpu.SemaphoreType.DMA((N_DEV - 1,)), pltpu.SemaphoreType.DMA((N_DEV - 1,)), pltpu.HBM(src.shape, src.dtype),
                   pltpu.HBM(land.shape, land.dtype), jax.ShapeDtypeStruct((8, LANES), F32)),
        in_specs=(HBM_SPEC, HBM_SPEC), out_specs=(SEM_SPEC, SEM_SPEC, HBM_SPEC, HBM_SPEC, pl.BlockSpec(memory_space=pltpu.VMEM)),
        input_output_aliases={0: 2, 1: 3}, compiler_params=SPLIT_PARAMS,
    )(pltpu.with_memory_space_constraint(src, pltpu.HBM), pltpu.with_memory_space_constraint(land, pltpu.HBM))


def _exchange_wait(started, after, all_gather, name):
    send_sems, recv_sems, src_thru, land_thru, _ = started

    def body(src_ref, land_ref, send_sems, recv_sems, after_ref, src_out, land_out):
        me, peers = _peers()
        for k, (px, py, pc) in enumerate(peers):
            part = src_ref if all_gather else src_ref.at[4 * px + 2 * py + pc]
            cp = pltpu.make_async_remote_copy(src_ref=part, dst_ref=land_ref.at[4 * px + 2 * py + pc], send_sem=send_sems.at[k],
                                              recv_sem=recv_sems.at[k], device_id=(px, py, pc), device_id_type=MESH)
            cp.wait_send()
            cp.wait_recv()

    return pl.pallas_call(
        body, name=name,
        out_shape=(pltpu.HBM(src_thru.shape, src_thru.dtype), pltpu.HBM(land_thru.shape, land_thru.dtype)),
        in_specs=(HBM_SPEC, HBM_SPEC, SEM_SPEC, SEM_SPEC, pl.BlockSpec(memory_space=pl.ANY)), out_specs=(HBM_SPEC, HBM_SPEC),
        input_output_aliases={0: 0, 1: 1}, compiler_params=SPLIT_PARAMS,
    )(src_thru, land_thru, send_sems, recv_sems, after)


def _own_slot(land, mine):
    x, y, c = _place()
    return lax.dynamic_update_slice(land, mine[None], (4 * x + 2 * y + c, 0, 0))


BIG = (("w_in", (D, 864), 1), ("w_conv_out", (CONV_W, 128), 1), ("w_att_out", (ATT_GW, 128), 1), ("w_mem_kv", (128, D), 0),
       ("w_mem_out", (MEM_W, 128), 1), ("w_out", (128, D), 0), ("w_ffn_in", (D, 704), 1), ("w_ffn_out", (352, D), 0),
       ("conv_dw", (CONV_K, 64), 1))
SMALL = (("rel_bias", (NUM_BUCKETS, 12)), ("norm_mix_pre", (DEPTH, D)), ("b_gate", (DEPTH, 3 * D)),
         ("conv_dw_bias", (DEPTH, CONV_W)), ("conv_ln_g", (DEPTH, CONV_W)), ("conv_ln_b", (DEPTH, CONV_W)),
         ("norm_mem", (DEPTH, D)), ("norm_mix_post", (DEPTH, D)), ("norm_ffn_pre", (DEPTH, D)), ("norm_ffn_post", (DEPTH, D)))
TWIN_WEIGHTS = ("rel_bias", "norm_mix_pre", "w_in", "b_gate", "conv_dw", "conv_dw_bias", "conv_ln_g", "conv_ln_b", "w_conv_out",
                "w_att_out", "norm_mem", "w_mem_kv", "w_mem_out", "w_out", "norm_mix_post", "norm_ffn_pre", "w_ffn_in",
                "w_ffn_out", "norm_ffn_post")


BIG_INFO = {n: (s, a) for n, s, a in BIG}
GROUPS = {"mix": ("w_in", "w_conv_out", "w_att_out", "w_mem_kv", "w_mem_out", "w_out", "conv_dw"), "ffn": ("w_ffn_in", "w_ffn_out")}
TILE_BYTES_PER_LANE = 32


def _rows_of(shape, dtype):
    tile = TILE_BYTES_PER_LANE // jnp.dtype(dtype).itemsize
    return -(-int(np.prod(shape)) // (D * tile)) * tile


def _row_form(shape, axis):
    return tuple(shape) if axis == 0 else (shape[1], shape[0])


def _as_slab_rows(p, lead=()):
    shape = p.shape[len(lead):]
    rows = _rows_of(shape, p.dtype)
    if shape == (rows, D):
        return p
    n = int(np.prod(shape))
    p = jnp.pad(p.reshape(*lead, n), [(0, 0)] * len(lead) + [(0, rows * D - n)])
    return p.reshape(*lead, rows, D)


def _from_slab_rows(rows, shape, lead=()):
    if rows.shape[len(lead):] == tuple(shape):
        return rows
    n = int(np.prod(shape))
    return rows.reshape(*lead, -1)[..., :n].reshape(*lead, *shape)


def _pack(pieces, dtype):
    return jnp.concatenate([_as_slab_rows(p.astype(dtype)) for p in pieces], axis=0)


def _unpack(slab, shapes, packed_as):
    out, r0 = [], 0
    for s in shapes:
        rows = _rows_of(s, packed_as)
        out.append(_from_slab_rows(slab[r0:r0 + rows], s))
        r0 += rows
    return out


def _pack_group(blocks, layer, group):
    return _pack([blocks[n][layer] if BIG_INFO[n][1] == 0 else blocks[n][layer].T for n in GROUPS[group]], BF16)


def _full_weights(gathered, group):
    out, r0 = {}, 0
    for name in GROUPS[group]:
        r, k = _row_form(*BIG_INFO[name])
        rows = _rows_of((r, k), BF16)
        out[name] = _from_slab_rows(gathered[:, r0:r0 + rows], (r, k), lead=(N_DEV,)).reshape(N_DEV * r, k)
        r0 += rows
    return out


def _scatter_layout(grads, group):
    parts = []
    for name in GROUPS[group]:
        r, k = _row_form(*BIG_INFO[name])
        parts.append(_as_slab_rows(grads[name].astype(BF16).reshape(N_DEV, r, k), lead=(N_DEV,)))
    return jnp.concatenate(parts, axis=1)


def _unpack_blocks(slab, group):
    out = _unpack(slab, [_row_form(*BIG_INFO[n]) for n in GROUPS[group]], BF16)
    return {n: b if BIG_INFO[n][1] == 0 else b.T for n, b in zip(GROUPS[group], out)}


def _split_w_in(w_in):
    edges = np.cumsum((0,) + SEG)
    return [w_in[edges[i]:edges[i + 1]] for i in range(4)]


def _conv_taps(conv_dw):
    return jnp.pad(conv_dw.astype(F32).T, ((0, 32 - CONV_K), (0, 0)))


def _layer_fwd(l, x, h, nm, get_w, sm, bias_tabs):
    t = x.shape[0]
    wts = get_w(l, "mix", h)
    w_seg = _split_w_in(wts["w_in"])
    zc = _mm(h, w_seg[0], "fwd_in_conv", tb=True)
    za = _mm(h, w_seg[1], "fwd_in_att", tb=True, tn=1152)
    zq = _mm(h, w_seg[2], "fwd_in_memq", tb=True)
    zg = _mm(h, w_seg[3], "fwd_in_gate", tb=True)
    y, cact = _conv_fwd(zc, _conv_taps(wts["conv_dw"]), sm["conv_dw_bias"][l], sm["conv_ln_g"][l], sm["conv_ln_b"][l],
                        "conv_fwd")
    qkv, outs, lses = [], [], []
    for g, (_, dil) in enumerate(ATT_PATTERNS):
        cols = tuple(i * 3 + g for i in range(3))
        if dil == 1:
            trio, at = (za, za, za), cols
        else:
            trio, at = _to_residue_major([(za, c) for c in cols], dil, f"qkv_residue_d{dil}"), (0, 0, 0)
        o, ls = _att_fwd(*trio, bias_tabs[g], t // dil, f"att_fwd_d{dil}", cols=at)
        qkv.append((trio, at))
        outs.append(o)
        lses.append(ls)
    att, lse = _att_combine(outs, lses, "att_combine")
    kv = _mm(nm, wts["w_mem_kv"], "fwd_mem_kv")
    om = _mem_fwd(zq, kv, "mem_fwd")
    w_br = (wts["w_conv_out"], wts["w_att_out"], wts["w_mem_out"])
    ys, merged = _branches_fwd((cact, att, om), w_br, zg, sm["b_gate"][l], "branches_fwd")
    ymix, x1, h2 = _mm_norm_res(merged, wts["w_out"], x, sm["norm_mix_post"][l], sm["norm_ffn_pre"][l], "fwd_out_norm")
    wff = get_w(l, "ffn", h2)
    gff, uff, act = _mm_swiglu_fwd(h2, wff["w_ffn_in"], "fwd_ffn_in_swiglu")
    g_next = sm["norm_mix_pre"][l + 1] if l + 1 < DEPTH else None
    f, x2, h_next = _mm_norm_res(act, wff["w_ffn_out"], x1, sm["norm_ffn_post"][l], g_next,
                                 "fwd_ffn_out_norm" if g_next is not None else "fwd_ffn_out_last")
    saved = dict(x=x, h=h, zc=zc, zq=zq, zg=zg, y=y, cact=cact, qkv=qkv, att=att, lse=lse, kv=kv, om=om, ys=ys,
                 merged=merged, ymix=ymix, x1=x1, h2=h2, gff=gff, uff=uff, act=act, f=f)
    return x2, h_next, saved


def _after(value, *tokens):
    for t in tokens:
        if t is not None:
            value = value + t[0, 0].astype(value.dtype)
    return value


def _layer_bwd(l, dx2, s, nm, mem, get_w, put_g, sm, bias_tabs, bucket_idx):
    t = dx2.shape[0]
    wts, wff = get_w(l, "mix", None), get_w(l, "ffn", None)
    gb, gf, gs = {}, {}, {}
    df, gs["norm_ffn_post"] = _norm_bwd(s["f"], sm["norm_ffn_post"][l], dx2, None, BF16, "bwd_norm_ffn_post")
    dg, du = _mm_swiglu_bwd(df, wff["w_ffn_out"], s["gff"], s["uff"], "bwd_ffn_out_d_swiglu")
    gf["w_ffn_out"] = _mm(s["act"], df, "bwd_ffn_out_w", ta=True, tm=1408)
    gf["w_ffn_in"] = jnp.concatenate([_mm(dg, s["h2"], "bwd_ffn_in_w_gate", ta=True, tm=1408),
                                      _mm(du, s["h2"], "bwd_ffn_in_w_up", ta=True, tm=1408)], axis=0)
    tok = put_g(l, "ffn", gf)
    dh2 = _mm_sum([(dg, _after(wff["w_ffn_in"][:FFN_H], tok), FFN_H // 2), (du, wff["w_ffn_in"][FFN_H:], FFN_H // 2)],
                  "bwd_ffn_in_d")
    dx1, dymix, gs["norm_ffn_pre"], gs["norm_mix_post"] = _norm_bwd_pair(
        s["x1"], sm["norm_ffn_pre"][l], dh2, dx2, s["ymix"], sm["norm_mix_post"][l], "bwd_norm_ffn_pre_mix_post")
    gb["w_out"] = _mm(s["merged"], dymix, "bwd_out_w", ta=True)
    w_br = (wts["w_conv_out"], wts["w_att_out"], wts["w_mem_out"])
    dys, dzg, gs["b_gate"], (dcact, datt, dom) = _branches_bwd(dymix, wts["w_out"], w_br, s["zg"], sm["b_gate"][l], s["ys"],
                                                               "branches_bwd")
    gb["w_conv_out"] = _mm(dys[0], s["cact"], "bwd_conv_out_w", ta=True)
    dzc, dconv, gs["conv_dw_bias"], gs["conv_ln_g"], gs["conv_ln_b"] = _conv_bwd(
        s["zc"], s["y"], dcact, _conv_taps(wts["conv_dw"]), sm["conv_ln_g"][l], sm["conv_ln_b"][l], "conv_bwd")
    gb["conv_dw"] = dconv[:CONV_K].T
    gb["w_att_out"] = _mm(dys[1], s["att"], "bwd_att_out_w", ta=True)
    delta = _att_delta(s["att"], datt, "att_delta")
    dqkv, rel = [], []
    for g, (_, dil) in enumerate(ATT_PATTERNS):
        trio, at = s["qkv"][g]
        if dil == 1:
            dl_r, do_r, lse_r = delta, datt, s["lse"]
        else:
            dl_r, do_r, lse_r = _to_residue_major([(delta, 0), (datt, 0), (s["lse"], 0)], dil, f"att_bwd_residue_d{dil}")
        dq, dk, dv, dbias = _att_bwd(*trio, bias_tabs[g], dl_r, do_r, lse_r, t // dil, f"att_bwd_d{dil}", cols=at)
        dqkv.append((dq, dk, dv))
        rel.append(_bias_grad(dbias, bucket_idx[g], f"bias_grad_d{dil}")[:HEADS, :NUM_BUCKETS])
    dza = _assemble_dza(dqkv, "att_bwd_assemble")
    gs["rel_bias"] = jnp.concatenate(rel, axis=0).T
    gb["w_mem_out"] = _mm(dys[2], s["om"], "bwd_mem_out_w", ta=True)
    dzq, dkv = _mem_bwd(s["zq"], s["kv"], dom, "mem_bwd")
    dkv = dkv.astype(BF16)
    gb["w_mem_kv"] = _mm(nm, dkv, "bwd_mem_kv_w", ta=True)
    dnm = _mm(dkv, wts["w_mem_kv"], "bwd_mem_kv_d", tb=True, out_dtype=F32)
    _, gs["norm_mem"] = _norm_bwd(mem, sm["norm_mem"][l], dnm, None, BF16, "bwd_norm_mem")
    segs = ((dzc, "conv", 1024), (dza, "att", 1152), (dzq, "memq", 512), (dzg, "gate", 1024))
    gb["w_in"] = jnp.concatenate([_mm(dz, s["h"], f"bwd_in_{nm_}_w", ta=True, tm=min(blk, 1152), tk=2048)
                                  for dz, nm_, blk in segs], axis=0)
    tok = put_g(l, "mix", gb)
    w_seg = _split_w_in(wts["w_in"])
    w_seg[0] = _after(w_seg[0], tok)
    dx, gs["norm_mix_pre"] = _mm_sum([(dz, w_seg[i], blk) for i, (dz, _, blk) in enumerate(segs)], "bwd_in_d_norm",
                                     norm=(s["x"], sm["norm_mix_pre"][l], dx1))
    return dx, gs


def _local_step(x, mem, target, sm, get_w, put_g, tokens=()):
    t = x.shape[0]
    bias_tabs, bucket_idx = [], []
    for g, (_, dil) in enumerate(ATT_PATTERNS):
        idx = _bucket_index(_att_tiles(t // dil)[1], dil)
        bucket_idx.append(jnp.asarray(idx))
        bias_tabs.append(_bias_table(sm["rel_bias"][:, g * HEADS:(g + 1) * HEADS], bucket_idx[g], f"bias_table_d{dil}"))
    h = _norm_plain(x, _after(sm["norm_mix_pre"][0], *tokens), "norm_first")
    saved, nms = [], []
    for l in range(DEPTH):
        nm = _norm_plain(mem, sm["norm_mem"][l], "norm_mem")
        x, h, s = _layer_fwd(l, x, h, nm, get_w, sm, bias_tabs)
        saved.append(s)
        nms.append(nm)
    loss, dx = _loss_head(x, target, "loss_head")
    gsmall = {}
    for l in reversed(range(DEPTH)):
        dx, gs = _layer_bwd(l, dx, saved[l], nms[l], mem, get_w, put_g, sm, bias_tabs, bucket_idx)
        for n, v in gs.items():
            gsmall.setdefault(n, {})[l] = v
    small = {}
    for n, _ in SMALL:
        small[n] = gsmall[n][0] + gsmall[n][1] if n == "rel_bias" else jnp.stack([gsmall[n][0], gsmall[n][1]])
    return loss, dx, small


def kernel(x, mem, rel_bias, norm_mix_pre, w_in, b_gate, conv_dw, conv_dw_bias, conv_ln_g, conv_ln_b, w_conv_out, w_att_out, norm_mem, w_mem_kv, w_mem_out, w_out, norm_mix_post, norm_ffn_pre, w_ffn_in, w_ffn_out, norm_ffn_post, loss_target, m_rel_bias, m_norm_mix_pre, m_w_in, m_b_gate, m_conv_dw, m_conv_dw_bias, m_conv_ln_g, m_conv_ln_b, m_w_conv_out, m_w_att_out, m_norm_mem, m_w_mem_kv, m_w_mem_out, m_w_out, m_norm_mix_post, m_norm_ffn_pre, m_w_ffn_in, m_w_ffn_out, m_norm_ffn_post, v_rel_bias, v_norm_mix_pre, v_w_in, v_b_gate, v_conv_dw, v_conv_dw_bias, v_conv_ln_g, v_conv_ln_b, v_w_conv_out, v_w_att_out, v_norm_mem, v_w_mem_kv, v_w_mem_out, v_w_out, v_norm_mix_post, v_norm_ffn_pre, v_w_ffn_in, v_w_ffn_out, v_norm_ffn_post):
    args = dict(locals())
    w = {n: args[n] for n in TWIN_WEIGHTS}
    m = {n: args["m_" + n] for n in TWIN_WEIGHTS}
    v = {n: args["v_" + n] for n in TWIN_WEIGHTS}
    sm = {n: w[n] for n, _ in SMALL}

    first = (0, "mix")
    keys = [(l, grp) for l in range(DEPTH) for grp in GROUPS]
    shard = {(l, grp): _pack_group(w, l, grp) for l, grp in keys}
    weights = {first: _full_weights(_all_gather(shard[first], "gather_l0_mix"), "mix")}
    settled = weights[first]["w_out"][:1, :1] * 0
    ag = {k: _exchange_start(_after(shard[k], settled), True, f"ag_start_l{k[0]}_{k[1]}") for k in keys if k != first}

    def get_w(l, grp, after):
        if (l, grp) not in weights:
            mine, land = _exchange_wait(ag[l, grp], after, True, f"ag_wait_l{l}_{grp}")
            weights[l, grp] = _full_weights(_own_slot(land, mine), grp)
        return weights[l, grp]

    rs = {}

    def put_g(l, grp, grads):
        rs[l, grp] = _exchange_start(_scatter_layout(grads, grp), False, f"rs_start_l{l}_{grp}")
        return rs[l, grp][4]

    loss, dx, gsmall = _local_step(x[0], mem[0], loss_target[0], sm, get_w, put_g, [st[4] for st in ag.values()])
    loss = lax.psum(loss, ("x", "y", "c"))

    xi, yi, ci = _place()
    me = 4 * xi + 2 * yi + ci
    g_slab = {}
    for k, st in rs.items():
        contrib, land = _exchange_wait(st, dx, False, f"rs_wait_l{k[0]}_{k[1]}")
        own = lax.dynamic_index_in_dim(contrib, me, axis=0, keepdims=False)
        g_slab[k] = _sum_slots(_own_slot(land, own), F32, f"rs_sum_l{k[0]}_{k[1]}")
    g_layers = [{**_unpack_blocks(g_slab[l, "mix"], "mix"), **_unpack_blocks(g_slab[l, "ffn"], "ffn")} for l in range(DEPTH)]
    small_shapes = [s for _, s in SMALL]
    g_small = _unpack(_sum_slots(_all_gather(_pack([gsmall[n] for n, _ in SMALL], F32), "gather_small"), F32, "sum_small"),
                      small_shapes, F32)
    grads = {n: jnp.stack([g_layers[0][n], g_layers[1][n]]) for n, _, _ in BIG}
    grads.update({n: g_small[i] for i, (n, _) in enumerate(SMALL)})

    delta, new_m, new_v = {}, {}, {}
    for n in TWIN_WEIGHTS:
        delta[n], new_m[n], new_v[n] = _adamw(w[n], grads[n], m[n], v[n], f"adamw_{n}")

    return (loss, dx[None], *[grads[n] for n in TWIN_WEIGHTS], *[delta[n] for n in TWIN_WEIGHTS],
            *[new_m[n] for n in TWIN_WEIGHTS], *[new_v[n] for n in TWIN_WEIGHTS])
```

```python
import functools

import numpy as np
import jax
import jax.numpy as jnp
from jax import lax
from jax.experimental import pallas as pl
from jax.experimental.pallas import tpu as pltpu

F32 = jnp.float32
BF16 = jnp.bfloat16

N_DEV = 8
D = 1024
DEPTH = 2
CONV_W = 512
CONV_K = 31
CONV_PAD = 16
ATT_PATTERNS = ((128, 1), (512, 4), (2048, 16))
ATT_RADIUS = 64
HEADS = 4
HEAD_DIM = 64
ATT_GW = HEADS * HEAD_DIM
ATT_W = 3 * ATT_GW
MEM_HEADS = 4
MEM_HD = 128
MEM_W = 512
N_MEM = 256
FFN_H = 2816
NUM_BUCKETS = 32
MAX_DISTANCE = 1024
RMS_EPS = 1e-6
LN_EPS = 1e-5
NEG_INF = -1e30
SEG = (2 * CONV_W, 3 * ATT_W, MEM_W, 3 * D)
ADAM_LR, ADAM_B1, ADAM_B2, ADAM_EPS, ADAM_WD, ADAM_STEP = 0.001, 0.9, 0.999, 1e-08, 0.01, 10

VMEM_LIMIT_V7X = 56 * 1024 * 1024
MESH = pl.DeviceIdType.MESH


def _cp(n_axes):
    return pltpu.CompilerParams(dimension_semantics=("arbitrary",) * n_axes, vmem_limit_bytes=VMEM_LIMIT_V7X)


def _row(v):
    return v.reshape(1, -1)


def _mm(a, b, name, *, ta=False, tb=False, out_dtype=BF16, tm=2048, tn=1024, tk=2048):
    kdim, m = a.shape if ta else a.shape[::-1]
    n, kb = b.shape if tb else b.shape[::-1]
    assert kb == kdim and a.dtype == BF16 and b.dtype == BF16
    tm, tn, tk = min(tm, m), min(tn, n), min(tk, kdim)
    assert m % tm == 0 and n % tn == 0 and kdim % tk == 0, (name, m, n, kdim, tm, tn, tk)
    nk = kdim // tk
    a_spec = pl.BlockSpec((tk, tm), lambda i, j, k: (k, i)) if ta else pl.BlockSpec((tm, tk), lambda i, j, k: (i, k))
    b_spec = pl.BlockSpec((tn, tk), lambda i, j, k: (j, k)) if tb else pl.BlockSpec((tk, tn), lambda i, j, k: (k, j))
    o_spec = pl.BlockSpec((tm, tn), lambda i, j, k: (i, j))
    dims = (((0 if ta else 1,), (1 if tb else 0,)), ((), ()))
    use_scratch = nk > 1 and out_dtype != F32

    def body(*refs):
        a_ref, b_ref, o_ref = refs[:3]
        p = lax.dot_general(a_ref[...], b_ref[...], dims, preferred_element_type=F32)
        if nk == 1:
            o_ref[...] = p.astype(o_ref.dtype)
            return
        k = pl.program_id(2)
        t_ref = refs[3] if use_scratch else o_ref

        @pl.when(k == 0)
        def _():
            t_ref[...] = p

        @pl.when(k > 0)
        def _():
            t_ref[...] += p

        if use_scratch:
            @pl.when(k == nk - 1)
            def _():
                o_ref[...] = t_ref[...].astype(o_ref.dtype)

    return pl.pallas_call(
        body, name=name, grid=(m // tm, n // tn, nk), in_specs=[a_spec, b_spec], out_specs=o_spec,
        out_shape=jax.ShapeDtypeStruct((m, n), out_dtype),
        scratch_shapes=[pltpu.VMEM((tm, tn), F32)] if use_scratch else [],
        compiler_params=_cp(3),
    )(a, b)


EPILOGUE_COLS = 512
EPILOGUE_ROWS = 512


def _chunks(size, step):
    return [(c, min(c + step, size)) for c in range(0, size, step)]


def _rms_bwd_val(v, g, dy):
    r = lax.rsqrt(jnp.mean(v * v, axis=-1, keepdims=True) + RMS_EPS)
    vhat = v * r
    dvh = dy * g
    dv = r * (dvh - vhat * jnp.mean(dvh * vhat, axis=-1, keepdims=True))
    return dv, jnp.sum(dy * vhat, axis=0, keepdims=True)


def _mm_sum(pairs, name, *, norm=None, tm=1024):
    m, n = pairs[0][0].shape[0], pairs[0][1].shape[1]
    tm = min(tm, m)
    starts, specs, total = [], [], 0
    for a, b, tk in pairs:
        assert a.shape == (m, b.shape[0]) and b.shape[1] == n and a.shape[1] % tk == 0 and a.dtype == b.dtype == BF16
        nk = a.shape[1] // tk
        starts.append((total, nk))
        step = functools.partial(lambda k, s0, nk_: jnp.clip(k - s0, 0, nk_ - 1), s0=total, nk_=nk)
        specs.append(pl.BlockSpec((tm, tk), functools.partial(lambda i, k, st: (i, st(k)), st=step)))
        specs.append(pl.BlockSpec((tk, n), functools.partial(lambda i, k, st: (st(k), 0), st=step)))
        total += nk
    np_ = 2 * len(pairs)
    row = pl.BlockSpec((tm, n), lambda i, k: (i, 0))

    def accumulate(refs, acc, k, upto=total):
        for p, (s0, nk) in enumerate(starts):
            @pl.when((k >= s0) & (k < min(s0 + nk, upto)))
            def _(p=p, s0=s0):
                part = jnp.dot(refs[2 * p][...], refs[2 * p + 1][...], preferred_element_type=F32)
                if s0 == 0:
                    @pl.when(k == 0)
                    def _():
                        acc[...] = part

                    @pl.when(k > 0)
                    def _():
                        acc[...] += part
                else:
                    acc[...] += part

    operands = [op for a, b, _ in pairs for op in (a, b)]
    if norm is None:
        def plain(*refs):
            accumulate(refs, refs[np_], pl.program_id(1))

        return pl.pallas_call(
            plain, name=name, grid=(m // tm, total), in_specs=specs, out_specs=row,
            out_shape=jax.ShapeDtypeStruct((m, n), F32), compiler_params=_cp(2),
        )(*operands)

    v, g, dres = norm

    def body(*refs):
        v_ref, g_ref, dres_ref, dv_ref, dg_ref, acc = refs[np_:np_ + 6]
        i, k = pl.program_id(0), pl.program_id(1)
        accumulate(refs, acc, k, upto=total - 1)

        @pl.when((i == 0) & (k == 0))
        def _():
            dg_ref[...] = jnp.zeros_like(dg_ref)

        @pl.when(k == total - 1)
        def _():
            a_ref, b_ref = refs[np_ - 2], refs[np_ - 1]
            for r0, r1 in _chunks(tm, EPILOGUE_ROWS):
                dy = jnp.dot(a_ref[r0:r1, :], b_ref[...], preferred_element_type=F32)
                if total > 1:
                    dy = dy + acc[r0:r1, :]
                dv, dg = _rms_bwd_val(v_ref[r0:r1, :], g_ref[...], dy)
                dv_ref[r0:r1, :] = dv + dres_ref[r0:r1, :]
                dg_ref[0:1, :] += dg

    once = pl.BlockSpec((tm, n), lambda i, k: (i, 0), pipeline_mode=pl.Buffered(1))
    dv, dg = pl.pallas_call(
        body, name=name, grid=(m // tm, total),
        in_specs=specs + [once, pl.BlockSpec((1, n), lambda i, k: (0, 0)), once],
        out_specs=[row, pl.BlockSpec((8, n), lambda i, k: (0, 0))],
        out_shape=[jax.ShapeDtypeStruct((m, n), F32), jax.ShapeDtypeStruct((8, n), F32)],
        scratch_shapes=[pltpu.VMEM((tm, n), F32)], compiler_params=_cp(2),
    )(*operands, v, _row(g), dres)
    return dv, dg[0]


def _mm_swiglu_fwd(h, w, name, *, tm=1024, tn=1408):
    m, kdim = h.shape
    hid = w.shape[0] // 2
    tm = min(tm, m)
    nj = hid // tn
    dims = (((1,), (1,)), ((), ()))

    def body(h_ref, wg_ref, wu_ref, g_ref, u_ref, a_ref):
        hv = h_ref[...]
        for c0, c1 in _chunks(tn, EPILOGUE_COLS):
            g = lax.dot_general(hv, wg_ref[c0:c1, :], dims, preferred_element_type=F32)
            u = lax.dot_general(hv, wu_ref[c0:c1, :], dims, preferred_element_type=F32)
            gb, ub = g.astype(BF16), u.astype(BF16)
            g_ref[:, c0:c1] = gb
            u_ref[:, c0:c1] = ub
            a_ref[:, c0:c1] = gb * _sigmoid(gb) * ub

    out = pl.BlockSpec((tm, tn), lambda i, j: (i, j))
    return pl.pallas_call(
        body, name=name, grid=(m // tm, nj),
        in_specs=[pl.BlockSpec((tm, kdim), lambda i, j: (i, 0)), pl.BlockSpec((tn, kdim), lambda i, j: (j, 0)),
                  pl.BlockSpec((tn, kdim), lambda i, j: (j + nj, 0))],
        out_specs=[out, out, out], out_shape=[jax.ShapeDtypeStruct((m, hid), BF16)] * 3, compiler_params=_cp(2),
    )(h, w, w)


def _mm_swiglu_bwd(df, w, g, u, name, *, tm=1024, tn=1408):
    m, kdim = df.shape
    hid = w.shape[0]
    tm = min(tm, m)

    def body(df_ref, w_ref, g_ref, u_ref, dg_ref, du_ref):
        dfv = df_ref[...]
        for c0, c1 in _chunks(tn, EPILOGUE_COLS):
            da = lax.dot_general(dfv, w_ref[c0:c1, :], (((1,), (1,)), ((), ())), preferred_element_type=F32)
            gg, uu, dab = g_ref[:, c0:c1], u_ref[:, c0:c1], da.astype(BF16)
            sg = _sigmoid(gg)
            dg_ref[:, c0:c1] = dab * uu * (sg * (1.0 + gg * (1.0 - sg)))
            du_ref[:, c0:c1] = dab * gg * sg

    blk = pl.BlockSpec((tm, tn), lambda i, j: (i, j))
    return pl.pallas_call(
        body, name=name, grid=(m // tm, hid // tn),
        in_specs=[pl.BlockSpec((tm, kdim), lambda i, j: (i, 0)), pl.BlockSpec((tn, kdim), lambda i, j: (j, 0)), blk, blk],
        out_specs=[blk, blk], out_shape=[jax.ShapeDtypeStruct((m, hid), BF16)] * 2, compiler_params=_cp(2),
    )(df, w, g, u)


def _mm_norm_res(a, b, xres, g, g_next, name, *, tm=1024):
    m, kdim = a.shape
    n = b.shape[1]
    tm = min(tm, m)
    two = g_next is not None

    def body(*refs):
        a_ref, b_ref, x_ref, g_ref = refs[:4]
        y_ref, xn_ref = refs[4 + two], refs[5 + two]
        for r0, r1 in _chunks(tm, EPILOGUE_ROWS):
            y = jnp.dot(a_ref[r0:r1, :], b_ref[...], preferred_element_type=F32)
            y_ref[r0:r1, :] = y.astype(BF16)
            xn = x_ref[r0:r1, :] + _rms_val(y, g_ref[...])
            xn_ref[r0:r1, :] = xn
            if two:
                refs[7][r0:r1, :] = _rms_val(xn, refs[4][...]).astype(BF16)

    row = pl.BlockSpec((tm, n), lambda i: (i, 0))
    vec = pl.BlockSpec((1, n), lambda i: (0, 0))
    outs = pl.pallas_call(
        body, name=name, grid=(m // tm,),
        in_specs=[pl.BlockSpec((tm, kdim), lambda i: (i, 0)), pl.BlockSpec((kdim, n), lambda i: (0, 0)), row, vec] + ([vec] if two else []),
        out_specs=[row, row] + ([row] if two else []),
        out_shape=[jax.ShapeDtypeStruct((m, n), BF16), jax.ShapeDtypeStruct((m, n), F32)]
        + ([jax.ShapeDtypeStruct((m, n), BF16)] if two else []),
        compiler_params=_cp(1),
    )(*((a, b, xres, _row(g)) + ((_row(g_next),) if two else ())))
    return (outs[0], outs[1], outs[2]) if two else (outs[0], outs[1], None)


def _rms_val(v, g):
    return v * lax.rsqrt(jnp.mean(v * v, axis=-1, keepdims=True) + RMS_EPS) * g


def _norm_plain(v, g, name, tr=512):
    t, d = v.shape
    tr = min(tr, t)

    def body(v_ref, g_ref, o_ref):
        o_ref[...] = _rms_val(v_ref[...], g_ref[...]).astype(BF16)

    return pl.pallas_call(
        body, name=name, grid=(t // tr,),
        in_specs=[pl.BlockSpec((tr, d), lambda i: (i, 0)), pl.BlockSpec((1, d), lambda i: (0, 0))],
        out_specs=pl.BlockSpec((tr, d), lambda i: (i, 0)),
        out_shape=jax.ShapeDtypeStruct((t, d), BF16), compiler_params=_cp(1),
    )(v, _row(g))


def _norm_bwd(v, g, dout, dres, out_dtype, name, tr=512):
    t, d = v.shape
    tr = min(tr, t)
    nt = t // tr
    has_res = dres is not None

    def body(*refs):
        v_ref, g_ref, do_ref = refs[:3]
        dv_ref, dg_ref = refs[3 + has_res], refs[4 + has_res]
        i = pl.program_id(0)
        vv = v_ref[...].astype(F32)
        dy = do_ref[...].astype(F32)
        r = lax.rsqrt(jnp.mean(vv * vv, axis=-1, keepdims=True) + RMS_EPS)
        vhat = vv * r
        dvh = dy * g_ref[...]
        dv = r * (dvh - vhat * jnp.mean(dvh * vhat, axis=-1, keepdims=True))
        if has_res:
            dv = dv + refs[3][...]
        dv_ref[...] = dv.astype(dv_ref.dtype)
        part = jnp.sum(dy * vhat, axis=0, keepdims=True)

        @pl.when(i == 0)
        def _():
            dg_ref[...] = jnp.zeros_like(dg_ref)

        dg_ref[0:1, :] += part

    row = pl.BlockSpec((tr, d), lambda i: (i, 0))
    dv, dg = pl.pallas_call(
        body, name=name, grid=(nt,),
        in_specs=[row, pl.BlockSpec((1, d), lambda i: (0, 0)), row] + ([row] if has_res else []),
        out_specs=[row, pl.BlockSpec((8, d), lambda i: (0, 0))],
        out_shape=[jax.ShapeDtypeStruct((t, d), out_dtype), jax.ShapeDtypeStruct((8, d), F32)],
        compiler_params=_cp(1),
    )(*((v, _row(g), dout) + ((dres,) if has_res else ())))
    return dv, dg[0]


def _norm_bwd_pair(v1, g1, dout, dres, v2, g2, name, tr=512):
    t, d = v1.shape
    tr = min(tr, t)

    def body(v1_ref, g1_ref, do_ref, dr_ref, v2_ref, g2_ref, d1_ref, d2_ref, dg1_ref, dg2_ref):
        @pl.when(pl.program_id(0) == 0)
        def _():
            dg1_ref[...] = jnp.zeros_like(dg1_ref)
            dg2_ref[...] = jnp.zeros_like(dg2_ref)

        dv1, dg1 = _rms_bwd_val(v1_ref[...].astype(F32), g1_ref[...], do_ref[...].astype(F32))
        d1 = dv1 + dr_ref[...]
        d1_ref[...] = d1
        dv2, dg2 = _rms_bwd_val(v2_ref[...].astype(F32), g2_ref[...], d1)
        d2_ref[...] = dv2.astype(BF16)
        dg1_ref[0:1, :] += dg1
        dg2_ref[0:1, :] += dg2

    row = pl.BlockSpec((tr, d), lambda i: (i, 0))
    vec = pl.BlockSpec((1, d), lambda i: (0, 0))
    acc = pl.BlockSpec((8, d), lambda i: (0, 0))
    d1, d2, dg1, dg2 = pl.pallas_call(
        body, name=name, grid=(t // tr,), in_specs=[row, vec, row, row, row, vec], out_specs=[row, row, acc, acc],
        out_shape=[jax.ShapeDtypeStruct((t, d), F32), jax.ShapeDtypeStruct((t, d), BF16),
                   jax.ShapeDtypeStruct((8, d), F32), jax.ShapeDtypeStruct((8, d), F32)],
        compiler_params=_cp(1),
    )(v1, _row(g1), dout, dres, v2, _row(g2))
    return d1, d2, dg1[0], dg2[0]


def _loss_head(y, target, name, tr=512):
    t, d = y.shape
    tr = min(tr, t)
    nt = t // tr

    def body(y_ref, t_ref, dy_ref, l_ref):
        i = pl.program_id(0)
        err = y_ref[...] - t_ref[...]
        dy_ref[...] = err * (1.0 / d)

        @pl.when(i == 0)
        def _():
            l_ref[...] = jnp.zeros_like(l_ref)

        l_ref[...] += jnp.sum(err * err) * (0.5 / d)

    row = pl.BlockSpec((tr, d), lambda i: (i, 0))
    dy, l = pl.pallas_call(
        body, name=name, grid=(nt,), in_specs=[row, row],
        out_specs=[row, pl.BlockSpec((8, 128), lambda i: (0, 0))],
        out_shape=[jax.ShapeDtypeStruct((t, d), F32), jax.ShapeDtypeStruct((8, 128), F32)],
        compiler_params=_cp(1),
    )(y, target)
    return l[0, 0], dy


def _sigmoid(v):
    return 1.0 / (1.0 + jnp.exp(-v))


def _halo_specs(tq, width, nt, halo):
    per = tq // halo
    last = nt * per - 1
    main = pl.BlockSpec((tq, width), lambda i: (i, 0))
    prev = pl.BlockSpec((halo, width), lambda i: (jnp.maximum(i * per - 1, 0), 0))
    nxt = pl.BlockSpec((halo, width), lambda i: (jnp.minimum((i + 1) * per, last), 0))
    return prev, main, nxt


def _glu_window(zp, zm, zn, i, nt):
    def glu(z):
        z = z.astype(F32)
        return z[:, :CONV_W] * _sigmoid(z[:, CONV_W:])

    up = jnp.where(i > 0, glu(zp), 0.0)
    un = jnp.where(i < nt - 1, glu(zn), 0.0)
    return jnp.concatenate([up, glu(zm), un], axis=0)


def _shifted(win, shift, rows):
    if shift == 0:
        return win[:rows]
    return pltpu.roll(win, win.shape[0] - shift, 0)[:rows]


def _conv_fwd(zc, w, b, ln_g, ln_b, name, tq=512):
    t = zc.shape[0]
    tq = min(tq, t)
    nt = t // tq

    def body(zp_ref, zm_ref, zn_ref, w_ref, b_ref, g_ref, bb_ref, y_ref, c_ref):
        i = pl.program_id(0)
        win = _glu_window(zp_ref[...], zm_ref[...], zn_ref[...], i, nt)
        wv = w_ref[...]
        y = jnp.zeros((tq, CONV_W), F32) + b_ref[...]
        for k in range(CONV_K):
            y = y + _shifted(win, k + 1, tq) * wv[k:k + 1, :]
        y_ref[...] = y
        mu = jnp.mean(y, axis=-1, keepdims=True)
        yc = y - mu
        ln = yc * lax.rsqrt(jnp.mean(yc * yc, axis=-1, keepdims=True) + LN_EPS) * g_ref[...] + bb_ref[...]
        c_ref[...] = (ln * _sigmoid(ln)).astype(BF16)

    vec = pl.BlockSpec((1, CONV_W), lambda i: (0, 0))
    out = pl.BlockSpec((tq, CONV_W), lambda i: (i, 0))
    return pl.pallas_call(
        body, name=name, grid=(nt,),
        in_specs=[*_halo_specs(tq, 2 * CONV_W, nt, CONV_PAD), pl.BlockSpec((32, CONV_W), lambda i: (0, 0)), vec, vec, vec],
        out_specs=[out, out],
        out_shape=[jax.ShapeDtypeStruct((t, CONV_W), F32), jax.ShapeDtypeStruct((t, CONV_W), BF16)],
        compiler_params=_cp(1),
    )(zc, zc, zc, w, _row(b), _row(ln_g), _row(ln_b))


def _conv_bwd(zc, y, dc, w, ln_g, ln_b, name, tq=256):
    t = zc.shape[0]
    tq = min(tq, t)
    nt = t // tq
    rows = tq + 2 * CONV_PAD

    def body(zp_ref, zm_ref, zn_ref, yp_ref, ym_ref, yn_ref, dp_ref, dm_ref, dn_ref, w_ref, g_ref, bb_ref,
             dz_ref, dw_ref, db_ref, dg_ref, dbb_ref):
        i = pl.program_id(0)
        uwin = _glu_window(zp_ref[...], zm_ref[...], zn_ref[...], i, nt)
        ywin = jnp.concatenate([yp_ref[...], ym_ref[...], yn_ref[...]], axis=0)
        dcw = jnp.concatenate([dp_ref[...], dm_ref[...], dn_ref[...]], axis=0).astype(F32)
        mu = jnp.mean(ywin, axis=-1, keepdims=True)
        yc = ywin - mu
        rstd = lax.rsqrt(jnp.mean(yc * yc, axis=-1, keepdims=True) + LN_EPS)
        yhat = yc * rstd
        ln = yhat * g_ref[...] + bb_ref[...]
        sg = _sigmoid(ln)
        dl = dcw * (sg * (1.0 + ln * (1.0 - sg)))
        ridx = lax.broadcasted_iota(jnp.int32, (rows, 1), 0)
        inside = ((ridx >= CONV_PAD) | (i > 0)) & ((ridx < tq + CONV_PAD) | (i < nt - 1))
        dl = jnp.where(inside, dl, 0.0)
        dyh = dl * g_ref[...]
        dy = rstd * (dyh - jnp.mean(dyh, axis=-1, keepdims=True) - yhat * jnp.mean(dyh * yhat, axis=-1, keepdims=True))
        dy = jnp.where(inside, dy, 0.0)
        main = slice(CONV_PAD, CONV_PAD + tq)
        dlm, yhm, dym = dl[main], yhat[main], dy[main]

        @pl.when(i == 0)
        def _():
            dw_ref[...] = jnp.zeros_like(dw_ref)
            db_ref[...] = jnp.zeros_like(db_ref)
            dg_ref[...] = jnp.zeros_like(dg_ref)
            dbb_ref[...] = jnp.zeros_like(dbb_ref)

        dg_ref[0:1, :] += jnp.sum(dlm * yhm, axis=0, keepdims=True)
        dbb_ref[0:1, :] += jnp.sum(dlm, axis=0, keepdims=True)
        db_ref[0:1, :] += jnp.sum(dym, axis=0, keepdims=True)
        wv = w_ref[...]
        du = jnp.zeros((tq, CONV_W), F32)
        for k in range(CONV_K):
            du = du + _shifted(dy, 2 * CONV_PAD - 1 - k, tq) * wv[k:k + 1, :]
            dw_ref[k:k + 1, :] += jnp.sum(dym * _shifted(uwin, k + 1, tq), axis=0, keepdims=True)
        zm = zm_ref[...].astype(F32)
        a, gt = zm[:, :CONV_W], zm[:, CONV_W:]
        sgt = _sigmoid(gt)
        dz_ref[:, :CONV_W] = (du * sgt).astype(BF16)
        dz_ref[:, CONV_W:] = (du * a * sgt * (1.0 - sgt)).astype(BF16)

    vec = pl.BlockSpec((1, CONV_W), lambda i: (0, 0))
    acc = pl.BlockSpec((8, CONV_W), lambda i: (0, 0))
    dz, dw, db, dg, dbb = pl.pallas_call(
        body, name=name, grid=(nt,),
        in_specs=[*_halo_specs(tq, 2 * CONV_W, nt, CONV_PAD), *_halo_specs(tq, CONV_W, nt, CONV_PAD),
                  *_halo_specs(tq, CONV_W, nt, CONV_PAD), pl.BlockSpec((32, CONV_W), lambda i: (0, 0)), vec, vec],
        out_specs=[pl.BlockSpec((tq, 2 * CONV_W), lambda i: (i, 0)), pl.BlockSpec((32, CONV_W), lambda i: (0, 0)), acc, acc, acc],
        out_shape=[jax.ShapeDtypeStruct((t, 2 * CONV_W), BF16), jax.ShapeDtypeStruct((32, CONV_W), F32),
                   jax.ShapeDtypeStruct((8, CONV_W), F32), jax.ShapeDtypeStruct((8, CONV_W), F32),
                   jax.ShapeDtypeStruct((8, CONV_W), F32)],
        compiler_params=_cp(1),
    )(zc, zc, zc, y, y, y, dc, dc, dc, w, _row(ln_g), _row(ln_b))
    return dz, dw, db[0], dg[0], dbb[0]


def _att_tiles(seq_len):
    tq = min(256, seq_len)
    return tq, min(128, tq)


def _t5_bucket_np(rel):
    nb = NUM_BUCKETS // 2
    max_exact = nb // 2
    ret = np.where(rel > 0, nb, 0)
    n = np.abs(rel)
    nf = np.maximum(n, 1).astype(np.float32)
    large = max_exact + (np.log(nf / np.float32(max_exact)) / np.float32(np.log(MAX_DISTANCE / max_exact))
                         * np.float32(nb - max_exact)).astype(np.int32)
    large = np.minimum(large, nb - 1)
    return ret + np.where(n < max_exact, n, large)


def _bucket_index(sb, dilation):
    off = np.arange(sb + 2 * ATT_RADIUS)[None, :] - ATT_RADIUS - np.arange(sb)[:, None]
    idx = _t5_bucket_np(off * dilation).astype(np.int32)
    return np.where(np.abs(off) <= ATT_RADIUS, idx, -1).astype(np.int32)


def _bias_table(tab, idx, name):
    sb, w = idx.shape

    def body(tab_ref, idx_ref, o_ref):
        ix = idx_ref[...]
        for h in range(HEADS):
            acc = jnp.full((sb, w), NEG_INF, F32)
            for b in range(NUM_BUCKETS):
                acc = jnp.where(ix == b, tab_ref[b, h], acc)
            o_ref[h] = acc

    return pl.pallas_call(
        body, name=name, out_shape=jax.ShapeDtypeStruct((HEADS, sb, w), F32),
        in_specs=[pl.BlockSpec(memory_space=pltpu.SMEM), pl.BlockSpec(memory_space=pltpu.VMEM)],
        out_specs=pl.BlockSpec(memory_space=pltpu.VMEM),
    )(tab, idx)


def _qkv_specs(tq, per, last_blk, cols, tile):
    def main(col):
        return pl.BlockSpec((tq, ATT_GW), lambda *g: (tile(*g), col))

    def prev(col):
        return pl.BlockSpec((ATT_RADIUS, ATT_GW), lambda *g: (jnp.maximum(tile(*g) * per - 1, 0), col))

    def nxt(col):
        return pl.BlockSpec((ATT_RADIUS, ATT_GW), lambda *g: (jnp.minimum((tile(*g) + 1) * per, last_blk), col))

    cq, ck, cv = cols
    return [main(cq), prev(ck), main(ck), nxt(ck), prev(cv), main(cv), nxt(cv)]


def _att_fwd(q, k, v, bias, seq_len, name, cols=(0, 0, 0)):
    t = q.shape[0]
    tq, sb = _att_tiles(seq_len)
    nbl = seq_len // tq
    nt = t // tq
    per = tq // ATT_RADIUS
    w = sb + 2 * ATT_RADIUS
    last_blk = t // ATT_RADIUS - 1

    def body(q_ref, kp_ref, kc_ref, kn_ref, vp_ref, vc_ref, vn_ref, b_ref, o_ref, l_ref):
        n = pl.program_id(0) % nbl
        kw = jnp.concatenate([kp_ref[...], kc_ref[...], kn_ref[...]], axis=0)
        vw = jnp.concatenate([vp_ref[...], vc_ref[...], vn_ref[...]], axis=0)
        kpos = n * tq - ATT_RADIUS + lax.broadcasted_iota(jnp.int32, (1, tq + 2 * ATT_RADIUS), 1)
        valid = (kpos >= 0) & (kpos < seq_len)
        for j in range(tq // sb):
            qj = q_ref[j * sb:(j + 1) * sb, :]
            kj, vj, okj = kw[j * sb:j * sb + w], vw[j * sb:j * sb + w], valid[:, j * sb:j * sb + w]
            outs, lses = [], []
            for h in range(HEADS):
                hs = slice(h * HEAD_DIM, (h + 1) * HEAD_DIM)
                s = lax.dot_general(qj[:, hs], kj[:, hs], NT, preferred_element_type=F32)
                s = jnp.where(okj, s * (HEAD_DIM ** -0.5) + b_ref[h], NEG_INF)
                m = jnp.max(s, axis=-1, keepdims=True)
                e = jnp.exp(s - m)
                den = jnp.sum(e, axis=-1, keepdims=True)
                outs.append(jnp.dot(e.astype(BF16), vj[:, hs], preferred_element_type=F32) / den)
                lses.append(jnp.broadcast_to(m + jnp.log(den), (sb, HEAD_DIM)))
            o_ref[j * sb:(j + 1) * sb, :] = jnp.concatenate(outs, axis=1).astype(BF16)
            l_ref[j * sb:(j + 1) * sb, :] = jnp.concatenate(lses, axis=1)

    main = pl.BlockSpec((tq, ATT_GW), lambda i: (i, 0))
    return pl.pallas_call(
        body, name=name, grid=(nt,),
        in_specs=[*_qkv_specs(tq, per, last_blk, cols, lambda i: i), pl.BlockSpec((HEADS, sb, w), lambda i: (0, 0, 0))],
        out_specs=[main, main],
        out_shape=[jax.ShapeDtypeStruct((t, ATT_GW), BF16), jax.ShapeDtypeStruct((t, ATT_GW), F32)],
        compiler_params=_cp(1),
    )(q, k, k, k, v, v, v, bias)


PERM_ROWS = 2048
LANES = 128


def _spec_rm(d, tr, width, col=0):
    return pl.BlockSpec((d, tr // d, width), lambda i: (0, i, col))


def _rm_view(a, d):
    return a.reshape(d, a.shape[0] // d, a.shape[1])


def _gather_residues(scr, val, d, out_ref, col0):
    tr, w = val.shape
    for c in range(w // LANES):
        scr[...] = val[:, c * LANES:(c + 1) * LANES].astype(F32)
        for r in range(d):
            out_ref[r, :, col0 + c * LANES:col0 + (c + 1) * LANES] = scr[pl.ds(r, tr // d, stride=d), :].astype(out_ref.dtype)


def _scatter_residues(scr, ref, d):
    w = ref.shape[2]
    cols = []
    for c in range(w // LANES):
        for r in range(d):
            scr[pl.ds(r, scr.shape[0] // d, stride=d), :] = ref[r, :, c * LANES:(c + 1) * LANES].astype(F32)
        cols.append(scr[...])
    return jnp.concatenate(cols, axis=1)


def _to_residue_major(srcs, d, name, tr=PERM_ROWS):
    t = srcs[0][0].shape[0]
    tr = min(tr, t)
    n = len(srcs)

    def body(*refs):
        scr = refs[2 * n]
        for k in range(n):
            _gather_residues(scr, refs[k][...], d, refs[n + k], 0)

    outs = pl.pallas_call(
        body, name=name, grid=(t // tr,),
        in_specs=[pl.BlockSpec((tr, ATT_GW), functools.partial(lambda i, col: (i, col), col=col)) for _, col in srcs],
        out_specs=[_spec_rm(d, tr, ATT_GW)] * n,
        out_shape=[jax.ShapeDtypeStruct((d, t // d, ATT_GW), a.dtype) for a, _ in srcs],
        scratch_shapes=[pltpu.VMEM((tr, LANES), F32)], compiler_params=_cp(1),
    )(*[a for a, _ in srcs])
    return [o.reshape(t, ATT_GW) for o in outs]


def _att_combine(outs, lses, name, tr=PERM_ROWS):
    t = outs[0].shape[0]
    tr = min(tr, t)
    dils = [d for _, d in ATT_PATTERNS]

    def body(o0, o1, o2, l0, l1, l2, a_ref, lse_ref, scr):
        os_, ls_ = [], []
        for o_ref, l_ref, d in zip((o0, o1, o2), (l0, l1, l2), dils):
            if d == 1:
                os_.append(o_ref[...].astype(F32))
                ls_.append(l_ref[...])
            else:
                os_.append(_scatter_residues(scr, o_ref, d))
                ls_.append(_scatter_residues(scr, l_ref, d))
        la, lb, lc = ls_
        m = jnp.maximum(jnp.maximum(la, lb), lc)
        ea, eb, ec = jnp.exp(la - m), jnp.exp(lb - m), jnp.exp(lc - m)
        den = ea + eb + ec
        a_ref[...] = ((ea * os_[0] + eb * os_[1] + ec * os_[2]) / den).astype(BF16)
        lse_ref[...] = m + jnp.log(den)

    row = pl.BlockSpec((tr, ATT_GW), lambda i: (i, 0))
    specs = [row if d == 1 else _spec_rm(d, tr, ATT_GW) for d in dils]
    views = lambda arrs: [a if d == 1 else _rm_view(a, d) for a, d in zip(arrs, dils)]
    return pl.pallas_call(
        body, name=name, grid=(t // tr,), in_specs=specs * 2, out_specs=[row, row],
        out_shape=[jax.ShapeDtypeStruct((t, ATT_GW), BF16), jax.ShapeDtypeStruct((t, ATT_GW), F32)],
        scratch_shapes=[pltpu.VMEM((tr, LANES), F32)], compiler_params=_cp(1),
    )(*views(outs), *views(lses))


def _assemble_dza(dqkv, name, tr=PERM_ROWS):
    t = dqkv[0][0].shape[0]
    tr = min(tr, t)
    dils = [d for _, d in ATT_PATTERNS]

    def body(*refs):
        o_ref, scr = refs[9], refs[10]
        for g, d in enumerate(dils):
            for c in range(3):
                ref = refs[g * 3 + c]
                val = ref[...] if d == 1 else _scatter_residues(scr, ref, d).astype(BF16)
                o_ref[:, (c * 3 + g) * ATT_GW:(c * 3 + g + 1) * ATT_GW] = val

    row = pl.BlockSpec((tr, ATT_GW), lambda i: (i, 0))
    specs = [row if d == 1 else _spec_rm(d, tr, ATT_GW) for d in dils for _ in range(3)]
    args = [a if d == 1 else _rm_view(a, d) for trio, d in zip(dqkv, dils) for a in trio]
    return pl.pallas_call(
        body, name=name, grid=(t // tr,), in_specs=specs, out_specs=pl.BlockSpec((tr, 3 * ATT_W), lambda i: (i, 0)),
        out_shape=jax.ShapeDtypeStruct((t, 3 * ATT_W), BF16),
        scratch_shapes=[pltpu.VMEM((tr, LANES), F32)], compiler_params=_cp(1),
    )(*args)


def _att_delta(o, do, name, tr=1024):
    t = o.shape[0]
    tr = min(tr, t)

    def body(o_ref, do_ref, d_ref):
        dd = o_ref[...].astype(F32) * do_ref[...].astype(F32)
        d_ref[...] = jnp.concatenate(
            [jnp.broadcast_to(jnp.sum(dd[:, h * HEAD_DIM:(h + 1) * HEAD_DIM], axis=-1, keepdims=True), (tr, HEAD_DIM))
             for h in range(HEADS)], axis=1)

    row = pl.BlockSpec((tr, ATT_GW), lambda i: (i, 0))
    return pl.pallas_call(
        body, name=name, grid=(t // tr,), in_specs=[row, row], out_specs=row,
        out_shape=jax.ShapeDtypeStruct((t, ATT_GW), F32), compiler_params=_cp(1),
    )(o, do)


def _att_bwd(q, k, v, bias, delta, do, lse, seq_len, name, cols=(0, 0, 0)):
    t = q.shape[0]
    tq, sb = _att_tiles(seq_len)
    nbl = seq_len // tq
    n_res = t // seq_len
    per = tq // ATT_RADIUS
    w = sb + 2 * ATT_RADIUS
    last_blk = t // ATT_RADIUS - 1
    acc_cols = 2 * tq - sb + w
    scale = HEAD_DIM ** -0.5

    def body(q_ref, kp_ref, kc_ref, kn_ref, vp_ref, vc_ref, vn_ref, b_ref, dl_ref, do_ref, l_ref,
             dq_ref, dk_ref, dv_ref, db_ref, ak_ref, av_ref):
        r, n = pl.program_id(0), pl.program_id(1)

        @pl.when((r == 0) & (n == 0))
        def _():
            db_ref[...] = jnp.zeros_like(db_ref)

        @pl.when(n == 0)
        def _():
            ak_ref[...] = jnp.zeros_like(ak_ref)
            av_ref[...] = jnp.zeros_like(av_ref)

        @pl.when(n < nbl)
        def _():
            kw = jnp.concatenate([kp_ref[...], kc_ref[...], kn_ref[...]], axis=0)
            vw = jnp.concatenate([vp_ref[...], vc_ref[...], vn_ref[...]], axis=0)
            kpos = n * tq - ATT_RADIUS + lax.broadcasted_iota(jnp.int32, (1, tq + 2 * ATT_RADIUS), 1)
            valid = (kpos >= 0) & (kpos < seq_len)
            for j in range(tq // sb):
                rows = slice(j * sb, (j + 1) * sb)
                qj, doj = q_ref[rows, :], do_ref[rows, :]
                dlj = dl_ref[rows, :]
                lj = l_ref[rows, :]
                kj, vj, okj = kw[j * sb:j * sb + w], vw[j * sb:j * sb + w], valid[:, j * sb:j * sb + w]
                heads = [slice(h * HEAD_DIM, (h + 1) * HEAD_DIM) for h in range(HEADS)]
                ss = [lax.dot_general(qj[:, hs], kj[:, hs], NT, preferred_element_type=F32) for hs in heads]
                dps = [lax.dot_general(doj[:, hs], vj[:, hs], NT, preferred_element_type=F32) for hs in heads]
                ps, dss, dsbs = [], [], []
                for h, hs in enumerate(heads):
                    s = jnp.where(okj, ss[h] * scale + b_ref[h], NEG_INF)
                    p = jnp.exp(s - lj[:, h * HEAD_DIM:h * HEAD_DIM + 1])
                    ds = p * (dps[h] - dlj[:, h * HEAD_DIM:h * HEAD_DIM + 1])
                    ps.append(p.astype(BF16))
                    dss.append(ds)
                    dsbs.append(ds.astype(BF16))
                dqs = [jnp.dot(dsbs[h], kj[:, hs], preferred_element_type=F32) * scale for h, hs in enumerate(heads)]
                dks = [lax.dot_general(qj[:, hs], dsbs[h], (((0,), (0,)), ((), ())), preferred_element_type=F32) * scale
                       for h, hs in enumerate(heads)]
                dvs = [lax.dot_general(doj[:, hs], ps[h], (((0,), (0,)), ((), ())), preferred_element_type=F32)
                       for h, hs in enumerate(heads)]
                dq_ref[rows, :] = jnp.concatenate(dqs, axis=1).astype(BF16)
                c0 = tq + j * sb
                ak_ref[:, c0:c0 + w] += jnp.concatenate(dks, axis=0)
                av_ref[:, c0:c0 + w] += jnp.concatenate(dvs, axis=0)
                for h in range(HEADS):
                    db_ref[h] += dss[h]

        dk_ref[...] = ak_ref[:, ATT_RADIUS:ATT_RADIUS + tq].T.astype(BF16)
        dv_ref[...] = av_ref[:, ATT_RADIUS:ATT_RADIUS + tq].T.astype(BF16)
        keep = acc_cols - tq
        nk, nv = ak_ref[:, tq:acc_cols], av_ref[:, tq:acc_cols]
        ak_ref[:, 0:keep] = nk
        av_ref[:, 0:keep] = nv
        ak_ref[:, keep:acc_cols] = jnp.zeros((ATT_GW, tq), F32)
        av_ref[:, keep:acc_cols] = jnp.zeros((ATT_GW, tq), F32)

    def tile(r, n):
        return r * nbl + jnp.minimum(n, nbl - 1)

    main = pl.BlockSpec((tq, ATT_GW), lambda r, n: (tile(r, n), 0))
    lag = pl.BlockSpec((tq, ATT_GW), lambda r, n: (r * nbl + jnp.maximum(n - 1, 0), 0))
    bspec = pl.BlockSpec((HEADS, sb, w), lambda r, n: (0, 0, 0))
    return pl.pallas_call(
        body, name=name, grid=(n_res, nbl + 1),
        in_specs=[*_qkv_specs(tq, per, last_blk, cols, tile), bspec, main, main, main],
        out_specs=[main, lag, lag, bspec],
        out_shape=[jax.ShapeDtypeStruct((t, ATT_GW), BF16)] * 3 + [jax.ShapeDtypeStruct((HEADS, sb, w), F32)],
        scratch_shapes=[pltpu.VMEM((ATT_GW, acc_cols), F32), pltpu.VMEM((ATT_GW, acc_cols), F32)],
        compiler_params=_cp(2),
    )(q, k, k, k, v, v, v, bias, delta, do, lse)


def _bias_grad(db, idx, name):
    _, sb, w = db.shape

    def body(db_ref, idx_ref, o_ref):
        ix = idx_ref[...]
        lane = lax.broadcasted_iota(jnp.int32, (1, 128), 1)
        rows = []
        for h in range(HEADS):
            d = db_ref[h]
            acc = jnp.zeros((1, 128), F32)
            for b in range(NUM_BUCKETS):
                acc = acc + jnp.where(lane == b, jnp.sum(jnp.where(ix == b, d, 0.0)), 0.0)
            rows.append(acc)
        o_ref[...] = jnp.concatenate(rows + [jnp.zeros((8 - HEADS, 128), F32)], axis=0)

    return pl.pallas_call(
        body, name=name, out_shape=jax.ShapeDtypeStruct((8, 128), F32),
        in_specs=[pl.BlockSpec(memory_space=pltpu.VMEM), pl.BlockSpec(memory_space=pltpu.VMEM)],
        out_specs=pl.BlockSpec(memory_space=pltpu.VMEM),
    )(db, idx)


def _mem_fwd(zq, kv, name, tq=512):
    t = zq.shape[0]
    tq = min(tq, t)
    scale = MEM_HD ** -0.5

    def body(q_ref, kv_ref, o_ref):
        outs = []
        for h in range(MEM_HEADS):
            hs = slice(h * MEM_HD, (h + 1) * MEM_HD)
            kh = kv_ref[:, h * MEM_HD:(h + 1) * MEM_HD]
            vh = kv_ref[:, MEM_W + h * MEM_HD:MEM_W + (h + 1) * MEM_HD]
            s = lax.dot_general(q_ref[:, hs], kh, (((1,), (1,)), ((), ())), preferred_element_type=F32) * scale
            e = jnp.exp(s - jnp.max(s, axis=-1, keepdims=True))
            den = jnp.sum(e, axis=-1, keepdims=True)
            outs.append(jnp.dot(e.astype(BF16), vh, preferred_element_type=F32) / den)
        o_ref[...] = jnp.concatenate(outs, axis=1).astype(BF16)

    row = pl.BlockSpec((tq, MEM_W), lambda i: (i, 0))
    return pl.pallas_call(
        body, name=name, grid=(t // tq,),
        in_specs=[row, pl.BlockSpec((N_MEM, 2 * MEM_W), lambda i: (0, 0))], out_specs=row,
        out_shape=jax.ShapeDtypeStruct((t, MEM_W), BF16), compiler_params=_cp(1),
    )(zq, kv)


def _mem_bwd(zq, kv, do, name, tq=512):
    t = zq.shape[0]
    tq = min(tq, t)
    nt = t // tq
    scale = MEM_HD ** -0.5

    def body(q_ref, kv_ref, do_ref, dq_ref, out_ref, dkv_ref):
        @pl.when(pl.program_id(0) == 0)
        def _():
            dkv_ref[...] = jnp.zeros_like(dkv_ref)

        ks = [slice(h * MEM_HD, (h + 1) * MEM_HD) for h in range(MEM_HEADS)]
        vs = [slice(MEM_W + h * MEM_HD, MEM_W + (h + 1) * MEM_HD) for h in range(MEM_HEADS)]
        ss = [lax.dot_general(q_ref[:, k], kv_ref[:, k], NT, preferred_element_type=F32) * scale for k in ks]
        dps = [lax.dot_general(do_ref[:, k], kv_ref[:, v], NT, preferred_element_type=F32) for k, v in zip(ks, vs)]
        pbs, dsbs = [], []
        for s, dp in zip(ss, dps):
            e = jnp.exp(s - jnp.max(s, axis=-1, keepdims=True))
            p = e / jnp.sum(e, axis=-1, keepdims=True)
            ds = p * (dp - jnp.sum(p * dp, axis=-1, keepdims=True))
            pbs.append(p.astype(BF16))
            dsbs.append(ds.astype(BF16))
        dq_ref[...] = jnp.concatenate([jnp.dot(dsb, kv_ref[:, k], preferred_element_type=F32) * scale
                                       for dsb, k in zip(dsbs, ks)], axis=1).astype(BF16)
        for dsb, pb, k, v in zip(dsbs, pbs, ks, vs):
            dkv_ref[k, :] += jnp.dot(q_ref[:, k].T, dsb, preferred_element_type=F32) * scale
            dkv_ref[v, :] += jnp.dot(do_ref[:, k].T, pb, preferred_element_type=F32)

        @pl.when(pl.program_id(0) == nt - 1)
        def _():
            out_ref[...] = dkv_ref[...].T

    row = pl.BlockSpec((tq, MEM_W), lambda i: (i, 0))
    full = pl.BlockSpec((N_MEM, 2 * MEM_W), lambda i: (0, 0))
    return pl.pallas_call(
        body, name=name, grid=(nt,), in_specs=[row, full, row], out_specs=[row, full],
        out_shape=[jax.ShapeDtypeStruct((t, MEM_W), BF16), jax.ShapeDtypeStruct((N_MEM, 2 * MEM_W), F32)],
        scratch_shapes=[pltpu.VMEM((2 * MEM_W, N_MEM), F32)], compiler_params=_cp(1),
    )(zq, kv, do)


NT = (((1,), (1,)), ((), ()))


def _branches_fwd(acts, ws, zg, bg, name, tr=512):
    t = zg.shape[0]
    tr = min(tr, t)

    def body(a0, a1, a2, w0, w1, w2, z_ref, b_ref, y0, y1, y2, o_ref):
        merged = None
        for i, (a_ref, w_ref, y_ref) in enumerate(((a0, w0, y0), (a1, w1, y1), (a2, w2, y2))):
            y = lax.dot_general(a_ref[...], w_ref[...], NT, preferred_element_type=F32)
            y_ref[...] = y.astype(BF16)
            g = _sigmoid(z_ref[:, i * D:(i + 1) * D].astype(F32) + b_ref[:, i * D:(i + 1) * D])
            merged = g * y if merged is None else merged + g * y
        o_ref[...] = merged.astype(BF16)

    row = pl.BlockSpec((tr, D), lambda i: (i, 0))
    outs = pl.pallas_call(
        body, name=name, grid=(t // tr,),
        in_specs=[pl.BlockSpec((tr, a.shape[1]), lambda i: (i, 0)) for a in acts]
        + [pl.BlockSpec(w.shape, lambda i: (0, 0)) for w in ws]
        + [pl.BlockSpec((tr, 3 * D), lambda i: (i, 0)), pl.BlockSpec((1, 3 * D), lambda i: (0, 0))],
        out_specs=[row] * 4, out_shape=[jax.ShapeDtypeStruct((t, D), BF16)] * 4, compiler_params=_cp(1),
    )(*acts, *ws, zg, _row(bg))
    return tuple(outs[:3]), outs[3]


def _branches_bwd(dymix, w_out, ws, zg, bg, ys, name, tr=512):
    t = zg.shape[0]
    tr = min(tr, t)

    def body(dy_ref, wo_ref, w0, w1, w2, z_ref, b_ref, y0, y1, y2, d0, d1, d2, dz_ref, db_ref, da0, da1, da2):
        @pl.when(pl.program_id(0) == 0)
        def _():
            db_ref[...] = jnp.zeros_like(db_ref)

        dm = lax.dot_general(dy_ref[...], wo_ref[...], NT, preferred_element_type=F32)
        for i, (y_ref, w_ref, d_ref, da_ref) in enumerate(((y0, w0, d0, da0), (y1, w1, d1, da1), (y2, w2, d2, da2))):
            gi = _sigmoid(z_ref[:, i * D:(i + 1) * D].astype(F32) + b_ref[:, i * D:(i + 1) * D])
            dy = (dm * gi).astype(BF16)
            d_ref[...] = dy
            dz = dm * y_ref[...].astype(F32) * gi * (1.0 - gi)
            dz_ref[:, i * D:(i + 1) * D] = dz.astype(BF16)
            db_ref[0:1, i * D:(i + 1) * D] += jnp.sum(dz, axis=0, keepdims=True)
            da_ref[...] = jnp.dot(dy, w_ref[...], preferred_element_type=F32).astype(BF16)

    row = pl.BlockSpec((tr, D), lambda i: (i, 0))
    wide = pl.BlockSpec((tr, 3 * D), lambda i: (i, 0))
    outs = pl.pallas_call(
        body, name=name, grid=(t // tr,),
        in_specs=[row, pl.BlockSpec(w_out.shape, lambda i: (0, 0))] + [pl.BlockSpec(w.shape, lambda i: (0, 0)) for w in ws]
        + [wide, pl.BlockSpec((1, 3 * D), lambda i: (0, 0)), row, row, row],
        out_specs=[row, row, row, wide, pl.BlockSpec((8, 3 * D), lambda i: (0, 0))]
        + [pl.BlockSpec((tr, w.shape[1]), lambda i: (i, 0)) for w in ws],
        out_shape=[jax.ShapeDtypeStruct((t, D), BF16)] * 3
        + [jax.ShapeDtypeStruct((t, 3 * D), BF16), jax.ShapeDtypeStruct((8, 3 * D), F32)]
        + [jax.ShapeDtypeStruct((t, w.shape[1]), BF16) for w in ws],
        compiler_params=_cp(1),
    )(dymix, w_out, *ws, zg, _row(bg), *ys)
    return tuple(outs[:3]), outs[3], outs[4][0], tuple(outs[5:])


def _adamw(w, g, m, v, name):
    shape = w.shape
    w, g, m, v = (a.reshape(-1, shape[-1]) for a in (w, g, m, v))
    r, d = w.shape
    tr = next((c for c in (512, 352, 256, 128, 64, 32, 16, 8) if r % c == 0), r)

    def body(w_ref, g_ref, m_ref, v_ref, d_ref, nm_ref, nv_ref):
        gg = g_ref[...]
        m2 = ADAM_B1 * m_ref[...] + (1.0 - ADAM_B1) * gg
        v2 = ADAM_B2 * v_ref[...] + (1.0 - ADAM_B2) * (gg * gg)
        m_hat = m2 / (1.0 - ADAM_B1 ** ADAM_STEP)
        v_hat = v2 / (1.0 - ADAM_B2 ** ADAM_STEP)
        d_ref[...] = -ADAM_LR * (m_hat / (jnp.sqrt(v_hat) + ADAM_EPS) + ADAM_WD * w_ref[...])
        nm_ref[...] = m2
        nv_ref[...] = v2

    row = pl.BlockSpec((tr, d), lambda i: (i, 0))
    outs = pl.pallas_call(
        body, name=name, grid=(r // tr,), in_specs=[row] * 4, out_specs=[row] * 3,
        out_shape=[jax.ShapeDtypeStruct((r, d), F32)] * 3, compiler_params=_cp(1),
    )(w, g, m, v)
    return [o.reshape(shape) for o in outs]


def _slab_tile(rows):
    return next((c for c in range(min(rows, 512), 15, -16) if rows % c == 0), rows)


def _sum_slots(parts, out_dtype, name):
    n, r, d = parts.shape
    tr = _slab_tile(r)

    def body(p_ref, o_ref):
        acc = p_ref[0].astype(F32)
        for s in range(1, n):
            acc = acc + p_ref[s].astype(F32)
        o_ref[...] = acc.astype(o_ref.dtype)

    return pl.pallas_call(
        body, name=name, grid=(r // tr,), in_specs=[pl.BlockSpec((n, tr, d), lambda i: (0, i, 0))],
        out_specs=pl.BlockSpec((tr, d), lambda i: (i, 0)),
        out_shape=jax.ShapeDtypeStruct((r, d), out_dtype), compiler_params=_cp(1),
    )(parts)


def _place():
    return lax.axis_index("x"), lax.axis_index("y"), lax.axis_index("c")


def _all_gather(shard, name):
    r, d = shard.shape

    def body(x_ref, out_ref, send_sems, recv_sems, local_sem):
        x, y, c = _place()
        me, sibling = (x, y, c), (x, y, 1 - c)
        chips = [(1 - x, y), (x, 1 - y), (1 - x, 1 - y)]

        def slot(px, py, pc):
            return out_ref.at[4 * px + 2 * py + pc]

        def copy(k, block, to, src=None):
            return pltpu.make_async_remote_copy(
                src_ref=slot(*block) if src is None else src, dst_ref=slot(*block),
                send_sem=send_sems.at[k], recv_sem=recv_sems.at[k], device_id=to, device_id_type=MESH)

        mine = pltpu.make_async_copy(x_ref, slot(*me), local_sem)
        mine.start()
        first = [copy(0, me, sibling, src=x_ref)]
        first += [copy(1 + j, me, (*chip, c), src=x_ref) for j, chip in enumerate(chips)]
        for cp in first:
            cp.start()
        passed = [copy(4 + j, (*chip, c), sibling) for j, chip in enumerate(chips)]
        for j, chip in enumerate(chips):
            copy(1 + j, (*chip, c), me).wait_recv()
            passed[j].start()
        copy(0, sibling, me).wait_recv()
        for j, chip in enumerate(chips):
            copy(4 + j, (*chip, 1 - c), me).wait_recv()
        for cp in first + passed:
            cp.wait_send()
        mine.wait()

    return pl.pallas_call(
        body, name=name, out_shape=jax.ShapeDtypeStruct((N_DEV, r, d), shard.dtype),
        in_specs=[pl.BlockSpec(memory_space=pl.ANY)], out_specs=pl.BlockSpec(memory_space=pl.ANY),
        scratch_shapes=[pltpu.SemaphoreType.DMA((7,)), pltpu.SemaphoreType.DMA((7,)), pltpu.SemaphoreType.DMA],
    )(shard)


HBM_SPEC = pl.BlockSpec(memory_space=pltpu.HBM)
SEM_SPEC = pl.BlockSpec(memory_space=pltpu.SEMAPHORE)
SPLIT_PARAMS = pltpu.CompilerParams(has_side_effects=pltpu.SideEffectType.DATAFLOW_SIDE_EFFECTING)


def _peers():
    x, y, c = _place()
    flip = lambda v, bit: 1 - v if bit else v
    return 4 * x + 2 * y + c, [((flip(x, k >> 2 & 1), flip(y, k >> 1 & 1), flip(c, k & 1))) for k in range(1, N_DEV)]


def _exchange_start(src, all_gather, name):
    r, d = src.shape[-2:]

    def body(src_ref, land_ref, send_sems, recv_sems, src_thru, land_thru, token):
        me, peers = _peers()
        for k, (px, py, pc) in enumerate(peers):
            part = src_ref if all_gather else src_ref.at[4 * px + 2 * py + pc]
            pltpu.make_async_remote_copy(src_ref=part, dst_ref=land_ref.at[me], send_sem=send_sems.at[k],
                                         recv_sem=recv_sems.at[k], device_id=(px, py, pc), device_id_type=MESH).start()
        token[...] = jnp.zeros_like(token)

    land = lax.empty((N_DEV, r, d), src.dtype)
    return pl.pallas_call(
        body, name=name,
        out_shape=(pltpu.SemaphoreType.DMA((N_DEV - 1,)), pltpu.SemaphoreType.DMA((N_DEV - 1,)), pltpu.HBM(src.shape, src.dtype),
                   pltpu.HBM(land.shape, land.dtype), jax.ShapeDtypeStruct((8, LANES), F32)),
        in_specs=(HBM_SPEC, HBM_SPEC), out_specs=(SEM_SPEC, SEM_SPEC, HBM_SPEC, HBM_SPEC, pl.BlockSpec(memory_space=pltpu.VMEM)),
        input_output_aliases={0: 2, 1: 3}, compiler_params=SPLIT_PARAMS,
    )(pltpu.with_memory_space_constraint(src, pltpu.HBM), pltpu.with_memory_space_constraint(land, pltpu.HBM))


def _exchange_wait(started, after, all_gather, name):
    send_sems, recv_sems, src_thru, land_thru, _ = started

    def body(src_ref, land_ref, send_sems, recv_sems, after_ref, src_out, land_out):
        me, peers = _peers()
        for k, (px, py, pc) in enumerate(peers):
            part = src_ref if all_gather else src_ref.at[4 * px + 2 * py + pc]
            cp = pltpu.make_async_remote_copy(src_ref=part, dst_ref=land_ref.at[4 * px + 2 * py + pc], send_sem=send_sems.at[k],
                                              recv_sem=recv_sems.at[k], device_id=(px, py, pc), device_id_type=MESH)
            cp.wait_send()
            cp.wait_recv()

    return pl.pallas_call(
        body, name=name,
        out_shape=(pltpu.HBM(src_thru.shape, src_thru.dtype), pltpu.HBM(land_thru.shape, land_thru.dtype)),
        in_specs=(HBM_SPEC, HBM_SPEC, SEM_SPEC, SEM_SPEC, pl.BlockSpec(memory_space=pl.ANY)), out_specs=(HBM_SPEC, HBM_SPEC),
        input_output_aliases={0: 0, 1: 1}, compiler_params=SPLIT_PARAMS,
    )(src_thru, land_thru, send_sems, recv_sems, after)


def _own_slot(land, mine):
    x, y, c = _place()
    return lax.dynamic_update_slice(land, mine[None], (4 * x + 2 * y + c, 0, 0))


BIG = (("w_in", (D, 864), 1), ("w_conv_out", (CONV_W, 128), 1), ("w_att_out", (ATT_GW, 128), 1), ("w_mem_kv", (128, D), 0),
       ("w_mem_out", (MEM_W, 128), 1), ("w_out", (128, D), 0), ("w_ffn_in", (D, 704), 1), ("w_ffn_out", (352, D), 0),
       ("conv_dw", (CONV_K, 64), 1))
SMALL = (("rel_bias", (NUM_BUCKETS, 12)), ("norm_mix_pre", (DEPTH, D)), ("b_gate", (DEPTH, 3 * D)),
         ("conv_dw_bias", (DEPTH, CONV_W)), ("conv_ln_g", (DEPTH, CONV_W)), ("conv_ln_b", (DEPTH, CONV_W)),
         ("norm_mem", (DEPTH, D)), ("norm_mix_post", (DEPTH, D)), ("norm_ffn_pre", (DEPTH, D)), ("norm_ffn_post", (DEPTH, D)))
TWIN_WEIGHTS = ("rel_bias", "norm_mix_pre", "w_in", "b_gate", "conv_dw", "conv_dw_bias", "conv_ln_g", "conv_ln_b", "w_conv_out",
                "w_att_out", "norm_mem", "w_mem_kv", "w_mem_out", "w_out", "norm_mix_post", "norm_ffn_pre", "w_ffn_in",
                "w_ffn_out", "norm_ffn_post")


BIG_INFO = {n: (s, a) for n, s, a in BIG}
GROUPS = {"in": ("w_in", "conv_dw"), "mix": ("w_conv_out", "w_att_out", "w_mem_kv", "w_mem_out", "w_out"),
          "ffn": ("w_ffn_in", "w_ffn_out")}
TILE_BYTES_PER_LANE = 32


def _rows_of(shape, dtype):
    tile = TILE_BYTES_PER_LANE // jnp.dtype(dtype).itemsize
    return -(-int(np.prod(shape)) // (D * tile)) * tile


def _row_form(shape, axis):
    return tuple(shape) if axis == 0 else (shape[1], shape[0])


def _as_slab_rows(p, lead=()):
    shape = p.shape[len(lead):]
    rows = _rows_of(shape, p.dtype)
    if shape == (rows, D):
        return p
    n = int(np.prod(shape))
    p = jnp.pad(p.reshape(*lead, n), [(0, 0)] * len(lead) + [(0, rows * D - n)])
    return p.reshape(*lead, rows, D)


def _from_slab_rows(rows, shape, lead=()):
    if rows.shape[len(lead):] == tuple(shape):
        return rows
    n = int(np.prod(shape))
    return rows.reshape(*lead, -1)[..., :n].reshape(*lead, *shape)


def _pack(pieces, dtype):
    return jnp.concatenate([_as_slab_rows(p.astype(dtype)) for p in pieces], axis=0)


def _unpack(slab, shapes, packed_as):
    out, r0 = [], 0
    for s in shapes:
        rows = _rows_of(s, packed_as)
        out.append(_from_slab_rows(slab[r0:r0 + rows], s))
        r0 += rows
    return out


def _pack_group(blocks, layer, group):
    return _pack([blocks[n][layer] if BIG_INFO[n][1] == 0 else blocks[n][layer].T for n in GROUPS[group]], BF16)


def _full_weights(gathered, group):
    out, r0 = {}, 0
    for name in GROUPS[group]:
        r, k = _row_form(*BIG_INFO[name])
        rows = _rows_of((r, k), BF16)
        out[name] = _from_slab_rows(gathered[:, r0:r0 + rows], (r, k), lead=(N_DEV,)).reshape(N_DEV * r, k)
        r0 += rows
    return out


def _scatter_layout(grads, group):
    parts = []
    for name in GROUPS[group]:
        r, k = _row_form(*BIG_INFO[name])
        parts.append(_as_slab_rows(grads[name].astype(BF16).reshape(N_DEV, r, k), lead=(N_DEV,)))
    return jnp.concatenate(parts, axis=1)


def _unpack_blocks(slab, group):
    out = _unpack(slab, [_row_form(*BIG_INFO[n]) for n in GROUPS[group]], BF16)
    return {n: b if BIG_INFO[n][1] == 0 else b.T for n, b in zip(GROUPS[group], out)}


def _split_w_in(w_in):
    edges = np.cumsum((0,) + SEG)
    return [w_in[edges[i]:edges[i + 1]] for i in range(4)]


def _conv_taps(conv_dw):
    return jnp.pad(conv_dw.astype(F32).T, ((0, 32 - CONV_K), (0, 0)))


def _layer_fwd(l, x, h, nm, get_w, sm, bias_tabs):
    t = x.shape[0]
    win = get_w(l, "in", h)
    w_seg = _split_w_in(win["w_in"])
    zc = _mm(h, w_seg[0], "fwd_in_conv", tb=True)
    za = _mm(h, w_seg[1], "fwd_in_att", tb=True, tn=1152)
    zq = _mm(h, w_seg[2], "fwd_in_memq", tb=True)
    zg = _mm(h, w_seg[3], "fwd_in_gate", tb=True)
    y, cact = _conv_fwd(zc, _conv_taps(win["conv_dw"]), sm["conv_dw_bias"][l], sm["conv_ln_g"][l], sm["conv_ln_b"][l],
                        "conv_fwd")
    wts = get_w(l, "mix", cact)
    qkv, outs, lses = [], [], []
    for g, (_, dil) in enumerate(ATT_PATTERNS):
        cols = tuple(i * 3 + g for i in range(3))
        if dil == 1:
            trio, at = (za, za, za), cols
        else:
            trio, at = _to_residue_major([(za, c) for c in cols], dil, f"qkv_residue_d{dil}"), (0, 0, 0)
        o, ls = _att_fwd(*trio, bias_tabs[g], t // dil, f"att_fwd_d{dil}", cols=at)
        qkv.append((trio, at))
        outs.append(o)
        lses.append(ls)
    att, lse = _att_combine(outs, lses, "att_combine")
    kv = _mm(nm, wts["w_mem_kv"], "fwd_mem_kv")
    om = _mem_fwd(zq, kv, "mem_fwd")
    w_br = (wts["w_conv_out"], wts["w_att_out"], wts["w_mem_out"])
    ys, merged = _branches_fwd((cact, att, om), w_br, zg, sm["b_gate"][l], "branches_fwd")
    ymix, x1, h2 = _mm_norm_res(merged, wts["w_out"], x, sm["norm_mix_post"][l], sm["norm_ffn_pre"][l], "fwd_out_norm")
    wff = get_w(l, "ffn", h2)
    gff, uff, act = _mm_swiglu_fwd(h2, wff["w_ffn_in"], "fwd_ffn_in_swiglu")
    g_next = sm["norm_mix_pre"][l + 1] if l + 1 < DEPTH else None
    f, x2, h_next = _mm_norm_res(act, wff["w_ffn_out"], x1, sm["norm_ffn_post"][l], g_next,
                                 "fwd_ffn_out_norm" if g_next is not None else "fwd_ffn_out_last")
    saved = dict(x=x, h=h, zc=zc, zq=zq, zg=zg, y=y, cact=cact, qkv=qkv, att=att, lse=lse, kv=kv, om=om, ys=ys,
                 merged=merged, ymix=ymix, x1=x1, h2=h2, gff=gff, uff=uff, act=act, f=f)
    return x2, h_next, saved


def _after(value, *tokens):
    for t in tokens:
        if t is not None:
            value = value + t[0, 0].astype(value.dtype)
    return value


def _layer_bwd(l, dx2, s, nm, mem, get_w, put_g, sm, bias_tabs, bucket_idx):
    t = dx2.shape[0]
    win, wts, wff = get_w(l, "in", None), get_w(l, "mix", None), get_w(l, "ffn", None)
    gi, gb, gf, gs = {}, {}, {}, {}
    df, gs["norm_ffn_post"] = _norm_bwd(s["f"], sm["norm_ffn_post"][l], dx2, None, BF16, "bwd_norm_ffn_post")
    dg, du = _mm_swiglu_bwd(df, wff["w_ffn_out"], s["gff"], s["uff"], "bwd_ffn_out_d_swiglu")
    gf["w_ffn_out"] = _mm(s["act"], df, "bwd_ffn_out_w", ta=True, tm=1408)
    gf["w_ffn_in"] = jnp.concatenate([_mm(dg, s["h2"], "bwd_ffn_in_w_gate", ta=True, tm=1408),
                                      _mm(du, s["h2"], "bwd_ffn_in_w_up", ta=True, tm=1408)], axis=0)
    tok = put_g(l, "ffn", gf)
    dh2 = _mm_sum([(dg, _after(wff["w_ffn_in"][:FFN_H], tok), FFN_H // 2), (du, wff["w_ffn_in"][FFN_H:], FFN_H // 2)],
                  "bwd_ffn_in_d")
    dx1, dymix, gs["norm_ffn_pre"], gs["norm_mix_post"] = _norm_bwd_pair(
        s["x1"], sm["norm_ffn_pre"][l], dh2, dx2, s["ymix"], sm["norm_mix_post"][l], "bwd_norm_ffn_pre_mix_post")
    gb["w_out"] = _mm(s["merged"], dymix, "bwd_out_w", ta=True)
    w_br = (wts["w_conv_out"], wts["w_att_out"], wts["w_mem_out"])
    dys, dzg, gs["b_gate"], (dcact, datt, dom) = _branches_bwd(dymix, wts["w_out"], w_br, s["zg"], sm["b_gate"][l], s["ys"],
                                                               "branches_bwd")
    gb["w_conv_out"] = _mm(dys[0], s["cact"], "bwd_conv_out_w", ta=True)
    dzc, dconv, gs["conv_dw_bias"], gs["conv_ln_g"], gs["conv_ln_b"] = _conv_bwd(
        s["zc"], s["y"], dcact, _conv_taps(win["conv_dw"]), sm["conv_ln_g"][l], sm["conv_ln_b"][l], "conv_bwd")
    gi["conv_dw"] = dconv[:CONV_K].T
    gb["w_att_out"] = _mm(dys[1], s["att"], "bwd_att_out_w", ta=True)
    delta = _att_delta(s["att"], datt, "att_delta")
    dqkv, rel = [], []
    for g, (_, dil) in enumerate(ATT_PATTERNS):
        trio, at = s["qkv"][g]
        if dil == 1:
            dl_r, do_r, lse_r = delta, datt, s["lse"]
        else:
            dl_r, do_r, lse_r = _to_residue_major([(delta, 0), (datt, 0), (s["lse"], 0)], dil, f"att_bwd_residue_d{dil}")
        dq, dk, dv, dbias = _att_bwd(*trio, bias_tabs[g], dl_r, do_r, lse_r, t // dil, f"att_bwd_d{dil}", cols=at)
        dqkv.append((dq, dk, dv))
        rel.append(_bias_grad(dbias, bucket_idx[g], f"bias_grad_d{dil}")[:HEADS, :NUM_BUCKETS])
    dza = _assemble_dza(dqkv, "att_bwd_assemble")
    gs["rel_bias"] = jnp.concatenate(rel, axis=0).T
    gb["w_mem_out"] = _mm(dys[2], s["om"], "bwd_mem_out_w", ta=True)
    dzq, dkv = _mem_bwd(s["zq"], s["kv"], dom, "mem_bwd")
    dkv = dkv.astype(BF16)
    gb["w_mem_kv"] = _mm(nm, dkv, "bwd_mem_kv_w", ta=True)
    tok = put_g(l, "mix", gb)
    dnm = _mm(dkv, _after(wts["w_mem_kv"], tok), "bwd_mem_kv_d", tb=True, out_dtype=F32)
    _, gs["norm_mem"] = _norm_bwd(mem, sm["norm_mem"][l], dnm, None, BF16, "bwd_norm_mem")
    segs = ((dzc, "conv", 1024), (dza, "att", 1152), (dzq, "memq", 512), (dzg, "gate", 1024))
    gi["w_in"] = jnp.concatenate([_mm(dz, s["h"], f"bwd_in_{nm_}_w", ta=True, tm=min(blk, 1152), tk=2048)
                                  for dz, nm_, blk in segs], axis=0)
    tok = put_g(l, "in", gi)
    w_seg = _split_w_in(win["w_in"])
    w_seg[0] = _after(w_seg[0], tok)
    dx, gs["norm_mix_pre"] = _mm_sum([(dz, w_seg[i], blk) for i, (dz, _, blk) in enumerate(segs)], "bwd_in_d_norm",
                                     norm=(s["x"], sm["norm_mix_pre"][l], dx1))
    return dx, gs


def _local_step(x, mem, target, sm, get_w, put_g, tokens=()):
    t = x.shape[0]
    bias_tabs, bucket_idx = [], []
    for g, (_, dil) in enumerate(ATT_PATTERNS):
        idx = _bucket_index(_att_tiles(t // dil)[1], dil)
        bucket_idx.append(jnp.asarray(idx))
        bias_tabs.append(_bias_table(sm["rel_bias"][:, g * HEADS:(g + 1) * HEADS], bucket_idx[g], f"bias_table_d{dil}"))
    h = _norm_plain(x, _after(sm["norm_mix_pre"][0], *tokens), "norm_first")
    saved, nms = [], []
    for l in range(DEPTH):
        nm = _norm_plain(mem, sm["norm_mem"][l], "norm_mem")
        x, h, s = _layer_fwd(l, x, h, nm, get_w, sm, bias_tabs)
        saved.append(s)
        nms.append(nm)
    loss, dx = _loss_head(x, target, "loss_head")
    gsmall = {}
    for l in reversed(range(DEPTH)):
        dx, gs = _layer_bwd(l, dx, saved[l], nms[l], mem, get_w, put_g, sm, bias_tabs, bucket_idx)
        for n, v in gs.items():
            gsmall.setdefault(n, {})[l] = v
    small = {}
    for n, _ in SMALL:
        small[n] = gsmall[n][0] + gsmall[n][1] if n == "rel_bias" else jnp.stack([gsmall[n][0], gsmall[n][1]])
    return loss, dx, small


def kernel(x, mem, rel_bias, norm_mix_pre, w_in, b_gate, conv_dw, conv_dw_bias, conv_ln_g, conv_ln_b, w_conv_out, w_att_out, norm_mem, w_mem_kv, w_mem_out, w_out, norm_mix_post, norm_ffn_pre, w_ffn_in, w_ffn_out, norm_ffn_post, loss_target, m_rel_bias, m_norm_mix_pre, m_w_in, m_b_gate, m_conv_dw, m_conv_dw_bias, m_conv_ln_g, m_conv_ln_b, m_w_conv_out, m_w_att_out, m_norm_mem, m_w_mem_kv, m_w_mem_out, m_w_out, m_norm_mix_post, m_norm_ffn_pre, m_w_ffn_in, m_w_ffn_out, m_norm_ffn_post, v_rel_bias, v_norm_mix_pre, v_w_in, v_b_gate, v_conv_dw, v_conv_dw_bias, v_conv_ln_g, v_conv_ln_b, v_w_conv_out, v_w_att_out, v_norm_mem, v_w_mem_kv, v_w_mem_out, v_w_out, v_norm_mix_post, v_norm_ffn_pre, v_w_ffn_in, v_w_ffn_out, v_norm_ffn_post):
    args = dict(locals())
    w = {n: args[n] for n in TWIN_WEIGHTS}
    m = {n: args["m_" + n] for n in TWIN_WEIGHTS}
    v = {n: args["v_" + n] for n in TWIN_WEIGHTS}
    sm = {n: w[n] for n, _ in SMALL}

    first = (0, "in")
    keys = [(l, grp) for l in range(DEPTH) for grp in GROUPS]
    shard = {(l, grp): _pack_group(w, l, grp) for l, grp in keys}
    weights = {first: _full_weights(_all_gather(shard[first], "gather_l0_in"), "in")}
    settled = weights[first]["w_in"][:1, :1] * 0
    ag = {k: _exchange_start(_after(shard[k], settled), True, f"ag_start_l{k[0]}_{k[1]}") for k in keys if k != first}

    def get_w(l, grp, after):
        if (l, grp) not in weights:
            mine, land = _exchange_wait(ag[l, grp], after, True, f"ag_wait_l{l}_{grp}")
            weights[l, grp] = _full_weights(_own_slot(land, mine), grp)
        return weights[l, grp]

    rs = {}

    def put_g(l, grp, grads):
        rs[l, grp] = _exchange_start(_scatter_layout(grads, grp), False, f"rs_start_l{l}_{grp}")
        return rs[l, grp][4]

    loss, dx, gsmall = _local_step(x[0], mem[0], loss_target[0], sm, get_w, put_g, [st[4] for st in ag.values()])
    loss = lax.psum(loss, ("x", "y", "c"))

    xi, yi, ci = _place()
    me = 4 * xi + 2 * yi + ci
    g_slab = {}
    for k, st in rs.items():
        contrib, land = _exchange_wait(st, dx, False, f"rs_wait_l{k[0]}_{k[1]}")
        own = lax.dynamic_index_in_dim(contrib, me, axis=0, keepdims=False)
        g_slab[k] = _sum_slots(_own_slot(land, own), F32, f"rs_sum_l{k[0]}_{k[1]}")
    g_layers = [{n: b for grp in GROUPS for n, b in _unpack_blocks(g_slab[l, grp], grp).items()} for l in range(DEPTH)]
    small_shapes = [s for _, s in SMALL]
    g_small = _unpack(_sum_slots(_all_gather(_pack([gsmall[n] for n, _ in SMALL], F32), "gather_small"), F32, "sum_small"),
                      small_shapes, F32)
    grads = {n: jnp.stack([g_layers[0][n], g_layers[1][n]]) for n, _, _ in BIG}
    grads.update({n: g_small[i] for i, (n, _) in enumerate(SMALL)})

    delta, new_m, new_v = {}, {}, {}
    for n in TWIN_WEIGHTS:
        delta[n], new_m[n], new_v[n] = _adamw(w[n], grads[n], m[n], v[n], f"adamw_{n}")

    return (loss, dx[None], *[grads[n] for n in TWIN_WEIGHTS], *[delta[n] for n in TWIN_WEIGHTS],
            *[new_m[n] for n in TWIN_WEIGHTS], *[new_v[n] for n in TWIN_WEIGHTS])
```

```python
import functools

import numpy as np
import jax
import jax.numpy as jnp
from jax import lax
from jax.experimental import pallas as pl
from jax.experimental.pallas import tpu as pltpu

F32 = jnp.float32
BF16 = jnp.bfloat16

N_DEV = 8
D = 1024
DEPTH = 2
CONV_W = 512
CONV_K = 31
CONV_PAD = 16
ATT_PATTERNS = ((128, 1), (512, 4), (2048, 16))
ATT_RADIUS = 64
HEADS = 4
HEAD_DIM = 64
ATT_GW = HEADS * HEAD_DIM
ATT_W = 3 * ATT_GW
MEM_HEADS = 4
MEM_HD = 128
MEM_W = 512
N_MEM = 256
FFN_H = 2816
NUM_BUCKETS = 32
MAX_DISTANCE = 1024
RMS_EPS = 1e-6
LN_EPS = 1e-5
NEG_INF = -1e30
SEG = (2 * CONV_W, 3 * ATT_W, MEM_W, 3 * D)
ADAM_LR, ADAM_B1, ADAM_B2, ADAM_EPS, ADAM_WD, ADAM_STEP = 0.001, 0.9, 0.999, 1e-08, 0.01, 10

VMEM_LIMIT_V7X = 56 * 1024 * 1024
MESH = pl.DeviceIdType.MESH


def _cp(n_axes):
    return pltpu.CompilerParams(dimension_semantics=("arbitrary",) * n_axes, vmem_limit_bytes=VMEM_LIMIT_V7X)


def _row(v):
    return v.reshape(1, -1)


def _mm(a, b, name, *, ta=False, tb=False, out_dtype=BF16, tm=2048, tn=1024, tk=2048):
    kdim, m = a.shape if ta else a.shape[::-1]
    n, kb = b.shape if tb else b.shape[::-1]
    assert kb == kdim and a.dtype == BF16 and b.dtype == BF16
    tm, tn, tk = min(tm, m), min(tn, n), min(tk, kdim)
    assert m % tm == 0 and n % tn == 0 and kdim % tk == 0, (name, m, n, kdim, tm, tn, tk)
    nk = kdim // tk
    a_spec = pl.BlockSpec((tk, tm), lambda i, j, k: (k, i)) if ta else pl.BlockSpec((tm, tk), lambda i, j, k: (i, k))
    b_spec = pl.BlockSpec((tn, tk), lambda i, j, k: (j, k)) if tb else pl.BlockSpec((tk, tn), lambda i, j, k: (k, j))
    o_spec = pl.BlockSpec((tm, tn), lambda i, j, k: (i, j))
    dims = (((0 if ta else 1,), (1 if tb else 0,)), ((), ()))
    use_scratch = nk > 1 and out_dtype != F32

    def body(*refs):
        a_ref, b_ref, o_ref = refs[:3]
        p = lax.dot_general(a_ref[...], b_ref[...], dims, preferred_element_type=F32)
        if nk == 1:
            o_ref[...] = p.astype(o_ref.dtype)
            return
        k = pl.program_id(2)
        t_ref = refs[3] if use_scratch else o_ref

        @pl.when(k == 0)
        def _():
            t_ref[...] = p

        @pl.when(k > 0)
        def _():
            t_ref[...] += p

        if use_scratch:
            @pl.when(k == nk - 1)
            def _():
                o_ref[...] = t_ref[...].astype(o_ref.dtype)

    return pl.pallas_call(
        body, name=name, grid=(m // tm, n // tn, nk), in_specs=[a_spec, b_spec], out_specs=o_spec,
        out_shape=jax.ShapeDtypeStruct((m, n), out_dtype),
        scratch_shapes=[pltpu.VMEM((tm, tn), F32)] if use_scratch else [],
        compiler_params=_cp(3),
    )(a, b)


EPILOGUE_COLS = 512
EPILOGUE_ROWS = 512


def _chunks(size, step):
    return [(c, min(c + step, size)) for c in range(0, size, step)]


def _rms_bwd_val(v, g, dy):
    r = lax.rsqrt(jnp.mean(v * v, axis=-1, keepdims=True) + RMS_EPS)
    vhat = v * r
    dvh = dy * g
    dv = r * (dvh - vhat * jnp.mean(dvh * vhat, axis=-1, keepdims=True))
    return dv, jnp.sum(dy * vhat, axis=0, keepdims=True)


def _mm_sum(pairs, name, *, norm=None, tm=1024):
    m, n = pairs[0][0].shape[0], pairs[0][1].shape[1]
    tm = min(tm, m)
    starts, specs, total = [], [], 0
    for a, b, tk in pairs:
        assert a.shape == (m, b.shape[0]) and b.shape[1] == n and a.shape[1] % tk == 0 and a.dtype == b.dtype == BF16
        nk = a.shape[1] // tk
        starts.append((total, nk))
        step = functools.partial(lambda k, s0, nk_: jnp.clip(k - s0, 0, nk_ - 1), s0=total, nk_=nk)
        specs.append(pl.BlockSpec((tm, tk), functools.partial(lambda i, k, st: (i, st(k)), st=step)))
        specs.append(pl.BlockSpec((tk, n), functools.partial(lambda i, k, st: (st(k), 0), st=step)))
        total += nk
    np_ = 2 * len(pairs)
    row = pl.BlockSpec((tm, n), lambda i, k: (i, 0))

    def accumulate(refs, acc, k, upto=total):
        for p, (s0, nk) in enumerate(starts):
            @pl.when((k >= s0) & (k < min(s0 + nk, upto)))
            def _(p=p, s0=s0):
                part = jnp.dot(refs[2 * p][...], refs[2 * p + 1][...], preferred_element_type=F32)
                if s0 == 0:
                    @pl.when(k == 0)
                    def _():
                        acc[...] = part

                    @pl.when(k > 0)
                    def _():
                        acc[...] += part
                else:
                    acc[...] += part

    operands = [op for a, b, _ in pairs for op in (a, b)]
    if norm is None:
        def plain(*refs):
            accumulate(refs, refs[np_], pl.program_id(1))

        return pl.pallas_call(
            plain, name=name, grid=(m // tm, total), in_specs=specs, out_specs=row,
            out_shape=jax.ShapeDtypeStruct((m, n), F32), compiler_params=_cp(2),
        )(*operands)

    v, g, dres = norm

    def body(*refs):
        v_ref, g_ref, dres_ref, dv_ref, dg_ref, acc = refs[np_:np_ + 6]
        i, k = pl.program_id(0), pl.program_id(1)
        accumulate(refs, acc, k, upto=total - 1)

        @pl.when((i == 0) & (k == 0))
        def _():
            dg_ref[...] = jnp.zeros_like(dg_ref)

        @pl.when(k == total - 1)
        def _():
            a_ref, b_ref = refs[np_ - 2], refs[np_ - 1]
            for r0, r1 in _chunks(tm, EPILOGUE_ROWS):
                dy = jnp.dot(a_ref[r0:r1, :], b_ref[...], preferred_element_type=F32)
                if total > 1:
                    dy = dy + acc[r0:r1, :]
                dv, dg = _rms_bwd_val(v_ref[r0:r1, :], g_ref[...], dy)
                dv_ref[r0:r1, :] = dv + dres_ref[r0:r1, :]
                dg_ref[0:1, :] += dg

    once = pl.BlockSpec((tm, n), lambda i, k: (i, 0), pipeline_mode=pl.Buffered(1))
    dv, dg = pl.pallas_call(
        body, name=name, grid=(m // tm, total),
        in_specs=specs + [once, pl.BlockSpec((1, n), lambda i, k: (0, 0)), once],
        out_specs=[row, pl.BlockSpec((8, n), lambda i, k: (0, 0))],
        out_shape=[jax.ShapeDtypeStruct((m, n), F32), jax.ShapeDtypeStruct((8, n), F32)],
        scratch_shapes=[pltpu.VMEM((tm, n), F32)], compiler_params=_cp(2),
    )(*operands, v, _row(g), dres)
    return dv, dg[0]


def _mm_swiglu_fwd(h, w, name, *, tm=1024, tn=1408):
    m, kdim = h.shape
    hid = w.shape[0] // 2
    tm = min(tm, m)
    nj = hid // tn
    dims = (((1,), (1,)), ((), ()))

    def body(h_ref, wg_ref, wu_ref, g_ref, u_ref, a_ref):
        hv = h_ref[...]
        for c0, c1 in _chunks(tn, EPILOGUE_COLS):
            g = lax.dot_general(hv, wg_ref[c0:c1, :], dims, preferred_element_type=F32)
            u = lax.dot_general(hv, wu_ref[c0:c1, :], dims, preferred_element_type=F32)
            gb, ub = g.astype(BF16), u.astype(BF16)
            g_ref[:, c0:c1] = gb
            u_ref[:, c0:c1] = ub
            a_ref[:, c0:c1] = gb * _sigmoid(gb) * ub

    out = pl.BlockSpec((tm, tn), lambda i, j: (i, j))
    return pl.pallas_call(
        body, name=name, grid=(m // tm, nj),
        in_specs=[pl.BlockSpec((tm, kdim), lambda i, j: (i, 0)), pl.BlockSpec((tn, kdim), lambda i, j: (j, 0)),
                  pl.BlockSpec((tn, kdim), lambda i, j: (j + nj, 0))],
        out_specs=[out, out, out], out_shape=[jax.ShapeDtypeStruct((m, hid), BF16)] * 3, compiler_params=_cp(2),
    )(h, w, w)


def _mm_swiglu_bwd(df, w, g, u, name, *, tm=1024, tn=1408):
    m, kdim = df.shape
    hid = w.shape[0]
    tm = min(tm, m)

    def body(df_ref, w_ref, g_ref, u_ref, dg_ref, du_ref):
        dfv = df_ref[...]
        for c0, c1 in _chunks(tn, EPILOGUE_COLS):
            da = lax.dot_general(dfv, w_ref[c0:c1, :], (((1,), (1,)), ((), ())), preferred_element_type=F32)
            gg, uu, dab = g_ref[:, c0:c1], u_ref[:, c0:c1], da.astype(BF16)
            sg = _sigmoid(gg)
            dg_ref[:, c0:c1] = dab * uu * (sg * (1.0 + gg * (1.0 - sg)))
            du_ref[:, c0:c1] = dab * gg * sg

    blk = pl.BlockSpec((tm, tn), lambda i, j: (i, j))
    return pl.pallas_call(
        body, name=name, grid=(m // tm, hid // tn),
        in_specs=[pl.BlockSpec((tm, kdim), lambda i, j: (i, 0)), pl.BlockSpec((tn, kdim), lambda i, j: (j, 0)), blk, blk],
        out_specs=[blk, blk], out_shape=[jax.ShapeDtypeStruct((m, hid), BF16)] * 2, compiler_params=_cp(2),
    )(df, w, g, u)


def _mm_norm_res(a, b, xres, g, g_next, name, *, tm=1024):
    m, kdim = a.shape
    n = b.shape[1]
    tm = min(tm, m)
    two = g_next is not None

    def body(*refs):
        a_ref, b_ref, x_ref, g_ref = refs[:4]
        y_ref, xn_ref = refs[4 + two], refs[5 + two]
        for r0, r1 in _chunks(tm, EPILOGUE_ROWS):
            y = jnp.dot(a_ref[r0:r1, :], b_ref[...], preferred_element_type=F32)
            y_ref[r0:r1, :] = y.astype(BF16)
            xn = x_ref[r0:r1, :] + _rms_val(y, g_ref[...])
            xn_ref[r0:r1, :] = xn
            if two:
                refs[7][r0:r1, :] = _rms_val(xn, refs[4][...]).astype(BF16)

    row = pl.BlockSpec((tm, n), lambda i: (i, 0))
    vec = pl.BlockSpec((1, n), lambda i: (0, 0))
    outs = pl.pallas_call(
        body, name=name, grid=(m // tm,),
        in_specs=[pl.BlockSpec((tm, kdim), lambda i: (i, 0)), pl.BlockSpec((kdim, n), lambda i: (0, 0)), row, vec] + ([vec] if two else []),
        out_specs=[row, row] + ([row] if two else []),
        out_shape=[jax.ShapeDtypeStruct((m, n), BF16), jax.ShapeDtypeStruct((m, n), F32)]
        + ([jax.ShapeDtypeStruct((m, n), BF16)] if two else []),
        compiler_params=_cp(1),
    )(*((a, b, xres, _row(g)) + ((_row(g_next),) if two else ())))
    return (outs[0], outs[1], outs[2]) if two else (outs[0], outs[1], None)


def _rms_val(v, g):
    return v * lax.rsqrt(jnp.mean(v * v, axis=-1, keepdims=True) + RMS_EPS) * g


def _norm_plain(v, g, name, tr=512):
    t, d = v.shape
    tr = min(tr, t)

    def body(v_ref, g_ref, o_ref):
        o_ref[...] = _rms_val(v_ref[...], g_ref[...]).astype(BF16)

    return pl.pallas_call(
        body, name=name, grid=(t // tr,),
        in_specs=[pl.BlockSpec((tr, d), lambda i: (i, 0)), pl.BlockSpec((1, d), lambda i: (0, 0))],
        out_specs=pl.BlockSpec((tr, d), lambda i: (i, 0)),
        out_shape=jax.ShapeDtypeStruct((t, d), BF16), compiler_params=_cp(1),
    )(v, _row(g))


def _norm_bwd(v, g, dout, dres, out_dtype, name, tr=512):
    t, d = v.shape
    tr = min(tr, t)
    nt = t // tr
    has_res = dres is not None

    def body(*refs):
        v_ref, g_ref, do_ref = refs[:3]
        dv_ref, dg_ref = refs[3 + has_res], refs[4 + has_res]
        i = pl.program_id(0)
        vv = v_ref[...].astype(F32)
        dy = do_ref[...].astype(F32)
        r = lax.rsqrt(jnp.mean(vv * vv, axis=-1, keepdims=True) + RMS_EPS)
        vhat = vv * r
        dvh = dy * g_ref[...]
        dv = r * (dvh - vhat * jnp.mean(dvh * vhat, axis=-1, keepdims=True))
        if has_res:
            dv = dv + refs[3][...]
        dv_ref[...] = dv.astype(dv_ref.dtype)
        part = jnp.sum(dy * vhat, axis=0, keepdims=True)

        @pl.when(i == 0)
        def _():
            dg_ref[...] = jnp.zeros_like(dg_ref)

        dg_ref[0:1, :] += part

    row = pl.BlockSpec((tr, d), lambda i: (i, 0))
    dv, dg = pl.pallas_call(
        body, name=name, grid=(nt,),
        in_specs=[row, pl.BlockSpec((1, d), lambda i: (0, 0)), row] + ([row] if has_res else []),
        out_specs=[row, pl.BlockSpec((8, d), lambda i: (0, 0))],
        out_shape=[jax.ShapeDtypeStruct((t, d), out_dtype), jax.ShapeDtypeStruct((8, d), F32)],
        compiler_params=_cp(1),
    )(*((v, _row(g), dout) + ((dres,) if has_res else ())))
    return dv, dg[0]


def _norm_bwd_pair(v1, g1, dout, dres, v2, g2, name, tr=512):
    t, d = v1.shape
    tr = min(tr, t)

    def body(v1_ref, g1_ref, do_ref, dr_ref, v2_ref, g2_ref, d1_ref, d2_ref, dg1_ref, dg2_ref):
        @pl.when(pl.program_id(0) == 0)
        def _():
            dg1_ref[...] = jnp.zeros_like(dg1_ref)
            dg2_ref[...] = jnp.zeros_like(dg2_ref)

        dv1, dg1 = _rms_bwd_val(v1_ref[...].astype(F32), g1_ref[...], do_ref[...].astype(F32))
        d1 = dv1 + dr_ref[...]
        d1_ref[...] = d1
        dv2, dg2 = _rms_bwd_val(v2_ref[...].astype(F32), g2_ref[...], d1)
        d2_ref[...] = dv2.astype(BF16)
        dg1_ref[0:1, :] += dg1
        dg2_ref[0:1, :] += dg2

    row = pl.BlockSpec((tr, d), lambda i: (i, 0))
    vec = pl.BlockSpec((1, d), lambda i: (0, 0))
    acc = pl.BlockSpec((8, d), lambda i: (0, 0))
    d1, d2, dg1, dg2 = pl.pallas_call(
        body, name=name, grid=(t // tr,), in_specs=[row, vec, row, row, row, vec], out_specs=[row, row, acc, acc],
        out_shape=[jax.ShapeDtypeStruct((t, d), F32), jax.ShapeDtypeStruct((t, d), BF16),
                   jax.ShapeDtypeStruct((8, d), F32), jax.ShapeDtypeStruct((8, d), F32)],
        compiler_params=_cp(1),
    )(v1, _row(g1), dout, dres, v2, _row(g2))
    return d1, d2, dg1[0], dg2[0]


def _loss_head(y, target, name, tr=512):
    t, d = y.shape
    tr = min(tr, t)
    nt = t // tr

    def body(y_ref, t_ref, dy_ref, l_ref):
        i = pl.program_id(0)
        err = y_ref[...] - t_ref[...]
        dy_ref[...] = err * (1.0 / d)

        @pl.when(i == 0)
        def _():
            l_ref[...] = jnp.zeros_like(l_ref)

        l_ref[...] += jnp.sum(err * err) * (0.5 / d)

    row = pl.BlockSpec((tr, d), lambda i: (i, 0))
    dy, l = pl.pallas_call(
        body, name=name, grid=(nt,), in_specs=[row, row],
        out_specs=[row, pl.BlockSpec((8, 128), lambda i: (0, 0))],
        out_shape=[jax.ShapeDtypeStruct((t, d), F32), jax.ShapeDtypeStruct((8, 128), F32)],
        compiler_params=_cp(1),
    )(y, target)
    return l[0, 0], dy


def _sigmoid(v):
    return 1.0 / (1.0 + jnp.exp(-v))


def _halo_specs(tq, width, nt, halo):
    per = tq // halo
    last = nt * per - 1
    main = pl.BlockSpec((tq, width), lambda i: (i, 0))
    prev = pl.BlockSpec((halo, width), lambda i: (jnp.maximum(i * per - 1, 0), 0))
    nxt = pl.BlockSpec((halo, width), lambda i: (jnp.minimum((i + 1) * per, last), 0))
    return prev, main, nxt


def _glu_window(zp, zm, zn, i, nt):
    def glu(z):
        z = z.astype(F32)
        return z[:, :CONV_W] * _sigmoid(z[:, CONV_W:])

    up = jnp.where(i > 0, glu(zp), 0.0)
    un = jnp.where(i < nt - 1, glu(zn), 0.0)
    return jnp.concatenate([up, glu(zm), un], axis=0)


def _shifted(win, shift, rows):
    if shift == 0:
        return win[:rows]
    return pltpu.roll(win, win.shape[0] - shift, 0)[:rows]


def _conv_fwd(zc, w, b, ln_g, ln_b, name, tq=512):
    t = zc.shape[0]
    tq = min(tq, t)
    nt = t // tq

    def body(zp_ref, zm_ref, zn_ref, w_ref, b_ref, g_ref, bb_ref, y_ref, c_ref):
        i = pl.program_id(0)
        win = _glu_window(zp_ref[...], zm_ref[...], zn_ref[...], i, nt)
        wv = w_ref[...]
        y = jnp.zeros((tq, CONV_W), F32) + b_ref[...]
        for k in range(CONV_K):
            y = y + _shifted(win, k + 1, tq) * wv[k:k + 1, :]
        y_ref[...] = y
        mu = jnp.mean(y, axis=-1, keepdims=True)
        yc = y - mu
        ln = yc * lax.rsqrt(jnp.mean(yc * yc, axis=-1, keepdims=True) + LN_EPS) * g_ref[...] + bb_ref[...]
        c_ref[...] = (ln * _sigmoid(ln)).astype(BF16)

    vec = pl.BlockSpec((1, CONV_W), lambda i: (0, 0))
    out = pl.BlockSpec((tq, CONV_W), lambda i: (i, 0))
    return pl.pallas_call(
        body, name=name, grid=(nt,),
        in_specs=[*_halo_specs(tq, 2 * CONV_W, nt, CONV_PAD), pl.BlockSpec((32, CONV_W), lambda i: (0, 0)), vec, vec, vec],
        out_specs=[out, out],
        out_shape=[jax.ShapeDtypeStruct((t, CONV_W), F32), jax.ShapeDtypeStruct((t, CONV_W), BF16)],
        compiler_params=_cp(1),
    )(zc, zc, zc, w, _row(b), _row(ln_g), _row(ln_b))


def _conv_bwd(zc, y, dc, w, ln_g, ln_b, name, tq=256):
    t = zc.shape[0]
    tq = min(tq, t)
    nt = t // tq
    rows = tq + 2 * CONV_PAD

    def body(zp_ref, zm_ref, zn_ref, yp_ref, ym_ref, yn_ref, dp_ref, dm_ref, dn_ref, w_ref, g_ref, bb_ref,
             dz_ref, dw_ref, db_ref, dg_ref, dbb_ref):
        i = pl.program_id(0)
        uwin = _glu_window(zp_ref[...], zm_ref[...], zn_ref[...], i, nt)
        ywin = jnp.concatenate([yp_ref[...], ym_ref[...], yn_ref[...]], axis=0)
        dcw = jnp.concatenate([dp_ref[...], dm_ref[...], dn_ref[...]], axis=0).astype(F32)
        mu = jnp.mean(ywin, axis=-1, keepdims=True)
        yc = ywin - mu
        rstd = lax.rsqrt(jnp.mean(yc * yc, axis=-1, keepdims=True) + LN_EPS)
        yhat = yc * rstd
        ln = yhat * g_ref[...] + bb_ref[...]
        sg = _sigmoid(ln)
        dl = dcw * (sg * (1.0 + ln * (1.0 - sg)))
        ridx = lax.broadcasted_iota(jnp.int32, (rows, 1), 0)
        inside = ((ridx >= CONV_PAD) | (i > 0)) & ((ridx < tq + CONV_PAD) | (i < nt - 1))
        dl = jnp.where(inside, dl, 0.0)
        dyh = dl * g_ref[...]
        dy = rstd * (dyh - jnp.mean(dyh, axis=-1, keepdims=True) - yhat * jnp.mean(dyh * yhat, axis=-1, keepdims=True))
        dy = jnp.where(inside, dy, 0.0)
        main = slice(CONV_PAD, CONV_PAD + tq)
        dlm, yhm, dym = dl[main], yhat[main], dy[main]

        @pl.when(i == 0)
        def _():
            dw_ref[...] = jnp.zeros_like(dw_ref)
            db_ref[...] = jnp.zeros_like(db_ref)
            dg_ref[...] = jnp.zeros_like(dg_ref)
            dbb_ref[...] = jnp.zeros_like(dbb_ref)

        dg_ref[0:1, :] += jnp.sum(dlm * yhm, axis=0, keepdims=True)
        dbb_ref[0:1, :] += jnp.sum(dlm, axis=0, keepdims=True)
        db_ref[0:1, :] += jnp.sum(dym, axis=0, keepdims=True)
        wv = w_ref[...]
        du = jnp.zeros((tq, CONV_W), F32)
        for k in range(CONV_K):
            du = du + _shifted(dy, 2 * CONV_PAD - 1 - k, tq) * wv[k:k + 1, :]
            dw_ref[k:k + 1, :] += jnp.sum(dym * _shifted(uwin, k + 1, tq), axis=0, keepdims=True)
        zm = zm_ref[...].astype(F32)
        a, gt = zm[:, :CONV_W], zm[:, CONV_W:]
        sgt = _sigmoid(gt)
        dz_ref[:, :CONV_W] = (du * sgt).astype(BF16)
        dz_ref[:, CONV_W:] = (du * a * sgt * (1.0 - sgt)).astype(BF16)

    vec = pl.BlockSpec((1, CONV_W), lambda i: (0, 0))
    acc = pl.BlockSpec((8, CONV_W), lambda i: (0, 0))
    dz, dw, db, dg, dbb = pl.pallas_call(
        body, name=name, grid=(nt,),
        in_specs=[*_halo_specs(tq, 2 * CONV_W, nt, CONV_PAD), *_halo_specs(tq, CONV_W, nt, CONV_PAD),
                  *_halo_specs(tq, CONV_W, nt, CONV_PAD), pl.BlockSpec((32, CONV_W), lambda i: (0, 0)), vec, vec],
        out_specs=[pl.BlockSpec((tq, 2 * CONV_W), lambda i: (i, 0)), pl.BlockSpec((32, CONV_W), lambda i: (0, 0)), acc, acc, acc],
        out_shape=[jax.ShapeDtypeStruct((t, 2 * CONV_W), BF16), jax.ShapeDtypeStruct((32, CONV_W), F32),
                   jax.ShapeDtypeStruct((8, CONV_W), F32), jax.ShapeDtypeStruct((8, CONV_W), F32),
                   jax.ShapeDtypeStruct((8, CONV_W), F32)],
        compiler_params=_cp(1),
    )(zc, zc, zc, y, y, y, dc, dc, dc, w, _row(ln_g), _row(ln_b))
    return dz, dw, db[0], dg[0], dbb[0]


def _att_tiles(seq_len):
    tq = min(256, seq_len)
    return tq, min(128, tq)


def _t5_bucket_np(rel):
    nb = NUM_BUCKETS // 2
    max_exact = nb // 2
    ret = np.where(rel > 0, nb, 0)
    n = np.abs(rel)
    nf = np.maximum(n, 1).astype(np.float32)
    large = max_exact + (np.log(nf / np.float32(max_exact)) / np.float32(np.log(MAX_DISTANCE / max_exact))
                         * np.float32(nb - max_exact)).astype(np.int32)
    large = np.minimum(large, nb - 1)
    return ret + np.where(n < max_exact, n, large)


def _bucket_index(sb, dilation):
    off = np.arange(sb + 2 * ATT_RADIUS)[None, :] - ATT_RADIUS - np.arange(sb)[:, None]
    idx = _t5_bucket_np(off * dilation).astype(np.int32)
    return np.where(np.abs(off) <= ATT_RADIUS, idx, -1).astype(np.int32)


def _bias_table(tab, idx, name):
    sb, w = idx.shape

    def body(tab_ref, idx_ref, o_ref):
        ix = idx_ref[...]
        for h in range(HEADS):
            acc = jnp.full((sb, w), NEG_INF, F32)
            for b in range(NUM_BUCKETS):
                acc = jnp.where(ix == b, tab_ref[b, h], acc)
            o_ref[h] = acc

    return pl.pallas_call(
        body, name=name, out_shape=jax.ShapeDtypeStruct((HEADS, sb, w), F32),
        in_specs=[pl.BlockSpec(memory_space=pltpu.SMEM), pl.BlockSpec(memory_space=pltpu.VMEM)],
        out_specs=pl.BlockSpec(memory_space=pltpu.VMEM),
    )(tab, idx)


def _qkv_specs(tq, per, last_blk, cols, tile):
    def main(col):
        return pl.BlockSpec((tq, ATT_GW), lambda *g: (tile(*g), col))

    def prev(col):
        return pl.BlockSpec((ATT_RADIUS, ATT_GW), lambda *g: (jnp.maximum(tile(*g) * per - 1, 0), col))

    def nxt(col):
        return pl.BlockSpec((ATT_RADIUS, ATT_GW), lambda *g: (jnp.minimum((tile(*g) + 1) * per, last_blk), col))

    cq, ck, cv = cols
    return [main(cq), prev(ck), main(ck), nxt(ck), prev(cv), main(cv), nxt(cv)]


def _att_fwd(q, k, v, bias, seq_len, name, cols=(0, 0, 0)):
    t = q.shape[0]
    tq, sb = _att_tiles(seq_len)
    nbl = seq_len // tq
    nt = t // tq
    per = tq // ATT_RADIUS
    w = sb + 2 * ATT_RADIUS
    last_blk = t // ATT_RADIUS - 1

    def body(q_ref, kp_ref, kc_ref, kn_ref, vp_ref, vc_ref, vn_ref, b_ref, o_ref, l_ref):
        n = pl.program_id(0) % nbl
        kw = jnp.concatenate([kp_ref[...], kc_ref[...], kn_ref[...]], axis=0)
        vw = jnp.concatenate([vp_ref[...], vc_ref[...], vn_ref[...]], axis=0)
        kpos = n * tq - ATT_RADIUS + lax.broadcasted_iota(jnp.int32, (1, tq + 2 * ATT_RADIUS), 1)
        valid = (kpos >= 0) & (kpos < seq_len)
        for j in range(tq // sb):
            qj = q_ref[j * sb:(j + 1) * sb, :]
            kj, vj, okj = kw[j * sb:j * sb + w], vw[j * sb:j * sb + w], valid[:, j * sb:j * sb + w]
            outs, lses = [], []
            for h in range(HEADS):
                hs = slice(h * HEAD_DIM, (h + 1) * HEAD_DIM)
                s = lax.dot_general(qj[:, hs], kj[:, hs], NT, preferred_element_type=F32)
                s = jnp.where(okj, s * (HEAD_DIM ** -0.5) + b_ref[h], NEG_INF)
                m = jnp.max(s, axis=-1, keepdims=True)
                e = jnp.exp(s - m)
                den = jnp.sum(e, axis=-1, keepdims=True)
                outs.append(jnp.dot(e.astype(BF16), vj[:, hs], preferred_element_type=F32) / den)
                lses.append(jnp.broadcast_to(m + jnp.log(den), (sb, HEAD_DIM)))
            o_ref[j * sb:(j + 1) * sb, :] = jnp.concatenate(outs, axis=1).astype(BF16)
            l_ref[j * sb:(j + 1) * sb, :] = jnp.concatenate(lses, axis=1)

    main = pl.BlockSpec((tq, ATT_GW), lambda i: (i, 0))
    return pl.pallas_call(
        body, name=name, grid=(nt,),
        in_specs=[*_qkv_specs(tq, per, last_blk, cols, lambda i: i), pl.BlockSpec((HEADS, sb, w), lambda i: (0, 0, 0))],
        out_specs=[main, main],
        out_shape=[jax.ShapeDtypeStruct((t, ATT_GW), BF16), jax.ShapeDtypeStruct((t, ATT_GW), F32)],
        compiler_params=_cp(1),
    )(q, k, k, k, v, v, v, bias)


PERM_ROWS = 2048
LANES = 128


def _spec_rm(d, tr, width, col=0):
    return pl.BlockSpec((d, tr // d, width), lambda i: (0, i, col))


def _rm_view(a, d):
    return a.reshape(d, a.shape[0] // d, a.shape[1])


def _gather_residues(scr, val, d, out_ref, col0):
    tr, w = val.shape
    for c in range(w // LANES):
        scr[...] = val[:, c * LANES:(c + 1) * LANES].astype(F32)
        for r in range(d):
            out_ref[r, :, col0 + c * LANES:col0 + (c + 1) * LANES] = scr[pl.ds(r, tr // d, stride=d), :].astype(out_ref.dtype)


def _scatter_residues(scr, ref, d):
    w = ref.shape[2]
    cols = []
    for c in range(w // LANES):
        for r in range(d):
            scr[pl.ds(r, scr.shape[0] // d, stride=d), :] = ref[r, :, c * LANES:(c + 1) * LANES].astype(F32)
        cols.append(scr[...])
    return jnp.concatenate(cols, axis=1)


def _to_residue_major(srcs, d, name, tr=PERM_ROWS):
    t = srcs[0][0].shape[0]
    tr = min(tr, t)
    n = len(srcs)

    def body(*refs):
        scr = refs[2 * n]
        for k in range(n):
            _gather_residues(scr, refs[k][...], d, refs[n + k], 0)

    outs = pl.pallas_call(
        body, name=name, grid=(t // tr,),
        in_specs=[pl.BlockSpec((tr, ATT_GW), functools.partial(lambda i, col: (i, col), col=col)) for _, col in srcs],
        out_specs=[_spec_rm(d, tr, ATT_GW)] * n,
        out_shape=[jax.ShapeDtypeStruct((d, t // d, ATT_GW), a.dtype) for a, _ in srcs],
        scratch_shapes=[pltpu.VMEM((tr, LANES), F32)], compiler_params=_cp(1),
    )(*[a for a, _ in srcs])
    return [o.reshape(t, ATT_GW) for o in outs]


def _att_combine(outs, lses, name, tr=PERM_ROWS):
    t = outs[0].shape[0]
    tr = min(tr, t)
    dils = [d for _, d in ATT_PATTERNS]

    def body(o0, o1, o2, l0, l1, l2, a_ref, lse_ref, scr):
        os_, ls_ = [], []
        for o_ref, l_ref, d in zip((o0, o1, o2), (l0, l1, l2), dils):
            if d == 1:
                os_.append(o_ref[...].astype(F32))
                ls_.append(l_ref[...])
            else:
                os_.append(_scatter_residues(scr, o_ref, d))
                ls_.append(_scatter_residues(scr, l_ref, d))
        la, lb, lc = ls_
        m = jnp.maximum(jnp.maximum(la, lb), lc)
        ea, eb, ec = jnp.exp(la - m), jnp.exp(lb - m), jnp.exp(lc - m)
        den = ea + eb + ec
        a_ref[...] = ((ea * os_[0] + eb * os_[1] + ec * os_[2]) / den).astype(BF16)
        lse_ref[...] = m + jnp.log(den)

    row = pl.BlockSpec((tr, ATT_GW), lambda i: (i, 0))
    specs = [row if d == 1 else _spec_rm(d, tr, ATT_GW) for d in dils]
    views = lambda arrs: [a if d == 1 else _rm_view(a, d) for a, d in zip(arrs, dils)]
    return pl.pallas_call(
        body, name=name, grid=(t // tr,), in_specs=specs * 2, out_specs=[row, row],
        out_shape=[jax.ShapeDtypeStruct((t, ATT_GW), BF16), jax.ShapeDtypeStruct((t, ATT_GW), F32)],
        scratch_shapes=[pltpu.VMEM((tr, LANES), F32)], compiler_params=_cp(1),
    )(*views(outs), *views(lses))


def _assemble_dza(dqkv, name, tr=PERM_ROWS):
    t = dqkv[0][0].shape[0]
    tr = min(tr, t)
    dils = [d for _, d in ATT_PATTERNS]

    def body(*refs):
        o_ref, scr = refs[9], refs[10]
        for g, d in enumerate(dils):
            for c in range(3):
                ref = refs[g * 3 + c]
                val = ref[...] if d == 1 else _scatter_residues(scr, ref, d).astype(BF16)
                o_ref[:, (c * 3 + g) * ATT_GW:(c * 3 + g + 1) * ATT_GW] = val

    row = pl.BlockSpec((tr, ATT_GW), lambda i: (i, 0))
    specs = [row if d == 1 else _spec_rm(d, tr, ATT_GW) for d in dils for _ in range(3)]
    args = [a if d == 1 else _rm_view(a, d) for trio, d in zip(dqkv, dils) for a in trio]
    return pl.pallas_call(
        body, name=name, grid=(t // tr,), in_specs=specs, out_specs=pl.BlockSpec((tr, 3 * ATT_W), lambda i: (i, 0)),
        out_shape=jax.ShapeDtypeStruct((t, 3 * ATT_W), BF16),
        scratch_shapes=[pltpu.VMEM((tr, LANES), F32)], compiler_params=_cp(1),
    )(*args)


def _att_delta(o, do, name, tr=1024):
    t = o.shape[0]
    tr = min(tr, t)

    def body(o_ref, do_ref, d_ref):
        dd = o_ref[...].astype(F32) * do_ref[...].astype(F32)
        d_ref[...] = jnp.concatenate(
            [jnp.broadcast_to(jnp.sum(dd[:, h * HEAD_DIM:(h + 1) * HEAD_DIM], axis=-1, keepdims=True), (tr, HEAD_DIM))
             for h in range(HEADS)], axis=1)

    row = pl.BlockSpec((tr, ATT_GW), lambda i: (i, 0))
    return pl.pallas_call(
        body, name=name, grid=(t // tr,), in_specs=[row, row], out_specs=row,
        out_shape=jax.ShapeDtypeStruct((t, ATT_GW), F32), compiler_params=_cp(1),
    )(o, do)


def _att_bwd(q, k, v, bias, delta, do, lse, seq_len, name, cols=(0, 0, 0)):
    t = q.shape[0]
    tq, sb = _att_tiles(seq_len)
    nbl = seq_len // tq
    n_res = t // seq_len
    per = tq // ATT_RADIUS
    w = sb + 2 * ATT_RADIUS
    last_blk = t // ATT_RADIUS - 1
    acc_cols = 2 * tq - sb + w
    scale = HEAD_DIM ** -0.5

    def body(q_ref, kp_ref, kc_ref, kn_ref, vp_ref, vc_ref, vn_ref, b_ref, dl_ref, do_ref, l_ref,
             dq_ref, dk_ref, dv_ref, db_ref, ak_ref, av_ref):
        r, n = pl.program_id(0), pl.program_id(1)

        @pl.when((r == 0) & (n == 0))
        def _():
            db_ref[...] = jnp.zeros_like(db_ref)

        @pl.when(n == 0)
        def _():
            ak_ref[...] = jnp.zeros_like(ak_ref)
            av_ref[...] = jnp.zeros_like(av_ref)

        @pl.when(n < nbl)
        def _():
            kw = jnp.concatenate([kp_ref[...], kc_ref[...], kn_ref[...]], axis=0)
            vw = jnp.concatenate([vp_ref[...], vc_ref[...], vn_ref[...]], axis=0)
            kpos = n * tq - ATT_RADIUS + lax.broadcasted_iota(jnp.int32, (1, tq + 2 * ATT_RADIUS), 1)
            valid = (kpos >= 0) & (kpos < seq_len)
            for j in range(tq // sb):
                rows = slice(j * sb, (j + 1) * sb)
                qj, doj = q_ref[rows, :], do_ref[rows, :]
                dlj = dl_ref[rows, :]
                lj = l_ref[rows, :]
                kj, vj, okj = kw[j * sb:j * sb + w], vw[j * sb:j * sb + w], valid[:, j * sb:j * sb + w]
                heads = [slice(h * HEAD_DIM, (h + 1) * HEAD_DIM) for h in range(HEADS)]
                ss = [lax.dot_general(qj[:, hs], kj[:, hs], NT, preferred_element_type=F32) for hs in heads]
                dps = [lax.dot_general(doj[:, hs], vj[:, hs], NT, preferred_element_type=F32) for hs in heads]
                ps, dss, dsbs = [], [], []
                for h, hs in enumerate(heads):
                    s = jnp.where(okj, ss[h] * scale + b_ref[h], NEG_INF)
                    p = jnp.exp(s - lj[:, h * HEAD_DIM:h * HEAD_DIM + 1])
                    ds = p * (dps[h] - dlj[:, h * HEAD_DIM:h * HEAD_DIM + 1])
                    ps.append(p.astype(BF16))
                    dss.append(ds)
                    dsbs.append(ds.astype(BF16))
                dqs = [jnp.dot(dsbs[h], kj[:, hs], preferred_element_type=F32) * scale for h, hs in enumerate(heads)]
                dks = [lax.dot_general(qj[:, hs], dsbs[h], (((0,), (0,)), ((), ())), preferred_element_type=F32) * scale
                       for h, hs in enumerate(heads)]
                dvs = [lax.dot_general(doj[:, hs], ps[h], (((0,), (0,)), ((), ())), preferred_element_type=F32)
                       for h, hs in enumerate(heads)]
                dq_ref[rows, :] = jnp.concatenate(dqs, axis=1).astype(BF16)
                c0 = tq + j * sb
                ak_ref[:, c0:c0 + w] += jnp.concatenate(dks, axis=0)
                av_ref[:, c0:c0 + w] += jnp.concatenate(dvs, axis=0)
                for h in range(HEADS):
                    db_ref[h] += dss[h]

        dk_ref[...] = ak_ref[:, ATT_RADIUS:ATT_RADIUS + tq].T.astype(BF16)
        dv_ref[...] = av_ref[:, ATT_RADIUS:ATT_RADIUS + tq].T.astype(BF16)
        keep = acc_cols - tq
        nk, nv = ak_ref[:, tq:acc_cols], av_ref[:, tq:acc_cols]
        ak_ref[:, 0:keep] = nk
        av_ref[:, 0:keep] = nv
        ak_ref[:, keep:acc_cols] = jnp.zeros((ATT_GW, tq), F32)
        av_ref[:, keep:acc_cols] = jnp.zeros((ATT_GW, tq), F32)

    def tile(r, n):
        return r * nbl + jnp.minimum(n, nbl - 1)

    main = pl.BlockSpec((tq, ATT_GW), lambda r, n: (tile(r, n), 0))
    lag = pl.BlockSpec((tq, ATT_GW), lambda r, n: (r * nbl + jnp.maximum(n - 1, 0), 0))
    bspec = pl.BlockSpec((HEADS, sb, w), lambda r, n: (0, 0, 0))
    return pl.pallas_call(
        body, name=name, grid=(n_res, nbl + 1),
        in_specs=[*_qkv_specs(tq, per, last_blk, cols, tile), bspec, main, main, main],
        out_specs=[main, lag, lag, bspec],
        out_shape=[jax.ShapeDtypeStruct((t, ATT_GW), BF16)] * 3 + [jax.ShapeDtypeStruct((HEADS, sb, w), F32)],
        scratch_shapes=[pltpu.VMEM((ATT_GW, acc_cols), F32), pltpu.VMEM((ATT_GW, acc_cols), F32)],
        compiler_params=_cp(2),
    )(q, k, k, k, v, v, v, bias, delta, do, lse)


def _bias_grad(db, idx, name):
    _, sb, w = db.shape

    def body(db_ref, idx_ref, o_ref):
        ix = idx_ref[...]
        lane = lax.broadcasted_iota(jnp.int32, (1, 128), 1)
        rows = []
        for h in range(HEADS):
            d = db_ref[h]
            acc = jnp.zeros((1, 128), F32)
            for b in range(NUM_BUCKETS):
                acc = acc + jnp.where(lane == b, jnp.sum(jnp.where(ix == b, d, 0.0)), 0.0)
            rows.append(acc)
        o_ref[...] = jnp.concatenate(rows + [jnp.zeros((8 - HEADS, 128), F32)], axis=0)

    return pl.pallas_call(
        body, name=name, out_shape=jax.ShapeDtypeStruct((8, 128), F32),
        in_specs=[pl.BlockSpec(memory_space=pltpu.VMEM), pl.BlockSpec(memory_space=pltpu.VMEM)],
        out_specs=pl.BlockSpec(memory_space=pltpu.VMEM),
    )(db, idx)


def _mem_fwd(zq, kv, name, tq=512):
    t = zq.shape[0]
    tq = min(tq, t)
    scale = MEM_HD ** -0.5

    def body(q_ref, kv_ref, o_ref):
        outs = []
        for h in range(MEM_HEADS):
            hs = slice(h * MEM_HD, (h + 1) * MEM_HD)
            kh = kv_ref[:, h * MEM_HD:(h + 1) * MEM_HD]
            vh = kv_ref[:, MEM_W + h * MEM_HD:MEM_W + (h + 1) * MEM_HD]
            s = lax.dot_general(q_ref[:, hs], kh, (((1,), (1,)), ((), ())), preferred_element_type=F32) * scale
            e = jnp.exp(s - jnp.max(s, axis=-1, keepdims=True))
            den = jnp.sum(e, axis=-1, keepdims=True)
            outs.append(jnp.dot(e.astype(BF16), vh, preferred_element_type=F32) / den)
        o_ref[...] = jnp.concatenate(outs, axis=1).astype(BF16)

    row = pl.BlockSpec((tq, MEM_W), lambda i: (i, 0))
    return pl.pallas_call(
        body, name=name, grid=(t // tq,),
        in_specs=[row, pl.BlockSpec((N_MEM, 2 * MEM_W), lambda i: (0, 0))], out_specs=row,
        out_shape=jax.ShapeDtypeStruct((t, MEM_W), BF16), compiler_params=_cp(1),
    )(zq, kv)


def _mem_bwd(zq, kv, do, name, tq=512):
    t = zq.shape[0]
    tq = min(tq, t)
    nt = t // tq
    scale = MEM_HD ** -0.5

    def body(q_ref, kv_ref, do_ref, dq_ref, out_ref, dkv_ref):
        @pl.when(pl.program_id(0) == 0)
        def _():
            dkv_ref[...] = jnp.zeros_like(dkv_ref)

        ks = [slice(h * MEM_HD, (h + 1) * MEM_HD) for h in range(MEM_HEADS)]
        vs = [slice(MEM_W + h * MEM_HD, MEM_W + (h + 1) * MEM_HD) for h in range(MEM_HEADS)]
        ss = [lax.dot_general(q_ref[:, k], kv_ref[:, k], NT, preferred_element_type=F32) * scale for k in ks]
        dps = [lax.dot_general(do_ref[:, k], kv_ref[:, v], NT, preferred_element_type=F32) for k, v in zip(ks, vs)]
        pbs, dsbs = [], []
        for s, dp in zip(ss, dps):
            e = jnp.exp(s - jnp.max(s, axis=-1, keepdims=True))
            p = e / jnp.sum(e, axis=-1, keepdims=True)
            ds = p * (dp - jnp.sum(p * dp, axis=-1, keepdims=True))
            pbs.append(p.astype(BF16))
            dsbs.append(ds.astype(BF16))
        dq_ref[...] = jnp.concatenate([jnp.dot(dsb, kv_ref[:, k], preferred_element_type=F32) * scale
                                       for dsb, k in zip(dsbs, ks)], axis=1).astype(BF16)
        for dsb, pb, k, v in zip(dsbs, pbs, ks, vs):
            dkv_ref[k, :] += jnp.dot(q_ref[:, k].T, dsb, preferred_element_type=F32) * scale
            dkv_ref[v, :] += jnp.dot(do_ref[:, k].T, pb, preferred_element_type=F32)

        @pl.when(pl.program_id(0) == nt - 1)
        def _():
            out_ref[...] = dkv_ref[...].T

    row = pl.BlockSpec((tq, MEM_W), lambda i: (i, 0))
    full = pl.BlockSpec((N_MEM, 2 * MEM_W), lambda i: (0, 0))
    return pl.pallas_call(
        body, name=name, grid=(nt,), in_specs=[row, full, row], out_specs=[row, full],
        out_shape=[jax.ShapeDtypeStruct((t, MEM_W), BF16), jax.ShapeDtypeStruct((N_MEM, 2 * MEM_W), F32)],
        scratch_shapes=[pltpu.VMEM((2 * MEM_W, N_MEM), F32)], compiler_params=_cp(1),
    )(zq, kv, do)


NT = (((1,), (1,)), ((), ()))


def _branches_fwd(acts, ws, zg, bg, name, tr=512):
    t = zg.shape[0]
    tr = min(tr, t)

    def body(a0, a1, a2, w0, w1, w2, z_ref, b_ref, y0, y1, y2, o_ref):
        merged = None
        for i, (a_ref, w_ref, y_ref) in enumerate(((a0, w0, y0), (a1, w1, y1), (a2, w2, y2))):
            y = lax.dot_general(a_ref[...], w_ref[...], NT, preferred_element_type=F32)
            y_ref[...] = y.astype(BF16)
            g = _sigmoid(z_ref[:, i * D:(i + 1) * D].astype(F32) + b_ref[:, i * D:(i + 1) * D])
            merged = g * y if merged is None else merged + g * y
        o_ref[...] = merged.astype(BF16)

    row = pl.BlockSpec((tr, D), lambda i: (i, 0))
    outs = pl.pallas_call(
        body, name=name, grid=(t // tr,),
        in_specs=[pl.BlockSpec((tr, a.shape[1]), lambda i: (i, 0)) for a in acts]
        + [pl.BlockSpec(w.shape, lambda i: (0, 0)) for w in ws]
        + [pl.BlockSpec((tr, 3 * D), lambda i: (i, 0)), pl.BlockSpec((1, 3 * D), lambda i: (0, 0))],
        out_specs=[row] * 4, out_shape=[jax.ShapeDtypeStruct((t, D), BF16)] * 4, compiler_params=_cp(1),
    )(*acts, *ws, zg, _row(bg))
    return tuple(outs[:3]), outs[3]


def _branches_bwd(dymix, w_out, ws, zg, bg, ys, name, tr=512):
    t = zg.shape[0]
    tr = min(tr, t)

    def body(dy_ref, wo_ref, w0, w1, w2, z_ref, b_ref, y0, y1, y2, d0, d1, d2, dz_ref, db_ref, da0, da1, da2):
        @pl.when(pl.program_id(0) == 0)
        def _():
            db_ref[...] = jnp.zeros_like(db_ref)

        dm = lax.dot_general(dy_ref[...], wo_ref[...], NT, preferred_element_type=F32)
        for i, (y_ref, w_ref, d_ref, da_ref) in enumerate(((y0, w0, d0, da0), (y1, w1, d1, da1), (y2, w2, d2, da2))):
            gi = _sigmoid(z_ref[:, i * D:(i + 1) * D].astype(F32) + b_ref[:, i * D:(i + 1) * D])
            dy = (dm * gi).astype(BF16)
            d_ref[...] = dy
            dz = dm * y_ref[...].astype(F32) * gi * (1.0 - gi)
            dz_ref[:, i * D:(i + 1) * D] = dz.astype(BF16)
            db_ref[0:1, i * D:(i + 1) * D] += jnp.sum(dz, axis=0, keepdims=True)
            da_ref[...] = jnp.dot(dy, w_ref[...], preferred_element_type=F32).astype(BF16)

    row = pl.BlockSpec((tr, D), lambda i: (i, 0))
    wide = pl.BlockSpec((tr, 3 * D), lambda i: (i, 0))
    outs = pl.pallas_call(
        body, name=name, grid=(t // tr,),
        in_specs=[row, pl.BlockSpec(w_out.shape, lambda i: (0, 0))] + [pl.BlockSpec(w.shape, lambda i: (0, 0)) for w in ws]
        + [wide, pl.BlockSpec((1, 3 * D), lambda i: (0, 0)), row, row, row],
        out_specs=[row, row, row, wide, pl.BlockSpec((8, 3 * D), lambda i: (0, 0))]
        + [pl.BlockSpec((tr, w.shape[1]), lambda i: (i, 0)) for w in ws],
        out_shape=[jax.ShapeDtypeStruct((t, D), BF16)] * 3
        + [jax.ShapeDtypeStruct((t, 3 * D), BF16), jax.ShapeDtypeStruct((8, 3 * D), F32)]
        + [jax.ShapeDtypeStruct((t, w.shape[1]), BF16) for w in ws],
        compiler_params=_cp(1),
    )(dymix, w_out, *ws, zg, _row(bg), *ys)
    return tuple(outs[:3]), outs[3], outs[4][0], tuple(outs[5:])


def _adamw(w, g, m, v, name):
    shape = w.shape
    w, g, m, v = (a.reshape(-1, shape[-1]) for a in (w, g, m, v))
    r, d = w.shape
    tr = next((c for c in (512, 352, 256, 128, 64, 32, 16, 8) if r % c == 0), r)

    def body(w_ref, g_ref, m_ref, v_ref, d_ref, nm_ref, nv_ref):
        gg = g_ref[...]
        m2 = ADAM_B1 * m_ref[...] + (1.0 - ADAM_B1) * gg
        v2 = ADAM_B2 * v_ref[...] + (1.0 - ADAM_B2) * (gg * gg)
        m_hat = m2 / (1.0 - ADAM_B1 ** ADAM_STEP)
        v_hat = v2 / (1.0 - ADAM_B2 ** ADAM_STEP)
        d_ref[...] = -ADAM_LR * (m_hat / (jnp.sqrt(v_hat) + ADAM_EPS) + ADAM_WD * w_ref[...])
        nm_ref[...] = m2
        nv_ref[...] = v2

    row = pl.BlockSpec((tr, d), lambda i: (i, 0))
    outs = pl.pallas_call(
        body, name=name, grid=(r // tr,), in_specs=[row] * 4, out_specs=[row] * 3,
        out_shape=[jax.ShapeDtypeStruct((r, d), F32)] * 3, compiler_params=_cp(1),
    )(w, g, m, v)
    return [o.reshape(shape) for o in outs]


def _slab_tile(rows):
    return next((c for c in range(min(rows, 512), 15, -16) if rows % c == 0), rows)


def _sum_slots(parts, out_dtype, name):
    n, r, d = parts.shape
    tr = _slab_tile(r)

    def body(p_ref, o_ref):
        acc = p_ref[0].astype(F32)
        for s in range(1, n):
            acc = acc + p_ref[s].astype(F32)
        o_ref[...] = acc.astype(o_ref.dtype)

    return pl.pallas_call(
        body, name=name, grid=(r // tr,), in_specs=[pl.BlockSpec((n, tr, d), lambda i: (0, i, 0))],
        out_specs=pl.BlockSpec((tr, d), lambda i: (i, 0)),
        out_shape=jax.ShapeDtypeStruct((r, d), out_dtype), compiler_params=_cp(1),
    )(parts)


def _place():
    return lax.axis_index("x"), lax.axis_index("y"), lax.axis_index("c")


def _all_gather(shard, name):
    r, d = shard.shape

    def body(x_ref, out_ref, send_sems, recv_sems, local_sem):
        x, y, c = _place()
        me, sibling = (x, y, c), (x, y, 1 - c)
        chips = [(1 - x, y), (x, 1 - y), (1 - x, 1 - y)]

        def slot(px, py, pc):
            return out_ref.at[4 * px + 2 * py + pc]

        def copy(k, block, to, src=None):
            return pltpu.make_async_remote_copy(
                src_ref=slot(*block) if src is None else src, dst_ref=slot(*block),
                send_sem=send_sems.at[k], recv_sem=recv_sems.at[k], device_id=to, device_id_type=MESH)

        mine = pltpu.make_async_copy(x_ref, slot(*me), local_sem)
        mine.start()
        first = [copy(0, me, sibling, src=x_ref)]
        first += [copy(1 + j, me, (*chip, c), src=x_ref) for j, chip in enumerate(chips)]
        for cp in first:
            cp.start()
        passed = [copy(4 + j, (*chip, c), sibling) for j, chip in enumerate(chips)]
        for j, chip in enumerate(chips):
            copy(1 + j, (*chip, c), me).wait_recv()
            passed[j].start()
        copy(0, sibling, me).wait_recv()
        for j, chip in enumerate(chips):
            copy(4 + j, (*chip, 1 - c), me).wait_recv()
        for cp in first + passed:
            cp.wait_send()
        mine.wait()

    return pl.pallas_call(
        body, name=name, out_shape=jax.ShapeDtypeStruct((N_DEV, r, d), shard.dtype),
        in_specs=[pl.BlockSpec(memory_space=pl.ANY)], out_specs=pl.BlockSpec(memory_space=pl.ANY),
        scratch_shapes=[pltpu.SemaphoreType.DMA((7,)), pltpu.SemaphoreType.DMA((7,)), pltpu.SemaphoreType.DMA],
    )(shard)


HBM_SPEC = pl.BlockSpec(memory_space=pltpu.HBM)
SEM_SPEC = pl.BlockSpec(memory_space=pltpu.SEMAPHORE)
SPLIT_PARAMS = pltpu.CompilerParams(has_side_effects=pltpu.SideEffectType.DATAFLOW_SIDE_EFFECTING)


def _peers():
    x, y, c = _place()
    flip = lambda v, bit: 1 - v if bit else v
    return 4 * x + 2 * y + c, [((flip(x, k >> 2 & 1), flip(y, k >> 1 & 1), flip(c, k & 1))) for k in range(1, N_DEV)]


def _exchange_start(src, all_gather, name):
    r, d = src.shape[-2:]

    def body(src_ref, land_ref, send_sems, recv_sems, src_thru, land_thru, token):
        me, peers = _peers()
        for k, (px, py, pc) in enumerate(peers):
            part = src_ref if all_gather else src_ref.at[4 * px + 2 * py + pc]
            pltpu.make_async_remote_copy(src_ref=part, dst_ref=land_ref.at[me], send_sem=send_sems.at[k],
                                         recv_sem=recv_sems.at[k], device_id=(px, py, pc), device_id_type=MESH).start()
        token[...] = jnp.zeros_like(token)

    land = lax.empty((N_DEV, r, d), src.dtype)
    return pl.pallas_call(
        body, name=name,
        out_shape=(pltpu.SemaphoreType.DMA((N_DEV - 1,)), pltpu.SemaphoreType.DMA((N_DEV - 1,)), pltpu.HBM(src.shape, src.dtype),
                   pltpu.HBM(land.shape, land.dtype), jax.ShapeDtypeStruct((8, LANES), F32)),
        in_specs=(HBM_SPEC, HBM_SPEC), out_specs=(SEM_SPEC, SEM_SPEC, HBM_SPEC, HBM_SPEC, pl.BlockSpec(memory_space=pltpu.VMEM)),
        input_output_aliases={0: 2, 1: 3}, compiler_params=SPLIT_PARAMS,
    )(pltpu.with_memory_space_constraint(src, pltpu.HBM), pltpu.with_memory_space_constraint(land, pltpu.HBM))


def _exchange_wait(started, after, all_gather, name):
    send_sems, recv_sems, src_thru, land_thru, _ = started

    def body(src_ref, land_ref, send_sems, recv_sems, after_ref, src_out, land_out):
        me, peers = _peers()
        for k, (px, py, pc) in enumerate(peers):
            part = src_ref if all_gather else src_ref.at[4 * px + 2 * py + pc]
            cp = pltpu.make_async_remote_copy(src_ref=part, dst_ref=land_ref.at[4 * px + 2 * py + pc], send_sem=send_sems.at[k],
                                              recv_sem=recv_sems.at[k], device_id=(px, py, pc), device_id_type=MESH)
            cp.wait_send()
            cp.wait_recv()

    return pl.pallas_call(
        body, name=name,
        out_shape=(pltpu.HBM(src_thru.shape, src_thru.dtype), pltpu.HBM(land_thru.shape, land_thru.dtype)),
        in_specs=(HBM_SPEC, HBM_SPEC, SEM_SPEC, SEM_SPEC, pl.BlockSpec(memory_space=pl.ANY)), out_specs=(HBM_SPEC, HBM_SPEC),
        input_output_aliases={0: 0, 1: 1}, compiler_params=SPLIT_PARAMS,
    )(src_thru, land_thru, send_sems, recv_sems, after)


def _own_slot(land, mine):
    x, y, c = _place()
    return lax.dynamic_update_slice(land, mine[None], (4 * x + 2 * y + c, 0, 0))


BIG = (("w_in", (D, 864), 1), ("w_conv_out", (CONV_W, 128), 1), ("w_att_out", (ATT_GW, 128), 1), ("w_mem_kv", (128, D), 0),
       ("w_mem_out", (MEM_W, 128), 1), ("w_out", (128, D), 0), ("w_ffn_in", (D, 704), 1), ("w_ffn_out", (352, D), 0),
       ("conv_dw", (CONV_K, 64), 1))
SMALL = (("rel_bias", (NUM_BUCKETS, 12)), ("norm_mix_pre", (DEPTH, D)), ("b_gate", (DEPTH, 3 * D)),
         ("conv_dw_bias", (DEPTH, CONV_W)), ("conv_ln_g", (DEPTH, CONV_W)), ("conv_ln_b", (DEPTH, CONV_W)),
         ("norm_mem", (DEPTH, D)), ("norm_mix_post", (DEPTH, D)), ("norm_ffn_pre", (DEPTH, D)), ("norm_ffn_post", (DEPTH, D)))
TWIN_WEIGHTS = ("rel_bias", "norm_mix_pre", "w_in", "b_gate", "conv_dw", "conv_dw_bias", "conv_ln_g", "conv_ln_b", "w_conv_out",
                "w_att_out", "norm_mem", "w_mem_kv", "w_mem_out", "w_out", "norm_mix_post", "norm_ffn_pre", "w_ffn_in",
                "w_ffn_out", "norm_ffn_post")


BIG_INFO = {n: (s, a) for n, s, a in BIG}
GROUPS = {"in": ("w_in", "conv_dw"), "mix": ("w_conv_out", "w_att_out", "w_mem_kv", "w_mem_out", "w_out"),
          "ffn": ("w_ffn_in", "w_ffn_out")}
TILE_BYTES_PER_LANE = 32


def _rows_of(shape, dtype):
    tile = TILE_BYTES_PER_LANE // jnp.dtype(dtype).itemsize
    return -(-int(np.prod(shape)) // (D * tile)) * tile


def _row_form(shape, axis):
    return tuple(shape) if axis == 0 else (shape[1], shape[0])


def _as_slab_rows(p, lead=()):
    shape = p.shape[len(lead):]
    rows = _rows_of(shape, p.dtype)
    if shape == (rows, D):
        return p
    n = int(np.prod(shape))
    p = jnp.pad(p.reshape(*lead, n), [(0, 0)] * len(lead) + [(0, rows * D - n)])
    return p.reshape(*lead, rows, D)


def _from_slab_rows(rows, shape, lead=()):
    if rows.shape[len(lead):] == tuple(shape):
        return rows
    n = int(np.prod(shape))
    return rows.reshape(*lead, -1)[..., :n].reshape(*lead, *shape)


def _pack(pieces, dtype):
    return jnp.concatenate([_as_slab_rows(p.astype(dtype)) for p in pieces], axis=0)


def _unpack(slab, shapes, packed_as):
    out, r0 = [], 0
    for s in shapes:
        rows = _rows_of(s, packed_as)
        out.append(_from_slab_rows(slab[r0:r0 + rows], s))
        r0 += rows
    return out


def _pack_group(blocks, layer, group):
    return _pack([blocks[n][layer] if BIG_INFO[n][1] == 0 else blocks[n][layer].T for n in GROUPS[group]], BF16)


def _full_weights(gathered, group):
    out, r0 = {}, 0
    for name in GROUPS[group]:
        r, k = _row_form(*BIG_INFO[name])
        rows = _rows_of((r, k), BF16)
        out[name] = _from_slab_rows(gathered[:, r0:r0 + rows], (r, k), lead=(N_DEV,)).reshape(N_DEV * r, k)
        r0 += rows
    return out


def _scatter_layout(grads, group):
    parts = []
    for name in GROUPS[group]:
        r, k = _row_form(*BIG_INFO[name])
        parts.append(_as_slab_rows(grads[name].astype(BF16).reshape(N_DEV, r, k), lead=(N_DEV,)))
    return jnp.concatenate(parts, axis=1)


def _unpack_blocks(slab, group):
    out = _unpack(slab, [_row_form(*BIG_INFO[n]) for n in GROUPS[group]], BF16)
    return {n: b if BIG_INFO[n][1] == 0 else b.T for n, b in zip(GROUPS[group], out)}


def _split_w_in(w_in):
    edges = np.cumsum((0,) + SEG)
    return [w_in[edges[i]:edges[i + 1]] for i in range(4)]


def _conv_taps(conv_dw):
    return jnp.pad(conv_dw.astype(F32).T, ((0, 32 - CONV_K), (0, 0)))


def _layer_fwd(l, x, h, nm, get_w, sm, bias_tabs):
    t = x.shape[0]
    win = get_w(l, "in", h)
    w_seg = _split_w_in(win["w_in"])
    zc = _mm(h, w_seg[0], "fwd_in_conv", tb=True)
    za = _mm(h, w_seg[1], "fwd_in_att", tb=True, tn=1152)
    zq = _mm(h, w_seg[2], "fwd_in_memq", tb=True)
    zg = _mm(h, w_seg[3], "fwd_in_gate", tb=True)
    y, cact = _conv_fwd(zc, _conv_taps(win["conv_dw"]), sm["conv_dw_bias"][l], sm["conv_ln_g"][l], sm["conv_ln_b"][l],
                        "conv_fwd")
    wts = get_w(l, "mix", cact)
    qkv, outs, lses = [], [], []
    for g, (_, dil) in enumerate(ATT_PATTERNS):
        cols = tuple(i * 3 + g for i in range(3))
        if dil == 1:
            trio, at = (za, za, za), cols
        else:
            trio, at = _to_residue_major([(za, c) for c in cols], dil, f"qkv_residue_d{dil}"), (0, 0, 0)
        o, ls = _att_fwd(*trio, bias_tabs[g], t // dil, f"att_fwd_d{dil}", cols=at)
        qkv.append((trio, at))
        outs.append(o)
        lses.append(ls)
    att, lse = _att_combine(outs, lses, "att_combine")
    kv = _mm(nm, wts["w_mem_kv"], "fwd_mem_kv")
    om = _mem_fwd(zq, kv, "mem_fwd")
    w_br = (wts["w_conv_out"], wts["w_att_out"], wts["w_mem_out"])
    ys, merged = _branches_fwd((cact, att, om), w_br, zg, sm["b_gate"][l], "branches_fwd")
    ymix, x1, h2 = _mm_norm_res(merged, wts["w_out"], x, sm["norm_mix_post"][l], sm["norm_ffn_pre"][l], "fwd_out_norm")
    wff = get_w(l, "ffn", h2)
    gff, uff, act = _mm_swiglu_fwd(h2, wff["w_ffn_in"], "fwd_ffn_in_swiglu")
    g_next = sm["norm_mix_pre"][l + 1] if l + 1 < DEPTH else None
    f, x2, h_next = _mm_norm_res(act, wff["w_ffn_out"], x1, sm["norm_ffn_post"][l], g_next,
                                 "fwd_ffn_out_norm" if g_next is not None else "fwd_ffn_out_last")
    saved = dict(x=x, h=h, zc=zc, zq=zq, zg=zg, y=y, cact=cact, qkv=qkv, att=att, lse=lse, kv=kv, om=om, ys=ys,
                 merged=merged, ymix=ymix, x1=x1, h2=h2, gff=gff, uff=uff, act=act, f=f)
    return x2, h_next, saved


def _after(value, *tokens):
    for t in tokens:
        if t is not None:
            value = value + t[0, 0].astype(value.dtype)
    return value


def _layer_bwd(l, dx2, s, nm, mem, get_w, put_g, sm, bias_tabs, bucket_idx):
    t = dx2.shape[0]
    win, wts, wff = get_w(l, "in", None), get_w(l, "mix", None), get_w(l, "ffn", None)
    gi, gb, gf, gs = {}, {}, {}, {}
    df, gs["norm_ffn_post"] = _norm_bwd(s["f"], sm["norm_ffn_post"][l], dx2, None, BF16, "bwd_norm_ffn_post")
    dg, du = _mm_swiglu_bwd(df, wff["w_ffn_out"], s["gff"], s["uff"], "bwd_ffn_out_d_swiglu")
    gf["w_ffn_out"] = _mm(s["act"], df, "bwd_ffn_out_w", ta=True, tm=1408)
    gf["w_ffn_in"] = jnp.concatenate([_mm(dg, s["h2"], "bwd_ffn_in_w_gate", ta=True, tm=1408),
                                      _mm(du, s["h2"], "bwd_ffn_in_w_up", ta=True, tm=1408)], axis=0)
    tok = put_g(l, "ffn", gf)
    dh2 = _mm_sum([(dg, _after(wff["w_ffn_in"][:FFN_H], tok), FFN_H // 2), (du, wff["w_ffn_in"][FFN_H:], FFN_H // 2)],
                  "bwd_ffn_in_d")
    dx1, dymix, gs["norm_ffn_pre"], gs["norm_mix_post"] = _norm_bwd_pair(
        s["x1"], sm["norm_ffn_pre"][l], dh2, dx2, s["ymix"], sm["norm_mix_post"][l], "bwd_norm_ffn_pre_mix_post")
    gb["w_out"] = _mm(s["merged"], dymix, "bwd_out_w", ta=True)
    w_br = (wts["w_conv_out"], wts["w_att_out"], wts["w_mem_out"])
    dys, dzg, gs["b_gate"], (dcact, datt, dom) = _branches_bwd(dymix, wts["w_out"], w_br, s["zg"], sm["b_gate"][l], s["ys"],
                                                               "branches_bwd")
    gb["w_conv_out"] = _mm(dys[0], s["cact"], "bwd_conv_out_w", ta=True)
    gb["w_att_out"] = _mm(dys[1], s["att"], "bwd_att_out_w", ta=True)
    delta = _att_delta(s["att"], datt, "att_delta")
    dqkv, rel = [], []
    for g, (_, dil) in enumerate(ATT_PATTERNS):
        trio, at = s["qkv"][g]
        if dil == 1:
            dl_r, do_r, lse_r = delta, datt, s["lse"]
        else:
            dl_r, do_r, lse_r = _to_residue_major([(delta, 0), (datt, 0), (s["lse"], 0)], dil, f"att_bwd_residue_d{dil}")
        dq, dk, dv, dbias = _att_bwd(*trio, bias_tabs[g], dl_r, do_r, lse_r, t // dil, f"att_bwd_d{dil}", cols=at)
        dqkv.append((dq, dk, dv))
        rel.append(_bias_grad(dbias, bucket_idx[g], f"bias_grad_d{dil}")[:HEADS, :NUM_BUCKETS])
    dza = _assemble_dza(dqkv, "att_bwd_assemble")
    gs["rel_bias"] = jnp.concatenate(rel, axis=0).T
    gb["w_mem_out"] = _mm(dys[2], s["om"], "bwd_mem_out_w", ta=True)
    dzq, dkv = _mem_bwd(s["zq"], s["kv"], dom, "mem_bwd")
    dkv = dkv.astype(BF16)
    gb["w_mem_kv"] = _mm(nm, dkv, "bwd_mem_kv_w", ta=True)
    tok = put_g(l, "mix", gb)
    dnm = _mm(dkv, wts["w_mem_kv"], "bwd_mem_kv_d", tb=True, out_dtype=F32)
    _, gs["norm_mem"] = _norm_bwd(mem, sm["norm_mem"][l], dnm, None, BF16, "bwd_norm_mem")
    dzc, dconv, gs["conv_dw_bias"], gs["conv_ln_g"], gs["conv_ln_b"] = _conv_bwd(
        s["zc"], s["y"], dcact, _after(_conv_taps(win["conv_dw"]), tok), sm["conv_ln_g"][l], sm["conv_ln_b"][l], "conv_bwd")
    gi["conv_dw"] = dconv[:CONV_K].T
    segs = ((dzc, "conv", 1024), (dza, "att", 1152), (dzq, "memq", 512), (dzg, "gate", 1024))
    gi["w_in"] = jnp.concatenate([_mm(dz, s["h"], f"bwd_in_{nm_}_w", ta=True, tm=min(blk, 1152), tk=2048)
                                  for dz, nm_, blk in segs], axis=0)
    tok = put_g(l, "in", gi)
    w_seg = _split_w_in(win["w_in"])
    w_seg[0] = _after(w_seg[0], tok)
    dx, gs["norm_mix_pre"] = _mm_sum([(dz, w_seg[i], blk) for i, (dz, _, blk) in enumerate(segs)], "bwd_in_d_norm",
                                     norm=(s["x"], sm["norm_mix_pre"][l], dx1))
    return dx, gs


def _local_step(x, mem, target, sm, get_w, put_g, tokens=()):
    t = x.shape[0]
    bias_tabs, bucket_idx = [], []
    for g, (_, dil) in enumerate(ATT_PATTERNS):
        idx = _bucket_index(_att_tiles(t // dil)[1], dil)
        bucket_idx.append(jnp.asarray(idx))
        bias_tabs.append(_bias_table(sm["rel_bias"][:, g * HEADS:(g + 1) * HEADS], bucket_idx[g], f"bias_table_d{dil}"))
    h = _norm_plain(x, _after(sm["norm_mix_pre"][0], *tokens), "norm_first")
    saved, nms = [], []
    for l in range(DEPTH):
        nm = _norm_plain(mem, sm["norm_mem"][l], "norm_mem")
        x, h, s = _layer_fwd(l, x, h, nm, get_w, sm, bias_tabs)
        saved.append(s)
        nms.append(nm)
    loss, dx = _loss_head(x, target, "loss_head")
    gsmall = {}
    for l in reversed(range(DEPTH)):
        dx, gs = _layer_bwd(l, dx, saved[l], nms[l], mem, get_w, put_g, sm, bias_tabs, bucket_idx)
        for n, v in gs.items():
            gsmall.setdefault(n, {})[l] = v
    small = {}
    for n, _ in SMALL:
        small[n] = gsmall[n][0] + gsmall[n][1] if n == "rel_bias" else jnp.stack([gsmall[n][0], gsmall[n][1]])
    return loss, dx, small


def kernel(x, mem, rel_bias, norm_mix_pre, w_in, b_gate, conv_dw, conv_dw_bias, conv_ln_g, conv_ln_b, w_conv_out, w_att_out, norm_mem, w_mem_kv, w_mem_out, w_out, norm_mix_post, norm_ffn_pre, w_ffn_in, w_ffn_out, norm_ffn_post, loss_target, m_rel_bias, m_norm_mix_pre, m_w_in, m_b_gate, m_conv_dw, m_conv_dw_bias, m_conv_ln_g, m_conv_ln_b, m_w_conv_out, m_w_att_out, m_norm_mem, m_w_mem_kv, m_w_mem_out, m_w_out, m_norm_mix_post, m_norm_ffn_pre, m_w_ffn_in, m_w_ffn_out, m_norm_ffn_post, v_rel_bias, v_norm_mix_pre, v_w_in, v_b_gate, v_conv_dw, v_conv_dw_bias, v_conv_ln_g, v_conv_ln_b, v_w_conv_out, v_w_att_out, v_norm_mem, v_w_mem_kv, v_w_mem_out, v_w_out, v_norm_mix_post, v_norm_ffn_pre, v_w_ffn_in, v_w_ffn_out, v_norm_ffn_post):
    args = dict(locals())
    w = {n: args[n] for n in TWIN_WEIGHTS}
    m = {n: args["m_" + n] for n in TWIN_WEIGHTS}
    v = {n: args["v_" + n] for n in TWIN_WEIGHTS}
    sm = {n: w[n] for n, _ in SMALL}

    first = (0, "in")
    keys = [(l, grp) for l in range(DEPTH) for grp in GROUPS]
    shard = {(l, grp): _pack_group(w, l, grp) for l, grp in keys}
    weights = {first: _full_weights(_all_gather(shard[first], "gather_l0_in"), "in")}
    settled = weights[first]["w_in"][:1, :1] * 0
    ag = {}
    for k in keys:
        if k != first:
            ag[k] = _exchange_start(_after(shard[k], settled), True, f"ag_start_l{k[0]}_{k[1]}")
            settled = ag[k][4]

    def get_w(l, grp, after):
        if (l, grp) not in weights:
            mine, land = _exchange_wait(ag[l, grp], after, True, f"ag_wait_l{l}_{grp}")
            weights[l, grp] = _full_weights(_own_slot(land, mine), grp)
        return weights[l, grp]

    rs = {}

    def put_g(l, grp, grads):
        rs[l, grp] = _exchange_start(_scatter_layout(grads, grp), False, f"rs_start_l{l}_{grp}")
        return rs[l, grp][4]

    loss, dx, gsmall = _local_step(x[0], mem[0], loss_target[0], sm, get_w, put_g, [st[4] for st in ag.values()])
    loss = lax.psum(loss, ("x", "y", "c"))

    xi, yi, ci = _place()
    me = 4 * xi + 2 * yi + ci
    g_slab = {}
    for k, st in rs.items():
        contrib, land = _exchange_wait(st, dx, False, f"rs_wait_l{k[0]}_{k[1]}")
        own = lax.dynamic_index_in_dim(contrib, me, axis=0, keepdims=False)
        g_slab[k] = _sum_slots(_own_slot(land, own), F32, f"rs_sum_l{k[0]}_{k[1]}")
    g_layers = [{n: b for grp in GROUPS for n, b in _unpack_blocks(g_slab[l, grp], grp).items()} for l in range(DEPTH)]
    small_shapes = [s for _, s in SMALL]
    g_small = _unpack(_sum_slots(_all_gather(_pack([gsmall[n] for n, _ in SMALL], F32), "gather_small"), F32, "sum_small"),
                      small_shapes, F32)
    grads = {n: jnp.stack([g_layers[0][n], g_layers[1][n]]) for n, _, _ in BIG}
    grads.update({n: g_small[i] for i, (n, _) in enumerate(SMALL)})

    delta, new_m, new_v = {}, {}, {}
    for n in TWIN_WEIGHTS:
        delta[n], new_m[n], new_v[n] = _adamw(w[n], grads[n], m[n], v[n], f"adamw_{n}")

    return (loss, dx[None], *[grads[n] for n in TWIN_WEIGHTS], *[delta[n] for n in TWIN_WEIGHTS],
            *[new_m[n] for n in TWIN_WEIGHTS], *[new_v[n] for n in TWIN_WEIGHTS])
```

```python
import functools

import numpy as np
import jax
import jax.numpy as jnp
from jax import lax
from jax.experimental import pallas as pl
from jax.experimental.pallas import tpu as pltpu

F32 = jnp.float32
BF16 = jnp.bfloat16

N_DEV = 8
D = 1024
DEPTH = 2
CONV_W = 512
CONV_K = 31
CONV_PAD = 16
ATT_PATTERNS = ((128, 1), (512, 4), (2048, 16))
ATT_RADIUS = 64
HEADS = 4
HEAD_DIM = 64
ATT_GW = HEADS * HEAD_DIM
ATT_W = 3 * ATT_GW
MEM_HEADS = 4
MEM_HD = 128
MEM_W = 512
N_MEM = 256
FFN_H = 2816
NUM_BUCKETS = 32
MAX_DISTANCE = 1024
RMS_EPS = 1e-6
LN_EPS = 1e-5
NEG_INF = -1e30
SEG = (2 * CONV_W, 3 * ATT_W, MEM_W, 3 * D)
ADAM_LR, ADAM_B1, ADAM_B2, ADAM_EPS, ADAM_WD, ADAM_STEP = 0.001, 0.9, 0.999, 1e-08, 0.01, 10

VMEM_LIMIT_V7X = 56 * 1024 * 1024
MESH = pl.DeviceIdType.MESH


def _cp(n_axes):
    return pltpu.CompilerParams(dimension_semantics=("arbitrary",) * n_axes, vmem_limit_bytes=VMEM_LIMIT_V7X)


def _row(v):
    return v.reshape(1, -1)


def _mm(a, b, name, *, ta=False, tb=False, out_dtype=BF16, tm=2048, tn=1024, tk=2048):
    kdim, m = a.shape if ta else a.shape[::-1]
    n, kb = b.shape if tb else b.shape[::-1]
    assert kb == kdim and a.dtype == BF16 and b.dtype == BF16
    tm, tn, tk = min(tm, m), min(tn, n), min(tk, kdim)
    assert m % tm == 0 and n % tn == 0 and kdim % tk == 0, (name, m, n, kdim, tm, tn, tk)
    nk = kdim // tk
    a_spec = pl.BlockSpec((tk, tm), lambda i, j, k: (k, i)) if ta else pl.BlockSpec((tm, tk), lambda i, j, k: (i, k))
    b_spec = pl.BlockSpec((tn, tk), lambda i, j, k: (j, k)) if tb else pl.BlockSpec((tk, tn), lambda i, j, k: (k, j))
    o_spec = pl.BlockSpec((tm, tn), lambda i, j, k: (i, j))
    dims = (((0 if ta else 1,), (1 if tb else 0,)), ((), ()))
    use_scratch = nk > 1 and out_dtype != F32

    def body(*refs):
        a_ref, b_ref, o_ref = refs[:3]
        p = lax.dot_general(a_ref[...], b_ref[...], dims, preferred_element_type=F32)
        if nk == 1:
            o_ref[...] = p.astype(o_ref.dtype)
            return
        k = pl.program_id(2)
        t_ref = refs[3] if use_scratch else o_ref

        @pl.when(k == 0)
        def _():
            t_ref[...] = p

        @pl.when(k > 0)
        def _():
            t_ref[...] += p

        if use_scratch:
            @pl.when(k == nk - 1)
            def _():
                o_ref[...] = t_ref[...].astype(o_ref.dtype)

    return pl.pallas_call(
        body, name=name, grid=(m // tm, n // tn, nk), in_specs=[a_spec, b_spec], out_specs=o_spec,
        out_shape=jax.ShapeDtypeStruct((m, n), out_dtype),
        scratch_shapes=[pltpu.VMEM((tm, tn), F32)] if use_scratch else [],
        compiler_params=_cp(3),
    )(a, b)


EPILOGUE_COLS = 512
EPILOGUE_ROWS = 512


def _chunks(size, step):
    return [(c, min(c + step, size)) for c in range(0, size, step)]


def _rms_bwd_val(v, g, dy):
    r = lax.rsqrt(jnp.mean(v * v, axis=-1, keepdims=True) + RMS_EPS)
    vhat = v * r
    dvh = dy * g
    dv = r * (dvh - vhat * jnp.mean(dvh * vhat, axis=-1, keepdims=True))
    return dv, jnp.sum(dy * vhat, axis=0, keepdims=True)


def _mm_sum(pairs, name, *, norm=None, tm=1024):
    m, n = pairs[0][0].shape[0], pairs[0][1].shape[1]
    tm = min(tm, m)
    starts, specs, total = [], [], 0
    for a, b, tk in pairs:
        assert a.shape == (m, b.shape[0]) and b.shape[1] == n and a.shape[1] % tk == 0 and a.dtype == b.dtype == BF16
        nk = a.shape[1] // tk
        starts.append((total, nk))
        step = functools.partial(lambda k, s0, nk_: jnp.clip(k - s0, 0, nk_ - 1), s0=total, nk_=nk)
        specs.append(pl.BlockSpec((tm, tk), functools.partial(lambda i, k, st: (i, st(k)), st=step)))
        specs.append(pl.BlockSpec((tk, n), functools.partial(lambda i, k, st: (st(k), 0), st=step)))
        total += nk
    np_ = 2 * len(pairs)
    row = pl.BlockSpec((tm, n), lambda i, k: (i, 0))

    def accumulate(refs, acc, k, upto=total):
        for p, (s0, nk) in enumerate(starts):
            @pl.when((k >= s0) & (k < min(s0 + nk, upto)))
            def _(p=p, s0=s0):
                part = jnp.dot(refs[2 * p][...], refs[2 * p + 1][...], preferred_element_type=F32)
                if s0 == 0:
                    @pl.when(k == 0)
                    def _():
                        acc[...] = part

                    @pl.when(k > 0)
                    def _():
                        acc[...] += part
                else:
                    acc[...] += part

    operands = [op for a, b, _ in pairs for op in (a, b)]
    if norm is None:
        def plain(*refs):
            k = pl.program_id(1)
            accumulate(refs, refs[np_ + 1], k)

            @pl.when(k == total - 1)
            def _():
                refs[np_][...] = refs[np_ + 1][...].astype(BF16)

        return pl.pallas_call(
            plain, name=name, grid=(m // tm, total), in_specs=specs, out_specs=row,
            out_shape=jax.ShapeDtypeStruct((m, n), BF16), scratch_shapes=[pltpu.VMEM((tm, n), F32)],
            compiler_params=_cp(2),
        )(*operands)

    v, g, dres = norm

    def body(*refs):
        v_ref, g_ref, dres_ref, dv_ref, dg_ref, acc = refs[np_:np_ + 6]
        i, k = pl.program_id(0), pl.program_id(1)
        accumulate(refs, acc, k, upto=total - 1)

        @pl.when((i == 0) & (k == 0))
        def _():
            dg_ref[...] = jnp.zeros_like(dg_ref)

        @pl.when(k == total - 1)
        def _():
            a_ref, b_ref = refs[np_ - 2], refs[np_ - 1]
            for r0, r1 in _chunks(tm, EPILOGUE_ROWS):
                dy = jnp.dot(a_ref[r0:r1, :], b_ref[...], preferred_element_type=F32)
                if total > 1:
                    dy = dy + acc[r0:r1, :]
                dv, dg = _rms_bwd_val(v_ref[r0:r1, :], g_ref[...], dy)
                dv_ref[r0:r1, :] = dv + dres_ref[r0:r1, :]
                dg_ref[0:1, :] += dg

    once = pl.BlockSpec((tm, n), lambda i, k: (i, 0), pipeline_mode=pl.Buffered(1))
    dv, dg = pl.pallas_call(
        body, name=name, grid=(m // tm, total),
        in_specs=specs + [once, pl.BlockSpec((1, n), lambda i, k: (0, 0)), once],
        out_specs=[row, pl.BlockSpec((8, n), lambda i, k: (0, 0))],
        out_shape=[jax.ShapeDtypeStruct((m, n), F32), jax.ShapeDtypeStruct((8, n), F32)],
        scratch_shapes=[pltpu.VMEM((tm, n), F32)], compiler_params=_cp(2),
    )(*operands, v, _row(g), dres)
    return dv, dg[0]


def _mm_swiglu_fwd(h, w, name, *, tm=1024, tn=1408):
    m, kdim = h.shape
    hid = w.shape[0] // 2
    tm = min(tm, m)
    nj = hid // tn
    dims = (((1,), (1,)), ((), ()))

    def body(h_ref, wg_ref, wu_ref, g_ref, u_ref, a_ref):
        hv = h_ref[...]
        for c0, c1 in _chunks(tn, EPILOGUE_COLS):
            g = lax.dot_general(hv, wg_ref[c0:c1, :], dims, preferred_element_type=F32)
            u = lax.dot_general(hv, wu_ref[c0:c1, :], dims, preferred_element_type=F32)
            g_ref[:, c0:c1] = g.astype(BF16)
            u_ref[:, c0:c1] = u.astype(BF16)
            a_ref[:, c0:c1] = (g * _sigmoid(g) * u).astype(BF16)

    out = pl.BlockSpec((tm, tn), lambda i, j: (i, j))
    return pl.pallas_call(
        body, name=name, grid=(m // tm, nj),
        in_specs=[pl.BlockSpec((tm, kdim), lambda i, j: (i, 0)), pl.BlockSpec((tn, kdim), lambda i, j: (j, 0)),
                  pl.BlockSpec((tn, kdim), lambda i, j: (j + nj, 0))],
        out_specs=[out, out, out], out_shape=[jax.ShapeDtypeStruct((m, hid), BF16)] * 3, compiler_params=_cp(2),
    )(h, w, w)


def _mm_swiglu_bwd(df, w, g, u, name, *, tm=1024, tn=1408):
    m, kdim = df.shape
    hid = w.shape[0]
    tm = min(tm, m)

    def body(df_ref, w_ref, g_ref, u_ref, dg_ref, du_ref):
        dfv = df_ref[...]
        for c0, c1 in _chunks(tn, EPILOGUE_COLS):
            da = lax.dot_general(dfv, w_ref[c0:c1, :], (((1,), (1,)), ((), ())), preferred_element_type=F32)
            gg, uu, dab = g_ref[:, c0:c1], u_ref[:, c0:c1], da.astype(BF16)
            sg = _sigmoid(gg)
            dg_ref[:, c0:c1] = dab * uu * (sg * (1.0 + gg * (1.0 - sg)))
            du_ref[:, c0:c1] = dab * gg * sg

    blk = pl.BlockSpec((tm, tn), lambda i, j: (i, j))
    return pl.pallas_call(
        body, name=name, grid=(m // tm, hid // tn),
        in_specs=[pl.BlockSpec((tm, kdim), lambda i, j: (i, 0)), pl.BlockSpec((tn, kdim), lambda i, j: (j, 0)), blk, blk],
        out_specs=[blk, blk], out_shape=[jax.ShapeDtypeStruct((m, hid), BF16)] * 2, compiler_params=_cp(2),
    )(df, w, g, u)


def _mm_norm_res(a, b, xres, g, g_next, name, *, tm=1024):
    m, kdim = a.shape
    n = b.shape[1]
    tm = min(tm, m)
    two = g_next is not None

    def body(*refs):
        a_ref, b_ref, x_ref, g_ref = refs[:4]
        y_ref, xn_ref = refs[4 + two], refs[5 + two]
        for r0, r1 in _chunks(tm, EPILOGUE_ROWS):
            y = jnp.dot(a_ref[r0:r1, :], b_ref[...], preferred_element_type=F32)
            y_ref[r0:r1, :] = y.astype(BF16)
            xn = x_ref[r0:r1, :] + _rms_val(y, g_ref[...])
            xn_ref[r0:r1, :] = xn
            if two:
                refs[7][r0:r1, :] = _rms_val(xn, refs[4][...]).astype(BF16)

    row = pl.BlockSpec((tm, n), lambda i: (i, 0))
    vec = pl.BlockSpec((1, n), lambda i: (0, 0))
    outs = pl.pallas_call(
        body, name=name, grid=(m // tm,),
        in_specs=[pl.BlockSpec((tm, kdim), lambda i: (i, 0)), pl.BlockSpec((kdim, n), lambda i: (0, 0)), row, vec] + ([vec] if two else []),
        out_specs=[row, row] + ([row] if two else []),
        out_shape=[jax.ShapeDtypeStruct((m, n), BF16), jax.ShapeDtypeStruct((m, n), F32)]
        + ([jax.ShapeDtypeStruct((m, n), BF16)] if two else []),
        compiler_params=_cp(1),
    )(*((a, b, xres, _row(g)) + ((_row(g_next),) if two else ())))
    return (outs[0], outs[1], outs[2]) if two else (outs[0], outs[1], None)


def _rms_val(v, g):
    return v * lax.rsqrt(jnp.mean(v * v, axis=-1, keepdims=True) + RMS_EPS) * g


def _norm_plain(v, g, name, tr=512):
    t, d = v.shape
    tr = min(tr, t)

    def body(v_ref, g_ref, o_ref):
        o_ref[...] = _rms_val(v_ref[...], g_ref[...]).astype(BF16)

    return pl.pallas_call(
        body, name=name, grid=(t // tr,),
        in_specs=[pl.BlockSpec((tr, d), lambda i: (i, 0)), pl.BlockSpec((1, d), lambda i: (0, 0))],
        out_specs=pl.BlockSpec((tr, d), lambda i: (i, 0)),
        out_shape=jax.ShapeDtypeStruct((t, d), BF16), compiler_params=_cp(1),
    )(v, _row(g))


def _norm_bwd(v, g, dout, dres, out_dtype, name, tr=512):
    t, d = v.shape
    tr = min(tr, t)
    nt = t // tr
    has_res = dres is not None

    def body(*refs):
        v_ref, g_ref, do_ref = refs[:3]
        dv_ref, dg_ref = refs[3 + has_res], refs[4 + has_res]
        i = pl.program_id(0)
        vv = v_ref[...].astype(F32)
        dy = do_ref[...].astype(F32)
        r = lax.rsqrt(jnp.mean(vv * vv, axis=-1, keepdims=True) + RMS_EPS)
        vhat = vv * r
        dvh = dy * g_ref[...]
        dv = r * (dvh - vhat * jnp.mean(dvh * vhat, axis=-1, keepdims=True))
        if has_res:
            dv = dv + refs[3][...]
        dv_ref[...] = dv.astype(dv_ref.dtype)
        part = jnp.sum(dy * vhat, axis=0, keepdims=True)

        @pl.when(i == 0)
        def _():
            dg_ref[...] = jnp.zeros_like(dg_ref)

        dg_ref[0:1, :] += part

    row = pl.BlockSpec((tr, d), lambda i: (i, 0))
    dv, dg = pl.pallas_call(
        body, name=name, grid=(nt,),
        in_specs=[row, pl.BlockSpec((1, d), lambda i: (0, 0)), row] + ([row] if has_res else []),
        out_specs=[row, pl.BlockSpec((8, d), lambda i: (0, 0))],
        out_shape=[jax.ShapeDtypeStruct((t, d), out_dtype), jax.ShapeDtypeStruct((8, d), F32)],
        compiler_params=_cp(1),
    )(*((v, _row(g), dout) + ((dres,) if has_res else ())))
    return dv, dg[0]


def _norm_bwd_pair(v1, g1, dout, dres, v2, g2, name, tr=512):
    t, d = v1.shape
    tr = min(tr, t)

    def body(v1_ref, g1_ref, do_ref, dr_ref, v2_ref, g2_ref, d1_ref, d2_ref, dg1_ref, dg2_ref):
        @pl.when(pl.program_id(0) == 0)
        def _():
            dg1_ref[...] = jnp.zeros_like(dg1_ref)
            dg2_ref[...] = jnp.zeros_like(dg2_ref)

        dv1, dg1 = _rms_bwd_val(v1_ref[...].astype(F32), g1_ref[...], do_ref[...].astype(F32))
        d1 = dv1 + dr_ref[...]
        d1_ref[...] = d1
        dv2, dg2 = _rms_bwd_val(v2_ref[...].astype(F32), g2_ref[...], d1)
        d2_ref[...] = dv2.astype(BF16)
        dg1_ref[0:1, :] += dg1
        dg2_ref[0:1, :] += dg2

    row = pl.BlockSpec((tr, d), lambda i: (i, 0))
    vec = pl.BlockSpec((1, d), lambda i: (0, 0))
    acc = pl.BlockSpec((8, d), lambda i: (0, 0))
    d1, d2, dg1, dg2 = pl.pallas_call(
        body, name=name, grid=(t // tr,), in_specs=[row, vec, row, row, row, vec], out_specs=[row, row, acc, acc],
        out_shape=[jax.ShapeDtypeStruct((t, d), F32), jax.ShapeDtypeStruct((t, d), BF16),
                   jax.ShapeDtypeStruct((8, d), F32), jax.ShapeDtypeStruct((8, d), F32)],
        compiler_params=_cp(1),
    )(v1, _row(g1), dout, dres, v2, _row(g2))
    return d1, d2, dg1[0], dg2[0]


def _loss_head(y, target, name, tr=512):
    t, d = y.shape
    tr = min(tr, t)
    nt = t // tr

    def body(y_ref, t_ref, dy_ref, l_ref):
        i = pl.program_id(0)
        err = y_ref[...] - t_ref[...]
        dy_ref[...] = err * (1.0 / d)

        @pl.when(i == 0)
        def _():
            l_ref[...] = jnp.zeros_like(l_ref)

        l_ref[...] += jnp.sum(err * err) * (0.5 / d)

    row = pl.BlockSpec((tr, d), lambda i: (i, 0))
    dy, l = pl.pallas_call(
        body, name=name, grid=(nt,), in_specs=[row, row],
        out_specs=[row, pl.BlockSpec((8, 128), lambda i: (0, 0))],
        out_shape=[jax.ShapeDtypeStruct((t, d), F32), jax.ShapeDtypeStruct((8, 128), F32)],
        compiler_params=_cp(1),
    )(y, target)
    return l[0, 0], dy


def _sigmoid(v):
    return 1.0 / (1.0 + jnp.exp(-v))


def _halo_specs(tq, width, nt, halo):
    per = tq // halo
    last = nt * per - 1
    main = pl.BlockSpec((tq, width), lambda i: (i, 0))
    prev = pl.BlockSpec((halo, width), lambda i: (jnp.maximum(i * per - 1, 0), 0))
    nxt = pl.BlockSpec((halo, width), lambda i: (jnp.minimum((i + 1) * per, last), 0))
    return prev, main, nxt


def _glu_window(zp, zm, zn, i, nt):
    def glu(z):
        z = z.astype(F32)
        return z[:, :CONV_W] * _sigmoid(z[:, CONV_W:])

    up = jnp.where(i > 0, glu(zp), 0.0)
    un = jnp.where(i < nt - 1, glu(zn), 0.0)
    return jnp.concatenate([up, glu(zm), un], axis=0)


def _shifted(win, shift, rows):
    if shift == 0:
        return win[:rows]
    return pltpu.roll(win, win.shape[0] - shift, 0)[:rows]


def _conv_fwd(zc, w, b, ln_g, ln_b, name, tq=512):
    t = zc.shape[0]
    tq = min(tq, t)
    nt = t // tq

    def body(zp_ref, zm_ref, zn_ref, w_ref, b_ref, g_ref, bb_ref, y_ref, c_ref):
        i = pl.program_id(0)
        win = _glu_window(zp_ref[...], zm_ref[...], zn_ref[...], i, nt)
        wv = w_ref[...]
        y = jnp.zeros((tq, CONV_W), F32) + b_ref[...]
        for k in range(CONV_K):
            y = y + _shifted(win, k + 1, tq) * wv[k:k + 1, :]
        y_ref[...] = y
        mu = jnp.mean(y, axis=-1, keepdims=True)
        yc = y - mu
        ln = yc * lax.rsqrt(jnp.mean(yc * yc, axis=-1, keepdims=True) + LN_EPS) * g_ref[...] + bb_ref[...]
        c_ref[...] = (ln * _sigmoid(ln)).astype(BF16)

    vec = pl.BlockSpec((1, CONV_W), lambda i: (0, 0))
    out = pl.BlockSpec((tq, CONV_W), lambda i: (i, 0))
    return pl.pallas_call(
        body, name=name, grid=(nt,),
        in_specs=[*_halo_specs(tq, 2 * CONV_W, nt, CONV_PAD), pl.BlockSpec((32, CONV_W), lambda i: (0, 0)), vec, vec, vec],
        out_specs=[out, out],
        out_shape=[jax.ShapeDtypeStruct((t, CONV_W), F32), jax.ShapeDtypeStruct((t, CONV_W), BF16)],
        compiler_params=_cp(1),
    )(zc, zc, zc, w, _row(b), _row(ln_g), _row(ln_b))


def _conv_bwd(zc, y, dc, w, ln_g, ln_b, name, tq=256):
    t = zc.shape[0]
    tq = min(tq, t)
    nt = t // tq
    rows = tq + 2 * CONV_PAD

    def body(zp_ref, zm_ref, zn_ref, yp_ref, ym_ref, yn_ref, dp_ref, dm_ref, dn_ref, w_ref, g_ref, bb_ref,
             dz_ref, dw_ref, db_ref, dg_ref, dbb_ref):
        i = pl.program_id(0)
        uwin = _glu_window(zp_ref[...], zm_ref[...], zn_ref[...], i, nt)
        ywin = jnp.concatenate([yp_ref[...], ym_ref[...], yn_ref[...]], axis=0)
        dcw = jnp.concatenate([dp_ref[...], dm_ref[...], dn_ref[...]], axis=0).astype(F32)
        mu = jnp.mean(ywin, axis=-1, keepdims=True)
        yc = ywin - mu
        rstd = lax.rsqrt(jnp.mean(yc * yc, axis=-1, keepdims=True) + LN_EPS)
        yhat = yc * rstd
        ln = yhat * g_ref[...] + bb_ref[...]
        sg = _sigmoid(ln)
        dl = dcw * (sg * (1.0 + ln * (1.0 - sg)))
        ridx = lax.broadcasted_iota(jnp.int32, (rows, 1), 0)
        inside = ((ridx >= CONV_PAD) | (i > 0)) & ((ridx < tq + CONV_PAD) | (i < nt - 1))
        dl = jnp.where(inside, dl, 0.0)
        dyh = dl * g_ref[...]
        dy = rstd * (dyh - jnp.mean(dyh, axis=-1, keepdims=True) - yhat * jnp.mean(dyh * yhat, axis=-1, keepdims=True))
        dy = jnp.where(inside, dy, 0.0)
        main = slice(CONV_PAD, CONV_PAD + tq)
        dlm, yhm, dym = dl[main], yhat[main], dy[main]

        @pl.when(i == 0)
        def _():
            dw_ref[...] = jnp.zeros_like(dw_ref)
            db_ref[...] = jnp.zeros_like(db_ref)
            dg_ref[...] = jnp.zeros_like(dg_ref)
            dbb_ref[...] = jnp.zeros_like(dbb_ref)

        dg_ref[0:1, :] += jnp.sum(dlm * yhm, axis=0, keepdims=True)
        dbb_ref[0:1, :] += jnp.sum(dlm, axis=0, keepdims=True)
        db_ref[0:1, :] += jnp.sum(dym, axis=0, keepdims=True)
        wv = w_ref[...]
        du = jnp.zeros((tq, CONV_W), F32)
        for k in range(CONV_K):
            du = du + _shifted(dy, 2 * CONV_PAD - 1 - k, tq) * wv[k:k + 1, :]
            dw_ref[k:k + 1, :] += jnp.sum(dym * _shifted(uwin, k + 1, tq), axis=0, keepdims=True)
        zm = zm_ref[...].astype(F32)
        a, gt = zm[:, :CONV_W], zm[:, CONV_W:]
        sgt = _sigmoid(gt)
        dz_ref[:, :CONV_W] = (du * sgt).astype(BF16)
        dz_ref[:, CONV_W:] = (du * a * sgt * (1.0 - sgt)).astype(BF16)

    vec = pl.BlockSpec((1, CONV_W), lambda i: (0, 0))
    acc = pl.BlockSpec((8, CONV_W), lambda i: (0, 0))
    dz, dw, db, dg, dbb = pl.pallas_call(
        body, name=name, grid=(nt,),
        in_specs=[*_halo_specs(tq, 2 * CONV_W, nt, CONV_PAD), *_halo_specs(tq, CONV_W, nt, CONV_PAD),
                  *_halo_specs(tq, CONV_W, nt, CONV_PAD), pl.BlockSpec((32, CONV_W), lambda i: (0, 0)), vec, vec],
        out_specs=[pl.BlockSpec((tq, 2 * CONV_W), lambda i: (i, 0)), pl.BlockSpec((32, CONV_W), lambda i: (0, 0)), acc, acc, acc],
        out_shape=[jax.ShapeDtypeStruct((t, 2 * CONV_W), BF16), jax.ShapeDtypeStruct((32, CONV_W), F32),
                   jax.ShapeDtypeStruct((8, CONV_W), F32), jax.ShapeDtypeStruct((8, CONV_W), F32),
                   jax.ShapeDtypeStruct((8, CONV_W), F32)],
        compiler_params=_cp(1),
    )(zc, zc, zc, y, y, y, dc, dc, dc, w, _row(ln_g), _row(ln_b))
    return dz, dw, db[0], dg[0], dbb[0]


def _att_tiles(seq_len):
    tq = min(256, seq_len)
    return tq, min(128, tq)


def _t5_bucket_np(rel):
    nb = NUM_BUCKETS // 2
    max_exact = nb // 2
    ret = np.where(rel > 0, nb, 0)
    n = np.abs(rel)
    nf = np.maximum(n, 1).astype(np.float32)
    large = max_exact + (np.log(nf / np.float32(max_exact)) / np.float32(np.log(MAX_DISTANCE / max_exact))
                         * np.float32(nb - max_exact)).astype(np.int32)
    large = np.minimum(large, nb - 1)
    return ret + np.where(n < max_exact, n, large)


def _bucket_index(sb, dilation):
    off = np.arange(sb + 2 * ATT_RADIUS)[None, :] - ATT_RADIUS - np.arange(sb)[:, None]
    idx = _t5_bucket_np(off * dilation).astype(np.int32)
    return np.where(np.abs(off) <= ATT_RADIUS, idx, -1).astype(np.int32)


def _bias_table(tab, idx, name):
    sb, w = idx.shape

    def body(tab_ref, idx_ref, o_ref):
        ix = idx_ref[...]
        for h in range(HEADS):
            acc = jnp.full((sb, w), NEG_INF, F32)
            for b in range(NUM_BUCKETS):
                acc = jnp.where(ix == b, tab_ref[b, h], acc)
            o_ref[h] = acc

    return pl.pallas_call(
        body, name=name, out_shape=jax.ShapeDtypeStruct((HEADS, sb, w), F32),
        in_specs=[pl.BlockSpec(memory_space=pltpu.SMEM), pl.BlockSpec(memory_space=pltpu.VMEM)],
        out_specs=pl.BlockSpec(memory_space=pltpu.VMEM),
    )(tab, idx)


def _qkv_specs(tq, per, last_blk, cols, tile):
    def main(col):
        return pl.BlockSpec((tq, ATT_GW), lambda *g: (tile(*g), col))

    def prev(col):
        return pl.BlockSpec((ATT_RADIUS, ATT_GW), lambda *g: (jnp.maximum(tile(*g) * per - 1, 0), col))

    def nxt(col):
        return pl.BlockSpec((ATT_RADIUS, ATT_GW), lambda *g: (jnp.minimum((tile(*g) + 1) * per, last_blk), col))

    cq, ck, cv = cols
    return [main(cq), prev(ck), main(ck), nxt(ck), prev(cv), main(cv), nxt(cv)]


def _att_fwd(q, k, v, bias, seq_len, name, cols=(0, 0, 0)):
    t = q.shape[0]
    tq, sb = _att_tiles(seq_len)
    nbl = seq_len // tq
    nt = t // tq
    per = tq // ATT_RADIUS
    w = sb + 2 * ATT_RADIUS
    last_blk = t // ATT_RADIUS - 1

    def body(q_ref, kp_ref, kc_ref, kn_ref, vp_ref, vc_ref, vn_ref, b_ref, o_ref, l_ref):
        n = pl.program_id(0) % nbl
        kw = jnp.concatenate([kp_ref[...], kc_ref[...], kn_ref[...]], axis=0)
        vw = jnp.concatenate([vp_ref[...], vc_ref[...], vn_ref[...]], axis=0)
        kpos = n * tq - ATT_RADIUS + lax.broadcasted_iota(jnp.int32, (1, tq + 2 * ATT_RADIUS), 1)
        valid = (kpos >= 0) & (kpos < seq_len)
        for j in range(tq // sb):
            qj = q_ref[j * sb:(j + 1) * sb, :]
            kj, vj, okj = kw[j * sb:j * sb + w], vw[j * sb:j * sb + w], valid[:, j * sb:j * sb + w]
            outs, lses = [], []
            for h in range(HEADS):
                hs = slice(h * HEAD_DIM, (h + 1) * HEAD_DIM)
                s = lax.dot_general(qj[:, hs], kj[:, hs], NT, preferred_element_type=F32)
                s = jnp.where(okj, s * (HEAD_DIM ** -0.5) + b_ref[h], NEG_INF)
                m = jnp.max(s, axis=-1, keepdims=True)
                e = jnp.exp(s - m)
                den = jnp.sum(e, axis=-1, keepdims=True)
                outs.append(jnp.dot(e.astype(BF16), vj[:, hs], preferred_element_type=F32) / den)
                lses.append(jnp.broadcast_to(m + jnp.log(den), (sb, HEAD_DIM)))
            o_ref[j * sb:(j + 1) * sb, :] = jnp.concatenate(outs, axis=1).astype(BF16)
            l_ref[j * sb:(j + 1) * sb, :] = jnp.concatenate(lses, axis=1)

    main = pl.BlockSpec((tq, ATT_GW), lambda i: (i, 0))
    return pl.pallas_call(
        body, name=name, grid=(nt,),
        in_specs=[*_qkv_specs(tq, per, last_blk, cols, lambda i: i), pl.BlockSpec((HEADS, sb, w), lambda i: (0, 0, 0))],
        out_specs=[main, main],
        out_shape=[jax.ShapeDtypeStruct((t, ATT_GW), BF16), jax.ShapeDtypeStruct((t, ATT_GW), F32)],
        compiler_params=_cp(1),
    )(q, k, k, k, v, v, v, bias)


PERM_ROWS = 2048
LANES = 128


def _spec_rm(d, tr, width, col=0):
    return pl.BlockSpec((d, tr // d, width), lambda i: (0, i, col))


def _rm_view(a, d):
    return a.reshape(d, a.shape[0] // d, a.shape[1])


def _gather_residues(scr, val, d, out_ref, col0):
    tr, w = val.shape
    for c in range(w // LANES):
        scr[...] = val[:, c * LANES:(c + 1) * LANES].astype(F32)
        for r in range(d):
            out_ref[r, :, col0 + c * LANES:col0 + (c + 1) * LANES] = scr[pl.ds(r, tr // d, stride=d), :].astype(out_ref.dtype)


def _scatter_residues(scr, ref, d):
    w = ref.shape[2]
    cols = []
    for c in range(w // LANES):
        for r in range(d):
            scr[pl.ds(r, scr.shape[0] // d, stride=d), :] = ref[r, :, c * LANES:(c + 1) * LANES].astype(F32)
        cols.append(scr[...])
    return jnp.concatenate(cols, axis=1)


def _to_residue_major(srcs, d, name, tr=PERM_ROWS):
    t = srcs[0][0].shape[0]
    tr = min(tr, t)
    n = len(srcs)

    def body(*refs):
        scr = refs[2 * n]
        for k in range(n):
            _gather_residues(scr, refs[k][...], d, refs[n + k], 0)

    outs = pl.pallas_call(
        body, name=name, grid=(t // tr,),
        in_specs=[pl.BlockSpec((tr, ATT_GW), functools.partial(lambda i, col: (i, col), col=col)) for _, col in srcs],
        out_specs=[_spec_rm(d, tr, ATT_GW)] * n,
        out_shape=[jax.ShapeDtypeStruct((d, t // d, ATT_GW), a.dtype) for a, _ in srcs],
        scratch_shapes=[pltpu.VMEM((tr, LANES), F32)], compiler_params=_cp(1),
    )(*[a for a, _ in srcs])
    return [o.reshape(t, ATT_GW) for o in outs]


def _att_combine(outs, lses, name, tr=PERM_ROWS):
    t = outs[0].shape[0]
    tr = min(tr, t)
    dils = [d for _, d in ATT_PATTERNS]

    def body(o0, o1, o2, l0, l1, l2, a_ref, lse_ref, scr):
        os_, ls_ = [], []
        for o_ref, l_ref, d in zip((o0, o1, o2), (l0, l1, l2), dils):
            if d == 1:
                os_.append(o_ref[...].astype(F32))
                ls_.append(l_ref[...])
            else:
                os_.append(_scatter_residues(scr, o_ref, d))
                ls_.append(_scatter_residues(scr, l_ref, d))
        la, lb, lc = ls_
        m = jnp.maximum(jnp.maximum(la, lb), lc)
        ea, eb, ec = jnp.exp(la - m), jnp.exp(lb - m), jnp.exp(lc - m)
        den = ea + eb + ec
        a_ref[...] = ((ea * os_[0] + eb * os_[1] + ec * os_[2]) / den).astype(BF16)
        lse_ref[...] = m + jnp.log(den)

    row = pl.BlockSpec((tr, ATT_GW), lambda i: (i, 0))
    specs = [row if d == 1 else _spec_rm(d, tr, ATT_GW) for d in dils]
    views = lambda arrs: [a if d == 1 else _rm_view(a, d) for a, d in zip(arrs, dils)]
    return pl.pallas_call(
        body, name=name, grid=(t // tr,), in_specs=specs * 2, out_specs=[row, row],
        out_shape=[jax.ShapeDtypeStruct((t, ATT_GW), BF16), jax.ShapeDtypeStruct((t, ATT_GW), F32)],
        scratch_shapes=[pltpu.VMEM((tr, LANES), F32)], compiler_params=_cp(1),
    )(*views(outs), *views(lses))


def _assemble_dza(dqkv, name, tr=PERM_ROWS):
    t = dqkv[0][0].shape[0]
    tr = min(tr, t)
    dils = [d for _, d in ATT_PATTERNS]

    def body(*refs):
        o_ref, scr = refs[9], refs[10]
        for g, d in enumerate(dils):
            for c in range(3):
                ref = refs[g * 3 + c]
                val = ref[...] if d == 1 else _scatter_residues(scr, ref, d).astype(BF16)
                o_ref[:, (c * 3 + g) * ATT_GW:(c * 3 + g + 1) * ATT_GW] = val

    row = pl.BlockSpec((tr, ATT_GW), lambda i: (i, 0))
    specs = [row if d == 1 else _spec_rm(d, tr, ATT_GW) for d in dils for _ in range(3)]
    args = [a if d == 1 else _rm_view(a, d) for trio, d in zip(dqkv, dils) for a in trio]
    return pl.pallas_call(
        body, name=name, grid=(t // tr,), in_specs=specs, out_specs=pl.BlockSpec((tr, 3 * ATT_W), lambda i: (i, 0)),
        out_shape=jax.ShapeDtypeStruct((t, 3 * ATT_W), BF16),
        scratch_shapes=[pltpu.VMEM((tr, LANES), F32)], compiler_params=_cp(1),
    )(*args)


def _att_delta(o, do, name, tr=1024):
    t = o.shape[0]
    tr = min(tr, t)

    def body(o_ref, do_ref, d_ref):
        dd = o_ref[...].astype(F32) * do_ref[...].astype(F32)
        d_ref[...] = jnp.concatenate(
            [jnp.broadcast_to(jnp.sum(dd[:, h * HEAD_DIM:(h + 1) * HEAD_DIM], axis=-1, keepdims=True), (tr, HEAD_DIM))
             for h in range(HEADS)], axis=1)

    row = pl.BlockSpec((tr, ATT_GW), lambda i: (i, 0))
    return pl.pallas_call(
        body, name=name, grid=(t // tr,), in_specs=[row, row], out_specs=row,
        out_shape=jax.ShapeDtypeStruct((t, ATT_GW), F32), compiler_params=_cp(1),
    )(o, do)


def _att_bwd(q, k, v, bias, delta, do, lse, seq_len, name, cols=(0, 0, 0)):
    t = q.shape[0]
    tq, sb = _att_tiles(seq_len)
    nbl = seq_len // tq
    n_res = t // seq_len
    per = tq // ATT_RADIUS
    w = sb + 2 * ATT_RADIUS
    last_blk = t // ATT_RADIUS - 1
    acc_cols = 2 * tq - sb + w
    scale = HEAD_DIM ** -0.5

    def body(q_ref, kp_ref, kc_ref, kn_ref, vp_ref, vc_ref, vn_ref, b_ref, dl_ref, do_ref, l_ref,
             dq_ref, dk_ref, dv_ref, db_ref, ak_ref, av_ref):
        r, n = pl.program_id(0), pl.program_id(1)

        @pl.when((r == 0) & (n == 0))
        def _():
            db_ref[...] = jnp.zeros_like(db_ref)

        @pl.when(n == 0)
        def _():
            ak_ref[...] = jnp.zeros_like(ak_ref)
            av_ref[...] = jnp.zeros_like(av_ref)

        @pl.when(n < nbl)
        def _():
            kw = jnp.concatenate([kp_ref[...], kc_ref[...], kn_ref[...]], axis=0)
            vw = jnp.concatenate([vp_ref[...], vc_ref[...], vn_ref[...]], axis=0)
            kpos = n * tq - ATT_RADIUS + lax.broadcasted_iota(jnp.int32, (1, tq + 2 * ATT_RADIUS), 1)
            valid = (kpos >= 0) & (kpos < seq_len)
            for j in range(tq // sb):
                rows = slice(j * sb, (j + 1) * sb)
                qj, doj = q_ref[rows, :], do_ref[rows, :]
                dlj = dl_ref[rows, :]
                lj = l_ref[rows, :]
                kj, vj, okj = kw[j * sb:j * sb + w], vw[j * sb:j * sb + w], valid[:, j * sb:j * sb + w]
                heads = [slice(h * HEAD_DIM, (h + 1) * HEAD_DIM) for h in range(HEADS)]
                ss = [lax.dot_general(qj[:, hs], kj[:, hs], NT, preferred_element_type=F32) for hs in heads]
                dps = [lax.dot_general(doj[:, hs], vj[:, hs], NT, preferred_element_type=F32) for hs in heads]
                ps, dss, dsbs = [], [], []
                for h, hs in enumerate(heads):
                    s = jnp.where(okj, ss[h] * scale + b_ref[h], NEG_INF)
                    p = jnp.exp(s - lj[:, h * HEAD_DIM:h * HEAD_DIM + 1])
                    ds = p * (dps[h] - dlj[:, h * HEAD_DIM:h * HEAD_DIM + 1])
                    ps.append(p.astype(BF16))
                    dss.append(ds)
                    dsbs.append(ds.astype(BF16))
                dqs = [jnp.dot(dsbs[h], kj[:, hs], preferred_element_type=F32) * scale for h, hs in enumerate(heads)]
                dks = [lax.dot_general(qj[:, hs], dsbs[h], (((0,), (0,)), ((), ())), preferred_element_type=F32) * scale
                       for h, hs in enumerate(heads)]
                dvs = [lax.dot_general(doj[:, hs], ps[h], (((0,), (0,)), ((), ())), preferred_element_type=F32)
                       for h, hs in enumerate(heads)]
                dq_ref[rows, :] = jnp.concatenate(dqs, axis=1).astype(BF16)
                c0 = tq + j * sb
                ak_ref[:, c0:c0 + w] += jnp.concatenate(dks, axis=0)
                av_ref[:, c0:c0 + w] += jnp.concatenate(dvs, axis=0)
                for h in range(HEADS):
                    db_ref[h] += dss[h]

        dk_ref[...] = ak_ref[:, ATT_RADIUS:ATT_RADIUS + tq].T.astype(BF16)
        dv_ref[...] = av_ref[:, ATT_RADIUS:ATT_RADIUS + tq].T.astype(BF16)
        keep = acc_cols - tq
        nk, nv = ak_ref[:, tq:acc_cols], av_ref[:, tq:acc_cols]
        ak_ref[:, 0:keep] = nk
        av_ref[:, 0:keep] = nv
        ak_ref[:, keep:acc_cols] = jnp.zeros((ATT_GW, tq), F32)
        av_ref[:, keep:acc_cols] = jnp.zeros((ATT_GW, tq), F32)

    def tile(r, n):
        return r * nbl + jnp.minimum(n, nbl - 1)

    main = pl.BlockSpec((tq, ATT_GW), lambda r, n: (tile(r, n), 0))
    lag = pl.BlockSpec((tq, ATT_GW), lambda r, n: (r * nbl + jnp.maximum(n - 1, 0), 0))
    bspec = pl.BlockSpec((HEADS, sb, w), lambda r, n: (0, 0, 0))
    return pl.pallas_call(
        body, name=name, grid=(n_res, nbl + 1),
        in_specs=[*_qkv_specs(tq, per, last_blk, cols, tile), bspec, main, main, main],
        out_specs=[main, lag, lag, bspec],
        out_shape=[jax.ShapeDtypeStruct((t, ATT_GW), BF16)] * 3 + [jax.ShapeDtypeStruct((HEADS, sb, w), F32)],
        scratch_shapes=[pltpu.VMEM((ATT_GW, acc_cols), F32), pltpu.VMEM((ATT_GW, acc_cols), F32)],
        compiler_params=_cp(2),
    )(q, k, k, k, v, v, v, bias, delta, do, lse)


def _bias_grad(db, idx, name):
    _, sb, w = db.shape

    def body(db_ref, idx_ref, o_ref):
        ix = idx_ref[...]
        lane = lax.broadcasted_iota(jnp.int32, (1, 128), 1)
        rows = []
        for h in range(HEADS):
            d = db_ref[h]
            acc = jnp.zeros((1, 128), F32)
            for b in range(NUM_BUCKETS):
                acc = acc + jnp.where(lane == b, jnp.sum(jnp.where(ix == b, d, 0.0)), 0.0)
            rows.append(acc)
        o_ref[...] = jnp.concatenate(rows + [jnp.zeros((8 - HEADS, 128), F32)], axis=0)

    return pl.pallas_call(
        body, name=name, out_shape=jax.ShapeDtypeStruct((8, 128), F32),
        in_specs=[pl.BlockSpec(memory_space=pltpu.VMEM), pl.BlockSpec(memory_space=pltpu.VMEM)],
        out_specs=pl.BlockSpec(memory_space=pltpu.VMEM),
    )(db, idx)


def _mem_fwd(zq, kv, name, tq=512):
    t = zq.shape[0]
    tq = min(tq, t)
    scale = MEM_HD ** -0.5

    def body(q_ref, kv_ref, o_ref):
        outs = []
        for h in range(MEM_HEADS):
            hs = slice(h * MEM_HD, (h + 1) * MEM_HD)
            kh = kv_ref[:, h * MEM_HD:(h + 1) * MEM_HD]
            vh = kv_ref[:, MEM_W + h * MEM_HD:MEM_W + (h + 1) * MEM_HD]
            s = lax.dot_general(q_ref[:, hs], kh, (((1,), (1,)), ((), ())), preferred_element_type=F32) * scale
            e = jnp.exp(s - jnp.max(s, axis=-1, keepdims=True))
            den = jnp.sum(e, axis=-1, keepdims=True)
            outs.append(jnp.dot(e.astype(BF16), vh, preferred_element_type=F32) / den)
        o_ref[...] = jnp.concatenate(outs, axis=1).astype(BF16)

    row = pl.BlockSpec((tq, MEM_W), lambda i: (i, 0))
    return pl.pallas_call(
        body, name=name, grid=(t // tq,),
        in_specs=[row, pl.BlockSpec((N_MEM, 2 * MEM_W), lambda i: (0, 0))], out_specs=row,
        out_shape=jax.ShapeDtypeStruct((t, MEM_W), BF16), compiler_params=_cp(1),
    )(zq, kv)


def _mem_bwd(zq, kv, do, name, tq=512):
    t = zq.shape[0]
    tq = min(tq, t)
    nt = t // tq
    scale = MEM_HD ** -0.5

    def body(q_ref, kv_ref, do_ref, dq_ref, out_ref, dkv_ref):
        @pl.when(pl.program_id(0) == 0)
        def _():
            dkv_ref[...] = jnp.zeros_like(dkv_ref)

        ks = [slice(h * MEM_HD, (h + 1) * MEM_HD) for h in range(MEM_HEADS)]
        vs = [slice(MEM_W + h * MEM_HD, MEM_W + (h + 1) * MEM_HD) for h in range(MEM_HEADS)]
        ss = [lax.dot_general(q_ref[:, k], kv_ref[:, k], NT, preferred_element_type=F32) * scale for k in ks]
        dps = [lax.dot_general(do_ref[:, k], kv_ref[:, v], NT, preferred_element_type=F32) for k, v in zip(ks, vs)]
        pbs, dsbs = [], []
        for s, dp in zip(ss, dps):
            e = jnp.exp(s - jnp.max(s, axis=-1, keepdims=True))
            p = e / jnp.sum(e, axis=-1, keepdims=True)
            ds = p * (dp - jnp.sum(p * dp, axis=-1, keepdims=True))
            pbs.append(p.astype(BF16))
            dsbs.append(ds.astype(BF16))
        dq_ref[...] = jnp.concatenate([jnp.dot(dsb, kv_ref[:, k], preferred_element_type=F32) * scale
                                       for dsb, k in zip(dsbs, ks)], axis=1).astype(BF16)
        for dsb, pb, k, v in zip(dsbs, pbs, ks, vs):
            dkv_ref[k, :] += jnp.dot(q_ref[:, k].T, dsb, preferred_element_type=F32) * scale
            dkv_ref[v, :] += jnp.dot(do_ref[:, k].T, pb, preferred_element_type=F32)

        @pl.when(pl.program_id(0) == nt - 1)
        def _():
            out_ref[...] = dkv_ref[...].T

    row = pl.BlockSpec((tq, MEM_W), lambda i: (i, 0))
    full = pl.BlockSpec((N_MEM, 2 * MEM_W), lambda i: (0, 0))
    return pl.pallas_call(
        body, name=name, grid=(nt,), in_specs=[row, full, row], out_specs=[row, full],
        out_shape=[jax.ShapeDtypeStruct((t, MEM_W), BF16), jax.ShapeDtypeStruct((N_MEM, 2 * MEM_W), F32)],
        scratch_shapes=[pltpu.VMEM((2 * MEM_W, N_MEM), F32)], compiler_params=_cp(1),
    )(zq, kv, do)


NT = (((1,), (1,)), ((), ()))


def _branches_fwd(acts, ws, zg, bg, name, tr=512):
    t = zg.shape[0]
    tr = min(tr, t)

    def body(a0, a1, a2, w0, w1, w2, z_ref, b_ref, y0, y1, y2, o_ref):
        merged = None
        for i, (a_ref, w_ref, y_ref) in enumerate(((a0, w0, y0), (a1, w1, y1), (a2, w2, y2))):
            y = lax.dot_general(a_ref[...], w_ref[...], NT, preferred_element_type=F32)
            y_ref[...] = y.astype(BF16)
            g = _sigmoid(z_ref[:, i * D:(i + 1) * D].astype(F32) + b_ref[:, i * D:(i + 1) * D])
            merged = g * y if merged is None else merged + g * y
        o_ref[...] = merged.astype(BF16)

    row = pl.BlockSpec((tr, D), lambda i: (i, 0))
    outs = pl.pallas_call(
        body, name=name, grid=(t // tr,),
        in_specs=[pl.BlockSpec((tr, a.shape[1]), lambda i: (i, 0)) for a in acts]
        + [pl.BlockSpec(w.shape, lambda i: (0, 0)) for w in ws]
        + [pl.BlockSpec((tr, 3 * D), lambda i: (i, 0)), pl.BlockSpec((1, 3 * D), lambda i: (0, 0))],
        out_specs=[row] * 4, out_shape=[jax.ShapeDtypeStruct((t, D), BF16)] * 4, compiler_params=_cp(1),
    )(*acts, *ws, zg, _row(bg))
    return tuple(outs[:3]), outs[3]


def _branches_bwd(dymix, w_out, ws, zg, bg, ys, name, tr=512):
    t = zg.shape[0]
    tr = min(tr, t)

    def body(dy_ref, wo_ref, w0, w1, w2, z_ref, b_ref, y0, y1, y2, d0, d1, d2, dz_ref, db_ref, da0, da1, da2):
        @pl.when(pl.program_id(0) == 0)
        def _():
            db_ref[...] = jnp.zeros_like(db_ref)

        dm = lax.dot_general(dy_ref[...], wo_ref[...], NT, preferred_element_type=F32)
        for i, (y_ref, w_ref, d_ref, da_ref) in enumerate(((y0, w0, d0, da0), (y1, w1, d1, da1), (y2, w2, d2, da2))):
            gi = _sigmoid(z_ref[:, i * D:(i + 1) * D].astype(F32) + b_ref[:, i * D:(i + 1) * D])
            dy = (dm * gi).astype(BF16)
            d_ref[...] = dy
            dz = dm * y_ref[...].astype(F32) * gi * (1.0 - gi)
            dz_ref[:, i * D:(i + 1) * D] = dz.astype(BF16)
            db_ref[0:1, i * D:(i + 1) * D] += jnp.sum(dz, axis=0, keepdims=True)
            da_ref[...] = jnp.dot(dy, w_ref[...], preferred_element_type=F32).astype(BF16)

    row = pl.BlockSpec((tr, D), lambda i: (i, 0))
    wide = pl.BlockSpec((tr, 3 * D), lambda i: (i, 0))
    outs = pl.pallas_call(
        body, name=name, grid=(t // tr,),
        in_specs=[row, pl.BlockSpec(w_out.shape, lambda i: (0, 0))] + [pl.BlockSpec(w.shape, lambda i: (0, 0)) for w in ws]
        + [wide, pl.BlockSpec((1, 3 * D), lambda i: (0, 0)), row, row, row],
        out_specs=[row, row, row, wide, pl.BlockSpec((8, 3 * D), lambda i: (0, 0))]
        + [pl.BlockSpec((tr, w.shape[1]), lambda i: (i, 0)) for w in ws],
        out_shape=[jax.ShapeDtypeStruct((t, D), BF16)] * 3
        + [jax.ShapeDtypeStruct((t, 3 * D), BF16), jax.ShapeDtypeStruct((8, 3 * D), F32)]
        + [jax.ShapeDtypeStruct((t, w.shape[1]), BF16) for w in ws],
        compiler_params=_cp(1),
    )(dymix, w_out, *ws, zg, _row(bg), *ys)
    return tuple(outs[:3]), outs[3], outs[4][0], tuple(outs[5:])


def _adamw(w, g, m, v, name):
    shape = w.shape
    w, g, m, v = (a.reshape(-1, shape[-1]) for a in (w, g, m, v))
    r, d = w.shape
    tr = next((c for c in (512, 352, 256, 128, 64, 32, 16, 8) if r % c == 0), r)

    def body(w_ref, g_ref, m_ref, v_ref, d_ref, nm_ref, nv_ref):
        gg = g_ref[...]
        m2 = ADAM_B1 * m_ref[...] + (1.0 - ADAM_B1) * gg
        v2 = ADAM_B2 * v_ref[...] + (1.0 - ADAM_B2) * (gg * gg)
        m_hat = m2 / (1.0 - ADAM_B1 ** ADAM_STEP)
        v_hat = v2 / (1.0 - ADAM_B2 ** ADAM_STEP)
        d_ref[...] = -ADAM_LR * (m_hat / (jnp.sqrt(v_hat) + ADAM_EPS) + ADAM_WD * w_ref[...])
        nm_ref[...] = m2
        nv_ref[...] = v2

    row = pl.BlockSpec((tr, d), lambda i: (i, 0))
    outs = pl.pallas_call(
        body, name=name, grid=(r // tr,), in_specs=[row] * 4, out_specs=[row] * 3,
        out_shape=[jax.ShapeDtypeStruct((r, d), F32)] * 3, compiler_params=_cp(1),
    )(w, g, m, v)
    return [o.reshape(shape) for o in outs]


def _slab_tile(rows):
    return next((c for c in range(min(rows, 512), 15, -16) if rows % c == 0), rows)


def _sum_slots(parts, out_dtype, name):
    n, r, d = parts.shape
    tr = _slab_tile(r)

    def body(p_ref, o_ref):
        acc = p_ref[0].astype(F32)
        for s in range(1, n):
            acc = acc + p_ref[s].astype(F32)
        o_ref[...] = acc.astype(o_ref.dtype)

    return pl.pallas_call(
        body, name=name, grid=(r // tr,), in_specs=[pl.BlockSpec((n, tr, d), lambda i: (0, i, 0))],
        out_specs=pl.BlockSpec((tr, d), lambda i: (i, 0)),
        out_shape=jax.ShapeDtypeStruct((r, d), out_dtype), compiler_params=_cp(1),
    )(parts)


def _place():
    return lax.axis_index("x"), lax.axis_index("y"), lax.axis_index("c")


def _all_gather(shard, name):
    r, d = shard.shape

    def body(x_ref, out_ref, send_sems, recv_sems, local_sem):
        x, y, c = _place()
        me, sibling = (x, y, c), (x, y, 1 - c)
        chips = [(1 - x, y), (x, 1 - y), (1 - x, 1 - y)]

        def slot(px, py, pc):
            return out_ref.at[4 * px + 2 * py + pc]

        def copy(k, block, to, src=None):
            return pltpu.make_async_remote_copy(
                src_ref=slot(*block) if src is None else src, dst_ref=slot(*block),
                send_sem=send_sems.at[k], recv_sem=recv_sems.at[k], device_id=to, device_id_type=MESH)

        mine = pltpu.make_async_copy(x_ref, slot(*me), local_sem)
        mine.start()
        first = [copy(0, me, sibling, src=x_ref)]
        first += [copy(1 + j, me, (*chip, c), src=x_ref) for j, chip in enumerate(chips)]
        for cp in first:
            cp.start()
        passed = [copy(4 + j, (*chip, c), sibling) for j, chip in enumerate(chips)]
        for j, chip in enumerate(chips):
            copy(1 + j, (*chip, c), me).wait_recv()
            passed[j].start()
        copy(0, sibling, me).wait_recv()
        for j, chip in enumerate(chips):
            copy(4 + j, (*chip, 1 - c), me).wait_recv()
        for cp in first + passed:
            cp.wait_send()
        mine.wait()

    return pl.pallas_call(
        body, name=name, out_shape=jax.ShapeDtypeStruct((N_DEV, r, d), shard.dtype),
        in_specs=[pl.BlockSpec(memory_space=pl.ANY)], out_specs=pl.BlockSpec(memory_space=pl.ANY),
        scratch_shapes=[pltpu.SemaphoreType.DMA((7,)), pltpu.SemaphoreType.DMA((7,)), pltpu.SemaphoreType.DMA],
    )(shard)


HBM_SPEC = pl.BlockSpec(memory_space=pltpu.HBM)
SEM_SPEC = pl.BlockSpec(memory_space=pltpu.SEMAPHORE)
SPLIT_PARAMS = pltpu.CompilerParams(has_side_effects=pltpu.SideEffectType.DATAFLOW_SIDE_EFFECTING)


def _peers():
    x, y, c = _place()
    flip = lambda v, bit: 1 - v if bit else v
    return 4 * x + 2 * y + c, [((flip(x, k >> 2 & 1), flip(y, k >> 1 & 1), flip(c, k & 1))) for k in range(1, N_DEV)]


def _exchange_start(src, all_gather, name):
    r, d = src.shape[-2:]

    def body(src_ref, land_ref, send_sems, recv_sems, src_thru, land_thru, token):
        me, peers = _peers()
        for k, (px, py, pc) in enumerate(peers):
            part = src_ref if all_gather else src_ref.at[4 * px + 2 * py + pc]
            pltpu.make_async_remote_copy(src_ref=part, dst_ref=land_ref.at[me], send_sem=send_sems.at[k],
                                         recv_sem=recv_sems.at[k], device_id=(px, py, pc), device_id_type=MESH).start()
        token[...] = jnp.zeros_like(token)

    land = lax.empty((N_DEV, r, d), src.dtype)
    return pl.pallas_call(
        body, name=name,
        out_shape=(pltpu.SemaphoreType.DMA((N_DEV - 1,)), pltpu.SemaphoreType.DMA((N_DEV - 1,)), pltpu.HBM(src.shape, src.dtype),
                   pltpu.HBM(land.shape, land.dtype), jax.ShapeDtypeStruct((8, LANES), F32)),
        in_specs=(HBM_SPEC, HBM_SPEC), out_specs=(SEM_SPEC, SEM_SPEC, HBM_SPEC, HBM_SPEC, pl.BlockSpec(memory_space=pltpu.VMEM)),
        input_output_aliases={0: 2, 1: 3}, compiler_params=SPLIT_PARAMS,
    )(pltpu.with_memory_space_constraint(src, pltpu.HBM), pltpu.with_memory_space_constraint(land, pltpu.HBM))


def _exchange_wait(started, after, all_gather, name):
    send_sems, recv_sems, src_thru, land_thru, _ = started

    def body(src_ref, land_ref, send_sems, recv_sems, after_ref, src_out, land_out):
        me, peers = _peers()
        for k, (px, py, pc) in enumerate(peers):
            part = src_ref if all_gather else src_ref.at[4 * px + 2 * py + pc]
            cp = pltpu.make_async_remote_copy(src_ref=part, dst_ref=land_ref.at[4 * px + 2 * py + pc], send_sem=send_sems.at[k],
                                              recv_sem=recv_sems.at[k], device_id=(px, py, pc), device_id_type=MESH)
            cp.wait_send()
            cp.wait_recv()

    return pl.pallas_call(
        body, name=name,
        out_shape=(pltpu.HBM(src_thru.shape, src_thru.dtype), pltpu.HBM(land_thru.shape, land_thru.dtype)),
        in_specs=(HBM_SPEC, HBM_SPEC, SEM_SPEC, SEM_SPEC, pl.BlockSpec(memory_space=pl.ANY)), out_specs=(HBM_SPEC, HBM_SPEC),
        input_output_aliases={0: 0, 1: 1}, compiler_params=SPLIT_PARAMS,
    )(src_thru, land_thru, send_sems, recv_sems, after)


def _own_slot(land, mine):
    x, y, c = _place()
    return lax.dynamic_update_slice(land, mine[None], (4 * x + 2 * y + c, 0, 0))


BIG = (("w_in", (D, 864), 1), ("w_conv_out", (CONV_W, 128), 1), ("w_att_out", (ATT_GW, 128), 1), ("w_mem_kv", (128, D), 0),
       ("w_mem_out", (MEM_W, 128), 1), ("w_out", (128, D), 0), ("w_ffn_in", (D, 704), 1), ("w_ffn_out", (352, D), 0),
       ("conv_dw", (CONV_K, 64), 1))
SMALL = (("rel_bias", (NUM_BUCKETS, 12)), ("norm_mix_pre", (DEPTH, D)), ("b_gate", (DEPTH, 3 * D)),
         ("conv_dw_bias", (DEPTH, CONV_W)), ("conv_ln_g", (DEPTH, CONV_W)), ("conv_ln_b", (DEPTH, CONV_W)),
         ("norm_mem", (DEPTH, D)), ("norm_mix_post", (DEPTH, D)), ("norm_ffn_pre", (DEPTH, D)), ("norm_ffn_post", (DEPTH, D)))
TWIN_WEIGHTS = ("rel_bias", "norm_mix_pre", "w_in", "b_gate", "conv_dw", "conv_dw_bias", "conv_ln_g", "conv_ln_b", "w_conv_out",
                "w_att_out", "norm_mem", "w_mem_kv", "w_mem_out", "w_out", "norm_mix_post", "norm_ffn_pre", "w_ffn_in",
                "w_ffn_out", "norm_ffn_post")


BIG_INFO = {n: (s, a) for n, s, a in BIG}
GROUPS = {"in": ("w_in", "conv_dw"), "mix": ("w_conv_out", "w_att_out", "w_mem_kv", "w_mem_out", "w_out"),
          "ffn": ("w_ffn_in", "w_ffn_out")}
TILE_BYTES_PER_LANE = 32


def _rows_of(shape, dtype):
    tile = TILE_BYTES_PER_LANE // jnp.dtype(dtype).itemsize
    return -(-int(np.prod(shape)) // (D * tile)) * tile


def _row_form(shape, axis):
    return tuple(shape) if axis == 0 else (shape[1], shape[0])


def _as_slab_rows(p, lead=()):
    shape = p.shape[len(lead):]
    rows = _rows_of(shape, p.dtype)
    if shape == (rows, D):
        return p
    n = int(np.prod(shape))
    p = jnp.pad(p.reshape(*lead, n), [(0, 0)] * len(lead) + [(0, rows * D - n)])
    return p.reshape(*lead, rows, D)


def _from_slab_rows(rows, shape, lead=()):
    if rows.shape[len(lead):] == tuple(shape):
        return rows
    n = int(np.prod(shape))
    return rows.reshape(*lead, -1)[..., :n].reshape(*lead, *shape)


def _pack(pieces, dtype):
    return jnp.concatenate([_as_slab_rows(p.astype(dtype)) for p in pieces], axis=0)


def _unpack(slab, shapes, packed_as):
    out, r0 = [], 0
    for s in shapes:
        rows = _rows_of(s, packed_as)
        out.append(_from_slab_rows(slab[r0:r0 + rows], s))
        r0 += rows
    return out


def _pack_group(blocks, layer, group):
    return _pack([blocks[n][layer] if BIG_INFO[n][1] == 0 else blocks[n][layer].T for n in GROUPS[group]], BF16)


def _full_weights(gathered, group):
    out, r0 = {}, 0
    for name in GROUPS[group]:
        r, k = _row_form(*BIG_INFO[name])
        rows = _rows_of((r, k), BF16)
        out[name] = _from_slab_rows(gathered[:, r0:r0 + rows], (r, k), lead=(N_DEV,)).reshape(N_DEV * r, k)
        r0 += rows
    return out


def _scatter_layout(grads, group):
    parts = []
    for name in GROUPS[group]:
        r, k = _row_form(*BIG_INFO[name])
        parts.append(_as_slab_rows(grads[name].astype(BF16).reshape(N_DEV, r, k), lead=(N_DEV,)))
    return jnp.concatenate(parts, axis=1)


def _unpack_blocks(slab, group):
    out = _unpack(slab, [_row_form(*BIG_INFO[n]) for n in GROUPS[group]], BF16)
    return {n: b if BIG_INFO[n][1] == 0 else b.T for n, b in zip(GROUPS[group], out)}


def _split_w_in(w_in):
    edges = np.cumsum((0,) + SEG)
    return [w_in[edges[i]:edges[i + 1]] for i in range(4)]


def _conv_taps(conv_dw):
    return jnp.pad(conv_dw.astype(F32).T, ((0, 32 - CONV_K), (0, 0)))


def _layer_fwd(l, x, h, nm, get_w, sm, bias_tabs):
    t = x.shape[0]
    win = get_w(l, "in", h)
    w_seg = _split_w_in(win["w_in"])
    zc = _mm(h, w_seg[0], "fwd_in_conv", tb=True)
    za = _mm(h, w_seg[1], "fwd_in_att", tb=True, tn=1152)
    zq = _mm(h, w_seg[2], "fwd_in_memq", tb=True)
    zg = _mm(h, w_seg[3], "fwd_in_gate", tb=True)
    y, cact = _conv_fwd(zc, _conv_taps(win["conv_dw"]), sm["conv_dw_bias"][l], sm["conv_ln_g"][l], sm["conv_ln_b"][l],
                        "conv_fwd")
    wts = get_w(l, "mix", cact)
    qkv, outs, lses = [], [], []
    for g, (_, dil) in enumerate(ATT_PATTERNS):
        cols = tuple(i * 3 + g for i in range(3))
        if dil == 1:
            trio, at = (za, za, za), cols
        else:
            trio, at = _to_residue_major([(za, c) for c in cols], dil, f"qkv_residue_d{dil}"), (0, 0, 0)
        o, ls = _att_fwd(*trio, bias_tabs[g], t // dil, f"att_fwd_d{dil}", cols=at)
        qkv.append((trio, at))
        outs.append(o)
        lses.append(ls)
    att, lse = _att_combine(outs, lses, "att_combine")
    kv = _mm(nm, wts["w_mem_kv"], "fwd_mem_kv")
    om = _mem_fwd(zq, kv, "mem_fwd")
    w_br = (wts["w_conv_out"], wts["w_att_out"], wts["w_mem_out"])
    ys, merged = _branches_fwd((cact, att, om), w_br, zg, sm["b_gate"][l], "branches_fwd")
    ymix, x1, h2 = _mm_norm_res(merged, wts["w_out"], x, sm["norm_mix_post"][l], sm["norm_ffn_pre"][l], "fwd_out_norm")
    wff = get_w(l, "ffn", h2)
    gff, uff, act = _mm_swiglu_fwd(h2, wff["w_ffn_in"], "fwd_ffn_in_swiglu")
    g_next = sm["norm_mix_pre"][l + 1] if l + 1 < DEPTH else None
    f, x2, h_next = _mm_norm_res(act, wff["w_ffn_out"], x1, sm["norm_ffn_post"][l], g_next,
                                 "fwd_ffn_out_norm" if g_next is not None else "fwd_ffn_out_last")
    saved = dict(x=x, h=h, zc=zc, zq=zq, zg=zg, y=y, cact=cact, qkv=qkv, att=att, lse=lse, kv=kv, om=om, ys=ys,
                 merged=merged, ymix=ymix, x1=x1, h2=h2, gff=gff, uff=uff, act=act, f=f)
    return x2, h_next, saved


def _after(value, *tokens):
    for t in tokens:
        if t is not None:
            value = value + t[0, 0].astype(value.dtype)
    return value


def _layer_bwd(l, dx2, s, nm, mem, get_w, put_g, sm, bias_tabs, bucket_idx):
    t = dx2.shape[0]
    win, wts, wff = get_w(l, "in", None), get_w(l, "mix", None), get_w(l, "ffn", None)
    gi, gb, gf, gs = {}, {}, {}, {}
    df, gs["norm_ffn_post"] = _norm_bwd(s["f"], sm["norm_ffn_post"][l], dx2, None, BF16, "bwd_norm_ffn_post")
    dg, du = _mm_swiglu_bwd(df, wff["w_ffn_out"], s["gff"], s["uff"], "bwd_ffn_out_d_swiglu")
    gf["w_ffn_out"] = _mm(s["act"], df, "bwd_ffn_out_w", ta=True, tm=1408)
    gf["w_ffn_in"] = jnp.concatenate([_mm(dg, s["h2"], "bwd_ffn_in_w_gate", ta=True, tm=1408),
                                      _mm(du, s["h2"], "bwd_ffn_in_w_up", ta=True, tm=1408)], axis=0)
    tok = put_g(l, "ffn", gf)
    dh2 = _mm_sum([(dg, _after(wff["w_ffn_in"][:FFN_H], tok), FFN_H // 2), (du, wff["w_ffn_in"][FFN_H:], FFN_H // 2)],
                  "bwd_ffn_in_d")
    dx1, dymix, gs["norm_ffn_pre"], gs["norm_mix_post"] = _norm_bwd_pair(
        s["x1"], sm["norm_ffn_pre"][l], dh2, dx2, s["ymix"], sm["norm_mix_post"][l], "bwd_norm_ffn_pre_mix_post")
    gb["w_out"] = _mm(s["merged"], dymix, "bwd_out_w", ta=True)
    w_br = (wts["w_conv_out"], wts["w_att_out"], wts["w_mem_out"])
    dys, dzg, gs["b_gate"], (dcact, datt, dom) = _branches_bwd(dymix, wts["w_out"], w_br, s["zg"], sm["b_gate"][l], s["ys"],
                                                               "branches_bwd")
    gb["w_conv_out"] = _mm(dys[0], s["cact"], "bwd_conv_out_w", ta=True)
    gb["w_att_out"] = _mm(dys[1], s["att"], "bwd_att_out_w", ta=True)
    delta = _att_delta(s["att"], datt, "att_delta")
    dqkv, rel = [], []
    for g, (_, dil) in enumerate(ATT_PATTERNS):
        trio, at = s["qkv"][g]
        if dil == 1:
            dl_r, do_r, lse_r = delta, datt, s["lse"]
        else:
            dl_r, do_r, lse_r = _to_residue_major([(delta, 0), (datt, 0), (s["lse"], 0)], dil, f"att_bwd_residue_d{dil}")
        dq, dk, dv, dbias = _att_bwd(*trio, bias_tabs[g], dl_r, do_r, lse_r, t // dil, f"att_bwd_d{dil}", cols=at)
        dqkv.append((dq, dk, dv))
        rel.append(_bias_grad(dbias, bucket_idx[g], f"bias_grad_d{dil}")[:HEADS, :NUM_BUCKETS])
    dza = _assemble_dza(dqkv, "att_bwd_assemble")
    gs["rel_bias"] = jnp.concatenate(rel, axis=0).T
    gb["w_mem_out"] = _mm(dys[2], s["om"], "bwd_mem_out_w", ta=True)
    dzq, dkv = _mem_bwd(s["zq"], s["kv"], dom, "mem_bwd")
    dkv = dkv.astype(BF16)
    gb["w_mem_kv"] = _mm(nm, dkv, "bwd_mem_kv_w", ta=True)
    tok = put_g(l, "mix", gb)
    dnm = _mm(dkv, wts["w_mem_kv"], "bwd_mem_kv_d", tb=True, out_dtype=F32)
    _, gs["norm_mem"] = _norm_bwd(mem, sm["norm_mem"][l], dnm, None, BF16, "bwd_norm_mem")
    dzc, dconv, gs["conv_dw_bias"], gs["conv_ln_g"], gs["conv_ln_b"] = _conv_bwd(
        s["zc"], s["y"], dcact, _after(_conv_taps(win["conv_dw"]), tok), sm["conv_ln_g"][l], sm["conv_ln_b"][l], "conv_bwd")
    gi["conv_dw"] = dconv[:CONV_K].T
    segs = ((dzc, "conv", 1024), (dza, "att", 1152), (dzq, "memq", 512), (dzg, "gate", 1024))
    gi["w_in"] = jnp.concatenate([_mm(dz, s["h"], f"bwd_in_{nm_}_w", ta=True, tm=min(blk, 1152), tk=2048)
                                  for dz, nm_, blk in segs], axis=0)
    tok = put_g(l, "in", gi)
    w_seg = _split_w_in(win["w_in"])
    w_seg[0] = _after(w_seg[0], tok)
    dx, gs["norm_mix_pre"] = _mm_sum([(dz, w_seg[i], blk) for i, (dz, _, blk) in enumerate(segs)], "bwd_in_d_norm",
                                     norm=(s["x"], sm["norm_mix_pre"][l], dx1))
    return dx, gs


def _local_step(x, mem, target, sm, get_w, put_g, tokens=()):
    t = x.shape[0]
    bias_tabs, bucket_idx = [], []
    for g, (_, dil) in enumerate(ATT_PATTERNS):
        idx = _bucket_index(_att_tiles(t // dil)[1], dil)
        bucket_idx.append(jnp.asarray(idx))
        bias_tabs.append(_bias_table(sm["rel_bias"][:, g * HEADS:(g + 1) * HEADS], bucket_idx[g], f"bias_table_d{dil}"))
    h = _norm_plain(x, _after(sm["norm_mix_pre"][0], *tokens), "norm_first")
    saved, nms = [], []
    for l in range(DEPTH):
        nm = _norm_plain(mem, sm["norm_mem"][l], "norm_mem")
        x, h, s = _layer_fwd(l, x, h, nm, get_w, sm, bias_tabs)
        saved.append(s)
        nms.append(nm)
    loss, dx = _loss_head(x, target, "loss_head")
    gsmall = {}
    for l in reversed(range(DEPTH)):
        dx, gs = _layer_bwd(l, dx, saved[l], nms[l], mem, get_w, put_g, sm, bias_tabs, bucket_idx)
        for n, v in gs.items():
            gsmall.setdefault(n, {})[l] = v
    small = {}
    for n, _ in SMALL:
        small[n] = gsmall[n][0] + gsmall[n][1] if n == "rel_bias" else jnp.stack([gsmall[n][0], gsmall[n][1]])
    return loss, dx, small


def kernel(x, mem, rel_bias, norm_mix_pre, w_in, b_gate, conv_dw, conv_dw_bias, conv_ln_g, conv_ln_b, w_conv_out, w_att_out, norm_mem, w_mem_kv, w_mem_out, w_out, norm_mix_post, norm_ffn_pre, w_ffn_in, w_ffn_out, norm_ffn_post, loss_target, m_rel_bias, m_norm_mix_pre, m_w_in, m_b_gate, m_conv_dw, m_conv_dw_bias, m_conv_ln_g, m_conv_ln_b, m_w_conv_out, m_w_att_out, m_norm_mem, m_w_mem_kv, m_w_mem_out, m_w_out, m_norm_mix_post, m_norm_ffn_pre, m_w_ffn_in, m_w_ffn_out, m_norm_ffn_post, v_rel_bias, v_norm_mix_pre, v_w_in, v_b_gate, v_conv_dw, v_conv_dw_bias, v_conv_ln_g, v_conv_ln_b, v_w_conv_out, v_w_att_out, v_norm_mem, v_w_mem_kv, v_w_mem_out, v_w_out, v_norm_mix_post, v_norm_ffn_pre, v_w_ffn_in, v_w_ffn_out, v_norm_ffn_post):
    args = dict(locals())
    w = {n: args[n] for n in TWIN_WEIGHTS}
    m = {n: args["m_" + n] for n in TWIN_WEIGHTS}
    v = {n: args["v_" + n] for n in TWIN_WEIGHTS}
    sm = {n: w[n] for n, _ in SMALL}

    first = (0, "in")
    keys = [(l, grp) for l in range(DEPTH) for grp in GROUPS]
    shard = {(l, grp): _pack_group(w, l, grp) for l, grp in keys}
    weights = {first: _full_weights(_all_gather(shard[first], "gather_l0_in"), "in")}
    settled = weights[first]["w_in"][:1, :1] * 0
    ag = {}
    for k in keys:
        if k != first:
            ag[k] = _exchange_start(_after(shard[k], settled), True, f"ag_start_l{k[0]}_{k[1]}")
            settled = ag[k][4]

    def get_w(l, grp, after):
        if (l, grp) not in weights:
            mine, land = _exchange_wait(ag[l, grp], after, True, f"ag_wait_l{l}_{grp}")
            weights[l, grp] = _full_weights(_own_slot(land, mine), grp)
        return weights[l, grp]

    rs = {}

    def put_g(l, grp, grads):
        rs[l, grp] = _exchange_start(_scatter_layout(grads, grp), False, f"rs_start_l{l}_{grp}")
        return rs[l, grp][4]

    loss, dx, gsmall = _local_step(x[0], mem[0], loss_target[0], sm, get_w, put_g, [st[4] for st in ag.values()])
    loss = lax.psum(loss, ("x", "y", "c"))

    xi, yi, ci = _place()
    me = 4 * xi + 2 * yi + ci
    g_slab = {}
    for k, st in rs.items():
        contrib, land = _exchange_wait(st, dx, False, f"rs_wait_l{k[0]}_{k[1]}")
        own = lax.dynamic_index_in_dim(contrib, me, axis=0, keepdims=False)
        g_slab[k] = _sum_slots(_own_slot(land, own), F32, f"rs_sum_l{k[0]}_{k[1]}")
    g_layers = [{n: b for grp in GROUPS for n, b in _unpack_blocks(g_slab[l, grp], grp).items()} for l in range(DEPTH)]
    small_shapes = [s for _, s in SMALL]
    g_small = _unpack(_sum_slots(_all_gather(_pack([gsmall[n] for n, _ in SMALL], F32), "gather_small"), F32, "sum_small"),
                      small_shapes, F32)
    grads = {n: jnp.stack([g_layers[0][n], g_layers[1][n]]) for n, _, _ in BIG}
    grads.update({n: g_small[i] for i, (n, _) in enumerate(SMALL)})

    delta, new_m, new_v = {}, {}, {}
    for n in TWIN_WEIGHTS:
        delta[n], new_m[n], new_v[n] = _adamw(w[n], grads[n], m[n], v[n], f"adamw_{n}")

    return (loss, dx[None], *[grads[n] for n in TWIN_WEIGHTS], *[delta[n] for n in TWIN_WEIGHTS],
            *[new_m[n] for n in TWIN_WEIGHTS], *[new_v[n] for n in TWIN_WEIGHTS])
```

```python
import functools

import numpy as np
import jax
import jax.numpy as jnp
from jax import lax
from jax.experimental import pallas as pl
from jax.experimental.pallas import tpu as pltpu

F32 = jnp.float32
BF16 = jnp.bfloat16

N_DEV = 8
D = 1024
DEPTH = 2
CONV_W = 512
CONV_K = 31
CONV_PAD = 16
ATT_PATTERNS = ((128, 1), (512, 4), (2048, 16))
ATT_RADIUS = 64
HEADS = 4
HEAD_DIM = 64
ATT_GW = HEADS * HEAD_DIM
ATT_W = 3 * ATT_GW
MEM_HEADS = 4
MEM_HD = 128
MEM_W = 512
N_MEM = 256
FFN_H = 2816
NUM_BUCKETS = 32
MAX_DISTANCE = 1024
RMS_EPS = 1e-6
LN_EPS = 1e-5
NEG_INF = -1e30
SEG = (2 * CONV_W, 3 * ATT_W, MEM_W, 3 * D)
ADAM_LR, ADAM_B1, ADAM_B2, ADAM_EPS, ADAM_WD, ADAM_STEP = 0.001, 0.9, 0.999, 1e-08, 0.01, 10

VMEM_LIMIT_V7X = 56 * 1024 * 1024
MESH = pl.DeviceIdType.MESH


def _cp(n_axes):
    return pltpu.CompilerParams(dimension_semantics=("arbitrary",) * n_axes, vmem_limit_bytes=VMEM_LIMIT_V7X)


def _row(v):
    return v.reshape(1, -1)


def _mm(a, b, name, *, ta=False, tb=False, out_dtype=BF16, tm=2048, tn=1024, tk=2048):
    kdim, m = a.shape if ta else a.shape[::-1]
    n, kb = b.shape if tb else b.shape[::-1]
    assert kb == kdim and a.dtype == BF16 and b.dtype == BF16
    tm, tn, tk = min(tm, m), min(tn, n), min(tk, kdim)
    assert m % tm == 0 and n % tn == 0 and kdim % tk == 0, (name, m, n, kdim, tm, tn, tk)
    nk = kdim // tk
    a_spec = pl.BlockSpec((tk, tm), lambda i, j, k: (k, i)) if ta else pl.BlockSpec((tm, tk), lambda i, j, k: (i, k))
    b_spec = pl.BlockSpec((tn, tk), lambda i, j, k: (j, k)) if tb else pl.BlockSpec((tk, tn), lambda i, j, k: (k, j))
    o_spec = pl.BlockSpec((tm, tn), lambda i, j, k: (i, j))
    dims = (((0 if ta else 1,), (1 if tb else 0,)), ((), ()))
    use_scratch = nk > 1 and out_dtype != F32

    def body(*refs):
        a_ref, b_ref, o_ref = refs[:3]
        p = lax.dot_general(a_ref[...], b_ref[...], dims, preferred_element_type=F32)
        if nk == 1:
            o_ref[...] = p.astype(o_ref.dtype)
            return
        k = pl.program_id(2)
        t_ref = refs[3] if use_scratch else o_ref

        @pl.when(k == 0)
        def _():
            t_ref[...] = p

        @pl.when(k > 0)
        def _():
            t_ref[...] += p

        if use_scratch:
            @pl.when(k == nk - 1)
            def _():
                o_ref[...] = t_ref[...].astype(o_ref.dtype)

    return pl.pallas_call(
        body, name=name, grid=(m // tm, n // tn, nk), in_specs=[a_spec, b_spec], out_specs=o_spec,
        out_shape=jax.ShapeDtypeStruct((m, n), out_dtype),
        scratch_shapes=[pltpu.VMEM((tm, tn), F32)] if use_scratch else [],
        compiler_params=_cp(3),
    )(a, b)


EPILOGUE_COLS = 512
EPILOGUE_ROWS = 512


def _chunks(size, step):
    return [(c, min(c + step, size)) for c in range(0, size, step)]


def _rms_bwd_val(v, g, dy):
    r = lax.rsqrt(jnp.mean(v * v, axis=-1, keepdims=True) + RMS_EPS)
    vhat = v * r
    dvh = dy * g
    dv = r * (dvh - vhat * jnp.mean(dvh * vhat, axis=-1, keepdims=True))
    return dv, jnp.sum(dy * vhat, axis=0, keepdims=True)


def _mm_sum(pairs, name, *, norm=None, tm=1024):
    m, n = pairs[0][0].shape[0], pairs[0][1].shape[1]
    tm = min(tm, m)
    starts, specs, total = [], [], 0
    for a, b, tk in pairs:
        assert a.shape == (m, b.shape[0]) and b.shape[1] == n and a.shape[1] % tk == 0 and a.dtype == b.dtype == BF16
        nk = a.shape[1] // tk
        starts.append((total, nk))
        step = functools.partial(lambda k, s0, nk_: jnp.clip(k - s0, 0, nk_ - 1), s0=total, nk_=nk)
        specs.append(pl.BlockSpec((tm, tk), functools.partial(lambda i, k, st: (i, st(k)), st=step)))
        specs.append(pl.BlockSpec((tk, n), functools.partial(lambda i, k, st: (st(k), 0), st=step)))
        total += nk
    np_ = 2 * len(pairs)
    row = pl.BlockSpec((tm, n), lambda i, k: (i, 0))

    def accumulate(refs, acc, k, upto=total):
        for p, (s0, nk) in enumerate(starts):
            @pl.when((k >= s0) & (k < min(s0 + nk, upto)))
            def _(p=p, s0=s0):
                part = jnp.dot(refs[2 * p][...], refs[2 * p + 1][...], preferred_element_type=F32)
                if s0 == 0:
                    @pl.when(k == 0)
                    def _():
                        acc[...] = part

                    @pl.when(k > 0)
                    def _():
                        acc[...] += part
                else:
                    acc[...] += part

    operands = [op for a, b, _ in pairs for op in (a, b)]
    if norm is None:
        def plain(*refs):
            k = pl.program_id(1)
            accumulate(refs, refs[np_ + 1], k)

            @pl.when(k == total - 1)
            def _():
                refs[np_][...] = refs[np_ + 1][...].astype(BF16)

        return pl.pallas_call(
            plain, name=name, grid=(m // tm, total), in_specs=specs, out_specs=row,
            out_shape=jax.ShapeDtypeStruct((m, n), BF16), scratch_shapes=[pltpu.VMEM((tm, n), F32)],
            compiler_params=_cp(2),
        )(*operands)

    v, g, dres = norm

    def body(*refs):
        v_ref, g_ref, dres_ref, dv_ref, dg_ref, acc = refs[np_:np_ + 6]
        i, k = pl.program_id(0), pl.program_id(1)
        accumulate(refs, acc, k, upto=total - 1)

        @pl.when((i == 0) & (k == 0))
        def _():
            dg_ref[...] = jnp.zeros_like(dg_ref)

        @pl.when(k == total - 1)
        def _():
            a_ref, b_ref = refs[np_ - 2], refs[np_ - 1]
            for r0, r1 in _chunks(tm, EPILOGUE_ROWS):
                dy = jnp.dot(a_ref[r0:r1, :], b_ref[...], preferred_element_type=F32)
                if total > 1:
                    dy = dy + acc[r0:r1, :]
                dv, dg = _rms_bwd_val(v_ref[r0:r1, :], g_ref[...], dy)
                dv_ref[r0:r1, :] = dv + dres_ref[r0:r1, :]
                dg_ref[0:1, :] += dg

    once = pl.BlockSpec((tm, n), lambda i, k: (i, 0), pipeline_mode=pl.Buffered(1))
    dv, dg = pl.pallas_call(
        body, name=name, grid=(m // tm, total),
        in_specs=specs + [once, pl.BlockSpec((1, n), lambda i, k: (0, 0)), once],
        out_specs=[row, pl.BlockSpec((8, n), lambda i, k: (0, 0))],
        out_shape=[jax.ShapeDtypeStruct((m, n), F32), jax.ShapeDtypeStruct((8, n), F32)],
        scratch_shapes=[pltpu.VMEM((tm, n), F32)], compiler_params=_cp(2),
    )(*operands, v, _row(g), dres)
    return dv, dg[0]


def _mm_swiglu_fwd(h, w, name, *, tm=1024, tn=1408):
    m, kdim = h.shape
    hid = w.shape[0] // 2
    tm = min(tm, m)
    nj = hid // tn
    dims = (((1,), (1,)), ((), ()))

    def body(h_ref, wg_ref, wu_ref, g_ref, u_ref, a_ref):
        hv = h_ref[...]
        for c0, c1 in _chunks(tn, EPILOGUE_COLS):
            g = lax.dot_general(hv, wg_ref[c0:c1, :], dims, preferred_element_type=F32)
            u = lax.dot_general(hv, wu_ref[c0:c1, :], dims, preferred_element_type=F32)
            g_ref[:, c0:c1] = g.astype(BF16)
            u_ref[:, c0:c1] = u.astype(BF16)
            a_ref[:, c0:c1] = (g * _sigmoid(g) * u).astype(BF16)

    out = pl.BlockSpec((tm, tn), lambda i, j: (i, j))
    return pl.pallas_call(
        body, name=name, grid=(m // tm, nj),
        in_specs=[pl.BlockSpec((tm, kdim), lambda i, j: (i, 0)), pl.BlockSpec((tn, kdim), lambda i, j: (j, 0)),
                  pl.BlockSpec((tn, kdim), lambda i, j: (j + nj, 0))],
        out_specs=[out, out, out], out_shape=[jax.ShapeDtypeStruct((m, hid), BF16)] * 3, compiler_params=_cp(2),
    )(h, w, w)


def _mm_swiglu_bwd(df, w, g, u, name, *, tm=1024, tn=1408):
    m, kdim = df.shape
    hid = w.shape[0]
    tm = min(tm, m)

    def body(df_ref, w_ref, g_ref, u_ref, dg_ref, du_ref):
        dfv = df_ref[...]
        for c0, c1 in _chunks(tn, EPILOGUE_COLS):
            da = lax.dot_general(dfv, w_ref[c0:c1, :], (((1,), (1,)), ((), ())), preferred_element_type=F32)
            gg, uu, dab = g_ref[:, c0:c1], u_ref[:, c0:c1], da.astype(BF16)
            sg = _sigmoid(gg)
            dg_ref[:, c0:c1] = dab * uu * (sg * (1.0 + gg * (1.0 - sg)))
            du_ref[:, c0:c1] = dab * gg * sg

    blk = pl.BlockSpec((tm, tn), lambda i, j: (i, j))
    return pl.pallas_call(
        body, name=name, grid=(m // tm, hid // tn),
        in_specs=[pl.BlockSpec((tm, kdim), lambda i, j: (i, 0)), pl.BlockSpec((tn, kdim), lambda i, j: (j, 0)), blk, blk],
        out_specs=[blk, blk], out_shape=[jax.ShapeDtypeStruct((m, hid), BF16)] * 2, compiler_params=_cp(2),
    )(df, w, g, u)


def _mm_norm_res(a, b, xres, g, g_next, name, *, tm=1024):
    m, kdim = a.shape
    n = b.shape[1]
    tm = min(tm, m)
    two = g_next is not None

    def body(*refs):
        a_ref, b_ref, x_ref, g_ref = refs[:4]
        y_ref, xn_ref = refs[4 + two], refs[5 + two]
        for r0, r1 in _chunks(tm, EPILOGUE_ROWS):
            y = jnp.dot(a_ref[r0:r1, :], b_ref[...], preferred_element_type=F32)
            y_ref[r0:r1, :] = y.astype(BF16)
            xn = x_ref[r0:r1, :] + _rms_val(y, g_ref[...])
            xn_ref[r0:r1, :] = xn
            if two:
                refs[7][r0:r1, :] = _rms_val(xn, refs[4][...]).astype(BF16)

    row = pl.BlockSpec((tm, n), lambda i: (i, 0))
    vec = pl.BlockSpec((1, n), lambda i: (0, 0))
    outs = pl.pallas_call(
        body, name=name, grid=(m // tm,),
        in_specs=[pl.BlockSpec((tm, kdim), lambda i: (i, 0)), pl.BlockSpec((kdim, n), lambda i: (0, 0)), row, vec] + ([vec] if two else []),
        out_specs=[row, row] + ([row] if two else []),
        out_shape=[jax.ShapeDtypeStruct((m, n), BF16), jax.ShapeDtypeStruct((m, n), F32)]
        + ([jax.ShapeDtypeStruct((m, n), BF16)] if two else []),
        compiler_params=_cp(1),
    )(*((a, b, xres, _row(g)) + ((_row(g_next),) if two else ())))
    return (outs[0], outs[1], outs[2]) if two else (outs[0], outs[1], None)


def _rms_val(v, g):
    return v * lax.rsqrt(jnp.mean(v * v, axis=-1, keepdims=True) + RMS_EPS) * g


def _norm_plain(v, g, name, tr=512):
    t, d = v.shape
    tr = min(tr, t)

    def body(v_ref, g_ref, o_ref):
        o_ref[...] = _rms_val(v_ref[...], g_ref[...]).astype(BF16)

    return pl.pallas_call(
        body, name=name, grid=(t // tr,),
        in_specs=[pl.BlockSpec((tr, d), lambda i: (i, 0)), pl.BlockSpec((1, d), lambda i: (0, 0))],
        out_specs=pl.BlockSpec((tr, d), lambda i: (i, 0)),
        out_shape=jax.ShapeDtypeStruct((t, d), BF16), compiler_params=_cp(1),
    )(v, _row(g))


def _norm_bwd(v, g, dout, dres, out_dtype, name, tr=512):
    t, d = v.shape
    tr = min(tr, t)
    nt = t // tr
    has_res = dres is not None

    def body(*refs):
        v_ref, g_ref, do_ref = refs[:3]
        dv_ref, dg_ref = refs[3 + has_res], refs[4 + has_res]
        i = pl.program_id(0)
        vv = v_ref[...].astype(F32)
        dy = do_ref[...].astype(F32)
        r = lax.rsqrt(jnp.mean(vv * vv, axis=-1, keepdims=True) + RMS_EPS)
        vhat = vv * r
        dvh = dy * g_ref[...]
        dv = r * (dvh - vhat * jnp.mean(dvh * vhat, axis=-1, keepdims=True))
        if has_res:
            dv = dv + refs[3][...]
        dv_ref[...] = dv.astype(dv_ref.dtype)
        part = jnp.sum(dy * vhat, axis=0, keepdims=True)

        @pl.when(i == 0)
        def _():
            dg_ref[...] = jnp.zeros_like(dg_ref)

        dg_ref[0:1, :] += part

    row = pl.BlockSpec((tr, d), lambda i: (i, 0))
    dv, dg = pl.pallas_call(
        body, name=name, grid=(nt,),
        in_specs=[row, pl.BlockSpec((1, d), lambda i: (0, 0)), row] + ([row] if has_res else []),
        out_specs=[row, pl.BlockSpec((8, d), lambda i: (0, 0))],
        out_shape=[jax.ShapeDtypeStruct((t, d), out_dtype), jax.ShapeDtypeStruct((8, d), F32)],
        compiler_params=_cp(1),
    )(*((v, _row(g), dout) + ((dres,) if has_res else ())))
    return dv, dg[0]


def _norm_bwd_pair(v1, g1, dout, dres, v2, g2, name, tr=512):
    t, d = v1.shape
    tr = min(tr, t)

    def body(v1_ref, g1_ref, do_ref, dr_ref, v2_ref, g2_ref, d1_ref, d2_ref, dg1_ref, dg2_ref):
        @pl.when(pl.program_id(0) == 0)
        def _():
            dg1_ref[...] = jnp.zeros_like(dg1_ref)
            dg2_ref[...] = jnp.zeros_like(dg2_ref)

        dv1, dg1 = _rms_bwd_val(v1_ref[...].astype(F32), g1_ref[...], do_ref[...].astype(F32))
        d1 = dv1 + dr_ref[...]
        d1_ref[...] = d1
        dv2, dg2 = _rms_bwd_val(v2_ref[...].astype(F32), g2_ref[...], d1)
        d2_ref[...] = dv2.astype(BF16)
        dg1_ref[0:1, :] += dg1
        dg2_ref[0:1, :] += dg2

    row = pl.BlockSpec((tr, d), lambda i: (i, 0))
    vec = pl.BlockSpec((1, d), lambda i: (0, 0))
    acc = pl.BlockSpec((8, d), lambda i: (0, 0))
    d1, d2, dg1, dg2 = pl.pallas_call(
        body, name=name, grid=(t // tr,), in_specs=[row, vec, row, row, row, vec], out_specs=[row, row, acc, acc],
        out_shape=[jax.ShapeDtypeStruct((t, d), F32), jax.ShapeDtypeStruct((t, d), BF16),
                   jax.ShapeDtypeStruct((8, d), F32), jax.ShapeDtypeStruct((8, d), F32)],
        compiler_params=_cp(1),
    )(v1, _row(g1), dout, dres, v2, _row(g2))
    return d1, d2, dg1[0], dg2[0]


def _loss_head(y, target, name, tr=512):
    t, d = y.shape
    tr = min(tr, t)
    nt = t // tr

    def body(y_ref, t_ref, dy_ref, l_ref):
        i = pl.program_id(0)
        err = y_ref[...] - t_ref[...]
        dy_ref[...] = err * (1.0 / d)

        @pl.when(i == 0)
        def _():
            l_ref[...] = jnp.zeros_like(l_ref)

        l_ref[...] += jnp.sum(err * err) * (0.5 / d)

    row = pl.BlockSpec((tr, d), lambda i: (i, 0))
    dy, l = pl.pallas_call(
        body, name=name, grid=(nt,), in_specs=[row, row],
        out_specs=[row, pl.BlockSpec((8, 128), lambda i: (0, 0))],
        out_shape=[jax.ShapeDtypeStruct((t, d), F32), jax.ShapeDtypeStruct((8, 128), F32)],
        compiler_params=_cp(1),
    )(y, target)
    return l[0, 0], dy


def _sigmoid(v):
    return 1.0 / (1.0 + jnp.exp(-v))


def _halo_specs(tq, width, nt, halo):
    per = tq // halo
    last = nt * per - 1
    main = pl.BlockSpec((tq, width), lambda i: (i, 0))
    prev = pl.BlockSpec((halo, width), lambda i: (jnp.maximum(i * per - 1, 0), 0))
    nxt = pl.BlockSpec((halo, width), lambda i: (jnp.minimum((i + 1) * per, last), 0))
    return prev, main, nxt


def _glu_window(zp, zm, zn, i, nt):
    def glu(z):
        z = z.astype(F32)
        return z[:, :CONV_W] * _sigmoid(z[:, CONV_W:])

    up = jnp.where(i > 0, glu(zp), 0.0)
    un = jnp.where(i < nt - 1, glu(zn), 0.0)
    return jnp.concatenate([up, glu(zm), un], axis=0)


def _shifted(win, shift, rows):
    if shift == 0:
        return win[:rows]
    return pltpu.roll(win, win.shape[0] - shift, 0)[:rows]


def _conv_fwd(zc, w, b, ln_g, ln_b, name, tq=512):
    t = zc.shape[0]
    tq = min(tq, t)
    nt = t // tq

    def body(zp_ref, zm_ref, zn_ref, w_ref, b_ref, g_ref, bb_ref, y_ref, c_ref):
        i = pl.program_id(0)
        win = _glu_window(zp_ref[...], zm_ref[...], zn_ref[...], i, nt)
        wv = w_ref[...]
        y = jnp.zeros((tq, CONV_W), F32) + b_ref[...]
        for k in range(CONV_K):
            y = y + _shifted(win, k + 1, tq) * wv[k:k + 1, :]
        y_ref[...] = y
        mu = jnp.mean(y, axis=-1, keepdims=True)
        yc = y - mu
        ln = yc * lax.rsqrt(jnp.mean(yc * yc, axis=-1, keepdims=True) + LN_EPS) * g_ref[...] + bb_ref[...]
        c_ref[...] = (ln * _sigmoid(ln)).astype(BF16)

    vec = pl.BlockSpec((1, CONV_W), lambda i: (0, 0))
    out = pl.BlockSpec((tq, CONV_W), lambda i: (i, 0))
    return pl.pallas_call(
        body, name=name, grid=(nt,),
        in_specs=[*_halo_specs(tq, 2 * CONV_W, nt, CONV_PAD), pl.BlockSpec((32, CONV_W), lambda i: (0, 0)), vec, vec, vec],
        out_specs=[out, out],
        out_shape=[jax.ShapeDtypeStruct((t, CONV_W), F32), jax.ShapeDtypeStruct((t, CONV_W), BF16)],
        compiler_params=_cp(1),
    )(zc, zc, zc, w, _row(b), _row(ln_g), _row(ln_b))


def _conv_bwd(zc, y, dc, w, ln_g, ln_b, name, tq=256):
    t = zc.shape[0]
    tq = min(tq, t)
    nt = t // tq
    rows = tq + 2 * CONV_PAD

    def body(zp_ref, zm_ref, zn_ref, yp_ref, ym_ref, yn_ref, dp_ref, dm_ref, dn_ref, w_ref, g_ref, bb_ref,
             dz_ref, dw_ref, db_ref, dg_ref, dbb_ref):
        i = pl.program_id(0)
        uwin = _glu_window(zp_ref[...], zm_ref[...], zn_ref[...], i, nt)
        ywin = jnp.concatenate([yp_ref[...], ym_ref[...], yn_ref[...]], axis=0)
        dcw = jnp.concatenate([dp_ref[...], dm_ref[...], dn_ref[...]], axis=0).astype(F32)
        mu = jnp.mean(ywin, axis=-1, keepdims=True)
        yc = ywin - mu
        rstd = lax.rsqrt(jnp.mean(yc * yc, axis=-1, keepdims=True) + LN_EPS)
        yhat = yc * rstd
        ln = yhat * g_ref[...] + bb_ref[...]
        sg = _sigmoid(ln)
        dl = dcw * (sg * (1.0 + ln * (1.0 - sg)))
        ridx = lax.broadcasted_iota(jnp.int32, (rows, 1), 0)
        inside = ((ridx >= CONV_PAD) | (i > 0)) & ((ridx < tq + CONV_PAD) | (i < nt - 1))
        dl = jnp.where(inside, dl, 0.0)
        dyh = dl * g_ref[...]
        dy = rstd * (dyh - jnp.mean(dyh, axis=-1, keepdims=True) - yhat * jnp.mean(dyh * yhat, axis=-1, keepdims=True))
        dy = jnp.where(inside, dy, 0.0)
        main = slice(CONV_PAD, CONV_PAD + tq)
        dlm, yhm, dym = dl[main], yhat[main], dy[main]

        @pl.when(i == 0)
        def _():
            dw_ref[...] = jnp.zeros_like(dw_ref)
            db_ref[...] = jnp.zeros_like(db_ref)
            dg_ref[...] = jnp.zeros_like(dg_ref)
            dbb_ref[...] = jnp.zeros_like(dbb_ref)

        dg_ref[0:1, :] += jnp.sum(dlm * yhm, axis=0, keepdims=True)
        dbb_ref[0:1, :] += jnp.sum(dlm, axis=0, keepdims=True)
        db_ref[0:1, :] += jnp.sum(dym, axis=0, keepdims=True)
        wv = w_ref[...]
        du = jnp.zeros((tq, CONV_W), F32)
        for k in range(CONV_K):
            du = du + _shifted(dy, 2 * CONV_PAD - 1 - k, tq) * wv[k:k + 1, :]
            dw_ref[k:k + 1, :] += jnp.sum(dym * _shifted(uwin, k + 1, tq), axis=0, keepdims=True)
        zm = zm_ref[...].astype(F32)
        a, gt = zm[:, :CONV_W], zm[:, CONV_W:]
        sgt = _sigmoid(gt)
        dz_ref[:, :CONV_W] = (du * sgt).astype(BF16)
        dz_ref[:, CONV_W:] = (du * a * sgt * (1.0 - sgt)).astype(BF16)

    vec = pl.BlockSpec((1, CONV_W), lambda i: (0, 0))
    acc = pl.BlockSpec((8, CONV_W), lambda i: (0, 0))
    dz, dw, db, dg, dbb = pl.pallas_call(
        body, name=name, grid=(nt,),
        in_specs=[*_halo_specs(tq, 2 * CONV_W, nt, CONV_PAD), *_halo_specs(tq, CONV_W, nt, CONV_PAD),
                  *_halo_specs(tq, CONV_W, nt, CONV_PAD), pl.BlockSpec((32, CONV_W), lambda i: (0, 0)), vec, vec],
        out_specs=[pl.BlockSpec((tq, 2 * CONV_W), lambda i: (i, 0)), pl.BlockSpec((32, CONV_W), lambda i: (0, 0)), acc, acc, acc],
        out_shape=[jax.ShapeDtypeStruct((t, 2 * CONV_W), BF16), jax.ShapeDtypeStruct((32, CONV_W), F32),
                   jax.ShapeDtypeStruct((8, CONV_W), F32), jax.ShapeDtypeStruct((8, CONV_W), F32),
                   jax.ShapeDtypeStruct((8, CONV_W), F32)],
        compiler_params=_cp(1),
    )(zc, zc, zc, y, y, y, dc, dc, dc, w, _row(ln_g), _row(ln_b))
    return dz, dw, db[0], dg[0], dbb[0]


def _att_tiles(seq_len):
    tq = min(256, seq_len)
    return tq, min(128, tq)


def _t5_bucket_np(rel):
    nb = NUM_BUCKETS // 2
    max_exact = nb // 2
    ret = np.where(rel > 0, nb, 0)
    n = np.abs(rel)
    nf = np.maximum(n, 1).astype(np.float32)
    large = max_exact + (np.log(nf / np.float32(max_exact)) / np.float32(np.log(MAX_DISTANCE / max_exact))
                         * np.float32(nb - max_exact)).astype(np.int32)
    large = np.minimum(large, nb - 1)
    return ret + np.where(n < max_exact, n, large)


def _bucket_index(sb, dilation):
    off = np.arange(sb + 2 * ATT_RADIUS)[None, :] - ATT_RADIUS - np.arange(sb)[:, None]
    idx = _t5_bucket_np(off * dilation).astype(np.int32)
    return np.where(np.abs(off) <= ATT_RADIUS, idx, -1).astype(np.int32)


def _bias_table(tab, idx, name):
    sb, w = idx.shape

    def body(tab_ref, idx_ref, o_ref):
        ix = idx_ref[...]
        for h in range(HEADS):
            acc = jnp.full((sb, w), NEG_INF, F32)
            for b in range(NUM_BUCKETS):
                acc = jnp.where(ix == b, tab_ref[b, h], acc)
            o_ref[h] = acc

    return pl.pallas_call(
        body, name=name, out_shape=jax.ShapeDtypeStruct((HEADS, sb, w), F32),
        in_specs=[pl.BlockSpec(memory_space=pltpu.SMEM), pl.BlockSpec(memory_space=pltpu.VMEM)],
        out_specs=pl.BlockSpec(memory_space=pltpu.VMEM),
    )(tab, idx)


def _qkv_specs(tq, per, last_blk, cols, tile):
    def main(col):
        return pl.BlockSpec((tq, ATT_GW), lambda *g: (tile(*g), col))

    def prev(col):
        return pl.BlockSpec((ATT_RADIUS, ATT_GW), lambda *g: (jnp.maximum(tile(*g) * per - 1, 0), col))

    def nxt(col):
        return pl.BlockSpec((ATT_RADIUS, ATT_GW), lambda *g: (jnp.minimum((tile(*g) + 1) * per, last_blk), col))

    cq, ck, cv = cols
    return [main(cq), prev(ck), main(ck), nxt(ck), prev(cv), main(cv), nxt(cv)]


def _att_fwd(q, k, v, bias, seq_len, name, cols=(0, 0, 0)):
    t = q.shape[0]
    tq, sb = _att_tiles(seq_len)
    nbl = seq_len // tq
    nt = t // tq
    per = tq // ATT_RADIUS
    w = sb + 2 * ATT_RADIUS
    last_blk = t // ATT_RADIUS - 1

    def body(q_ref, kp_ref, kc_ref, kn_ref, vp_ref, vc_ref, vn_ref, b_ref, o_ref, l_ref):
        n = pl.program_id(0) % nbl
        kw = jnp.concatenate([kp_ref[...], kc_ref[...], kn_ref[...]], axis=0)
        vw = jnp.concatenate([vp_ref[...], vc_ref[...], vn_ref[...]], axis=0)
        kpos = n * tq - ATT_RADIUS + lax.broadcasted_iota(jnp.int32, (1, tq + 2 * ATT_RADIUS), 1)
        valid = (kpos >= 0) & (kpos < seq_len)
        for j in range(tq // sb):
            qj = q_ref[j * sb:(j + 1) * sb, :]
            kj, vj, okj = kw[j * sb:j * sb + w], vw[j * sb:j * sb + w], valid[:, j * sb:j * sb + w]
            outs, lses = [], []
            for h in range(HEADS):
                hs = slice(h * HEAD_DIM, (h + 1) * HEAD_DIM)
                s = lax.dot_general(qj[:, hs], kj[:, hs], NT, preferred_element_type=F32)
                s = jnp.where(okj, s * (HEAD_DIM ** -0.5) + b_ref[h], NEG_INF)
                m = jnp.max(s, axis=-1, keepdims=True)
                e = jnp.exp(s - m)
                den = jnp.sum(e, axis=-1, keepdims=True)
                outs.append(jnp.dot(e.astype(BF16), vj[:, hs], preferred_element_type=F32) / den)
                lses.append(jnp.broadcast_to(m + jnp.log(den), (sb, HEAD_DIM)))
            o_ref[j * sb:(j + 1) * sb, :] = jnp.concatenate(outs, axis=1).astype(BF16)
            l_ref[j * sb:(j + 1) * sb, :] = jnp.concatenate(lses, axis=1)

    main = pl.BlockSpec((tq, ATT_GW), lambda i: (i, 0))
    return pl.pallas_call(
        body, name=name, grid=(nt,),
        in_specs=[*_qkv_specs(tq, per, last_blk, cols, lambda i: i), pl.BlockSpec((HEADS, sb, w), lambda i: (0, 0, 0))],
        out_specs=[main, main],
        out_shape=[jax.ShapeDtypeStruct((t, ATT_GW), BF16), jax.ShapeDtypeStruct((t, ATT_GW), F32)],
        compiler_params=_cp(1),
    )(q, k, k, k, v, v, v, bias)


PERM_ROWS = 2048
LANES = 128


def _spec_rm(d, tr, width, col=0):
    return pl.BlockSpec((d, tr // d, width), lambda i: (0, i, col))


def _rm_view(a, d):
    return a.reshape(d, a.shape[0] // d, a.shape[1])


def _gather_residues(scr, val, d, out_ref, col0):
    tr, w = val.shape
    for c in range(w // LANES):
        scr[...] = val[:, c * LANES:(c + 1) * LANES].astype(F32)
        for r in range(d):
            out_ref[r, :, col0 + c * LANES:col0 + (c + 1) * LANES] = scr[pl.ds(r, tr // d, stride=d), :].astype(out_ref.dtype)


def _scatter_residues(scr, ref, d):
    w = ref.shape[2]
    cols = []
    for c in range(w // LANES):
        for r in range(d):
            scr[pl.ds(r, scr.shape[0] // d, stride=d), :] = ref[r, :, c * LANES:(c + 1) * LANES].astype(F32)
        cols.append(scr[...])
    return jnp.concatenate(cols, axis=1)


def _to_residue_major(srcs, d, name, tr=PERM_ROWS):
    t = srcs[0][0].shape[0]
    tr = min(tr, t)
    n = len(srcs)

    def body(*refs):
        scr = refs[2 * n]
        for k in range(n):
            _gather_residues(scr, refs[k][...], d, refs[n + k], 0)

    outs = pl.pallas_call(
        body, name=name, grid=(t // tr,),
        in_specs=[pl.BlockSpec((tr, ATT_GW), functools.partial(lambda i, col: (i, col), col=col)) for _, col in srcs],
        out_specs=[_spec_rm(d, tr, ATT_GW)] * n,
        out_shape=[jax.ShapeDtypeStruct((d, t // d, ATT_GW), a.dtype) for a, _ in srcs],
        scratch_shapes=[pltpu.VMEM((tr, LANES), F32)], compiler_params=_cp(1),
    )(*[a for a, _ in srcs])
    return [o.reshape(t, ATT_GW) for o in outs]


def _att_combine(outs, lses, name, tr=PERM_ROWS):
    t = outs[0].shape[0]
    tr = min(tr, t)
    dils = [d for _, d in ATT_PATTERNS]

    def body(o0, o1, o2, l0, l1, l2, a_ref, lse_ref, scr):
        os_, ls_ = [], []
        for o_ref, l_ref, d in zip((o0, o1, o2), (l0, l1, l2), dils):
            if d == 1:
                os_.append(o_ref[...].astype(F32))
                ls_.append(l_ref[...])
            else:
                os_.append(_scatter_residues(scr, o_ref, d))
                ls_.append(_scatter_residues(scr, l_ref, d))
        la, lb, lc = ls_
        m = jnp.maximum(jnp.maximum(la, lb), lc)
        ea, eb, ec = jnp.exp(la - m), jnp.exp(lb - m), jnp.exp(lc - m)
        den = ea + eb + ec
        a_ref[...] = ((ea * os_[0] + eb * os_[1] + ec * os_[2]) / den).astype(BF16)
        lse_ref[...] = m + jnp.log(den)

    row = pl.BlockSpec((tr, ATT_GW), lambda i: (i, 0))
    specs = [row if d == 1 else _spec_rm(d, tr, ATT_GW) for d in dils]
    views = lambda arrs: [a if d == 1 else _rm_view(a, d) for a, d in zip(arrs, dils)]
    return pl.pallas_call(
        body, name=name, grid=(t // tr,), in_specs=specs * 2, out_specs=[row, row],
        out_shape=[jax.ShapeDtypeStruct((t, ATT_GW), BF16), jax.ShapeDtypeStruct((t, ATT_GW), F32)],
        scratch_shapes=[pltpu.VMEM((tr, LANES), F32)], compiler_params=_cp(1),
    )(*views(outs), *views(lses))


def _assemble_dza(dqkv, name, tr=PERM_ROWS):
    t = dqkv[0][0].shape[0]
    tr = min(tr, t)
    dils = [d for _, d in ATT_PATTERNS]

    def body(*refs):
        o_ref, scr = refs[9], refs[10]
        for g, d in enumerate(dils):
            for c in range(3):
                ref = refs[g * 3 + c]
                val = ref[...] if d == 1 else _scatter_residues(scr, ref, d).astype(BF16)
                o_ref[:, (c * 3 + g) * ATT_GW:(c * 3 + g + 1) * ATT_GW] = val

    row = pl.BlockSpec((tr, ATT_GW), lambda i: (i, 0))
    specs = [row if d == 1 else _spec_rm(d, tr, ATT_GW) for d in dils for _ in range(3)]
    args = [a if d == 1 else _rm_view(a, d) for trio, d in zip(dqkv, dils) for a in trio]
    return pl.pallas_call(
        body, name=name, grid=(t // tr,), in_specs=specs, out_specs=pl.BlockSpec((tr, 3 * ATT_W), lambda i: (i, 0)),
        out_shape=jax.ShapeDtypeStruct((t, 3 * ATT_W), BF16),
        scratch_shapes=[pltpu.VMEM((tr, LANES), F32)], compiler_params=_cp(1),
    )(*args)


def _att_delta(o, do, name, tr=1024):
    t = o.shape[0]
    tr = min(tr, t)

    def body(o_ref, do_ref, d_ref):
        dd = o_ref[...].astype(F32) * do_ref[...].astype(F32)
        d_ref[...] = jnp.concatenate(
            [jnp.broadcast_to(jnp.sum(dd[:, h * HEAD_DIM:(h + 1) * HEAD_DIM], axis=-1, keepdims=True), (tr, HEAD_DIM))
             for h in range(HEADS)], axis=1)

    row = pl.BlockSpec((tr, ATT_GW), lambda i: (i, 0))
    return pl.pallas_call(
        body, name=name, grid=(t // tr,), in_specs=[row, row], out_specs=row,
        out_shape=jax.ShapeDtypeStruct((t, ATT_GW), F32), compiler_params=_cp(1),
    )(o, do)


def _att_bwd(q, k, v, bias, delta, do, lse, seq_len, name, cols=(0, 0, 0)):
    t = q.shape[0]
    tq, sb = _att_tiles(seq_len)
    nbl = seq_len // tq
    n_res = t // seq_len
    per = tq // ATT_RADIUS
    w = sb + 2 * ATT_RADIUS
    last_blk = t // ATT_RADIUS - 1
    acc_cols = 2 * tq - sb + w
    scale = HEAD_DIM ** -0.5

    def body(q_ref, kp_ref, kc_ref, kn_ref, vp_ref, vc_ref, vn_ref, b_ref, dl_ref, do_ref, l_ref,
             dq_ref, dk_ref, dv_ref, db_ref, ak_ref, av_ref):
        r, n = pl.program_id(0), pl.program_id(1)

        @pl.when((r == 0) & (n == 0))
        def _():
            db_ref[...] = jnp.zeros_like(db_ref)

        @pl.when(n == 0)
        def _():
            ak_ref[...] = jnp.zeros_like(ak_ref)
            av_ref[...] = jnp.zeros_like(av_ref)

        @pl.when(n < nbl)
        def _():
            kw = jnp.concatenate([kp_ref[...], kc_ref[...], kn_ref[...]], axis=0)
            vw = jnp.concatenate([vp_ref[...], vc_ref[...], vn_ref[...]], axis=0)
            kpos = n * tq - ATT_RADIUS + lax.broadcasted_iota(jnp.int32, (1, tq + 2 * ATT_RADIUS), 1)
            valid = (kpos >= 0) & (kpos < seq_len)
            for j in range(tq // sb):
                rows = slice(j * sb, (j + 1) * sb)
                qj, doj = q_ref[rows, :], do_ref[rows, :]
                dlj = dl_ref[rows, :]
                lj = l_ref[rows, :]
                kj, vj, okj = kw[j * sb:j * sb + w], vw[j * sb:j * sb + w], valid[:, j * sb:j * sb + w]
                heads = [slice(h * HEAD_DIM, (h + 1) * HEAD_DIM) for h in range(HEADS)]
                ss = [lax.dot_general(qj[:, hs], kj[:, hs], NT, preferred_element_type=F32) for hs in heads]
                dps = [lax.dot_general(doj[:, hs], vj[:, hs], NT, preferred_element_type=F32) for hs in heads]
                ps, dss, dsbs = [], [], []
                for h, hs in enumerate(heads):
                    s = jnp.where(okj, ss[h] * scale + b_ref[h], NEG_INF)
                    p = jnp.exp(s - lj[:, h * HEAD_DIM:h * HEAD_DIM + 1])
                    ds = p * (dps[h] - dlj[:, h * HEAD_DIM:h * HEAD_DIM + 1])
                    ps.append(p.astype(BF16))
                    dss.append(ds)
                    dsbs.append(ds.astype(BF16))
                dqs = [jnp.dot(dsbs[h], kj[:, hs], preferred_element_type=F32) * scale for h, hs in enumerate(heads)]
                dks = [lax.dot_general(qj[:, hs], dsbs[h], (((0,), (0,)), ((), ())), preferred_element_type=F32) * scale
                       for h, hs in enumerate(heads)]
                dvs = [lax.dot_general(doj[:, hs], ps[h], (((0,), (0,)), ((), ())), preferred_element_type=F32)
                       for h, hs in enumerate(heads)]
                dq_ref[rows, :] = jnp.concatenate(dqs, axis=1).astype(BF16)
                c0 = tq + j * sb
                ak_ref[:, c0:c0 + w] += jnp.concatenate(dks, axis=0)
                av_ref[:, c0:c0 + w] += jnp.concatenate(dvs, axis=0)
                for h in range(HEADS):
                    db_ref[h] += dss[h]

        dk_ref[...] = ak_ref[:, ATT_RADIUS:ATT_RADIUS + tq].T.astype(BF16)
        dv_ref[...] = av_ref[:, ATT_RADIUS:ATT_RADIUS + tq].T.astype(BF16)
        keep = acc_cols - tq
        nk, nv = ak_ref[:, tq:acc_cols], av_ref[:, tq:acc_cols]
        ak_ref[:, 0:keep] = nk
        av_ref[:, 0:keep] = nv
        ak_ref[:, keep:acc_cols] = jnp.zeros((ATT_GW, tq), F32)
        av_ref[:, keep:acc_cols] = jnp.zeros((ATT_GW, tq), F32)

    def tile(r, n):
        return r * nbl + jnp.minimum(n, nbl - 1)

    main = pl.BlockSpec((tq, ATT_GW), lambda r, n: (tile(r, n), 0))
    lag = pl.BlockSpec((tq, ATT_GW), lambda r, n: (r * nbl + jnp.maximum(n - 1, 0), 0))
    bspec = pl.BlockSpec((HEADS, sb, w), lambda r, n: (0, 0, 0))
    return pl.pallas_call(
        body, name=name, grid=(n_res, nbl + 1),
        in_specs=[*_qkv_specs(tq, per, last_blk, cols, tile), bspec, main, main, main],
        out_specs=[main, lag, lag, bspec],
        out_shape=[jax.ShapeDtypeStruct((t, ATT_GW), BF16)] * 3 + [jax.ShapeDtypeStruct((HEADS, sb, w), F32)],
        scratch_shapes=[pltpu.VMEM((ATT_GW, acc_cols), F32), pltpu.VMEM((ATT_GW, acc_cols), F32)],
        compiler_params=_cp(2),
    )(q, k, k, k, v, v, v, bias, delta, do, lse)


def _bias_grad(db, idx, name):
    _, sb, w = db.shape

    def body(db_ref, idx_ref, o_ref):
        ix = idx_ref[...]
        lane = lax.broadcasted_iota(jnp.int32, (1, 128), 1)
        rows = []
        for h in range(HEADS):
            d = db_ref[h]
            acc = jnp.zeros((1, 128), F32)
            for b in range(NUM_BUCKETS):
                acc = acc + jnp.where(lane == b, jnp.sum(jnp.where(ix == b, d, 0.0)), 0.0)
            rows.append(acc)
        o_ref[...] = jnp.concatenate(rows + [jnp.zeros((8 - HEADS, 128), F32)], axis=0)

    return pl.pallas_call(
        body, name=name, out_shape=jax.ShapeDtypeStruct((8, 128), F32),
        in_specs=[pl.BlockSpec(memory_space=pltpu.VMEM), pl.BlockSpec(memory_space=pltpu.VMEM)],
        out_specs=pl.BlockSpec(memory_space=pltpu.VMEM),
    )(db, idx)


def _mem_fwd(zq, kv, name, tq=512):
    t = zq.shape[0]
    tq = min(tq, t)
    scale = MEM_HD ** -0.5

    def body(q_ref, kv_ref, o_ref):
        outs = []
        for h in range(MEM_HEADS):
            hs = slice(h * MEM_HD, (h + 1) * MEM_HD)
            kh = kv_ref[:, h * MEM_HD:(h + 1) * MEM_HD]
            vh = kv_ref[:, MEM_W + h * MEM_HD:MEM_W + (h + 1) * MEM_HD]
            s = lax.dot_general(q_ref[:, hs], kh, (((1,), (1,)), ((), ())), preferred_element_type=F32) * scale
            e = jnp.exp(s - jnp.max(s, axis=-1, keepdims=True))
            den = jnp.sum(e, axis=-1, keepdims=True)
            outs.append(jnp.dot(e.astype(BF16), vh, preferred_element_type=F32) / den)
        o_ref[...] = jnp.concatenate(outs, axis=1).astype(BF16)

    row = pl.BlockSpec((tq, MEM_W), lambda i: (i, 0))
    return pl.pallas_call(
        body, name=name, grid=(t // tq,),
        in_specs=[row, pl.BlockSpec((N_MEM, 2 * MEM_W), lambda i: (0, 0))], out_specs=row,
        out_shape=jax.ShapeDtypeStruct((t, MEM_W), BF16), compiler_params=_cp(1),
    )(zq, kv)


def _mem_bwd(zq, kv, do, name, tq=512):
    t = zq.shape[0]
    tq = min(tq, t)
    nt = t // tq
    scale = MEM_HD ** -0.5

    def body(q_ref, kv_ref, do_ref, dq_ref, out_ref, dkv_ref):
        @pl.when(pl.program_id(0) == 0)
        def _():
            dkv_ref[...] = jnp.zeros_like(dkv_ref)

        ks = [slice(h * MEM_HD, (h + 1) * MEM_HD) for h in range(MEM_HEADS)]
        vs = [slice(MEM_W + h * MEM_HD, MEM_W + (h + 1) * MEM_HD) for h in range(MEM_HEADS)]
        ss = [lax.dot_general(q_ref[:, k], kv_ref[:, k], NT, preferred_element_type=F32) * scale for k in ks]
        dps = [lax.dot_general(do_ref[:, k], kv_ref[:, v], NT, preferred_element_type=F32) for k, v in zip(ks, vs)]
        pbs, dsbs = [], []
        for s, dp in zip(ss, dps):
            e = jnp.exp(s - jnp.max(s, axis=-1, keepdims=True))
            p = e / jnp.sum(e, axis=-1, keepdims=True)
            ds = p * (dp - jnp.sum(p * dp, axis=-1, keepdims=True))
            pbs.append(p.astype(BF16))
            dsbs.append(ds.astype(BF16))
        dq_ref[...] = jnp.concatenate([jnp.dot(dsb, kv_ref[:, k], preferred_element_type=F32) * scale
                                       for dsb, k in zip(dsbs, ks)], axis=1).astype(BF16)
        for dsb, pb, k, v in zip(dsbs, pbs, ks, vs):
            dkv_ref[k, :] += jnp.dot(q_ref[:, k].T, dsb, preferred_element_type=F32) * scale
            dkv_ref[v, :] += jnp.dot(do_ref[:, k].T, pb, preferred_element_type=F32)

        @pl.when(pl.program_id(0) == nt - 1)
        def _():
            out_ref[...] = dkv_ref[...].T

    row = pl.BlockSpec((tq, MEM_W), lambda i: (i, 0))
    full = pl.BlockSpec((N_MEM, 2 * MEM_W), lambda i: (0, 0))
    return pl.pallas_call(
        body, name=name, grid=(nt,), in_specs=[row, full, row], out_specs=[row, full],
        out_shape=[jax.ShapeDtypeStruct((t, MEM_W), BF16), jax.ShapeDtypeStruct((N_MEM, 2 * MEM_W), F32)],
        scratch_shapes=[pltpu.VMEM((2 * MEM_W, N_MEM), F32)], compiler_params=_cp(1),
    )(zq, kv, do)


NT = (((1,), (1,)), ((), ()))


def _branches_fwd(acts, ws, zg, bg, name, tr=512):
    t = zg.shape[0]
    tr = min(tr, t)

    def body(a0, a1, a2, w0, w1, w2, z_ref, b_ref, y0, y1, y2, o_ref):
        merged = None
        for i, (a_ref, w_ref, y_ref) in enumerate(((a0, w0, y0), (a1, w1, y1), (a2, w2, y2))):
            y = lax.dot_general(a_ref[...], w_ref[...], NT, preferred_element_type=F32)
            y_ref[...] = y.astype(BF16)
            g = _sigmoid(z_ref[:, i * D:(i + 1) * D].astype(F32) + b_ref[:, i * D:(i + 1) * D])
            merged = g * y if merged is None else merged + g * y
        o_ref[...] = merged.astype(BF16)

    row = pl.BlockSpec((tr, D), lambda i: (i, 0))
    outs = pl.pallas_call(
        body, name=name, grid=(t // tr,),
        in_specs=[pl.BlockSpec((tr, a.shape[1]), lambda i: (i, 0)) for a in acts]
        + [pl.BlockSpec(w.shape, lambda i: (0, 0)) for w in ws]
        + [pl.BlockSpec((tr, 3 * D), lambda i: (i, 0)), pl.BlockSpec((1, 3 * D), lambda i: (0, 0))],
        out_specs=[row] * 4, out_shape=[jax.ShapeDtypeStruct((t, D), BF16)] * 4, compiler_params=_cp(1),
    )(*acts, *ws, zg, _row(bg))
    return tuple(outs[:3]), outs[3]


def _branches_bwd(dymix, w_out, ws, zg, bg, ys, name, tr=512):
    t = zg.shape[0]
    tr = min(tr, t)

    def body(dy_ref, wo_ref, w0, w1, w2, z_ref, b_ref, y0, y1, y2, d0, d1, d2, dz_ref, db_ref, da0, da1, da2):
        @pl.when(pl.program_id(0) == 0)
        def _():
            db_ref[...] = jnp.zeros_like(db_ref)

        dm = lax.dot_general(dy_ref[...], wo_ref[...], NT, preferred_element_type=F32)
        for i, (y_ref, w_ref, d_ref, da_ref) in enumerate(((y0, w0, d0, da0), (y1, w1, d1, da1), (y2, w2, d2, da2))):
            gi = _sigmoid(z_ref[:, i * D:(i + 1) * D].astype(F32) + b_ref[:, i * D:(i + 1) * D])
            dy = (dm * gi).astype(BF16)
            d_ref[...] = dy
            dz = dm * y_ref[...].astype(F32) * gi * (1.0 - gi)
            dz_ref[:, i * D:(i + 1) * D] = dz.astype(BF16)
            db_ref[0:1, i * D:(i + 1) * D] += jnp.sum(dz, axis=0, keepdims=True)
            da_ref[...] = jnp.dot(dy, w_ref[...], preferred_element_type=F32).astype(BF16)

    row = pl.BlockSpec((tr, D), lambda i: (i, 0))
    wide = pl.BlockSpec((tr, 3 * D), lambda i: (i, 0))
    outs = pl.pallas_call(
        body, name=name, grid=(t // tr,),
        in_specs=[row, pl.BlockSpec(w_out.shape, lambda i: (0, 0))] + [pl.BlockSpec(w.shape, lambda i: (0, 0)) for w in ws]
        + [wide, pl.BlockSpec((1, 3 * D), lambda i: (0, 0)), row, row, row],
        out_specs=[row, row, row, wide, pl.BlockSpec((8, 3 * D), lambda i: (0, 0))]
        + [pl.BlockSpec((tr, w.shape[1]), lambda i: (i, 0)) for w in ws],
        out_shape=[jax.ShapeDtypeStruct((t, D), BF16)] * 3
        + [jax.ShapeDtypeStruct((t, 3 * D), BF16), jax.ShapeDtypeStruct((8, 3 * D), F32)]
        + [jax.ShapeDtypeStruct((t, w.shape[1]), BF16) for w in ws],
        compiler_params=_cp(1),
    )(dymix, w_out, *ws, zg, _row(bg), *ys)
    return tuple(outs[:3]), outs[3], outs[4][0], tuple(outs[5:])


def _adamw(w, g, m, v, name):
    shape = w.shape
    w, g, m, v = (a.reshape(-1, shape[-1]) for a in (w, g, m, v))
    r, d = w.shape
    tr = next((c for c in (512, 352, 256, 128, 64, 32, 16, 8) if r % c == 0), r)

    def body(w_ref, g_ref, m_ref, v_ref, d_ref, nm_ref, nv_ref):
        gg = g_ref[...]
        m2 = ADAM_B1 * m_ref[...] + (1.0 - ADAM_B1) * gg
        v2 = ADAM_B2 * v_ref[...] + (1.0 - ADAM_B2) * (gg * gg)
        m_hat = m2 / (1.0 - ADAM_B1 ** ADAM_STEP)
        v_hat = v2 / (1.0 - ADAM_B2 ** ADAM_STEP)
        d_ref[...] = -ADAM_LR * (m_hat / (jnp.sqrt(v_hat) + ADAM_EPS) + ADAM_WD * w_ref[...])
        nm_ref[...] = m2
        nv_ref[...] = v2

    row = pl.BlockSpec((tr, d), lambda i: (i, 0))
    outs = pl.pallas_call(
        body, name=name, grid=(r // tr,), in_specs=[row] * 4, out_specs=[row] * 3,
        out_shape=[jax.ShapeDtypeStruct((r, d), F32)] * 3, compiler_params=_cp(1),
    )(w, g, m, v)
    return [o.reshape(shape) for o in outs]


def _slab_tile(rows):
    return next((c for c in range(min(rows, 512), 15, -16) if rows % c == 0), rows)


def _sum_slots(parts, out_dtype, name):
    n, r, d = parts.shape
    tr = _slab_tile(r)

    def body(p_ref, o_ref):
        acc = p_ref[0].astype(F32)
        for s in range(1, n):
            acc = acc + p_ref[s].astype(F32)
        o_ref[...] = acc.astype(o_ref.dtype)

    return pl.pallas_call(
        body, name=name, grid=(r // tr,), in_specs=[pl.BlockSpec((n, tr, d), lambda i: (0, i, 0))],
        out_specs=pl.BlockSpec((tr, d), lambda i: (i, 0)),
        out_shape=jax.ShapeDtypeStruct((r, d), out_dtype), compiler_params=_cp(1),
    )(parts)


def _place():
    return lax.axis_index("x"), lax.axis_index("y"), lax.axis_index("c")


def _all_gather(shard, name):
    r, d = shard.shape

    def body(x_ref, out_ref, send_sems, recv_sems, local_sem):
        x, y, c = _place()
        me, sibling = (x, y, c), (x, y, 1 - c)
        chips = [(1 - x, y), (x, 1 - y), (1 - x, 1 - y)]

        def slot(px, py, pc):
            return out_ref.at[4 * px + 2 * py + pc]

        def copy(k, block, to, src=None):
            return pltpu.make_async_remote_copy(
                src_ref=slot(*block) if src is None else src, dst_ref=slot(*block),
                send_sem=send_sems.at[k], recv_sem=recv_sems.at[k], device_id=to, device_id_type=MESH)

        mine = pltpu.make_async_copy(x_ref, slot(*me), local_sem)
        mine.start()
        first = [copy(0, me, sibling, src=x_ref)]
        first += [copy(1 + j, me, (*chip, c), src=x_ref) for j, chip in enumerate(chips)]
        for cp in first:
            cp.start()
        passed = [copy(4 + j, (*chip, c), sibling) for j, chip in enumerate(chips)]
        for j, chip in enumerate(chips):
            copy(1 + j, (*chip, c), me).wait_recv()
            passed[j].start()
        copy(0, sibling, me).wait_recv()
        for j, chip in enumerate(chips):
            copy(4 + j, (*chip, 1 - c), me).wait_recv()
        for cp in first + passed:
            cp.wait_send()
        mine.wait()

    return pl.pallas_call(
        body, name=name, out_shape=jax.ShapeDtypeStruct((N_DEV, r, d), shard.dtype),
        in_specs=[pl.BlockSpec(memory_space=pl.ANY)], out_specs=pl.BlockSpec(memory_space=pl.ANY),
        scratch_shapes=[pltpu.SemaphoreType.DMA((7,)), pltpu.SemaphoreType.DMA((7,)), pltpu.SemaphoreType.DMA],
    )(shard)


HBM_SPEC = pl.BlockSpec(memory_space=pltpu.HBM)
SEM_SPEC = pl.BlockSpec(memory_space=pltpu.SEMAPHORE)
SPLIT_PARAMS = pltpu.CompilerParams(has_side_effects=pltpu.SideEffectType.DATAFLOW_SIDE_EFFECTING)


def _peers():
    x, y, c = _place()
    flip = lambda v, bit: 1 - v if bit else v
    return 4 * x + 2 * y + c, [((flip(x, k >> 2 & 1), flip(y, k >> 1 & 1), flip(c, k & 1))) for k in range(1, N_DEV)]


def _exchange_start(src, all_gather, name):
    r, d = src.shape[-2:]

    def body(src_ref, land_ref, send_sems, recv_sems, src_thru, land_thru, token):
        me, peers = _peers()
        for k, (px, py, pc) in enumerate(peers):
            part = src_ref if all_gather else src_ref.at[4 * px + 2 * py + pc]
            pltpu.make_async_remote_copy(src_ref=part, dst_ref=land_ref.at[me], send_sem=send_sems.at[k],
                                         recv_sem=recv_sems.at[k], device_id=(px, py, pc), device_id_type=MESH).start()
        token[...] = jnp.zeros_like(token)

    land = lax.empty((N_DEV, r, d), src.dtype)
    return pl.pallas_call(
        body, name=name,
        out_shape=(pltpu.SemaphoreType.DMA((N_DEV - 1,)), pltpu.SemaphoreType.DMA((N_DEV - 1,)), pltpu.HBM(src.shape, src.dtype),
                   pltpu.HBM(land.shape, land.dtype), jax.ShapeDtypeStruct((8, LANES), F32)),
        in_specs=(HBM_SPEC, HBM_SPEC), out_specs=(SEM_SPEC, SEM_SPEC, HBM_SPEC, HBM_SPEC, pl.BlockSpec(memory_space=pltpu.VMEM)),
        input_output_aliases={0: 2, 1: 3}, compiler_params=SPLIT_PARAMS,
    )(pltpu.with_memory_space_constraint(src, pltpu.HBM), pltpu.with_memory_space_constraint(land, pltpu.HBM))


def _exchange_wait(started, after, all_gather, name):
    send_sems, recv_sems, src_thru, land_thru, _ = started

    def body(src_ref, land_ref, send_sems, recv_sems, after_ref, src_out, land_out):
        me, peers = _peers()
        for k, (px, py, pc) in enumerate(peers):
            part = src_ref if all_gather else src_ref.at[4 * px + 2 * py + pc]
            cp = pltpu.make_async_remote_copy(src_ref=part, dst_ref=land_ref.at[4 * px + 2 * py + pc], send_sem=send_sems.at[k],
                                              recv_sem=recv_sems.at[k], device_id=(px, py, pc), device_id_type=MESH)
            cp.wait_send()
            cp.wait_recv()

    return pl.pallas_call(
        body, name=name,
        out_shape=(pltpu.HBM(src_thru.shape, src_thru.dtype), pltpu.HBM(land_thru.shape, land_thru.dtype)),
        in_specs=(HBM_SPEC, HBM_SPEC, SEM_SPEC, SEM_SPEC, pl.BlockSpec(memory_space=pl.ANY)), out_specs=(HBM_SPEC, HBM_SPEC),
        input_output_aliases={0: 0, 1: 1}, compiler_params=SPLIT_PARAMS,
    )(src_thru, land_thru, send_sems, recv_sems, after)


def _own_slot(land, mine):
    x, y, c = _place()
    return lax.dynamic_update_slice(land, mine[None], (4 * x + 2 * y + c, 0, 0))


BIG = (("w_in", (D, 864), 1), ("w_conv_out", (CONV_W, 128), 1), ("w_att_out", (ATT_GW, 128), 1), ("w_mem_kv", (128, D), 0),
       ("w_mem_out", (MEM_W, 128), 1), ("w_out", (128, D), 0), ("w_ffn_in", (D, 704), 1), ("w_ffn_out", (352, D), 0),
       ("conv_dw", (CONV_K, 64), 1))
SMALL = (("rel_bias", (NUM_BUCKETS, 12)), ("norm_mix_pre", (DEPTH, D)), ("b_gate", (DEPTH, 3 * D)),
         ("conv_dw_bias", (DEPTH, CONV_W)), ("conv_ln_g", (DEPTH, CONV_W)), ("conv_ln_b", (DEPTH, CONV_W)),
         ("norm_mem", (DEPTH, D)), ("norm_mix_post", (DEPTH, D)), ("norm_ffn_pre", (DEPTH, D)), ("norm_ffn_post", (DEPTH, D)))
TWIN_WEIGHTS = ("rel_bias", "norm_mix_pre", "w_in", "b_gate", "conv_dw", "conv_dw_bias", "conv_ln_g", "conv_ln_b", "w_conv_out",
                "w_att_out", "norm_mem", "w_mem_kv", "w_mem_out", "w_out", "norm_mix_post", "norm_ffn_pre", "w_ffn_in",
                "w_ffn_out", "norm_ffn_post")


BIG_INFO = {n: (s, a) for n, s, a in BIG}
GROUPS = {"in": ("w_in",), "mix": ("conv_dw", "w_conv_out", "w_att_out", "w_mem_kv", "w_mem_out", "w_out"),
          "ffn_in": ("w_ffn_in",), "ffn_out": ("w_ffn_out",)}
TILE_BYTES_PER_LANE = 32


def _rows_of(shape, dtype):
    tile = TILE_BYTES_PER_LANE // jnp.dtype(dtype).itemsize
    return -(-int(np.prod(shape)) // (D * tile)) * tile


def _row_form(shape, axis):
    return tuple(shape) if axis == 0 else (shape[1], shape[0])


def _as_slab_rows(p, lead=()):
    shape = p.shape[len(lead):]
    rows = _rows_of(shape, p.dtype)
    if shape == (rows, D):
        return p
    n = int(np.prod(shape))
    p = jnp.pad(p.reshape(*lead, n), [(0, 0)] * len(lead) + [(0, rows * D - n)])
    return p.reshape(*lead, rows, D)


def _from_slab_rows(rows, shape, lead=()):
    if rows.shape[len(lead):] == tuple(shape):
        return rows
    n = int(np.prod(shape))
    return rows.reshape(*lead, -1)[..., :n].reshape(*lead, *shape)


def _pack(pieces, dtype):
    return jnp.concatenate([_as_slab_rows(p.astype(dtype)) for p in pieces], axis=0)


def _unpack(slab, shapes, packed_as):
    out, r0 = [], 0
    for s in shapes:
        rows = _rows_of(s, packed_as)
        out.append(_from_slab_rows(slab[r0:r0 + rows], s))
        r0 += rows
    return out


def _pack_group(blocks, layer, group):
    return _pack([blocks[n][layer] if BIG_INFO[n][1] == 0 else blocks[n][layer].T for n in GROUPS[group]], BF16)


def _full_weights(gathered, group):
    out, r0 = {}, 0
    for name in GROUPS[group]:
        r, k = _row_form(*BIG_INFO[name])
        rows = _rows_of((r, k), BF16)
        out[name] = _from_slab_rows(gathered[:, r0:r0 + rows], (r, k), lead=(N_DEV,)).reshape(N_DEV * r, k)
        r0 += rows
    return out


def _scatter_layout(grads, group):
    parts = []
    for name in GROUPS[group]:
        r, k = _row_form(*BIG_INFO[name])
        parts.append(_as_slab_rows(grads[name].astype(BF16).reshape(N_DEV, r, k), lead=(N_DEV,)))
    return jnp.concatenate(parts, axis=1)


def _unpack_blocks(slab, group):
    out = _unpack(slab, [_row_form(*BIG_INFO[n]) for n in GROUPS[group]], BF16)
    return {n: b if BIG_INFO[n][1] == 0 else b.T for n, b in zip(GROUPS[group], out)}


def _split_w_in(w_in):
    edges = np.cumsum((0,) + SEG)
    return [w_in[edges[i]:edges[i + 1]] for i in range(4)]


def _conv_taps(conv_dw):
    return jnp.pad(conv_dw.astype(F32).T, ((0, 32 - CONV_K), (0, 0)))


def _layer_fwd(l, x, h, nm, get_w, sm, bias_tabs):
    t = x.shape[0]
    win = get_w(l, "in", h)
    w_seg = _split_w_in(win["w_in"])
    zc = _mm(h, w_seg[0], "fwd_in_conv", tb=True)
    za = _mm(h, w_seg[1], "fwd_in_att", tb=True, tn=1152)
    zq = _mm(h, w_seg[2], "fwd_in_memq", tb=True)
    zg = _mm(h, w_seg[3], "fwd_in_gate", tb=True)
    wts = get_w(l, "mix", zg)
    y, cact = _conv_fwd(zc, _conv_taps(wts["conv_dw"]), sm["conv_dw_bias"][l], sm["conv_ln_g"][l], sm["conv_ln_b"][l],
                        "conv_fwd")
    qkv, outs, lses = [], [], []
    for g, (_, dil) in enumerate(ATT_PATTERNS):
        cols = tuple(i * 3 + g for i in range(3))
        if dil == 1:
            trio, at = (za, za, za), cols
        else:
            trio, at = _to_residue_major([(za, c) for c in cols], dil, f"qkv_residue_d{dil}"), (0, 0, 0)
        o, ls = _att_fwd(*trio, bias_tabs[g], t // dil, f"att_fwd_d{dil}", cols=at)
        qkv.append((trio, at))
        outs.append(o)
        lses.append(ls)
    att, lse = _att_combine(outs, lses, "att_combine")
    kv = _mm(nm, wts["w_mem_kv"], "fwd_mem_kv")
    om = _mem_fwd(zq, kv, "mem_fwd")
    w_br = (wts["w_conv_out"], wts["w_att_out"], wts["w_mem_out"])
    ys, merged = _branches_fwd((cact, att, om), w_br, zg, sm["b_gate"][l], "branches_fwd")
    ymix, x1, h2 = _mm_norm_res(merged, wts["w_out"], x, sm["norm_mix_post"][l], sm["norm_ffn_pre"][l], "fwd_out_norm")
    gff, uff, act = _mm_swiglu_fwd(h2, get_w(l, "ffn_in", h2)["w_ffn_in"], "fwd_ffn_in_swiglu")
    g_next = sm["norm_mix_pre"][l + 1] if l + 1 < DEPTH else None
    f, x2, h_next = _mm_norm_res(act, get_w(l, "ffn_out", act)["w_ffn_out"], x1, sm["norm_ffn_post"][l], g_next,
                                 "fwd_ffn_out_norm" if g_next is not None else "fwd_ffn_out_last")
    saved = dict(x=x, h=h, zc=zc, zq=zq, zg=zg, y=y, cact=cact, qkv=qkv, att=att, lse=lse, kv=kv, om=om, ys=ys,
                 merged=merged, ymix=ymix, x1=x1, h2=h2, gff=gff, uff=uff, act=act, f=f)
    return x2, h_next, saved


def _after(value, *tokens):
    for t in tokens:
        if t is not None:
            value = value + t[0, 0].astype(value.dtype)
    return value


def _layer_bwd(l, dx2, s, nm, mem, get_w, put_g, sm, bias_tabs, bucket_idx):
    t = dx2.shape[0]
    win, wts = get_w(l, "in", None), get_w(l, "mix", None)
    w_ffn_in, w_ffn_out = get_w(l, "ffn_in", None)["w_ffn_in"], get_w(l, "ffn_out", None)["w_ffn_out"]
    gb, gs = {}, {}
    df, gs["norm_ffn_post"] = _norm_bwd(s["f"], sm["norm_ffn_post"][l], dx2, None, BF16, "bwd_norm_ffn_post")
    dg, du = _mm_swiglu_bwd(df, w_ffn_out, s["gff"], s["uff"], "bwd_ffn_out_d_swiglu")
    tok_out = put_g(l, "ffn_out", {"w_ffn_out": _mm(s["act"], df, "bwd_ffn_out_w", ta=True, tm=1408)})
    tok_in = put_g(l, "ffn_in", {"w_ffn_in": jnp.concatenate([_mm(dg, s["h2"], "bwd_ffn_in_w_gate", ta=True, tm=1408),
                                                              _mm(du, s["h2"], "bwd_ffn_in_w_up", ta=True, tm=1408)], axis=0)})
    dh2 = _mm_sum([(dg, _after(w_ffn_in[:FFN_H], tok_out, tok_in), FFN_H // 2), (du, w_ffn_in[FFN_H:], FFN_H // 2)],
                  "bwd_ffn_in_d")
    dx1, dymix, gs["norm_ffn_pre"], gs["norm_mix_post"] = _norm_bwd_pair(
        s["x1"], sm["norm_ffn_pre"][l], dh2, dx2, s["ymix"], sm["norm_mix_post"][l], "bwd_norm_ffn_pre_mix_post")
    gb["w_out"] = _mm(s["merged"], dymix, "bwd_out_w", ta=True)
    w_br = (wts["w_conv_out"], wts["w_att_out"], wts["w_mem_out"])
    dys, dzg, gs["b_gate"], (dcact, datt, dom) = _branches_bwd(dymix, wts["w_out"], w_br, s["zg"], sm["b_gate"][l], s["ys"],
                                                               "branches_bwd")
    gb["w_conv_out"] = _mm(dys[0], s["cact"], "bwd_conv_out_w", ta=True)
    dzc, dconv, gs["conv_dw_bias"], gs["conv_ln_g"], gs["conv_ln_b"] = _conv_bwd(
        s["zc"], s["y"], dcact, _conv_taps(wts["conv_dw"]), sm["conv_ln_g"][l], sm["conv_ln_b"][l], "conv_bwd")
    gb["conv_dw"] = dconv[:CONV_K].T
    gb["w_att_out"] = _mm(dys[1], s["att"], "bwd_att_out_w", ta=True)
    delta = _att_delta(s["att"], datt, "att_delta")
    dqkv, rel = [], []
    for g, (_, dil) in enumerate(ATT_PATTERNS):
        trio, at = s["qkv"][g]
        if dil == 1:
            dl_r, do_r, lse_r = delta, datt, s["lse"]
        else:
            dl_r, do_r, lse_r = _to_residue_major([(delta, 0), (datt, 0), (s["lse"], 0)], dil, f"att_bwd_residue_d{dil}")
        dq, dk, dv, dbias = _att_bwd(*trio, bias_tabs[g], dl_r, do_r, lse_r, t // dil, f"att_bwd_d{dil}", cols=at)
        dqkv.append((dq, dk, dv))
        rel.append(_bias_grad(dbias, bucket_idx[g], f"bias_grad_d{dil}")[:HEADS, :NUM_BUCKETS])
    dza = _assemble_dza(dqkv, "att_bwd_assemble")
    gs["rel_bias"] = jnp.concatenate(rel, axis=0).T
    gb["w_mem_out"] = _mm(dys[2], s["om"], "bwd_mem_out_w", ta=True)
    dzq, dkv = _mem_bwd(s["zq"], s["kv"], dom, "mem_bwd")
    dkv = dkv.astype(BF16)
    gb["w_mem_kv"] = _mm(nm, dkv, "bwd_mem_kv_w", ta=True)
    tok = put_g(l, "mix", gb)
    dnm = _mm(dkv, wts["w_mem_kv"], "bwd_mem_kv_d", tb=True, out_dtype=F32)
    _, gs["norm_mem"] = _norm_bwd(mem, sm["norm_mem"][l], dnm, None, BF16, "bwd_norm_mem")
    segs = ((dzc, "conv", 1024), (dza, "att", 1152), (_after(dzq, tok), "memq", 512), (dzg, "gate", 1024))
    tok = put_g(l, "in", {"w_in": jnp.concatenate([_mm(dz, s["h"], f"bwd_in_{nm_}_w", ta=True, tm=min(blk, 1152), tk=2048)
                                                   for dz, nm_, blk in segs], axis=0)})
    w_seg = _split_w_in(win["w_in"])
    w_seg[0] = _after(w_seg[0], tok)
    dx, gs["norm_mix_pre"] = _mm_sum([(dz, w_seg[i], blk) for i, (dz, _, blk) in enumerate(segs)], "bwd_in_d_norm",
                                     norm=(s["x"], sm["norm_mix_pre"][l], dx1))
    return dx, gs


def _local_step(x, mem, target, sm, get_w, put_g, tokens=()):
    t = x.shape[0]
    bias_tabs, bucket_idx = [], []
    for g, (_, dil) in enumerate(ATT_PATTERNS):
        idx = _bucket_index(_att_tiles(t // dil)[1], dil)
        bucket_idx.append(jnp.asarray(idx))
        bias_tabs.append(_bias_table(sm["rel_bias"][:, g * HEADS:(g + 1) * HEADS], bucket_idx[g], f"bias_table_d{dil}"))
    h = _norm_plain(x, _after(sm["norm_mix_pre"][0], *tokens), "norm_first")
    saved, nms = [], []
    for l in range(DEPTH):
        nm = _norm_plain(mem, sm["norm_mem"][l], "norm_mem")
        x, h, s = _layer_fwd(l, x, h, nm, get_w, sm, bias_tabs)
        saved.append(s)
        nms.append(nm)
    loss, dx = _loss_head(x, target, "loss_head")
    gsmall = {}
    for l in reversed(range(DEPTH)):
        dx, gs = _layer_bwd(l, dx, saved[l], nms[l], mem, get_w, put_g, sm, bias_tabs, bucket_idx)
        for n, v in gs.items():
            gsmall.setdefault(n, {})[l] = v
    small = {}
    for n, _ in SMALL:
        small[n] = gsmall[n][0] + gsmall[n][1] if n == "rel_bias" else jnp.stack([gsmall[n][0], gsmall[n][1]])
    return loss, dx, small


def kernel(x, mem, rel_bias, norm_mix_pre, w_in, b_gate, conv_dw, conv_dw_bias, conv_ln_g, conv_ln_b, w_conv_out, w_att_out, norm_mem, w_mem_kv, w_mem_out, w_out, norm_mix_post, norm_ffn_pre, w_ffn_in, w_ffn_out, norm_ffn_post, loss_target, m_rel_bias, m_norm_mix_pre, m_w_in, m_b_gate, m_conv_dw, m_conv_dw_bias, m_conv_ln_g, m_conv_ln_b, m_w_conv_out, m_w_att_out, m_norm_mem, m_w_mem_kv, m_w_mem_out, m_w_out, m_norm_mix_post, m_norm_ffn_pre, m_w_ffn_in, m_w_ffn_out, m_norm_ffn_post, v_rel_bias, v_norm_mix_pre, v_w_in, v_b_gate, v_conv_dw, v_conv_dw_bias, v_conv_ln_g, v_conv_ln_b, v_w_conv_out, v_w_att_out, v_norm_mem, v_w_mem_kv, v_w_mem_out, v_w_out, v_norm_mix_post, v_norm_ffn_pre, v_w_ffn_in, v_w_ffn_out, v_norm_ffn_post):
    args = dict(locals())
    w = {n: args[n] for n in TWIN_WEIGHTS}
    m = {n: args["m_" + n] for n in TWIN_WEIGHTS}
    v = {n: args["v_" + n] for n in TWIN_WEIGHTS}
    sm = {n: w[n] for n, _ in SMALL}

    first = (0, "in")
    keys = [(l, grp) for l in range(DEPTH) for grp in GROUPS]
    shard = {(l, grp): _pack_group(w, l, grp) for l, grp in keys}
    weights = {first: _full_weights(_all_gather(shard[first], "gather_l0_in"), "in")}
    settled = weights[first]["w_in"][:1, :1] * 0
    ag = {}
    for k in keys:
        if k != first:
            ag[k] = _exchange_start(_after(shard[k], settled), True, f"ag_start_l{k[0]}_{k[1]}")
            settled = ag[k][4]

    def get_w(l, grp, after):
        if (l, grp) not in weights:
            mine, land = _exchange_wait(ag[l, grp], after, True, f"ag_wait_l{l}_{grp}")
            weights[l, grp] = _full_weights(_own_slot(land, mine), grp)
        return weights[l, grp]

    rs = {}

    def put_g(l, grp, grads):
        rs[l, grp] = _exchange_start(_scatter_layout(grads, grp), False, f"rs_start_l{l}_{grp}")
        return rs[l, grp][4]

    loss, dx, gsmall = _local_step(x[0], mem[0], loss_target[0], sm, get_w, put_g, [st[4] for st in ag.values()])
    loss = lax.psum(loss, ("x", "y", "c"))

    xi, yi, ci = _place()
    me = 4 * xi + 2 * yi + ci
    g_slab = {}
    for k, st in rs.items():
        contrib, land = _exchange_wait(st, dx, False, f"rs_wait_l{k[0]}_{k[1]}")
        own = lax.dynamic_index_in_dim(contrib, me, axis=0, keepdims=False)
        g_slab[k] = _sum_slots(_own_slot(land, own), F32, f"rs_sum_l{k[0]}_{k[1]}")
    g_layers = [{n: b for grp in GROUPS for n, b in _unpack_blocks(g_slab[l, grp], grp).items()} for l in range(DEPTH)]
    small_shapes = [s for _, s in SMALL]
    g_small = _unpack(_sum_slots(_all_gather(_pack([gsmall[n] for n, _ in SMALL], F32), "gather_small"), F32, "sum_small"),
                      small_shapes, F32)
    grads = {n: jnp.stack([g_layers[0][n], g_layers[1][n]]) for n, _, _ in BIG}
    grads.update({n: g_small[i] for i, (n, _) in enumerate(SMALL)})

    delta, new_m, new_v = {}, {}, {}
    for n in TWIN_WEIGHTS:
        delta[n], new_m[n], new_v[n] = _adamw(w[n], grads[n], m[n], v[n], f"adamw_{n}")

    return (loss, dx[None], *[grads[n] for n in TWIN_WEIGHTS], *[delta[n] for n in TWIN_WEIGHTS],
            *[new_m[n] for n in TWIN_WEIGHTS], *[new_v[n] for n in TWIN_WEIGHTS])
```

```python
import functools

import numpy as np
import jax
import jax.numpy as jnp
from jax import lax
from jax.experimental import pallas as pl
from jax.experimental.pallas import tpu as pltpu

F32 = jnp.float32
BF16 = jnp.bfloat16

N_DEV = 8
D = 1024
DEPTH = 2
CONV_W = 512
CONV_K = 31
CONV_PAD = 16
ATT_PATTERNS = ((128, 1), (512, 4), (2048, 16))
ATT_RADIUS = 64
HEADS = 4
HEAD_DIM = 64
ATT_GW = HEADS * HEAD_DIM
ATT_W = 3 * ATT_GW
MEM_HEADS = 4
MEM_HD = 128
MEM_W = 512
N_MEM = 256
FFN_H = 2816
NUM_BUCKETS = 32
MAX_DISTANCE = 1024
RMS_EPS = 1e-6
LN_EPS = 1e-5
NEG_INF = -1e30
SEG = (2 * CONV_W, 3 * ATT_W, MEM_W, 3 * D)
ADAM_LR, ADAM_B1, ADAM_B2, ADAM_EPS, ADAM_WD, ADAM_STEP = 0.001, 0.9, 0.999, 1e-08, 0.01, 10

VMEM_LIMIT_V7X = 56 * 1024 * 1024
MESH = pl.DeviceIdType.MESH


def _cp(n_axes):
    return pltpu.CompilerParams(dimension_semantics=("arbitrary",) * n_axes, vmem_limit_bytes=VMEM_LIMIT_V7X)


def _row(v):
    return v.reshape(1, -1)


def _mm(a, b, name, *, ta=False, tb=False, out_dtype=BF16, tm=2048, tn=1024, tk=2048):
    kdim, m = a.shape if ta else a.shape[::-1]
    n, kb = b.shape if tb else b.shape[::-1]
    assert kb == kdim and a.dtype == BF16 and b.dtype == BF16
    tm, tn, tk = min(tm, m), min(tn, n), min(tk, kdim)
    assert m % tm == 0 and n % tn == 0 and kdim % tk == 0, (name, m, n, kdim, tm, tn, tk)
    nk = kdim // tk
    a_spec = pl.BlockSpec((tk, tm), lambda i, j, k: (k, i)) if ta else pl.BlockSpec((tm, tk), lambda i, j, k: (i, k))
    b_spec = pl.BlockSpec((tn, tk), lambda i, j, k: (j, k)) if tb else pl.BlockSpec((tk, tn), lambda i, j, k: (k, j))
    o_spec = pl.BlockSpec((tm, tn), lambda i, j, k: (i, j))
    dims = (((0 if ta else 1,), (1 if tb else 0,)), ((), ()))
    use_scratch = nk > 1 and out_dtype != F32

    def body(*refs):
        a_ref, b_ref, o_ref = refs[:3]
        p = lax.dot_general(a_ref[...], b_ref[...], dims, preferred_element_type=F32)
        if nk == 1:
            o_ref[...] = p.astype(o_ref.dtype)
            return
        k = pl.program_id(2)
        t_ref = refs[3] if use_scratch else o_ref

        @pl.when(k == 0)
        def _():
            t_ref[...] = p

        @pl.when(k > 0)
        def _():
            t_ref[...] += p

        if use_scratch:
            @pl.when(k == nk - 1)
            def _():
                o_ref[...] = t_ref[...].astype(o_ref.dtype)

    return pl.pallas_call(
        body, name=name, grid=(m // tm, n // tn, nk), in_specs=[a_spec, b_spec], out_specs=o_spec,
        out_shape=jax.ShapeDtypeStruct((m, n), out_dtype),
        scratch_shapes=[pltpu.VMEM((tm, tn), F32)] if use_scratch else [],
        compiler_params=_cp(3),
    )(a, b)


EPILOGUE_COLS = 512
EPILOGUE_ROWS = 512


def _chunks(size, step):
    return [(c, min(c + step, size)) for c in range(0, size, step)]


def _rms_bwd_val(v, g, dy):
    r = lax.rsqrt(jnp.mean(v * v, axis=-1, keepdims=True) + RMS_EPS)
    vhat = v * r
    dvh = dy * g
    dv = r * (dvh - vhat * jnp.mean(dvh * vhat, axis=-1, keepdims=True))
    return dv, jnp.sum(dy * vhat, axis=0, keepdims=True)


def _mm_sum(pairs, name, *, norm=None, tm=1024):
    m, n = pairs[0][0].shape[0], pairs[0][1].shape[1]
    tm = min(tm, m)
    starts, specs, total = [], [], 0
    for a, b, tk in pairs:
        assert a.shape == (m, b.shape[0]) and b.shape[1] == n and a.shape[1] % tk == 0 and a.dtype == b.dtype == BF16
        nk = a.shape[1] // tk
        starts.append((total, nk))
        step = functools.partial(lambda k, s0, nk_: jnp.clip(k - s0, 0, nk_ - 1), s0=total, nk_=nk)
        specs.append(pl.BlockSpec((tm, tk), functools.partial(lambda i, k, st: (i, st(k)), st=step)))
        specs.append(pl.BlockSpec((tk, n), functools.partial(lambda i, k, st: (st(k), 0), st=step)))
        total += nk
    np_ = 2 * len(pairs)
    row = pl.BlockSpec((tm, n), lambda i, k: (i, 0))

    def accumulate(refs, acc, k, upto=total):
        for p, (s0, nk) in enumerate(starts):
            @pl.when((k >= s0) & (k < min(s0 + nk, upto)))
            def _(p=p, s0=s0):
                part = jnp.dot(refs[2 * p][...], refs[2 * p + 1][...], preferred_element_type=F32)
                if s0 == 0:
                    @pl.when(k == 0)
                    def _():
                        acc[...] = part

                    @pl.when(k > 0)
                    def _():
                        acc[...] += part
                else:
                    acc[...] += part

    operands = [op for a, b, _ in pairs for op in (a, b)]
    if norm is None:
        def plain(*refs):
            k = pl.program_id(1)
            accumulate(refs, refs[np_ + 1], k)

            @pl.when(k == total - 1)
            def _():
                refs[np_][...] = refs[np_ + 1][...].astype(BF16)

        return pl.pallas_call(
            plain, name=name, grid=(m // tm, total), in_specs=specs, out_specs=row,
            out_shape=jax.ShapeDtypeStruct((m, n), BF16), scratch_shapes=[pltpu.VMEM((tm, n), F32)],
            compiler_params=_cp(2),
        )(*operands)

    v, g, dres = norm

    def body(*refs):
        v_ref, g_ref, dres_ref, dv_ref, dg_ref, acc = refs[np_:np_ + 6]
        i, k = pl.program_id(0), pl.program_id(1)
        accumulate(refs, acc, k, upto=total - 1)

        @pl.when((i == 0) & (k == 0))
        def _():
            dg_ref[...] = jnp.zeros_like(dg_ref)

        @pl.when(k == total - 1)
        def _():
            a_ref, b_ref = refs[np_ - 2], refs[np_ - 1]
            for r0, r1 in _chunks(tm, EPILOGUE_ROWS):
                dy = jnp.dot(a_ref[r0:r1, :], b_ref[...], preferred_element_type=F32)
                if total > 1:
                    dy = dy + acc[r0:r1, :]
                dv, dg = _rms_bwd_val(v_ref[r0:r1, :], g_ref[...], dy)
                dv_ref[r0:r1, :] = dv + dres_ref[r0:r1, :]
                dg_ref[0:1, :] += dg

    once = pl.BlockSpec((tm, n), lambda i, k: (i, 0), pipeline_mode=pl.Buffered(1))
    dv, dg = pl.pallas_call(
        body, name=name, grid=(m // tm, total),
        in_specs=specs + [once, pl.BlockSpec((1, n), lambda i, k: (0, 0)), once],
        out_specs=[row, pl.BlockSpec((8, n), lambda i, k: (0, 0))],
        out_shape=[jax.ShapeDtypeStruct((m, n), F32), jax.ShapeDtypeStruct((8, n), F32)],
        scratch_shapes=[pltpu.VMEM((tm, n), F32)], compiler_params=_cp(2),
    )(*operands, v, _row(g), dres)
    return dv, dg[0]


def _mm_swiglu_fwd(h, w, name, *, tm=1024, tn=1408):
    m, kdim = h.shape
    hid = w.shape[0] // 2
    tm = min(tm, m)
    nj = hid // tn
    dims = (((1,), (1,)), ((), ()))

    def body(h_ref, wg_ref, wu_ref, g_ref, u_ref, a_ref):
        hv = h_ref[...]
        for c0, c1 in _chunks(tn, EPILOGUE_COLS):
            g = lax.dot_general(hv, wg_ref[c0:c1, :], dims, preferred_element_type=F32)
            u = lax.dot_general(hv, wu_ref[c0:c1, :], dims, preferred_element_type=F32)
            g_ref[:, c0:c1] = g.astype(BF16)
            u_ref[:, c0:c1] = u.astype(BF16)
            a_ref[:, c0:c1] = (g * _sigmoid(g) * u).astype(BF16)

    out = pl.BlockSpec((tm, tn), lambda i, j: (i, j))
    return pl.pallas_call(
        body, name=name, grid=(m // tm, nj),
        in_specs=[pl.BlockSpec((tm, kdim), lambda i, j: (i, 0)), pl.BlockSpec((tn, kdim), lambda i, j: (j, 0)),
                  pl.BlockSpec((tn, kdim), lambda i, j: (j + nj, 0))],
        out_specs=[out, out, out], out_shape=[jax.ShapeDtypeStruct((m, hid), BF16)] * 3, compiler_params=_cp(2),
    )(h, w, w)


def _mm_swiglu_bwd(df, w, g, u, name, *, tm=1024, tn=1408):
    m, kdim = df.shape
    hid = w.shape[0]
    tm = min(tm, m)

    def body(df_ref, w_ref, g_ref, u_ref, dg_ref, du_ref):
        dfv = df_ref[...]
        for c0, c1 in _chunks(tn, EPILOGUE_COLS):
            da = lax.dot_general(dfv, w_ref[c0:c1, :], (((1,), (1,)), ((), ())), preferred_element_type=F32)
            gg, uu, dab = g_ref[:, c0:c1], u_ref[:, c0:c1], da.astype(BF16)
            sg = _sigmoid(gg)
            dg_ref[:, c0:c1] = dab * uu * (sg * (1.0 + gg * (1.0 - sg)))
            du_ref[:, c0:c1] = dab * gg * sg

    blk = pl.BlockSpec((tm, tn), lambda i, j: (i, j))
    return pl.pallas_call(
        body, name=name, grid=(m // tm, hid // tn),
        in_specs=[pl.BlockSpec((tm, kdim), lambda i, j: (i, 0)), pl.BlockSpec((tn, kdim), lambda i, j: (j, 0)), blk, blk],
        out_specs=[blk, blk], out_shape=[jax.ShapeDtypeStruct((m, hid), BF16)] * 2, compiler_params=_cp(2),
    )(df, w, g, u)


def _mm_norm_res(a, b, xres, g, g_next, name, *, tm=1024):
    m, kdim = a.shape
    n = b.shape[1]
    tm = min(tm, m)
    two = g_next is not None

    def body(*refs):
        a_ref, b_ref, x_ref, g_ref = refs[:4]
        y_ref, xn_ref = refs[4 + two], refs[5 + two]
        for r0, r1 in _chunks(tm, EPILOGUE_ROWS):
            y = jnp.dot(a_ref[r0:r1, :], b_ref[...], preferred_element_type=F32)
            y_ref[r0:r1, :] = y.astype(BF16)
            xn = x_ref[r0:r1, :] + _rms_val(y, g_ref[...])
            xn_ref[r0:r1, :] = xn
            if two:
                refs[7][r0:r1, :] = _rms_val(xn, refs[4][...]).astype(BF16)

    row = pl.BlockSpec((tm, n), lambda i: (i, 0))
    vec = pl.BlockSpec((1, n), lambda i: (0, 0))
    outs = pl.pallas_call(
        body, name=name, grid=(m // tm,),
        in_specs=[pl.BlockSpec((tm, kdim), lambda i: (i, 0)), pl.BlockSpec((kdim, n), lambda i: (0, 0)), row, vec] + ([vec] if two else []),
        out_specs=[row, row] + ([row] if two else []),
        out_shape=[jax.ShapeDtypeStruct((m, n), BF16), jax.ShapeDtypeStruct((m, n), F32)]
        + ([jax.ShapeDtypeStruct((m, n), BF16)] if two else []),
        compiler_params=_cp(1),
    )(*((a, b, xres, _row(g)) + ((_row(g_next),) if two else ())))
    return (outs[0], outs[1], outs[2]) if two else (outs[0], outs[1], None)


def _rms_val(v, g):
    return v * lax.rsqrt(jnp.mean(v * v, axis=-1, keepdims=True) + RMS_EPS) * g


def _norm_plain(v, g, name, tr=512):
    t, d = v.shape
    tr = min(tr, t)

    def body(v_ref, g_ref, o_ref):
        o_ref[...] = _rms_val(v_ref[...], g_ref[...]).astype(BF16)

    return pl.pallas_call(
        body, name=name, grid=(t // tr,),
        in_specs=[pl.BlockSpec((tr, d), lambda i: (i, 0)), pl.BlockSpec((1, d), lambda i: (0, 0))],
        out_specs=pl.BlockSpec((tr, d), lambda i: (i, 0)),
        out_shape=jax.ShapeDtypeStruct((t, d), BF16), compiler_params=_cp(1),
    )(v, _row(g))


def _norm_bwd(v, g, dout, dres, out_dtype, name, tr=1024):
    t, d = v.shape
    tr = min(tr, t)
    nt = t // tr
    has_res = dres is not None

    def body(*refs):
        v_ref, g_ref, do_ref = refs[:3]
        dv_ref, dg_ref = refs[3 + has_res], refs[4 + has_res]
        i = pl.program_id(0)
        vv = v_ref[...].astype(F32)
        dy = do_ref[...].astype(F32)
        r = lax.rsqrt(jnp.mean(vv * vv, axis=-1, keepdims=True) + RMS_EPS)
        vhat = vv * r
        dvh = dy * g_ref[...]
        dv = r * (dvh - vhat * jnp.mean(dvh * vhat, axis=-1, keepdims=True))
        if has_res:
            dv = dv + refs[3][...]
        dv_ref[...] = dv.astype(dv_ref.dtype)
        part = jnp.sum(dy * vhat, axis=0, keepdims=True)

        @pl.when(i == 0)
        def _():
            dg_ref[...] = jnp.zeros_like(dg_ref)

        dg_ref[0:1, :] += part

    row = pl.BlockSpec((tr, d), lambda i: (i, 0))
    dv, dg = pl.pallas_call(
        body, name=name, grid=(nt,),
        in_specs=[row, pl.BlockSpec((1, d), lambda i: (0, 0)), row] + ([row] if has_res else []),
        out_specs=[row, pl.BlockSpec((8, d), lambda i: (0, 0))],
        out_shape=[jax.ShapeDtypeStruct((t, d), out_dtype), jax.ShapeDtypeStruct((8, d), F32)],
        compiler_params=_cp(1),
    )(*((v, _row(g), dout) + ((dres,) if has_res else ())))
    return dv, dg[0]


def _norm_bwd_pair(v1, g1, dout, dres, v2, g2, name, tr=1024):
    t, d = v1.shape
    tr = min(tr, t)

    def body(v1_ref, g1_ref, do_ref, dr_ref, v2_ref, g2_ref, d1_ref, d2_ref, dg1_ref, dg2_ref):
        @pl.when(pl.program_id(0) == 0)
        def _():
            dg1_ref[...] = jnp.zeros_like(dg1_ref)
            dg2_ref[...] = jnp.zeros_like(dg2_ref)

        dv1, dg1 = _rms_bwd_val(v1_ref[...].astype(F32), g1_ref[...], do_ref[...].astype(F32))
        d1 = dv1 + dr_ref[...]
        d1_ref[...] = d1
        dv2, dg2 = _rms_bwd_val(v2_ref[...].astype(F32), g2_ref[...], d1)
        d2_ref[...] = dv2.astype(BF16)
        dg1_ref[0:1, :] += dg1
        dg2_ref[0:1, :] += dg2

    row = pl.BlockSpec((tr, d), lambda i: (i, 0))
    vec = pl.BlockSpec((1, d), lambda i: (0, 0))
    acc = pl.BlockSpec((8, d), lambda i: (0, 0))
    d1, d2, dg1, dg2 = pl.pallas_call(
        body, name=name, grid=(t // tr,), in_specs=[row, vec, row, row, row, vec], out_specs=[row, row, acc, acc],
        out_shape=[jax.ShapeDtypeStruct((t, d), F32), jax.ShapeDtypeStruct((t, d), BF16),
                   jax.ShapeDtypeStruct((8, d), F32), jax.ShapeDtypeStruct((8, d), F32)],
        compiler_params=_cp(1),
    )(v1, _row(g1), dout, dres, v2, _row(g2))
    return d1, d2, dg1[0], dg2[0]


def _loss_head(y, target, name, tr=1024):
    t, d = y.shape
    tr = min(tr, t)
    nt = t // tr

    def body(y_ref, t_ref, dy_ref, l_ref):
        i = pl.program_id(0)
        err = y_ref[...] - t_ref[...]
        dy_ref[...] = err * (1.0 / d)

        @pl.when(i == 0)
        def _():
            l_ref[...] = jnp.zeros_like(l_ref)

        l_ref[...] += jnp.sum(err * err) * (0.5 / d)

    row = pl.BlockSpec((tr, d), lambda i: (i, 0))
    dy, l = pl.pallas_call(
        body, name=name, grid=(nt,), in_specs=[row, row],
        out_specs=[row, pl.BlockSpec((8, 128), lambda i: (0, 0))],
        out_shape=[jax.ShapeDtypeStruct((t, d), F32), jax.ShapeDtypeStruct((8, 128), F32)],
        compiler_params=_cp(1),
    )(y, target)
    return l[0, 0], dy


def _sigmoid(v):
    return 1.0 / (1.0 + jnp.exp(-v))


def _halo_specs(tq, width, nt, halo):
    per = tq // halo
    last = nt * per - 1
    main = pl.BlockSpec((tq, width), lambda i: (i, 0))
    prev = pl.BlockSpec((halo, width), lambda i: (jnp.maximum(i * per - 1, 0), 0))
    nxt = pl.BlockSpec((halo, width), lambda i: (jnp.minimum((i + 1) * per, last), 0))
    return prev, main, nxt


def _glu_window(zp, zm, zn, i, nt):
    def glu(z):
        z = z.astype(F32)
        return z[:, :CONV_W] * _sigmoid(z[:, CONV_W:])

    up = jnp.where(i > 0, glu(zp), 0.0)
    un = jnp.where(i < nt - 1, glu(zn), 0.0)
    return jnp.concatenate([up, glu(zm), un], axis=0)


def _shifted(win, shift, rows):
    if shift == 0:
        return win[:rows]
    return pltpu.roll(win, win.shape[0] - shift, 0)[:rows]


def _conv_fwd(zc, w, b, ln_g, ln_b, name, tq=512):
    t = zc.shape[0]
    tq = min(tq, t)
    nt = t // tq

    def body(zp_ref, zm_ref, zn_ref, w_ref, b_ref, g_ref, bb_ref, y_ref, c_ref):
        i = pl.program_id(0)
        win = _glu_window(zp_ref[...], zm_ref[...], zn_ref[...], i, nt)
        wv = w_ref[...]
        y = jnp.zeros((tq, CONV_W), F32) + b_ref[...]
        for k in range(CONV_K):
            y = y + _shifted(win, k + 1, tq) * wv[k:k + 1, :]
        y_ref[...] = y
        mu = jnp.mean(y, axis=-1, keepdims=True)
        yc = y - mu
        ln = yc * lax.rsqrt(jnp.mean(yc * yc, axis=-1, keepdims=True) + LN_EPS) * g_ref[...] + bb_ref[...]
        c_ref[...] = (ln * _sigmoid(ln)).astype(BF16)

    vec = pl.BlockSpec((1, CONV_W), lambda i: (0, 0))
    out = pl.BlockSpec((tq, CONV_W), lambda i: (i, 0))
    return pl.pallas_call(
        body, name=name, grid=(nt,),
        in_specs=[*_halo_specs(tq, 2 * CONV_W, nt, CONV_PAD), pl.BlockSpec((32, CONV_W), lambda i: (0, 0)), vec, vec, vec],
        out_specs=[out, out],
        out_shape=[jax.ShapeDtypeStruct((t, CONV_W), F32), jax.ShapeDtypeStruct((t, CONV_W), BF16)],
        compiler_params=_cp(1),
    )(zc, zc, zc, w, _row(b), _row(ln_g), _row(ln_b))


def _conv_bwd(zc, y, dc, w, ln_g, ln_b, name, tq=256):
    t = zc.shape[0]
    tq = min(tq, t)
    nt = t // tq
    rows = tq + 2 * CONV_PAD

    def body(zp_ref, zm_ref, zn_ref, yp_ref, ym_ref, yn_ref, dp_ref, dm_ref, dn_ref, w_ref, g_ref, bb_ref,
             dz_ref, dw_ref, db_ref, dg_ref, dbb_ref):
        i = pl.program_id(0)
        uwin = _glu_window(zp_ref[...], zm_ref[...], zn_ref[...], i, nt)
        ywin = jnp.concatenate([yp_ref[...], ym_ref[...], yn_ref[...]], axis=0)
        dcw = jnp.concatenate([dp_ref[...], dm_ref[...], dn_ref[...]], axis=0).astype(F32)
        mu = jnp.mean(ywin, axis=-1, keepdims=True)
        yc = ywin - mu
        rstd = lax.rsqrt(jnp.mean(yc * yc, axis=-1, keepdims=True) + LN_EPS)
        yhat = yc * rstd
        ln = yhat * g_ref[...] + bb_ref[...]
        sg = _sigmoid(ln)
        dl = dcw * (sg * (1.0 + ln * (1.0 - sg)))
        ridx = lax.broadcasted_iota(jnp.int32, (rows, 1), 0)
        inside = ((ridx >= CONV_PAD) | (i > 0)) & ((ridx < tq + CONV_PAD) | (i < nt - 1))
        dl = jnp.where(inside, dl, 0.0)
        dyh = dl * g_ref[...]
        dy = rstd * (dyh - jnp.mean(dyh, axis=-1, keepdims=True) - yhat * jnp.mean(dyh * yhat, axis=-1, keepdims=True))
        dy = jnp.where(inside, dy, 0.0)
        main = slice(CONV_PAD, CONV_PAD + tq)
        dlm, yhm, dym = dl[main], yhat[main], dy[main]

        @pl.when(i == 0)
        def _():
            dw_ref[...] = jnp.zeros_like(dw_ref)
            db_ref[...] = jnp.zeros_like(db_ref)
            dg_ref[...] = jnp.zeros_like(dg_ref)
            dbb_ref[...] = jnp.zeros_like(dbb_ref)

        dg_ref[0:1, :] += jnp.sum(dlm * yhm, axis=0, keepdims=True)
        dbb_ref[0:1, :] += jnp.sum(dlm, axis=0, keepdims=True)
        db_ref[0:1, :] += jnp.sum(dym, axis=0, keepdims=True)
        wv = w_ref[...]
        du = jnp.zeros((tq, CONV_W), F32)
        for k in range(CONV_K):
            du = du + _shifted(dy, 2 * CONV_PAD - 1 - k, tq) * wv[k:k + 1, :]
            dw_ref[k:k + 1, :] += jnp.sum(dym * _shifted(uwin, k + 1, tq), axis=0, keepdims=True)
        zm = zm_ref[...].astype(F32)
        a, gt = zm[:, :CONV_W], zm[:, CONV_W:]
        sgt = _sigmoid(gt)
        dz_ref[:, :CONV_W] = (du * sgt).astype(BF16)
        dz_ref[:, CONV_W:] = (du * a * sgt * (1.0 - sgt)).astype(BF16)

    vec = pl.BlockSpec((1, CONV_W), lambda i: (0, 0))
    acc = pl.BlockSpec((8, CONV_W), lambda i: (0, 0))
    dz, dw, db, dg, dbb = pl.pallas_call(
        body, name=name, grid=(nt,),
        in_specs=[*_halo_specs(tq, 2 * CONV_W, nt, CONV_PAD), *_halo_specs(tq, CONV_W, nt, CONV_PAD),
                  *_halo_specs(tq, CONV_W, nt, CONV_PAD), pl.BlockSpec((32, CONV_W), lambda i: (0, 0)), vec, vec],
        out_specs=[pl.BlockSpec((tq, 2 * CONV_W), lambda i: (i, 0)), pl.BlockSpec((32, CONV_W), lambda i: (0, 0)), acc, acc, acc],
        out_shape=[jax.ShapeDtypeStruct((t, 2 * CONV_W), BF16), jax.ShapeDtypeStruct((32, CONV_W), F32),
                   jax.ShapeDtypeStruct((8, CONV_W), F32), jax.ShapeDtypeStruct((8, CONV_W), F32),
                   jax.ShapeDtypeStruct((8, CONV_W), F32)],
        compiler_params=_cp(1),
    )(zc, zc, zc, y, y, y, dc, dc, dc, w, _row(ln_g), _row(ln_b))
    return dz, dw, db[0], dg[0], dbb[0]


def _att_tiles(seq_len):
    tq = min(256, seq_len)
    return tq, min(128, tq)


def _t5_bucket_np(rel):
    nb = NUM_BUCKETS // 2
    max_exact = nb // 2
    ret = np.where(rel > 0, nb, 0)
    n = np.abs(rel)
    nf = np.maximum(n, 1).astype(np.float32)
    large = max_exact + (np.log(nf / np.float32(max_exact)) / np.float32(np.log(MAX_DISTANCE / max_exact))
                         * np.float32(nb - max_exact)).astype(np.int32)
    large = np.minimum(large, nb - 1)
    return ret + np.where(n < max_exact, n, large)


def _bucket_index(sb, dilation):
    off = np.arange(sb + 2 * ATT_RADIUS)[None, :] - ATT_RADIUS - np.arange(sb)[:, None]
    idx = _t5_bucket_np(off * dilation).astype(np.int32)
    return np.where(np.abs(off) <= ATT_RADIUS, idx, -1).astype(np.int32)


def _bias_table(tab, idx, name):
    sb, w = idx.shape

    def body(tab_ref, idx_ref, o_ref):
        ix = idx_ref[...]
        for h in range(HEADS):
            acc = jnp.full((sb, w), NEG_INF, F32)
            for b in range(NUM_BUCKETS):
                acc = jnp.where(ix == b, tab_ref[b, h], acc)
            o_ref[h] = acc

    return pl.pallas_call(
        body, name=name, out_shape=jax.ShapeDtypeStruct((HEADS, sb, w), F32),
        in_specs=[pl.BlockSpec(memory_space=pltpu.SMEM), pl.BlockSpec(memory_space=pltpu.VMEM)],
        out_specs=pl.BlockSpec(memory_space=pltpu.VMEM),
    )(tab, idx)


def _qkv_specs(tq, per, last_blk, cols, tile):
    def main(col):
        return pl.BlockSpec((tq, ATT_GW), lambda *g: (tile(*g), col))

    def prev(col):
        return pl.BlockSpec((ATT_RADIUS, ATT_GW), lambda *g: (jnp.maximum(tile(*g) * per - 1, 0), col))

    def nxt(col):
        return pl.BlockSpec((ATT_RADIUS, ATT_GW), lambda *g: (jnp.minimum((tile(*g) + 1) * per, last_blk), col))

    cq, ck, cv = cols
    return [main(cq), prev(ck), main(ck), nxt(ck), prev(cv), main(cv), nxt(cv)]


def _att_fwd(q, k, v, bias, seq_len, name, cols=(0, 0, 0)):
    t = q.shape[0]
    tq, sb = _att_tiles(seq_len)
    nbl = seq_len // tq
    nt = t // tq
    per = tq // ATT_RADIUS
    w = sb + 2 * ATT_RADIUS
    last_blk = t // ATT_RADIUS - 1

    def body(q_ref, kp_ref, kc_ref, kn_ref, vp_ref, vc_ref, vn_ref, b_ref, o_ref, l_ref):
        n = pl.program_id(0) % nbl
        kw = jnp.concatenate([kp_ref[...], kc_ref[...], kn_ref[...]], axis=0)
        vw = jnp.concatenate([vp_ref[...], vc_ref[...], vn_ref[...]], axis=0)
        kpos = n * tq - ATT_RADIUS + lax.broadcasted_iota(jnp.int32, (1, tq + 2 * ATT_RADIUS), 1)
        valid = (kpos >= 0) & (kpos < seq_len)
        for j in range(tq // sb):
            qj = q_ref[j * sb:(j + 1) * sb, :]
            kj, vj, okj = kw[j * sb:j * sb + w], vw[j * sb:j * sb + w], valid[:, j * sb:j * sb + w]
            outs, lses = [], []
            for h in range(HEADS):
                hs = slice(h * HEAD_DIM, (h + 1) * HEAD_DIM)
                s = lax.dot_general(qj[:, hs], kj[:, hs], NT, preferred_element_type=F32)
                s = jnp.where(okj, s * (HEAD_DIM ** -0.5) + b_ref[h], NEG_INF)
                m = jnp.max(s, axis=-1, keepdims=True)
                e = jnp.exp(s - m)
                den = jnp.sum(e, axis=-1, keepdims=True)
                outs.append(jnp.dot(e.astype(BF16), vj[:, hs], preferred_element_type=F32) / den)
                lses.append(jnp.broadcast_to(m + jnp.log(den), (sb, HEAD_DIM)))
            o_ref[j * sb:(j + 1) * sb, :] = jnp.concatenate(outs, axis=1).astype(BF16)
            l_ref[j * sb:(j + 1) * sb, :] = jnp.concatenate(lses, axis=1)

    main = pl.BlockSpec((tq, ATT_GW), lambda i: (i, 0))
    return pl.pallas_call(
        body, name=name, grid=(nt,),
        in_specs=[*_qkv_specs(tq, per, last_blk, cols, lambda i: i), pl.BlockSpec((HEADS, sb, w), lambda i: (0, 0, 0))],
        out_specs=[main, main],
        out_shape=[jax.ShapeDtypeStruct((t, ATT_GW), BF16), jax.ShapeDtypeStruct((t, ATT_GW), F32)],
        compiler_params=_cp(1),
    )(q, k, k, k, v, v, v, bias)


PERM_ROWS = 2048
LANES = 128


def _spec_rm(d, tr, width, col=0):
    return pl.BlockSpec((d, tr // d, width), lambda i: (0, i, col))


def _rm_view(a, d):
    return a.reshape(d, a.shape[0] // d, a.shape[1])


def _gather_residues(scr, val, d, out_ref, col0):
    tr, w = val.shape
    for c in range(w // LANES):
        scr[...] = val[:, c * LANES:(c + 1) * LANES].astype(F32)
        for r in range(d):
            out_ref[r, :, col0 + c * LANES:col0 + (c + 1) * LANES] = scr[pl.ds(r, tr // d, stride=d), :].astype(out_ref.dtype)


def _scatter_residues(scr, ref, d):
    w = ref.shape[2]
    cols = []
    for c in range(w // LANES):
        for r in range(d):
            scr[pl.ds(r, scr.shape[0] // d, stride=d), :] = ref[r, :, c * LANES:(c + 1) * LANES].astype(F32)
        cols.append(scr[...])
    return jnp.concatenate(cols, axis=1)


def _to_residue_major(srcs, d, name, tr=PERM_ROWS):
    t = srcs[0][0].shape[0]
    tr = min(tr, t)
    n = len(srcs)

    def body(*refs):
        scr = refs[2 * n]
        for k in range(n):
            _gather_residues(scr, refs[k][...], d, refs[n + k], 0)

    outs = pl.pallas_call(
        body, name=name, grid=(t // tr,),
        in_specs=[pl.BlockSpec((tr, ATT_GW), functools.partial(lambda i, col: (i, col), col=col)) for _, col in srcs],
        out_specs=[_spec_rm(d, tr, ATT_GW)] * n,
        out_shape=[jax.ShapeDtypeStruct((d, t // d, ATT_GW), a.dtype) for a, _ in srcs],
        scratch_shapes=[pltpu.VMEM((tr, LANES), F32)], compiler_params=_cp(1),
    )(*[a for a, _ in srcs])
    return [o.reshape(t, ATT_GW) for o in outs]


def _att_combine(outs, lses, name, tr=PERM_ROWS):
    t = outs[0].shape[0]
    tr = min(tr, t)
    dils = [d for _, d in ATT_PATTERNS]

    def body(o0, o1, o2, l0, l1, l2, a_ref, lse_ref, scr):
        os_, ls_ = [], []
        for o_ref, l_ref, d in zip((o0, o1, o2), (l0, l1, l2), dils):
            if d == 1:
                os_.append(o_ref[...].astype(F32))
                ls_.append(l_ref[...])
            else:
                os_.append(_scatter_residues(scr, o_ref, d))
                ls_.append(_scatter_residues(scr, l_ref, d))
        la, lb, lc = ls_
        m = jnp.maximum(jnp.maximum(la, lb), lc)
        ea, eb, ec = jnp.exp(la - m), jnp.exp(lb - m), jnp.exp(lc - m)
        den = ea + eb + ec
        a_ref[...] = ((ea * os_[0] + eb * os_[1] + ec * os_[2]) / den).astype(BF16)
        lse_ref[...] = m + jnp.log(den)

    row = pl.BlockSpec((tr, ATT_GW), lambda i: (i, 0))
    specs = [row if d == 1 else _spec_rm(d, tr, ATT_GW) for d in dils]
    views = lambda arrs: [a if d == 1 else _rm_view(a, d) for a, d in zip(arrs, dils)]
    return pl.pallas_call(
        body, name=name, grid=(t // tr,), in_specs=specs * 2, out_specs=[row, row],
        out_shape=[jax.ShapeDtypeStruct((t, ATT_GW), BF16), jax.ShapeDtypeStruct((t, ATT_GW), F32)],
        scratch_shapes=[pltpu.VMEM((tr, LANES), F32)], compiler_params=_cp(1),
    )(*views(outs), *views(lses))


def _assemble_dza(dqkv, name, tr=PERM_ROWS):
    t = dqkv[0][0].shape[0]
    tr = min(tr, t)
    dils = [d for _, d in ATT_PATTERNS]

    def body(*refs):
        o_ref, scr = refs[9], refs[10]
        for g, d in enumerate(dils):
            for c in range(3):
                ref = refs[g * 3 + c]
                val = ref[...] if d == 1 else _scatter_residues(scr, ref, d).astype(BF16)
                o_ref[:, (c * 3 + g) * ATT_GW:(c * 3 + g + 1) * ATT_GW] = val

    row = pl.BlockSpec((tr, ATT_GW), lambda i: (i, 0))
    specs = [row if d == 1 else _spec_rm(d, tr, ATT_GW) for d in dils for _ in range(3)]
    args = [a if d == 1 else _rm_view(a, d) for trio, d in zip(dqkv, dils) for a in trio]
    return pl.pallas_call(
        body, name=name, grid=(t // tr,), in_specs=specs, out_specs=pl.BlockSpec((tr, 3 * ATT_W), lambda i: (i, 0)),
        out_shape=jax.ShapeDtypeStruct((t, 3 * ATT_W), BF16),
        scratch_shapes=[pltpu.VMEM((tr, LANES), F32)], compiler_params=_cp(1),
    )(*args)


def _att_bwd(q, k, v, bias, delta, do, lse, seq_len, name, cols=(0, 0, 0)):
    t = q.shape[0]
    tq, sb = _att_tiles(seq_len)
    nbl = seq_len // tq
    n_res = t // seq_len
    per = tq // ATT_RADIUS
    w = sb + 2 * ATT_RADIUS
    last_blk = t // ATT_RADIUS - 1
    acc_cols = 2 * tq - sb + w
    scale = HEAD_DIM ** -0.5

    def body(q_ref, kp_ref, kc_ref, kn_ref, vp_ref, vc_ref, vn_ref, b_ref, dl_ref, do_ref, l_ref,
             dq_ref, dk_ref, dv_ref, db_ref, ak_ref, av_ref):
        r, n = pl.program_id(0), pl.program_id(1)

        @pl.when((r == 0) & (n == 0))
        def _():
            db_ref[...] = jnp.zeros_like(db_ref)

        @pl.when(n == 0)
        def _():
            ak_ref[...] = jnp.zeros_like(ak_ref)
            av_ref[...] = jnp.zeros_like(av_ref)

        @pl.when(n < nbl)
        def _():
            kw = jnp.concatenate([kp_ref[...], kc_ref[...], kn_ref[...]], axis=0)
            vw = jnp.concatenate([vp_ref[...], vc_ref[...], vn_ref[...]], axis=0)
            kpos = n * tq - ATT_RADIUS + lax.broadcasted_iota(jnp.int32, (1, tq + 2 * ATT_RADIUS), 1)
            valid = (kpos >= 0) & (kpos < seq_len)
            for j in range(tq // sb):
                rows = slice(j * sb, (j + 1) * sb)
                qj, doj = q_ref[rows, :], do_ref[rows, :]
                dlj = dl_ref[rows, :]
                lj = l_ref[rows, :]
                kj, vj, okj = kw[j * sb:j * sb + w], vw[j * sb:j * sb + w], valid[:, j * sb:j * sb + w]
                heads = [slice(h * HEAD_DIM, (h + 1) * HEAD_DIM) for h in range(HEADS)]
                ss = [lax.dot_general(qj[:, hs], kj[:, hs], NT, preferred_element_type=F32) for hs in heads]
                dps = [lax.dot_general(doj[:, hs], vj[:, hs], NT, preferred_element_type=F32) for hs in heads]
                ps, dss, dsbs = [], [], []
                for h, hs in enumerate(heads):
                    s = jnp.where(okj, ss[h] * scale + b_ref[h], NEG_INF)
                    p = jnp.exp(s - lj[:, h * HEAD_DIM:h * HEAD_DIM + 1])
                    ds = p * (dps[h] - dlj[:, h * HEAD_DIM:h * HEAD_DIM + 1])
                    ps.append(p.astype(BF16))
                    dss.append(ds)
                    dsbs.append(ds.astype(BF16))
                dqs = [jnp.dot(dsbs[h], kj[:, hs], preferred_element_type=F32) * scale for h, hs in enumerate(heads)]
                dks = [lax.dot_general(qj[:, hs], dsbs[h], (((0,), (0,)), ((), ())), preferred_element_type=F32) * scale
                       for h, hs in enumerate(heads)]
                dvs = [lax.dot_general(doj[:, hs], ps[h], (((0,), (0,)), ((), ())), preferred_element_type=F32)
                       for h, hs in enumerate(heads)]
                dq_ref[rows, :] = jnp.concatenate(dqs, axis=1).astype(BF16)
                c0 = tq + j * sb
                ak_ref[:, c0:c0 + w] += jnp.concatenate(dks, axis=0)
                av_ref[:, c0:c0 + w] += jnp.concatenate(dvs, axis=0)
                for h in range(HEADS):
                    db_ref[h] += dss[h]

        dk_ref[...] = ak_ref[:, ATT_RADIUS:ATT_RADIUS + tq].T.astype(BF16)
        dv_ref[...] = av_ref[:, ATT_RADIUS:ATT_RADIUS + tq].T.astype(BF16)
        keep = acc_cols - tq
        nk, nv = ak_ref[:, tq:acc_cols], av_ref[:, tq:acc_cols]
        ak_ref[:, 0:keep] = nk
        av_ref[:, 0:keep] = nv
        ak_ref[:, keep:acc_cols] = jnp.zeros((ATT_GW, tq), F32)
        av_ref[:, keep:acc_cols] = jnp.zeros((ATT_GW, tq), F32)

    def tile(r, n):
        return r * nbl + jnp.minimum(n, nbl - 1)

    main = pl.BlockSpec((tq, ATT_GW), lambda r, n: (tile(r, n), 0))
    lag = pl.BlockSpec((tq, ATT_GW), lambda r, n: (r * nbl + jnp.maximum(n - 1, 0), 0))
    bspec = pl.BlockSpec((HEADS, sb, w), lambda r, n: (0, 0, 0))
    return pl.pallas_call(
        body, name=name, grid=(n_res, nbl + 1),
        in_specs=[*_qkv_specs(tq, per, last_blk, cols, tile), bspec, main, main, main],
        out_specs=[main, lag, lag, bspec],
        out_shape=[jax.ShapeDtypeStruct((t, ATT_GW), BF16)] * 3 + [jax.ShapeDtypeStruct((HEADS, sb, w), F32)],
        scratch_shapes=[pltpu.VMEM((ATT_GW, acc_cols), F32), pltpu.VMEM((ATT_GW, acc_cols), F32)],
        compiler_params=_cp(2),
    )(q, k, k, k, v, v, v, bias, delta, do, lse)


def _bias_grad(db, idx, name):
    _, sb, w = db.shape

    def body(db_ref, idx_ref, o_ref):
        ix = idx_ref[...]
        lane = lax.broadcasted_iota(jnp.int32, (1, 128), 1)
        rows = []
        for h in range(HEADS):
            d = db_ref[h]
            acc = jnp.zeros((1, 128), F32)
            for b in range(NUM_BUCKETS):
                acc = acc + jnp.where(lane == b, jnp.sum(jnp.where(ix == b, d, 0.0)), 0.0)
            rows.append(acc)
        o_ref[...] = jnp.concatenate(rows + [jnp.zeros((8 - HEADS, 128), F32)], axis=0)

    return pl.pallas_call(
        body, name=name, out_shape=jax.ShapeDtypeStruct((8, 128), F32),
        in_specs=[pl.BlockSpec(memory_space=pltpu.VMEM), pl.BlockSpec(memory_space=pltpu.VMEM)],
        out_specs=pl.BlockSpec(memory_space=pltpu.VMEM),
    )(db, idx)


def _mem_fwd(zq, kv, name, tq=512):
    t = zq.shape[0]
    tq = min(tq, t)
    scale = MEM_HD ** -0.5

    def body(q_ref, kv_ref, o_ref):
        outs = []
        for h in range(MEM_HEADS):
            hs = slice(h * MEM_HD, (h + 1) * MEM_HD)
            kh = kv_ref[:, h * MEM_HD:(h + 1) * MEM_HD]
            vh = kv_ref[:, MEM_W + h * MEM_HD:MEM_W + (h + 1) * MEM_HD]
            s = lax.dot_general(q_ref[:, hs], kh, (((1,), (1,)), ((), ())), preferred_element_type=F32) * scale
            e = jnp.exp(s - jnp.max(s, axis=-1, keepdims=True))
            den = jnp.sum(e, axis=-1, keepdims=True)
            outs.append(jnp.dot(e.astype(BF16), vh, preferred_element_type=F32) / den)
        o_ref[...] = jnp.concatenate(outs, axis=1).astype(BF16)

    row = pl.BlockSpec((tq, MEM_W), lambda i: (i, 0))
    return pl.pallas_call(
        body, name=name, grid=(t // tq,),
        in_specs=[row, pl.BlockSpec((N_MEM, 2 * MEM_W), lambda i: (0, 0))], out_specs=row,
        out_shape=jax.ShapeDtypeStruct((t, MEM_W), BF16), compiler_params=_cp(1),
    )(zq, kv)


def _mem_bwd(zq, kv, do, name, tq=512):
    t = zq.shape[0]
    tq = min(tq, t)
    nt = t // tq
    scale = MEM_HD ** -0.5

    def body(q_ref, kv_ref, do_ref, dq_ref, out_ref, dkv_ref):
        @pl.when(pl.program_id(0) == 0)
        def _():
            dkv_ref[...] = jnp.zeros_like(dkv_ref)

        ks = [slice(h * MEM_HD, (h + 1) * MEM_HD) for h in range(MEM_HEADS)]
        vs = [slice(MEM_W + h * MEM_HD, MEM_W + (h + 1) * MEM_HD) for h in range(MEM_HEADS)]
        ss = [lax.dot_general(q_ref[:, k], kv_ref[:, k], NT, preferred_element_type=F32) * scale for k in ks]
        dps = [lax.dot_general(do_ref[:, k], kv_ref[:, v], NT, preferred_element_type=F32) for k, v in zip(ks, vs)]
        pbs, dsbs = [], []
        for s, dp in zip(ss, dps):
            e = jnp.exp(s - jnp.max(s, axis=-1, keepdims=True))
            p = e / jnp.sum(e, axis=-1, keepdims=True)
            ds = p * (dp - jnp.sum(p * dp, axis=-1, keepdims=True))
            pbs.append(p.astype(BF16))
            dsbs.append(ds.astype(BF16))
        dq_ref[...] = jnp.concatenate([jnp.dot(dsb, kv_ref[:, k], preferred_element_type=F32) * scale
                                       for dsb, k in zip(dsbs, ks)], axis=1).astype(BF16)
        for dsb, pb, k, v in zip(dsbs, pbs, ks, vs):
            dkv_ref[k, :] += jnp.dot(q_ref[:, k].T, dsb, preferred_element_type=F32) * scale
            dkv_ref[v, :] += jnp.dot(do_ref[:, k].T, pb, preferred_element_type=F32)

        @pl.when(pl.program_id(0) == nt - 1)
        def _():
            out_ref[...] = dkv_ref[...].T

    row = pl.BlockSpec((tq, MEM_W), lambda i: (i, 0))
    full = pl.BlockSpec((N_MEM, 2 * MEM_W), lambda i: (0, 0))
    return pl.pallas_call(
        body, name=name, grid=(nt,), in_specs=[row, full, row], out_specs=[row, full],
        out_shape=[jax.ShapeDtypeStruct((t, MEM_W), BF16), jax.ShapeDtypeStruct((N_MEM, 2 * MEM_W), F32)],
        scratch_shapes=[pltpu.VMEM((2 * MEM_W, N_MEM), F32)], compiler_params=_cp(1),
    )(zq, kv, do)


NT = (((1,), (1,)), ((), ()))


def _branches_fwd(acts, ws, zg, bg, name, tr=512):
    t = zg.shape[0]
    tr = min(tr, t)

    def body(a0, a1, a2, w0, w1, w2, z_ref, b_ref, y0, y1, y2, o_ref):
        merged = None
        for i, (a_ref, w_ref, y_ref) in enumerate(((a0, w0, y0), (a1, w1, y1), (a2, w2, y2))):
            y = lax.dot_general(a_ref[...], w_ref[...], NT, preferred_element_type=F32)
            y_ref[...] = y.astype(BF16)
            g = _sigmoid(z_ref[:, i * D:(i + 1) * D].astype(F32) + b_ref[:, i * D:(i + 1) * D])
            merged = g * y if merged is None else merged + g * y
        o_ref[...] = merged.astype(BF16)

    row = pl.BlockSpec((tr, D), lambda i: (i, 0))
    outs = pl.pallas_call(
        body, name=name, grid=(t // tr,),
        in_specs=[pl.BlockSpec((tr, a.shape[1]), lambda i: (i, 0)) for a in acts]
        + [pl.BlockSpec(w.shape, lambda i: (0, 0)) for w in ws]
        + [pl.BlockSpec((tr, 3 * D), lambda i: (i, 0)), pl.BlockSpec((1, 3 * D), lambda i: (0, 0))],
        out_specs=[row] * 4, out_shape=[jax.ShapeDtypeStruct((t, D), BF16)] * 4, compiler_params=_cp(1),
    )(*acts, *ws, zg, _row(bg))
    return tuple(outs[:3]), outs[3]


def _head_sums(dd):
    return jnp.concatenate(
        [jnp.broadcast_to(jnp.sum(dd[:, h * HEAD_DIM:(h + 1) * HEAD_DIM], axis=-1, keepdims=True), (dd.shape[0], HEAD_DIM))
         for h in range(HEADS)], axis=1)


def _branches_bwd(dymix, w_out, ws, zg, bg, ys, att, name, tr=512):
    t = zg.shape[0]
    tr = min(tr, t)

    def body(dy_ref, wo_ref, w0, w1, w2, z_ref, b_ref, y0, y1, y2, att_ref, d0, d1, d2, dz_ref, db_ref, da0, da1, da2, dl_ref):
        @pl.when(pl.program_id(0) == 0)
        def _():
            db_ref[...] = jnp.zeros_like(db_ref)

        dm = lax.dot_general(dy_ref[...], wo_ref[...], NT, preferred_element_type=F32)
        for i, (y_ref, w_ref, d_ref, da_ref) in enumerate(((y0, w0, d0, da0), (y1, w1, d1, da1), (y2, w2, d2, da2))):
            gi = _sigmoid(z_ref[:, i * D:(i + 1) * D].astype(F32) + b_ref[:, i * D:(i + 1) * D])
            dy = (dm * gi).astype(BF16)
            d_ref[...] = dy
            dz = dm * y_ref[...].astype(F32) * gi * (1.0 - gi)
            dz_ref[:, i * D:(i + 1) * D] = dz.astype(BF16)
            db_ref[0:1, i * D:(i + 1) * D] += jnp.sum(dz, axis=0, keepdims=True)
            da = jnp.dot(dy, w_ref[...], preferred_element_type=F32).astype(BF16)
            da_ref[...] = da
            if i == 1:
                dl_ref[...] = _head_sums(att_ref[...].astype(F32) * da.astype(F32))

    row = pl.BlockSpec((tr, D), lambda i: (i, 0))
    wide = pl.BlockSpec((tr, 3 * D), lambda i: (i, 0))
    heads = pl.BlockSpec((tr, ATT_GW), lambda i: (i, 0))
    outs = pl.pallas_call(
        body, name=name, grid=(t // tr,),
        in_specs=[row, pl.BlockSpec(w_out.shape, lambda i: (0, 0))] + [pl.BlockSpec(w.shape, lambda i: (0, 0)) for w in ws]
        + [wide, pl.BlockSpec((1, 3 * D), lambda i: (0, 0)), row, row, row, heads],
        out_specs=[row, row, row, wide, pl.BlockSpec((8, 3 * D), lambda i: (0, 0))]
        + [pl.BlockSpec((tr, w.shape[1]), lambda i: (i, 0)) for w in ws] + [heads],
        out_shape=[jax.ShapeDtypeStruct((t, D), BF16)] * 3
        + [jax.ShapeDtypeStruct((t, 3 * D), BF16), jax.ShapeDtypeStruct((8, 3 * D), F32)]
        + [jax.ShapeDtypeStruct((t, w.shape[1]), BF16) for w in ws] + [jax.ShapeDtypeStruct((t, ATT_GW), F32)],
        compiler_params=_cp(1),
    )(dymix, w_out, *ws, zg, _row(bg), *ys, att)
    return tuple(outs[:3]), outs[3], outs[4][0], tuple(outs[5:8]), outs[8]


def _adamw(w, g, m, v, name):
    shape = w.shape
    w, g, m, v = (a.reshape(-1, shape[-1]) for a in (w, g, m, v))
    r, d = w.shape
    tr = next((c for c in (512, 352, 256, 128, 64, 32, 16, 8) if r % c == 0), r)

    def body(w_ref, g_ref, m_ref, v_ref, d_ref, nm_ref, nv_ref):
        gg = g_ref[...]
        m2 = ADAM_B1 * m_ref[...] + (1.0 - ADAM_B1) * gg
        v2 = ADAM_B2 * v_ref[...] + (1.0 - ADAM_B2) * (gg * gg)
        m_hat = m2 / (1.0 - ADAM_B1 ** ADAM_STEP)
        v_hat = v2 / (1.0 - ADAM_B2 ** ADAM_STEP)
        d_ref[...] = -ADAM_LR * (m_hat / (jnp.sqrt(v_hat) + ADAM_EPS) + ADAM_WD * w_ref[...])
        nm_ref[...] = m2
        nv_ref[...] = v2

    row = pl.BlockSpec((tr, d), lambda i: (i, 0))
    outs = pl.pallas_call(
        body, name=name, grid=(r // tr,), in_specs=[row] * 4, out_specs=[row] * 3,
        out_shape=[jax.ShapeDtypeStruct((r, d), F32)] * 3, compiler_params=_cp(1),
    )(w, g, m, v)
    return [o.reshape(shape) for o in outs]


def _slab_tile(rows):
    return next((c for c in range(min(rows, 512), 15, -16) if rows % c == 0), rows)


def _sum_slots(parts, out_dtype, name):
    n, r, d = parts.shape
    tr = _slab_tile(r)

    def body(p_ref, o_ref):
        acc = p_ref[0].astype(F32)
        for s in range(1, n):
            acc = acc + p_ref[s].astype(F32)
        o_ref[...] = acc.astype(o_ref.dtype)

    return pl.pallas_call(
        body, name=name, grid=(r // tr,), in_specs=[pl.BlockSpec((n, tr, d), lambda i: (0, i, 0))],
        out_specs=pl.BlockSpec((tr, d), lambda i: (i, 0)),
        out_shape=jax.ShapeDtypeStruct((r, d), out_dtype), compiler_params=_cp(1),
    )(parts)


def _place():
    return lax.axis_index("x"), lax.axis_index("y"), lax.axis_index("c")


def _all_gather(shard, name):
    r, d = shard.shape

    def body(x_ref, out_ref, send_sems, recv_sems, local_sem):
        x, y, c = _place()
        me, sibling = (x, y, c), (x, y, 1 - c)
        chips = [(1 - x, y), (x, 1 - y), (1 - x, 1 - y)]

        def slot(px, py, pc):
            return out_ref.at[4 * px + 2 * py + pc]

        def copy(k, block, to, src=None):
            return pltpu.make_async_remote_copy(
                src_ref=slot(*block) if src is None else src, dst_ref=slot(*block),
                send_sem=send_sems.at[k], recv_sem=recv_sems.at[k], device_id=to, device_id_type=MESH)

        mine = pltpu.make_async_copy(x_ref, slot(*me), local_sem)
        mine.start()
        first = [copy(0, me, sibling, src=x_ref)]
        first += [copy(1 + j, me, (*chip, c), src=x_ref) for j, chip in enumerate(chips)]
        for cp in first:
            cp.start()
        passed = [copy(4 + j, (*chip, c), sibling) for j, chip in enumerate(chips)]
        for j, chip in enumerate(chips):
            copy(1 + j, (*chip, c), me).wait_recv()
            passed[j].start()
        copy(0, sibling, me).wait_recv()
        for j, chip in enumerate(chips):
            copy(4 + j, (*chip, 1 - c), me).wait_recv()
        for cp in first + passed:
            cp.wait_send()
        mine.wait()

    return pl.pallas_call(
        body, name=name, out_shape=jax.ShapeDtypeStruct((N_DEV, r, d), shard.dtype),
        in_specs=[pl.BlockSpec(memory_space=pl.ANY)], out_specs=pl.BlockSpec(memory_space=pl.ANY),
        scratch_shapes=[pltpu.SemaphoreType.DMA((7,)), pltpu.SemaphoreType.DMA((7,)), pltpu.SemaphoreType.DMA],
    )(shard)


HBM_SPEC = pl.BlockSpec(memory_space=pltpu.HBM)
SEM_SPEC = pl.BlockSpec(memory_space=pltpu.SEMAPHORE)
SPLIT_PARAMS = pltpu.CompilerParams(has_side_effects=pltpu.SideEffectType.DATAFLOW_SIDE_EFFECTING)


def _peers():
    x, y, c = _place()
    flip = lambda v, bit: 1 - v if bit else v
    return 4 * x + 2 * y + c, [((flip(x, k >> 2 & 1), flip(y, k >> 1 & 1), flip(c, k & 1))) for k in range(1, N_DEV)]


def _exchange_start(src, all_gather, name):
    r, d = src.shape[-2:]

    def body(src_ref, land_ref, send_sems, recv_sems, src_thru, land_thru, token):
        me, peers = _peers()
        for k, (px, py, pc) in enumerate(peers):
            part = src_ref if all_gather else src_ref.at[4 * px + 2 * py + pc]
            pltpu.make_async_remote_copy(src_ref=part, dst_ref=land_ref.at[me], send_sem=send_sems.at[k],
                                         recv_sem=recv_sems.at[k], device_id=(px, py, pc), device_id_type=MESH).start()
        token[...] = jnp.zeros_like(token)

    land = lax.empty((N_DEV, r, d), src.dtype)
    return pl.pallas_call(
        body, name=name,
        out_shape=(pltpu.SemaphoreType.DMA((N_DEV - 1,)), pltpu.SemaphoreType.DMA((N_DEV - 1,)), pltpu.HBM(src.shape, src.dtype),
                   pltpu.HBM(land.shape, land.dtype), jax.ShapeDtypeStruct((8, LANES), F32)),
        in_specs=(HBM_SPEC, HBM_SPEC), out_specs=(SEM_SPEC, SEM_SPEC, HBM_SPEC, HBM_SPEC, pl.BlockSpec(memory_space=pltpu.VMEM)),
        input_output_aliases={0: 2, 1: 3}, compiler_params=SPLIT_PARAMS,
    )(pltpu.with_memory_space_constraint(src, pltpu.HBM), pltpu.with_memory_space_constraint(land, pltpu.HBM))


def _exchange_wait(started, after, all_gather, name):
    send_sems, recv_sems, src_thru, land_thru, _ = started

    def body(src_ref, land_ref, send_sems, recv_sems, after_ref, src_out, land_out):
        me, peers = _peers()
        for k, (px, py, pc) in enumerate(peers):
            part = src_ref if all_gather else src_ref.at[4 * px + 2 * py + pc]
            cp = pltpu.make_async_remote_copy(src_ref=part, dst_ref=land_ref.at[4 * px + 2 * py + pc], send_sem=send_sems.at[k],
                                              recv_sem=recv_sems.at[k], device_id=(px, py, pc), device_id_type=MESH)
            cp.wait_send()
            cp.wait_recv()

    return pl.pallas_call(
        body, name=name,
        out_shape=(pltpu.HBM(src_thru.shape, src_thru.dtype), pltpu.HBM(land_thru.shape, land_thru.dtype)),
        in_specs=(HBM_SPEC, HBM_SPEC, SEM_SPEC, SEM_SPEC, pl.BlockSpec(memory_space=pl.ANY)), out_specs=(HBM_SPEC, HBM_SPEC),
        input_output_aliases={0: 0, 1: 1}, compiler_params=SPLIT_PARAMS,
    )(src_thru, land_thru, send_sems, recv_sems, after)


def _own_slot(land, mine):
    x, y, c = _place()
    return lax.dynamic_update_slice(land, mine[None], (4 * x + 2 * y + c, 0, 0))


BIG = (("w_in", (D, 864), 1), ("w_conv_out", (CONV_W, 128), 1), ("w_att_out", (ATT_GW, 128), 1), ("w_mem_kv", (128, D), 0),
       ("w_mem_out", (MEM_W, 128), 1), ("w_out", (128, D), 0), ("w_ffn_in", (D, 704), 1), ("w_ffn_out", (352, D), 0),
       ("conv_dw", (CONV_K, 64), 1))
SMALL = (("rel_bias", (NUM_BUCKETS, 12)), ("norm_mix_pre", (DEPTH, D)), ("b_gate", (DEPTH, 3 * D)),
         ("conv_dw_bias", (DEPTH, CONV_W)), ("conv_ln_g", (DEPTH, CONV_W)), ("conv_ln_b", (DEPTH, CONV_W)),
         ("norm_mem", (DEPTH, D)), ("norm_mix_post", (DEPTH, D)), ("norm_ffn_pre", (DEPTH, D)), ("norm_ffn_post", (DEPTH, D)))
TWIN_WEIGHTS = ("rel_bias", "norm_mix_pre", "w_in", "b_gate", "conv_dw", "conv_dw_bias", "conv_ln_g", "conv_ln_b", "w_conv_out",
                "w_att_out", "norm_mem", "w_mem_kv", "w_mem_out", "w_out", "norm_mix_post", "norm_ffn_pre", "w_ffn_in",
                "w_ffn_out", "norm_ffn_post")


BIG_INFO = {n: (s, a) for n, s, a in BIG}
GROUPS = {"in": ("w_in",), "mix": ("conv_dw", "w_conv_out", "w_att_out", "w_mem_kv", "w_mem_out", "w_out"),
          "ffn_in": ("w_ffn_in",), "ffn_out": ("w_ffn_out",)}
TILE_BYTES_PER_LANE = 32


def _rows_of(shape, dtype):
    tile = TILE_BYTES_PER_LANE // jnp.dtype(dtype).itemsize
    return -(-int(np.prod(shape)) // (D * tile)) * tile


def _row_form(shape, axis):
    return tuple(shape) if axis == 0 else (shape[1], shape[0])


def _as_slab_rows(p, lead=()):
    shape = p.shape[len(lead):]
    rows = _rows_of(shape, p.dtype)
    if shape == (rows, D):
        return p
    n = int(np.prod(shape))
    p = jnp.pad(p.reshape(*lead, n), [(0, 0)] * len(lead) + [(0, rows * D - n)])
    return p.reshape(*lead, rows, D)


def _from_slab_rows(rows, shape, lead=()):
    if rows.shape[len(lead):] == tuple(shape):
        return rows
    n = int(np.prod(shape))
    return rows.reshape(*lead, -1)[..., :n].reshape(*lead, *shape)


def _pack(pieces, dtype):
    return jnp.concatenate([_as_slab_rows(p.astype(dtype)) for p in pieces], axis=0)


def _unpack(slab, shapes, packed_as):
    out, r0 = [], 0
    for s in shapes:
        rows = _rows_of(s, packed_as)
        out.append(_from_slab_rows(slab[r0:r0 + rows], s))
        r0 += rows
    return out


def _pack_group(blocks, layer, group):
    return _pack([blocks[n][layer] if BIG_INFO[n][1] == 0 else blocks[n][layer].T for n in GROUPS[group]], BF16)


def _full_weights(gathered, group):
    out, r0 = {}, 0
    for name in GROUPS[group]:
        r, k = _row_form(*BIG_INFO[name])
        rows = _rows_of((r, k), BF16)
        out[name] = _from_slab_rows(gathered[:, r0:r0 + rows], (r, k), lead=(N_DEV,)).reshape(N_DEV * r, k)
        r0 += rows
    return out


def _scatter_layout(grads, group):
    parts = []
    for name in GROUPS[group]:
        r, k = _row_form(*BIG_INFO[name])
        parts.append(_as_slab_rows(grads[name].astype(BF16).reshape(N_DEV, r, k), lead=(N_DEV,)))
    return jnp.concatenate(parts, axis=1)


def _unpack_blocks(slab, group):
    out = _unpack(slab, [_row_form(*BIG_INFO[n]) for n in GROUPS[group]], BF16)
    return {n: b if BIG_INFO[n][1] == 0 else b.T for n, b in zip(GROUPS[group], out)}


def _split_w_in(w_in):
    edges = np.cumsum((0,) + SEG)
    return [w_in[edges[i]:edges[i + 1]] for i in range(4)]


def _conv_taps(conv_dw):
    return jnp.pad(conv_dw.astype(F32).T, ((0, 32 - CONV_K), (0, 0)))


def _layer_fwd(l, x, h, nm, get_w, sm, bias_tabs):
    t = x.shape[0]
    win = get_w(l, "in", h)
    w_seg = _split_w_in(win["w_in"])
    zc = _mm(h, w_seg[0], "fwd_in_conv", tb=True)
    za = _mm(h, w_seg[1], "fwd_in_att", tb=True, tn=1152)
    zq = _mm(h, w_seg[2], "fwd_in_memq", tb=True)
    zg = _mm(h, w_seg[3], "fwd_in_gate", tb=True)
    wts = get_w(l, "mix", zg)
    y, cact = _conv_fwd(zc, _conv_taps(wts["conv_dw"]), sm["conv_dw_bias"][l], sm["conv_ln_g"][l], sm["conv_ln_b"][l],
                        "conv_fwd")
    qkv, outs, lses = [], [], []
    for g, (_, dil) in enumerate(ATT_PATTERNS):
        cols = tuple(i * 3 + g for i in range(3))
        if dil == 1:
            trio, at = (za, za, za), cols
        else:
            trio, at = _to_residue_major([(za, c) for c in cols], dil, f"qkv_residue_d{dil}"), (0, 0, 0)
        o, ls = _att_fwd(*trio, bias_tabs[g], t // dil, f"att_fwd_d{dil}", cols=at)
        qkv.append((trio, at))
        outs.append(o)
        lses.append(ls)
    att, lse = _att_combine(outs, lses, "att_combine")
    kv = _mm(nm, wts["w_mem_kv"], "fwd_mem_kv")
    om = _mem_fwd(zq, kv, "mem_fwd")
    w_br = (wts["w_conv_out"], wts["w_att_out"], wts["w_mem_out"])
    ys, merged = _branches_fwd((cact, att, om), w_br, zg, sm["b_gate"][l], "branches_fwd")
    ymix, x1, h2 = _mm_norm_res(merged, wts["w_out"], x, sm["norm_mix_post"][l], sm["norm_ffn_pre"][l], "fwd_out_norm")
    gff, uff, act = _mm_swiglu_fwd(h2, get_w(l, "ffn_in", h2)["w_ffn_in"], "fwd_ffn_in_swiglu")
    g_next = sm["norm_mix_pre"][l + 1] if l + 1 < DEPTH else None
    f, x2, h_next = _mm_norm_res(act, get_w(l, "ffn_out", act)["w_ffn_out"], x1, sm["norm_ffn_post"][l], g_next,
                                 "fwd_ffn_out_norm" if g_next is not None else "fwd_ffn_out_last")
    saved = dict(x=x, h=h, zc=zc, zq=zq, zg=zg, y=y, cact=cact, qkv=qkv, att=att, lse=lse, kv=kv, om=om, ys=ys,
                 merged=merged, ymix=ymix, x1=x1, h2=h2, gff=gff, uff=uff, act=act, f=f)
    return x2, h_next, saved


def _after(value, *tokens):
    for t in tokens:
        if t is not None:
            value = value + t[0, 0].astype(value.dtype)
    return value


def _layer_bwd(l, dx2, s, nm, mem, get_w, put_g, sm, bias_tabs, bucket_idx):
    t = dx2.shape[0]
    win, wts = get_w(l, "in", None), get_w(l, "mix", None)
    w_ffn_in, w_ffn_out = get_w(l, "ffn_in", None)["w_ffn_in"], get_w(l, "ffn_out", None)["w_ffn_out"]
    gb, gs = {}, {}
    df, gs["norm_ffn_post"] = _norm_bwd(s["f"], sm["norm_ffn_post"][l], dx2, None, BF16, "bwd_norm_ffn_post")
    dg, du = _mm_swiglu_bwd(df, w_ffn_out, s["gff"], s["uff"], "bwd_ffn_out_d_swiglu")
    tok_out = put_g(l, "ffn_out", {"w_ffn_out": _mm(s["act"], df, "bwd_ffn_out_w", ta=True, tm=1408)})
    tok_in = put_g(l, "ffn_in", {"w_ffn_in": jnp.concatenate([_mm(dg, s["h2"], "bwd_ffn_in_w_gate", ta=True, tm=1408),
                                                              _mm(du, s["h2"], "bwd_ffn_in_w_up", ta=True, tm=1408)], axis=0)})
    dh2 = _mm_sum([(dg, _after(w_ffn_in[:FFN_H], tok_out, tok_in), FFN_H // 2), (du, w_ffn_in[FFN_H:], FFN_H // 2)],
                  "bwd_ffn_in_d")
    dx1, dymix, gs["norm_ffn_pre"], gs["norm_mix_post"] = _norm_bwd_pair(
        s["x1"], sm["norm_ffn_pre"][l], dh2, dx2, s["ymix"], sm["norm_mix_post"][l], "bwd_norm_ffn_pre_mix_post")
    gb["w_out"] = _mm(s["merged"], dymix, "bwd_out_w", ta=True)
    w_br = (wts["w_conv_out"], wts["w_att_out"], wts["w_mem_out"])
    dys, dzg, gs["b_gate"], (dcact, datt, dom), delta = _branches_bwd(
        dymix, wts["w_out"], w_br, s["zg"], sm["b_gate"][l], s["ys"], s["att"], "branches_bwd")
    gb["w_conv_out"] = _mm(dys[0], s["cact"], "bwd_conv_out_w", ta=True)
    dzc, dconv, gs["conv_dw_bias"], gs["conv_ln_g"], gs["conv_ln_b"] = _conv_bwd(
        s["zc"], s["y"], dcact, _conv_taps(wts["conv_dw"]), sm["conv_ln_g"][l], sm["conv_ln_b"][l], "conv_bwd")
    gb["conv_dw"] = dconv[:CONV_K].T
    gb["w_att_out"] = _mm(dys[1], s["att"], "bwd_att_out_w", ta=True)
    dqkv, rel = [], []
    for g, (_, dil) in enumerate(ATT_PATTERNS):
        trio, at = s["qkv"][g]
        if dil == 1:
            dl_r, do_r, lse_r = delta, datt, s["lse"]
        else:
            dl_r, do_r, lse_r = _to_residue_major([(delta, 0), (datt, 0), (s["lse"], 0)], dil, f"att_bwd_residue_d{dil}")
        dq, dk, dv, dbias = _att_bwd(*trio, bias_tabs[g], dl_r, do_r, lse_r, t // dil, f"att_bwd_d{dil}", cols=at)
        dqkv.append((dq, dk, dv))
        rel.append(_bias_grad(dbias, bucket_idx[g], f"bias_grad_d{dil}")[:HEADS, :NUM_BUCKETS])
    dza = _assemble_dza(dqkv, "att_bwd_assemble")
    gs["rel_bias"] = jnp.concatenate(rel, axis=0).T
    gb["w_mem_out"] = _mm(dys[2], s["om"], "bwd_mem_out_w", ta=True)
    dzq, dkv = _mem_bwd(s["zq"], s["kv"], dom, "mem_bwd")
    dkv = dkv.astype(BF16)
    gb["w_mem_kv"] = _mm(nm, dkv, "bwd_mem_kv_w", ta=True)
    tok = put_g(l, "mix", gb)
    dnm = _mm(dkv, wts["w_mem_kv"], "bwd_mem_kv_d", tb=True, out_dtype=F32)
    _, gs["norm_mem"] = _norm_bwd(mem, sm["norm_mem"][l], dnm, None, BF16, "bwd_norm_mem")
    segs = ((dzc, "conv", 1024), (dza, "att", 1152), (_after(dzq, tok), "memq", 512), (dzg, "gate", 1024))
    tok = put_g(l, "in", {"w_in": jnp.concatenate([_mm(dz, s["h"], f"bwd_in_{nm_}_w", ta=True, tm=min(blk, 1152), tk=2048)
                                                   for dz, nm_, blk in segs], axis=0)})
    w_seg = _split_w_in(win["w_in"])
    w_seg[0] = _after(w_seg[0], tok)
    dx, gs["norm_mix_pre"] = _mm_sum([(dz, w_seg[i], blk) for i, (dz, _, blk) in enumerate(segs)], "bwd_in_d_norm",
                                     norm=(s["x"], sm["norm_mix_pre"][l], dx1))
    return dx, gs


def _local_step(x, mem, target, sm, get_w, put_g, tokens=()):
    t = x.shape[0]
    bias_tabs, bucket_idx = [], []
    for g, (_, dil) in enumerate(ATT_PATTERNS):
        idx = _bucket_index(_att_tiles(t // dil)[1], dil)
        bucket_idx.append(jnp.asarray(idx))
        bias_tabs.append(_bias_table(sm["rel_bias"][:, g * HEADS:(g + 1) * HEADS], bucket_idx[g], f"bias_table_d{dil}"))
    h = _norm_plain(x, _after(sm["norm_mix_pre"][0], *tokens), "norm_first")
    saved, nms = [], []
    for l in range(DEPTH):
        nm = _norm_plain(mem, sm["norm_mem"][l], "norm_mem")
        x, h, s = _layer_fwd(l, x, h, nm, get_w, sm, bias_tabs)
        saved.append(s)
        nms.append(nm)
    loss, dx = _loss_head(x, target, "loss_head")
    gsmall = {}
    for l in reversed(range(DEPTH)):
        dx, gs = _layer_bwd(l, dx, saved[l], nms[l], mem, get_w, put_g, sm, bias_tabs, bucket_idx)
        for n, v in gs.items():
            gsmall.setdefault(n, {})[l] = v
    small = {}
    for n, _ in SMALL:
        small[n] = gsmall[n][0] + gsmall[n][1] if n == "rel_bias" else jnp.stack([gsmall[n][0], gsmall[n][1]])
    return loss, dx, small


def kernel(x, mem, rel_bias, norm_mix_pre, w_in, b_gate, conv_dw, conv_dw_bias, conv_ln_g, conv_ln_b, w_conv_out, w_att_out, norm_mem, w_mem_kv, w_mem_out, w_out, norm_mix_post, norm_ffn_pre, w_ffn_in, w_ffn_out, norm_ffn_post, loss_target, m_rel_bias, m_norm_mix_pre, m_w_in, m_b_gate, m_conv_dw, m_conv_dw_bias, m_conv_ln_g, m_conv_ln_b, m_w_conv_out, m_w_att_out, m_norm_mem, m_w_mem_kv, m_w_mem_out, m_w_out, m_norm_mix_post, m_norm_ffn_pre, m_w_ffn_in, m_w_ffn_out, m_norm_ffn_post, v_rel_bias, v_norm_mix_pre, v_w_in, v_b_gate, v_conv_dw, v_conv_dw_bias, v_conv_ln_g, v_conv_ln_b, v_w_conv_out, v_w_att_out, v_norm_mem, v_w_mem_kv, v_w_mem_out, v_w_out, v_norm_mix_post, v_norm_ffn_pre, v_w_ffn_in, v_w_ffn_out, v_norm_ffn_post):
    args = dict(locals())
    w = {n: args[n] for n in TWIN_WEIGHTS}
    m = {n: args["m_" + n] for n in TWIN_WEIGHTS}
    v = {n: args["v_" + n] for n in TWIN_WEIGHTS}
    sm = {n: w[n] for n, _ in SMALL}

    first = (0, "in")
    keys = [(l, grp) for l in range(DEPTH) for grp in GROUPS]
    shard = {(l, grp): _pack_group(w, l, grp) for l, grp in keys}
    weights = {first: _full_weights(_all_gather(shard[first], "gather_l0_in"), "in")}
    settled = weights[first]["w_in"][:1, :1] * 0
    ag = {}
    for k in keys:
        if k != first:
            ag[k] = _exchange_start(_after(shard[k], settled), True, f"ag_start_l{k[0]}_{k[1]}")
            settled = ag[k][4]

    def get_w(l, grp, after):
        if (l, grp) not in weights:
            mine, land = _exchange_wait(ag[l, grp], after, True, f"ag_wait_l{l}_{grp}")
            weights[l, grp] = _full_weights(_own_slot(land, mine), grp)
        return weights[l, grp]

    rs = {}

    def put_g(l, grp, grads):
        rs[l, grp] = _exchange_start(_scatter_layout(grads, grp), False, f"rs_start_l{l}_{grp}")
        return rs[l, grp][4]

    loss, dx, gsmall = _local_step(x[0], mem[0], loss_target[0], sm, get_w, put_g, [st[4] for st in ag.values()])
    loss = lax.psum(loss, ("x", "y", "c"))

    xi, yi, ci = _place()
    me = 4 * xi + 2 * yi + ci
    g_slab = {}
    for k, st in rs.items():
        contrib, land = _exchange_wait(st, dx, False, f"rs_wait_l{k[0]}_{k[1]}")
        own = lax.dynamic_index_in_dim(contrib, me, axis=0, keepdims=False)
        g_slab[k] = _sum_slots(_own_slot(land, own), F32, f"rs_sum_l{k[0]}_{k[1]}")
    g_layers = [{n: b for grp in GROUPS for n, b in _unpack_blocks(g_slab[l, grp], grp).items()} for l in range(DEPTH)]
    small_shapes = [s for _, s in SMALL]
    g_small = _unpack(_sum_slots(_all_gather(_pack([gsmall[n] for n, _ in SMALL], F32), "gather_small"), F32, "sum_small"),
                      small_shapes, F32)
    grads = {n: jnp.stack([g_layers[0][n], g_layers[1][n]]) for n, _, _ in BIG}
    grads.update({n: g_small[i] for i, (n, _) in enumerate(SMALL)})

    delta, new_m, new_v = {}, {}, {}
    for n in TWIN_WEIGHTS:
        delta[n], new_m[n], new_v[n] = _adamw(w[n], grads[n], m[n], v[n], f"adamw_{n}")

    return (loss, dx[None], *[grads[n] for n in TWIN_WEIGHTS], *[delta[n] for n in TWIN_WEIGHTS],
            *[new_m[n] for n in TWIN_WEIGHTS], *[new_v[n] for n in TWIN_WEIGHTS])
```

```python
import functools

import numpy as np
import jax
import jax.numpy as jnp
from jax import lax
from jax.experimental import pallas as pl
from jax.experimental.pallas import tpu as pltpu

F32 = jnp.float32
BF16 = jnp.bfloat16

N_DEV = 8
D = 1024
DEPTH = 2
CONV_W = 512
CONV_K = 31
CONV_PAD = 16
ATT_PATTERNS = ((128, 1), (512, 4), (2048, 16))
ATT_RADIUS = 64
HEADS = 4
HEAD_DIM = 64
ATT_GW = HEADS * HEAD_DIM
ATT_W = 3 * ATT_GW
MEM_HEADS = 4
MEM_HD = 128
MEM_W = 512
N_MEM = 256
FFN_H = 2816
NUM_BUCKETS = 32
MAX_DISTANCE = 1024
RMS_EPS = 1e-6
LN_EPS = 1e-5
NEG_INF = -1e30
SEG = (2 * CONV_W, 3 * ATT_W, MEM_W, 3 * D)
ADAM_LR, ADAM_B1, ADAM_B2, ADAM_EPS, ADAM_WD, ADAM_STEP = 0.001, 0.9, 0.999, 1e-08, 0.01, 10

VMEM_LIMIT_V7X = 56 * 1024 * 1024
MESH = pl.DeviceIdType.MESH


def _cp(n_axes):
    return pltpu.CompilerParams(dimension_semantics=("arbitrary",) * n_axes, vmem_limit_bytes=VMEM_LIMIT_V7X)


def _row(v):
    return v.reshape(1, -1)


def _mm(a, b, name, *, ta=False, tb=False, out_dtype=BF16, tm=2048, tn=1024, tk=2048):
    kdim, m = a.shape if ta else a.shape[::-1]
    n, kb = b.shape if tb else b.shape[::-1]
    assert kb == kdim and a.dtype == BF16 and b.dtype == BF16
    tm, tn, tk = min(tm, m), min(tn, n), min(tk, kdim)
    assert m % tm == 0 and n % tn == 0 and kdim % tk == 0, (name, m, n, kdim, tm, tn, tk)
    nk = kdim // tk
    a_spec = pl.BlockSpec((tk, tm), lambda i, j, k: (k, i)) if ta else pl.BlockSpec((tm, tk), lambda i, j, k: (i, k))
    b_spec = pl.BlockSpec((tn, tk), lambda i, j, k: (j, k)) if tb else pl.BlockSpec((tk, tn), lambda i, j, k: (k, j))
    o_spec = pl.BlockSpec((tm, tn), lambda i, j, k: (i, j))
    dims = (((0 if ta else 1,), (1 if tb else 0,)), ((), ()))
    use_scratch = nk > 1 and out_dtype != F32

    def body(*refs):
        a_ref, b_ref, o_ref = refs[:3]
        p = lax.dot_general(a_ref[...], b_ref[...], dims, preferred_element_type=F32)
        if nk == 1:
            o_ref[...] = p.astype(o_ref.dtype)
            return
        k = pl.program_id(2)
        t_ref = refs[3] if use_scratch else o_ref

        @pl.when(k == 0)
        def _():
            t_ref[...] = p

        @pl.when(k > 0)
        def _():
            t_ref[...] += p

        if use_scratch:
            @pl.when(k == nk - 1)
            def _():
                o_ref[...] = t_ref[...].astype(o_ref.dtype)

    return pl.pallas_call(
        body, name=name, grid=(m // tm, n // tn, nk), in_specs=[a_spec, b_spec], out_specs=o_spec,
        out_shape=jax.ShapeDtypeStruct((m, n), out_dtype),
        scratch_shapes=[pltpu.VMEM((tm, tn), F32)] if use_scratch else [],
        compiler_params=_cp(3),
    )(a, b)


EPILOGUE_COLS = 512
EPILOGUE_ROWS = 512


def _chunks(size, step):
    return [(c, min(c + step, size)) for c in range(0, size, step)]


def _rms_bwd_val(v, g, dy):
    r = lax.rsqrt(jnp.mean(v * v, axis=-1, keepdims=True) + RMS_EPS)
    vhat = v * r
    dvh = dy * g
    dv = r * (dvh - vhat * jnp.mean(dvh * vhat, axis=-1, keepdims=True))
    return dv, jnp.sum(dy * vhat, axis=0, keepdims=True)


def _mm_sum(pairs, name, *, norm=None, tm=1024):
    m, n = pairs[0][0].shape[0], pairs[0][1].shape[1]
    tm = min(tm, m)
    starts, specs, total = [], [], 0
    for a, b, tk in pairs:
        assert a.shape == (m, b.shape[0]) and b.shape[1] == n and a.shape[1] % tk == 0 and a.dtype == b.dtype == BF16
        nk = a.shape[1] // tk
        starts.append((total, nk))
        step = functools.partial(lambda k, s0, nk_: jnp.clip(k - s0, 0, nk_ - 1), s0=total, nk_=nk)
        specs.append(pl.BlockSpec((tm, tk), functools.partial(lambda i, k, st: (i, st(k)), st=step)))
        specs.append(pl.BlockSpec((tk, n), functools.partial(lambda i, k, st: (st(k), 0), st=step)))
        total += nk
    np_ = 2 * len(pairs)
    row = pl.BlockSpec((tm, n), lambda i, k: (i, 0))

    def accumulate(refs, acc, k, upto=total):
        for p, (s0, nk) in enumerate(starts):
            @pl.when((k >= s0) & (k < min(s0 + nk, upto)))
            def _(p=p, s0=s0):
                part = jnp.dot(refs[2 * p][...], refs[2 * p + 1][...], preferred_element_type=F32)
                if s0 == 0:
                    @pl.when(k == 0)
                    def _():
                        acc[...] = part

                    @pl.when(k > 0)
                    def _():
                        acc[...] += part
                else:
                    acc[...] += part

    operands = [op for a, b, _ in pairs for op in (a, b)]
    if norm is None:
        def plain(*refs):
            k = pl.program_id(1)
            accumulate(refs, refs[np_ + 1], k)

            @pl.when(k == total - 1)
            def _():
                refs[np_][...] = refs[np_ + 1][...].astype(BF16)

        return pl.pallas_call(
            plain, name=name, grid=(m // tm, total), in_specs=specs, out_specs=row,
            out_shape=jax.ShapeDtypeStruct((m, n), BF16), scratch_shapes=[pltpu.VMEM((tm, n), F32)],
            compiler_params=_cp(2),
        )(*operands)

    v, g, dres = norm

    def body(*refs):
        v_ref, g_ref, dres_ref, dv_ref, dg_ref, acc = refs[np_:np_ + 6]
        i, k = pl.program_id(0), pl.program_id(1)
        accumulate(refs, acc, k, upto=total - 1)

        @pl.when((i == 0) & (k == 0))
        def _():
            dg_ref[...] = jnp.zeros_like(dg_ref)

        @pl.when(k == total - 1)
        def _():
            a_ref, b_ref = refs[np_ - 2], refs[np_ - 1]
            for r0, r1 in _chunks(tm, EPILOGUE_ROWS):
                dy = jnp.dot(a_ref[r0:r1, :], b_ref[...], preferred_element_type=F32)
                if total > 1:
                    dy = dy + acc[r0:r1, :]
                dv, dg = _rms_bwd_val(v_ref[r0:r1, :], g_ref[...], dy)
                dv_ref[r0:r1, :] = dv + dres_ref[r0:r1, :]
                dg_ref[0:1, :] += dg

    once = pl.BlockSpec((tm, n), lambda i, k: (i, 0), pipeline_mode=pl.Buffered(1))
    dv, dg = pl.pallas_call(
        body, name=name, grid=(m // tm, total),
        in_specs=specs + [once, pl.BlockSpec((1, n), lambda i, k: (0, 0)), once],
        out_specs=[row, pl.BlockSpec((8, n), lambda i, k: (0, 0))],
        out_shape=[jax.ShapeDtypeStruct((m, n), F32), jax.ShapeDtypeStruct((8, n), F32)],
        scratch_shapes=[pltpu.VMEM((tm, n), F32)], compiler_params=_cp(2),
    )(*operands, v, _row(g), dres)
    return dv, dg[0]


def _mm_swiglu_fwd(h, w, name, *, tm=1024, tn=1408):
    m, kdim = h.shape
    hid = w.shape[0] // 2
    tm = min(tm, m)
    nj = hid // tn
    dims = (((1,), (1,)), ((), ()))

    def body(h_ref, wg_ref, wu_ref, g_ref, u_ref, a_ref):
        hv = h_ref[...]
        for c0, c1 in _chunks(tn, EPILOGUE_COLS):
            g = lax.dot_general(hv, wg_ref[c0:c1, :], dims, preferred_element_type=F32)
            u = lax.dot_general(hv, wu_ref[c0:c1, :], dims, preferred_element_type=F32)
            g_ref[:, c0:c1] = g.astype(BF16)
            u_ref[:, c0:c1] = u.astype(BF16)
            a_ref[:, c0:c1] = (g * _sigmoid(g) * u).astype(BF16)

    out = pl.BlockSpec((tm, tn), lambda i, j: (i, j))
    return pl.pallas_call(
        body, name=name, grid=(m // tm, nj),
        in_specs=[pl.BlockSpec((tm, kdim), lambda i, j: (i, 0)), pl.BlockSpec((tn, kdim), lambda i, j: (j, 0)),
                  pl.BlockSpec((tn, kdim), lambda i, j: (j + nj, 0))],
        out_specs=[out, out, out], out_shape=[jax.ShapeDtypeStruct((m, hid), BF16)] * 3, compiler_params=_cp(2),
    )(h, w, w)


def _mm_swiglu_bwd(df, w, g, u, name, *, tm=1024, tn=1408):
    m, kdim = df.shape
    hid = w.shape[0]
    tm = min(tm, m)

    def body(df_ref, w_ref, g_ref, u_ref, dg_ref, du_ref):
        dfv = df_ref[...]
        for c0, c1 in _chunks(tn, EPILOGUE_COLS):
            da = lax.dot_general(dfv, w_ref[c0:c1, :], (((1,), (1,)), ((), ())), preferred_element_type=F32)
            gg, uu, dab = g_ref[:, c0:c1], u_ref[:, c0:c1], da.astype(BF16)
            sg = _sigmoid(gg)
            dg_ref[:, c0:c1] = dab * uu * (sg * (1.0 + gg * (1.0 - sg)))
            du_ref[:, c0:c1] = dab * gg * sg

    blk = pl.BlockSpec((tm, tn), lambda i, j: (i, j))
    return pl.pallas_call(
        body, name=name, grid=(m // tm, hid // tn),
        in_specs=[pl.BlockSpec((tm, kdim), lambda i, j: (i, 0)), pl.BlockSpec((tn, kdim), lambda i, j: (j, 0)), blk, blk],
        out_specs=[blk, blk], out_shape=[jax.ShapeDtypeStruct((m, hid), BF16)] * 2, compiler_params=_cp(2),
    )(df, w, g, u)


def _mm_norm_res(a, b, xres, g, g_next, name, *, tm=1024):
    m, kdim = a.shape
    n = b.shape[1]
    tm = min(tm, m)
    two = g_next is not None

    def body(*refs):
        a_ref, b_ref, x_ref, g_ref = refs[:4]
        y_ref, xn_ref = refs[4 + two], refs[5 + two]
        for r0, r1 in _chunks(tm, EPILOGUE_ROWS):
            y = jnp.dot(a_ref[r0:r1, :], b_ref[...], preferred_element_type=F32)
            y_ref[r0:r1, :] = y.astype(BF16)
            xn = x_ref[r0:r1, :] + _rms_val(y, g_ref[...])
            xn_ref[r0:r1, :] = xn
            if two:
                refs[7][r0:r1, :] = _rms_val(xn, refs[4][...]).astype(BF16)

    row = pl.BlockSpec((tm, n), lambda i: (i, 0))
    vec = pl.BlockSpec((1, n), lambda i: (0, 0))
    outs = pl.pallas_call(
        body, name=name, grid=(m // tm,),
        in_specs=[pl.BlockSpec((tm, kdim), lambda i: (i, 0)), pl.BlockSpec((kdim, n), lambda i: (0, 0)), row, vec] + ([vec] if two else []),
        out_specs=[row, row] + ([row] if two else []),
        out_shape=[jax.ShapeDtypeStruct((m, n), BF16), jax.ShapeDtypeStruct((m, n), F32)]
        + ([jax.ShapeDtypeStruct((m, n), BF16)] if two else []),
        compiler_params=_cp(1),
    )(*((a, b, xres, _row(g)) + ((_row(g_next),) if two else ())))
    return (outs[0], outs[1], outs[2]) if two else (outs[0], outs[1], None)


def _rms_val(v, g):
    return v * lax.rsqrt(jnp.mean(v * v, axis=-1, keepdims=True) + RMS_EPS) * g


def _norm_plain(v, g, name, tr=512):
    t, d = v.shape
    tr = min(tr, t)

    def body(v_ref, g_ref, o_ref):
        o_ref[...] = _rms_val(v_ref[...], g_ref[...]).astype(BF16)

    return pl.pallas_call(
        body, name=name, grid=(t // tr,),
        in_specs=[pl.BlockSpec((tr, d), lambda i: (i, 0)), pl.BlockSpec((1, d), lambda i: (0, 0))],
        out_specs=pl.BlockSpec((tr, d), lambda i: (i, 0)),
        out_shape=jax.ShapeDtypeStruct((t, d), BF16), compiler_params=_cp(1),
    )(v, _row(g))


def _norm_bwd(v, g, dout, dres, out_dtype, name, tr=1024):
    t, d = v.shape
    tr = min(tr, t)
    nt = t // tr
    has_res = dres is not None

    def body(*refs):
        v_ref, g_ref, do_ref = refs[:3]
        dv_ref, dg_ref = refs[3 + has_res], refs[4 + has_res]
        i = pl.program_id(0)
        vv = v_ref[...].astype(F32)
        dy = do_ref[...].astype(F32)
        r = lax.rsqrt(jnp.mean(vv * vv, axis=-1, keepdims=True) + RMS_EPS)
        vhat = vv * r
        dvh = dy * g_ref[...]
        dv = r * (dvh - vhat * jnp.mean(dvh * vhat, axis=-1, keepdims=True))
        if has_res:
            dv = dv + refs[3][...]
        dv_ref[...] = dv.astype(dv_ref.dtype)
        part = jnp.sum(dy * vhat, axis=0, keepdims=True)

        @pl.when(i == 0)
        def _():
            dg_ref[...] = jnp.zeros_like(dg_ref)

        dg_ref[0:1, :] += part

    row = pl.BlockSpec((tr, d), lambda i: (i, 0))
    dv, dg = pl.pallas_call(
        body, name=name, grid=(nt,),
        in_specs=[row, pl.BlockSpec((1, d), lambda i: (0, 0)), row] + ([row] if has_res else []),
        out_specs=[row, pl.BlockSpec((8, d), lambda i: (0, 0))],
        out_shape=[jax.ShapeDtypeStruct((t, d), out_dtype), jax.ShapeDtypeStruct((8, d), F32)],
        compiler_params=_cp(1),
    )(*((v, _row(g), dout) + ((dres,) if has_res else ())))
    return dv, dg[0]


def _norm_bwd_pair(v1, g1, dout, dres, v2, g2, name, tr=1024):
    t, d = v1.shape
    tr = min(tr, t)

    def body(v1_ref, g1_ref, do_ref, dr_ref, v2_ref, g2_ref, d1_ref, d2_ref, dg1_ref, dg2_ref):
        @pl.when(pl.program_id(0) == 0)
        def _():
            dg1_ref[...] = jnp.zeros_like(dg1_ref)
            dg2_ref[...] = jnp.zeros_like(dg2_ref)

        dv1, dg1 = _rms_bwd_val(v1_ref[...].astype(F32), g1_ref[...], do_ref[...].astype(F32))
        d1 = dv1 + dr_ref[...]
        d1_ref[...] = d1
        dv2, dg2 = _rms_bwd_val(v2_ref[...].astype(F32), g2_ref[...], d1)
        d2_ref[...] = dv2.astype(BF16)
        dg1_ref[0:1, :] += dg1
        dg2_ref[0:1, :] += dg2

    row = pl.BlockSpec((tr, d), lambda i: (i, 0))
    vec = pl.BlockSpec((1, d), lambda i: (0, 0))
    acc = pl.BlockSpec((8, d), lambda i: (0, 0))
    d1, d2, dg1, dg2 = pl.pallas_call(
        body, name=name, grid=(t // tr,), in_specs=[row, vec, row, row, row, vec], out_specs=[row, row, acc, acc],
        out_shape=[jax.ShapeDtypeStruct((t, d), F32), jax.ShapeDtypeStruct((t, d), BF16),
                   jax.ShapeDtypeStruct((8, d), F32), jax.ShapeDtypeStruct((8, d), F32)],
        compiler_params=_cp(1),
    )(v1, _row(g1), dout, dres, v2, _row(g2))
    return d1, d2, dg1[0], dg2[0]


def _loss_head(y, target, name, tr=1024):
    t, d = y.shape
    tr = min(tr, t)
    nt = t // tr

    def body(y_ref, t_ref, dy_ref, l_ref):
        i = pl.program_id(0)
        err = y_ref[...] - t_ref[...]
        dy_ref[...] = err * (1.0 / d)

        @pl.when(i == 0)
        def _():
            l_ref[...] = jnp.zeros_like(l_ref)

        l_ref[...] += jnp.sum(err * err) * (0.5 / d)

    row = pl.BlockSpec((tr, d), lambda i: (i, 0))
    dy, l = pl.pallas_call(
        body, name=name, grid=(nt,), in_specs=[row, row],
        out_specs=[row, pl.BlockSpec((8, 128), lambda i: (0, 0))],
        out_shape=[jax.ShapeDtypeStruct((t, d), F32), jax.ShapeDtypeStruct((8, 128), F32)],
        compiler_params=_cp(1),
    )(y, target)
    return l[0, 0], dy


def _sigmoid(v):
    return 1.0 / (1.0 + jnp.exp(-v))


def _halo_specs(tq, width, nt, halo):
    per = tq // halo
    last = nt * per - 1
    main = pl.BlockSpec((tq, width), lambda i: (i, 0))
    prev = pl.BlockSpec((halo, width), lambda i: (jnp.maximum(i * per - 1, 0), 0))
    nxt = pl.BlockSpec((halo, width), lambda i: (jnp.minimum((i + 1) * per, last), 0))
    return prev, main, nxt


def _glu_window(zp, zm, zn, i, nt):
    def glu(z):
        z = z.astype(F32)
        return z[:, :CONV_W] * _sigmoid(z[:, CONV_W:])

    up = jnp.where(i > 0, glu(zp), 0.0)
    un = jnp.where(i < nt - 1, glu(zn), 0.0)
    return jnp.concatenate([up, glu(zm), un], axis=0)


def _shifted(win, shift, rows):
    if shift == 0:
        return win[:rows]
    return pltpu.roll(win, win.shape[0] - shift, 0)[:rows]


def _conv_fwd(zc, w, b, ln_g, ln_b, name, tq=512):
    t = zc.shape[0]
    tq = min(tq, t)
    nt = t // tq

    def body(zp_ref, zm_ref, zn_ref, w_ref, b_ref, g_ref, bb_ref, y_ref, c_ref):
        i = pl.program_id(0)
        win = _glu_window(zp_ref[...], zm_ref[...], zn_ref[...], i, nt)
        wv = w_ref[...]
        y = jnp.zeros((tq, CONV_W), F32) + b_ref[...]
        for k in range(CONV_K):
            y = y + _shifted(win, k + 1, tq) * wv[k:k + 1, :]
        y_ref[...] = y
        mu = jnp.mean(y, axis=-1, keepdims=True)
        yc = y - mu
        ln = yc * lax.rsqrt(jnp.mean(yc * yc, axis=-1, keepdims=True) + LN_EPS) * g_ref[...] + bb_ref[...]
        c_ref[...] = (ln * _sigmoid(ln)).astype(BF16)

    vec = pl.BlockSpec((1, CONV_W), lambda i: (0, 0))
    out = pl.BlockSpec((tq, CONV_W), lambda i: (i, 0))
    return pl.pallas_call(
        body, name=name, grid=(nt,),
        in_specs=[*_halo_specs(tq, 2 * CONV_W, nt, CONV_PAD), pl.BlockSpec((32, CONV_W), lambda i: (0, 0)), vec, vec, vec],
        out_specs=[out, out],
        out_shape=[jax.ShapeDtypeStruct((t, CONV_W), F32), jax.ShapeDtypeStruct((t, CONV_W), BF16)],
        compiler_params=_cp(1),
    )(zc, zc, zc, w, _row(b), _row(ln_g), _row(ln_b))


def _conv_bwd(zc, y, dc, w, ln_g, ln_b, name, tq=256):
    t = zc.shape[0]
    tq = min(tq, t)
    nt = t // tq
    rows = tq + 2 * CONV_PAD

    def body(zp_ref, zm_ref, zn_ref, yp_ref, ym_ref, yn_ref, dp_ref, dm_ref, dn_ref, w_ref, g_ref, bb_ref,
             dz_ref, dw_ref, db_ref, dg_ref, dbb_ref):
        i = pl.program_id(0)
        uwin = _glu_window(zp_ref[...], zm_ref[...], zn_ref[...], i, nt)
        ywin = jnp.concatenate([yp_ref[...], ym_ref[...], yn_ref[...]], axis=0)
        dcw = jnp.concatenate([dp_ref[...], dm_ref[...], dn_ref[...]], axis=0).astype(F32)
        mu = jnp.mean(ywin, axis=-1, keepdims=True)
        yc = ywin - mu
        rstd = lax.rsqrt(jnp.mean(yc * yc, axis=-1, keepdims=True) + LN_EPS)
        yhat = yc * rstd
        ln = yhat * g_ref[...] + bb_ref[...]
        sg = _sigmoid(ln)
        dl = dcw * (sg * (1.0 + ln * (1.0 - sg)))
        ridx = lax.broadcasted_iota(jnp.int32, (rows, 1), 0)
        inside = ((ridx >= CONV_PAD) | (i > 0)) & ((ridx < tq + CONV_PAD) | (i < nt - 1))
        dl = jnp.where(inside, dl, 0.0)
        dyh = dl * g_ref[...]
        dy = rstd * (dyh - jnp.mean(dyh, axis=-1, keepdims=True) - yhat * jnp.mean(dyh * yhat, axis=-1, keepdims=True))
        dy = jnp.where(inside, dy, 0.0)
        main = slice(CONV_PAD, CONV_PAD + tq)
        dlm, yhm, dym = dl[main], yhat[main], dy[main]

        @pl.when(i == 0)
        def _():
            dw_ref[...] = jnp.zeros_like(dw_ref)
            db_ref[...] = jnp.zeros_like(db_ref)
            dg_ref[...] = jnp.zeros_like(dg_ref)
            dbb_ref[...] = jnp.zeros_like(dbb_ref)

        dg_ref[0:1, :] += jnp.sum(dlm * yhm, axis=0, keepdims=True)
        dbb_ref[0:1, :] += jnp.sum(dlm, axis=0, keepdims=True)
        db_ref[0:1, :] += jnp.sum(dym, axis=0, keepdims=True)
        wv = w_ref[...]
        du = jnp.zeros((tq, CONV_W), F32)
        for k in range(CONV_K):
            du = du + _shifted(dy, 2 * CONV_PAD - 1 - k, tq) * wv[k:k + 1, :]
            dw_ref[k:k + 1, :] += jnp.sum(dym * _shifted(uwin, k + 1, tq), axis=0, keepdims=True)
        zm = zm_ref[...].astype(F32)
        a, gt = zm[:, :CONV_W], zm[:, CONV_W:]
        sgt = _sigmoid(gt)
        dz_ref[:, :CONV_W] = (du * sgt).astype(BF16)
        dz_ref[:, CONV_W:] = (du * a * sgt * (1.0 - sgt)).astype(BF16)

    vec = pl.BlockSpec((1, CONV_W), lambda i: (0, 0))
    acc = pl.BlockSpec((8, CONV_W), lambda i: (0, 0))
    dz, dw, db, dg, dbb = pl.pallas_call(
        body, name=name, grid=(nt,),
        in_specs=[*_halo_specs(tq, 2 * CONV_W, nt, CONV_PAD), *_halo_specs(tq, CONV_W, nt, CONV_PAD),
                  *_halo_specs(tq, CONV_W, nt, CONV_PAD), pl.BlockSpec((32, CONV_W), lambda i: (0, 0)), vec, vec],
        out_specs=[pl.BlockSpec((tq, 2 * CONV_W), lambda i: (i, 0)), pl.BlockSpec((32, CONV_W), lambda i: (0, 0)), acc, acc, acc],
        out_shape=[jax.ShapeDtypeStruct((t, 2 * CONV_W), BF16), jax.ShapeDtypeStruct((32, CONV_W), F32),
                   jax.ShapeDtypeStruct((8, CONV_W), F32), jax.ShapeDtypeStruct((8, CONV_W), F32),
                   jax.ShapeDtypeStruct((8, CONV_W), F32)],
        compiler_params=_cp(1),
    )(zc, zc, zc, y, y, y, dc, dc, dc, w, _row(ln_g), _row(ln_b))
    return dz, dw, db[0], dg[0], dbb[0]


def _att_tiles(seq_len):
    tq = min(512, seq_len)
    return tq, min(128, tq)


def _t5_bucket_np(rel):
    nb = NUM_BUCKETS // 2
    max_exact = nb // 2
    ret = np.where(rel > 0, nb, 0)
    n = np.abs(rel)
    nf = np.maximum(n, 1).astype(np.float32)
    large = max_exact + (np.log(nf / np.float32(max_exact)) / np.float32(np.log(MAX_DISTANCE / max_exact))
                         * np.float32(nb - max_exact)).astype(np.int32)
    large = np.minimum(large, nb - 1)
    return ret + np.where(n < max_exact, n, large)


def _bucket_index(sb, dilation):
    off = np.arange(sb + 2 * ATT_RADIUS)[None, :] - ATT_RADIUS - np.arange(sb)[:, None]
    idx = _t5_bucket_np(off * dilation).astype(np.int32)
    return np.where(np.abs(off) <= ATT_RADIUS, idx, -1).astype(np.int32)


def _bias_table(tab, idx, name):
    sb, w = idx.shape

    def body(tab_ref, idx_ref, o_ref):
        ix = idx_ref[...]
        for h in range(HEADS):
            acc = jnp.full((sb, w), NEG_INF, F32)
            for b in range(NUM_BUCKETS):
                acc = jnp.where(ix == b, tab_ref[b, h], acc)
            o_ref[h] = acc

    return pl.pallas_call(
        body, name=name, out_shape=jax.ShapeDtypeStruct((HEADS, sb, w), F32),
        in_specs=[pl.BlockSpec(memory_space=pltpu.SMEM), pl.BlockSpec(memory_space=pltpu.VMEM)],
        out_specs=pl.BlockSpec(memory_space=pltpu.VMEM),
    )(tab, idx)


def _qkv_specs(tq, per, last_blk, cols, tile):
    def main(col):
        return pl.BlockSpec((tq, ATT_GW), lambda *g: (tile(*g), col))

    def prev(col):
        return pl.BlockSpec((ATT_RADIUS, ATT_GW), lambda *g: (jnp.maximum(tile(*g) * per - 1, 0), col))

    def nxt(col):
        return pl.BlockSpec((ATT_RADIUS, ATT_GW), lambda *g: (jnp.minimum((tile(*g) + 1) * per, last_blk), col))

    cq, ck, cv = cols
    return [main(cq), prev(ck), main(ck), nxt(ck), prev(cv), main(cv), nxt(cv)]


def _att_fwd(q, k, v, bias, seq_len, name, cols=(0, 0, 0)):
    t = q.shape[0]
    tq, sb = _att_tiles(seq_len)
    nbl = seq_len // tq
    nt = t // tq
    per = tq // ATT_RADIUS
    w = sb + 2 * ATT_RADIUS
    last_blk = t // ATT_RADIUS - 1

    def body(q_ref, kp_ref, kc_ref, kn_ref, vp_ref, vc_ref, vn_ref, b_ref, o_ref, l_ref):
        n = pl.program_id(0) % nbl
        kw = jnp.concatenate([kp_ref[...], kc_ref[...], kn_ref[...]], axis=0)
        vw = jnp.concatenate([vp_ref[...], vc_ref[...], vn_ref[...]], axis=0)
        kpos = n * tq - ATT_RADIUS + lax.broadcasted_iota(jnp.int32, (1, tq + 2 * ATT_RADIUS), 1)
        valid = (kpos >= 0) & (kpos < seq_len)
        for j in range(tq // sb):
            qj = q_ref[j * sb:(j + 1) * sb, :]
            kj, vj, okj = kw[j * sb:j * sb + w], vw[j * sb:j * sb + w], valid[:, j * sb:j * sb + w]
            outs, lses = [], []
            for h in range(HEADS):
                hs = slice(h * HEAD_DIM, (h + 1) * HEAD_DIM)
                s = lax.dot_general(qj[:, hs], kj[:, hs], NT, preferred_element_type=F32)
                s = jnp.where(okj, s * (HEAD_DIM ** -0.5) + b_ref[h], NEG_INF)
                m = jnp.max(s, axis=-1, keepdims=True)
                e = jnp.exp(s - m)
                den = jnp.sum(e, axis=-1, keepdims=True)
                outs.append(jnp.dot(e.astype(BF16), vj[:, hs], preferred_element_type=F32) / den)
                lses.append(jnp.broadcast_to(m + jnp.log(den), (sb, HEAD_DIM)))
            o_ref[j * sb:(j + 1) * sb, :] = jnp.concatenate(outs, axis=1).astype(BF16)
            l_ref[j * sb:(j + 1) * sb, :] = jnp.concatenate(lses, axis=1)

    main = pl.BlockSpec((tq, ATT_GW), lambda i: (i, 0))
    return pl.pallas_call(
        body, name=name, grid=(nt,),
        in_specs=[*_qkv_specs(tq, per, last_blk, cols, lambda i: i), pl.BlockSpec((HEADS, sb, w), lambda i: (0, 0, 0))],
        out_specs=[main, main],
        out_shape=[jax.ShapeDtypeStruct((t, ATT_GW), BF16), jax.ShapeDtypeStruct((t, ATT_GW), F32)],
        compiler_params=_cp(1),
    )(q, k, k, k, v, v, v, bias)


PERM_ROWS = 2048
LANES = 128


def _spec_rm(d, tr, width, col=0):
    return pl.BlockSpec((d, tr // d, width), lambda i: (0, i, col))


def _rm_view(a, d):
    return a.reshape(d, a.shape[0] // d, a.shape[1])


def _gather_residues(scr, val, d, out_ref, col0):
    tr, w = val.shape
    for c in range(w // LANES):
        scr[...] = val[:, c * LANES:(c + 1) * LANES].astype(F32)
        for r in range(d):
            out_ref[r, :, col0 + c * LANES:col0 + (c + 1) * LANES] = scr[pl.ds(r, tr // d, stride=d), :].astype(out_ref.dtype)


def _scatter_residues(scr, ref, d):
    w = ref.shape[2]
    cols = []
    for c in range(w // LANES):
        for r in range(d):
            scr[pl.ds(r, scr.shape[0] // d, stride=d), :] = ref[r, :, c * LANES:(c + 1) * LANES].astype(F32)
        cols.append(scr[...])
    return jnp.concatenate(cols, axis=1)


def _to_residue_major(srcs, d, name, tr=PERM_ROWS):
    t = srcs[0][0].shape[0]
    tr = min(tr, t)
    n = len(srcs)

    def body(*refs):
        scr = refs[2 * n]
        for k in range(n):
            _gather_residues(scr, refs[k][...], d, refs[n + k], 0)

    outs = pl.pallas_call(
        body, name=name, grid=(t // tr,),
        in_specs=[pl.BlockSpec((tr, ATT_GW), functools.partial(lambda i, col: (i, col), col=col)) for _, col in srcs],
        out_specs=[_spec_rm(d, tr, ATT_GW)] * n,
        out_shape=[jax.ShapeDtypeStruct((d, t // d, ATT_GW), a.dtype) for a, _ in srcs],
        scratch_shapes=[pltpu.VMEM((tr, LANES), F32)], compiler_params=_cp(1),
    )(*[a for a, _ in srcs])
    return [o.reshape(t, ATT_GW) for o in outs]


def _att_combine(outs, lses, name, tr=PERM_ROWS):
    t = outs[0].shape[0]
    tr = min(tr, t)
    dils = [d for _, d in ATT_PATTERNS]

    def body(o0, o1, o2, l0, l1, l2, a_ref, lse_ref, scr):
        os_, ls_ = [], []
        for o_ref, l_ref, d in zip((o0, o1, o2), (l0, l1, l2), dils):
            if d == 1:
                os_.append(o_ref[...].astype(F32))
                ls_.append(l_ref[...])
            else:
                os_.append(_scatter_residues(scr, o_ref, d))
                ls_.append(_scatter_residues(scr, l_ref, d))
        la, lb, lc = ls_
        m = jnp.maximum(jnp.maximum(la, lb), lc)
        ea, eb, ec = jnp.exp(la - m), jnp.exp(lb - m), jnp.exp(lc - m)
        den = ea + eb + ec
        a_ref[...] = ((ea * os_[0] + eb * os_[1] + ec * os_[2]) / den).astype(BF16)
        lse_ref[...] = m + jnp.log(den)

    row = pl.BlockSpec((tr, ATT_GW), lambda i: (i, 0))
    specs = [row if d == 1 else _spec_rm(d, tr, ATT_GW) for d in dils]
    views = lambda arrs: [a if d == 1 else _rm_view(a, d) for a, d in zip(arrs, dils)]
    return pl.pallas_call(
        body, name=name, grid=(t // tr,), in_specs=specs * 2, out_specs=[row, row],
        out_shape=[jax.ShapeDtypeStruct((t, ATT_GW), BF16), jax.ShapeDtypeStruct((t, ATT_GW), F32)],
        scratch_shapes=[pltpu.VMEM((tr, LANES), F32)], compiler_params=_cp(1),
    )(*views(outs), *views(lses))


def _assemble_dza(dqkv, name, tr=PERM_ROWS):
    t = dqkv[0][0].shape[0]
    tr = min(tr, t)
    dils = [d for _, d in ATT_PATTERNS]

    def body(*refs):
        o_ref, scr = refs[9], refs[10]
        for g, d in enumerate(dils):
            for c in range(3):
                ref = refs[g * 3 + c]
                val = ref[...] if d == 1 else _scatter_residues(scr, ref, d).astype(BF16)
                o_ref[:, (c * 3 + g) * ATT_GW:(c * 3 + g + 1) * ATT_GW] = val

    row = pl.BlockSpec((tr, ATT_GW), lambda i: (i, 0))
    specs = [row if d == 1 else _spec_rm(d, tr, ATT_GW) for d in dils for _ in range(3)]
    args = [a if d == 1 else _rm_view(a, d) for trio, d in zip(dqkv, dils) for a in trio]
    return pl.pallas_call(
        body, name=name, grid=(t // tr,), in_specs=specs, out_specs=pl.BlockSpec((tr, 3 * ATT_W), lambda i: (i, 0)),
        out_shape=jax.ShapeDtypeStruct((t, 3 * ATT_W), BF16),
        scratch_shapes=[pltpu.VMEM((tr, LANES), F32)], compiler_params=_cp(1),
    )(*args)


def _att_bwd(q, k, v, bias, delta, do, lse, seq_len, name, cols=(0, 0, 0)):
    t = q.shape[0]
    tq, sb = _att_tiles(seq_len)
    nbl = seq_len // tq
    n_res = t // seq_len
    per = tq // ATT_RADIUS
    w = sb + 2 * ATT_RADIUS
    last_blk = t // ATT_RADIUS - 1
    acc_cols = 2 * tq - sb + w
    scale = HEAD_DIM ** -0.5

    def body(q_ref, kp_ref, kc_ref, kn_ref, vp_ref, vc_ref, vn_ref, b_ref, dl_ref, do_ref, l_ref,
             dq_ref, dk_ref, dv_ref, db_ref, ak_ref, av_ref):
        r, n = pl.program_id(0), pl.program_id(1)

        @pl.when((r == 0) & (n == 0))
        def _():
            db_ref[...] = jnp.zeros_like(db_ref)

        @pl.when(n == 0)
        def _():
            ak_ref[...] = jnp.zeros_like(ak_ref)
            av_ref[...] = jnp.zeros_like(av_ref)

        @pl.when(n < nbl)
        def _():
            kw = jnp.concatenate([kp_ref[...], kc_ref[...], kn_ref[...]], axis=0)
            vw = jnp.concatenate([vp_ref[...], vc_ref[...], vn_ref[...]], axis=0)
            kpos = n * tq - ATT_RADIUS + lax.broadcasted_iota(jnp.int32, (1, tq + 2 * ATT_RADIUS), 1)
            valid = (kpos >= 0) & (kpos < seq_len)
            for j in range(tq // sb):
                rows = slice(j * sb, (j + 1) * sb)
                qj, doj = q_ref[rows, :], do_ref[rows, :]
                dlj = dl_ref[rows, :]
                lj = l_ref[rows, :]
                kj, vj, okj = kw[j * sb:j * sb + w], vw[j * sb:j * sb + w], valid[:, j * sb:j * sb + w]
                heads = [slice(h * HEAD_DIM, (h + 1) * HEAD_DIM) for h in range(HEADS)]
                ss = [lax.dot_general(qj[:, hs], kj[:, hs], NT, preferred_element_type=F32) for hs in heads]
                dps = [lax.dot_general(doj[:, hs], vj[:, hs], NT, preferred_element_type=F32) for hs in heads]
                ps, dss, dsbs = [], [], []
                for h, hs in enumerate(heads):
                    s = jnp.where(okj, ss[h] * scale + b_ref[h], NEG_INF)
                    p = jnp.exp(s - lj[:, h * HEAD_DIM:h * HEAD_DIM + 1])
                    ds = p * (dps[h] - dlj[:, h * HEAD_DIM:h * HEAD_DIM + 1])
                    ps.append(p.astype(BF16))
                    dss.append(ds)
                    dsbs.append(ds.astype(BF16))
                dqs = [jnp.dot(dsbs[h], kj[:, hs], preferred_element_type=F32) * scale for h, hs in enumerate(heads)]
                dks = [lax.dot_general(qj[:, hs], dsbs[h], (((0,), (0,)), ((), ())), preferred_element_type=F32) * scale
                       for h, hs in enumerate(heads)]
                dvs = [lax.dot_general(doj[:, hs], ps[h], (((0,), (0,)), ((), ())), preferred_element_type=F32)
                       for h, hs in enumerate(heads)]
                dq_ref[rows, :] = jnp.concatenate(dqs, axis=1).astype(BF16)
                c0 = tq + j * sb
                ak_ref[:, c0:c0 + w] += jnp.concatenate(dks, axis=0)
                av_ref[:, c0:c0 + w] += jnp.concatenate(dvs, axis=0)
                for h in range(HEADS):
                    db_ref[h] += dss[h]

        dk_ref[...] = ak_ref[:, ATT_RADIUS:ATT_RADIUS + tq].T.astype(BF16)
        dv_ref[...] = av_ref[:, ATT_RADIUS:ATT_RADIUS + tq].T.astype(BF16)
        keep = acc_cols - tq
        nk, nv = ak_ref[:, tq:acc_cols], av_ref[:, tq:acc_cols]
        ak_ref[:, 0:keep] = nk
        av_ref[:, 0:keep] = nv
        ak_ref[:, keep:acc_cols] = jnp.zeros((ATT_GW, tq), F32)
        av_ref[:, keep:acc_cols] = jnp.zeros((ATT_GW, tq), F32)

    def tile(r, n):
        return r * nbl + jnp.minimum(n, nbl - 1)

    main = pl.BlockSpec((tq, ATT_GW), lambda r, n: (tile(r, n), 0))
    lag = pl.BlockSpec((tq, ATT_GW), lambda r, n: (r * nbl + jnp.maximum(n - 1, 0), 0))
    bspec = pl.BlockSpec((HEADS, sb, w), lambda r, n: (0, 0, 0))
    return pl.pallas_call(
        body, name=name, grid=(n_res, nbl + 1),
        in_specs=[*_qkv_specs(tq, per, last_blk, cols, tile), bspec, main, main, main],
        out_specs=[main, lag, lag, bspec],
        out_shape=[jax.ShapeDtypeStruct((t, ATT_GW), BF16)] * 3 + [jax.ShapeDtypeStruct((HEADS, sb, w), F32)],
        scratch_shapes=[pltpu.VMEM((ATT_GW, acc_cols), F32), pltpu.VMEM((ATT_GW, acc_cols), F32)],
        compiler_params=_cp(2),
    )(q, k, k, k, v, v, v, bias, delta, do, lse)


def _bias_grad(db, idx, name):
    _, sb, w = db.shape

    def body(db_ref, idx_ref, o_ref):
        ix = idx_ref[...]
        lane = lax.broadcasted_iota(jnp.int32, (1, 128), 1)
        rows = []
        for h in range(HEADS):
            d = db_ref[h]
            acc = jnp.zeros((1, 128), F32)
            for b in range(NUM_BUCKETS):
                acc = acc + jnp.where(lane == b, jnp.sum(jnp.where(ix == b, d, 0.0)), 0.0)
            rows.append(acc)
        o_ref[...] = jnp.concatenate(rows + [jnp.zeros((8 - HEADS, 128), F32)], axis=0)

    return pl.pallas_call(
        body, name=name, out_shape=jax.ShapeDtypeStruct((8, 128), F32),
        in_specs=[pl.BlockSpec(memory_space=pltpu.VMEM), pl.BlockSpec(memory_space=pltpu.VMEM)],
        out_specs=pl.BlockSpec(memory_space=pltpu.VMEM),
    )(db, idx)


def _mem_fwd(zq, kv, name, tq=512):
    t = zq.shape[0]
    tq = min(tq, t)
    scale = MEM_HD ** -0.5

    def body(q_ref, kv_ref, o_ref):
        outs = []
        for h in range(MEM_HEADS):
            hs = slice(h * MEM_HD, (h + 1) * MEM_HD)
            kh = kv_ref[:, h * MEM_HD:(h + 1) * MEM_HD]
            vh = kv_ref[:, MEM_W + h * MEM_HD:MEM_W + (h + 1) * MEM_HD]
            s = lax.dot_general(q_ref[:, hs], kh, (((1,), (1,)), ((), ())), preferred_element_type=F32) * scale
            e = jnp.exp(s - jnp.max(s, axis=-1, keepdims=True))
            den = jnp.sum(e, axis=-1, keepdims=True)
            outs.append(jnp.dot(e.astype(BF16), vh, preferred_element_type=F32) / den)
        o_ref[...] = jnp.concatenate(outs, axis=1).astype(BF16)

    row = pl.BlockSpec((tq, MEM_W), lambda i: (i, 0))
    return pl.pallas_call(
        body, name=name, grid=(t // tq,),
        in_specs=[row, pl.BlockSpec((N_MEM, 2 * MEM_W), lambda i: (0, 0))], out_specs=row,
        out_shape=jax.ShapeDtypeStruct((t, MEM_W), BF16), compiler_params=_cp(1),
    )(zq, kv)


def _mem_bwd(zq, kv, do, name, tq=512):
    t = zq.shape[0]
    tq = min(tq, t)
    nt = t // tq
    scale = MEM_HD ** -0.5

    def body(q_ref, kv_ref, do_ref, dq_ref, out_ref, dkv_ref):
        @pl.when(pl.program_id(0) == 0)
        def _():
            dkv_ref[...] = jnp.zeros_like(dkv_ref)

        ks = [slice(h * MEM_HD, (h + 1) * MEM_HD) for h in range(MEM_HEADS)]
        vs = [slice(MEM_W + h * MEM_HD, MEM_W + (h + 1) * MEM_HD) for h in range(MEM_HEADS)]
        ss = [lax.dot_general(q_ref[:, k], kv_ref[:, k], NT, preferred_element_type=F32) * scale for k in ks]
        dps = [lax.dot_general(do_ref[:, k], kv_ref[:, v], NT, preferred_element_type=F32) for k, v in zip(ks, vs)]
        pbs, dsbs = [], []
        for s, dp in zip(ss, dps):
            e = jnp.exp(s - jnp.max(s, axis=-1, keepdims=True))
            p = e / jnp.sum(e, axis=-1, keepdims=True)
            ds = p * (dp - jnp.sum(p * dp, axis=-1, keepdims=True))
            pbs.append(p.astype(BF16))
            dsbs.append(ds.astype(BF16))
        dq_ref[...] = jnp.concatenate([jnp.dot(dsb, kv_ref[:, k], preferred_element_type=F32) * scale
                                       for dsb, k in zip(dsbs, ks)], axis=1).astype(BF16)
        for dsb, pb, k, v in zip(dsbs, pbs, ks, vs):
            dkv_ref[k, :] += jnp.dot(q_ref[:, k].T, dsb, preferred_element_type=F32) * scale
            dkv_ref[v, :] += jnp.dot(do_ref[:, k].T, pb, preferred_element_type=F32)

        @pl.when(pl.program_id(0) == nt - 1)
        def _():
            out_ref[...] = dkv_ref[...].T

    row = pl.BlockSpec((tq, MEM_W), lambda i: (i, 0))
    full = pl.BlockSpec((N_MEM, 2 * MEM_W), lambda i: (0, 0))
    return pl.pallas_call(
        body, name=name, grid=(nt,), in_specs=[row, full, row], out_specs=[row, full],
        out_shape=[jax.ShapeDtypeStruct((t, MEM_W), BF16), jax.ShapeDtypeStruct((N_MEM, 2 * MEM_W), F32)],
        scratch_shapes=[pltpu.VMEM((2 * MEM_W, N_MEM), F32)], compiler_params=_cp(1),
    )(zq, kv, do)


NT = (((1,), (1,)), ((), ()))


def _branches_fwd(acts, ws, zg, bg, name, tr=512):
    t = zg.shape[0]
    tr = min(tr, t)

    def body(a0, a1, a2, w0, w1, w2, z_ref, b_ref, y0, y1, y2, o_ref):
        merged = None
        for i, (a_ref, w_ref, y_ref) in enumerate(((a0, w0, y0), (a1, w1, y1), (a2, w2, y2))):
            y = lax.dot_general(a_ref[...], w_ref[...], NT, preferred_element_type=F32)
            y_ref[...] = y.astype(BF16)
            g = _sigmoid(z_ref[:, i * D:(i + 1) * D].astype(F32) + b_ref[:, i * D:(i + 1) * D])
            merged = g * y if merged is None else merged + g * y
        o_ref[...] = merged.astype(BF16)

    row = pl.BlockSpec((tr, D), lambda i: (i, 0))
    outs = pl.pallas_call(
        body, name=name, grid=(t // tr,),
        in_specs=[pl.BlockSpec((tr, a.shape[1]), lambda i: (i, 0)) for a in acts]
        + [pl.BlockSpec(w.shape, lambda i: (0, 0)) for w in ws]
        + [pl.BlockSpec((tr, 3 * D), lambda i: (i, 0)), pl.BlockSpec((1, 3 * D), lambda i: (0, 0))],
        out_specs=[row] * 4, out_shape=[jax.ShapeDtypeStruct((t, D), BF16)] * 4, compiler_params=_cp(1),
    )(*acts, *ws, zg, _row(bg))
    return tuple(outs[:3]), outs[3]


def _head_sums(dd):
    return jnp.concatenate(
        [jnp.broadcast_to(jnp.sum(dd[:, h * HEAD_DIM:(h + 1) * HEAD_DIM], axis=-1, keepdims=True), (dd.shape[0], HEAD_DIM))
         for h in range(HEADS)], axis=1)


def _branches_bwd(dymix, w_out, ws, zg, bg, ys, att, name, tr=512):
    t = zg.shape[0]
    tr = min(tr, t)

    def body(dy_ref, wo_ref, w0, w1, w2, z_ref, b_ref, y0, y1, y2, att_ref, d0, d1, d2, dz_ref, db_ref, da0, da1, da2, dl_ref):
        @pl.when(pl.program_id(0) == 0)
        def _():
            db_ref[...] = jnp.zeros_like(db_ref)

        dm = lax.dot_general(dy_ref[...], wo_ref[...], NT, preferred_element_type=F32)
        for i, (y_ref, w_ref, d_ref, da_ref) in enumerate(((y0, w0, d0, da0), (y1, w1, d1, da1), (y2, w2, d2, da2))):
            gi = _sigmoid(z_ref[:, i * D:(i + 1) * D].astype(F32) + b_ref[:, i * D:(i + 1) * D])
            dy = (dm * gi).astype(BF16)
            d_ref[...] = dy
            dz = dm * y_ref[...].astype(F32) * gi * (1.0 - gi)
            dz_ref[:, i * D:(i + 1) * D] = dz.astype(BF16)
            db_ref[0:1, i * D:(i + 1) * D] += jnp.sum(dz, axis=0, keepdims=True)
            da = jnp.dot(dy, w_ref[...], preferred_element_type=F32).astype(BF16)
            da_ref[...] = da
            if i == 1:
                dl_ref[...] = _head_sums(att_ref[...].astype(F32) * da.astype(F32))

    row = pl.BlockSpec((tr, D), lambda i: (i, 0))
    wide = pl.BlockSpec((tr, 3 * D), lambda i: (i, 0))
    heads = pl.BlockSpec((tr, ATT_GW), lambda i: (i, 0))
    outs = pl.pallas_call(
        body, name=name, grid=(t // tr,),
        in_specs=[row, pl.BlockSpec(w_out.shape, lambda i: (0, 0))] + [pl.BlockSpec(w.shape, lambda i: (0, 0)) for w in ws]
        + [wide, pl.BlockSpec((1, 3 * D), lambda i: (0, 0)), row, row, row, heads],
        out_specs=[row, row, row, wide, pl.BlockSpec((8, 3 * D), lambda i: (0, 0))]
        + [pl.BlockSpec((tr, w.shape[1]), lambda i: (i, 0)) for w in ws] + [heads],
        out_shape=[jax.ShapeDtypeStruct((t, D), BF16)] * 3
        + [jax.ShapeDtypeStruct((t, 3 * D), BF16), jax.ShapeDtypeStruct((8, 3 * D), F32)]
        + [jax.ShapeDtypeStruct((t, w.shape[1]), BF16) for w in ws] + [jax.ShapeDtypeStruct((t, ATT_GW), F32)],
        compiler_params=_cp(1),
    )(dymix, w_out, *ws, zg, _row(bg), *ys, att)
    return tuple(outs[:3]), outs[3], outs[4][0], tuple(outs[5:8]), outs[8]


def _adamw(w, g, m, v, name):
    shape = w.shape
    w, g, m, v = (a.reshape(-1, shape[-1]) for a in (w, g, m, v))
    r, d = w.shape
    tr = next((c for c in (512, 352, 256, 128, 64, 32, 16, 8) if r % c == 0), r)

    def body(w_ref, g_ref, m_ref, v_ref, d_ref, nm_ref, nv_ref):
        gg = g_ref[...]
        m2 = ADAM_B1 * m_ref[...] + (1.0 - ADAM_B1) * gg
        v2 = ADAM_B2 * v_ref[...] + (1.0 - ADAM_B2) * (gg * gg)
        m_hat = m2 / (1.0 - ADAM_B1 ** ADAM_STEP)
        v_hat = v2 / (1.0 - ADAM_B2 ** ADAM_STEP)
        d_ref[...] = -ADAM_LR * (m_hat / (jnp.sqrt(v_hat) + ADAM_EPS) + ADAM_WD * w_ref[...])
        nm_ref[...] = m2
        nv_ref[...] = v2

    row = pl.BlockSpec((tr, d), lambda i: (i, 0))
    outs = pl.pallas_call(
        body, name=name, grid=(r // tr,), in_specs=[row] * 4, out_specs=[row] * 3,
        out_shape=[jax.ShapeDtypeStruct((r, d), F32)] * 3, compiler_params=_cp(1),
    )(w, g, m, v)
    return [o.reshape(shape) for o in outs]


def _slab_tile(rows):
    return next((c for c in range(min(rows, 512), 15, -16) if rows % c == 0), rows)


def _sum_slots(parts, out_dtype, name):
    n, r, d = parts.shape
    tr = _slab_tile(r)

    def body(p_ref, o_ref):
        acc = p_ref[0].astype(F32)
        for s in range(1, n):
            acc = acc + p_ref[s].astype(F32)
        o_ref[...] = acc.astype(o_ref.dtype)

    return pl.pallas_call(
        body, name=name, grid=(r // tr,), in_specs=[pl.BlockSpec((n, tr, d), lambda i: (0, i, 0))],
        out_specs=pl.BlockSpec((tr, d), lambda i: (i, 0)),
        out_shape=jax.ShapeDtypeStruct((r, d), out_dtype), compiler_params=_cp(1),
    )(parts)


def _place():
    return lax.axis_index("x"), lax.axis_index("y"), lax.axis_index("c")


def _all_gather(shard, name):
    r, d = shard.shape

    def body(x_ref, out_ref, send_sems, recv_sems, local_sem):
        x, y, c = _place()
        me, sibling = (x, y, c), (x, y, 1 - c)
        chips = [(1 - x, y), (x, 1 - y), (1 - x, 1 - y)]

        def slot(px, py, pc):
            return out_ref.at[4 * px + 2 * py + pc]

        def copy(k, block, to, src=None):
            return pltpu.make_async_remote_copy(
                src_ref=slot(*block) if src is None else src, dst_ref=slot(*block),
                send_sem=send_sems.at[k], recv_sem=recv_sems.at[k], device_id=to, device_id_type=MESH)

        mine = pltpu.make_async_copy(x_ref, slot(*me), local_sem)
        mine.start()
        first = [copy(0, me, sibling, src=x_ref)]
        first += [copy(1 + j, me, (*chip, c), src=x_ref) for j, chip in enumerate(chips)]
        for cp in first:
            cp.start()
        passed = [copy(4 + j, (*chip, c), sibling) for j, chip in enumerate(chips)]
        for j, chip in enumerate(chips):
            copy(1 + j, (*chip, c), me).wait_recv()
            passed[j].start()
        copy(0, sibling, me).wait_recv()
        for j, chip in enumerate(chips):
            copy(4 + j, (*chip, 1 - c), me).wait_recv()
        for cp in first + passed:
            cp.wait_send()
        mine.wait()

    return pl.pallas_call(
        body, name=name, out_shape=jax.ShapeDtypeStruct((N_DEV, r, d), shard.dtype),
        in_specs=[pl.BlockSpec(memory_space=pl.ANY)], out_specs=pl.BlockSpec(memory_space=pl.ANY),
        scratch_shapes=[pltpu.SemaphoreType.DMA((7,)), pltpu.SemaphoreType.DMA((7,)), pltpu.SemaphoreType.DMA],
    )(shard)


HBM_SPEC = pl.BlockSpec(memory_space=pltpu.HBM)
SEM_SPEC = pl.BlockSpec(memory_space=pltpu.SEMAPHORE)
SPLIT_PARAMS = pltpu.CompilerParams(has_side_effects=pltpu.SideEffectType.DATAFLOW_SIDE_EFFECTING)


def _peers():
    x, y, c = _place()
    flip = lambda v, bit: 1 - v if bit else v
    return 4 * x + 2 * y + c, [((flip(x, k >> 2 & 1), flip(y, k >> 1 & 1), flip(c, k & 1))) for k in range(1, N_DEV)]


def _exchange_start(src, all_gather, name):
    r, d = src.shape[-2:]

    def body(src_ref, land_ref, send_sems, recv_sems, src_thru, land_thru, token):
        me, peers = _peers()
        for k, (px, py, pc) in enumerate(peers):
            part = src_ref if all_gather else src_ref.at[4 * px + 2 * py + pc]
            pltpu.make_async_remote_copy(src_ref=part, dst_ref=land_ref.at[me], send_sem=send_sems.at[k],
                                         recv_sem=recv_sems.at[k], device_id=(px, py, pc), device_id_type=MESH).start()
        token[...] = jnp.zeros_like(token)

    land = lax.empty((N_DEV, r, d), src.dtype)
    return pl.pallas_call(
        body, name=name,
        out_shape=(pltpu.SemaphoreType.DMA((N_DEV - 1,)), pltpu.SemaphoreType.DMA((N_DEV - 1,)), pltpu.HBM(src.shape, src.dtype),
                   pltpu.HBM(land.shape, land.dtype), jax.ShapeDtypeStruct((8, LANES), F32)),
        in_specs=(HBM_SPEC, HBM_SPEC), out_specs=(SEM_SPEC, SEM_SPEC, HBM_SPEC, HBM_SPEC, pl.BlockSpec(memory_space=pltpu.VMEM)),
        input_output_aliases={0: 2, 1: 3}, compiler_params=SPLIT_PARAMS,
    )(pltpu.with_memory_space_constraint(src, pltpu.HBM), pltpu.with_memory_space_constraint(land, pltpu.HBM))


def _exchange_wait(started, after, all_gather, name):
    send_sems, recv_sems, src_thru, land_thru, _ = started

    def body(src_ref, land_ref, send_sems, recv_sems, after_ref, src_out, land_out):
        me, peers = _peers()
        for k, (px, py, pc) in enumerate(peers):
            part = src_ref if all_gather else src_ref.at[4 * px + 2 * py + pc]
            cp = pltpu.make_async_remote_copy(src_ref=part, dst_ref=land_ref.at[4 * px + 2 * py + pc], send_sem=send_sems.at[k],
                                              recv_sem=recv_sems.at[k], device_id=(px, py, pc), device_id_type=MESH)
            cp.wait_send()
            cp.wait_recv()

    return pl.pallas_call(
        body, name=name,
        out_shape=(pltpu.HBM(src_thru.shape, src_thru.dtype), pltpu.HBM(land_thru.shape, land_thru.dtype)),
        in_specs=(HBM_SPEC, HBM_SPEC, SEM_SPEC, SEM_SPEC, pl.BlockSpec(memory_space=pl.ANY)), out_specs=(HBM_SPEC, HBM_SPEC),
        input_output_aliases={0: 0, 1: 1}, compiler_params=SPLIT_PARAMS,
    )(src_thru, land_thru, send_sems, recv_sems, after)


def _own_slot(land, mine):
    x, y, c = _place()
    return lax.dynamic_update_slice(land, mine[None], (4 * x + 2 * y + c, 0, 0))


BIG = (("w_in", (D, 864), 1), ("w_conv_out", (CONV_W, 128), 1), ("w_att_out", (ATT_GW, 128), 1), ("w_mem_kv", (128, D), 0),
       ("w_mem_out", (MEM_W, 128), 1), ("w_out", (128, D), 0), ("w_ffn_in", (D, 704), 1), ("w_ffn_out", (352, D), 0),
       ("conv_dw", (CONV_K, 64), 1))
SMALL = (("rel_bias", (NUM_BUCKETS, 12)), ("norm_mix_pre", (DEPTH, D)), ("b_gate", (DEPTH, 3 * D)),
         ("conv_dw_bias", (DEPTH, CONV_W)), ("conv_ln_g", (DEPTH, CONV_W)), ("conv_ln_b", (DEPTH, CONV_W)),
         ("norm_mem", (DEPTH, D)), ("norm_mix_post", (DEPTH, D)), ("norm_ffn_pre", (DEPTH, D)), ("norm_ffn_post", (DEPTH, D)))
TWIN_WEIGHTS = ("rel_bias", "norm_mix_pre", "w_in", "b_gate", "conv_dw", "conv_dw_bias", "conv_ln_g", "conv_ln_b", "w_conv_out",
                "w_att_out", "norm_mem", "w_mem_kv", "w_mem_out", "w_out", "norm_mix_post", "norm_ffn_pre", "w_ffn_in",
                "w_ffn_out", "norm_ffn_post")


BIG_INFO = {n: (s, a) for n, s, a in BIG}
GROUPS = {"in": ("w_in",), "mix": ("conv_dw", "w_conv_out", "w_att_out", "w_mem_kv", "w_mem_out", "w_out"),
          "ffn_in": ("w_ffn_in",), "ffn_out": ("w_ffn_out",)}
TILE_BYTES_PER_LANE = 32


def _rows_of(shape, dtype):
    tile = TILE_BYTES_PER_LANE // jnp.dtype(dtype).itemsize
    return -(-int(np.prod(shape)) // (D * tile)) * tile


def _row_form(shape, axis):
    return tuple(shape) if axis == 0 else (shape[1], shape[0])


def _as_slab_rows(p, lead=()):
    shape = p.shape[len(lead):]
    rows = _rows_of(shape, p.dtype)
    if shape == (rows, D):
        return p
    n = int(np.prod(shape))
    p = jnp.pad(p.reshape(*lead, n), [(0, 0)] * len(lead) + [(0, rows * D - n)])
    return p.reshape(*lead, rows, D)


def _from_slab_rows(rows, shape, lead=()):
    if rows.shape[len(lead):] == tuple(shape):
        return rows
    n = int(np.prod(shape))
    return rows.reshape(*lead, -1)[..., :n].reshape(*lead, *shape)


def _pack(pieces, dtype):
    return jnp.concatenate([_as_slab_rows(p.astype(dtype)) for p in pieces], axis=0)


def _unpack(slab, shapes, packed_as):
    out, r0 = [], 0
    for s in shapes:
        rows = _rows_of(s, packed_as)
        out.append(_from_slab_rows(slab[r0:r0 + rows], s))
        r0 += rows
    return out


def _pack_group(blocks, layer, group):
    return _pack([blocks[n][layer] if BIG_INFO[n][1] == 0 else blocks[n][layer].T for n in GROUPS[group]], BF16)


def _full_weights(gathered, group):
    out, r0 = {}, 0
    for name in GROUPS[group]:
        r, k = _row_form(*BIG_INFO[name])
        rows = _rows_of((r, k), BF16)
        out[name] = _from_slab_rows(gathered[:, r0:r0 + rows], (r, k), lead=(N_DEV,)).reshape(N_DEV * r, k)
        r0 += rows
    return out


def _scatter_layout(grads, group):
    parts = []
    for name in GROUPS[group]:
        r, k = _row_form(*BIG_INFO[name])
        parts.append(_as_slab_rows(grads[name].astype(BF16).reshape(N_DEV, r, k), lead=(N_DEV,)))
    return jnp.concatenate(parts, axis=1)


def _unpack_blocks(slab, group):
    out = _unpack(slab, [_row_form(*BIG_INFO[n]) for n in GROUPS[group]], BF16)
    return {n: b if BIG_INFO[n][1] == 0 else b.T for n, b in zip(GROUPS[group], out)}


def _split_w_in(w_in):
    edges = np.cumsum((0,) + SEG)
    return [w_in[edges[i]:edges[i + 1]] for i in range(4)]


def _conv_taps(conv_dw):
    return jnp.pad(conv_dw.astype(F32).T, ((0, 32 - CONV_K), (0, 0)))


def _layer_fwd(l, x, h, nm, get_w, sm, bias_tabs):
    t = x.shape[0]
    win = get_w(l, "in", h)
    w_seg = _split_w_in(win["w_in"])
    zc = _mm(h, w_seg[0], "fwd_in_conv", tb=True)
    za = _mm(h, w_seg[1], "fwd_in_att", tb=True, tn=1152)
    zq = _mm(h, w_seg[2], "fwd_in_memq", tb=True)
    zg = _mm(h, w_seg[3], "fwd_in_gate", tb=True)
    wts = get_w(l, "mix", zg)
    y, cact = _conv_fwd(zc, _conv_taps(wts["conv_dw"]), sm["conv_dw_bias"][l], sm["conv_ln_g"][l], sm["conv_ln_b"][l],
                        "conv_fwd")
    qkv, outs, lses = [], [], []
    for g, (_, dil) in enumerate(ATT_PATTERNS):
        cols = tuple(i * 3 + g for i in range(3))
        if dil == 1:
            trio, at = (za, za, za), cols
        else:
            trio, at = _to_residue_major([(za, c) for c in cols], dil, f"qkv_residue_d{dil}"), (0, 0, 0)
        o, ls = _att_fwd(*trio, bias_tabs[g], t // dil, f"att_fwd_d{dil}", cols=at)
        qkv.append((trio, at))
        outs.append(o)
        lses.append(ls)
    att, lse = _att_combine(outs, lses, "att_combine")
    kv = _mm(nm, wts["w_mem_kv"], "fwd_mem_kv")
    om = _mem_fwd(zq, kv, "mem_fwd")
    w_br = (wts["w_conv_out"], wts["w_att_out"], wts["w_mem_out"])
    ys, merged = _branches_fwd((cact, att, om), w_br, zg, sm["b_gate"][l], "branches_fwd")
    ymix, x1, h2 = _mm_norm_res(merged, wts["w_out"], x, sm["norm_mix_post"][l], sm["norm_ffn_pre"][l], "fwd_out_norm")
    gff, uff, act = _mm_swiglu_fwd(h2, get_w(l, "ffn_in", h2)["w_ffn_in"], "fwd_ffn_in_swiglu")
    g_next = sm["norm_mix_pre"][l + 1] if l + 1 < DEPTH else None
    f, x2, h_next = _mm_norm_res(act, get_w(l, "ffn_out", act)["w_ffn_out"], x1, sm["norm_ffn_post"][l], g_next,
                                 "fwd_ffn_out_norm" if g_next is not None else "fwd_ffn_out_last")
    saved = dict(x=x, h=h, zc=zc, zq=zq, zg=zg, y=y, cact=cact, qkv=qkv, att=att, lse=lse, kv=kv, om=om, ys=ys,
                 merged=merged, ymix=ymix, x1=x1, h2=h2, gff=gff, uff=uff, act=act, f=f)
    return x2, h_next, saved


def _after(value, *tokens):
    for t in tokens:
        if t is not None:
            value = value + t[0, 0].astype(value.dtype)
    return value


def _layer_bwd(l, dx2, s, nm, mem, get_w, put_g, sm, bias_tabs, bucket_idx):
    t = dx2.shape[0]
    win, wts = get_w(l, "in", None), get_w(l, "mix", None)
    w_ffn_in, w_ffn_out = get_w(l, "ffn_in", None)["w_ffn_in"], get_w(l, "ffn_out", None)["w_ffn_out"]
    gb, gs = {}, {}
    df, gs["norm_ffn_post"] = _norm_bwd(s["f"], sm["norm_ffn_post"][l], dx2, None, BF16, "bwd_norm_ffn_post")
    dg, du = _mm_swiglu_bwd(df, w_ffn_out, s["gff"], s["uff"], "bwd_ffn_out_d_swiglu")
    tok_out = put_g(l, "ffn_out", {"w_ffn_out": _mm(s["act"], df, "bwd_ffn_out_w", ta=True, tm=1408)})
    tok_in = put_g(l, "ffn_in", {"w_ffn_in": jnp.concatenate([_mm(dg, s["h2"], "bwd_ffn_in_w_gate", ta=True, tm=1408),
                                                              _mm(du, s["h2"], "bwd_ffn_in_w_up", ta=True, tm=1408)], axis=0)})
    dh2 = _mm_sum([(dg, _after(w_ffn_in[:FFN_H], tok_out, tok_in), FFN_H // 2), (du, w_ffn_in[FFN_H:], FFN_H // 2)],
                  "bwd_ffn_in_d")
    dx1, dymix, gs["norm_ffn_pre"], gs["norm_mix_post"] = _norm_bwd_pair(
        s["x1"], sm["norm_ffn_pre"][l], dh2, dx2, s["ymix"], sm["norm_mix_post"][l], "bwd_norm_ffn_pre_mix_post")
    gb["w_out"] = _mm(s["merged"], dymix, "bwd_out_w", ta=True)
    w_br = (wts["w_conv_out"], wts["w_att_out"], wts["w_mem_out"])
    dys, dzg, gs["b_gate"], (dcact, datt, dom), delta = _branches_bwd(
        dymix, wts["w_out"], w_br, s["zg"], sm["b_gate"][l], s["ys"], s["att"], "branches_bwd")
    gb["w_conv_out"] = _mm(dys[0], s["cact"], "bwd_conv_out_w", ta=True)
    dzc, dconv, gs["conv_dw_bias"], gs["conv_ln_g"], gs["conv_ln_b"] = _conv_bwd(
        s["zc"], s["y"], dcact, _conv_taps(wts["conv_dw"]), sm["conv_ln_g"][l], sm["conv_ln_b"][l], "conv_bwd")
    gb["conv_dw"] = dconv[:CONV_K].T
    gb["w_att_out"] = _mm(dys[1], s["att"], "bwd_att_out_w", ta=True)
    dqkv, rel = [], []
    for g, (_, dil) in enumerate(ATT_PATTERNS):
        trio, at = s["qkv"][g]
        if dil == 1:
            dl_r, do_r, lse_r = delta, datt, s["lse"]
        else:
            dl_r, do_r, lse_r = _to_residue_major([(delta, 0), (datt, 0), (s["lse"], 0)], dil, f"att_bwd_residue_d{dil}")
        dq, dk, dv, dbias = _att_bwd(*trio, bias_tabs[g], dl_r, do_r, lse_r, t // dil, f"att_bwd_d{dil}", cols=at)
        dqkv.append((dq, dk, dv))
        rel.append(_bias_grad(dbias, bucket_idx[g], f"bias_grad_d{dil}")[:HEADS, :NUM_BUCKETS])
    dza = _assemble_dza(dqkv, "att_bwd_assemble")
    gs["rel_bias"] = jnp.concatenate(rel, axis=0).T
    gb["w_mem_out"] = _mm(dys[2], s["om"], "bwd_mem_out_w", ta=True)
    dzq, dkv = _mem_bwd(s["zq"], s["kv"], dom, "mem_bwd")
    dkv = dkv.astype(BF16)
    gb["w_mem_kv"] = _mm(nm, dkv, "bwd_mem_kv_w", ta=True)
    tok = put_g(l, "mix", gb)
    dnm = _mm(dkv, wts["w_mem_kv"], "bwd_mem_kv_d", tb=True, out_dtype=F32)
    _, gs["norm_mem"] = _norm_bwd(mem, sm["norm_mem"][l], dnm, None, BF16, "bwd_norm_mem")
    segs = ((dzc, "conv", 1024), (dza, "att", 1152), (_after(dzq, tok), "memq", 512), (dzg, "gate", 1024))
    tok = put_g(l, "in", {"w_in": jnp.concatenate([_mm(dz, s["h"], f"bwd_in_{nm_}_w", ta=True, tm=min(blk, 1152), tk=2048)
                                                   for dz, nm_, blk in segs], axis=0)})
    w_seg = _split_w_in(win["w_in"])
    w_seg[0] = _after(w_seg[0], tok)
    dx, gs["norm_mix_pre"] = _mm_sum([(dz, w_seg[i], blk) for i, (dz, _, blk) in enumerate(segs)], "bwd_in_d_norm",
                                     norm=(s["x"], sm["norm_mix_pre"][l], dx1))
    return dx, gs


def _local_step(x, mem, target, sm, get_w, put_g, tokens=()):
    t = x.shape[0]
    bias_tabs, bucket_idx = [], []
    for g, (_, dil) in enumerate(ATT_PATTERNS):
        idx = _bucket_index(_att_tiles(t // dil)[1], dil)
        bucket_idx.append(jnp.asarray(idx))
        bias_tabs.append(_bias_table(sm["rel_bias"][:, g * HEADS:(g + 1) * HEADS], bucket_idx[g], f"bias_table_d{dil}"))
    h = _norm_plain(x, _after(sm["norm_mix_pre"][0], *tokens), "norm_first")
    saved, nms = [], []
    for l in range(DEPTH):
        nm = _norm_plain(mem, sm["norm_mem"][l], "norm_mem")
        x, h, s = _layer_fwd(l, x, h, nm, get_w, sm, bias_tabs)
        saved.append(s)
        nms.append(nm)
    loss, dx = _loss_head(x, target, "loss_head")
    gsmall = {}
    for l in reversed(range(DEPTH)):
        dx, gs = _layer_bwd(l, dx, saved[l], nms[l], mem, get_w, put_g, sm, bias_tabs, bucket_idx)
        for n, v in gs.items():
            gsmall.setdefault(n, {})[l] = v
    small = {}
    for n, _ in SMALL:
        small[n] = gsmall[n][0] + gsmall[n][1] if n == "rel_bias" else jnp.stack([gsmall[n][0], gsmall[n][1]])
    return loss, dx, small


def kernel(x, mem, rel_bias, norm_mix_pre, w_in, b_gate, conv_dw, conv_dw_bias, conv_ln_g, conv_ln_b, w_conv_out, w_att_out, norm_mem, w_mem_kv, w_mem_out, w_out, norm_mix_post, norm_ffn_pre, w_ffn_in, w_ffn_out, norm_ffn_post, loss_target, m_rel_bias, m_norm_mix_pre, m_w_in, m_b_gate, m_conv_dw, m_conv_dw_bias, m_conv_ln_g, m_conv_ln_b, m_w_conv_out, m_w_att_out, m_norm_mem, m_w_mem_kv, m_w_mem_out, m_w_out, m_norm_mix_post, m_norm_ffn_pre, m_w_ffn_in, m_w_ffn_out, m_norm_ffn_post, v_rel_bias, v_norm_mix_pre, v_w_in, v_b_gate, v_conv_dw, v_conv_dw_bias, v_conv_ln_g, v_conv_ln_b, v_w_conv_out, v_w_att_out, v_norm_mem, v_w_mem_kv, v_w_mem_out, v_w_out, v_norm_mix_post, v_norm_ffn_pre, v_w_ffn_in, v_w_ffn_out, v_norm_ffn_post):
    args = dict(locals())
    w = {n: args[n] for n in TWIN_WEIGHTS}
    m = {n: args["m_" + n] for n in TWIN_WEIGHTS}
    v = {n: args["v_" + n] for n in TWIN_WEIGHTS}
    sm = {n: w[n] for n, _ in SMALL}

    first = (0, "in")
    keys = [(l, grp) for l in range(DEPTH) for grp in GROUPS]
    shard = {(l, grp): _pack_group(w, l, grp) for l, grp in keys}
    weights = {first: _full_weights(_all_gather(shard[first], "gather_l0_in"), "in")}
    settled = weights[first]["w_in"][:1, :1] * 0
    ag = {}
    for k in keys:
        if k != first:
            ag[k] = _exchange_start(_after(shard[k], settled), True, f"ag_start_l{k[0]}_{k[1]}")
            settled = ag[k][4]

    def get_w(l, grp, after):
        if (l, grp) not in weights:
            mine, land = _exchange_wait(ag[l, grp], after, True, f"ag_wait_l{l}_{grp}")
            weights[l, grp] = _full_weights(_own_slot(land, mine), grp)
        return weights[l, grp]

    rs = {}

    def put_g(l, grp, grads):
        rs[l, grp] = _exchange_start(_scatter_layout(grads, grp), False, f"rs_start_l{l}_{grp}")
        return rs[l, grp][4]

    loss, dx, gsmall = _local_step(x[0], mem[0], loss_target[0], sm, get_w, put_g, [st[4] for st in ag.values()])
    loss = lax.psum(loss, ("x", "y", "c"))

    xi, yi, ci = _place()
    me = 4 * xi + 2 * yi + ci
    g_slab = {}
    for k, st in rs.items():
        contrib, land = _exchange_wait(st, dx, False, f"rs_wait_l{k[0]}_{k[1]}")
        own = lax.dynamic_index_in_dim(contrib, me, axis=0, keepdims=False)
        g_slab[k] = _sum_slots(_own_slot(land, own), F32, f"rs_sum_l{k[0]}_{k[1]}")
    g_layers = [{n: b for grp in GROUPS for n, b in _unpack_blocks(g_slab[l, grp], grp).items()} for l in range(DEPTH)]
    small_shapes = [s for _, s in SMALL]
    g_small = _unpack(_sum_slots(_all_gather(_pack([gsmall[n] for n, _ in SMALL], F32), "gather_small"), F32, "sum_small"),
                      small_shapes, F32)
    grads = {n: jnp.stack([g_layers[0][n], g_layers[1][n]]) for n, _, _ in BIG}
    grads.update({n: g_small[i] for i, (n, _) in enumerate(SMALL)})

    delta, new_m, new_v = {}, {}, {}
    for n in TWIN_WEIGHTS:
        delta[n], new_m[n], new_v[n] = _adamw(w[n], grads[n], m[n], v[n], f"adamw_{n}")

    return (loss, dx[None], *[grads[n] for n in TWIN_WEIGHTS], *[delta[n] for n in TWIN_WEIGHTS],
            *[new_m[n] for n in TWIN_WEIGHTS], *[new_v[n] for n in TWIN_WEIGHTS])
```

```python
import functools

import numpy as np
import jax
import jax.numpy as jnp
from jax import lax
from jax.experimental import pallas as pl
from jax.experimental.pallas import tpu as pltpu

F32 = jnp.float32
BF16 = jnp.bfloat16

N_DEV = 8
D = 1024
DEPTH = 2
CONV_W = 512
CONV_K = 31
CONV_PAD = 16
ATT_PATTERNS = ((128, 1), (512, 4), (2048, 16))
ATT_RADIUS = 64
HEADS = 4
HEAD_DIM = 64
ATT_GW = HEADS * HEAD_DIM
ATT_W = 3 * ATT_GW
MEM_HEADS = 4
MEM_HD = 128
MEM_W = 512
N_MEM = 256
FFN_H = 2816
NUM_BUCKETS = 32
MAX_DISTANCE = 1024
RMS_EPS = 1e-6
LN_EPS = 1e-5
NEG_INF = -1e30
SEG = (2 * CONV_W, 3 * ATT_W, MEM_W, 3 * D)
ADAM_LR, ADAM_B1, ADAM_B2, ADAM_EPS, ADAM_WD, ADAM_STEP = 0.001, 0.9, 0.999, 1e-08, 0.01, 10

VMEM_LIMIT_V7X = 56 * 1024 * 1024
MESH = pl.DeviceIdType.MESH


def _cp(n_axes):
    return pltpu.CompilerParams(dimension_semantics=("arbitrary",) * n_axes, vmem_limit_bytes=VMEM_LIMIT_V7X)


def _row(v):
    return v.reshape(1, -1)


def _mm(a, b, name, *, ta=False, tb=False, out_dtype=BF16, tm=2048, tn=1024, tk=2048):
    kdim, m = a.shape if ta else a.shape[::-1]
    n, kb = b.shape if tb else b.shape[::-1]
    assert kb == kdim and a.dtype == BF16 and b.dtype == BF16
    tm, tn, tk = min(tm, m), min(tn, n), min(tk, kdim)
    assert m % tm == 0 and n % tn == 0 and kdim % tk == 0, (name, m, n, kdim, tm, tn, tk)
    nk = kdim // tk
    a_spec = pl.BlockSpec((tk, tm), lambda i, j, k: (k, i)) if ta else pl.BlockSpec((tm, tk), lambda i, j, k: (i, k))
    b_spec = pl.BlockSpec((tn, tk), lambda i, j, k: (j, k)) if tb else pl.BlockSpec((tk, tn), lambda i, j, k: (k, j))
    o_spec = pl.BlockSpec((tm, tn), lambda i, j, k: (i, j))
    dims = (((0 if ta else 1,), (1 if tb else 0,)), ((), ()))
    use_scratch = nk > 1 and out_dtype != F32

    def body(*refs):
        a_ref, b_ref, o_ref = refs[:3]
        p = lax.dot_general(a_ref[...], b_ref[...], dims, preferred_element_type=F32)
        if nk == 1:
            o_ref[...] = p.astype(o_ref.dtype)
            return
        k = pl.program_id(2)
        t_ref = refs[3] if use_scratch else o_ref

        @pl.when(k == 0)
        def _():
            t_ref[...] = p

        @pl.when(k > 0)
        def _():
            t_ref[...] += p

        if use_scratch:
            @pl.when(k == nk - 1)
            def _():
                o_ref[...] = t_ref[...].astype(o_ref.dtype)

    return pl.pallas_call(
        body, name=name, grid=(m // tm, n // tn, nk), in_specs=[a_spec, b_spec], out_specs=o_spec,
        out_shape=jax.ShapeDtypeStruct((m, n), out_dtype),
        scratch_shapes=[pltpu.VMEM((tm, tn), F32)] if use_scratch else [],
        compiler_params=_cp(3),
    )(a, b)


EPILOGUE_COLS = 512
EPILOGUE_ROWS = 512


def _chunks(size, step):
    return [(c, min(c + step, size)) for c in range(0, size, step)]


def _rms_bwd_val(v, g, dy):
    r = lax.rsqrt(jnp.mean(v * v, axis=-1, keepdims=True) + RMS_EPS)
    vhat = v * r
    dvh = dy * g
    dv = r * (dvh - vhat * jnp.mean(dvh * vhat, axis=-1, keepdims=True))
    return dv, jnp.sum(dy * vhat, axis=0, keepdims=True)


def _mm_sum(pairs, name, *, norm=None, tm=1024):
    m, n = pairs[0][0].shape[0], pairs[0][1].shape[1]
    tm = min(tm, m)
    starts, specs, total = [], [], 0
    for a, b, tk in pairs:
        assert a.shape == (m, b.shape[0]) and b.shape[1] == n and a.shape[1] % tk == 0 and a.dtype == b.dtype == BF16
        nk = a.shape[1] // tk
        starts.append((total, nk))
        step = functools.partial(lambda k, s0, nk_: jnp.clip(k - s0, 0, nk_ - 1), s0=total, nk_=nk)
        specs.append(pl.BlockSpec((tm, tk), functools.partial(lambda i, k, st: (i, st(k)), st=step)))
        specs.append(pl.BlockSpec((tk, n), functools.partial(lambda i, k, st: (st(k), 0), st=step)))
        total += nk
    np_ = 2 * len(pairs)
    row = pl.BlockSpec((tm, n), lambda i, k: (i, 0))

    def accumulate(refs, acc, k, upto=total):
        for p, (s0, nk) in enumerate(starts):
            @pl.when((k >= s0) & (k < min(s0 + nk, upto)))
            def _(p=p, s0=s0):
                part = jnp.dot(refs[2 * p][...], refs[2 * p + 1][...], preferred_element_type=F32)
                if s0 == 0:
                    @pl.when(k == 0)
                    def _():
                        acc[...] = part

                    @pl.when(k > 0)
                    def _():
                        acc[...] += part
                else:
                    acc[...] += part

    operands = [op for a, b, _ in pairs for op in (a, b)]
    if norm is None:
        def plain(*refs):
            k = pl.program_id(1)
            accumulate(refs, refs[np_ + 1], k)

            @pl.when(k == total - 1)
            def _():
                refs[np_][...] = refs[np_ + 1][...].astype(BF16)

        return pl.pallas_call(
            plain, name=name, grid=(m // tm, total), in_specs=specs, out_specs=row,
            out_shape=jax.ShapeDtypeStruct((m, n), BF16), scratch_shapes=[pltpu.VMEM((tm, n), F32)],
            compiler_params=_cp(2),
        )(*operands)

    v, g, dres = norm

    def body(*refs):
        v_ref, g_ref, dres_ref, dv_ref, dg_ref, acc = refs[np_:np_ + 6]
        i, k = pl.program_id(0), pl.program_id(1)
        accumulate(refs, acc, k, upto=total - 1)

        @pl.when((i == 0) & (k == 0))
        def _():
            dg_ref[...] = jnp.zeros_like(dg_ref)

        @pl.when(k == total - 1)
        def _():
            a_ref, b_ref = refs[np_ - 2], refs[np_ - 1]
            for r0, r1 in _chunks(tm, EPILOGUE_ROWS):
                dy = jnp.dot(a_ref[r0:r1, :], b_ref[...], preferred_element_type=F32)
                if total > 1:
                    dy = dy + acc[r0:r1, :]
                dv, dg = _rms_bwd_val(v_ref[r0:r1, :], g_ref[...], dy)
                dv_ref[r0:r1, :] = dv + dres_ref[r0:r1, :]
                dg_ref[0:1, :] += dg

    once = pl.BlockSpec((tm, n), lambda i, k: (i, 0), pipeline_mode=pl.Buffered(1))
    dv, dg = pl.pallas_call(
        body, name=name, grid=(m // tm, total),
        in_specs=specs + [once, pl.BlockSpec((1, n), lambda i, k: (0, 0)), once],
        out_specs=[row, pl.BlockSpec((8, n), lambda i, k: (0, 0))],
        out_shape=[jax.ShapeDtypeStruct((m, n), F32), jax.ShapeDtypeStruct((8, n), F32)],
        scratch_shapes=[pltpu.VMEM((tm, n), F32)], compiler_params=_cp(2),
    )(*operands, v, _row(g), dres)
    return dv, dg[0]


def _mm_swiglu_fwd(h, w, name, *, tm=1024, tn=1408):
    m, kdim = h.shape
    hid = w.shape[0] // 2
    tm = min(tm, m)
    nj = hid // tn
    dims = (((1,), (1,)), ((), ()))

    def body(h_ref, wg_ref, wu_ref, g_ref, u_ref, a_ref):
        hv = h_ref[...]
        for c0, c1 in _chunks(tn, EPILOGUE_COLS):
            g = lax.dot_general(hv, wg_ref[c0:c1, :], dims, preferred_element_type=F32)
            u = lax.dot_general(hv, wu_ref[c0:c1, :], dims, preferred_element_type=F32)
            g_ref[:, c0:c1] = g.astype(BF16)
            u_ref[:, c0:c1] = u.astype(BF16)
            a_ref[:, c0:c1] = (g * _sigmoid(g) * u).astype(BF16)

    out = pl.BlockSpec((tm, tn), lambda i, j: (i, j))
    return pl.pallas_call(
        body, name=name, grid=(m // tm, nj),
        in_specs=[pl.BlockSpec((tm, kdim), lambda i, j: (i, 0)), pl.BlockSpec((tn, kdim), lambda i, j: (j, 0)),
                  pl.BlockSpec((tn, kdim), lambda i, j: (j + nj, 0))],
        out_specs=[out, out, out], out_shape=[jax.ShapeDtypeStruct((m, hid), BF16)] * 3, compiler_params=_cp(2),
    )(h, w, w)


def _mm_swiglu_bwd(df, w, g, u, name, *, tm=1024, tn=1408):
    m, kdim = df.shape
    hid = w.shape[0]
    tm = min(tm, m)

    def body(df_ref, w_ref, g_ref, u_ref, dg_ref, du_ref):
        dfv = df_ref[...]
        for c0, c1 in _chunks(tn, EPILOGUE_COLS):
            da = lax.dot_general(dfv, w_ref[c0:c1, :], (((1,), (1,)), ((), ())), preferred_element_type=F32)
            gg, uu, dab = g_ref[:, c0:c1], u_ref[:, c0:c1], da.astype(BF16)
            sg = _sigmoid(gg)
            dg_ref[:, c0:c1] = dab * uu * (sg * (1.0 + gg * (1.0 - sg)))
            du_ref[:, c0:c1] = dab * gg * sg

    blk = pl.BlockSpec((tm, tn), lambda i, j: (i, j))
    return pl.pallas_call(
        body, name=name, grid=(m // tm, hid // tn),
        in_specs=[pl.BlockSpec((tm, kdim), lambda i, j: (i, 0)), pl.BlockSpec((tn, kdim), lambda i, j: (j, 0)), blk, blk],
        out_specs=[blk, blk], out_shape=[jax.ShapeDtypeStruct((m, hid), BF16)] * 2, compiler_params=_cp(2),
    )(df, w, g, u)


def _mm_norm_res(a, b, xres, g, g_next, name, *, tm=1024):
    m, kdim = a.shape
    n = b.shape[1]
    tm = min(tm, m)
    two = g_next is not None

    def body(*refs):
        a_ref, b_ref, x_ref, g_ref = refs[:4]
        y_ref, xn_ref = refs[4 + two], refs[5 + two]
        for r0, r1 in _chunks(tm, EPILOGUE_ROWS):
            y = jnp.dot(a_ref[r0:r1, :], b_ref[...], preferred_element_type=F32)
            y_ref[r0:r1, :] = y.astype(BF16)
            xn = x_ref[r0:r1, :] + _rms_val(y, g_ref[...])
            xn_ref[r0:r1, :] = xn
            if two:
                refs[7][r0:r1, :] = _rms_val(xn, refs[4][...]).astype(BF16)

    row = pl.BlockSpec((tm, n), lambda i: (i, 0))
    vec = pl.BlockSpec((1, n), lambda i: (0, 0))
    outs = pl.pallas_call(
        body, name=name, grid=(m // tm,),
        in_specs=[pl.BlockSpec((tm, kdim), lambda i: (i, 0)), pl.BlockSpec((kdim, n), lambda i: (0, 0)), row, vec] + ([vec] if two else []),
        out_specs=[row, row] + ([row] if two else []),
        out_shape=[jax.ShapeDtypeStruct((m, n), BF16), jax.ShapeDtypeStruct((m, n), F32)]
        + ([jax.ShapeDtypeStruct((m, n), BF16)] if two else []),
        compiler_params=_cp(1),
    )(*((a, b, xres, _row(g)) + ((_row(g_next),) if two else ())))
    return (outs[0], outs[1], outs[2]) if two else (outs[0], outs[1], None)


def _rms_val(v, g):
    return v * lax.rsqrt(jnp.mean(v * v, axis=-1, keepdims=True) + RMS_EPS) * g


def _norm_plain(v, g, name, tr=512):
    t, d = v.shape
    tr = min(tr, t)

    def body(v_ref, g_ref, o_ref):
        o_ref[...] = _rms_val(v_ref[...], g_ref[...]).astype(BF16)

    return pl.pallas_call(
        body, name=name, grid=(t // tr,),
        in_specs=[pl.BlockSpec((tr, d), lambda i: (i, 0)), pl.BlockSpec((1, d), lambda i: (0, 0))],
        out_specs=pl.BlockSpec((tr, d), lambda i: (i, 0)),
        out_shape=jax.ShapeDtypeStruct((t, d), BF16), compiler_params=_cp(1),
    )(v, _row(g))


def _norm_bwd(v, g, dout, dres, out_dtype, name, tr=1024):
    t, d = v.shape
    tr = min(tr, t)
    nt = t // tr
    has_res = dres is not None

    def body(*refs):
        v_ref, g_ref, do_ref = refs[:3]
        dv_ref, dg_ref = refs[3 + has_res], refs[4 + has_res]
        i = pl.program_id(0)
        vv = v_ref[...].astype(F32)
        dy = do_ref[...].astype(F32)
        r = lax.rsqrt(jnp.mean(vv * vv, axis=-1, keepdims=True) + RMS_EPS)
        vhat = vv * r
        dvh = dy * g_ref[...]
        dv = r * (dvh - vhat * jnp.mean(dvh * vhat, axis=-1, keepdims=True))
        if has_res:
            dv = dv + refs[3][...]
        dv_ref[...] = dv.astype(dv_ref.dtype)
        part = jnp.sum(dy * vhat, axis=0, keepdims=True)

        @pl.when(i == 0)
        def _():
            dg_ref[...] = jnp.zeros_like(dg_ref)

        dg_ref[0:1, :] += part

    row = pl.BlockSpec((tr, d), lambda i: (i, 0))
    dv, dg = pl.pallas_call(
        body, name=name, grid=(nt,),
        in_specs=[row, pl.BlockSpec((1, d), lambda i: (0, 0)), row] + ([row] if has_res else []),
        out_specs=[row, pl.BlockSpec((8, d), lambda i: (0, 0))],
        out_shape=[jax.ShapeDtypeStruct((t, d), out_dtype), jax.ShapeDtypeStruct((8, d), F32)],
        compiler_params=_cp(1),
    )(*((v, _row(g), dout) + ((dres,) if has_res else ())))
    return dv, dg[0]


def _norm_bwd_pair(v1, g1, dout, dres, v2, g2, name, tr=1024):
    t, d = v1.shape
    tr = min(tr, t)

    def body(v1_ref, g1_ref, do_ref, dr_ref, v2_ref, g2_ref, d1_ref, d2_ref, dg1_ref, dg2_ref):
        @pl.when(pl.program_id(0) == 0)
        def _():
            dg1_ref[...] = jnp.zeros_like(dg1_ref)
            dg2_ref[...] = jnp.zeros_like(dg2_ref)

        dv1, dg1 = _rms_bwd_val(v1_ref[...].astype(F32), g1_ref[...], do_ref[...].astype(F32))
        d1 = dv1 + dr_ref[...]
        d1_ref[...] = d1
        dv2, dg2 = _rms_bwd_val(v2_ref[...].astype(F32), g2_ref[...], d1)
        d2_ref[...] = dv2.astype(BF16)
        dg1_ref[0:1, :] += dg1
        dg2_ref[0:1, :] += dg2

    row = pl.BlockSpec((tr, d), lambda i: (i, 0))
    vec = pl.BlockSpec((1, d), lambda i: (0, 0))
    acc = pl.BlockSpec((8, d), lambda i: (0, 0))
    d1, d2, dg1, dg2 = pl.pallas_call(
        body, name=name, grid=(t // tr,), in_specs=[row, vec, row, row, row, vec], out_specs=[row, row, acc, acc],
        out_shape=[jax.ShapeDtypeStruct((t, d), F32), jax.ShapeDtypeStruct((t, d), BF16),
                   jax.ShapeDtypeStruct((8, d), F32), jax.ShapeDtypeStruct((8, d), F32)],
        compiler_params=_cp(1),
    )(v1, _row(g1), dout, dres, v2, _row(g2))
    return d1, d2, dg1[0], dg2[0]


def _loss_head(y, target, name, tr=1024):
    t, d = y.shape
    tr = min(tr, t)
    nt = t // tr

    def body(y_ref, t_ref, dy_ref, l_ref):
        i = pl.program_id(0)
        err = y_ref[...] - t_ref[...]
        dy_ref[...] = err * (1.0 / d)

        @pl.when(i == 0)
        def _():
            l_ref[...] = jnp.zeros_like(l_ref)

        l_ref[...] += jnp.sum(err * err) * (0.5 / d)

    row = pl.BlockSpec((tr, d), lambda i: (i, 0))
    dy, l = pl.pallas_call(
        body, name=name, grid=(nt,), in_specs=[row, row],
        out_specs=[row, pl.BlockSpec((8, 128), lambda i: (0, 0))],
        out_shape=[jax.ShapeDtypeStruct((t, d), F32), jax.ShapeDtypeStruct((8, 128), F32)],
        compiler_params=_cp(1),
    )(y, target)
    return l[0, 0], dy


def _sigmoid(v):
    return 1.0 / (1.0 + jnp.exp(-v))


def _halo_specs(tq, width, nt, halo):
    per = tq // halo
    last = nt * per - 1
    main = pl.BlockSpec((tq, width), lambda i: (i, 0))
    prev = pl.BlockSpec((halo, width), lambda i: (jnp.maximum(i * per - 1, 0), 0))
    nxt = pl.BlockSpec((halo, width), lambda i: (jnp.minimum((i + 1) * per, last), 0))
    return prev, main, nxt


def _glu_window(zp, zm, zn, i, nt):
    def glu(z):
        z = z.astype(F32)
        return z[:, :CONV_W] * _sigmoid(z[:, CONV_W:])

    up = jnp.where(i > 0, glu(zp), 0.0)
    un = jnp.where(i < nt - 1, glu(zn), 0.0)
    return jnp.concatenate([up, glu(zm), un], axis=0)


def _shifted(win, shift, rows):
    if shift == 0:
        return win[:rows]
    return pltpu.roll(win, win.shape[0] - shift, 0)[:rows]


def _conv_fwd(zc, w, b, ln_g, ln_b, name, tq=512):
    t = zc.shape[0]
    tq = min(tq, t)
    nt = t // tq

    def body(zp_ref, zm_ref, zn_ref, w_ref, b_ref, g_ref, bb_ref, y_ref, c_ref):
        i = pl.program_id(0)
        win = _glu_window(zp_ref[...], zm_ref[...], zn_ref[...], i, nt)
        wv = w_ref[...]
        y = jnp.zeros((tq, CONV_W), F32) + b_ref[...]
        for k in range(CONV_K):
            y = y + _shifted(win, k + 1, tq) * wv[k:k + 1, :]
        y_ref[...] = y
        mu = jnp.mean(y, axis=-1, keepdims=True)
        yc = y - mu
        ln = yc * lax.rsqrt(jnp.mean(yc * yc, axis=-1, keepdims=True) + LN_EPS) * g_ref[...] + bb_ref[...]
        c_ref[...] = (ln * _sigmoid(ln)).astype(BF16)

    vec = pl.BlockSpec((1, CONV_W), lambda i: (0, 0))
    out = pl.BlockSpec((tq, CONV_W), lambda i: (i, 0))
    return pl.pallas_call(
        body, name=name, grid=(nt,),
        in_specs=[*_halo_specs(tq, 2 * CONV_W, nt, CONV_PAD), pl.BlockSpec((32, CONV_W), lambda i: (0, 0)), vec, vec, vec],
        out_specs=[out, out],
        out_shape=[jax.ShapeDtypeStruct((t, CONV_W), F32), jax.ShapeDtypeStruct((t, CONV_W), BF16)],
        compiler_params=_cp(1),
    )(zc, zc, zc, w, _row(b), _row(ln_g), _row(ln_b))


def _conv_bwd(zc, y, dc, w, ln_g, ln_b, name, tq=256):
    t = zc.shape[0]
    tq = min(tq, t)
    nt = t // tq
    rows = tq + 2 * CONV_PAD

    def body(zp_ref, zm_ref, zn_ref, yp_ref, ym_ref, yn_ref, dp_ref, dm_ref, dn_ref, w_ref, g_ref, bb_ref,
             dz_ref, dw_ref, db_ref, dg_ref, dbb_ref):
        i = pl.program_id(0)
        uwin = _glu_window(zp_ref[...], zm_ref[...], zn_ref[...], i, nt)
        ywin = jnp.concatenate([yp_ref[...], ym_ref[...], yn_ref[...]], axis=0)
        dcw = jnp.concatenate([dp_ref[...], dm_ref[...], dn_ref[...]], axis=0).astype(F32)
        mu = jnp.mean(ywin, axis=-1, keepdims=True)
        yc = ywin - mu
        rstd = lax.rsqrt(jnp.mean(yc * yc, axis=-1, keepdims=True) + LN_EPS)
        yhat = yc * rstd
        ln = yhat * g_ref[...] + bb_ref[...]
        sg = _sigmoid(ln)
        dl = dcw * (sg * (1.0 + ln * (1.0 - sg)))
        ridx = lax.broadcasted_iota(jnp.int32, (rows, 1), 0)
        inside = ((ridx >= CONV_PAD) | (i > 0)) & ((ridx < tq + CONV_PAD) | (i < nt - 1))
        dl = jnp.where(inside, dl, 0.0)
        dyh = dl * g_ref[...]
        dy = rstd * (dyh - jnp.mean(dyh, axis=-1, keepdims=True) - yhat * jnp.mean(dyh * yhat, axis=-1, keepdims=True))
        dy = jnp.where(inside, dy, 0.0)
        main = slice(CONV_PAD, CONV_PAD + tq)
        dlm, yhm, dym = dl[main], yhat[main], dy[main]

        @pl.when(i == 0)
        def _():
            dw_ref[...] = jnp.zeros_like(dw_ref)
            db_ref[...] = jnp.zeros_like(db_ref)
            dg_ref[...] = jnp.zeros_like(dg_ref)
            dbb_ref[...] = jnp.zeros_like(dbb_ref)

        dg_ref[0:1, :] += jnp.sum(dlm * yhm, axis=0, keepdims=True)
        dbb_ref[0:1, :] += jnp.sum(dlm, axis=0, keepdims=True)
        db_ref[0:1, :] += jnp.sum(dym, axis=0, keepdims=True)
        wv = w_ref[...]
        du = jnp.zeros((tq, CONV_W), F32)
        for k in range(CONV_K):
            du = du + _shifted(dy, 2 * CONV_PAD - 1 - k, tq) * wv[k:k + 1, :]
            dw_ref[k:k + 1, :] += jnp.sum(dym * _shifted(uwin, k + 1, tq), axis=0, keepdims=True)
        zm = zm_ref[...].astype(F32)
        a, gt = zm[:, :CONV_W], zm[:, CONV_W:]
        sgt = _sigmoid(gt)
        dz_ref[:, :CONV_W] = (du * sgt).astype(BF16)
        dz_ref[:, CONV_W:] = (du * a * sgt * (1.0 - sgt)).astype(BF16)

    vec = pl.BlockSpec((1, CONV_W), lambda i: (0, 0))
    acc = pl.BlockSpec((8, CONV_W), lambda i: (0, 0))
    dz, dw, db, dg, dbb = pl.pallas_call(
        body, name=name, grid=(nt,),
        in_specs=[*_halo_specs(tq, 2 * CONV_W, nt, CONV_PAD), *_halo_specs(tq, CONV_W, nt, CONV_PAD),
                  *_halo_specs(tq, CONV_W, nt, CONV_PAD), pl.BlockSpec((32, CONV_W), lambda i: (0, 0)), vec, vec],
        out_specs=[pl.BlockSpec((tq, 2 * CONV_W), lambda i: (i, 0)), pl.BlockSpec((32, CONV_W), lambda i: (0, 0)), acc, acc, acc],
        out_shape=[jax.ShapeDtypeStruct((t, 2 * CONV_W), BF16), jax.ShapeDtypeStruct((32, CONV_W), F32),
                   jax.ShapeDtypeStruct((8, CONV_W), F32), jax.ShapeDtypeStruct((8, CONV_W), F32),
                   jax.ShapeDtypeStruct((8, CONV_W), F32)],
        compiler_params=_cp(1),
    )(zc, zc, zc, y, y, y, dc, dc, dc, w, _row(ln_g), _row(ln_b))
    return dz, dw, db[0], dg[0], dbb[0]


def _att_tiles(seq_len):
    tq = min(1024, seq_len)
    return tq, min(128, tq)


def _t5_bucket_np(rel):
    nb = NUM_BUCKETS // 2
    max_exact = nb // 2
    ret = np.where(rel > 0, nb, 0)
    n = np.abs(rel)
    nf = np.maximum(n, 1).astype(np.float32)
    large = max_exact + (np.log(nf / np.float32(max_exact)) / np.float32(np.log(MAX_DISTANCE / max_exact))
                         * np.float32(nb - max_exact)).astype(np.int32)
    large = np.minimum(large, nb - 1)
    return ret + np.where(n < max_exact, n, large)


def _bucket_index(sb, dilation):
    off = np.arange(sb + 2 * ATT_RADIUS)[None, :] - ATT_RADIUS - np.arange(sb)[:, None]
    idx = _t5_bucket_np(off * dilation).astype(np.int32)
    return np.where(np.abs(off) <= ATT_RADIUS, idx, -1).astype(np.int32)


def _bias_table(tab, idx, name):
    sb, w = idx.shape

    def body(tab_ref, idx_ref, o_ref):
        ix = idx_ref[...]
        for h in range(HEADS):
            acc = jnp.full((sb, w), NEG_INF, F32)
            for b in range(NUM_BUCKETS):
                acc = jnp.where(ix == b, tab_ref[b, h], acc)
            o_ref[h] = acc

    return pl.pallas_call(
        body, name=name, out_shape=jax.ShapeDtypeStruct((HEADS, sb, w), F32),
        in_specs=[pl.BlockSpec(memory_space=pltpu.SMEM), pl.BlockSpec(memory_space=pltpu.VMEM)],
        out_specs=pl.BlockSpec(memory_space=pltpu.VMEM),
    )(tab, idx)


def _qkv_specs(tq, per, last_blk, cols, tile):
    def main(col):
        return pl.BlockSpec((tq, ATT_GW), lambda *g: (tile(*g), col))

    def prev(col):
        return pl.BlockSpec((ATT_RADIUS, ATT_GW), lambda *g: (jnp.maximum(tile(*g) * per - 1, 0), col))

    def nxt(col):
        return pl.BlockSpec((ATT_RADIUS, ATT_GW), lambda *g: (jnp.minimum((tile(*g) + 1) * per, last_blk), col))

    cq, ck, cv = cols
    return [main(cq), prev(ck), main(ck), nxt(ck), prev(cv), main(cv), nxt(cv)]


def _att_fwd(q, k, v, bias, seq_len, name, cols=(0, 0, 0)):
    t = q.shape[0]
    tq, sb = _att_tiles(seq_len)
    nbl = seq_len // tq
    nt = t // tq
    per = tq // ATT_RADIUS
    w = sb + 2 * ATT_RADIUS
    last_blk = t // ATT_RADIUS - 1

    def body(q_ref, kp_ref, kc_ref, kn_ref, vp_ref, vc_ref, vn_ref, b_ref, o_ref, l_ref):
        n = pl.program_id(0) % nbl
        kw = jnp.concatenate([kp_ref[...], kc_ref[...], kn_ref[...]], axis=0)
        vw = jnp.concatenate([vp_ref[...], vc_ref[...], vn_ref[...]], axis=0)
        kpos = n * tq - ATT_RADIUS + lax.broadcasted_iota(jnp.int32, (1, tq + 2 * ATT_RADIUS), 1)
        valid = (kpos >= 0) & (kpos < seq_len)
        for j in range(tq // sb):
            qj = q_ref[j * sb:(j + 1) * sb, :]
            kj, vj, okj = kw[j * sb:j * sb + w], vw[j * sb:j * sb + w], valid[:, j * sb:j * sb + w]
            outs, lses = [], []
            for h in range(HEADS):
                hs = slice(h * HEAD_DIM, (h + 1) * HEAD_DIM)
                s = lax.dot_general(qj[:, hs], kj[:, hs], NT, preferred_element_type=F32)
                s = jnp.where(okj, s * (HEAD_DIM ** -0.5) + b_ref[h], NEG_INF)
                m = jnp.max(s, axis=-1, keepdims=True)
                e = jnp.exp(s - m)
                den = jnp.sum(e, axis=-1, keepdims=True)
                outs.append(jnp.dot(e.astype(BF16), vj[:, hs], preferred_element_type=F32) / den)
                lses.append(jnp.broadcast_to(m + jnp.log(den), (sb, HEAD_DIM)))
            o_ref[j * sb:(j + 1) * sb, :] = jnp.concatenate(outs, axis=1).astype(BF16)
            l_ref[j * sb:(j + 1) * sb, :] = jnp.concatenate(lses, axis=1)

    main = pl.BlockSpec((tq, ATT_GW), lambda i: (i, 0))
    return pl.pallas_call(
        body, name=name, grid=(nt,),
        in_specs=[*_qkv_specs(tq, per, last_blk, cols, lambda i: i), pl.BlockSpec((HEADS, sb, w), lambda i: (0, 0, 0))],
        out_specs=[main, main],
        out_shape=[jax.ShapeDtypeStruct((t, ATT_GW), BF16), jax.ShapeDtypeStruct((t, ATT_GW), F32)],
        compiler_params=_cp(1),
    )(q, k, k, k, v, v, v, bias)


PERM_ROWS = 2048
LANES = 128


def _spec_rm(d, tr, width, col=0):
    return pl.BlockSpec((d, tr // d, width), lambda i: (0, i, col))


def _rm_view(a, d):
    return a.reshape(d, a.shape[0] // d, a.shape[1])


def _gather_residues(scr, val, d, out_ref, col0):
    tr, w = val.shape
    for c in range(w // LANES):
        scr[...] = val[:, c * LANES:(c + 1) * LANES].astype(F32)
        for r in range(d):
            out_ref[r, :, col0 + c * LANES:col0 + (c + 1) * LANES] = scr[pl.ds(r, tr // d, stride=d), :].astype(out_ref.dtype)


def _scatter_residues(scr, ref, d):
    w = ref.shape[2]
    cols = []
    for c in range(w // LANES):
        for r in range(d):
            scr[pl.ds(r, scr.shape[0] // d, stride=d), :] = ref[r, :, c * LANES:(c + 1) * LANES].astype(F32)
        cols.append(scr[...])
    return jnp.concatenate(cols, axis=1)


def _to_residue_major(srcs, d, name, tr=PERM_ROWS):
    t = srcs[0][0].shape[0]
    tr = min(tr, t)
    n = len(srcs)

    def body(*refs):
        scr = refs[2 * n]
        for k in range(n):
            _gather_residues(scr, refs[k][...], d, refs[n + k], 0)

    outs = pl.pallas_call(
        body, name=name, grid=(t // tr,),
        in_specs=[pl.BlockSpec((tr, ATT_GW), functools.partial(lambda i, col: (i, col), col=col)) for _, col in srcs],
        out_specs=[_spec_rm(d, tr, ATT_GW)] * n,
        out_shape=[jax.ShapeDtypeStruct((d, t // d, ATT_GW), a.dtype) for a, _ in srcs],
        scratch_shapes=[pltpu.VMEM((tr, LANES), F32)], compiler_params=_cp(1),
    )(*[a for a, _ in srcs])
    return [o.reshape(t, ATT_GW) for o in outs]


def _att_combine(outs, lses, name, tr=PERM_ROWS):
    t = outs[0].shape[0]
    tr = min(tr, t)
    dils = [d for _, d in ATT_PATTERNS]

    def body(o0, o1, o2, l0, l1, l2, a_ref, lse_ref, scr):
        os_, ls_ = [], []
        for o_ref, l_ref, d in zip((o0, o1, o2), (l0, l1, l2), dils):
            if d == 1:
                os_.append(o_ref[...].astype(F32))
                ls_.append(l_ref[...])
            else:
                os_.append(_scatter_residues(scr, o_ref, d))
                ls_.append(_scatter_residues(scr, l_ref, d))
        la, lb, lc = ls_
        m = jnp.maximum(jnp.maximum(la, lb), lc)
        ea, eb, ec = jnp.exp(la - m), jnp.exp(lb - m), jnp.exp(lc - m)
        den = ea + eb + ec
        a_ref[...] = ((ea * os_[0] + eb * os_[1] + ec * os_[2]) / den).astype(BF16)
        lse_ref[...] = m + jnp.log(den)

    row = pl.BlockSpec((tr, ATT_GW), lambda i: (i, 0))
    specs = [row if d == 1 else _spec_rm(d, tr, ATT_GW) for d in dils]
    views = lambda arrs: [a if d == 1 else _rm_view(a, d) for a, d in zip(arrs, dils)]
    return pl.pallas_call(
        body, name=name, grid=(t // tr,), in_specs=specs * 2, out_specs=[row, row],
        out_shape=[jax.ShapeDtypeStruct((t, ATT_GW), BF16), jax.ShapeDtypeStruct((t, ATT_GW), F32)],
        scratch_shapes=[pltpu.VMEM((tr, LANES), F32)], compiler_params=_cp(1),
    )(*views(outs), *views(lses))


def _assemble_dza(dqkv, name, tr=PERM_ROWS):
    t = dqkv[0][0].shape[0]
    tr = min(tr, t)
    dils = [d for _, d in ATT_PATTERNS]

    def body(*refs):
        o_ref, scr = refs[9], refs[10]
        for g, d in enumerate(dils):
            for c in range(3):
                ref = refs[g * 3 + c]
                val = ref[...] if d == 1 else _scatter_residues(scr, ref, d).astype(BF16)
                o_ref[:, (c * 3 + g) * ATT_GW:(c * 3 + g + 1) * ATT_GW] = val

    row = pl.BlockSpec((tr, ATT_GW), lambda i: (i, 0))
    specs = [row if d == 1 else _spec_rm(d, tr, ATT_GW) for d in dils for _ in range(3)]
    args = [a if d == 1 else _rm_view(a, d) for trio, d in zip(dqkv, dils) for a in trio]
    return pl.pallas_call(
        body, name=name, grid=(t // tr,), in_specs=specs, out_specs=pl.BlockSpec((tr, 3 * ATT_W), lambda i: (i, 0)),
        out_shape=jax.ShapeDtypeStruct((t, 3 * ATT_W), BF16),
        scratch_shapes=[pltpu.VMEM((tr, LANES), F32)], compiler_params=_cp(1),
    )(*args)


def _att_bwd(q, k, v, bias, delta, do, lse, seq_len, name, cols=(0, 0, 0)):
    t = q.shape[0]
    tq, sb = _att_tiles(seq_len)
    nbl = seq_len // tq
    n_res = t // seq_len
    per = tq // ATT_RADIUS
    w = sb + 2 * ATT_RADIUS
    last_blk = t // ATT_RADIUS - 1
    acc_cols = 2 * tq - sb + w
    scale = HEAD_DIM ** -0.5

    def body(q_ref, kp_ref, kc_ref, kn_ref, vp_ref, vc_ref, vn_ref, b_ref, dl_ref, do_ref, l_ref,
             dq_ref, dk_ref, dv_ref, db_ref, ak_ref, av_ref):
        r, n = pl.program_id(0), pl.program_id(1)

        @pl.when((r == 0) & (n == 0))
        def _():
            db_ref[...] = jnp.zeros_like(db_ref)

        @pl.when(n == 0)
        def _():
            ak_ref[...] = jnp.zeros_like(ak_ref)
            av_ref[...] = jnp.zeros_like(av_ref)

        @pl.when(n < nbl)
        def _():
            kw = jnp.concatenate([kp_ref[...], kc_ref[...], kn_ref[...]], axis=0)
            vw = jnp.concatenate([vp_ref[...], vc_ref[...], vn_ref[...]], axis=0)
            kpos = n * tq - ATT_RADIUS + lax.broadcasted_iota(jnp.int32, (1, tq + 2 * ATT_RADIUS), 1)
            valid = (kpos >= 0) & (kpos < seq_len)
            for j in range(tq // sb):
                rows = slice(j * sb, (j + 1) * sb)
                qj, doj = q_ref[rows, :], do_ref[rows, :]
                dlj = dl_ref[rows, :]
                lj = l_ref[rows, :]
                kj, vj, okj = kw[j * sb:j * sb + w], vw[j * sb:j * sb + w], valid[:, j * sb:j * sb + w]
                heads = [slice(h * HEAD_DIM, (h + 1) * HEAD_DIM) for h in range(HEADS)]
                ss = [lax.dot_general(qj[:, hs], kj[:, hs], NT, preferred_element_type=F32) for hs in heads]
                dps = [lax.dot_general(doj[:, hs], vj[:, hs], NT, preferred_element_type=F32) for hs in heads]
                ps, dss, dsbs = [], [], []
                for h, hs in enumerate(heads):
                    s = jnp.where(okj, ss[h] * scale + b_ref[h], NEG_INF)
                    p = jnp.exp(s - lj[:, h * HEAD_DIM:h * HEAD_DIM + 1])
                    ds = p * (dps[h] - dlj[:, h * HEAD_DIM:h * HEAD_DIM + 1])
                    ps.append(p.astype(BF16))
                    dss.append(ds)
                    dsbs.append(ds.astype(BF16))
                dqs = [jnp.dot(dsbs[h], kj[:, hs], preferred_element_type=F32) * scale for h, hs in enumerate(heads)]
                dks = [lax.dot_general(qj[:, hs], dsbs[h], (((0,), (0,)), ((), ())), preferred_element_type=F32) * scale
                       for h, hs in enumerate(heads)]
                dvs = [lax.dot_general(doj[:, hs], ps[h], (((0,), (0,)), ((), ())), preferred_element_type=F32)
                       for h, hs in enumerate(heads)]
                dq_ref[rows, :] = jnp.concatenate(dqs, axis=1).astype(BF16)
                c0 = tq + j * sb
                ak_ref[:, c0:c0 + w] += jnp.concatenate(dks, axis=0)
                av_ref[:, c0:c0 + w] += jnp.concatenate(dvs, axis=0)
                for h in range(HEADS):
                    db_ref[h] += dss[h]

        dk_ref[...] = ak_ref[:, ATT_RADIUS:ATT_RADIUS + tq].T.astype(BF16)
        dv_ref[...] = av_ref[:, ATT_RADIUS:ATT_RADIUS + tq].T.astype(BF16)
        keep = acc_cols - tq
        nk, nv = ak_ref[:, tq:acc_cols], av_ref[:, tq:acc_cols]
        ak_ref[:, 0:keep] = nk
        av_ref[:, 0:keep] = nv
        ak_ref[:, keep:acc_cols] = jnp.zeros((ATT_GW, tq), F32)
        av_ref[:, keep:acc_cols] = jnp.zeros((ATT_GW, tq), F32)

    def tile(r, n):
        return r * nbl + jnp.minimum(n, nbl - 1)

    main = pl.BlockSpec((tq, ATT_GW), lambda r, n: (tile(r, n), 0))
    lag = pl.BlockSpec((tq, ATT_GW), lambda r, n: (r * nbl + jnp.maximum(n - 1, 0), 0))
    bspec = pl.BlockSpec((HEADS, sb, w), lambda r, n: (0, 0, 0))
    return pl.pallas_call(
        body, name=name, grid=(n_res, nbl + 1),
        in_specs=[*_qkv_specs(tq, per, last_blk, cols, tile), bspec, main, main, main],
        out_specs=[main, lag, lag, bspec],
        out_shape=[jax.ShapeDtypeStruct((t, ATT_GW), BF16)] * 3 + [jax.ShapeDtypeStruct((HEADS, sb, w), F32)],
        scratch_shapes=[pltpu.VMEM((ATT_GW, acc_cols), F32), pltpu.VMEM((ATT_GW, acc_cols), F32)],
        compiler_params=_cp(2),
    )(q, k, k, k, v, v, v, bias, delta, do, lse)


def _bias_grad(db, idx, name):
    _, sb, w = db.shape

    def body(db_ref, idx_ref, o_ref):
        ix = idx_ref[...]
        lane = lax.broadcasted_iota(jnp.int32, (1, 128), 1)
        rows = []
        for h in range(HEADS):
            d = db_ref[h]
            acc = jnp.zeros((1, 128), F32)
            for b in range(NUM_BUCKETS):
                acc = acc + jnp.where(lane == b, jnp.sum(jnp.where(ix == b, d, 0.0)), 0.0)
            rows.append(acc)
        o_ref[...] = jnp.concatenate(rows + [jnp.zeros((8 - HEADS, 128), F32)], axis=0)

    return pl.pallas_call(
        body, name=name, out_shape=jax.ShapeDtypeStruct((8, 128), F32),
        in_specs=[pl.BlockSpec(memory_space=pltpu.VMEM), pl.BlockSpec(memory_space=pltpu.VMEM)],
        out_specs=pl.BlockSpec(memory_space=pltpu.VMEM),
    )(db, idx)


def _mem_fwd(zq, kv, name, tq=512):
    t = zq.shape[0]
    tq = min(tq, t)
    scale = MEM_HD ** -0.5

    def body(q_ref, kv_ref, o_ref):
        outs = []
        for h in range(MEM_HEADS):
            hs = slice(h * MEM_HD, (h + 1) * MEM_HD)
            kh = kv_ref[:, h * MEM_HD:(h + 1) * MEM_HD]
            vh = kv_ref[:, MEM_W + h * MEM_HD:MEM_W + (h + 1) * MEM_HD]
            s = lax.dot_general(q_ref[:, hs], kh, (((1,), (1,)), ((), ())), preferred_element_type=F32) * scale
            e = jnp.exp(s - jnp.max(s, axis=-1, keepdims=True))
            den = jnp.sum(e, axis=-1, keepdims=True)
            outs.append(jnp.dot(e.astype(BF16), vh, preferred_element_type=F32) / den)
        o_ref[...] = jnp.concatenate(outs, axis=1).astype(BF16)

    row = pl.BlockSpec((tq, MEM_W), lambda i: (i, 0))
    return pl.pallas_call(
        body, name=name, grid=(t // tq,),
        in_specs=[row, pl.BlockSpec((N_MEM, 2 * MEM_W), lambda i: (0, 0))], out_specs=row,
        out_shape=jax.ShapeDtypeStruct((t, MEM_W), BF16), compiler_params=_cp(1),
    )(zq, kv)


def _mem_bwd(zq, kv, do, name, tq=512):
    t = zq.shape[0]
    tq = min(tq, t)
    nt = t // tq
    scale = MEM_HD ** -0.5

    def body(q_ref, kv_ref, do_ref, dq_ref, out_ref, dkv_ref):
        @pl.when(pl.program_id(0) == 0)
        def _():
            dkv_ref[...] = jnp.zeros_like(dkv_ref)

        ks = [slice(h * MEM_HD, (h + 1) * MEM_HD) for h in range(MEM_HEADS)]
        vs = [slice(MEM_W + h * MEM_HD, MEM_W + (h + 1) * MEM_HD) for h in range(MEM_HEADS)]
        ss = [lax.dot_general(q_ref[:, k], kv_ref[:, k], NT, preferred_element_type=F32) * scale for k in ks]
        dps = [lax.dot_general(do_ref[:, k], kv_ref[:, v], NT, preferred_element_type=F32) for k, v in zip(ks, vs)]
        pbs, dsbs = [], []
        for s, dp in zip(ss, dps):
            e = jnp.exp(s - jnp.max(s, axis=-1, keepdims=True))
            p = e / jnp.sum(e, axis=-1, keepdims=True)
            ds = p * (dp - jnp.sum(p * dp, axis=-1, keepdims=True))
            pbs.append(p.astype(BF16))
            dsbs.append(ds.astype(BF16))
        dq_ref[...] = jnp.concatenate([jnp.dot(dsb, kv_ref[:, k], preferred_element_type=F32) * scale
                                       for dsb, k in zip(dsbs, ks)], axis=1).astype(BF16)
        for dsb, pb, k, v in zip(dsbs, pbs, ks, vs):
            dkv_ref[k, :] += jnp.dot(q_ref[:, k].T, dsb, preferred_element_type=F32) * scale
            dkv_ref[v, :] += jnp.dot(do_ref[:, k].T, pb, preferred_element_type=F32)

        @pl.when(pl.program_id(0) == nt - 1)
        def _():
            out_ref[...] = dkv_ref[...].T

    row = pl.BlockSpec((tq, MEM_W), lambda i: (i, 0))
    full = pl.BlockSpec((N_MEM, 2 * MEM_W), lambda i: (0, 0))
    return pl.pallas_call(
        body, name=name, grid=(nt,), in_specs=[row, full, row], out_specs=[row, full],
        out_shape=[jax.ShapeDtypeStruct((t, MEM_W), BF16), jax.ShapeDtypeStruct((N_MEM, 2 * MEM_W), F32)],
        scratch_shapes=[pltpu.VMEM((2 * MEM_W, N_MEM), F32)], compiler_params=_cp(1),
    )(zq, kv, do)


NT = (((1,), (1,)), ((), ()))


def _branches_fwd(acts, ws, zg, bg, name, tr=512):
    t = zg.shape[0]
    tr = min(tr, t)

    def body(a0, a1, a2, w0, w1, w2, z_ref, b_ref, y0, y1, y2, o_ref):
        merged = None
        for i, (a_ref, w_ref, y_ref) in enumerate(((a0, w0, y0), (a1, w1, y1), (a2, w2, y2))):
            y = lax.dot_general(a_ref[...], w_ref[...], NT, preferred_element_type=F32)
            y_ref[...] = y.astype(BF16)
            g = _sigmoid(z_ref[:, i * D:(i + 1) * D].astype(F32) + b_ref[:, i * D:(i + 1) * D])
            merged = g * y if merged is None else merged + g * y
        o_ref[...] = merged.astype(BF16)

    row = pl.BlockSpec((tr, D), lambda i: (i, 0))
    outs = pl.pallas_call(
        body, name=name, grid=(t // tr,),
        in_specs=[pl.BlockSpec((tr, a.shape[1]), lambda i: (i, 0)) for a in acts]
        + [pl.BlockSpec(w.shape, lambda i: (0, 0)) for w in ws]
        + [pl.BlockSpec((tr, 3 * D), lambda i: (i, 0)), pl.BlockSpec((1, 3 * D), lambda i: (0, 0))],
        out_specs=[row] * 4, out_shape=[jax.ShapeDtypeStruct((t, D), BF16)] * 4, compiler_params=_cp(1),
    )(*acts, *ws, zg, _row(bg))
    return tuple(outs[:3]), outs[3]


def _head_sums(dd):
    return jnp.concatenate(
        [jnp.broadcast_to(jnp.sum(dd[:, h * HEAD_DIM:(h + 1) * HEAD_DIM], axis=-1, keepdims=True), (dd.shape[0], HEAD_DIM))
         for h in range(HEADS)], axis=1)


def _branches_bwd(dymix, w_out, ws, zg, bg, ys, att, name, tr=512):
    t = zg.shape[0]
    tr = min(tr, t)

    def body(dy_ref, wo_ref, w0, w1, w2, z_ref, b_ref, y0, y1, y2, att_ref, d0, d1, d2, dz_ref, db_ref, da0, da1, da2, dl_ref):
        @pl.when(pl.program_id(0) == 0)
        def _():
            db_ref[...] = jnp.zeros_like(db_ref)

        dm = lax.dot_general(dy_ref[...], wo_ref[...], NT, preferred_element_type=F32)
        for i, (y_ref, w_ref, d_ref, da_ref) in enumerate(((y0, w0, d0, da0), (y1, w1, d1, da1), (y2, w2, d2, da2))):
            gi = _sigmoid(z_ref[:, i * D:(i + 1) * D].astype(F32) + b_ref[:, i * D:(i + 1) * D])
            dy = (dm * gi).astype(BF16)
            d_ref[...] = dy
            dz = dm * y_ref[...].astype(F32) * gi * (1.0 - gi)
            dz_ref[:, i * D:(i + 1) * D] = dz.astype(BF16)
            db_ref[0:1, i * D:(i + 1) * D] += jnp.sum(dz, axis=0, keepdims=True)
            da = jnp.dot(dy, w_ref[...], preferred_element_type=F32).astype(BF16)
            da_ref[...] = da
            if i == 1:
                dl_ref[...] = _head_sums(att_ref[...].astype(F32) * da.astype(F32))

    row = pl.BlockSpec((tr, D), lambda i: (i, 0))
    wide = pl.BlockSpec((tr, 3 * D), lambda i: (i, 0))
    heads = pl.BlockSpec((tr, ATT_GW), lambda i: (i, 0))
    outs = pl.pallas_call(
        body, name=name, grid=(t // tr,),
        in_specs=[row, pl.BlockSpec(w_out.shape, lambda i: (0, 0))] + [pl.BlockSpec(w.shape, lambda i: (0, 0)) for w in ws]
        + [wide, pl.BlockSpec((1, 3 * D), lambda i: (0, 0)), row, row, row, heads],
        out_specs=[row, row, row, wide, pl.BlockSpec((8, 3 * D), lambda i: (0, 0))]
        + [pl.BlockSpec((tr, w.shape[1]), lambda i: (i, 0)) for w in ws] + [heads],
        out_shape=[jax.ShapeDtypeStruct((t, D), BF16)] * 3
        + [jax.ShapeDtypeStruct((t, 3 * D), BF16), jax.ShapeDtypeStruct((8, 3 * D), F32)]
        + [jax.ShapeDtypeStruct((t, w.shape[1]), BF16) for w in ws] + [jax.ShapeDtypeStruct((t, ATT_GW), F32)],
        compiler_params=_cp(1),
    )(dymix, w_out, *ws, zg, _row(bg), *ys, att)
    return tuple(outs[:3]), outs[3], outs[4][0], tuple(outs[5:8]), outs[8]


def _adamw(w, g, m, v, name):
    shape = w.shape
    w, g, m, v = (a.reshape(-1, shape[-1]) for a in (w, g, m, v))
    r, d = w.shape
    tr = next((c for c in (512, 352, 256, 128, 64, 32, 16, 8) if r % c == 0), r)

    def body(w_ref, g_ref, m_ref, v_ref, d_ref, nm_ref, nv_ref):
        gg = g_ref[...]
        m2 = ADAM_B1 * m_ref[...] + (1.0 - ADAM_B1) * gg
        v2 = ADAM_B2 * v_ref[...] + (1.0 - ADAM_B2) * (gg * gg)
        m_hat = m2 / (1.0 - ADAM_B1 ** ADAM_STEP)
        v_hat = v2 / (1.0 - ADAM_B2 ** ADAM_STEP)
        d_ref[...] = -ADAM_LR * (m_hat / (jnp.sqrt(v_hat) + ADAM_EPS) + ADAM_WD * w_ref[...])
        nm_ref[...] = m2
        nv_ref[...] = v2

    row = pl.BlockSpec((tr, d), lambda i: (i, 0))
    outs = pl.pallas_call(
        body, name=name, grid=(r // tr,), in_specs=[row] * 4, out_specs=[row] * 3,
        out_shape=[jax.ShapeDtypeStruct((r, d), F32)] * 3, compiler_params=_cp(1),
    )(w, g, m, v)
    return [o.reshape(shape) for o in outs]


def _slab_tile(rows):
    return next((c for c in range(min(rows, 512), 15, -16) if rows % c == 0), rows)


def _sum_slots(parts, out_dtype, name):
    n, r, d = parts.shape
    tr = _slab_tile(r)

    def body(p_ref, o_ref):
        acc = p_ref[0].astype(F32)
        for s in range(1, n):
            acc = acc + p_ref[s].astype(F32)
        o_ref[...] = acc.astype(o_ref.dtype)

    return pl.pallas_call(
        body, name=name, grid=(r // tr,), in_specs=[pl.BlockSpec((n, tr, d), lambda i: (0, i, 0))],
        out_specs=pl.BlockSpec((tr, d), lambda i: (i, 0)),
        out_shape=jax.ShapeDtypeStruct((r, d), out_dtype), compiler_params=_cp(1),
    )(parts)


def _place():
    return lax.axis_index("x"), lax.axis_index("y"), lax.axis_index("c")


def _all_gather(shard, name):
    r, d = shard.shape

    def body(x_ref, out_ref, send_sems, recv_sems, local_sem):
        x, y, c = _place()
        me, sibling = (x, y, c), (x, y, 1 - c)
        chips = [(1 - x, y), (x, 1 - y), (1 - x, 1 - y)]

        def slot(px, py, pc):
            return out_ref.at[4 * px + 2 * py + pc]

        def copy(k, block, to, src=None):
            return pltpu.make_async_remote_copy(
                src_ref=slot(*block) if src is None else src, dst_ref=slot(*block),
                send_sem=send_sems.at[k], recv_sem=recv_sems.at[k], device_id=to, device_id_type=MESH)

        mine = pltpu.make_async_copy(x_ref, slot(*me), local_sem)
        mine.start()
        first = [copy(0, me, sibling, src=x_ref)]
        first += [copy(1 + j, me, (*chip, c), src=x_ref) for j, chip in enumerate(chips)]
        for cp in first:
            cp.start()
        passed = [copy(4 + j, (*chip, c), sibling) for j, chip in enumerate(chips)]
        for j, chip in enumerate(chips):
            copy(1 + j, (*chip, c), me).wait_recv()
            passed[j].start()
        copy(0, sibling, me).wait_recv()
        for j, chip in enumerate(chips):
            copy(4 + j, (*chip, 1 - c), me).wait_recv()
        for cp in first + passed:
            cp.wait_send()
        mine.wait()

    return pl.pallas_call(
        body, name=name, out_shape=jax.ShapeDtypeStruct((N_DEV, r, d), shard.dtype),
        in_specs=[pl.BlockSpec(memory_space=pl.ANY)], out_specs=pl.BlockSpec(memory_space=pl.ANY),
        scratch_shapes=[pltpu.SemaphoreType.DMA((7,)), pltpu.SemaphoreType.DMA((7,)), pltpu.SemaphoreType.DMA],
    )(shard)


HBM_SPEC = pl.BlockSpec(memory_space=pltpu.HBM)
SEM_SPEC = pl.BlockSpec(memory_space=pltpu.SEMAPHORE)
SPLIT_PARAMS = pltpu.CompilerParams(has_side_effects=pltpu.SideEffectType.DATAFLOW_SIDE_EFFECTING)


def _peers():
    x, y, c = _place()
    flip = lambda v, bit: 1 - v if bit else v
    return 4 * x + 2 * y + c, [((flip(x, k >> 2 & 1), flip(y, k >> 1 & 1), flip(c, k & 1))) for k in range(1, N_DEV)]


def _exchange_start(src, all_gather, name):
    r, d = src.shape[-2:]

    def body(src_ref, land_ref, send_sems, recv_sems, src_thru, land_thru, token):
        me, peers = _peers()
        for k, (px, py, pc) in enumerate(peers):
            part = src_ref if all_gather else src_ref.at[4 * px + 2 * py + pc]
            pltpu.make_async_remote_copy(src_ref=part, dst_ref=land_ref.at[me], send_sem=send_sems.at[k],
                                         recv_sem=recv_sems.at[k], device_id=(px, py, pc), device_id_type=MESH).start()
        token[...] = jnp.zeros_like(token)

    land = lax.empty((N_DEV, r, d), src.dtype)
    return pl.pallas_call(
        body, name=name,
        out_shape=(pltpu.SemaphoreType.DMA((N_DEV - 1,)), pltpu.SemaphoreType.DMA((N_DEV - 1,)), pltpu.HBM(src.shape, src.dtype),
                   pltpu.HBM(land.shape, land.dtype), jax.ShapeDtypeStruct((8, LANES), F32)),
        in_specs=(HBM_SPEC, HBM_SPEC), out_specs=(SEM_SPEC, SEM_SPEC, HBM_SPEC, HBM_SPEC, pl.BlockSpec(memory_space=pltpu.VMEM)),
        input_output_aliases={0: 2, 1: 3}, compiler_params=SPLIT_PARAMS,
    )(pltpu.with_memory_space_constraint(src, pltpu.HBM), pltpu.with_memory_space_constraint(land, pltpu.HBM))


def _exchange_wait(started, after, all_gather, name):
    send_sems, recv_sems, src_thru, land_thru, _ = started

    def body(src_ref, land_ref, send_sems, recv_sems, after_ref, src_out, land_out):
        me, peers = _peers()
        for k, (px, py, pc) in enumerate(peers):
            part = src_ref if all_gather else src_ref.at[4 * px + 2 * py + pc]
            cp = pltpu.make_async_remote_copy(src_ref=part, dst_ref=land_ref.at[4 * px + 2 * py + pc], send_sem=send_sems.at[k],
                                              recv_sem=recv_sems.at[k], device_id=(px, py, pc), device_id_type=MESH)
            cp.wait_send()
            cp.wait_recv()

    return pl.pallas_call(
        body, name=name,
        out_shape=(pltpu.HBM(src_thru.shape, src_thru.dtype), pltpu.HBM(land_thru.shape, land_thru.dtype)),
        in_specs=(HBM_SPEC, HBM_SPEC, SEM_SPEC, SEM_SPEC, pl.BlockSpec(memory_space=pl.ANY)), out_specs=(HBM_SPEC, HBM_SPEC),
        input_output_aliases={0: 0, 1: 1}, compiler_params=SPLIT_PARAMS,
    )(src_thru, land_thru, send_sems, recv_sems, after)


def _own_slot(land, mine):
    x, y, c = _place()
    return lax.dynamic_update_slice(land, mine[None], (4 * x + 2 * y + c, 0, 0))


BIG = (("w_in", (D, 864), 1), ("w_conv_out", (CONV_W, 128), 1), ("w_att_out", (ATT_GW, 128), 1), ("w_mem_kv", (128, D), 0),
       ("w_mem_out", (MEM_W, 128), 1), ("w_out", (128, D), 0), ("w_ffn_in", (D, 704), 1), ("w_ffn_out", (352, D), 0),
       ("conv_dw", (CONV_K, 64), 1))
SMALL = (("rel_bias", (NUM_BUCKETS, 12)), ("norm_mix_pre", (DEPTH, D)), ("b_gate", (DEPTH, 3 * D)),
         ("conv_dw_bias", (DEPTH, CONV_W)), ("conv_ln_g", (DEPTH, CONV_W)), ("conv_ln_b", (DEPTH, CONV_W)),
         ("norm_mem", (DEPTH, D)), ("norm_mix_post", (DEPTH, D)), ("norm_ffn_pre", (DEPTH, D)), ("norm_ffn_post", (DEPTH, D)))
TWIN_WEIGHTS = ("rel_bias", "norm_mix_pre", "w_in", "b_gate", "conv_dw", "conv_dw_bias", "conv_ln_g", "conv_ln_b", "w_conv_out",
                "w_att_out", "norm_mem", "w_mem_kv", "w_mem_out", "w_out", "norm_mix_post", "norm_ffn_pre", "w_ffn_in",
                "w_ffn_out", "norm_ffn_post")


BIG_INFO = {n: (s, a) for n, s, a in BIG}
GROUPS = {"in": ("w_in",), "mix": ("conv_dw", "w_conv_out", "w_att_out", "w_mem_kv", "w_mem_out", "w_out"),
          "ffn_in": ("w_ffn_in",), "ffn_out": ("w_ffn_out",)}
TILE_BYTES_PER_LANE = 32


def _rows_of(shape, dtype):
    tile = TILE_BYTES_PER_LANE // jnp.dtype(dtype).itemsize
    return -(-int(np.prod(shape)) // (D * tile)) * tile


def _row_form(shape, axis):
    return tuple(shape) if axis == 0 else (shape[1], shape[0])


def _as_slab_rows(p, lead=()):
    shape = p.shape[len(lead):]
    rows = _rows_of(shape, p.dtype)
    if shape == (rows, D):
        return p
    n = int(np.prod(shape))
    p = jnp.pad(p.reshape(*lead, n), [(0, 0)] * len(lead) + [(0, rows * D - n)])
    return p.reshape(*lead, rows, D)


def _from_slab_rows(rows, shape, lead=()):
    if rows.shape[len(lead):] == tuple(shape):
        return rows
    n = int(np.prod(shape))
    return rows.reshape(*lead, -1)[..., :n].reshape(*lead, *shape)


def _pack(pieces, dtype):
    return jnp.concatenate([_as_slab_rows(p.astype(dtype)) for p in pieces], axis=0)


def _unpack(slab, shapes, packed_as):
    out, r0 = [], 0
    for s in shapes:
        rows = _rows_of(s, packed_as)
        out.append(_from_slab_rows(slab[r0:r0 + rows], s))
        r0 += rows
    return out


def _pack_group(blocks, layer, group):
    return _pack([blocks[n][layer] if BIG_INFO[n][1] == 0 else blocks[n][layer].T for n in GROUPS[group]], BF16)


def _full_weights(gathered, group):
    out, r0 = {}, 0
    for name in GROUPS[group]:
        r, k = _row_form(*BIG_INFO[name])
        rows = _rows_of((r, k), BF16)
        out[name] = _from_slab_rows(gathered[:, r0:r0 + rows], (r, k), lead=(N_DEV,)).reshape(N_DEV * r, k)
        r0 += rows
    return out


def _scatter_layout(grads, group):
    parts = []
    for name in GROUPS[group]:
        r, k = _row_form(*BIG_INFO[name])
        parts.append(_as_slab_rows(grads[name].astype(BF16).reshape(N_DEV, r, k), lead=(N_DEV,)))
    return jnp.concatenate(parts, axis=1)


def _unpack_blocks(slab, group):
    out = _unpack(slab, [_row_form(*BIG_INFO[n]) for n in GROUPS[group]], BF16)
    return {n: b if BIG_INFO[n][1] == 0 else b.T for n, b in zip(GROUPS[group], out)}


def _split_w_in(w_in):
    edges = np.cumsum((0,) + SEG)
    return [w_in[edges[i]:edges[i + 1]] for i in range(4)]


def _conv_taps(conv_dw):
    return jnp.pad(conv_dw.astype(F32).T, ((0, 32 - CONV_K), (0, 0)))


def _layer_fwd(l, x, h, nm, get_w, sm, bias_tabs):
    t = x.shape[0]
    win = get_w(l, "in", h)
    w_seg = _split_w_in(win["w_in"])
    zc = _mm(h, w_seg[0], "fwd_in_conv", tb=True)
    za = _mm(h, w_seg[1], "fwd_in_att", tb=True, tn=1152)
    zq = _mm(h, w_seg[2], "fwd_in_memq", tb=True)
    zg = _mm(h, w_seg[3], "fwd_in_gate", tb=True)
    wts = get_w(l, "mix", zg)
    y, cact = _conv_fwd(zc, _conv_taps(wts["conv_dw"]), sm["conv_dw_bias"][l], sm["conv_ln_g"][l], sm["conv_ln_b"][l],
                        "conv_fwd")
    qkv, outs, lses = [], [], []
    for g, (_, dil) in enumerate(ATT_PATTERNS):
        cols = tuple(i * 3 + g for i in range(3))
        if dil == 1:
            trio, at = (za, za, za), cols
        else:
            trio, at = _to_residue_major([(za, c) for c in cols], dil, f"qkv_residue_d{dil}"), (0, 0, 0)
        o, ls = _att_fwd(*trio, bias_tabs[g], t // dil, f"att_fwd_d{dil}", cols=at)
        qkv.append((trio, at))
        outs.append(o)
        lses.append(ls)
    att, lse = _att_combine(outs, lses, "att_combine")
    kv = _mm(nm, wts["w_mem_kv"], "fwd_mem_kv")
    om = _mem_fwd(zq, kv, "mem_fwd")
    w_br = (wts["w_conv_out"], wts["w_att_out"], wts["w_mem_out"])
    ys, merged = _branches_fwd((cact, att, om), w_br, zg, sm["b_gate"][l], "branches_fwd")
    ymix, x1, h2 = _mm_norm_res(merged, wts["w_out"], x, sm["norm_mix_post"][l], sm["norm_ffn_pre"][l], "fwd_out_norm")
    gff, uff, act = _mm_swiglu_fwd(h2, get_w(l, "ffn_in", h2)["w_ffn_in"], "fwd_ffn_in_swiglu")
    g_next = sm["norm_mix_pre"][l + 1] if l + 1 < DEPTH else None
    f, x2, h_next = _mm_norm_res(act, get_w(l, "ffn_out", act)["w_ffn_out"], x1, sm["norm_ffn_post"][l], g_next,
                                 "fwd_ffn_out_norm" if g_next is not None else "fwd_ffn_out_last")
    saved = dict(x=x, h=h, zc=zc, zq=zq, zg=zg, y=y, cact=cact, qkv=qkv, att=att, lse=lse, kv=kv, om=om, ys=ys,
                 merged=merged, ymix=ymix, x1=x1, h2=h2, gff=gff, uff=uff, act=act, f=f)
    return x2, h_next, saved


def _after(value, *tokens):
    for t in tokens:
        if t is not None:
            value = value + t[0, 0].astype(value.dtype)
    return value


def _layer_bwd(l, dx2, s, nm, mem, get_w, put_g, sm, bias_tabs, bucket_idx):
    t = dx2.shape[0]
    win, wts = get_w(l, "in", None), get_w(l, "mix", None)
    w_ffn_in, w_ffn_out = get_w(l, "ffn_in", None)["w_ffn_in"], get_w(l, "ffn_out", None)["w_ffn_out"]
    gb, gs = {}, {}
    df, gs["norm_ffn_post"] = _norm_bwd(s["f"], sm["norm_ffn_post"][l], dx2, None, BF16, "bwd_norm_ffn_post")
    dg, du = _mm_swiglu_bwd(df, w_ffn_out, s["gff"], s["uff"], "bwd_ffn_out_d_swiglu")
    tok_out = put_g(l, "ffn_out", {"w_ffn_out": _mm(s["act"], df, "bwd_ffn_out_w", ta=True, tm=1408)})
    tok_in = put_g(l, "ffn_in", {"w_ffn_in": jnp.concatenate([_mm(dg, s["h2"], "bwd_ffn_in_w_gate", ta=True, tm=1408),
                                                              _mm(du, s["h2"], "bwd_ffn_in_w_up", ta=True, tm=1408)], axis=0)})
    dh2 = _mm_sum([(dg, _after(w_ffn_in[:FFN_H], tok_out, tok_in), FFN_H // 2), (du, w_ffn_in[FFN_H:], FFN_H // 2)],
                  "bwd_ffn_in_d")
    dx1, dymix, gs["norm_ffn_pre"], gs["norm_mix_post"] = _norm_bwd_pair(
        s["x1"], sm["norm_ffn_pre"][l], dh2, dx2, s["ymix"], sm["norm_mix_post"][l], "bwd_norm_ffn_pre_mix_post")
    gb["w_out"] = _mm(s["merged"], dymix, "bwd_out_w", ta=True)
    w_br = (wts["w_conv_out"], wts["w_att_out"], wts["w_mem_out"])
    dys, dzg, gs["b_gate"], (dcact, datt, dom), delta = _branches_bwd(
        dymix, wts["w_out"], w_br, s["zg"], sm["b_gate"][l], s["ys"], s["att"], "branches_bwd")
    gb["w_conv_out"] = _mm(dys[0], s["cact"], "bwd_conv_out_w", ta=True)
    dzc, dconv, gs["conv_dw_bias"], gs["conv_ln_g"], gs["conv_ln_b"] = _conv_bwd(
        s["zc"], s["y"], dcact, _conv_taps(wts["conv_dw"]), sm["conv_ln_g"][l], sm["conv_ln_b"][l], "conv_bwd")
    gb["conv_dw"] = dconv[:CONV_K].T
    gb["w_att_out"] = _mm(dys[1], s["att"], "bwd_att_out_w", ta=True)
    dqkv, rel = [], []
    for g, (_, dil) in enumerate(ATT_PATTERNS):
        trio, at = s["qkv"][g]
        if dil == 1:
            dl_r, do_r, lse_r = delta, datt, s["lse"]
        else:
            dl_r, do_r, lse_r = _to_residue_major([(delta, 0), (datt, 0), (s["lse"], 0)], dil, f"att_bwd_residue_d{dil}")
        dq, dk, dv, dbias = _att_bwd(*trio, bias_tabs[g], dl_r, do_r, lse_r, t // dil, f"att_bwd_d{dil}", cols=at)
        dqkv.append((dq, dk, dv))
        rel.append(_bias_grad(dbias, bucket_idx[g], f"bias_grad_d{dil}")[:HEADS, :NUM_BUCKETS])
    dza = _assemble_dza(dqkv, "att_bwd_assemble")
    gs["rel_bias"] = jnp.concatenate(rel, axis=0).T
    gb["w_mem_out"] = _mm(dys[2], s["om"], "bwd_mem_out_w", ta=True)
    dzq, dkv = _mem_bwd(s["zq"], s["kv"], dom, "mem_bwd")
    dkv = dkv.astype(BF16)
    gb["w_mem_kv"] = _mm(nm, dkv, "bwd_mem_kv_w", ta=True)
    tok = put_g(l, "mix", gb)
    dnm = _mm(dkv, wts["w_mem_kv"], "bwd_mem_kv_d", tb=True, out_dtype=F32)
    _, gs["norm_mem"] = _norm_bwd(mem, sm["norm_mem"][l], dnm, None, BF16, "bwd_norm_mem")
    segs = ((dzc, "conv", 1024), (dza, "att", 1152), (_after(dzq, tok), "memq", 512), (dzg, "gate", 1024))
    tok = put_g(l, "in", {"w_in": jnp.concatenate([_mm(dz, s["h"], f"bwd_in_{nm_}_w", ta=True, tm=min(blk, 1152), tk=2048)
                                                   for dz, nm_, blk in segs], axis=0)})
    w_seg = _split_w_in(win["w_in"])
    w_seg[0] = _after(w_seg[0], tok)
    dx, gs["norm_mix_pre"] = _mm_sum([(dz, w_seg[i], blk) for i, (dz, _, blk) in enumerate(segs)], "bwd_in_d_norm",
                                     norm=(s["x"], sm["norm_mix_pre"][l], dx1))
    return dx, gs


def _local_step(x, mem, target, sm, get_w, put_g, tokens=()):
    t = x.shape[0]
    bias_tabs, bucket_idx = [], []
    for g, (_, dil) in enumerate(ATT_PATTERNS):
        idx = _bucket_index(_att_tiles(t // dil)[1], dil)
        bucket_idx.append(jnp.asarray(idx))
        bias_tabs.append(_bias_table(sm["rel_bias"][:, g * HEADS:(g + 1) * HEADS], bucket_idx[g], f"bias_table_d{dil}"))
    h = _norm_plain(x, _after(sm["norm_mix_pre"][0], *tokens), "norm_first")
    saved, nms = [], []
    for l in range(DEPTH):
        nm = _norm_plain(mem, sm["norm_mem"][l], "norm_mem")
        x, h, s = _layer_fwd(l, x, h, nm, get_w, sm, bias_tabs)
        saved.append(s)
        nms.append(nm)
    loss, dx = _loss_head(x, target, "loss_head")
    gsmall = {}
    for l in reversed(range(DEPTH)):
        dx, gs = _layer_bwd(l, dx, saved[l], nms[l], mem, get_w, put_g, sm, bias_tabs, bucket_idx)
        for n, v in gs.items():
            gsmall.setdefault(n, {})[l] = v
    small = {}
    for n, _ in SMALL:
        small[n] = gsmall[n][0] + gsmall[n][1] if n == "rel_bias" else jnp.stack([gsmall[n][0], gsmall[n][1]])
    return loss, dx, small


def kernel(x, mem, rel_bias, norm_mix_pre, w_in, b_gate, conv_dw, conv_dw_bias, conv_ln_g, conv_ln_b, w_conv_out, w_att_out, norm_mem, w_mem_kv, w_mem_out, w_out, norm_mix_post, norm_ffn_pre, w_ffn_in, w_ffn_out, norm_ffn_post, loss_target, m_rel_bias, m_norm_mix_pre, m_w_in, m_b_gate, m_conv_dw, m_conv_dw_bias, m_conv_ln_g, m_conv_ln_b, m_w_conv_out, m_w_att_out, m_norm_mem, m_w_mem_kv, m_w_mem_out, m_w_out, m_norm_mix_post, m_norm_ffn_pre, m_w_ffn_in, m_w_ffn_out, m_norm_ffn_post, v_rel_bias, v_norm_mix_pre, v_w_in, v_b_gate, v_conv_dw, v_conv_dw_bias, v_conv_ln_g, v_conv_ln_b, v_w_conv_out, v_w_att_out, v_norm_mem, v_w_mem_kv, v_w_mem_out, v_w_out, v_norm_mix_post, v_norm_ffn_pre, v_w_ffn_in, v_w_ffn_out, v_norm_ffn_post):
    args = dict(locals())
    w = {n: args[n] for n in TWIN_WEIGHTS}
    m = {n: args["m_" + n] for n in TWIN_WEIGHTS}
    v = {n: args["v_" + n] for n in TWIN_WEIGHTS}
    sm = {n: w[n] for n, _ in SMALL}

    first = (0, "in")
    keys = [(l, grp) for l in range(DEPTH) for grp in GROUPS]
    shard = {(l, grp): _pack_group(w, l, grp) for l, grp in keys}
    weights = {first: _full_weights(_all_gather(shard[first], "gather_l0_in"), "in")}
    settled = weights[first]["w_in"][:1, :1] * 0
    ag = {}
    for k in keys:
        if k != first:
            ag[k] = _exchange_start(_after(shard[k], settled), True, f"ag_start_l{k[0]}_{k[1]}")
            settled = ag[k][4]

    def get_w(l, grp, after):
        if (l, grp) not in weights:
            mine, land = _exchange_wait(ag[l, grp], after, True, f"ag_wait_l{l}_{grp}")
            weights[l, grp] = _full_weights(_own_slot(land, mine), grp)
        return weights[l, grp]

    rs = {}

    def put_g(l, grp, grads):
        rs[l, grp] = _exchange_start(_scatter_layout(grads, grp), False, f"rs_start_l{l}_{grp}")
        return rs[l, grp][4]

    loss, dx, gsmall = _local_step(x[0], mem[0], loss_target[0], sm, get_w, put_g, [st[4] for st in ag.values()])
    loss = lax.psum(loss, ("x", "y", "c"))

    xi, yi, ci = _place()
    me = 4 * xi + 2 * yi + ci
    g_slab = {}
    for k, st in rs.items():
        contrib, land = _exchange_wait(st, dx, False, f"rs_wait_l{k[0]}_{k[1]}")
        own = lax.dynamic_index_in_dim(contrib, me, axis=0, keepdims=False)
        g_slab[k] = _sum_slots(_own_slot(land, own), F32, f"rs_sum_l{k[0]}_{k[1]}")
    g_layers = [{n: b for grp in GROUPS for n, b in _unpack_blocks(g_slab[l, grp], grp).items()} for l in range(DEPTH)]
    small_shapes = [s for _, s in SMALL]
    g_small = _unpack(_sum_slots(_all_gather(_pack([gsmall[n] for n, _ in SMALL], F32), "gather_small"), F32, "sum_small"),
                      small_shapes, F32)
    grads = {n: jnp.stack([g_layers[0][n], g_layers[1][n]]) for n, _, _ in BIG}
    grads.update({n: g_small[i] for i, (n, _) in enumerate(SMALL)})

    delta, new_m, new_v = {}, {}, {}
    for n in TWIN_WEIGHTS:
        delta[n], new_m[n], new_v[n] = _adamw(w[n], grads[n], m[n], v[n], f"adamw_{n}")

    return (loss, dx[None], *[grads[n] for n in TWIN_WEIGHTS], *[delta[n] for n in TWIN_WEIGHTS],
            *[new_m[n] for n in TWIN_WEIGHTS], *[new_v[n] for n in TWIN_WEIGHTS])
```

```python
import functools

import numpy as np
import jax
import jax.numpy as jnp
from jax import lax
from jax.experimental import pallas as pl
from jax.experimental.pallas import tpu as pltpu

F32 = jnp.float32
BF16 = jnp.bfloat16

N_DEV = 8
D = 1024
DEPTH = 2
CONV_W = 512
CONV_K = 31
CONV_PAD = 16
ATT_PATTERNS = ((128, 1), (512, 4), (2048, 16))
ATT_RADIUS = 64
HEADS = 4
HEAD_DIM = 64
ATT_GW = HEADS * HEAD_DIM
ATT_W = 3 * ATT_GW
MEM_HEADS = 4
MEM_HD = 128
MEM_W = 512
N_MEM = 256
FFN_H = 2816
NUM_BUCKETS = 32
MAX_DISTANCE = 1024
RMS_EPS = 1e-6
LN_EPS = 1e-5
NEG_INF = -1e30
SEG = (2 * CONV_W, 3 * ATT_W, MEM_W, 3 * D)
ADAM_LR, ADAM_B1, ADAM_B2, ADAM_EPS, ADAM_WD, ADAM_STEP = 0.001, 0.9, 0.999, 1e-08, 0.01, 10

VMEM_LIMIT_V7X = 56 * 1024 * 1024
MESH = pl.DeviceIdType.MESH


def _cp(n_axes):
    return pltpu.CompilerParams(dimension_semantics=("arbitrary",) * n_axes, vmem_limit_bytes=VMEM_LIMIT_V7X)


def _row(v):
    return v.reshape(1, -1)


def _mm(a, b, name, *, ta=False, tb=False, out_dtype=BF16, tm=2048, tn=1024, tk=2048):
    kdim, m = a.shape if ta else a.shape[::-1]
    n, kb = b.shape if tb else b.shape[::-1]
    assert kb == kdim and a.dtype == BF16 and b.dtype == BF16
    tm, tn, tk = min(tm, m), min(tn, n), min(tk, kdim)
    assert m % tm == 0 and n % tn == 0 and kdim % tk == 0, (name, m, n, kdim, tm, tn, tk)
    nk = kdim // tk
    a_spec = pl.BlockSpec((tk, tm), lambda i, j, k: (k, i)) if ta else pl.BlockSpec((tm, tk), lambda i, j, k: (i, k))
    b_spec = pl.BlockSpec((tn, tk), lambda i, j, k: (j, k)) if tb else pl.BlockSpec((tk, tn), lambda i, j, k: (k, j))
    o_spec = pl.BlockSpec((tm, tn), lambda i, j, k: (i, j))
    dims = (((0 if ta else 1,), (1 if tb else 0,)), ((), ()))
    use_scratch = nk > 1 and out_dtype != F32

    def body(*refs):
        a_ref, b_ref, o_ref = refs[:3]
        p = lax.dot_general(a_ref[...], b_ref[...], dims, preferred_element_type=F32)
        if nk == 1:
            o_ref[...] = p.astype(o_ref.dtype)
            return
        k = pl.program_id(2)
        t_ref = refs[3] if use_scratch else o_ref

        @pl.when(k == 0)
        def _():
            t_ref[...] = p

        @pl.when(k > 0)
        def _():
            t_ref[...] += p

        if use_scratch:
            @pl.when(k == nk - 1)
            def _():
                o_ref[...] = t_ref[...].astype(o_ref.dtype)

    return pl.pallas_call(
        body, name=name, grid=(m // tm, n // tn, nk), in_specs=[a_spec, b_spec], out_specs=o_spec,
        out_shape=jax.ShapeDtypeStruct((m, n), out_dtype),
        scratch_shapes=[pltpu.VMEM((tm, tn), F32)] if use_scratch else [],
        compiler_params=_cp(3),
    )(a, b)


EPILOGUE_COLS = 512
EPILOGUE_ROWS = 512


def _chunks(size, step):
    return [(c, min(c + step, size)) for c in range(0, size, step)]


def _rms_bwd_val(v, g, dy):
    r = lax.rsqrt(jnp.mean(v * v, axis=-1, keepdims=True) + RMS_EPS)
    vhat = v * r
    dvh = dy * g
    dv = r * (dvh - vhat * jnp.mean(dvh * vhat, axis=-1, keepdims=True))
    return dv, jnp.sum(dy * vhat, axis=0, keepdims=True)


def _mm_sum(pairs, name, *, norm=None, tm=1024):
    m, n = pairs[0][0].shape[0], pairs[0][1].shape[1]
    tm = min(tm, m)
    starts, specs, total = [], [], 0
    for a, b, tk in pairs:
        assert a.shape == (m, b.shape[0]) and b.shape[1] == n and a.shape[1] % tk == 0 and a.dtype == b.dtype == BF16
        nk = a.shape[1] // tk
        starts.append((total, nk))
        step = functools.partial(lambda k, s0, nk_: jnp.clip(k - s0, 0, nk_ - 1), s0=total, nk_=nk)
        specs.append(pl.BlockSpec((tm, tk), functools.partial(lambda i, k, st: (i, st(k)), st=step)))
        specs.append(pl.BlockSpec(memory_space=pl.ANY))
        total += nk
    np_ = 2 * len(pairs)
    row = pl.BlockSpec((tm, n), lambda i, k: (i, 0))
    resident = [pltpu.VMEM(b.shape, BF16) for _, b, _ in pairs] + [pltpu.SemaphoreType.DMA((len(pairs),))]

    def load_weights(refs, wres, sem):
        @pl.when((pl.program_id(0) == 0) & (pl.program_id(1) == 0))
        def _():
            copies = [pltpu.make_async_copy(refs[2 * p + 1], wres[p], sem.at[p]) for p in range(len(pairs))]
            for cp in copies:
                cp.start()
            for cp in copies:
                cp.wait()

    def weight(wres, p, kk):
        tk = pairs[p][2]
        return wres[p][pl.ds(pl.multiple_of(kk * tk, tk), tk), :]

    def accumulate(refs, wres, acc, k, upto=total):
        for p, (s0, nk) in enumerate(starts):
            @pl.when((k >= s0) & (k < min(s0 + nk, upto)))
            def _(p=p, s0=s0):
                part = jnp.dot(refs[2 * p][...], weight(wres, p, k - s0), preferred_element_type=F32)
                if s0 == 0:
                    @pl.when(k == 0)
                    def _():
                        acc[...] = part

                    @pl.when(k > 0)
                    def _():
                        acc[...] += part
                else:
                    acc[...] += part

    operands = [op for a, b, _ in pairs for op in (a, b)]
    if norm is None:
        def plain(*refs):
            k = pl.program_id(1)
            wres, sem = refs[np_ + 2:np_ + 2 + len(pairs)], refs[-1]
            load_weights(refs, wres, sem)
            accumulate(refs, wres, refs[np_ + 1], k)

            @pl.when(k == total - 1)
            def _():
                refs[np_][...] = refs[np_ + 1][...].astype(BF16)

        return pl.pallas_call(
            plain, name=name, grid=(m // tm, total), in_specs=specs, out_specs=row,
            out_shape=jax.ShapeDtypeStruct((m, n), BF16), scratch_shapes=[pltpu.VMEM((tm, n), F32)] + resident,
            compiler_params=_cp(2),
        )(*operands)

    v, g, dres = norm

    def body(*refs):
        v_ref, g_ref, dres_ref, dv_ref, dg_ref, acc = refs[np_:np_ + 6]
        wres, sem = refs[np_ + 6:np_ + 6 + len(pairs)], refs[-1]
        i, k = pl.program_id(0), pl.program_id(1)
        load_weights(refs, wres, sem)
        accumulate(refs, wres, acc, k, upto=total - 1)

        @pl.when((i == 0) & (k == 0))
        def _():
            dg_ref[...] = jnp.zeros_like(dg_ref)

        @pl.when(k == total - 1)
        def _():
            a_ref, b_last = refs[np_ - 2], weight(wres, len(pairs) - 1, starts[-1][1] - 1)
            for r0, r1 in _chunks(tm, EPILOGUE_ROWS):
                dy = jnp.dot(a_ref[r0:r1, :], b_last, preferred_element_type=F32)
                if total > 1:
                    dy = dy + acc[r0:r1, :]
                dv, dg = _rms_bwd_val(v_ref[r0:r1, :], g_ref[...], dy)
                dv_ref[r0:r1, :] = dv + dres_ref[r0:r1, :]
                dg_ref[0:1, :] += dg

    once = pl.BlockSpec((tm, n), lambda i, k: (i, 0), pipeline_mode=pl.Buffered(1))
    dv, dg = pl.pallas_call(
        body, name=name, grid=(m // tm, total),
        in_specs=specs + [once, pl.BlockSpec((1, n), lambda i, k: (0, 0)), once],
        out_specs=[row, pl.BlockSpec((8, n), lambda i, k: (0, 0))],
        out_shape=[jax.ShapeDtypeStruct((m, n), F32), jax.ShapeDtypeStruct((8, n), F32)],
        scratch_shapes=[pltpu.VMEM((tm, n), F32)] + resident, compiler_params=_cp(2),
    )(*operands, v, _row(g), dres)
    return dv, dg[0]


def _mm_swiglu_fwd(h, w, name, *, tm=1024, tn=1408):
    m, kdim = h.shape
    hid = w.shape[0] // 2
    tm = min(tm, m)
    nj = hid // tn
    dims = (((1,), (1,)), ((), ()))

    def body(h_ref, wg_ref, wu_ref, g_ref, u_ref, a_ref):
        hv = h_ref[...]
        for c0, c1 in _chunks(tn, EPILOGUE_COLS):
            g = lax.dot_general(hv, wg_ref[c0:c1, :], dims, preferred_element_type=F32)
            u = lax.dot_general(hv, wu_ref[c0:c1, :], dims, preferred_element_type=F32)
            g_ref[:, c0:c1] = g.astype(BF16)
            u_ref[:, c0:c1] = u.astype(BF16)
            a_ref[:, c0:c1] = (g * _sigmoid(g) * u).astype(BF16)

    out = pl.BlockSpec((tm, tn), lambda i, j: (i, j))
    return pl.pallas_call(
        body, name=name, grid=(m // tm, nj),
        in_specs=[pl.BlockSpec((tm, kdim), lambda i, j: (i, 0)), pl.BlockSpec((tn, kdim), lambda i, j: (j, 0)),
                  pl.BlockSpec((tn, kdim), lambda i, j: (j + nj, 0))],
        out_specs=[out, out, out], out_shape=[jax.ShapeDtypeStruct((m, hid), BF16)] * 3, compiler_params=_cp(2),
    )(h, w, w)


def _mm_swiglu_bwd(df, w, g, u, name, *, tm=1024, tn=1408):
    m, kdim = df.shape
    hid = w.shape[0]
    tm = min(tm, m)

    def body(df_ref, w_ref, g_ref, u_ref, dg_ref, du_ref):
        dfv = df_ref[...]
        for c0, c1 in _chunks(tn, EPILOGUE_COLS):
            da = lax.dot_general(dfv, w_ref[c0:c1, :], (((1,), (1,)), ((), ())), preferred_element_type=F32)
            gg, uu, dab = g_ref[:, c0:c1], u_ref[:, c0:c1], da.astype(BF16)
            sg = _sigmoid(gg)
            dg_ref[:, c0:c1] = dab * uu * (sg * (1.0 + gg * (1.0 - sg)))
            du_ref[:, c0:c1] = dab * gg * sg

    blk = pl.BlockSpec((tm, tn), lambda i, j: (i, j))
    return pl.pallas_call(
        body, name=name, grid=(m // tm, hid // tn),
        in_specs=[pl.BlockSpec((tm, kdim), lambda i, j: (i, 0)), pl.BlockSpec((tn, kdim), lambda i, j: (j, 0)), blk, blk],
        out_specs=[blk, blk], out_shape=[jax.ShapeDtypeStruct((m, hid), BF16)] * 2, compiler_params=_cp(2),
    )(df, w, g, u)


def _mm_norm_res(a, b, xres, g, g_next, name, *, tm=1024):
    m, kdim = a.shape
    n = b.shape[1]
    tm = min(tm, m)
    two = g_next is not None

    def body(*refs):
        a_ref, b_ref, x_ref, g_ref = refs[:4]
        y_ref, xn_ref = refs[4 + two], refs[5 + two]
        for r0, r1 in _chunks(tm, EPILOGUE_ROWS):
            y = jnp.dot(a_ref[r0:r1, :], b_ref[...], preferred_element_type=F32)
            y_ref[r0:r1, :] = y.astype(BF16)
            xn = x_ref[r0:r1, :] + _rms_val(y, g_ref[...])
            xn_ref[r0:r1, :] = xn
            if two:
                refs[7][r0:r1, :] = _rms_val(xn, refs[4][...]).astype(BF16)

    row = pl.BlockSpec((tm, n), lambda i: (i, 0))
    vec = pl.BlockSpec((1, n), lambda i: (0, 0))
    outs = pl.pallas_call(
        body, name=name, grid=(m // tm,),
        in_specs=[pl.BlockSpec((tm, kdim), lambda i: (i, 0)), pl.BlockSpec((kdim, n), lambda i: (0, 0)), row, vec] + ([vec] if two else []),
        out_specs=[row, row] + ([row] if two else []),
        out_shape=[jax.ShapeDtypeStruct((m, n), BF16), jax.ShapeDtypeStruct((m, n), F32)]
        + ([jax.ShapeDtypeStruct((m, n), BF16)] if two else []),
        compiler_params=_cp(1),
    )(*((a, b, xres, _row(g)) + ((_row(g_next),) if two else ())))
    return (outs[0], outs[1], outs[2]) if two else (outs[0], outs[1], None)


def _rms_val(v, g):
    return v * lax.rsqrt(jnp.mean(v * v, axis=-1, keepdims=True) + RMS_EPS) * g


def _norm_plain(v, g, name, tr=512):
    t, d = v.shape
    tr = min(tr, t)

    def body(v_ref, g_ref, o_ref):
        o_ref[...] = _rms_val(v_ref[...], g_ref[...]).astype(BF16)

    return pl.pallas_call(
        body, name=name, grid=(t // tr,),
        in_specs=[pl.BlockSpec((tr, d), lambda i: (i, 0)), pl.BlockSpec((1, d), lambda i: (0, 0))],
        out_specs=pl.BlockSpec((tr, d), lambda i: (i, 0)),
        out_shape=jax.ShapeDtypeStruct((t, d), BF16), compiler_params=_cp(1),
    )(v, _row(g))


def _norm_bwd(v, g, dout, dres, out_dtype, name, tr=1024):
    t, d = v.shape
    tr = min(tr, t)
    nt = t // tr
    has_res = dres is not None

    def body(*refs):
        v_ref, g_ref, do_ref = refs[:3]
        dv_ref, dg_ref = refs[3 + has_res], refs[4 + has_res]
        i = pl.program_id(0)
        vv = v_ref[...].astype(F32)
        dy = do_ref[...].astype(F32)
        r = lax.rsqrt(jnp.mean(vv * vv, axis=-1, keepdims=True) + RMS_EPS)
        vhat = vv * r
        dvh = dy * g_ref[...]
        dv = r * (dvh - vhat * jnp.mean(dvh * vhat, axis=-1, keepdims=True))
        if has_res:
            dv = dv + refs[3][...]
        dv_ref[...] = dv.astype(dv_ref.dtype)
        part = jnp.sum(dy * vhat, axis=0, keepdims=True)

        @pl.when(i == 0)
        def _():
            dg_ref[...] = jnp.zeros_like(dg_ref)

        dg_ref[0:1, :] += part

    row = pl.BlockSpec((tr, d), lambda i: (i, 0))
    dv, dg = pl.pallas_call(
        body, name=name, grid=(nt,),
        in_specs=[row, pl.BlockSpec((1, d), lambda i: (0, 0)), row] + ([row] if has_res else []),
        out_specs=[row, pl.BlockSpec((8, d), lambda i: (0, 0))],
        out_shape=[jax.ShapeDtypeStruct((t, d), out_dtype), jax.ShapeDtypeStruct((8, d), F32)],
        compiler_params=_cp(1),
    )(*((v, _row(g), dout) + ((dres,) if has_res else ())))
    return dv, dg[0]


def _norm_bwd_pair(v1, g1, dout, dres, v2, g2, name, tr=1024):
    t, d = v1.shape
    tr = min(tr, t)

    def body(v1_ref, g1_ref, do_ref, dr_ref, v2_ref, g2_ref, d1_ref, d2_ref, dg1_ref, dg2_ref):
        @pl.when(pl.program_id(0) == 0)
        def _():
            dg1_ref[...] = jnp.zeros_like(dg1_ref)
            dg2_ref[...] = jnp.zeros_like(dg2_ref)

        dv1, dg1 = _rms_bwd_val(v1_ref[...].astype(F32), g1_ref[...], do_ref[...].astype(F32))
        d1 = dv1 + dr_ref[...]
        d1_ref[...] = d1
        dv2, dg2 = _rms_bwd_val(v2_ref[...].astype(F32), g2_ref[...], d1)
        d2_ref[...] = dv2.astype(BF16)
        dg1_ref[0:1, :] += dg1
        dg2_ref[0:1, :] += dg2

    row = pl.BlockSpec((tr, d), lambda i: (i, 0))
    vec = pl.BlockSpec((1, d), lambda i: (0, 0))
    acc = pl.BlockSpec((8, d), lambda i: (0, 0))
    d1, d2, dg1, dg2 = pl.pallas_call(
        body, name=name, grid=(t // tr,), in_specs=[row, vec, row, row, row, vec], out_specs=[row, row, acc, acc],
        out_shape=[jax.ShapeDtypeStruct((t, d), F32), jax.ShapeDtypeStruct((t, d), BF16),
                   jax.ShapeDtypeStruct((8, d), F32), jax.ShapeDtypeStruct((8, d), F32)],
        compiler_params=_cp(1),
    )(v1, _row(g1), dout, dres, v2, _row(g2))
    return d1, d2, dg1[0], dg2[0]


def _loss_head(y, target, name, tr=1024):
    t, d = y.shape
    tr = min(tr, t)
    nt = t // tr

    def body(y_ref, t_ref, dy_ref, l_ref):
        i = pl.program_id(0)
        err = y_ref[...] - t_ref[...]
        dy_ref[...] = err * (1.0 / d)

        @pl.when(i == 0)
        def _():
            l_ref[...] = jnp.zeros_like(l_ref)

        l_ref[...] += jnp.sum(err * err) * (0.5 / d)

    row = pl.BlockSpec((tr, d), lambda i: (i, 0))
    dy, l = pl.pallas_call(
        body, name=name, grid=(nt,), in_specs=[row, row],
        out_specs=[row, pl.BlockSpec((8, 128), lambda i: (0, 0))],
        out_shape=[jax.ShapeDtypeStruct((t, d), F32), jax.ShapeDtypeStruct((8, 128), F32)],
        compiler_params=_cp(1),
    )(y, target)
    return l[0, 0], dy


def _sigmoid(v):
    return 1.0 / (1.0 + jnp.exp(-v))


def _halo_specs(tq, width, nt, halo):
    per = tq // halo
    last = nt * per - 1
    main = pl.BlockSpec((tq, width), lambda i: (i, 0))
    prev = pl.BlockSpec((halo, width), lambda i: (jnp.maximum(i * per - 1, 0), 0))
    nxt = pl.BlockSpec((halo, width), lambda i: (jnp.minimum((i + 1) * per, last), 0))
    return prev, main, nxt


def _glu_window(zp, zm, zn, i, nt):
    def glu(z):
        z = z.astype(F32)
        return z[:, :CONV_W] * _sigmoid(z[:, CONV_W:])

    up = jnp.where(i > 0, glu(zp), 0.0)
    un = jnp.where(i < nt - 1, glu(zn), 0.0)
    return jnp.concatenate([up, glu(zm), un], axis=0)


def _shifted(win, shift, rows):
    if shift == 0:
        return win[:rows]
    return pltpu.roll(win, win.shape[0] - shift, 0)[:rows]


def _conv_fwd(zc, w, b, ln_g, ln_b, name, tq=512):
    t = zc.shape[0]
    tq = min(tq, t)
    nt = t // tq

    def body(zp_ref, zm_ref, zn_ref, w_ref, b_ref, g_ref, bb_ref, y_ref, c_ref):
        i = pl.program_id(0)
        win = _glu_window(zp_ref[...], zm_ref[...], zn_ref[...], i, nt)
        wv = w_ref[...]
        y = jnp.zeros((tq, CONV_W), F32) + b_ref[...]
        for k in range(CONV_K):
            y = y + _shifted(win, k + 1, tq) * wv[k:k + 1, :]
        y_ref[...] = y
        mu = jnp.mean(y, axis=-1, keepdims=True)
        yc = y - mu
        ln = yc * lax.rsqrt(jnp.mean(yc * yc, axis=-1, keepdims=True) + LN_EPS) * g_ref[...] + bb_ref[...]
        c_ref[...] = (ln * _sigmoid(ln)).astype(BF16)

    vec = pl.BlockSpec((1, CONV_W), lambda i: (0, 0))
    out = pl.BlockSpec((tq, CONV_W), lambda i: (i, 0))
    return pl.pallas_call(
        body, name=name, grid=(nt,),
        in_specs=[*_halo_specs(tq, 2 * CONV_W, nt, CONV_PAD), pl.BlockSpec((32, CONV_W), lambda i: (0, 0)), vec, vec, vec],
        out_specs=[out, out],
        out_shape=[jax.ShapeDtypeStruct((t, CONV_W), F32), jax.ShapeDtypeStruct((t, CONV_W), BF16)],
        compiler_params=_cp(1),
    )(zc, zc, zc, w, _row(b), _row(ln_g), _row(ln_b))


def _conv_bwd(zc, y, dc, w, ln_g, ln_b, name, tq=256):
    t = zc.shape[0]
    tq = min(tq, t)
    nt = t // tq
    rows = tq + 2 * CONV_PAD

    def body(zp_ref, zm_ref, zn_ref, yp_ref, ym_ref, yn_ref, dp_ref, dm_ref, dn_ref, w_ref, g_ref, bb_ref,
             dz_ref, dw_ref, db_ref, dg_ref, dbb_ref):
        i = pl.program_id(0)
        uwin = _glu_window(zp_ref[...], zm_ref[...], zn_ref[...], i, nt)
        ywin = jnp.concatenate([yp_ref[...], ym_ref[...], yn_ref[...]], axis=0)
        dcw = jnp.concatenate([dp_ref[...], dm_ref[...], dn_ref[...]], axis=0).astype(F32)
        mu = jnp.mean(ywin, axis=-1, keepdims=True)
        yc = ywin - mu
        rstd = lax.rsqrt(jnp.mean(yc * yc, axis=-1, keepdims=True) + LN_EPS)
        yhat = yc * rstd
        ln = yhat * g_ref[...] + bb_ref[...]
        sg = _sigmoid(ln)
        dl = dcw * (sg * (1.0 + ln * (1.0 - sg)))
        ridx = lax.broadcasted_iota(jnp.int32, (rows, 1), 0)
        inside = ((ridx >= CONV_PAD) | (i > 0)) & ((ridx < tq + CONV_PAD) | (i < nt - 1))
        dl = jnp.where(inside, dl, 0.0)
        dyh = dl * g_ref[...]
        dy = rstd * (dyh - jnp.mean(dyh, axis=-1, keepdims=True) - yhat * jnp.mean(dyh * yhat, axis=-1, keepdims=True))
        dy = jnp.where(inside, dy, 0.0)
        main = slice(CONV_PAD, CONV_PAD + tq)
        dlm, yhm, dym = dl[main], yhat[main], dy[main]

        @pl.when(i == 0)
        def _():
            dw_ref[...] = jnp.zeros_like(dw_ref)
            db_ref[...] = jnp.zeros_like(db_ref)
            dg_ref[...] = jnp.zeros_like(dg_ref)
            dbb_ref[...] = jnp.zeros_like(dbb_ref)

        dg_ref[0:1, :] += jnp.sum(dlm * yhm, axis=0, keepdims=True)
        dbb_ref[0:1, :] += jnp.sum(dlm, axis=0, keepdims=True)
        db_ref[0:1, :] += jnp.sum(dym, axis=0, keepdims=True)
        wv = w_ref[...]
        du = jnp.zeros((tq, CONV_W), F32)
        for k in range(CONV_K):
            du = du + _shifted(dy, 2 * CONV_PAD - 1 - k, tq) * wv[k:k + 1, :]
            dw_ref[k:k + 1, :] += jnp.sum(dym * _shifted(uwin, k + 1, tq), axis=0, keepdims=True)
        zm = zm_ref[...].astype(F32)
        a, gt = zm[:, :CONV_W], zm[:, CONV_W:]
        sgt = _sigmoid(gt)
        dz_ref[:, :CONV_W] = (du * sgt).astype(BF16)
        dz_ref[:, CONV_W:] = (du * a * sgt * (1.0 - sgt)).astype(BF16)

    vec = pl.BlockSpec((1, CONV_W), lambda i: (0, 0))
    acc = pl.BlockSpec((8, CONV_W), lambda i: (0, 0))
    dz, dw, db, dg, dbb = pl.pallas_call(
        body, name=name, grid=(nt,),
        in_specs=[*_halo_specs(tq, 2 * CONV_W, nt, CONV_PAD), *_halo_specs(tq, CONV_W, nt, CONV_PAD),
                  *_halo_specs(tq, CONV_W, nt, CONV_PAD), pl.BlockSpec((32, CONV_W), lambda i: (0, 0)), vec, vec],
        out_specs=[pl.BlockSpec((tq, 2 * CONV_W), lambda i: (i, 0)), pl.BlockSpec((32, CONV_W), lambda i: (0, 0)), acc, acc, acc],
        out_shape=[jax.ShapeDtypeStruct((t, 2 * CONV_W), BF16), jax.ShapeDtypeStruct((32, CONV_W), F32),
                   jax.ShapeDtypeStruct((8, CONV_W), F32), jax.ShapeDtypeStruct((8, CONV_W), F32),
                   jax.ShapeDtypeStruct((8, CONV_W), F32)],
        compiler_params=_cp(1),
    )(zc, zc, zc, y, y, y, dc, dc, dc, w, _row(ln_g), _row(ln_b))
    return dz, dw, db[0], dg[0], dbb[0]


def _att_tiles(seq_len):
    tq = min(512, seq_len)
    return tq, min(128, tq)


def _t5_bucket_np(rel):
    nb = NUM_BUCKETS // 2
    max_exact = nb // 2
    ret = np.where(rel > 0, nb, 0)
    n = np.abs(rel)
    nf = np.maximum(n, 1).astype(np.float32)
    large = max_exact + (np.log(nf / np.float32(max_exact)) / np.float32(np.log(MAX_DISTANCE / max_exact))
                         * np.float32(nb - max_exact)).astype(np.int32)
    large = np.minimum(large, nb - 1)
    return ret + np.where(n < max_exact, n, large)


def _bucket_index(sb, dilation):
    off = np.arange(sb + 2 * ATT_RADIUS)[None, :] - ATT_RADIUS - np.arange(sb)[:, None]
    idx = _t5_bucket_np(off * dilation).astype(np.int32)
    return np.where(np.abs(off) <= ATT_RADIUS, idx, -1).astype(np.int32)


def _bias_table(tab, idx, name):
    sb, w = idx.shape

    def body(tab_ref, idx_ref, o_ref):
        ix = idx_ref[...]
        for h in range(HEADS):
            acc = jnp.full((sb, w), NEG_INF, F32)
            for b in range(NUM_BUCKETS):
                acc = jnp.where(ix == b, tab_ref[b, h], acc)
            o_ref[h] = acc

    return pl.pallas_call(
        body, name=name, out_shape=jax.ShapeDtypeStruct((HEADS, sb, w), F32),
        in_specs=[pl.BlockSpec(memory_space=pltpu.SMEM), pl.BlockSpec(memory_space=pltpu.VMEM)],
        out_specs=pl.BlockSpec(memory_space=pltpu.VMEM),
    )(tab, idx)


def _qkv_specs(tq, per, last_blk, cols, tile):
    def main(col):
        return pl.BlockSpec((tq, ATT_GW), lambda *g: (tile(*g), col))

    def prev(col):
        return pl.BlockSpec((ATT_RADIUS, ATT_GW), lambda *g: (jnp.maximum(tile(*g) * per - 1, 0), col))

    def nxt(col):
        return pl.BlockSpec((ATT_RADIUS, ATT_GW), lambda *g: (jnp.minimum((tile(*g) + 1) * per, last_blk), col))

    cq, ck, cv = cols
    return [main(cq), prev(ck), main(ck), nxt(ck), prev(cv), main(cv), nxt(cv)]


def _att_fwd(q, k, v, bias, seq_len, name, cols=(0, 0, 0)):
    t = q.shape[0]
    tq, sb = _att_tiles(seq_len)
    nbl = seq_len // tq
    nt = t // tq
    per = tq // ATT_RADIUS
    w = sb + 2 * ATT_RADIUS
    last_blk = t // ATT_RADIUS - 1

    def body(q_ref, kp_ref, kc_ref, kn_ref, vp_ref, vc_ref, vn_ref, b_ref, o_ref, l_ref):
        n = pl.program_id(0) % nbl
        kw = jnp.concatenate([kp_ref[...], kc_ref[...], kn_ref[...]], axis=0)
        vw = jnp.concatenate([vp_ref[...], vc_ref[...], vn_ref[...]], axis=0)
        kpos = n * tq - ATT_RADIUS + lax.broadcasted_iota(jnp.int32, (1, tq + 2 * ATT_RADIUS), 1)
        valid = (kpos >= 0) & (kpos < seq_len)
        for j in range(tq // sb):
            qj = q_ref[j * sb:(j + 1) * sb, :]
            kj, vj, okj = kw[j * sb:j * sb + w], vw[j * sb:j * sb + w], valid[:, j * sb:j * sb + w]
            outs, lses = [], []
            for h in range(HEADS):
                hs = slice(h * HEAD_DIM, (h + 1) * HEAD_DIM)
                s = lax.dot_general(qj[:, hs], kj[:, hs], NT, preferred_element_type=F32)
                s = jnp.where(okj, s * (HEAD_DIM ** -0.5) + b_ref[h], NEG_INF)
                m = jnp.max(s, axis=-1, keepdims=True)
                e = jnp.exp(s - m)
                den = jnp.sum(e, axis=-1, keepdims=True)
                outs.append(jnp.dot(e.astype(BF16), vj[:, hs], preferred_element_type=F32) / den)
                lses.append(jnp.broadcast_to(m + jnp.log(den), (sb, HEAD_DIM)))
            o_ref[j * sb:(j + 1) * sb, :] = jnp.concatenate(outs, axis=1).astype(BF16)
            l_ref[j * sb:(j + 1) * sb, :] = jnp.concatenate(lses, axis=1)

    main = pl.BlockSpec((tq, ATT_GW), lambda i: (i, 0))
    return pl.pallas_call(
        body, name=name, grid=(nt,),
        in_specs=[*_qkv_specs(tq, per, last_blk, cols, lambda i: i), pl.BlockSpec((HEADS, sb, w), lambda i: (0, 0, 0))],
        out_specs=[main, main],
        out_shape=[jax.ShapeDtypeStruct((t, ATT_GW), BF16), jax.ShapeDtypeStruct((t, ATT_GW), F32)],
        compiler_params=_cp(1),
    )(q, k, k, k, v, v, v, bias)


PERM_ROWS = 2048
LANES = 128


def _spec_rm(d, tr, width, col=0):
    return pl.BlockSpec((d, tr // d, width), lambda i: (0, i, col))


def _rm_view(a, d):
    return a.reshape(d, a.shape[0] // d, a.shape[1])


def _gather_residues(scr, val, d, out_ref, col0):
    tr, w = val.shape
    for c in range(w // LANES):
        scr[...] = val[:, c * LANES:(c + 1) * LANES].astype(F32)
        for r in range(d):
            out_ref[r, :, col0 + c * LANES:col0 + (c + 1) * LANES] = scr[pl.ds(r, tr // d, stride=d), :].astype(out_ref.dtype)


def _scatter_residues(scr, ref, d):
    w = ref.shape[2]
    cols = []
    for c in range(w // LANES):
        for r in range(d):
            scr[pl.ds(r, scr.shape[0] // d, stride=d), :] = ref[r, :, c * LANES:(c + 1) * LANES].astype(F32)
        cols.append(scr[...])
    return jnp.concatenate(cols, axis=1)


def _to_residue_major(srcs, d, name, tr=PERM_ROWS):
    t = srcs[0][0].shape[0]
    tr = min(tr, t)
    n = len(srcs)

    def body(*refs):
        scr = refs[2 * n]
        for k in range(n):
            _gather_residues(scr, refs[k][...], d, refs[n + k], 0)

    outs = pl.pallas_call(
        body, name=name, grid=(t // tr,),
        in_specs=[pl.BlockSpec((tr, ATT_GW), functools.partial(lambda i, col: (i, col), col=col)) for _, col in srcs],
        out_specs=[_spec_rm(d, tr, ATT_GW)] * n,
        out_shape=[jax.ShapeDtypeStruct((d, t // d, ATT_GW), a.dtype) for a, _ in srcs],
        scratch_shapes=[pltpu.VMEM((tr, LANES), F32)], compiler_params=_cp(1),
    )(*[a for a, _ in srcs])
    return [o.reshape(t, ATT_GW) for o in outs]


def _att_combine(outs, lses, name, tr=PERM_ROWS):
    t = outs[0].shape[0]
    tr = min(tr, t)
    dils = [d for _, d in ATT_PATTERNS]

    def body(o0, o1, o2, l0, l1, l2, a_ref, lse_ref, scr):
        os_, ls_ = [], []
        for o_ref, l_ref, d in zip((o0, o1, o2), (l0, l1, l2), dils):
            if d == 1:
                os_.append(o_ref[...].astype(F32))
                ls_.append(l_ref[...])
            else:
                os_.append(_scatter_residues(scr, o_ref, d))
                ls_.append(_scatter_residues(scr, l_ref, d))
        la, lb, lc = ls_
        m = jnp.maximum(jnp.maximum(la, lb), lc)
        ea, eb, ec = jnp.exp(la - m), jnp.exp(lb - m), jnp.exp(lc - m)
        den = ea + eb + ec
        a_ref[...] = ((ea * os_[0] + eb * os_[1] + ec * os_[2]) / den).astype(BF16)
        lse_ref[...] = m + jnp.log(den)

    row = pl.BlockSpec((tr, ATT_GW), lambda i: (i, 0))
    specs = [row if d == 1 else _spec_rm(d, tr, ATT_GW) for d in dils]
    views = lambda arrs: [a if d == 1 else _rm_view(a, d) for a, d in zip(arrs, dils)]
    return pl.pallas_call(
        body, name=name, grid=(t // tr,), in_specs=specs * 2, out_specs=[row, row],
        out_shape=[jax.ShapeDtypeStruct((t, ATT_GW), BF16), jax.ShapeDtypeStruct((t, ATT_GW), F32)],
        scratch_shapes=[pltpu.VMEM((tr, LANES), F32)], compiler_params=_cp(1),
    )(*views(outs), *views(lses))


def _assemble_dza(dqkv, name, tr=PERM_ROWS):
    t = dqkv[0][0].shape[0]
    tr = min(tr, t)
    dils = [d for _, d in ATT_PATTERNS]

    def body(*refs):
        o_ref, scr = refs[9], refs[10]
        for g, d in enumerate(dils):
            for c in range(3):
                ref = refs[g * 3 + c]
                val = ref[...] if d == 1 else _scatter_residues(scr, ref, d).astype(BF16)
                o_ref[:, (c * 3 + g) * ATT_GW:(c * 3 + g + 1) * ATT_GW] = val

    row = pl.BlockSpec((tr, ATT_GW), lambda i: (i, 0))
    specs = [row if d == 1 else _spec_rm(d, tr, ATT_GW) for d in dils for _ in range(3)]
    args = [a if d == 1 else _rm_view(a, d) for trio, d in zip(dqkv, dils) for a in trio]
    return pl.pallas_call(
        body, name=name, grid=(t // tr,), in_specs=specs, out_specs=pl.BlockSpec((tr, 3 * ATT_W), lambda i: (i, 0)),
        out_shape=jax.ShapeDtypeStruct((t, 3 * ATT_W), BF16),
        scratch_shapes=[pltpu.VMEM((tr, LANES), F32)], compiler_params=_cp(1),
    )(*args)


def _att_bwd(q, k, v, bias, delta, do, lse, seq_len, name, cols=(0, 0, 0)):
    t = q.shape[0]
    tq, sb = _att_tiles(seq_len)
    nbl = seq_len // tq
    n_res = t // seq_len
    per = tq // ATT_RADIUS
    w = sb + 2 * ATT_RADIUS
    last_blk = t // ATT_RADIUS - 1
    acc_cols = 2 * tq - sb + w
    scale = HEAD_DIM ** -0.5

    def body(q_ref, kp_ref, kc_ref, kn_ref, vp_ref, vc_ref, vn_ref, b_ref, dl_ref, do_ref, l_ref,
             dq_ref, dk_ref, dv_ref, db_ref, ak_ref, av_ref):
        r, n = pl.program_id(0), pl.program_id(1)

        @pl.when((r == 0) & (n == 0))
        def _():
            db_ref[...] = jnp.zeros_like(db_ref)

        @pl.when(n == 0)
        def _():
            ak_ref[...] = jnp.zeros_like(ak_ref)
            av_ref[...] = jnp.zeros_like(av_ref)

        @pl.when(n < nbl)
        def _():
            kw = jnp.concatenate([kp_ref[...], kc_ref[...], kn_ref[...]], axis=0)
            vw = jnp.concatenate([vp_ref[...], vc_ref[...], vn_ref[...]], axis=0)
            kpos = n * tq - ATT_RADIUS + lax.broadcasted_iota(jnp.int32, (1, tq + 2 * ATT_RADIUS), 1)
            valid = (kpos >= 0) & (kpos < seq_len)
            for j in range(tq // sb):
                rows = slice(j * sb, (j + 1) * sb)
                qj, doj = q_ref[rows, :], do_ref[rows, :]
                dlj = dl_ref[rows, :]
                lj = l_ref[rows, :]
                kj, vj, okj = kw[j * sb:j * sb + w], vw[j * sb:j * sb + w], valid[:, j * sb:j * sb + w]
                heads = [slice(h * HEAD_DIM, (h + 1) * HEAD_DIM) for h in range(HEADS)]
                ss = [lax.dot_general(qj[:, hs], kj[:, hs], NT, preferred_element_type=F32) for hs in heads]
                dps = [lax.dot_general(doj[:, hs], vj[:, hs], NT, preferred_element_type=F32) for hs in heads]
                ps, dss, dsbs = [], [], []
                for h, hs in enumerate(heads):
                    s = jnp.where(okj, ss[h] * scale + b_ref[h], NEG_INF)
                    p = jnp.exp(s - lj[:, h * HEAD_DIM:h * HEAD_DIM + 1])
                    ds = p * (dps[h] - dlj[:, h * HEAD_DIM:h * HEAD_DIM + 1])
                    ps.append(p.astype(BF16))
                    dss.append(ds)
                    dsbs.append(ds.astype(BF16))
                dqs = [jnp.dot(dsbs[h], kj[:, hs], preferred_element_type=F32) * scale for h, hs in enumerate(heads)]
                dks = [lax.dot_general(qj[:, hs], dsbs[h], (((0,), (0,)), ((), ())), preferred_element_type=F32) * scale
                       for h, hs in enumerate(heads)]
                dvs = [lax.dot_general(doj[:, hs], ps[h], (((0,), (0,)), ((), ())), preferred_element_type=F32)
                       for h, hs in enumerate(heads)]
                dq_ref[rows, :] = jnp.concatenate(dqs, axis=1).astype(BF16)
                c0 = tq + j * sb
                ak_ref[:, c0:c0 + w] += jnp.concatenate(dks, axis=0)
                av_ref[:, c0:c0 + w] += jnp.concatenate(dvs, axis=0)
                for h in range(HEADS):
                    db_ref[h] += dss[h]

        dk_ref[...] = ak_ref[:, ATT_RADIUS:ATT_RADIUS + tq].T.astype(BF16)
        dv_ref[...] = av_ref[:, ATT_RADIUS:ATT_RADIUS + tq].T.astype(BF16)
        keep = acc_cols - tq
        nk, nv = ak_ref[:, tq:acc_cols], av_ref[:, tq:acc_cols]
        ak_ref[:, 0:keep] = nk
        av_ref[:, 0:keep] = nv
        ak_ref[:, keep:acc_cols] = jnp.zeros((ATT_GW, tq), F32)
        av_ref[:, keep:acc_cols] = jnp.zeros((ATT_GW, tq), F32)

    def tile(r, n):
        return r * nbl + jnp.minimum(n, nbl - 1)

    main = pl.BlockSpec((tq, ATT_GW), lambda r, n: (tile(r, n), 0))
    lag = pl.BlockSpec((tq, ATT_GW), lambda r, n: (r * nbl + jnp.maximum(n - 1, 0), 0))
    bspec = pl.BlockSpec((HEADS, sb, w), lambda r, n: (0, 0, 0))
    return pl.pallas_call(
        body, name=name, grid=(n_res, nbl + 1),
        in_specs=[*_qkv_specs(tq, per, last_blk, cols, tile), bspec, main, main, main],
        out_specs=[main, lag, lag, bspec],
        out_shape=[jax.ShapeDtypeStruct((t, ATT_GW), BF16)] * 3 + [jax.ShapeDtypeStruct((HEADS, sb, w), F32)],
        scratch_shapes=[pltpu.VMEM((ATT_GW, acc_cols), F32), pltpu.VMEM((ATT_GW, acc_cols), F32)],
        compiler_params=_cp(2),
    )(q, k, k, k, v, v, v, bias, delta, do, lse)


def _bias_grad(db, idx, name):
    _, sb, w = db.shape

    def body(db_ref, idx_ref, o_ref):
        ix = idx_ref[...]
        lane = lax.broadcasted_iota(jnp.int32, (1, 128), 1)
        rows = []
        for h in range(HEADS):
            d = db_ref[h]
            acc = jnp.zeros((1, 128), F32)
            for b in range(NUM_BUCKETS):
                acc = acc + jnp.where(lane == b, jnp.sum(jnp.where(ix == b, d, 0.0)), 0.0)
            rows.append(acc)
        o_ref[...] = jnp.concatenate(rows + [jnp.zeros((8 - HEADS, 128), F32)], axis=0)

    return pl.pallas_call(
        body, name=name, out_shape=jax.ShapeDtypeStruct((8, 128), F32),
        in_specs=[pl.BlockSpec(memory_space=pltpu.VMEM), pl.BlockSpec(memory_space=pltpu.VMEM)],
        out_specs=pl.BlockSpec(memory_space=pltpu.VMEM),
    )(db, idx)


def _mem_fwd(zq, kv, name, tq=512):
    t = zq.shape[0]
    tq = min(tq, t)
    scale = MEM_HD ** -0.5

    def body(q_ref, kv_ref, o_ref):
        outs = []
        for h in range(MEM_HEADS):
            hs = slice(h * MEM_HD, (h + 1) * MEM_HD)
            kh = kv_ref[:, h * MEM_HD:(h + 1) * MEM_HD]
            vh = kv_ref[:, MEM_W + h * MEM_HD:MEM_W + (h + 1) * MEM_HD]
            s = lax.dot_general(q_ref[:, hs], kh, (((1,), (1,)), ((), ())), preferred_element_type=F32) * scale
            e = jnp.exp(s - jnp.max(s, axis=-1, keepdims=True))
            den = jnp.sum(e, axis=-1, keepdims=True)
            outs.append(jnp.dot(e.astype(BF16), vh, preferred_element_type=F32) / den)
        o_ref[...] = jnp.concatenate(outs, axis=1).astype(BF16)

    row = pl.BlockSpec((tq, MEM_W), lambda i: (i, 0))
    return pl.pallas_call(
        body, name=name, grid=(t // tq,),
        in_specs=[row, pl.BlockSpec((N_MEM, 2 * MEM_W), lambda i: (0, 0))], out_specs=row,
        out_shape=jax.ShapeDtypeStruct((t, MEM_W), BF16), compiler_params=_cp(1),
    )(zq, kv)


def _mem_bwd(zq, kv, do, name, tq=512):
    t = zq.shape[0]
    tq = min(tq, t)
    nt = t // tq
    scale = MEM_HD ** -0.5

    def body(q_ref, kv_ref, do_ref, dq_ref, out_ref, dkv_ref):
        @pl.when(pl.program_id(0) == 0)
        def _():
            dkv_ref[...] = jnp.zeros_like(dkv_ref)

        ks = [slice(h * MEM_HD, (h + 1) * MEM_HD) for h in range(MEM_HEADS)]
        vs = [slice(MEM_W + h * MEM_HD, MEM_W + (h + 1) * MEM_HD) for h in range(MEM_HEADS)]
        ss = [lax.dot_general(q_ref[:, k], kv_ref[:, k], NT, preferred_element_type=F32) * scale for k in ks]
        dps = [lax.dot_general(do_ref[:, k], kv_ref[:, v], NT, preferred_element_type=F32) for k, v in zip(ks, vs)]
        pbs, dsbs = [], []
        for s, dp in zip(ss, dps):
            e = jnp.exp(s - jnp.max(s, axis=-1, keepdims=True))
            p = e / jnp.sum(e, axis=-1, keepdims=True)
            ds = p * (dp - jnp.sum(p * dp, axis=-1, keepdims=True))
            pbs.append(p.astype(BF16))
            dsbs.append(ds.astype(BF16))
        dq_ref[...] = jnp.concatenate([jnp.dot(dsb, kv_ref[:, k], preferred_element_type=F32) * scale
                                       for dsb, k in zip(dsbs, ks)], axis=1).astype(BF16)
        for dsb, pb, k, v in zip(dsbs, pbs, ks, vs):
            dkv_ref[k, :] += jnp.dot(q_ref[:, k].T, dsb, preferred_element_type=F32) * scale
            dkv_ref[v, :] += jnp.dot(do_ref[:, k].T, pb, preferred_element_type=F32)

        @pl.when(pl.program_id(0) == nt - 1)
        def _():
            out_ref[...] = dkv_ref[...].T

    row = pl.BlockSpec((tq, MEM_W), lambda i: (i, 0))
    full = pl.BlockSpec((N_MEM, 2 * MEM_W), lambda i: (0, 0))
    return pl.pallas_call(
        body, name=name, grid=(nt,), in_specs=[row, full, row], out_specs=[row, full],
        out_shape=[jax.ShapeDtypeStruct((t, MEM_W), BF16), jax.ShapeDtypeStruct((N_MEM, 2 * MEM_W), F32)],
        scratch_shapes=[pltpu.VMEM((2 * MEM_W, N_MEM), F32)], compiler_params=_cp(1),
    )(zq, kv, do)


NT = (((1,), (1,)), ((), ()))


def _branches_fwd(acts, ws, zg, bg, name, tr=512):
    t = zg.shape[0]
    tr = min(tr, t)

    def body(a0, a1, a2, w0, w1, w2, z_ref, b_ref, y0, y1, y2, o_ref):
        merged = None
        for i, (a_ref, w_ref, y_ref) in enumerate(((a0, w0, y0), (a1, w1, y1), (a2, w2, y2))):
            y = lax.dot_general(a_ref[...], w_ref[...], NT, preferred_element_type=F32)
            y_ref[...] = y.astype(BF16)
            g = _sigmoid(z_ref[:, i * D:(i + 1) * D].astype(F32) + b_ref[:, i * D:(i + 1) * D])
            merged = g * y if merged is None else merged + g * y
        o_ref[...] = merged.astype(BF16)

    row = pl.BlockSpec((tr, D), lambda i: (i, 0))
    outs = pl.pallas_call(
        body, name=name, grid=(t // tr,),
        in_specs=[pl.BlockSpec((tr, a.shape[1]), lambda i: (i, 0)) for a in acts]
        + [pl.BlockSpec(w.shape, lambda i: (0, 0)) for w in ws]
        + [pl.BlockSpec((tr, 3 * D), lambda i: (i, 0)), pl.BlockSpec((1, 3 * D), lambda i: (0, 0))],
        out_specs=[row] * 4, out_shape=[jax.ShapeDtypeStruct((t, D), BF16)] * 4, compiler_params=_cp(1),
    )(*acts, *ws, zg, _row(bg))
    return tuple(outs[:3]), outs[3]


def _head_sums(dd):
    return jnp.concatenate(
        [jnp.broadcast_to(jnp.sum(dd[:, h * HEAD_DIM:(h + 1) * HEAD_DIM], axis=-1, keepdims=True), (dd.shape[0], HEAD_DIM))
         for h in range(HEADS)], axis=1)


def _branches_bwd(dymix, w_out, ws, zg, bg, ys, att, name, tr=512):
    t = zg.shape[0]
    tr = min(tr, t)

    def body(dy_ref, wo_ref, w0, w1, w2, z_ref, b_ref, y0, y1, y2, att_ref, d0, d1, d2, dz_ref, db_ref, da0, da1, da2, dl_ref):
        @pl.when(pl.program_id(0) == 0)
        def _():
            db_ref[...] = jnp.zeros_like(db_ref)

        dm = lax.dot_general(dy_ref[...], wo_ref[...], NT, preferred_element_type=F32)
        for i, (y_ref, w_ref, d_ref, da_ref) in enumerate(((y0, w0, d0, da0), (y1, w1, d1, da1), (y2, w2, d2, da2))):
            gi = _sigmoid(z_ref[:, i * D:(i + 1) * D].astype(F32) + b_ref[:, i * D:(i + 1) * D])
            dy = (dm * gi).astype(BF16)
            d_ref[...] = dy
            dz = dm * y_ref[...].astype(F32) * gi * (1.0 - gi)
            dz_ref[:, i * D:(i + 1) * D] = dz.astype(BF16)
            db_ref[0:1, i * D:(i + 1) * D] += jnp.sum(dz, axis=0, keepdims=True)
            da = jnp.dot(dy, w_ref[...], preferred_element_type=F32).astype(BF16)
            da_ref[...] = da
            if i == 1:
                dl_ref[...] = _head_sums(att_ref[...].astype(F32) * da.astype(F32))

    row = pl.BlockSpec((tr, D), lambda i: (i, 0))
    wide = pl.BlockSpec((tr, 3 * D), lambda i: (i, 0))
    heads = pl.BlockSpec((tr, ATT_GW), lambda i: (i, 0))
    outs = pl.pallas_call(
        body, name=name, grid=(t // tr,),
        in_specs=[row, pl.BlockSpec(w_out.shape, lambda i: (0, 0))] + [pl.BlockSpec(w.shape, lambda i: (0, 0)) for w in ws]
        + [wide, pl.BlockSpec((1, 3 * D), lambda i: (0, 0)), row, row, row, heads],
        out_specs=[row, row, row, wide, pl.BlockSpec((8, 3 * D), lambda i: (0, 0))]
        + [pl.BlockSpec((tr, w.shape[1]), lambda i: (i, 0)) for w in ws] + [heads],
        out_shape=[jax.ShapeDtypeStruct((t, D), BF16)] * 3
        + [jax.ShapeDtypeStruct((t, 3 * D), BF16), jax.ShapeDtypeStruct((8, 3 * D), F32)]
        + [jax.ShapeDtypeStruct((t, w.shape[1]), BF16) for w in ws] + [jax.ShapeDtypeStruct((t, ATT_GW), F32)],
        compiler_params=_cp(1),
    )(dymix, w_out, *ws, zg, _row(bg), *ys, att)
    return tuple(outs[:3]), outs[3], outs[4][0], tuple(outs[5:8]), outs[8]


def _adamw(w, g, m, v, name):
    shape = w.shape
    w, g, m, v = (a.reshape(-1, shape[-1]) for a in (w, g, m, v))
    r, d = w.shape
    tr = next((c for c in (512, 352, 256, 128, 64, 32, 16, 8) if r % c == 0), r)

    def body(w_ref, g_ref, m_ref, v_ref, d_ref, nm_ref, nv_ref):
        gg = g_ref[...]
        m2 = ADAM_B1 * m_ref[...] + (1.0 - ADAM_B1) * gg
        v2 = ADAM_B2 * v_ref[...] + (1.0 - ADAM_B2) * (gg * gg)
        m_hat = m2 / (1.0 - ADAM_B1 ** ADAM_STEP)
        v_hat = v2 / (1.0 - ADAM_B2 ** ADAM_STEP)
        d_ref[...] = -ADAM_LR * (m_hat / (jnp.sqrt(v_hat) + ADAM_EPS) + ADAM_WD * w_ref[...])
        nm_ref[...] = m2
        nv_ref[...] = v2

    row = pl.BlockSpec((tr, d), lambda i: (i, 0))
    outs = pl.pallas_call(
        body, name=name, grid=(r // tr,), in_specs=[row] * 4, out_specs=[row] * 3,
        out_shape=[jax.ShapeDtypeStruct((r, d), F32)] * 3, compiler_params=_cp(1),
    )(w, g, m, v)
    return [o.reshape(shape) for o in outs]


def _slab_tile(rows):
    return next((c for c in range(min(rows, 512), 15, -16) if rows % c == 0), rows)


def _sum_slots(parts, out_dtype, name):
    n, r, d = parts.shape
    tr = _slab_tile(r)

    def body(p_ref, o_ref):
        acc = p_ref[0].astype(F32)
        for s in range(1, n):
            acc = acc + p_ref[s].astype(F32)
        o_ref[...] = acc.astype(o_ref.dtype)

    return pl.pallas_call(
        body, name=name, grid=(r // tr,), in_specs=[pl.BlockSpec((n, tr, d), lambda i: (0, i, 0))],
        out_specs=pl.BlockSpec((tr, d), lambda i: (i, 0)),
        out_shape=jax.ShapeDtypeStruct((r, d), out_dtype), compiler_params=_cp(1),
    )(parts)


def _place():
    return lax.axis_index("x"), lax.axis_index("y"), lax.axis_index("c")


def _all_gather(shard, name):
    r, d = shard.shape

    def body(x_ref, out_ref, send_sems, recv_sems, local_sem):
        x, y, c = _place()
        me, sibling = (x, y, c), (x, y, 1 - c)
        chips = [(1 - x, y), (x, 1 - y), (1 - x, 1 - y)]

        def slot(px, py, pc):
            return out_ref.at[4 * px + 2 * py + pc]

        def copy(k, block, to, src=None):
            return pltpu.make_async_remote_copy(
                src_ref=slot(*block) if src is None else src, dst_ref=slot(*block),
                send_sem=send_sems.at[k], recv_sem=recv_sems.at[k], device_id=to, device_id_type=MESH)

        mine = pltpu.make_async_copy(x_ref, slot(*me), local_sem)
        mine.start()
        first = [copy(0, me, sibling, src=x_ref)]
        first += [copy(1 + j, me, (*chip, c), src=x_ref) for j, chip in enumerate(chips)]
        for cp in first:
            cp.start()
        passed = [copy(4 + j, (*chip, c), sibling) for j, chip in enumerate(chips)]
        for j, chip in enumerate(chips):
            copy(1 + j, (*chip, c), me).wait_recv()
            passed[j].start()
        copy(0, sibling, me).wait_recv()
        for j, chip in enumerate(chips):
            copy(4 + j, (*chip, 1 - c), me).wait_recv()
        for cp in first + passed:
            cp.wait_send()
        mine.wait()

    return pl.pallas_call(
        body, name=name, out_shape=jax.ShapeDtypeStruct((N_DEV, r, d), shard.dtype),
        in_specs=[pl.BlockSpec(memory_space=pl.ANY)], out_specs=pl.BlockSpec(memory_space=pl.ANY),
        scratch_shapes=[pltpu.SemaphoreType.DMA((7,)), pltpu.SemaphoreType.DMA((7,)), pltpu.SemaphoreType.DMA],
    )(shard)


HBM_SPEC = pl.BlockSpec(memory_space=pltpu.HBM)
SEM_SPEC = pl.BlockSpec(memory_space=pltpu.SEMAPHORE)
SPLIT_PARAMS = pltpu.CompilerParams(has_side_effects=pltpu.SideEffectType.DATAFLOW_SIDE_EFFECTING)


def _peers():
    x, y, c = _place()
    flip = lambda v, bit: 1 - v if bit else v
    return 4 * x + 2 * y + c, [((flip(x, k >> 2 & 1), flip(y, k >> 1 & 1), flip(c, k & 1))) for k in range(1, N_DEV)]


def _exchange_start(src, all_gather, name):
    r, d = src.shape[-2:]

    def body(src_ref, land_ref, send_sems, recv_sems, src_thru, land_thru, token):
        me, peers = _peers()
        for k, (px, py, pc) in enumerate(peers):
            part = src_ref if all_gather else src_ref.at[4 * px + 2 * py + pc]
            pltpu.make_async_remote_copy(src_ref=part, dst_ref=land_ref.at[me], send_sem=send_sems.at[k],
                                         recv_sem=recv_sems.at[k], device_id=(px, py, pc), device_id_type=MESH).start()
        token[...] = jnp.zeros_like(token)

    land = lax.empty((N_DEV, r, d), src.dtype)
    return pl.pallas_call(
        body, name=name,
        out_shape=(pltpu.SemaphoreType.DMA((N_DEV - 1,)), pltpu.SemaphoreType.DMA((N_DEV - 1,)), pltpu.HBM(src.shape, src.dtype),
                   pltpu.HBM(land.shape, land.dtype), jax.ShapeDtypeStruct((8, LANES), F32)),
        in_specs=(HBM_SPEC, HBM_SPEC), out_specs=(SEM_SPEC, SEM_SPEC, HBM_SPEC, HBM_SPEC, pl.BlockSpec(memory_space=pltpu.VMEM)),
        input_output_aliases={0: 2, 1: 3}, compiler_params=SPLIT_PARAMS,
    )(pltpu.with_memory_space_constraint(src, pltpu.HBM), pltpu.with_memory_space_constraint(land, pltpu.HBM))


def _exchange_wait(started, after, all_gather, name):
    send_sems, recv_sems, src_thru, land_thru, _ = started

    def body(src_ref, land_ref, send_sems, recv_sems, after_ref, src_out, land_out):
        me, peers = _peers()
        for k, (px, py, pc) in enumerate(peers):
            part = src_ref if all_gather else src_ref.at[4 * px + 2 * py + pc]
            cp = pltpu.make_async_remote_copy(src_ref=part, dst_ref=land_ref.at[4 * px + 2 * py + pc], send_sem=send_sems.at[k],
                                              recv_sem=recv_sems.at[k], device_id=(px, py, pc), device_id_type=MESH)
            cp.wait_send()
            cp.wait_recv()

    return pl.pallas_call(
        body, name=name,
        out_shape=(pltpu.HBM(src_thru.shape, src_thru.dtype), pltpu.HBM(land_thru.shape, land_thru.dtype)),
        in_specs=(HBM_SPEC, HBM_SPEC, SEM_SPEC, SEM_SPEC, pl.BlockSpec(memory_space=pl.ANY)), out_specs=(HBM_SPEC, HBM_SPEC),
        input_output_aliases={0: 0, 1: 1}, compiler_params=SPLIT_PARAMS,
    )(src_thru, land_thru, send_sems, recv_sems, after)


def _own_slot(land, mine):
    x, y, c = _place()
    return lax.dynamic_update_slice(land, mine[None], (4 * x + 2 * y + c, 0, 0))


BIG = (("w_in", (D, 864), 1), ("w_conv_out", (CONV_W, 128), 1), ("w_att_out", (ATT_GW, 128), 1), ("w_mem_kv", (128, D), 0),
       ("w_mem_out", (MEM_W, 128), 1), ("w_out", (128, D), 0), ("w_ffn_in", (D, 704), 1), ("w_ffn_out", (352, D), 0),
       ("conv_dw", (CONV_K, 64), 1))
SMALL = (("rel_bias", (NUM_BUCKETS, 12)), ("norm_mix_pre", (DEPTH, D)), ("b_gate", (DEPTH, 3 * D)),
         ("conv_dw_bias", (DEPTH, CONV_W)), ("conv_ln_g", (DEPTH, CONV_W)), ("conv_ln_b", (DEPTH, CONV_W)),
         ("norm_mem", (DEPTH, D)), ("norm_mix_post", (DEPTH, D)), ("norm_ffn_pre", (DEPTH, D)), ("norm_ffn_post", (DEPTH, D)))
TWIN_WEIGHTS = ("rel_bias", "norm_mix_pre", "w_in", "b_gate", "conv_dw", "conv_dw_bias", "conv_ln_g", "conv_ln_b", "w_conv_out",
                "w_att_out", "norm_mem", "w_mem_kv", "w_mem_out", "w_out", "norm_mix_post", "norm_ffn_pre", "w_ffn_in",
                "w_ffn_out", "norm_ffn_post")


BIG_INFO = {n: (s, a) for n, s, a in BIG}
GROUPS = {"in": ("w_in",), "mix": ("conv_dw", "w_conv_out", "w_att_out", "w_mem_kv", "w_mem_out", "w_out"),
          "ffn_in": ("w_ffn_in",), "ffn_out": ("w_ffn_out",)}
TILE_BYTES_PER_LANE = 32


def _rows_of(shape, dtype):
    tile = TILE_BYTES_PER_LANE // jnp.dtype(dtype).itemsize
    return -(-int(np.prod(shape)) // (D * tile)) * tile


def _row_form(shape, axis):
    return tuple(shape) if axis == 0 else (shape[1], shape[0])


def _as_slab_rows(p, lead=()):
    shape = p.shape[len(lead):]
    rows = _rows_of(shape, p.dtype)
    if shape == (rows, D):
        return p
    n = int(np.prod(shape))
    p = jnp.pad(p.reshape(*lead, n), [(0, 0)] * len(lead) + [(0, rows * D - n)])
    return p.reshape(*lead, rows, D)


def _from_slab_rows(rows, shape, lead=()):
    if rows.shape[len(lead):] == tuple(shape):
        return rows
    n = int(np.prod(shape))
    return rows.reshape(*lead, -1)[..., :n].reshape(*lead, *shape)


def _pack(pieces, dtype):
    return jnp.concatenate([_as_slab_rows(p.astype(dtype)) for p in pieces], axis=0)


def _unpack(slab, shapes, packed_as):
    out, r0 = [], 0
    for s in shapes:
        rows = _rows_of(s, packed_as)
        out.append(_from_slab_rows(slab[r0:r0 + rows], s))
        r0 += rows
    return out


def _pack_group(blocks, layer, group):
    return _pack([blocks[n][layer] if BIG_INFO[n][1] == 0 else blocks[n][layer].T for n in GROUPS[group]], BF16)


def _full_weights(gathered, group):
    out, r0 = {}, 0
    for name in GROUPS[group]:
        r, k = _row_form(*BIG_INFO[name])
        rows = _rows_of((r, k), BF16)
        out[name] = _from_slab_rows(gathered[:, r0:r0 + rows], (r, k), lead=(N_DEV,)).reshape(N_DEV * r, k)
        r0 += rows
    return out


def _scatter_layout(grads, group):
    parts = []
    for name in GROUPS[group]:
        r, k = _row_form(*BIG_INFO[name])
        parts.append(_as_slab_rows(grads[name].astype(BF16).reshape(N_DEV, r, k), lead=(N_DEV,)))
    return jnp.concatenate(parts, axis=1)


def _unpack_blocks(slab, group):
    out = _unpack(slab, [_row_form(*BIG_INFO[n]) for n in GROUPS[group]], BF16)
    return {n: b if BIG_INFO[n][1] == 0 else b.T for n, b in zip(GROUPS[group], out)}


def _split_w_in(w_in):
    edges = np.cumsum((0,) + SEG)
    return [w_in[edges[i]:edges[i + 1]] for i in range(4)]


def _conv_taps(conv_dw):
    return jnp.pad(conv_dw.astype(F32).T, ((0, 32 - CONV_K), (0, 0)))


def _layer_fwd(l, x, h, nm, get_w, sm, bias_tabs):
    t = x.shape[0]
    win = get_w(l, "in", h)
    w_seg = _split_w_in(win["w_in"])
    zc = _mm(h, w_seg[0], "fwd_in_conv", tb=True)
    za = _mm(h, w_seg[1], "fwd_in_att", tb=True, tn=1152)
    zq = _mm(h, w_seg[2], "fwd_in_memq", tb=True)
    zg = _mm(h, w_seg[3], "fwd_in_gate", tb=True)
    wts = get_w(l, "mix", zg)
    y, cact = _conv_fwd(zc, _conv_taps(wts["conv_dw"]), sm["conv_dw_bias"][l], sm["conv_ln_g"][l], sm["conv_ln_b"][l],
                        "conv_fwd")
    qkv, outs, lses = [], [], []
    for g, (_, dil) in enumerate(ATT_PATTERNS):
        cols = tuple(i * 3 + g for i in range(3))
        if dil == 1:
            trio, at = (za, za, za), cols
        else:
            trio, at = _to_residue_major([(za, c) for c in cols], dil, f"qkv_residue_d{dil}"), (0, 0, 0)
        o, ls = _att_fwd(*trio, bias_tabs[g], t // dil, f"att_fwd_d{dil}", cols=at)
        qkv.append((trio, at))
        outs.append(o)
        lses.append(ls)
    att, lse = _att_combine(outs, lses, "att_combine")
    kv = _mm(nm, wts["w_mem_kv"], "fwd_mem_kv")
    om = _mem_fwd(zq, kv, "mem_fwd")
    w_br = (wts["w_conv_out"], wts["w_att_out"], wts["w_mem_out"])
    ys, merged = _branches_fwd((cact, att, om), w_br, zg, sm["b_gate"][l], "branches_fwd")
    ymix, x1, h2 = _mm_norm_res(merged, wts["w_out"], x, sm["norm_mix_post"][l], sm["norm_ffn_pre"][l], "fwd_out_norm")
    gff, uff, act = _mm_swiglu_fwd(h2, get_w(l, "ffn_in", h2)["w_ffn_in"], "fwd_ffn_in_swiglu")
    g_next = sm["norm_mix_pre"][l + 1] if l + 1 < DEPTH else None
    f, x2, h_next = _mm_norm_res(act, get_w(l, "ffn_out", act)["w_ffn_out"], x1, sm["norm_ffn_post"][l], g_next,
                                 "fwd_ffn_out_norm" if g_next is not None else "fwd_ffn_out_last")
    saved = dict(x=x, h=h, zc=zc, zq=zq, zg=zg, y=y, cact=cact, qkv=qkv, att=att, lse=lse, kv=kv, om=om, ys=ys,
                 merged=merged, ymix=ymix, x1=x1, h2=h2, gff=gff, uff=uff, act=act, f=f)
    return x2, h_next, saved


def _after(value, *tokens):
    for t in tokens:
        if t is not None:
            value = value + t[0, 0].astype(value.dtype)
    return value


def _layer_bwd(l, dx2, s, nm, mem, get_w, put_g, sm, bias_tabs, bucket_idx):
    t = dx2.shape[0]
    win, wts = get_w(l, "in", None), get_w(l, "mix", None)
    w_ffn_in, w_ffn_out = get_w(l, "ffn_in", None)["w_ffn_in"], get_w(l, "ffn_out", None)["w_ffn_out"]
    gb, gs = {}, {}
    df, gs["norm_ffn_post"] = _norm_bwd(s["f"], sm["norm_ffn_post"][l], dx2, None, BF16, "bwd_norm_ffn_post")
    dg, du = _mm_swiglu_bwd(df, w_ffn_out, s["gff"], s["uff"], "bwd_ffn_out_d_swiglu")
    tok_out = put_g(l, "ffn_out", {"w_ffn_out": _mm(s["act"], df, "bwd_ffn_out_w", ta=True, tm=1408)})
    tok_in = put_g(l, "ffn_in", {"w_ffn_in": jnp.concatenate([_mm(dg, s["h2"], "bwd_ffn_in_w_gate", ta=True, tm=1408),
                                                              _mm(du, s["h2"], "bwd_ffn_in_w_up", ta=True, tm=1408)], axis=0)})
    dh2 = _mm_sum([(dg, _after(w_ffn_in[:FFN_H], tok_out, tok_in), FFN_H // 2), (du, w_ffn_in[FFN_H:], FFN_H // 2)],
                  "bwd_ffn_in_d")
    dx1, dymix, gs["norm_ffn_pre"], gs["norm_mix_post"] = _norm_bwd_pair(
        s["x1"], sm["norm_ffn_pre"][l], dh2, dx2, s["ymix"], sm["norm_mix_post"][l], "bwd_norm_ffn_pre_mix_post")
    gb["w_out"] = _mm(s["merged"], dymix, "bwd_out_w", ta=True)
    w_br = (wts["w_conv_out"], wts["w_att_out"], wts["w_mem_out"])
    dys, dzg, gs["b_gate"], (dcact, datt, dom), delta = _branches_bwd(
        dymix, wts["w_out"], w_br, s["zg"], sm["b_gate"][l], s["ys"], s["att"], "branches_bwd")
    gb["w_conv_out"] = _mm(dys[0], s["cact"], "bwd_conv_out_w", ta=True)
    dzc, dconv, gs["conv_dw_bias"], gs["conv_ln_g"], gs["conv_ln_b"] = _conv_bwd(
        s["zc"], s["y"], dcact, _conv_taps(wts["conv_dw"]), sm["conv_ln_g"][l], sm["conv_ln_b"][l], "conv_bwd")
    gb["conv_dw"] = dconv[:CONV_K].T
    gb["w_att_out"] = _mm(dys[1], s["att"], "bwd_att_out_w", ta=True)
    dqkv, rel = [], []
    for g, (_, dil) in enumerate(ATT_PATTERNS):
        trio, at = s["qkv"][g]
        if dil == 1:
            dl_r, do_r, lse_r = delta, datt, s["lse"]
        else:
            dl_r, do_r, lse_r = _to_residue_major([(delta, 0), (datt, 0), (s["lse"], 0)], dil, f"att_bwd_residue_d{dil}")
        dq, dk, dv, dbias = _att_bwd(*trio, bias_tabs[g], dl_r, do_r, lse_r, t // dil, f"att_bwd_d{dil}", cols=at)
        dqkv.append((dq, dk, dv))
        rel.append(_bias_grad(dbias, bucket_idx[g], f"bias_grad_d{dil}")[:HEADS, :NUM_BUCKETS])
    dza = _assemble_dza(dqkv, "att_bwd_assemble")
    gs["rel_bias"] = jnp.concatenate(rel, axis=0).T
    gb["w_mem_out"] = _mm(dys[2], s["om"], "bwd_mem_out_w", ta=True)
    dzq, dkv = _mem_bwd(s["zq"], s["kv"], dom, "mem_bwd")
    dkv = dkv.astype(BF16)
    gb["w_mem_kv"] = _mm(nm, dkv, "bwd_mem_kv_w", ta=True)
    tok = put_g(l, "mix", gb)
    dnm = _mm(dkv, wts["w_mem_kv"], "bwd_mem_kv_d", tb=True, out_dtype=F32)
    _, gs["norm_mem"] = _norm_bwd(mem, sm["norm_mem"][l], dnm, None, BF16, "bwd_norm_mem")
    segs = ((dzc, "conv", 1024), (dza, "att", 1152), (_after(dzq, tok), "memq", 512), (dzg, "gate", 1024))
    tok = put_g(l, "in", {"w_in": jnp.concatenate([_mm(dz, s["h"], f"bwd_in_{nm_}_w", ta=True, tm=min(blk, 1152), tk=2048)
                                                   for dz, nm_, blk in segs], axis=0)})
    w_seg = _split_w_in(win["w_in"])
    w_seg[0] = _after(w_seg[0], tok)
    dx, gs["norm_mix_pre"] = _mm_sum([(dz, w_seg[i], blk) for i, (dz, _, blk) in enumerate(segs)], "bwd_in_d_norm",
                                     norm=(s["x"], sm["norm_mix_pre"][l], dx1))
    return dx, gs


def _local_step(x, mem, target, sm, get_w, put_g, tokens=()):
    t = x.shape[0]
    bias_tabs, bucket_idx = [], []
    for g, (_, dil) in enumerate(ATT_PATTERNS):
        idx = _bucket_index(_att_tiles(t // dil)[1], dil)
        bucket_idx.append(jnp.asarray(idx))
        bias_tabs.append(_bias_table(sm["rel_bias"][:, g * HEADS:(g + 1) * HEADS], bucket_idx[g], f"bias_table_d{dil}"))
    h = _norm_plain(x, _after(sm["norm_mix_pre"][0], *tokens), "norm_first")
    saved, nms = [], []
    for l in range(DEPTH):
        nm = _norm_plain(mem, sm["norm_mem"][l], "norm_mem")
        x, h, s = _layer_fwd(l, x, h, nm, get_w, sm, bias_tabs)
        saved.append(s)
        nms.append(nm)
    loss, dx = _loss_head(x, target, "loss_head")
    gsmall = {}
    for l in reversed(range(DEPTH)):
        dx, gs = _layer_bwd(l, dx, saved[l], nms[l], mem, get_w, put_g, sm, bias_tabs, bucket_idx)
        for n, v in gs.items():
            gsmall.setdefault(n, {})[l] = v
    small = {}
    for n, _ in SMALL:
        small[n] = gsmall[n][0] + gsmall[n][1] if n == "rel_bias" else jnp.stack([gsmall[n][0], gsmall[n][1]])
    return loss, dx, small


def kernel(x, mem, rel_bias, norm_mix_pre, w_in, b_gate, conv_dw, conv_dw_bias, conv_ln_g, conv_ln_b, w_conv_out, w_att_out, norm_mem, w_mem_kv, w_mem_out, w_out, norm_mix_post, norm_ffn_pre, w_ffn_in, w_ffn_out, norm_ffn_post, loss_target, m_rel_bias, m_norm_mix_pre, m_w_in, m_b_gate, m_conv_dw, m_conv_dw_bias, m_conv_ln_g, m_conv_ln_b, m_w_conv_out, m_w_att_out, m_norm_mem, m_w_mem_kv, m_w_mem_out, m_w_out, m_norm_mix_post, m_norm_ffn_pre, m_w_ffn_in, m_w_ffn_out, m_norm_ffn_post, v_rel_bias, v_norm_mix_pre, v_w_in, v_b_gate, v_conv_dw, v_conv_dw_bias, v_conv_ln_g, v_conv_ln_b, v_w_conv_out, v_w_att_out, v_norm_mem, v_w_mem_kv, v_w_mem_out, v_w_out, v_norm_mix_post, v_norm_ffn_pre, v_w_ffn_in, v_w_ffn_out, v_norm_ffn_post):
    args = dict(locals())
    w = {n: args[n] for n in TWIN_WEIGHTS}
    m = {n: args["m_" + n] for n in TWIN_WEIGHTS}
    v = {n: args["v_" + n] for n in TWIN_WEIGHTS}
    sm = {n: w[n] for n, _ in SMALL}

    first = (0, "in")
    keys = [(l, grp) for l in range(DEPTH) for grp in GROUPS]
    shard = {(l, grp): _pack_group(w, l, grp) for l, grp in keys}
    weights = {first: _full_weights(_all_gather(shard[first], "gather_l0_in"), "in")}
    settled = weights[first]["w_in"][:1, :1] * 0
    ag = {}
    for k in keys:
        if k != first:
            ag[k] = _exchange_start(_after(shard[k], settled), True, f"ag_start_l{k[0]}_{k[1]}")
            settled = ag[k][4]

    def get_w(l, grp, after):
        if (l, grp) not in weights:
            mine, land = _exchange_wait(ag[l, grp], after, True, f"ag_wait_l{l}_{grp}")
            weights[l, grp] = _full_weights(_own_slot(land, mine), grp)
        return weights[l, grp]

    rs = {}

    def put_g(l, grp, grads):
        rs[l, grp] = _exchange_start(_scatter_layout(grads, grp), False, f"rs_start_l{l}_{grp}")
        return rs[l, grp][4]

    loss, dx, gsmall = _local_step(x[0], mem[0], loss_target[0], sm, get_w, put_g, [st[4] for st in ag.values()])
    loss = lax.psum(loss, ("x", "y", "c"))

    xi, yi, ci = _place()
    me = 4 * xi + 2 * yi + ci
    g_slab = {}
    for k, st in rs.items():
        contrib, land = _exchange_wait(st, dx, False, f"rs_wait_l{k[0]}_{k[1]}")
        own = lax.dynamic_index_in_dim(contrib, me, axis=0, keepdims=False)
        g_slab[k] = _sum_slots(_own_slot(land, own), F32, f"rs_sum_l{k[0]}_{k[1]}")
    g_layers = [{n: b for grp in GROUPS for n, b in _unpack_blocks(g_slab[l, grp], grp).items()} for l in range(DEPTH)]
    small_shapes = [s for _, s in SMALL]
    g_small = _unpack(_sum_slots(_all_gather(_pack([gsmall[n] for n, _ in SMALL], F32), "gather_small"), F32, "sum_small"),
                      small_shapes, F32)
    grads = {n: jnp.stack([g_layers[0][n], g_layers[1][n]]) for n, _, _ in BIG}
    grads.update({n: g_small[i] for i, (n, _) in enumerate(SMALL)})

    delta, new_m, new_v = {}, {}, {}
    for n in TWIN_WEIGHTS:
        delta[n], new_m[n], new_v[n] = _adamw(w[n], grads[n], m[n], v[n], f"adamw_{n}")

    return (loss, dx[None], *[grads[n] for n in TWIN_WEIGHTS], *[delta[n] for n in TWIN_WEIGHTS],
            *[new_m[n] for n in TWIN_WEIGHTS], *[new_v[n] for n in TWIN_WEIGHTS])
```
